```python
import jax, jax.numpy as jnp
from jax import lax
import numpy as np

D_MODEL = 1024
BATCH = 8
SEQ = 8192
DEPTH = 1

D_MIX = D_MODEL
D_A = D_MIX // 2
D_B = D_MIX - D_A
N_HEADS_A = 8
HEAD_DIM_A = D_A // N_HEADS_A
N_GROUPS_B = 8
GROUP_DIM_B = D_B // N_GROUPS_B
CHUNK = 128
CONV_WIDTH = 31
D_IN = 2 * D_A + 2 * D_B
D_FF = -(-8 * D_MODEL // (3 * 256)) * 256
N_MOD = 6
EPS = 1e-6

kernel_name = "hybrid_sgu_conformer_conv_adaln_block"


def rmsnorm(x, g):
    xf = x.astype(jnp.float32)
    y = xf * lax.rsqrt(jnp.mean(xf * xf, axis=-1, keepdims=True) + EPS)
    return (y * g.astype(jnp.float32)).astype(x.dtype)


def layernorm(x, g, b):
    xf = x.astype(jnp.float32)
    mu = jnp.mean(xf, axis=-1, keepdims=True)
    xc = xf - mu
    var = jnp.mean(xc * xc, axis=-1, keepdims=True)
    y = xc * lax.rsqrt(var + EPS) * g.astype(jnp.float32) + b.astype(jnp.float32)
    return y.astype(x.dtype)


def modulate(h, shift, scale):
    return h * (1 + scale[:, None, :]) + shift[:, None, :]


def spatial_gating_mixer(u, v, ln_g, ln_b, w_s, b_s):
    B, S, _ = v.shape
    v = layernorm(v, ln_g, ln_b)
    v = v.reshape(B, S // CHUNK, CHUNK, N_HEADS_A, HEAD_DIM_A)
    causal = jnp.tril(jnp.ones((CHUNK, CHUNK), dtype=bool))
    w = jnp.where(causal[None], w_s, jnp.zeros_like(w_s)).astype(v.dtype)
    mixed = jnp.einsum('hts,bcshd->bcthd', w, v) + b_s.T.astype(v.dtype)[None, None, :, :, None]
    return u * mixed.reshape(B, S, D_A)


def conformer_conv_mixer(val, gate, conv_w, conv_b, gn_g, gn_b):
    B, S, _ = val.shape
    y = val * jax.nn.sigmoid(gate)
    y = lax.conv_general_dilated(
        y, conv_w[:, None, :].astype(y.dtype), window_strides=(1,),
        padding=[(CONV_WIDTH - 1, 0)],
        dimension_numbers=('NWC', 'WIO', 'NWC'),
        feature_group_count=D_B) + conv_b.astype(y.dtype)
    y = y.reshape(B, S, N_GROUPS_B, GROUP_DIM_B)
    y = layernorm(y, gn_g.reshape(N_GROUPS_B, GROUP_DIM_B), gn_b.reshape(N_GROUPS_B, GROUP_DIM_B))
    return jax.nn.silu(y.reshape(B, S, D_B))


def _fwd_setup_inputs(seed: int = 0) -> dict:
    key = jax.random.key(seed)
    ks = jax.random.split(key, 24)
    f32 = jnp.float32
    nrm = lambda k, shape, s: jax.random.normal(k, shape, f32) * s
    L = DEPTH
    return {
        "x": jax.random.normal(ks[0], (BATCH, SEQ, D_MODEL), f32),
        "c": jax.random.normal(ks[1], (BATCH, D_MODEL), f32),
        "ada_w": nrm(ks[2], (L, D_MODEL, N_MOD * D_MODEL), D_MODEL ** -0.5),
        "ada_b": nrm(ks[3], (L, N_MOD * D_MODEL), 0.02),
        "norm1_g": 1.0 + nrm(ks[4], (L, D_MODEL), 0.05),
        "w_in": nrm(ks[5], (L, D_MODEL, D_IN), D_MODEL ** -0.5),
        "b_in": nrm(ks[6], (L, D_IN), 0.02),
        "a_ln_g": 1.0 + nrm(ks[7], (L, D_A), 0.05),
        "a_ln_b": nrm(ks[8], (L, D_A), 0.02),
        "a_spatial_w": nrm(ks[9], (L, N_HEADS_A, CHUNK, CHUNK), CHUNK ** -0.5),
        "a_spatial_b": 1.0 + nrm(ks[10], (L, N_HEADS_A, CHUNK), 0.1),
        "b_conv_w": nrm(ks[11], (L, CONV_WIDTH, D_B), CONV_WIDTH ** -0.5),
        "b_conv_b": nrm(ks[12], (L, D_B), 0.02),
        "b_gn_g": 1.0 + nrm(ks[13], (L, D_B), 0.05),
        "b_gn_b": nrm(ks[14], (L, D_B), 0.02),
        "out_norm_a_g": 1.0 + nrm(ks[15], (L, D_A), 0.05),
        "out_norm_b_g": 1.0 + nrm(ks[16], (L, D_B), 0.05),
        "w_out": nrm(ks[17], (L, D_MIX, D_MODEL), D_MIX ** -0.5),
        "norm2_g": 1.0 + nrm(ks[18], (L, D_MODEL), 0.05),
        "w_ffn_in": nrm(ks[19], (L, D_MODEL, 2 * D_FF), D_MODEL ** -0.5),
        "w_ffn_out": nrm(ks[20], (L, D_FF, D_MODEL), D_FF ** -0.5),
        "ada_f_w": nrm(ks[21], (D_MODEL, 2 * D_MODEL), D_MODEL ** -0.5),
        "ada_f_b": nrm(ks[22], (2 * D_MODEL,), 0.02),
        "norm_f_g": 1.0 + nrm(ks[23], (D_MODEL,), 0.05),
    }


def _fwd_reference(x, c, ada_w, ada_b, norm1_g, w_in, b_in, a_ln_g, a_ln_b, a_spatial_w,
              a_spatial_b, b_conv_w, b_conv_b, b_gn_g, b_gn_b, out_norm_a_g, out_norm_b_g,
              w_out, norm2_g, w_ffn_in, w_ffn_out, ada_f_w, ada_f_b, norm_f_g):
    c_act = jax.nn.silu(c)
    for i in range(DEPTH):
        cond = c_act @ ada_w[i] + ada_b[i]
        shift1, scale1, gate1, shift2, scale2, gate2 = jnp.split(cond, N_MOD, axis=-1)

        h = modulate(rmsnorm(x, norm1_g[i]), shift1, scale1)
        z = h @ w_in[i] + b_in[i]
        u, v, val, gate = jnp.split(z, [D_A, 2 * D_A, 2 * D_A + D_B], axis=-1)
        y_a = spatial_gating_mixer(jax.nn.gelu(u, approximate=False),
                                   jax.nn.gelu(v, approximate=False),
                                   a_ln_g[i], a_ln_b[i], a_spatial_w[i], a_spatial_b[i])
        y_b = conformer_conv_mixer(val, gate, b_conv_w[i], b_conv_b[i], b_gn_g[i], b_gn_b[i])
        y = jnp.concatenate([rmsnorm(y_a, out_norm_a_g[i]), rmsnorm(y_b, out_norm_b_g[i])], axis=-1)
        x = x + gate1[:, None, :] * (y @ w_out[i])

        h2 = modulate(rmsnorm(x, norm2_g[i]), shift2, scale2)
        g_ff, up_ff = jnp.split(h2 @ w_ffn_in[i], 2, axis=-1)
        x = x + gate2[:, None, :] * ((jax.nn.silu(g_ff) * up_ff) @ w_ffn_out[i])

    shift_f, scale_f = jnp.split(c_act @ ada_f_w + ada_f_b, 2, axis=-1)
    return modulate(rmsnorm(x, norm_f_g), shift_f, scale_f)


import jax as _jax
import jax.numpy as _jnp

TWIN_FORMAT = 'train_step'
FWD_PARAMS = ['x', 'c', 'ada_w', 'ada_b', 'norm1_g', 'w_in', 'b_in', 'a_ln_g', 'a_ln_b', 'a_spatial_w', 'a_spatial_b', 'b_conv_w', 'b_conv_b', 'b_gn_g', 'b_gn_b', 'out_norm_a_g', 'out_norm_b_g', 'w_out', 'norm2_g', 'w_ffn_in', 'w_ffn_out', 'ada_f_w', 'ada_f_b', 'norm_f_g']
TWIN_WEIGHTS = ['ada_w', 'ada_b', 'norm1_g', 'w_in', 'b_in', 'a_ln_g', 'a_ln_b', 'a_spatial_w', 'a_spatial_b', 'b_conv_w', 'b_conv_b', 'b_gn_g', 'b_gn_b', 'out_norm_a_g', 'out_norm_b_g', 'w_out', 'norm2_g', 'w_ffn_in', 'w_ffn_out', 'ada_f_w', 'ada_f_b', 'norm_f_g']
TWIN_DIFF_INPUT = 'x'
TWIN_INPUTS = ['x', 'c', 'ada_w', 'ada_b', 'norm1_g', 'w_in', 'b_in', 'a_ln_g', 'a_ln_b', 'a_spatial_w', 'a_spatial_b', 'b_conv_w', 'b_conv_b', 'b_gn_g', 'b_gn_b', 'out_norm_a_g', 'out_norm_b_g', 'w_out', 'norm2_g', 'w_ffn_in', 'w_ffn_out', 'ada_f_w', 'ada_f_b', 'norm_f_g', 'loss_target', 'm_ada_w', 'm_ada_b', 'm_norm1_g', 'm_w_in', 'm_b_in', 'm_a_ln_g', 'm_a_ln_b', 'm_a_spatial_w', 'm_a_spatial_b', 'm_b_conv_w', 'm_b_conv_b', 'm_b_gn_g', 'm_b_gn_b', 'm_out_norm_a_g', 'm_out_norm_b_g', 'm_w_out', 'm_norm2_g', 'm_w_ffn_in', 'm_w_ffn_out', 'm_ada_f_w', 'm_ada_f_b', 'm_norm_f_g', 'v_ada_w', 'v_ada_b', 'v_norm1_g', 'v_w_in', 'v_b_in', 'v_a_ln_g', 'v_a_ln_b', 'v_a_spatial_w', 'v_a_spatial_b', 'v_b_conv_w', 'v_b_conv_b', 'v_b_gn_g', 'v_b_gn_b', 'v_out_norm_a_g', 'v_out_norm_b_g', 'v_w_out', 'v_norm2_g', 'v_w_ffn_in', 'v_w_ffn_out', 'v_ada_f_w', 'v_ada_f_b', 'v_norm_f_g']
TWIN_OUTPUTS = ['loss', 'grad_x', 'grad_ada_w', 'grad_ada_b', 'grad_norm1_g', 'grad_w_in', 'grad_b_in', 'grad_a_ln_g', 'grad_a_ln_b', 'grad_a_spatial_w', 'grad_a_spatial_b', 'grad_b_conv_w', 'grad_b_conv_b', 'grad_b_gn_g', 'grad_b_gn_b', 'grad_out_norm_a_g', 'grad_out_norm_b_g', 'grad_w_out', 'grad_norm2_g', 'grad_w_ffn_in', 'grad_w_ffn_out', 'grad_ada_f_w', 'grad_ada_f_b', 'grad_norm_f_g', 'delta_ada_w', 'delta_ada_b', 'delta_norm1_g', 'delta_w_in', 'delta_b_in', 'delta_a_ln_g', 'delta_a_ln_b', 'delta_a_spatial_w', 'delta_a_spatial_b', 'delta_b_conv_w', 'delta_b_conv_b', 'delta_b_gn_g', 'delta_b_gn_b', 'delta_out_norm_a_g', 'delta_out_norm_b_g', 'delta_w_out', 'delta_norm2_g', 'delta_w_ffn_in', 'delta_w_ffn_out', 'delta_ada_f_w', 'delta_ada_f_b', 'delta_norm_f_g', 'new_m_ada_w', 'new_m_ada_b', 'new_m_norm1_g', 'new_m_w_in', 'new_m_b_in', 'new_m_a_ln_g', 'new_m_a_ln_b', 'new_m_a_spatial_w', 'new_m_a_spatial_b', 'new_m_b_conv_w', 'new_m_b_conv_b', 'new_m_b_gn_g', 'new_m_b_gn_b', 'new_m_out_norm_a_g', 'new_m_out_norm_b_g', 'new_m_w_out', 'new_m_norm2_g', 'new_m_w_ffn_in', 'new_m_w_ffn_out', 'new_m_ada_f_w', 'new_m_ada_f_b', 'new_m_norm_f_g', 'new_v_ada_w', 'new_v_ada_b', 'new_v_norm1_g', 'new_v_w_in', 'new_v_b_in', 'new_v_a_ln_g', 'new_v_a_ln_b', 'new_v_a_spatial_w', 'new_v_a_spatial_b', 'new_v_b_conv_w', 'new_v_b_conv_b', 'new_v_b_gn_g', 'new_v_b_gn_b', 'new_v_out_norm_a_g', 'new_v_out_norm_b_g', 'new_v_w_out', 'new_v_norm2_g', 'new_v_w_ffn_in', 'new_v_w_ffn_out', 'new_v_ada_f_w', 'new_v_ada_f_b', 'new_v_norm_f_g']
TWIN_LEAF_KINDS = {'loss': 'loss', 'grad_x': 'grad_x', 'grad_ada_w': 'grad_w', 'grad_ada_b': 'grad_w', 'grad_norm1_g': 'grad_w', 'grad_w_in': 'grad_w', 'grad_b_in': 'grad_w', 'grad_a_ln_g': 'grad_w', 'grad_a_ln_b': 'grad_w', 'grad_a_spatial_w': 'grad_w', 'grad_a_spatial_b': 'grad_w', 'grad_b_conv_w': 'grad_w', 'grad_b_conv_b': 'grad_w', 'grad_b_gn_g': 'grad_w', 'grad_b_gn_b': 'grad_w', 'grad_out_norm_a_g': 'grad_w', 'grad_out_norm_b_g': 'grad_w', 'grad_w_out': 'grad_w', 'grad_norm2_g': 'grad_w', 'grad_w_ffn_in': 'grad_w', 'grad_w_ffn_out': 'grad_w', 'grad_ada_f_w': 'grad_w', 'grad_ada_f_b': 'grad_w', 'grad_norm_f_g': 'grad_w', 'delta_ada_w': 'delta_w', 'delta_ada_b': 'delta_w', 'delta_norm1_g': 'delta_w', 'delta_w_in': 'delta_w', 'delta_b_in': 'delta_w', 'delta_a_ln_g': 'delta_w', 'delta_a_ln_b': 'delta_w', 'delta_a_spatial_w': 'delta_w', 'delta_a_spatial_b': 'delta_w', 'delta_b_conv_w': 'delta_w', 'delta_b_conv_b': 'delta_w', 'delta_b_gn_g': 'delta_w', 'delta_b_gn_b': 'delta_w', 'delta_out_norm_a_g': 'delta_w', 'delta_out_norm_b_g': 'delta_w', 'delta_w_out': 'delta_w', 'delta_norm2_g': 'delta_w', 'delta_w_ffn_in': 'delta_w', 'delta_w_ffn_out': 'delta_w', 'delta_ada_f_w': 'delta_w', 'delta_ada_f_b': 'delta_w', 'delta_norm_f_g': 'delta_w', 'new_m_ada_w': 'new_m', 'new_m_ada_b': 'new_m', 'new_m_norm1_g': 'new_m', 'new_m_w_in': 'new_m', 'new_m_b_in': 'new_m', 'new_m_a_ln_g': 'new_m', 'new_m_a_ln_b': 'new_m', 'new_m_a_spatial_w': 'new_m', 'new_m_a_spatial_b': 'new_m', 'new_m_b_conv_w': 'new_m', 'new_m_b_conv_b': 'new_m', 'new_m_b_gn_g': 'new_m', 'new_m_b_gn_b': 'new_m', 'new_m_out_norm_a_g': 'new_m', 'new_m_out_norm_b_g': 'new_m', 'new_m_w_out': 'new_m', 'new_m_norm2_g': 'new_m', 'new_m_w_ffn_in': 'new_m', 'new_m_w_ffn_out': 'new_m', 'new_m_ada_f_w': 'new_m', 'new_m_ada_f_b': 'new_m', 'new_m_norm_f_g': 'new_m', 'new_v_ada_w': 'new_v', 'new_v_ada_b': 'new_v', 'new_v_norm1_g': 'new_v', 'new_v_w_in': 'new_v', 'new_v_b_in': 'new_v', 'new_v_a_ln_g': 'new_v', 'new_v_a_ln_b': 'new_v', 'new_v_a_spatial_w': 'new_v', 'new_v_a_spatial_b': 'new_v', 'new_v_b_conv_w': 'new_v', 'new_v_b_conv_b': 'new_v', 'new_v_b_gn_g': 'new_v', 'new_v_b_gn_b': 'new_v', 'new_v_out_norm_a_g': 'new_v', 'new_v_out_norm_b_g': 'new_v', 'new_v_w_out': 'new_v', 'new_v_norm2_g': 'new_v', 'new_v_w_ffn_in': 'new_v', 'new_v_w_ffn_out': 'new_v', 'new_v_ada_f_w': 'new_v', 'new_v_ada_f_b': 'new_v', 'new_v_norm_f_g': 'new_v'}


def _forward(args):
    return _fwd_reference(*[args[k] for k in FWD_PARAMS])


def _output_shape():
    def fwd():
        inp = _fwd_setup_inputs(0)
        return _fwd_reference(*[inp[k] for k in FWD_PARAMS])
    out = _jax.eval_shape(fwd)
    return out.shape, out.dtype

N_MICROBATCH = 1
ADAM_LR = 0.001
ADAM_B1 = 0.9
ADAM_B2 = 0.999
ADAM_EPS = 1e-08
ADAM_WD = 0.01
ADAM_STEP = 10
PER_EXAMPLE_BATCH_AXIS = {'x': 0, 'c': 0, 'loss_target': 0}
SHARED_INPUTS = []
_WEIGHT_DTYPES = {'ada_w': _jnp.float32, 'ada_b': _jnp.float32, 'norm1_g': _jnp.float32, 'w_in': _jnp.float32, 'b_in': _jnp.float32, 'a_ln_g': _jnp.float32, 'a_ln_b': _jnp.float32, 'a_spatial_w': _jnp.float32, 'a_spatial_b': _jnp.float32, 'b_conv_w': _jnp.float32, 'b_conv_b': _jnp.float32, 'b_gn_g': _jnp.float32, 'b_gn_b': _jnp.float32, 'out_norm_a_g': _jnp.float32, 'out_norm_b_g': _jnp.float32, 'w_out': _jnp.float32, 'norm2_g': _jnp.float32, 'w_ffn_in': _jnp.float32, 'w_ffn_out': _jnp.float32, 'ada_f_w': _jnp.float32, 'ada_f_b': _jnp.float32, 'norm_f_g': _jnp.float32}
MOMENT_SCALE = {'ada_w': 4.708197e+00, 'ada_b': 7.994465e+00, 'norm1_g': 3.995139e-01, 'w_in': 2.076173e+00, 'b_in': 3.871196e+00, 'a_ln_g': 2.536992e-01, 'a_ln_b': 2.946937e-01, 'a_spatial_w': 1.189426e-01, 'a_spatial_b': 2.305237e-01, 'b_conv_w': 3.317661e+00, 'b_conv_b': 1.029357e+01, 'b_gn_g': 6.529661e+00, 'b_gn_b': 8.323816e+00, 'out_norm_a_g': 4.425066e+00, 'out_norm_b_g': 4.691731e+00, 'w_out': 4.737536e+00, 'norm2_g': 1.367815e+00, 'w_ffn_in': 1.320838e+00, 'w_ffn_out': 2.542047e+00, 'ada_f_w': 1.700257e+01, 'ada_f_b': 5.063838e+01, 'norm_f_g': 9.802150e+01}


def _to_microbatches(a, axis):
    t = _jnp.moveaxis(a, axis, 0)
    t = t.reshape((N_MICROBATCH, t.shape[0] // N_MICROBATCH) + t.shape[1:])
    return _jnp.moveaxis(t, 1, axis + 1)


def setup_inputs(seed: int = 0) -> dict:
    inp = _fwd_setup_inputs(seed)
    key = _jax.random.fold_in(_jax.random.key(seed), 7919)
    shape, _ = _output_shape()
    out = dict(inp)
    out["loss_target"] = _jax.random.normal(_jax.random.fold_in(key, 0), shape, _jnp.float32)
    for i, name in enumerate(TWIN_WEIGHTS):
        w = inp[name].astype(_jnp.float32)
        if MOMENT_SCALE is None:
            s = _jnp.sqrt(_jnp.mean(_jnp.square(w)) + 1e-30)
        else:
            s = MOMENT_SCALE[name]
        km, kv = _jax.random.split(_jax.random.fold_in(key, i + 1))
        out[name] = w
        out["m_" + name] = s * _jax.random.normal(km, w.shape, _jnp.float32)
        out["v_" + name] = (s * s) * _jax.random.uniform(kv, w.shape, _jnp.float32, 0.5, 1.5)
    if N_MICROBATCH > 1:
        for name, axis in PER_EXAMPLE_BATCH_AXIS.items():
            out[name] = _to_microbatches(out[name], axis)
    return {'x': out['x'], 'c': out['c'], 'ada_w': out['ada_w'], 'ada_b': out['ada_b'], 'norm1_g': out['norm1_g'], 'w_in': out['w_in'], 'b_in': out['b_in'], 'a_ln_g': out['a_ln_g'], 'a_ln_b': out['a_ln_b'], 'a_spatial_w': out['a_spatial_w'], 'a_spatial_b': out['a_spatial_b'], 'b_conv_w': out['b_conv_w'], 'b_conv_b': out['b_conv_b'], 'b_gn_g': out['b_gn_g'], 'b_gn_b': out['b_gn_b'], 'out_norm_a_g': out['out_norm_a_g'], 'out_norm_b_g': out['out_norm_b_g'], 'w_out': out['w_out'], 'norm2_g': out['norm2_g'], 'w_ffn_in': out['w_ffn_in'], 'w_ffn_out': out['w_ffn_out'], 'ada_f_w': out['ada_f_w'], 'ada_f_b': out['ada_f_b'], 'norm_f_g': out['norm_f_g'], 'loss_target': out['loss_target'], 'm_ada_w': out['m_ada_w'], 'm_ada_b': out['m_ada_b'], 'm_norm1_g': out['m_norm1_g'], 'm_w_in': out['m_w_in'], 'm_b_in': out['m_b_in'], 'm_a_ln_g': out['m_a_ln_g'], 'm_a_ln_b': out['m_a_ln_b'], 'm_a_spatial_w': out['m_a_spatial_w'], 'm_a_spatial_b': out['m_a_spatial_b'], 'm_b_conv_w': out['m_b_conv_w'], 'm_b_conv_b': out['m_b_conv_b'], 'm_b_gn_g': out['m_b_gn_g'], 'm_b_gn_b': out['m_b_gn_b'], 'm_out_norm_a_g': out['m_out_norm_a_g'], 'm_out_norm_b_g': out['m_out_norm_b_g'], 'm_w_out': out['m_w_out'], 'm_norm2_g': out['m_norm2_g'], 'm_w_ffn_in': out['m_w_ffn_in'], 'm_w_ffn_out': out['m_w_ffn_out'], 'm_ada_f_w': out['m_ada_f_w'], 'm_ada_f_b': out['m_ada_f_b'], 'm_norm_f_g': out['m_norm_f_g'], 'v_ada_w': out['v_ada_w'], 'v_ada_b': out['v_ada_b'], 'v_norm1_g': out['v_norm1_g'], 'v_w_in': out['v_w_in'], 'v_b_in': out['v_b_in'], 'v_a_ln_g': out['v_a_ln_g'], 'v_a_ln_b': out['v_a_ln_b'], 'v_a_spatial_w': out['v_a_spatial_w'], 'v_a_spatial_b': out['v_a_spatial_b'], 'v_b_conv_w': out['v_b_conv_w'], 'v_b_conv_b': out['v_b_conv_b'], 'v_b_gn_g': out['v_b_gn_g'], 'v_b_gn_b': out['v_b_gn_b'], 'v_out_norm_a_g': out['v_out_norm_a_g'], 'v_out_norm_b_g': out['v_out_norm_b_g'], 'v_w_out': out['v_w_out'], 'v_norm2_g': out['v_norm2_g'], 'v_w_ffn_in': out['v_w_ffn_in'], 'v_w_ffn_out': out['v_w_ffn_out'], 'v_ada_f_w': out['v_ada_f_w'], 'v_ada_f_b': out['v_ada_f_b'], 'v_norm_f_g': out['v_norm_f_g']}


def _loss(weights, diff, rest, loss_target):
    with _jax.named_scope("forward"):
        args = {**rest, TWIN_DIFF_INPUT: diff, **{k: w.astype(_WEIGHT_DTYPES[k]) for k, w in weights.items()}}
        y = _forward(args)
    with _jax.named_scope("loss_head"):
        err = _jnp.square(y.astype(_jnp.float32) - loss_target)
        return 0.5 * _jnp.sum(_jnp.mean(err, axis=-1)) if err.ndim else 0.5 * err


def _adamw(w, g, m, v):
    m = ADAM_B1 * m + (1.0 - ADAM_B1) * g
    v = ADAM_B2 * v + (1.0 - ADAM_B2) * _jnp.square(g)
    m_hat = m / (1.0 - ADAM_B1 ** ADAM_STEP)
    v_hat = v / (1.0 - ADAM_B2 ** ADAM_STEP)
    delta = -ADAM_LR * (m_hat / (_jnp.sqrt(v_hat) + ADAM_EPS) + ADAM_WD * w)
    return delta, m, v


def reference(x, c, ada_w, ada_b, norm1_g, w_in, b_in, a_ln_g, a_ln_b, a_spatial_w, a_spatial_b, b_conv_w, b_conv_b, b_gn_g, b_gn_b, out_norm_a_g, out_norm_b_g, w_out, norm2_g, w_ffn_in, w_ffn_out, ada_f_w, ada_f_b, norm_f_g, loss_target, m_ada_w, m_ada_b, m_norm1_g, m_w_in, m_b_in, m_a_ln_g, m_a_ln_b, m_a_spatial_w, m_a_spatial_b, m_b_conv_w, m_b_conv_b, m_b_gn_g, m_b_gn_b, m_out_norm_a_g, m_out_norm_b_g, m_w_out, m_norm2_g, m_w_ffn_in, m_w_ffn_out, m_ada_f_w, m_ada_f_b, m_norm_f_g, v_ada_w, v_ada_b, v_norm1_g, v_w_in, v_b_in, v_a_ln_g, v_a_ln_b, v_a_spatial_w, v_a_spatial_b, v_b_conv_w, v_b_conv_b, v_b_gn_g, v_b_gn_b, v_out_norm_a_g, v_out_norm_b_g, v_w_out, v_norm2_g, v_w_ffn_in, v_w_ffn_out, v_ada_f_w, v_ada_f_b, v_norm_f_g):
    given = dict(x=x, c=c, ada_w=ada_w, ada_b=ada_b, norm1_g=norm1_g, w_in=w_in, b_in=b_in, a_ln_g=a_ln_g, a_ln_b=a_ln_b, a_spatial_w=a_spatial_w, a_spatial_b=a_spatial_b, b_conv_w=b_conv_w, b_conv_b=b_conv_b, b_gn_g=b_gn_g, b_gn_b=b_gn_b, out_norm_a_g=out_norm_a_g, out_norm_b_g=out_norm_b_g, w_out=w_out, norm2_g=norm2_g, w_ffn_in=w_ffn_in, w_ffn_out=w_ffn_out, ada_f_w=ada_f_w, ada_f_b=ada_f_b, norm_f_g=norm_f_g, loss_target=loss_target, m_ada_w=m_ada_w, m_ada_b=m_ada_b, m_norm1_g=m_norm1_g, m_w_in=m_w_in, m_b_in=m_b_in, m_a_ln_g=m_a_ln_g, m_a_ln_b=m_a_ln_b, m_a_spatial_w=m_a_spatial_w, m_a_spatial_b=m_a_spatial_b, m_b_conv_w=m_b_conv_w, m_b_conv_b=m_b_conv_b, m_b_gn_g=m_b_gn_g, m_b_gn_b=m_b_gn_b, m_out_norm_a_g=m_out_norm_a_g, m_out_norm_b_g=m_out_norm_b_g, m_w_out=m_w_out, m_norm2_g=m_norm2_g, m_w_ffn_in=m_w_ffn_in, m_w_ffn_out=m_w_ffn_out, m_ada_f_w=m_ada_f_w, m_ada_f_b=m_ada_f_b, m_norm_f_g=m_norm_f_g, v_ada_w=v_ada_w, v_ada_b=v_ada_b, v_norm1_g=v_norm1_g, v_w_in=v_w_in, v_b_in=v_b_in, v_a_ln_g=v_a_ln_g, v_a_ln_b=v_a_ln_b, v_a_spatial_w=v_a_spatial_w, v_a_spatial_b=v_a_spatial_b, v_b_conv_w=v_b_conv_w, v_b_conv_b=v_b_conv_b, v_b_gn_g=v_b_gn_g, v_b_gn_b=v_b_gn_b, v_out_norm_a_g=v_out_norm_a_g, v_out_norm_b_g=v_out_norm_b_g, v_w_out=v_w_out, v_norm2_g=v_norm2_g, v_w_ffn_in=v_w_ffn_in, v_w_ffn_out=v_w_ffn_out, v_ada_f_w=v_ada_f_w, v_ada_f_b=v_ada_f_b, v_norm_f_g=v_norm_f_g)
    weights = {n: given[n] for n in TWIN_WEIGHTS}
    shared = {n: given[n] for n in SHARED_INPUTS}
    per_example = {n: given[n] for n in ['x', 'c']}
    grad_fn = _jax.value_and_grad(_loss, argnums=(0, 1))

    def one_microbatch(ex, loss_target):
        ex = dict(ex)
        diff = ex.pop(TWIN_DIFF_INPUT)
        return grad_fn(weights, diff, {**shared, **ex}, loss_target)

    if N_MICROBATCH == 1:
        loss, (grad_w, grad_x) = one_microbatch(per_example, given["loss_target"])
    else:
        def body(carry, xs):
            loss_sum, grad_sum = carry
            l_k, (gw_k, gx_k) = one_microbatch(xs[0], xs[1])
            with _jax.named_scope("update"):
                return (loss_sum + l_k, _jax.tree.map(_jnp.add, grad_sum, gw_k)), gx_k

        init = (_jnp.zeros((), _jnp.float32), _jax.tree.map(_jnp.zeros_like, weights))
        (loss, grad_w), grad_x = _jax.lax.scan(body, init, (per_example, given["loss_target"]))
    with _jax.named_scope("update"):
        delta_w, new_m, new_v = {}, {}, {}
        for n in TWIN_WEIGHTS:
            delta_w[n], new_m[n], new_v[n] = _adamw(weights[n], grad_w[n], given["m_" + n], given["v_" + n])
    return (loss, grad_x, *[grad_w[n] for n in TWIN_WEIGHTS], *[delta_w[n] for n in TWIN_WEIGHTS],
            *[new_m[n] for n in TWIN_WEIGHTS], *[new_v[n] for n in TWIN_WEIGHTS])
```

```python
import functools
import math

import jax
import jax.numpy as jnp
from jax import lax
from jax.experimental import pallas as pl
from jax.experimental.pallas import tpu as pltpu

F32 = jnp.float32
BF16 = jnp.bfloat16

D = 1024
DA = 512
DB = 512
DIN = 2048
DFF = 2816
NH = 8
CH = 128
KW = 31
HALO = 32
NDEV = 8
EPS = 1e-6
NVEC = 56
VMEM_LIMIT = 56 * 1024 * 1024

ADAM_LR, ADAM_B1, ADAM_B2, ADAM_EPS, ADAM_WD, ADAM_STEP = 0.001, 0.9, 0.999, 1e-08, 0.01, 10

MESH = pl.DeviceIdType.MESH


def _dot(a, b):
    return jnp.dot(a, b, preferred_element_type=F32)


def _dot_nt(a, b):
    return lax.dot_general(a, b, (((1,), (1,)), ((), ())), preferred_element_type=F32)


def _dot_tn(a, b):
    return lax.dot_general(a, b, (((0,), (0,)), ((), ())), preferred_element_type=F32)


def _rs(v):
    return lax.rsqrt(jnp.mean(v * v, axis=-1, keepdims=True) + EPS)


def _sig(v):
    return 1.0 / (1.0 + jnp.exp(-v))


_INV_SQRT2 = 1.0 / math.sqrt(2.0)
_INV_SQRT2PI = 1.0 / math.sqrt(2.0 * math.pi)


def _gelu_parts(v):
    cdf = 0.5 * (1.0 + lax.erf(v * _INV_SQRT2))
    pdf = jnp.exp(-0.5 * v * v) * _INV_SQRT2PI
    return v * cdf, cdf + v * pdf


def _grp_mean(v, pm):
    hi = v.astype(BF16)
    lo = (v - hi.astype(F32)).astype(BF16)
    return _dot(hi, pm) + _dot(lo, pm)


def _colsum(v):
    return jnp.sum(v, axis=0, keepdims=True)


def _full(shape):
    nd = len(shape)
    return pl.BlockSpec(shape, lambda *_: (0,) * nd)


def _resident(shape):
    nd = len(shape)
    return pl.BlockSpec(shape, lambda *_: (0,) * nd, pipeline_mode=pl.Buffered(1))


def _head_of_lane(rows):
    return lax.broadcasted_iota(jnp.int32, (rows, DA), 1) >> 6


def _block_pick(r, lane_head):
    out = jnp.zeros((CH, DA), F32)
    for h in range(NH):
        out = jnp.where(lane_head == h, r[h * CH:(h + 1) * CH, :], out)
    return out


def _mix_fwd(x, mod, g1, win, b_in, lng, lnb, wcat, bsf, cw, cb, gng, gnb, oga, ogb, wout, pm, tm):
    T = x.shape[0]
    nt = T // tm
    nch = tm // CH

    def body(x_ref, mod_ref, g1_ref, win_ref, bin_ref, lng_ref, lnb_ref, wcat_ref, bsf_ref, cw_ref, cb_ref,
             gng_ref, gnb_ref, oga_ref, ogb_ref, wout_ref, pm_ref,
             x1_ref, h_ref, z_ref, mixed_ref, yc_ref, y_ref, o_ref, glbuf):
        i = pl.program_id(0)
        xv = x_ref[...]
        shift1 = mod_ref[0:1, :]
        scale1 = mod_ref[1:2, :]
        gate1 = mod_ref[2:3, :]
        h = (xv * _rs(xv) * g1_ref[...]) * (1.0 + scale1) + shift1
        hb = h.astype(BF16)
        h_ref[...] = hb
        z = _dot(hb, win_ref[...]) + bin_ref[...]
        z_ref[...] = z
        gu, _ = _gelu_parts(z[:, 0:DA])
        gv, _ = _gelu_parts(z[:, DA:2 * DA])
        xc = gv - jnp.mean(gv, axis=-1, keepdims=True)
        vn = xc * lax.rsqrt(jnp.mean(xc * xc, axis=-1, keepdims=True) + EPS) * lng_ref[...] + lnb_ref[...]
        vnb = vn.astype(BF16)
        lane_head = _head_of_lane(CH)
        chunks = []
        for ci in range(nch):
            r = _dot(wcat_ref[...], vnb[ci * CH:(ci + 1) * CH, :])
            chunks.append(_block_pick(r, lane_head) + bsf_ref[...])
        mixed = jnp.concatenate(chunks, axis=0) if nch > 1 else chunks[0]
        mixed_ref[...] = mixed
        ya = gu * mixed
        gl = z[:, 2 * DA:2 * DA + DB] * _sig(z[:, 2 * DA + DB:])

        @pl.when(i == 0)
        def _():
            glbuf[0:HALO, :] = jnp.zeros((HALO, DB), F32)

        glbuf[HALO:HALO + tm, :] = gl
        yc = jnp.zeros((tm, DB), F32) + cb_ref[...]
        for k in range(KW):
            off = HALO - (KW - 1) + k
            yc = yc + cw_ref[k:k + 1, :] * glbuf[off:off + tm, :]
        glbuf[0:HALO, :] = gl[tm - HALO:, :]
        yc_ref[...] = yc
        pmv = pm_ref[...]
        dc = yc - _grp_mean(yc, pmv)
        yg = dc * lax.rsqrt(_grp_mean(dc * dc, pmv) + EPS) * gng_ref[...] + gnb_ref[...]
        yb = yg * _sig(yg)
        na = ya * _rs(ya) * oga_ref[...]
        nb = yb * _rs(yb) * ogb_ref[...]
        yv = jnp.concatenate([na, nb], axis=1).astype(BF16)
        y_ref[...] = yv
        o = _dot(yv, wout_ref[...])
        o_ref[...] = o
        x1_ref[...] = xv + gate1 * o

    tile = lambda w: pl.BlockSpec((tm, w), lambda i: (i, 0))
    return pl.pallas_call(
        body,
        name="mix_fwd",
        grid=(nt,),
        in_specs=[tile(D), _full((8, D)), _full((1, D)), _resident((D, DIN)), _full((1, DIN)), _full((1, DA)),
                  _full((1, DA)), _full((NH * CH, CH)), _full((CH, DA)), _full((HALO, DB)), _full((1, DB)),
                  _full((1, DB)), _full((1, DB)), _full((1, DA)), _full((1, DB)), _resident((D, D)),
                  _full((DB, DB))],
        out_specs=[tile(D), tile(D), tile(DIN), tile(DA), tile(DB), tile(D), tile(D)],
        out_shape=[jax.ShapeDtypeStruct((T, D), F32), jax.ShapeDtypeStruct((T, D), BF16),
                   jax.ShapeDtypeStruct((T, DIN), F32), jax.ShapeDtypeStruct((T, DA), F32),
                   jax.ShapeDtypeStruct((T, DB), F32), jax.ShapeDtypeStruct((T, D), BF16),
                   jax.ShapeDtypeStruct((T, D), F32)],
        scratch_shapes=[pltpu.VMEM((HALO + tm, DB), F32)],
        compiler_params=pltpu.CompilerParams(dimension_semantics=("arbitrary",), vmem_limit_bytes=VMEM_LIMIT),
    )(x, mod, g1, win, b_in, lng, lnb, wcat, bsf, cw, cb, gng, gnb, oga, ogb, wout, pm)


FFB = 1408


def _ffn(x1, tgt, mod, g2, gf, wfi, wfo, tm):
    T = x1.shape[0]
    nt = T // tm
    nfb = DFF // FFB

    def body(x1_ref, tgt_ref, mod_ref, g2_ref, gf_ref, wfi_ref, wfo_ref,
             dx1_ref, h2_ref, dgu_ref, act_ref, dxg_ref, acc_ref, g_s, u_s):
        i = pl.program_id(0)

        @pl.when(i == 0)
        def _():
            acc_ref[...] = jnp.zeros((8, D), F32)

        x1 = x1_ref[...]
        shift2 = mod_ref[3:4, :]
        scale2 = mod_ref[4:5, :]
        gate2 = mod_ref[5:6, :]
        shiftf = mod_ref[6:7, :]
        scalef = mod_ref[7:8, :]
        g2v = g2_ref[...]
        gfv = gf_ref[...]
        r2 = _rs(x1)
        xn2 = x1 * r2
        h2b = (xn2 * g2v * (1.0 + scale2) + shift2).astype(BF16)
        h2_ref[...] = h2b
        f = jnp.zeros((tm, D), F32)
        for j in range(nfb):
            cs = slice(j * FFB, (j + 1) * FFB)
            g = _dot(h2b, wfi_ref[:, j * FFB:(j + 1) * FFB])
            u = _dot(h2b, wfi_ref[:, DFF + j * FFB:DFF + (j + 1) * FFB])
            g_s[:, cs] = g
            u_s[:, cs] = u
            actb = (g * _sig(g) * u).astype(BF16)
            act_ref[:, cs] = actb
            f = f + _dot(actb, wfo_ref[j * FFB:(j + 1) * FFB, :])
        x2 = x1 + gate2 * f
        rf = _rs(x2)
        xnf = x2 * rf
        out = xnf * gfv * (1.0 + scalef) + shiftf
        e = out - tgt_ref[...]
        dout = e * (1.0 / D)
        acc_ref[7:8, :] += _colsum(e * e)
        acc_ref[0:1, :] += _colsum(dout)
        acc_ref[1:2, :] += _colsum(dout * xnf * gfv)
        acc_ref[2:3, :] += _colsum(dout * (1.0 + scalef) * xnf)
        dxnf = dout * (1.0 + scalef) * gfv
        dx2 = rf * (dxnf - xnf * jnp.mean(dxnf * xnf, axis=-1, keepdims=True))
        acc_ref[3:4, :] += _colsum(dx2 * f)
        dxgb = (dx2 * gate2).astype(BF16)
        dxg_ref[...] = dxgb
        dh2 = jnp.zeros((tm, D), F32)
        for j in range(nfb):
            cs = slice(j * FFB, (j + 1) * FFB)
            dact = _dot_nt(dxgb, wfo_ref[j * FFB:(j + 1) * FFB, :])
            g = g_s[:, cs]
            u = u_s[:, cs]
            s = _sig(g)
            dgb = (dact * u * (s * (1.0 + g * (1.0 - s)))).astype(BF16)
            dub = (dact * (g * s)).astype(BF16)
            dgu_ref[:, j * FFB:(j + 1) * FFB] = dgb
            dgu_ref[:, DFF + j * FFB:DFF + (j + 1) * FFB] = dub
            dh2 = dh2 + _dot_nt(dgb, wfi_ref[:, j * FFB:(j + 1) * FFB])
            dh2 = dh2 + _dot_nt(dub, wfi_ref[:, DFF + j * FFB:DFF + (j + 1) * FFB])
        acc_ref[4:5, :] += _colsum(dh2)
        acc_ref[5:6, :] += _colsum(dh2 * xn2 * g2v)
        acc_ref[6:7, :] += _colsum(dh2 * (1.0 + scale2) * xn2)
        dxn2 = dh2 * (1.0 + scale2) * g2v
        dx1_ref[...] = dx2 + r2 * (dxn2 - xn2 * jnp.mean(dxn2 * xn2, axis=-1, keepdims=True))

    tile = lambda w: pl.BlockSpec((tm, w), lambda i: (i, 0))
    return pl.pallas_call(
        body,
        name="ffn_fwd_bwd",
        grid=(nt,),
        in_specs=[tile(D), tile(D), _full((8, D)), _full((1, D)), _full((1, D)),
                  _resident((D, 2 * DFF)), _resident((DFF, D))],
        out_specs=[tile(D), tile(D), tile(2 * DFF), tile(DFF), tile(D), _full((8, D))],
        out_shape=[jax.ShapeDtypeStruct((T, D), F32), jax.ShapeDtypeStruct((T, D), BF16),
                   jax.ShapeDtypeStruct((T, 2 * DFF), BF16), jax.ShapeDtypeStruct((T, DFF), BF16),
                   jax.ShapeDtypeStruct((T, D), BF16), jax.ShapeDtypeStruct((8, D), F32)],
        scratch_shapes=[pltpu.VMEM((tm, DFF), F32), pltpu.VMEM((tm, DFF), F32)],
        compiler_params=pltpu.CompilerParams(dimension_semantics=("arbitrary",), vmem_limit_bytes=VMEM_LIMIT),
    )(x1, tgt, mod, g2, gf, wfi, wfo)


def _mix_bwd(dx1, x, z, mixed, yc, o, mod, g1, win, lng, lnb, wcat, wcat_t, cw, gng, gnb, oga, ogb, wout, pm,
             esel, tm):
    T = x.shape[0]
    nt = T // tm
    nch = tm // CH

    def body(dx1_ref, x_ref, z_ref, mixed_ref, yc_ref, o_ref, mod_ref, g1_ref, win_ref, lng_ref, lnb_ref,
             wcat_ref, wcatt_ref, cw_ref, gng_ref, gnb_ref, oga_ref, ogb_ref, wout_ref, pm_ref, esel_ref,
             gx_ref, dz_ref, dog_ref, accv_ref, accb_ref, acca_ref, accbs_ref, accws_ref, acccw_ref,
             dycbuf, bs_s):
        i = pl.program_id(0)

        @pl.when(i == 0)
        def _():
            accv_ref[...] = jnp.zeros((8, D), F32)
            accb_ref[...] = jnp.zeros((1, DIN), F32)
            acca_ref[...] = jnp.zeros((8, DA), F32)
            accws_ref[...] = jnp.zeros((NH * CH, CH), F32)
            acccw_ref[...] = jnp.zeros((HALO, DB), F32)
            bs_s[...] = jnp.zeros((CH, DA), F32)
            dycbuf[tm:tm + HALO, :] = jnp.zeros((HALO, DB), F32)

        shift1 = mod_ref[0:1, :]
        scale1 = mod_ref[1:2, :]
        gate1 = mod_ref[2:3, :]
        g1v = g1_ref[...]
        xv = x_ref[...]
        r1 = _rs(xv)
        xn1 = xv * r1
        z = z_ref[...]
        u = z[:, 0:DA]
        v = z[:, DA:2 * DA]
        val = z[:, 2 * DA:2 * DA + DB]
        gate = z[:, 2 * DA + DB:]
        gu, dgelu_u = _gelu_parts(u)
        gv, dgelu_v = _gelu_parts(v)
        xc = gv - jnp.mean(gv, axis=-1, keepdims=True)
        rsl = lax.rsqrt(jnp.mean(xc * xc, axis=-1, keepdims=True) + EPS)
        vhat = xc * rsl
        lngv = lng_ref[...]
        vnb = (vhat * lngv + lnb_ref[...]).astype(BF16)
        mixed = mixed_ref[...]
        ya = gu * mixed
        ra = _rs(ya)
        yan = ya * ra
        sgt = _sig(gate)
        gl = val * sgt
        pmv = pm_ref[...]
        ycv = yc_ref[...]
        dcen = ycv - _grp_mean(ycv, pmv)
        rsg = lax.rsqrt(_grp_mean(dcen * dcen, pmv) + EPS)
        yhat = dcen * rsg
        gngv = gng_ref[...]
        yg = yhat * gngv + gnb_ref[...]
        sgy = _sig(yg)
        yb = yg * sgy
        rb = _rs(yb)
        ybn = yb * rb
        dx1 = dx1_ref[...]
        accv_ref[0:1, :] += _colsum(dx1 * o_ref[...])
        dogb = (dx1 * gate1).astype(BF16)
        dog_ref[...] = dogb
        dy = _dot_nt(dogb, wout_ref[...])
        dna = dy[:, 0:DA]
        dnb = dy[:, DA:]
        ogav = oga_ref[...]
        ogbv = ogb_ref[...]
        acca_ref[2:3, :] += _colsum(dna * yan)
        acca_ref[3:4, :] += _colsum(dnb * ybn)
        ta = dna * ogav
        dya = ra * (ta - yan * jnp.mean(ta * yan, axis=-1, keepdims=True))
        tb = dnb * ogbv
        dyb = rb * (tb - ybn * jnp.mean(tb * ybn, axis=-1, keepdims=True))
        dgu = dya * mixed
        dm = dya * gu
        lane_head = _head_of_lane(CH)
        dvn_chunks = []
        bs_acc = bs_s[...]
        for ci in range(nch):
            dmc = dm[ci * CH:(ci + 1) * CH, :]
            bs_acc = bs_acc + dmc
            dmcb = dmc.astype(BF16)
            dvn_chunks.append(_block_pick(_dot(wcatt_ref[...], dmcb), lane_head))
            zero = jnp.zeros((CH, DA), BF16)
            stack = jnp.concatenate([jnp.where(lane_head == h, dmcb, zero) for h in range(NH)], axis=0)
            accws_ref[...] += _dot_nt(stack, vnb[ci * CH:(ci + 1) * CH, :])
        bs_s[...] = bs_acc
        dvn = jnp.concatenate(dvn_chunks, axis=0) if nch > 1 else dvn_chunks[0]
        acca_ref[0:1, :] += _colsum(dvn * vhat)
        acca_ref[1:2, :] += _colsum(dvn)
        dvh = dvn * lngv
        dgv = rsl * (dvh - jnp.mean(dvh, axis=-1, keepdims=True)
                     - vhat * jnp.mean(dvh * vhat, axis=-1, keepdims=True))
        du = dgu * dgelu_u
        dv = dgv * dgelu_v
        dyg = dyb * (sgy * (1.0 + yg * (1.0 - sgy)))
        acca_ref[5:6, :] += _colsum(dyg * yhat)
        acca_ref[6:7, :] += _colsum(dyg)
        dyh = dyg * gngv
        dyc = rsg * (dyh - _grp_mean(dyh, pmv) - yhat * _grp_mean(dyh * yhat, pmv))
        acca_ref[4:5, :] += _colsum(dyc)
        dycbuf[0:tm, :] = dyc
        dgl = jnp.zeros((tm, DB), F32)
        for k in range(KW):
            off = KW - 1 - k
            win_k = dycbuf[off:off + tm, :]
            dgl = dgl + cw_ref[k:k + 1, :] * win_k
            acccw_ref[k:k + 1, :] += _colsum(win_k * gl)
        dycbuf[tm:tm + HALO, :] = dyc[0:HALO, :]
        dval = dgl * sgt
        dgate = dgl * val * sgt * (1.0 - sgt)
        dz = jnp.concatenate([du, dv, dval, dgate], axis=1)
        accb_ref[...] += _colsum(dz)
        dzb = dz.astype(BF16)
        dz_ref[...] = dzb
        dh = _dot_nt(dzb, win_ref[...])
        accv_ref[1:2, :] += _colsum(dh)
        accv_ref[2:3, :] += _colsum(dh * xn1 * g1v)
        accv_ref[3:4, :] += _colsum(dh * (1.0 + scale1) * xn1)
        dxn1 = dh * (1.0 + scale1) * g1v
        gx_ref[...] = dx1 + r1 * (dxn1 - xn1 * jnp.mean(dxn1 * xn1, axis=-1, keepdims=True))

        @pl.when(i == nt - 1)
        def _():
            rows = lax.broadcasted_iota(jnp.int32, (NH * CH, CH), 0) & (CH - 1)
            cols = lax.broadcasted_iota(jnp.int32, (NH * CH, CH), 1)
            accws_ref[...] = jnp.where(cols <= rows, accws_ref[...], 0.0)
            bs = bs_s[...]
            hi = bs.astype(BF16)
            r1_ = bs - hi.astype(F32)
            mid = r1_.astype(BF16)
            lo = (r1_ - mid.astype(F32)).astype(BF16)
            ev = esel_ref[...]
            accbs_ref[...] = _dot(hi, ev) + _dot(mid, ev) + _dot(lo, ev)

    rev = lambda w: pl.BlockSpec((tm, w), lambda i: (nt - 1 - i, 0))
    return pl.pallas_call(
        body,
        name="mix_bwd",
        grid=(nt,),
        in_specs=[rev(D), rev(D), rev(DIN), rev(DA), rev(DB), rev(D), _full((8, D)), _full((1, D)),
                  _resident((D, DIN)), _full((1, DA)), _full((1, DA)), _full((NH * CH, CH)), _full((NH * CH, CH)),
                  _full((HALO, DB)), _full((1, DB)), _full((1, DB)), _full((1, DA)), _full((1, DB)),
                  _resident((D, D)), _full((DB, DB)), _full((DA, CH))],
        out_specs=[rev(D), rev(DIN), rev(D), _full((8, D)), _full((1, DIN)), _full((8, DA)), _full((CH, CH)),
                   _full((NH * CH, CH)), _full((HALO, DB))],
        out_shape=[jax.ShapeDtypeStruct((T, D), F32), jax.ShapeDtypeStruct((T, DIN), BF16),
                   jax.ShapeDtypeStruct((T, D), BF16), jax.ShapeDtypeStruct((8, D), F32),
                   jax.ShapeDtypeStruct((1, DIN), F32), jax.ShapeDtypeStruct((8, DA), F32),
                   jax.ShapeDtypeStruct((CH, CH), F32), jax.ShapeDtypeStruct((NH * CH, CH), F32),
                   jax.ShapeDtypeStruct((HALO, DB), F32)],
        scratch_shapes=[pltpu.VMEM((tm + HALO, DB), F32), pltpu.VMEM((CH, DA), F32)],
        compiler_params=pltpu.CompilerParams(dimension_semantics=("arbitrary",), vmem_limit_bytes=VMEM_LIMIT),
    )(dx1, x, z, mixed, yc, o, mod, g1, win, lng, lnb, wcat, wcat_t, cw, gng, gnb, oga, ogb, wout, pm, esel)


def _wgrad(a, b, bm, bn, tk, name):
    T, M = a.shape
    N = b.shape[1]
    nk = T // tk

    def body(a_ref, b_ref, o_ref, acc):
        k = pl.program_id(2)

        @pl.when(k == 0)
        def _():
            acc[...] = jnp.zeros((bm, bn), F32)

        acc[...] += _dot_tn(a_ref[...], b_ref[...])

        @pl.when(k == nk - 1)
        def _():
            o_ref[...] = acc[...].astype(BF16)

    return pl.pallas_call(
        body,
        name=name,
        grid=(M // bm, N // bn, nk),
        in_specs=[pl.BlockSpec((tk, bm), lambda i, j, k: (k, i)), pl.BlockSpec((tk, bn), lambda i, j, k: (k, j))],
        out_specs=pl.BlockSpec((bm, bn), lambda i, j, k: (i, j)),
        out_shape=jax.ShapeDtypeStruct((M, N), BF16),
        scratch_shapes=[pltpu.VMEM((bm, bn), F32)],
        compiler_params=pltpu.CompilerParams(dimension_semantics=("arbitrary", "arbitrary", "arbitrary"),
                                             vmem_limit_bytes=VMEM_LIMIT),
    )(a, b)


def _place():
    x, y, c = lax.axis_index("x"), lax.axis_index("y"), lax.axis_index("c")
    return x, y, c, 4 * x + 2 * y + c


def _dev(t):
    return (t >> 2, (t >> 1) & 1, t & 1)


def _gather(c_row, ada_w, ada_b8, ada_f_w, ada_f_b8, conv_s, shards):
    nw = len(shards)

    def body(c_ref, adaw_ref, adab_ref, adafw_ref, adafb_ref, conv_ref, *rest):
        w_in = rest[:nw]
        call_ref, cparts_ref, cfparts_ref, convg_ref = rest[nw:nw + 4]
        w_out = rest[nw + 4:2 * nw + 4]
        part_s, partf_s, wss, wrs, lsem, s1, r1, s2, r2, s3, r3, s4, r4 = rest[2 * nw + 4:]
        x, y, c, idx = _place()
        me, sibling = (x, y, c), (x, y, 1 - c)
        chips = [(1 - x, y), (x, 1 - y), (1 - x, 1 - y)]

        def blk(p):
            return 4 * p[0] + 2 * p[1] + p[2]

        def wcopy(a, k, block, to, src=None):
            dst = w_out[a].at[blk(block)]
            return pltpu.make_async_remote_copy(src_ref=dst if src is None else src, dst_ref=dst,
                                                send_sem=wss.at[a, k], recv_sem=wrs.at[a, k],
                                                device_id=to, device_id_type=MESH)

        mine = [pltpu.make_async_copy(w_in[a], w_out[a].at[idx], lsem.at[a]) for a in range(nw)]
        for cp in mine:
            cp.start()
        first = []
        for a in range(nw):
            first.append(wcopy(a, 0, me, sibling, src=w_in[a]))
            first += [wcopy(a, 1 + j, me, (*chip, c), src=w_in[a]) for j, chip in enumerate(chips)]
        for cp in first:
            cp.start()

        def small(src, dst, ss, rs, k, to):
            return pltpu.make_async_remote_copy(src_ref=src, dst_ref=dst, send_sem=ss.at[k], recv_sem=rs.at[k],
                                                device_id=to, device_id_type=MESH)

        call_ref[pl.ds(idx, 1), :] = c_ref[...]
        convg_ref[idx] = conv_ref[...]
        ph1 = []
        for k in range(1, NDEV):
            to = _dev(idx ^ k)
            ph1.append(small(c_ref, call_ref.at[pl.ds(idx, 1)], s1, r1, k - 1, to))
            ph1.append(small(conv_ref, convg_ref.at[idx], s2, r2, k - 1, to))
        for cp in ph1:
            cp.start()
        for k in range(1, NDEV):
            src_dev = idx ^ k
            small(c_ref, call_ref.at[pl.ds(src_dev, 1)], s1, r1, k - 1, me).wait_recv()
            small(conv_ref, convg_ref.at[src_dev], s2, r2, k - 1, me).wait_recv()
        call = call_ref[...]
        cact = (call * _sig(call))
        part_s[...] = jnp.dot(cact, adaw_ref[...], preferred_element_type=F32,
                              precision=lax.Precision.HIGHEST) + adab_ref[pl.ds(idx, 1), :]
        partf_s[...] = jnp.dot(cact, adafw_ref[...], preferred_element_type=F32,
                               precision=lax.Precision.HIGHEST) + adafb_ref[pl.ds(idx, 1), :]
        cparts_ref[pl.ds(idx, 1), :] = part_s[pl.ds(idx, 1), :]
        cfparts_ref[pl.ds(idx, 1), :] = partf_s[pl.ds(idx, 1), :]
        ph2 = []
        for k in range(1, NDEV):
            t = idx ^ k
            ph2.append(small(part_s.at[pl.ds(t, 1)], cparts_ref.at[pl.ds(idx, 1)], s3, r3, k - 1, _dev(t)))
            ph2.append(small(partf_s.at[pl.ds(t, 1)], cfparts_ref.at[pl.ds(idx, 1)], s4, r4, k - 1, _dev(t)))
        for cp in ph2:
            cp.start()
        for k in range(1, NDEV):
            src_dev = idx ^ k
            small(part_s.at[pl.ds(0, 1)], cparts_ref.at[pl.ds(src_dev, 1)], s3, r3, k - 1, me).wait_recv()
            small(partf_s.at[pl.ds(0, 1)], cfparts_ref.at[pl.ds(src_dev, 1)], s4, r4, k - 1, me).wait_recv()
        for cp in ph1 + ph2:
            cp.wait_send()

        passed = [[wcopy(a, 4 + j, (*chip, c), sibling) for j, chip in enumerate(chips)] for a in range(nw)]
        for j, chip in enumerate(chips):
            for a in range(nw):
                wcopy(a, 1 + j, (*chip, c), me).wait_recv()
                passed[a][j].start()
        for a in range(nw):
            wcopy(a, 0, sibling, me).wait_recv()
            for j, chip in enumerate(chips):
                wcopy(a, 4 + j, (*chip, 1 - c), me).wait_recv()
        for cp in first:
            cp.wait_send()
        for a in range(nw):
            for cp in passed[a]:
                cp.wait_send()
        for cp in mine:
            cp.wait()

    vm = pl.BlockSpec(memory_space=pltpu.VMEM)
    hbm = pl.BlockSpec(memory_space=pl.ANY)
    dma7 = pltpu.SemaphoreType.DMA((NDEV - 1,))
    outs = pl.pallas_call(
        body,
        name="gather_weights",
        in_specs=[vm] * 6 + [hbm] * nw,
        out_specs=[vm] * 4 + [hbm] * nw,
        out_shape=[jax.ShapeDtypeStruct((NDEV, D), F32), jax.ShapeDtypeStruct((NDEV, ada_w.shape[1]), F32),
                   jax.ShapeDtypeStruct((NDEV, ada_f_w.shape[1]), F32),
                   jax.ShapeDtypeStruct((NDEV,) + conv_s.shape, F32)]
                  + [jax.ShapeDtypeStruct((NDEV,) + s.shape, s.dtype) for s in shards],
        scratch_shapes=[pltpu.VMEM((NDEV, ada_w.shape[1]), F32), pltpu.VMEM((NDEV, ada_f_w.shape[1]), F32),
                        pltpu.SemaphoreType.DMA((nw, 7)), pltpu.SemaphoreType.DMA((nw, 7)),
                        pltpu.SemaphoreType.DMA((nw,)),
                        dma7, dma7, dma7, dma7, dma7, dma7, dma7, dma7],
        compiler_params=pltpu.CompilerParams(vmem_limit_bytes=VMEM_LIMIT),
    )(c_row, ada_w, ada_b8, ada_f_w, ada_f_b8, conv_s, *shards)
    return outs[0], outs[1], outs[2], outs[3], outs[4:]


def _reduce(vecs, dws, grads):
    nw = len(grads)

    def body(vecs_ref, dws_ref, *rest):
        g_in = rest[:nw]
        vsum_ref, dcond_ref, wssum_ref = rest[nw:nw + 3]
        r_out = rest[nw + 3:2 * nw + 3]
        vbuf, wbuf, gss, grs, lsem, s1, r1, s2, r2 = rest[2 * nw + 3:]
        x, y, c, idx = _place()
        me = (x, y, c)
        mine = [pltpu.make_async_copy(g_in[a].at[idx], r_out[a].at[0], lsem.at[a]) for a in range(nw)]
        for cp in mine:
            cp.start()
        big = []
        for k in range(1, NDEV):
            t = idx ^ k
            for a in range(nw):
                big.append(pltpu.make_async_remote_copy(src_ref=g_in[a].at[t], dst_ref=r_out[a].at[k],
                                                        send_sem=gss.at[a, k - 1], recv_sem=grs.at[a, k - 1],
                                                        device_id=_dev(t), device_id_type=MESH))
        vbuf[idx] = vecs_ref[...]
        wbuf[idx] = dws_ref[...]
        sm = []
        for k in range(1, NDEV):
            to = _dev(idx ^ k)
            sm.append(pltpu.make_async_remote_copy(src_ref=vecs_ref, dst_ref=vbuf.at[idx], send_sem=s1.at[k - 1],
                                                   recv_sem=r1.at[k - 1], device_id=to, device_id_type=MESH))
            sm.append(pltpu.make_async_remote_copy(src_ref=dws_ref, dst_ref=wbuf.at[idx], send_sem=s2.at[k - 1],
                                                   recv_sem=r2.at[k - 1], device_id=to, device_id_type=MESH))
        for cp in sm:
            cp.start()
        for cp in big:
            cp.start()
        for k in range(1, NDEV):
            src_dev = idx ^ k
            pltpu.make_async_remote_copy(src_ref=vecs_ref, dst_ref=vbuf.at[src_dev], send_sem=s1.at[k - 1],
                                         recv_sem=r1.at[k - 1], device_id=me, device_id_type=MESH).wait_recv()
            pltpu.make_async_remote_copy(src_ref=dws_ref, dst_ref=wbuf.at[src_dev], send_sem=s2.at[k - 1],
                                         recv_sem=r2.at[k - 1], device_id=me, device_id_type=MESH).wait_recv()
        vs = vbuf[0]
        ws = wbuf[0]
        for d in range(1, NDEV):
            vs = vs + vbuf[d]
            ws = ws + wbuf[d]
        vsum_ref[...] = vs
        wssum_ref[...] = ws
        for d in range(NDEV):
            dcond_ref[d] = vbuf[d, 0:8, :]
        for cp in sm:
            cp.wait_send()
        for k in range(1, NDEV):
            for a in range(nw):
                pltpu.make_async_remote_copy(src_ref=g_in[a].at[0], dst_ref=r_out[a].at[k],
                                             send_sem=gss.at[a, k - 1], recv_sem=grs.at[a, k - 1],
                                             device_id=me, device_id_type=MESH).wait_recv()
        for cp in big:
            cp.wait_send()
        for cp in mine:
            cp.wait()

    vm = pl.BlockSpec(memory_space=pltpu.VMEM)
    hbm = pl.BlockSpec(memory_space=pl.ANY)
    dma7 = pltpu.SemaphoreType.DMA((NDEV - 1,))
    outs = pl.pallas_call(
        body,
        name="reduce_grads",
        in_specs=[vm, vm] + [hbm] * nw,
        out_specs=[vm, vm, vm] + [hbm] * nw,
        out_shape=[jax.ShapeDtypeStruct(vecs.shape, F32), jax.ShapeDtypeStruct((NDEV, 8, D), F32),
                   jax.ShapeDtypeStruct(dws.shape, F32)]
                  + [jax.ShapeDtypeStruct(g.shape, g.dtype) for g in grads],
        scratch_shapes=[pltpu.VMEM((NDEV,) + vecs.shape, F32), pltpu.VMEM((NDEV,) + dws.shape, F32),
                        pltpu.SemaphoreType.DMA((nw, 7)), pltpu.SemaphoreType.DMA((nw, 7)),
                        pltpu.SemaphoreType.DMA((nw,)), dma7, dma7, dma7, dma7],
        compiler_params=pltpu.CompilerParams(vmem_limit_bytes=VMEM_LIMIT),
    )(vecs, dws, *grads)
    return outs[0], outs[1], outs[2], outs[3:]


def _adamw(w, g, m, v):
    m2 = ADAM_B1 * m + (1.0 - ADAM_B1) * g
    v2 = ADAM_B2 * v + (1.0 - ADAM_B2) * (g * g)
    m_hat = m2 / (1.0 - ADAM_B1 ** ADAM_STEP)
    v_hat = v2 / (1.0 - ADAM_B2 ** ADAM_STEP)
    delta = -ADAM_LR * (m_hat / (jnp.sqrt(v_hat) + ADAM_EPS) + ADAM_WD * w)
    return delta, m2, v2


def _adam_big(r, w, m, v, rb, name):
    R, C = w.shape

    def body(r_ref, w_ref, m_ref, v_ref, g_ref, d_ref, m2_ref, v2_ref):
        g = r_ref[0].astype(F32)
        for k in range(1, NDEV):
            g = g + r_ref[k].astype(F32)
        g_ref[...] = g
        d_ref[...], m2_ref[...], v2_ref[...] = _adamw(w_ref[...], g, m_ref[...], v_ref[...])

    t2 = pl.BlockSpec((rb, C), lambda i: (i, 0))
    sd = jax.ShapeDtypeStruct((R, C), F32)
    return pl.pallas_call(
        body, name=name, grid=(R // rb,),
        in_specs=[pl.BlockSpec((NDEV, rb, C), lambda i: (0, i, 0)), t2, t2, t2],
        out_specs=[t2, t2, t2, t2], out_shape=[sd, sd, sd, sd],
        compiler_params=pltpu.CompilerParams(dimension_semantics=("arbitrary",), vmem_limit_bytes=VMEM_LIMIT),
    )(r, w, m, v)


def _adam_ada(cact_t, dcs, w, m, v, rb, name):
    R, C = w.shape

    def body(ct_ref, dc_ref, w_ref, m_ref, v_ref, g_ref, d_ref, m2_ref, v2_ref):
        g = jnp.dot(ct_ref[...], dc_ref[...], preferred_element_type=F32, precision=lax.Precision.HIGHEST)
        g_ref[...] = g
        d_ref[...], m2_ref[...], v2_ref[...] = _adamw(w_ref[...], g, m_ref[...], v_ref[...])

    t2 = pl.BlockSpec((rb, C), lambda i: (i, 0))
    sd = jax.ShapeDtypeStruct((R, C), F32)
    return pl.pallas_call(
        body, name=name, grid=(R // rb,),
        in_specs=[pl.BlockSpec((rb, NDEV), lambda i: (i, 0)), _full((NDEV, C)), t2, t2, t2],
        out_specs=[t2, t2, t2, t2], out_shape=[sd, sd, sd, sd],
        compiler_params=pltpu.CompilerParams(dimension_semantics=("arbitrary",), vmem_limit_bytes=VMEM_LIMIT),
    )(cact_t, dcs, w, m, v)


_VEC_AT = {
    "norm1_g": (8, 0, D), "a_ln_g": (11, 0, DA), "a_ln_b": (11, DA, DA), "a_spatial_b": (12, 0, D),
    "b_conv_b": (13, 0, DB), "b_gn_g": (13, DB, DB), "b_gn_b": (14, 0, DB), "out_norm_a_g": (14, DB, DA),
    "out_norm_b_g": (15, 0, DB), "norm2_g": (16, 0, D), "norm_f_g": (17, 0, D),
}
_CW_ROW = 18
_SMALL = ["ada_b", "ada_f_b", "norm1_g", "b_in", "a_ln_g", "a_ln_b", "a_spatial_b", "b_conv_b", "b_gn_g", "b_gn_b",
          "out_norm_a_g", "out_norm_b_g", "norm2_g", "norm_f_g", "a_spatial_w", "b_conv_w"]


def _adam_small(vsum, wssum, gcw, params):
    names = _SMALL
    flat = []
    for n in names:
        flat += list(params[n])

    def body(vs_ref, ws_ref, gcw_ref, *rest):
        ins = rest[:3 * len(names)]
        outs = rest[3 * len(names):]
        for pi, n in enumerate(names):
            w_ref, m_ref, v_ref = ins[3 * pi:3 * pi + 3]
            g_ref, d_ref, m2_ref, v2_ref = outs[4 * pi:4 * pi + 4]
            if n in ("ada_b", "ada_f_b", "b_in"):
                row0 = {"ada_b": 0, "ada_f_b": 6, "b_in": 9}[n]
                pieces = [(vs_ref[row0 + r:row0 + r + 1, :], slice(r * D, (r + 1) * D))
                          for r in range(w_ref.shape[1] // D)]
            elif n == "a_spatial_w":
                pieces = [(ws_ref[...], slice(None))]
            elif n == "b_conv_w":
                pieces = [(gcw_ref[...], slice(None))]
            else:
                row, off, width = _VEC_AT[n]
                pieces = [(vs_ref[row:row + 1, off:off + width], slice(None))]
            for g, cs in pieces:
                g_ref[:, cs] = g
                d_ref[:, cs], m2_ref[:, cs], v2_ref[:, cs] = _adamw(w_ref[:, cs], g, m_ref[:, cs], v_ref[:, cs])

    vm = pl.BlockSpec(memory_space=pltpu.VMEM)
    out_shape = []
    for n in names:
        out_shape += [jax.ShapeDtypeStruct(params[n][0].shape, F32)] * 4
    outs = pl.pallas_call(
        body, name="adam_small",
        in_specs=[vm] * (3 + len(flat)), out_specs=[vm] * len(out_shape), out_shape=out_shape,
        compiler_params=pltpu.CompilerParams(vmem_limit_bytes=VMEM_LIMIT),
    )(vsum, wssum, gcw, *flat)
    return {n: outs[4 * pi:4 * pi + 4] for pi, n in enumerate(names)}


def _token_tile(T, want):
    return want if T % want == 0 else T


def kernel(x, c, ada_w, ada_b, norm1_g, w_in, b_in, a_ln_g, a_ln_b, a_spatial_w, a_spatial_b, b_conv_w, b_conv_b, b_gn_g, b_gn_b, out_norm_a_g, out_norm_b_g, w_out, norm2_g, w_ffn_in, w_ffn_out, ada_f_w, ada_f_b, norm_f_g, loss_target, m_ada_w, m_ada_b, m_norm1_g, m_w_in, m_b_in, m_a_ln_g, m_a_ln_b, m_a_spatial_w, m_a_spatial_b, m_b_conv_w, m_b_conv_b, m_b_gn_g, m_b_gn_b, m_out_norm_a_g, m_out_norm_b_g, m_w_out, m_norm2_g, m_w_ffn_in, m_w_ffn_out, m_ada_f_w, m_ada_f_b, m_norm_f_g, v_ada_w, v_ada_b, v_norm1_g, v_w_in, v_b_in, v_a_ln_g, v_a_ln_b, v_a_spatial_w, v_a_spatial_b, v_b_conv_w, v_b_conv_b, v_b_gn_g, v_b_gn_b, v_out_norm_a_g, v_out_norm_b_g, v_w_out, v_norm2_g, v_w_ffn_in, v_w_ffn_out, v_ada_f_w, v_ada_f_b, v_norm_f_g):
    T = x.shape[1]
    idx = 4 * lax.axis_index("x") + 2 * lax.axis_index("y") + lax.axis_index("c")
    x2d = x.reshape(T, D)
    tgt = loss_target.reshape(T, D)

    conv_s = jnp.pad(b_conv_w[0], ((0, HALO - KW), (0, 0)))
    shards = [w_in[0].astype(BF16), w_out[0].astype(BF16), w_ffn_in[0].astype(BF16), w_ffn_out[0].astype(BF16)]
    call, cparts, cfparts, convg, (win_g, wout_g, wfi_g, wfo_g) = _gather(
        c, ada_w[0], ada_b.reshape(NDEV, -1), ada_f_w, ada_f_b.reshape(NDEV, -1), conv_s, shards)
    mod = jnp.concatenate([cparts.reshape(6, D), cfparts.reshape(2, D)], axis=0)
    win = jnp.transpose(win_g, (1, 0, 2)).reshape(D, DIN)
    wout = wout_g.reshape(D, D)
    wfi = jnp.transpose(wfi_g, (1, 0, 2)).reshape(D, 2 * DFF)
    wfo = wfo_g.reshape(DFF, D)
    cw = jnp.transpose(convg, (1, 0, 2)).reshape(HALO, DB)

    tril = jnp.tril(jnp.ones((CH, CH), dtype=bool))
    wsm = jnp.where(tril[None], a_spatial_w[0], 0.0).astype(BF16)
    wcat = wsm.reshape(NH * CH, CH)
    wcat_t = jnp.transpose(wsm, (0, 2, 1)).reshape(NH * CH, CH)
    bsf = jnp.repeat(a_spatial_b[0].T, DA // NH, axis=1)
    lane = jnp.arange(DB)
    pm = jnp.where((lane[:, None] >> 6) == (lane[None, :] >> 6), 1.0 / 64.0, 0.0).astype(BF16)
    esel = jnp.where((lane[:, None] >> 6) == jnp.arange(CH)[None, :], 1.0, 0.0).astype(BF16)

    tm = _token_tile(T, 256)
    x1, hb, z, mixed, yc, yb, o = _mix_fwd(x2d, mod, norm1_g, win, b_in, a_ln_g, a_ln_b, wcat, bsf, cw, b_conv_b,
                                           b_gn_g, b_gn_b, out_norm_a_g, out_norm_b_g, wout, pm, tm)
    dx1, h2b, dgu, act, dxg, acc_f = _ffn(x1, tgt, mod, norm2_g, norm_f_g.reshape(1, D), wfi, wfo, tm)
    gx, dz, dog, acc_v, acc_b, acc_a, acc_bs, acc_ws, acc_cw = _mix_bwd(
        dx1, x2d, z, mixed, yc, o, mod, norm1_g, win, a_ln_g, a_ln_b, wcat, wcat_t, cw, b_gn_g, b_gn_b,
        out_norm_a_g, out_norm_b_g, wout, pm, esel, tm)

    tk = _token_tile(T, 1024)
    g_win = _wgrad(hb, dz, D, DIN // 2, tk, "wgrad_in")
    g_wout = _wgrad(yb, dog, D, D, tk, "wgrad_out")
    g_wfi = _wgrad(h2b, dgu, D, FFB, tk, "wgrad_ffn_in")
    g_wfo = _wgrad(act, dxg, FFB, D, tk, "wgrad_ffn_out")
    grads = [jnp.transpose(g_win.reshape(D, NDEV, DIN // NDEV), (1, 0, 2)),
             g_wout.reshape(NDEV, D // NDEV, D),
             jnp.transpose(g_wfi.reshape(D, NDEV, 2 * DFF // NDEV), (1, 0, 2)),
             g_wfo.reshape(NDEV, DFF // NDEV, D)]

    half = lambda a, b_: jnp.concatenate([a, b_], axis=1)
    zero_h = jnp.zeros((1, DA), F32)
    dbs = acc_bs[:, 0:NH].T.reshape(1, D)
    rows = [acc_v[1:2], acc_v[2:3], acc_v[0:1], acc_f[4:5], acc_f[5:6], acc_f[3:4], acc_f[0:1], acc_f[1:2],
            acc_v[3:4], acc_b[:, 0:D], acc_b[:, D:], half(acc_a[0:1], acc_a[1:2]), dbs,
            half(acc_a[4:5], acc_a[5:6]), half(acc_a[6:7], acc_a[2:3]), half(acc_a[3:4], zero_h),
            acc_f[6:7], acc_f[2:3],
            jnp.pad(acc_cw, ((0, 0), (0, D - DB))),
            jnp.zeros((NVEC - _CW_ROW - HALO, D), F32)]
    vecs = jnp.concatenate(rows, axis=0)

    vsum, dcond_all, wssum, (r_win, r_wout, r_wfi, r_wfo) = _reduce(vecs, acc_ws, grads)

    res = {}
    res["w_in"] = _adam_big(r_win, w_in[0], m_w_in[0], v_w_in[0], 256, "adam_w_in")
    res["w_out"] = _adam_big(r_wout, w_out[0], m_w_out[0], v_w_out[0], D // NDEV, "adam_w_out")
    res["w_ffn_in"] = _adam_big(r_wfi, w_ffn_in[0], m_w_ffn_in[0], v_w_ffn_in[0], 256, "adam_w_ffn_in")
    res["w_ffn_out"] = _adam_big(r_wfo, w_ffn_out[0], m_w_ffn_out[0], v_w_ffn_out[0], DFF // NDEV // 2,
                                 "adam_w_ffn_out")
    cact_t = (call * jax.nn.sigmoid(call)).T
    dcond = dcond_all.reshape(NDEV, 8 * D)
    nada = ada_w.shape[2]
    nadf = ada_f_w.shape[1]
    dcs = lax.dynamic_slice(dcond, (0, idx * nada), (NDEV, nada))
    dcfs = lax.dynamic_slice(dcond, (0, 6 * D + idx * nadf), (NDEV, nadf))
    res["ada_w"] = _adam_ada(cact_t, dcs, ada_w[0], m_ada_w[0], v_ada_w[0], 256, "adam_ada_w")
    res["ada_f_w"] = _adam_ada(cact_t, dcfs, ada_f_w, m_ada_f_w, v_ada_f_w, 256, "adam_ada_f_w")
    ncw = b_conv_w.shape[2]
    gcw = lax.dynamic_slice(vsum, (_CW_ROW, idx * ncw), (KW, ncw))
    two = lambda a: a.reshape(1, -1) if a.ndim == 1 else a.reshape(-1, a.shape[-1])
    small_in = {
        "ada_b": (ada_b, m_ada_b, v_ada_b), "ada_f_b": (ada_f_b, m_ada_f_b, v_ada_f_b),
        "norm1_g": (norm1_g, m_norm1_g, v_norm1_g), "b_in": (b_in, m_b_in, v_b_in),
        "a_ln_g": (a_ln_g, m_a_ln_g, v_a_ln_g), "a_ln_b": (a_ln_b, m_a_ln_b, v_a_ln_b),
        "a_spatial_b": (a_spatial_b.reshape(1, D), m_a_spatial_b.reshape(1, D), v_a_spatial_b.reshape(1, D)),
        "b_conv_b": (b_conv_b, m_b_conv_b, v_b_conv_b), "b_gn_g": (b_gn_g, m_b_gn_g, v_b_gn_g),
        "b_gn_b": (b_gn_b, m_b_gn_b, v_b_gn_b), "out_norm_a_g": (out_norm_a_g, m_out_norm_a_g, v_out_norm_a_g),
        "out_norm_b_g": (out_norm_b_g, m_out_norm_b_g, v_out_norm_b_g),
        "norm2_g": (norm2_g, m_norm2_g, v_norm2_g), "norm_f_g": (norm_f_g, m_norm_f_g, v_norm_f_g),
        "a_spatial_w": (a_spatial_w, m_a_spatial_w, v_a_spatial_w),
        "b_conv_w": (b_conv_w[0], m_b_conv_w[0], v_b_conv_w[0]),
    }
    small_in = {n: tuple(two(a) for a in t) for n, t in small_in.items()}
    res.update(_adam_small(vsum, wssum, gcw, small_in))

    loss = lax.psum(0.5 / D * jnp.sum(acc_f[7]), ("x", "y", "c"))
    shapes = {"ada_w": ada_w, "ada_b": ada_b, "norm1_g": norm1_g, "w_in": w_in, "b_in": b_in, "a_ln_g": a_ln_g,
              "a_ln_b": a_ln_b, "a_spatial_w": a_spatial_w, "a_spatial_b": a_spatial_b, "b_conv_w": b_conv_w,
              "b_conv_b": b_conv_b, "b_gn_g": b_gn_g, "b_gn_b": b_gn_b, "out_norm_a_g": out_norm_a_g,
              "out_norm_b_g": out_norm_b_g, "w_out": w_out, "norm2_g": norm2_g, "w_ffn_in": w_ffn_in,
              "w_ffn_out": w_ffn_out, "ada_f_w": ada_f_w, "ada_f_b": ada_f_b, "norm_f_g": norm_f_g}
    order = list(shapes)
    outs = [loss, gx.reshape(x.shape)]
    for which in range(4):
        outs += [res[n][which].reshape(shapes[n].shape) for n in order]
    return tuple(outs)
```

```python
import math

import jax
import jax.numpy as jnp
from jax import lax
from jax.experimental import pallas as pl
from jax.experimental.pallas import tpu as pltpu

F32 = jnp.float32
BF16 = jnp.bfloat16

D = 1024
DA = 512
DB = 512
DIN = 2048
DFF = 2816
NH = 8
CH = 128
KW = 31
HALO = 32
NDEV = 8
WIN_B = DIN // NDEV
WFI_B = 2 * DFF // NDEV
NFB = DFF // WFI_B
EPS = 1e-6
NVEC = 56
VMEM_LIMIT = 56 * 1024 * 1024

ADAM_LR, ADAM_B1, ADAM_B2, ADAM_EPS, ADAM_WD, ADAM_STEP = 0.001, 0.9, 0.999, 1e-08, 0.01, 10

MESH = pl.DeviceIdType.MESH


def _dot(a, b):
    return jnp.dot(a, b, preferred_element_type=F32)


def _dot_nt(a, b):
    return lax.dot_general(a, b, (((1,), (1,)), ((), ())), preferred_element_type=F32)


def _dot_tn(a, b):
    return lax.dot_general(a, b, (((0,), (0,)), ((), ())), preferred_element_type=F32)


def _rs(v):
    return lax.rsqrt(jnp.mean(v * v, axis=-1, keepdims=True) + EPS)


def _sig(v):
    return 1.0 / (1.0 + jnp.exp(-v))


_INV_SQRT2 = 1.0 / math.sqrt(2.0)
_INV_SQRT2PI = 1.0 / math.sqrt(2.0 * math.pi)


def _gelu_parts(v):
    cdf = 0.5 * (1.0 + lax.erf(v * _INV_SQRT2))
    pdf = jnp.exp(-0.5 * v * v) * _INV_SQRT2PI
    return v * cdf, cdf + v * pdf


def _grp_mean(v, pm):
    hi = v.astype(BF16)
    lo = (v - hi.astype(F32)).astype(BF16)
    return _dot(hi, pm) + _dot(lo, pm)


def _colsum(v):
    return jnp.sum(v, axis=0, keepdims=True)


def _full(shape):
    nd = len(shape)
    return pl.BlockSpec(shape, lambda *_: (0,) * nd)


def _resident(shape):
    nd = len(shape)
    return pl.BlockSpec(shape, lambda *_: (0,) * nd, pipeline_mode=pl.Buffered(1))


HBM = pl.BlockSpec(memory_space=pl.ANY)
VM = pl.BlockSpec(memory_space=pltpu.VMEM)


def _head_of_lane(rows):
    return lax.broadcasted_iota(jnp.int32, (rows, DA), 1) >> 6


def _block_pick(r, lane_head):
    out = jnp.zeros((CH, DA), F32)
    for h in range(NH):
        out = jnp.where(lane_head == h, r[h * CH:(h + 1) * CH, :], out)
    return out


def _place():
    x, y, c = lax.axis_index("x"), lax.axis_index("y"), lax.axis_index("c")
    return x, y, c, 4 * x + 2 * y + c


def _dev(t):
    return (t >> 2, (t >> 1) & 1, t & 1)


class _AllGather:
    def __init__(self, w_in, w_out, wss, wrs, lsem):
        x, y, c, idx = _place()
        me, sibling = (x, y, c), (x, y, 1 - c)
        chips = [(1 - x, y), (x, 1 - y), (1 - x, 1 - y)]
        nw = len(w_in)

        def blk(p):
            return 4 * p[0] + 2 * p[1] + p[2]

        def wcopy(a, k, block, to, src=None):
            dst = w_out[a].at[blk(block)]
            return pltpu.make_async_remote_copy(src_ref=dst if src is None else src, dst_ref=dst,
                                                send_sem=wss.at[a, k], recv_sem=wrs.at[a, k],
                                                device_id=to, device_id_type=MESH)

        self.mine = [pltpu.make_async_copy(w_in[a], w_out[a].at[idx], lsem.at[a]) for a in range(nw)]
        self.first = []
        for a in range(nw):
            self.first.append(wcopy(a, 0, me, sibling, src=w_in[a]))
            self.first += [wcopy(a, 1 + j, me, (*chip, c), src=w_in[a]) for j, chip in enumerate(chips)]
        self.landed = [[wcopy(a, 1 + j, (*chip, c), me) for a in range(nw)] for j, chip in enumerate(chips)]
        self.passed = [[wcopy(a, 4 + j, (*chip, c), sibling) for a in range(nw)] for j, chip in enumerate(chips)]
        self.from_sibling = []
        for a in range(nw):
            self.from_sibling.append(wcopy(a, 0, sibling, me))
            self.from_sibling += [wcopy(a, 4 + j, (*chip, 1 - c), me) for j, chip in enumerate(chips)]

    def start(self):
        for cp in self.mine + self.first:
            cp.start()

    def forward(self):
        for land, pas in zip(self.landed, self.passed):
            for l, p in zip(land, pas):
                l.wait_recv()
                p.start()

    def finish(self):
        for cp in self.from_sibling:
            cp.wait_recv()
        for cp in self.first:
            cp.wait_send()
        for pas in self.passed:
            for p in pas:
                p.wait_send()
        for cp in self.mine:
            cp.wait()


AG_SEMS = lambda nw: [pltpu.SemaphoreType.DMA((nw, 7)), pltpu.SemaphoreType.DMA((nw, 7)),
                      pltpu.SemaphoreType.DMA((nw,))]


class _ReduceScatter:
    def __init__(self, g_in, r_out, gss, grs, lsem):
        x, y, c, idx = _place()
        me = (x, y, c)
        nw = len(g_in)
        self.mine = [pltpu.make_async_copy(g_in[a].at[idx], r_out[a].at[0], lsem.at[a]) for a in range(nw)]
        self.sends, self.recvs = [], []
        for k in range(1, NDEV):
            t = idx ^ k
            for a in range(nw):
                self.sends.append(pltpu.make_async_remote_copy(
                    src_ref=g_in[a].at[t], dst_ref=r_out[a].at[k], send_sem=gss.at[a, k - 1],
                    recv_sem=grs.at[a, k - 1], device_id=_dev(t), device_id_type=MESH))
                self.recvs.append(pltpu.make_async_remote_copy(
                    src_ref=g_in[a].at[0], dst_ref=r_out[a].at[k], send_sem=gss.at[a, k - 1],
                    recv_sem=grs.at[a, k - 1], device_id=me, device_id_type=MESH))

    def start(self):
        for cp in self.mine + self.sends:
            cp.start()

    def finish(self):
        for cp in self.recvs:
            cp.wait_recv()
        for cp in self.sends:
            cp.wait_send()
        for cp in self.mine:
            cp.wait()


RS_SEMS = AG_SEMS


def _mix_fwd(x, mod, g1, win, b_in, lng, lnb, wcat, bsf, cw, cb, gng, gnb, oga, ogb, wout, pm, ffn_shards, tm):
    T = x.shape[0]
    nt = T // tm
    nch = tm // CH
    nw = len(ffn_shards)
    fwd_step = (5 * nt) // 8

    def body(x_ref, mod_ref, g1_ref, win_ref, bin_ref, lng_ref, lnb_ref, wcat_ref, bsf_ref, cw_ref, cb_ref,
             gng_ref, gnb_ref, oga_ref, ogb_ref, wout_ref, pm_ref, *rest):
        sh_in = rest[:nw]
        x1_ref, h_ref, z_ref, mixed_ref, yc_ref, y_ref, o_ref = rest[nw:nw + 7]
        sh_out = rest[nw + 7:2 * nw + 7]
        glbuf, wss, wrs, lsem = rest[2 * nw + 7:]
        i = pl.program_id(0)

        @pl.when(i == 0)
        def _():
            _AllGather(sh_in, sh_out, wss, wrs, lsem).start()

        xv = x_ref[...]
        shift1 = mod_ref[0:1, :]
        scale1 = mod_ref[1:2, :]
        gate1 = mod_ref[2:3, :]
        h = (xv * _rs(xv) * g1_ref[...]) * (1.0 + scale1) + shift1
        hb = h.astype(BF16)
        h_ref[...] = hb
        z = jnp.concatenate([_dot(hb, win_ref[j]) for j in range(NDEV)], axis=1) + bin_ref[...]
        z_ref[...] = z
        gu, _ = _gelu_parts(z[:, 0:DA])
        gv, _ = _gelu_parts(z[:, DA:2 * DA])
        xc = gv - jnp.mean(gv, axis=-1, keepdims=True)
        vn = xc * lax.rsqrt(jnp.mean(xc * xc, axis=-1, keepdims=True) + EPS) * lng_ref[...] + lnb_ref[...]
        vnb = vn.astype(BF16)
        lane_head = _head_of_lane(CH)
        chunks = []
        for ci in range(nch):
            r = _dot(wcat_ref[...], vnb[ci * CH:(ci + 1) * CH, :])
            chunks.append(_block_pick(r, lane_head) + bsf_ref[...])
        mixed = jnp.concatenate(chunks, axis=0) if nch > 1 else chunks[0]
        mixed_ref[...] = mixed
        ya = gu * mixed
        gl = z[:, 2 * DA:2 * DA + DB] * _sig(z[:, 2 * DA + DB:])

        @pl.when(i == 0)
        def _():
            glbuf[0:HALO, :] = jnp.zeros((HALO, DB), F32)

        glbuf[HALO:HALO + tm, :] = gl
        yc = jnp.zeros((tm, DB), F32) + cb_ref[...]
        for k in range(KW):
            off = HALO - (KW - 1) + k
            yc = yc + cw_ref[k:k + 1, :] * glbuf[off:off + tm, :]
        glbuf[0:HALO, :] = gl[tm - HALO:, :]
        yc_ref[...] = yc
        pmv = pm_ref[...]
        dc = yc - _grp_mean(yc, pmv)
        yg = dc * lax.rsqrt(_grp_mean(dc * dc, pmv) + EPS) * gng_ref[...] + gnb_ref[...]
        yb = yg * _sig(yg)
        na = ya * _rs(ya) * oga_ref[...]
        nb = yb * _rs(yb) * ogb_ref[...]
        yv = jnp.concatenate([na, nb], axis=1).astype(BF16)
        y_ref[...] = yv
        o = _dot(yv, wout_ref[...])
        o_ref[...] = o
        x1_ref[...] = xv + gate1 * o

        @pl.when(i == fwd_step)
        def _():
            _AllGather(sh_in, sh_out, wss, wrs, lsem).forward()

        @pl.when(i == nt - 1)
        def _():
            _AllGather(sh_in, sh_out, wss, wrs, lsem).finish()

    tile = lambda w: pl.BlockSpec((tm, w), lambda i: (i, 0))
    outs = pl.pallas_call(
        body,
        name="mix_fwd",
        grid=(nt,),
        in_specs=[tile(D), _full((8, D)), _full((1, D)), _resident((NDEV, D, WIN_B)), _full((1, DIN)),
                  _full((1, DA)), _full((1, DA)), _full((NH * CH, CH)), _full((CH, DA)), _full((HALO, DB)),
                  _full((1, DB)), _full((1, DB)), _full((1, DB)), _full((1, DA)), _full((1, DB)),
                  _resident((D, D)), _full((DB, DB))] + [HBM] * nw,
        out_specs=[tile(D), tile(D), tile(DIN), tile(DA), tile(DB), tile(D), tile(D)] + [HBM] * nw,
        out_shape=[jax.ShapeDtypeStruct((T, D), F32), jax.ShapeDtypeStruct((T, D), BF16),
                   jax.ShapeDtypeStruct((T, DIN), F32), jax.ShapeDtypeStruct((T, DA), F32),
                   jax.ShapeDtypeStruct((T, DB), F32), jax.ShapeDtypeStruct((T, D), BF16),
                   jax.ShapeDtypeStruct((T, D), F32)]
                  + [jax.ShapeDtypeStruct((NDEV,) + s.shape, s.dtype) for s in ffn_shards],
        scratch_shapes=[pltpu.VMEM((HALO + tm, DB), F32)] + AG_SEMS(nw),
        compiler_params=pltpu.CompilerParams(dimension_semantics=("arbitrary",), vmem_limit_bytes=VMEM_LIMIT),
    )(x, mod, g1, win, b_in, lng, lnb, wcat, bsf, cw, cb, gng, gnb, oga, ogb, wout, pm, *ffn_shards)
    return outs[:7], outs[7:]


def _ffn(x1, tgt, mod, g2, gf, wfi, wfo, tm):
    T = x1.shape[0]
    nt = T // tm

    def body(x1_ref, tgt_ref, mod_ref, g2_ref, gf_ref, wfi_ref, wfo_ref,
             dx1_ref, h2_ref, dgu_ref, act_ref, dxg_ref, acc_ref, g_s, u_s):
        i = pl.program_id(0)

        @pl.when(i == 0)
        def _():
            acc_ref[...] = jnp.zeros((8, D), F32)

        x1 = x1_ref[...]
        shift2 = mod_ref[3:4, :]
        scale2 = mod_ref[4:5, :]
        gate2 = mod_ref[5:6, :]
        shiftf = mod_ref[6:7, :]
        scalef = mod_ref[7:8, :]
        g2v = g2_ref[...]
        gfv = gf_ref[...]
        r2 = _rs(x1)
        xn2 = x1 * r2
        h2b = (xn2 * g2v * (1.0 + scale2) + shift2).astype(BF16)
        h2_ref[...] = h2b
        f = jnp.zeros((tm, D), F32)
        for j in range(NFB):
            g = _dot(h2b, wfi_ref[j])
            u = _dot(h2b, wfi_ref[NFB + j])
            g_s[j] = g
            u_s[j] = u
            actb = (g * _sig(g) * u).astype(BF16)
            act_ref[j] = actb
            f = f + _dot(actb, wfo_ref[j * WFI_B:(j + 1) * WFI_B, :])
        x2 = x1 + gate2 * f
        rf = _rs(x2)
        xnf = x2 * rf
        out = xnf * gfv * (1.0 + scalef) + shiftf
        e = out - tgt_ref[...]
        dout = e * (1.0 / D)
        acc_ref[7:8, :] += _colsum(e * e)
        acc_ref[0:1, :] += _colsum(dout)
        acc_ref[1:2, :] += _colsum(dout * xnf * gfv)
        acc_ref[2:3, :] += _colsum(dout * (1.0 + scalef) * xnf)
        dxnf = dout * (1.0 + scalef) * gfv
        dx2 = rf * (dxnf - xnf * jnp.mean(dxnf * xnf, axis=-1, keepdims=True))
        acc_ref[3:4, :] += _colsum(dx2 * f)
        dxgb = (dx2 * gate2).astype(BF16)
        dxg_ref[...] = dxgb
        dh2 = jnp.zeros((tm, D), F32)
        for j in range(NFB):
            dact = _dot_nt(dxgb, wfo_ref[j * WFI_B:(j + 1) * WFI_B, :])
            g = g_s[j]
            u = u_s[j]
            s = _sig(g)
            dgb = (dact * u * (s * (1.0 + g * (1.0 - s)))).astype(BF16)
            dub = (dact * (g * s)).astype(BF16)
            dgu_ref[j] = dgb
            dgu_ref[NFB + j] = dub
            dh2 = dh2 + _dot_nt(dgb, wfi_ref[j])
            dh2 = dh2 + _dot_nt(dub, wfi_ref[NFB + j])
        acc_ref[4:5, :] += _colsum(dh2)
        acc_ref[5:6, :] += _colsum(dh2 * xn2 * g2v)
        acc_ref[6:7, :] += _colsum(dh2 * (1.0 + scale2) * xn2)
        dxn2 = dh2 * (1.0 + scale2) * g2v
        dx1_ref[...] = dx2 + r2 * (dxn2 - xn2 * jnp.mean(dxn2 * xn2, axis=-1, keepdims=True))

    tile = lambda w: pl.BlockSpec((tm, w), lambda i: (i, 0))
    blocked = lambda n: pl.BlockSpec((n, tm, WFI_B), lambda i: (0, i, 0))
    return pl.pallas_call(
        body,
        name="ffn_fwd_bwd",
        grid=(nt,),
        in_specs=[tile(D), tile(D), _full((8, D)), _full((1, D)), _full((1, D)),
                  _resident((NDEV, D, WFI_B)), _resident((DFF, D))],
        out_specs=[tile(D), tile(D), blocked(NDEV), blocked(NFB), tile(D), _full((8, D))],
        out_shape=[jax.ShapeDtypeStruct((T, D), F32), jax.ShapeDtypeStruct((T, D), BF16),
                   jax.ShapeDtypeStruct((NDEV, T, WFI_B), BF16), jax.ShapeDtypeStruct((NFB, T, WFI_B), BF16),
                   jax.ShapeDtypeStruct((T, D), BF16), jax.ShapeDtypeStruct((8, D), F32)],
        scratch_shapes=[pltpu.VMEM((NFB, tm, WFI_B), F32), pltpu.VMEM((NFB, tm, WFI_B), F32)],
        compiler_params=pltpu.CompilerParams(dimension_semantics=("arbitrary",), vmem_limit_bytes=VMEM_LIMIT),
    )(x1, tgt, mod, g2, gf, wfi, wfo)


def _mix_bwd(dx1, x, z, mixed, yc, o, mod, g1, win, lng, lnb, wcat, wcat_t, cw, gng, gnb, oga, ogb, wout, pm,
             esel, ffn_grads, tm):
    T = x.shape[0]
    nt = T // tm
    nch = tm // CH
    nw = len(ffn_grads)

    def body(dx1_ref, x_ref, z_ref, mixed_ref, yc_ref, o_ref, mod_ref, g1_ref, win_ref, lng_ref, lnb_ref,
             wcat_ref, wcatt_ref, cw_ref, gng_ref, gnb_ref, oga_ref, ogb_ref, wout_ref, pm_ref, esel_ref, *rest):
        g_in = rest[:nw]
        gx_ref, dz_ref, dog_ref, accv_ref, accb_ref, acca_ref, accbs_ref, accws_ref, acccw_ref = rest[nw:nw + 9]
        r_out = rest[nw + 9:2 * nw + 9]
        dycbuf, bs_s, gss, grs, lsem = rest[2 * nw + 9:]
        i = pl.program_id(0)

        @pl.when(i == 0)
        def _():
            _ReduceScatter(g_in, r_out, gss, grs, lsem).start()
            accv_ref[...] = jnp.zeros((8, D), F32)
            accb_ref[...] = jnp.zeros((1, DIN), F32)
            acca_ref[...] = jnp.zeros((8, DA), F32)
            accws_ref[...] = jnp.zeros((NH * CH, CH), F32)
            acccw_ref[...] = jnp.zeros((HALO, DB), F32)
            bs_s[...] = jnp.zeros((CH, DA), F32)
            dycbuf[tm:tm + HALO, :] = jnp.zeros((HALO, DB), F32)

        shift1 = mod_ref[0:1, :]
        scale1 = mod_ref[1:2, :]
        gate1 = mod_ref[2:3, :]
        g1v = g1_ref[...]
        xv = x_ref[...]
        r1 = _rs(xv)
        xn1 = xv * r1
        z = z_ref[...]
        u = z[:, 0:DA]
        v = z[:, DA:2 * DA]
        val = z[:, 2 * DA:2 * DA + DB]
        gate = z[:, 2 * DA + DB:]
        gu, dgelu_u = _gelu_parts(u)
        gv, dgelu_v = _gelu_parts(v)
        xc = gv - jnp.mean(gv, axis=-1, keepdims=True)
        rsl = lax.rsqrt(jnp.mean(xc * xc, axis=-1, keepdims=True) + EPS)
        vhat = xc * rsl
        lngv = lng_ref[...]
        vnb = (vhat * lngv + lnb_ref[...]).astype(BF16)
        mixed = mixed_ref[...]
        ya = gu * mixed
        ra = _rs(ya)
        yan = ya * ra
        sgt = _sig(gate)
        gl = val * sgt
        pmv = pm_ref[...]
        ycv = yc_ref[...]
        dcen = ycv - _grp_mean(ycv, pmv)
        rsg = lax.rsqrt(_grp_mean(dcen * dcen, pmv) + EPS)
        yhat = dcen * rsg
        gngv = gng_ref[...]
        yg = yhat * gngv + gnb_ref[...]
        sgy = _sig(yg)
        yb = yg * sgy
        rb = _rs(yb)
        ybn = yb * rb
        dx1 = dx1_ref[...]
        accv_ref[0:1, :] += _colsum(dx1 * o_ref[...])
        dogb = (dx1 * gate1).astype(BF16)
        dog_ref[...] = dogb
        dy = _dot_nt(dogb, wout_ref[...])
        dna = dy[:, 0:DA]
        dnb = dy[:, DA:]
        ogav = oga_ref[...]
        ogbv = ogb_ref[...]
        acca_ref[2:3, :] += _colsum(dna * yan)
        acca_ref[3:4, :] += _colsum(dnb * ybn)
        ta = dna * ogav
        dya = ra * (ta - yan * jnp.mean(ta * yan, axis=-1, keepdims=True))
        tb = dnb * ogbv
        dyb = rb * (tb - ybn * jnp.mean(tb * ybn, axis=-1, keepdims=True))
        dgu = dya * mixed
        dm = dya * gu
        lane_head = _head_of_lane(CH)
        dvn_chunks = []
        bs_acc = bs_s[...]
        for ci in range(nch):
            dmc = dm[ci * CH:(ci + 1) * CH, :]
            bs_acc = bs_acc + dmc
            dmcb = dmc.astype(BF16)
            dvn_chunks.append(_block_pick(_dot(wcatt_ref[...], dmcb), lane_head))
            zero = jnp.zeros((CH, DA), BF16)
            stack = jnp.concatenate([jnp.where(lane_head == h, dmcb, zero) for h in range(NH)], axis=0)
            accws_ref[...] += _dot_nt(stack, vnb[ci * CH:(ci + 1) * CH, :])
        bs_s[...] = bs_acc
        dvn = jnp.concatenate(dvn_chunks, axis=0) if nch > 1 else dvn_chunks[0]
        acca_ref[0:1, :] += _colsum(dvn * vhat)
        acca_ref[1:2, :] += _colsum(dvn)
        dvh = dvn * lngv
        dgv = rsl * (dvh - jnp.mean(dvh, axis=-1, keepdims=True)
                     - vhat * jnp.mean(dvh * vhat, axis=-1, keepdims=True))
        du = dgu * dgelu_u
        dv = dgv * dgelu_v
        dyg = dyb * (sgy * (1.0 + yg * (1.0 - sgy)))
        acca_ref[5:6, :] += _colsum(dyg * yhat)
        acca_ref[6:7, :] += _colsum(dyg)
        dyh = dyg * gngv
        dyc = rsg * (dyh - _grp_mean(dyh, pmv) - yhat * _grp_mean(dyh * yhat, pmv))
        acca_ref[4:5, :] += _colsum(dyc)
        dycbuf[0:tm, :] = dyc
        dgl = jnp.zeros((tm, DB), F32)
        for k in range(KW):
            off = KW - 1 - k
            win_k = dycbuf[off:off + tm, :]
            dgl = dgl + cw_ref[k:k + 1, :] * win_k
            acccw_ref[k:k + 1, :] += _colsum(win_k * gl)
        dycbuf[tm:tm + HALO, :] = dyc[0:HALO, :]
        dval = dgl * sgt
        dgate = dgl * val * sgt * (1.0 - sgt)
        dz = jnp.concatenate([du, dv, dval, dgate], axis=1)
        accb_ref[...] += _colsum(dz)
        dzb = dz.astype(BF16)
        dz_ref[...] = dzb
        dh = _dot_nt(dzb[:, 0:WIN_B], win_ref[0])
        for j in range(1, NDEV):
            dh = dh + _dot_nt(dzb[:, j * WIN_B:(j + 1) * WIN_B], win_ref[j])
        accv_ref[1:2, :] += _colsum(dh)
        accv_ref[2:3, :] += _colsum(dh * xn1 * g1v)
        accv_ref[3:4, :] += _colsum(dh * (1.0 + scale1) * xn1)
        dxn1 = dh * (1.0 + scale1) * g1v
        gx_ref[...] = dx1 + r1 * (dxn1 - xn1 * jnp.mean(dxn1 * xn1, axis=-1, keepdims=True))

        @pl.when(i == nt - 1)
        def _():
            rows = lax.broadcasted_iota(jnp.int32, (NH * CH, CH), 0) & (CH - 1)
            cols = lax.broadcasted_iota(jnp.int32, (NH * CH, CH), 1)
            accws_ref[...] = jnp.where(cols <= rows, accws_ref[...], 0.0)
            bs = bs_s[...]
            hi = bs.astype(BF16)
            r1_ = bs - hi.astype(F32)
            mid = r1_.astype(BF16)
            lo = (r1_ - mid.astype(F32)).astype(BF16)
            ev = esel_ref[...]
            accbs_ref[...] = _dot(hi, ev) + _dot(mid, ev) + _dot(lo, ev)
            _ReduceScatter(g_in, r_out, gss, grs, lsem).finish()

    rev = lambda w: pl.BlockSpec((tm, w), lambda i: (nt - 1 - i, 0))
    outs = pl.pallas_call(
        body,
        name="mix_bwd",
        grid=(nt,),
        in_specs=[rev(D), rev(D), rev(DIN), rev(DA), rev(DB), rev(D), _full((8, D)), _full((1, D)),
                  _resident((NDEV, D, WIN_B)), _full((1, DA)), _full((1, DA)), _full((NH * CH, CH)),
                  _full((NH * CH, CH)), _full((HALO, DB)), _full((1, DB)), _full((1, DB)), _full((1, DA)),
                  _full((1, DB)), _resident((D, D)), _full((DB, DB)), _full((DA, CH))] + [HBM] * nw,
        out_specs=[rev(D), rev(DIN), rev(D), _full((8, D)), _full((1, DIN)), _full((8, DA)), _full((CH, CH)),
                   _full((NH * CH, CH)), _full((HALO, DB))] + [HBM] * nw,
        out_shape=[jax.ShapeDtypeStruct((T, D), F32), jax.ShapeDtypeStruct((T, DIN), BF16),
                   jax.ShapeDtypeStruct((T, D), BF16), jax.ShapeDtypeStruct((8, D), F32),
                   jax.ShapeDtypeStruct((1, DIN), F32), jax.ShapeDtypeStruct((8, DA), F32),
                   jax.ShapeDtypeStruct((CH, CH), F32), jax.ShapeDtypeStruct((NH * CH, CH), F32),
                   jax.ShapeDtypeStruct((HALO, DB), F32)]
                  + [jax.ShapeDtypeStruct(g.shape, g.dtype) for g in ffn_grads],
        scratch_shapes=[pltpu.VMEM((tm + HALO, DB), F32), pltpu.VMEM((CH, DA), F32)] + RS_SEMS(nw),
        compiler_params=pltpu.CompilerParams(dimension_semantics=("arbitrary",), vmem_limit_bytes=VMEM_LIMIT),
    )(dx1, x, z, mixed, yc, o, mod, g1, win, lng, lnb, wcat, wcat_t, cw, gng, gnb, oga, ogb, wout, pm, esel,
      *ffn_grads)
    return outs[:9], outs[9:]


def _wgrad_cols(a, b, nblk, tk, name):
    T, M = a.shape
    bw = b.shape[1] // nblk
    nk = T // tk

    def body(a_ref, b_ref, o_ref, acc):
        k = pl.program_id(0)

        @pl.when(k == 0)
        def _():
            acc[...] = jnp.zeros((nblk, M, bw), F32)

        av = a_ref[...]
        for j in range(nblk):
            acc[j] += _dot_tn(av, b_ref[:, j * bw:(j + 1) * bw])

        @pl.when(k == nk - 1)
        def _():
            o_ref[...] = acc[...].astype(BF16)

    return pl.pallas_call(
        body, name=name, grid=(nk,),
        in_specs=[pl.BlockSpec((tk, M), lambda k: (k, 0)), pl.BlockSpec((tk, nblk * bw), lambda k: (k, 0))],
        out_specs=_full((nblk, M, bw)),
        out_shape=jax.ShapeDtypeStruct((nblk, M, bw), BF16),
        scratch_shapes=[pltpu.VMEM((nblk, M, bw), F32)],
        compiler_params=pltpu.CompilerParams(dimension_semantics=("arbitrary",), vmem_limit_bytes=VMEM_LIMIT),
    )(a, b)


def _wgrad_b_blocked(a, b3, per, tk, name):
    T, M = a.shape
    nb, _, bw = b3.shape
    nk = T // tk

    def body(a_ref, b_ref, o_ref, acc):
        k = pl.program_id(1)

        @pl.when(k == 0)
        def _():
            acc[...] = jnp.zeros((per, M, bw), F32)

        av = a_ref[...]
        for j in range(per):
            acc[j] += _dot_tn(av, b_ref[j])

        @pl.when(k == nk - 1)
        def _():
            o_ref[...] = acc[...].astype(BF16)

    return pl.pallas_call(
        body, name=name, grid=(nb // per, nk),
        in_specs=[pl.BlockSpec((tk, M), lambda j, k: (k, 0)), pl.BlockSpec((per, tk, bw), lambda j, k: (j, k, 0))],
        out_specs=pl.BlockSpec((per, M, bw), lambda j, k: (j, 0, 0)),
        out_shape=jax.ShapeDtypeStruct((nb, M, bw), BF16),
        scratch_shapes=[pltpu.VMEM((per, M, bw), F32)],
        compiler_params=pltpu.CompilerParams(dimension_semantics=("arbitrary", "arbitrary"),
                                             vmem_limit_bytes=VMEM_LIMIT),
    )(a, b3)


def _wgrad_a_blocked(a3, b, tk, name):
    nb, T, bw = a3.shape
    N = b.shape[1]
    nk = T // tk

    def body(a_ref, b_ref, o_ref, acc):
        k = pl.program_id(1)

        @pl.when(k == 0)
        def _():
            acc[...] = jnp.zeros((bw, N), F32)

        acc[...] += _dot_tn(a_ref[0], b_ref[...])

        @pl.when(k == nk - 1)
        def _():
            o_ref[0] = acc[...].astype(BF16)

    return pl.pallas_call(
        body, name=name, grid=(nb, nk),
        in_specs=[pl.BlockSpec((1, tk, bw), lambda j, k: (j, k, 0)), pl.BlockSpec((tk, N), lambda j, k: (k, 0))],
        out_specs=pl.BlockSpec((1, bw, N), lambda j, k: (j, 0, 0)),
        out_shape=jax.ShapeDtypeStruct((nb, bw, N), BF16),
        scratch_shapes=[pltpu.VMEM((bw, N), F32)],
        compiler_params=pltpu.CompilerParams(dimension_semantics=("arbitrary", "arbitrary"),
                                             vmem_limit_bytes=VMEM_LIMIT),
    )(a3, b)


def _small_copy(src, dst, ss, rs, k, to):
    return pltpu.make_async_remote_copy(src_ref=src, dst_ref=dst, send_sem=ss.at[k], recv_sem=rs.at[k],
                                        device_id=to, device_id_type=MESH)


def _gather(c_row, ada_w, ada_b8, ada_f_w, ada_f_b8, conv_s, shards):
    nw = len(shards)

    def body(c_ref, adaw_ref, adab_ref, adafw_ref, adafb_ref, conv_ref, *rest):
        w_in = rest[:nw]
        call_ref, cparts_ref, cfparts_ref, convg_ref = rest[nw:nw + 4]
        w_out = rest[nw + 4:2 * nw + 4]
        part_s, partf_s, wss, wrs, lsem, s1, r1, s2, r2, s3, r3, s4, r4 = rest[2 * nw + 4:]
        x, y, c, idx = _place()
        me = (x, y, c)
        ag = _AllGather(w_in, w_out, wss, wrs, lsem)
        ag.start()
        call_ref[pl.ds(idx, 1), :] = c_ref[...]
        convg_ref[idx] = conv_ref[...]
        ph1 = []
        for k in range(1, NDEV):
            to = _dev(idx ^ k)
            ph1.append(_small_copy(c_ref, call_ref.at[pl.ds(idx, 1)], s1, r1, k - 1, to))
            ph1.append(_small_copy(conv_ref, convg_ref.at[idx], s2, r2, k - 1, to))
        for cp in ph1:
            cp.start()
        for k in range(1, NDEV):
            src_dev = idx ^ k
            _small_copy(c_ref, call_ref.at[pl.ds(src_dev, 1)], s1, r1, k - 1, me).wait_recv()
            _small_copy(conv_ref, convg_ref.at[src_dev], s2, r2, k - 1, me).wait_recv()
        call = call_ref[...]
        cact = (call * _sig(call))
        part_s[...] = jnp.dot(cact, adaw_ref[...], preferred_element_type=F32,
                              precision=lax.Precision.HIGHEST) + adab_ref[pl.ds(idx, 1), :]
        partf_s[...] = jnp.dot(cact, adafw_ref[...], preferred_element_type=F32,
                               precision=lax.Precision.HIGHEST) + adafb_ref[pl.ds(idx, 1), :]
        cparts_ref[pl.ds(idx, 1), :] = part_s[pl.ds(idx, 1), :]
        cfparts_ref[pl.ds(idx, 1), :] = partf_s[pl.ds(idx, 1), :]
        ph2 = []
        for k in range(1, NDEV):
            t = idx ^ k
            ph2.append(_small_copy(part_s.at[pl.ds(t, 1)], cparts_ref.at[pl.ds(idx, 1)], s3, r3, k - 1, _dev(t)))
            ph2.append(_small_copy(partf_s.at[pl.ds(t, 1)], cfparts_ref.at[pl.ds(idx, 1)], s4, r4, k - 1, _dev(t)))
        for cp in ph2:
            cp.start()
        for k in range(1, NDEV):
            src_dev = idx ^ k
            _small_copy(part_s.at[pl.ds(0, 1)], cparts_ref.at[pl.ds(src_dev, 1)], s3, r3, k - 1, me).wait_recv()
            _small_copy(partf_s.at[pl.ds(0, 1)], cfparts_ref.at[pl.ds(src_dev, 1)], s4, r4, k - 1, me).wait_recv()
        for cp in ph1 + ph2:
            cp.wait_send()
        ag.forward()
        ag.finish()

    dma7 = pltpu.SemaphoreType.DMA((NDEV - 1,))
    outs = pl.pallas_call(
        body,
        name="gather_weights",
        in_specs=[VM] * 6 + [HBM] * nw,
        out_specs=[VM] * 4 + [HBM] * nw,
        out_shape=[jax.ShapeDtypeStruct((NDEV, D), F32), jax.ShapeDtypeStruct((NDEV, ada_w.shape[1]), F32),
                   jax.ShapeDtypeStruct((NDEV, ada_f_w.shape[1]), F32),
                   jax.ShapeDtypeStruct((NDEV,) + conv_s.shape, F32)]
                  + [jax.ShapeDtypeStruct((NDEV,) + s.shape, s.dtype) for s in shards],
        scratch_shapes=[pltpu.VMEM((NDEV, ada_w.shape[1]), F32), pltpu.VMEM((NDEV, ada_f_w.shape[1]), F32)]
                       + AG_SEMS(nw) + [dma7] * 8,
        compiler_params=pltpu.CompilerParams(vmem_limit_bytes=VMEM_LIMIT),
    )(c_row, ada_w, ada_b8, ada_f_w, ada_f_b8, conv_s, *shards)
    return outs[0], outs[1], outs[2], outs[3], outs[4:]


_VEC_AT = {
    "norm1_g": (8, 0, D), "a_ln_g": (11, 0, DA), "a_ln_b": (11, DA, DA), "a_spatial_b": (12, 0, D),
    "b_conv_b": (13, 0, DB), "b_gn_g": (13, DB, DB), "b_gn_b": (14, 0, DB), "out_norm_a_g": (14, DB, DA),
    "out_norm_b_g": (15, 0, DB), "norm2_g": (16, 0, D), "norm_f_g": (17, 0, D),
}
_CW_ROW = 24


def _reduce(acc_f, acc_v, acc_b, acc_a, acc_bs, acc_cw, dws, grads):
    nw = len(grads)

    def body(accf_ref, accv_ref, accb_ref, acca_ref, accbs_ref, acccw_ref, dws_ref, *rest):
        g_in = rest[:nw]
        vsum_ref, dcond_ref, wssum_ref = rest[nw:nw + 3]
        r_out = rest[nw + 3:2 * nw + 3]
        vloc, vbuf, wbuf, gss, grs, lsem, s1, r1, s2, r2 = rest[2 * nw + 3:]
        x, y, c, idx = _place()
        me = (x, y, c)
        rs = _ReduceScatter(g_in, r_out, gss, grs, lsem)
        rs.start()
        vloc[...] = jnp.zeros((NVEC, D), F32)
        vloc[0:1, :] = accv_ref[1:2, :]
        vloc[1:2, :] = accv_ref[2:3, :]
        vloc[2:3, :] = accv_ref[0:1, :]
        vloc[3:4, :] = accf_ref[4:5, :]
        vloc[4:5, :] = accf_ref[5:6, :]
        vloc[5:6, :] = accf_ref[3:4, :]
        vloc[6:7, :] = accf_ref[0:1, :]
        vloc[7:8, :] = accf_ref[1:2, :]
        vloc[8:9, :] = accv_ref[3:4, :]
        vloc[9:10, :] = accb_ref[:, 0:D]
        vloc[10:11, :] = accb_ref[:, D:]
        vloc[11:12, 0:DA] = acca_ref[0:1, :]
        vloc[11:12, DA:] = acca_ref[1:2, :]
        bst = accbs_ref[...].T
        for h in range(NH):
            vloc[12:13, h * CH:(h + 1) * CH] = bst[h:h + 1, :]
        vloc[13:14, 0:DB] = acca_ref[4:5, :]
        vloc[13:14, DB:] = acca_ref[5:6, :]
        vloc[14:15, 0:DB] = acca_ref[6:7, :]
        vloc[14:15, DB:] = acca_ref[2:3, :]
        vloc[15:16, 0:DB] = acca_ref[3:4, :]
        vloc[16:17, :] = accf_ref[6:7, :]
        vloc[17:18, :] = accf_ref[2:3, :]
        vloc[_CW_ROW:_CW_ROW + HALO, 0:DB] = acccw_ref[...]
        vbuf[idx] = vloc[...]
        wbuf[idx] = dws_ref[...]
        sm = []
        for k in range(1, NDEV):
            to = _dev(idx ^ k)
            sm.append(_small_copy(vloc, vbuf.at[idx], s1, r1, k - 1, to))
            sm.append(_small_copy(dws_ref, wbuf.at[idx], s2, r2, k - 1, to))
        for cp in sm:
            cp.start()
        for k in range(1, NDEV):
            src_dev = idx ^ k
            _small_copy(vloc, vbuf.at[src_dev], s1, r1, k - 1, me).wait_recv()
            _small_copy(dws_ref, wbuf.at[src_dev], s2, r2, k - 1, me).wait_recv()
        vs = vbuf[0]
        ws = wbuf[0]
        for d in range(1, NDEV):
            vs = vs + vbuf[d]
            ws = ws + wbuf[d]
        vsum_ref[...] = vs
        wssum_ref[...] = ws
        for d in range(NDEV):
            dcond_ref[d] = vbuf[d, 0:8, :]
        for cp in sm:
            cp.wait_send()
        rs.finish()

    dma7 = pltpu.SemaphoreType.DMA((NDEV - 1,))
    outs = pl.pallas_call(
        body,
        name="reduce_grads",
        in_specs=[VM] * 7 + [HBM] * nw,
        out_specs=[VM, VM, VM] + [HBM] * nw,
        out_shape=[jax.ShapeDtypeStruct((NVEC, D), F32), jax.ShapeDtypeStruct((NDEV, 8, D), F32),
                   jax.ShapeDtypeStruct(dws.shape, F32)]
                  + [jax.ShapeDtypeStruct(g.shape, g.dtype) for g in grads],
        scratch_shapes=[pltpu.VMEM((NVEC, D), F32), pltpu.VMEM((NDEV, NVEC, D), F32),
                        pltpu.VMEM((NDEV,) + dws.shape, F32)] + RS_SEMS(nw) + [dma7] * 4,
        compiler_params=pltpu.CompilerParams(vmem_limit_bytes=VMEM_LIMIT),
    )(acc_f, acc_v, acc_b, acc_a, acc_bs, acc_cw, dws, *grads)
    return outs[0], outs[1], outs[2], outs[3:]


def _adamw(w, g, m, v):
    m2 = ADAM_B1 * m + (1.0 - ADAM_B1) * g
    v2 = ADAM_B2 * v + (1.0 - ADAM_B2) * (g * g)
    m_hat = m2 / (1.0 - ADAM_B1 ** ADAM_STEP)
    v_hat = v2 / (1.0 - ADAM_B2 ** ADAM_STEP)
    delta = -ADAM_LR * (m_hat / (jnp.sqrt(v_hat) + ADAM_EPS) + ADAM_WD * w)
    return delta, m2, v2


def _adam_big(r, w, m, v, rb, name):
    R, C = w.shape

    def body(r_ref, w_ref, m_ref, v_ref, g_ref, d_ref, m2_ref, v2_ref):
        g = r_ref[0].astype(F32)
        for k in range(1, NDEV):
            g = g + r_ref[k].astype(F32)
        g_ref[...] = g
        d_ref[...], m2_ref[...], v2_ref[...] = _adamw(w_ref[...], g, m_ref[...], v_ref[...])

    t2 = pl.BlockSpec((rb, C), lambda i: (i, 0))
    sd = jax.ShapeDtypeStruct((R, C), F32)
    return pl.pallas_call(
        body, name=name, grid=(R // rb,),
        in_specs=[pl.BlockSpec((NDEV, rb, C), lambda i: (0, i, 0)), t2, t2, t2],
        out_specs=[t2, t2, t2, t2], out_shape=[sd, sd, sd, sd],
        compiler_params=pltpu.CompilerParams(dimension_semantics=("arbitrary",), vmem_limit_bytes=VMEM_LIMIT),
    )(r, w, m, v)


def _adam_ada(cact_t, dcs, w, m, v, rb, name):
    R, C = w.shape

    def body(ct_ref, dc_ref, w_ref, m_ref, v_ref, g_ref, d_ref, m2_ref, v2_ref):
        g = jnp.dot(ct_ref[...], dc_ref[...], preferred_element_type=F32, precision=lax.Precision.HIGHEST)
        g_ref[...] = g
        d_ref[...], m2_ref[...], v2_ref[...] = _adamw(w_ref[...], g, m_ref[...], v_ref[...])

    t2 = pl.BlockSpec((rb, C), lambda i: (i, 0))
    sd = jax.ShapeDtypeStruct((R, C), F32)
    return pl.pallas_call(
        body, name=name, grid=(R // rb,),
        in_specs=[pl.BlockSpec((rb, NDEV), lambda i: (i, 0)), _full((NDEV, C)), t2, t2, t2],
        out_specs=[t2, t2, t2, t2], out_shape=[sd, sd, sd, sd],
        compiler_params=pltpu.CompilerParams(dimension_semantics=("arbitrary",), vmem_limit_bytes=VMEM_LIMIT),
    )(cact_t, dcs, w, m, v)


_SMALL = ["ada_b", "ada_f_b", "norm1_g", "b_in", "a_ln_g", "a_ln_b", "a_spatial_b", "b_conv_b", "b_gn_g", "b_gn_b",
          "out_norm_a_g", "out_norm_b_g", "norm2_g", "norm_f_g", "a_spatial_w", "b_conv_w"]


def _adam_small(vsum, wssum, gcw, params):
    names = _SMALL
    flat = []
    for n in names:
        flat += list(params[n])

    def body(vs_ref, ws_ref, gcw_ref, *rest):
        ins = rest[:3 * len(names)]
        outs = rest[3 * len(names):]
        for pi, n in enumerate(names):
            w_ref, m_ref, v_ref = ins[3 * pi:3 * pi + 3]
            g_ref, d_ref, m2_ref, v2_ref = outs[4 * pi:4 * pi + 4]
            if n in ("ada_b", "ada_f_b", "b_in"):
                row0 = {"ada_b": 0, "ada_f_b": 6, "b_in": 9}[n]
                pieces = [(vs_ref[row0 + r:row0 + r + 1, :], slice(r * D, (r + 1) * D))
                          for r in range(w_ref.shape[1] // D)]
            elif n == "a_spatial_w":
                pieces = [(ws_ref[...], slice(None))]
            elif n == "b_conv_w":
                pieces = [(gcw_ref[...], slice(None))]
            else:
                row, off, width = _VEC_AT[n]
                pieces = [(vs_ref[row:row + 1, off:off + width], slice(None))]
            for g, cs in pieces:
                g_ref[:, cs] = g
                d_ref[:, cs], m2_ref[:, cs], v2_ref[:, cs] = _adamw(w_ref[:, cs], g, m_ref[:, cs], v_ref[:, cs])

    out_shape = []
    for n in names:
        out_shape += [jax.ShapeDtypeStruct(params[n][0].shape, F32)] * 4
    outs = pl.pallas_call(
        body, name="adam_small",
        in_specs=[VM] * (3 + len(flat)), out_specs=[VM] * len(out_shape), out_shape=out_shape,
        compiler_params=pltpu.CompilerParams(vmem_limit_bytes=VMEM_LIMIT),
    )(vsum, wssum, gcw, *flat)
    return {n: outs[4 * pi:4 * pi + 4] for pi, n in enumerate(names)}


def _token_tile(T, want):
    return want if T % want == 0 else T


def kernel(x, c, ada_w, ada_b, norm1_g, w_in, b_in, a_ln_g, a_ln_b, a_spatial_w, a_spatial_b, b_conv_w, b_conv_b, b_gn_g, b_gn_b, out_norm_a_g, out_norm_b_g, w_out, norm2_g, w_ffn_in, w_ffn_out, ada_f_w, ada_f_b, norm_f_g, loss_target, m_ada_w, m_ada_b, m_norm1_g, m_w_in, m_b_in, m_a_ln_g, m_a_ln_b, m_a_spatial_w, m_a_spatial_b, m_b_conv_w, m_b_conv_b, m_b_gn_g, m_b_gn_b, m_out_norm_a_g, m_out_norm_b_g, m_w_out, m_norm2_g, m_w_ffn_in, m_w_ffn_out, m_ada_f_w, m_ada_f_b, m_norm_f_g, v_ada_w, v_ada_b, v_norm1_g, v_w_in, v_b_in, v_a_ln_g, v_a_ln_b, v_a_spatial_w, v_a_spatial_b, v_b_conv_w, v_b_conv_b, v_b_gn_g, v_b_gn_b, v_out_norm_a_g, v_out_norm_b_g, v_w_out, v_norm2_g, v_w_ffn_in, v_w_ffn_out, v_ada_f_w, v_ada_f_b, v_norm_f_g):
    T = x.shape[1]
    idx = 4 * lax.axis_index("x") + 2 * lax.axis_index("y") + lax.axis_index("c")
    x2d = x.reshape(T, D)
    tgt = loss_target.reshape(T, D)

    conv_s = jnp.pad(b_conv_w[0], ((0, HALO - KW), (0, 0)))
    call, cparts, cfparts, convg, (win_g, wout_g) = _gather(
        c, ada_w[0], ada_b.reshape(NDEV, -1), ada_f_w, ada_f_b.reshape(NDEV, -1), conv_s,
        [w_in[0].astype(BF16), w_out[0].astype(BF16)])
    mod = jnp.concatenate([cparts.reshape(6, D), cfparts.reshape(2, D)], axis=0)
    wout = wout_g.reshape(D, D)
    cw = jnp.transpose(convg, (1, 0, 2)).reshape(HALO, DB)

    tril = jnp.tril(jnp.ones((CH, CH), dtype=bool))
    wsm = jnp.where(tril[None], a_spatial_w[0], 0.0).astype(BF16)
    wcat = wsm.reshape(NH * CH, CH)
    wcat_t = jnp.transpose(wsm, (0, 2, 1)).reshape(NH * CH, CH)
    bsf = jnp.repeat(a_spatial_b[0].T, DA // NH, axis=1)
    lane = jnp.arange(DB)
    pm = jnp.where((lane[:, None] >> 6) == (lane[None, :] >> 6), 1.0 / 64.0, 0.0).astype(BF16)
    esel = jnp.where((lane[:, None] >> 6) == jnp.arange(CH)[None, :], 1.0, 0.0).astype(BF16)

    tm = _token_tile(T, 256)
    tk = _token_tile(T, 1024)
    (x1, hb, z, mixed, yc, yb, o), (wfi_g, wfo_g) = _mix_fwd(
        x2d, mod, norm1_g, win_g, b_in, a_ln_g, a_ln_b, wcat, bsf, cw, b_conv_b, b_gn_g, b_gn_b, out_norm_a_g,
        out_norm_b_g, wout, pm, [w_ffn_in[0].astype(BF16), w_ffn_out[0].astype(BF16)], tm)
    dx1, h2b, dgu, act, dxg, acc_f = _ffn(x1, tgt, mod, norm2_g, norm_f_g.reshape(1, D), wfi_g,
                                          wfo_g.reshape(DFF, D), tm)
    g_wfi = _wgrad_b_blocked(h2b, dgu, 2, tk, "wgrad_ffn_in")
    g_wfo = _wgrad_a_blocked(act, dxg, tk, "wgrad_ffn_out").reshape(NDEV, DFF // NDEV, D)
    (gx, dz, dog, acc_v, acc_b, acc_a, acc_bs, acc_ws, acc_cw), (r_wfi, r_wfo) = _mix_bwd(
        dx1, x2d, z, mixed, yc, o, mod, norm1_g, win_g, a_ln_g, a_ln_b, wcat, wcat_t, cw, b_gn_g, b_gn_b,
        out_norm_a_g, out_norm_b_g, wout, pm, esel, [g_wfi, g_wfo], tm)
    g_win = _wgrad_cols(hb, dz, NDEV, tk, "wgrad_in")
    g_wout = _wgrad_cols(yb, dog, 1, tk, "wgrad_out").reshape(NDEV, D // NDEV, D)

    vsum, dcond_all, wssum, (r_win, r_wout) = _reduce(acc_f, acc_v, acc_b, acc_a, acc_bs, acc_cw, acc_ws,
                                                      [g_win, g_wout])

    res = {}
    res["w_in"] = _adam_big(r_win, w_in[0], m_w_in[0], v_w_in[0], 256, "adam_w_in")
    res["w_out"] = _adam_big(r_wout, w_out[0], m_w_out[0], v_w_out[0], D // NDEV, "adam_w_out")
    res["w_ffn_in"] = _adam_big(r_wfi, w_ffn_in[0], m_w_ffn_in[0], v_w_ffn_in[0], 256, "adam_w_ffn_in")
    res["w_ffn_out"] = _adam_big(r_wfo, w_ffn_out[0], m_w_ffn_out[0], v_w_ffn_out[0], DFF // NDEV // 2,
                                 "adam_w_ffn_out")
    cact_t = (call * jax.nn.sigmoid(call)).T
    dcond = dcond_all.reshape(NDEV, 8 * D)
    nada = ada_w.shape[2]
    nadf = ada_f_w.shape[1]
    dcs = lax.dynamic_slice(dcond, (0, idx * nada), (NDEV, nada))
    dcfs = lax.dynamic_slice(dcond, (0, 6 * D + idx * nadf), (NDEV, nadf))
    res["ada_w"] = _adam_ada(cact_t, dcs, ada_w[0], m_ada_w[0], v_ada_w[0], 256, "adam_ada_w")
    res["ada_f_w"] = _adam_ada(cact_t, dcfs, ada_f_w, m_ada_f_w, v_ada_f_w, 256, "adam_ada_f_w")
    ncw = b_conv_w.shape[2]
    gcw = lax.dynamic_slice(vsum, (_CW_ROW, idx * ncw), (KW, ncw))
    two = lambda a: a.reshape(1, -1) if a.ndim == 1 else a.reshape(-1, a.shape[-1])
    small_in = {
        "ada_b": (ada_b, m_ada_b, v_ada_b), "ada_f_b": (ada_f_b, m_ada_f_b, v_ada_f_b),
        "norm1_g": (norm1_g, m_norm1_g, v_norm1_g), "b_in": (b_in, m_b_in, v_b_in),
        "a_ln_g": (a_ln_g, m_a_ln_g, v_a_ln_g), "a_ln_b": (a_ln_b, m_a_ln_b, v_a_ln_b),
        "a_spatial_b": (a_spatial_b.reshape(1, D), m_a_spatial_b.reshape(1, D), v_a_spatial_b.reshape(1, D)),
        "b_conv_b": (b_conv_b, m_b_conv_b, v_b_conv_b), "b_gn_g": (b_gn_g, m_b_gn_g, v_b_gn_g),
        "b_gn_b": (b_gn_b, m_b_gn_b, v_b_gn_b), "out_norm_a_g": (out_norm_a_g, m_out_norm_a_g, v_out_norm_a_g),
        "out_norm_b_g": (out_norm_b_g, m_out_norm_b_g, v_out_norm_b_g),
        "norm2_g": (norm2_g, m_norm2_g, v_norm2_g), "norm_f_g": (norm_f_g, m_norm_f_g, v_norm_f_g),
        "a_spatial_w": (a_spatial_w, m_a_spatial_w, v_a_spatial_w),
        "b_conv_w": (b_conv_w[0], m_b_conv_w[0], v_b_conv_w[0]),
    }
    small_in = {n: tuple(two(a) for a in t) for n, t in small_in.items()}
    res.update(_adam_small(vsum, wssum, gcw, small_in))

    loss = lax.psum(0.5 / D * jnp.sum(acc_f[7]), ("x", "y", "c"))
    shapes = {"ada_w": ada_w, "ada_b": ada_b, "norm1_g": norm1_g, "w_in": w_in, "b_in": b_in, "a_ln_g": a_ln_g,
              "a_ln_b": a_ln_b, "a_spatial_w": a_spatial_w, "a_spatial_b": a_spatial_b, "b_conv_w": b_conv_w,
              "b_conv_b": b_conv_b, "b_gn_g": b_gn_g, "b_gn_b": b_gn_b, "out_norm_a_g": out_norm_a_g,
              "out_norm_b_g": out_norm_b_g, "w_out": w_out, "norm2_g": norm2_g, "w_ffn_in": w_ffn_in,
              "w_ffn_out": w_ffn_out, "ada_f_w": ada_f_w, "ada_f_b": ada_f_b, "norm_f_g": norm_f_g}
    order = list(shapes)
    outs = [loss, gx.reshape(x.shape)]
    for which in range(4):
        outs += [res[n][which].reshape(shapes[n].shape) for n in order]
    return tuple(outs)
```

```python
import math

import jax
import jax.numpy as jnp
from jax import lax
from jax.experimental import pallas as pl
from jax.experimental.pallas import tpu as pltpu

F32 = jnp.float32
BF16 = jnp.bfloat16

D = 1024
DA = 512
DB = 512
DIN = 2048
DFF = 2816
NH = 8
CH = 128
KW = 31
HALO = 32
NDEV = 8
WIN_B = DIN // NDEV
WFI_B = 2 * DFF // NDEV
NFB = DFF // WFI_B
EPS = 1e-6
NVEC = 56
VMEM_LIMIT = 56 * 1024 * 1024

ADAM_LR, ADAM_B1, ADAM_B2, ADAM_EPS, ADAM_WD, ADAM_STEP = 0.001, 0.9, 0.999, 1e-08, 0.01, 10

MESH = pl.DeviceIdType.MESH


def _dot(a, b):
    return jnp.dot(a, b, preferred_element_type=F32)


def _dot_nt(a, b):
    return lax.dot_general(a, b, (((1,), (1,)), ((), ())), preferred_element_type=F32)


def _dot_tn(a, b):
    return lax.dot_general(a, b, (((0,), (0,)), ((), ())), preferred_element_type=F32)


def _rs(v):
    return lax.rsqrt(jnp.mean(v * v, axis=-1, keepdims=True) + EPS)


def _sig(v):
    return 1.0 / (1.0 + jnp.exp(-v))


_INV_SQRT2 = 1.0 / math.sqrt(2.0)
_INV_SQRT2PI = 1.0 / math.sqrt(2.0 * math.pi)


def _gelu_parts(v):
    cdf = 0.5 * (1.0 + lax.erf(v * _INV_SQRT2))
    pdf = jnp.exp(-0.5 * v * v) * _INV_SQRT2PI
    return v * cdf, cdf + v * pdf


def _grp_mean(v, pm):
    hi = v.astype(BF16)
    lo = (v - hi.astype(F32)).astype(BF16)
    return _dot(hi, pm) + _dot(lo, pm)


def _colsum(v):
    return jnp.sum(v, axis=0, keepdims=True)


def _full(shape):
    nd = len(shape)
    return pl.BlockSpec(shape, lambda *_: (0,) * nd)


def _resident(shape):
    nd = len(shape)
    return pl.BlockSpec(shape, lambda *_: (0,) * nd, pipeline_mode=pl.Buffered(1))


HBM = pl.BlockSpec(memory_space=pl.ANY)
VM = pl.BlockSpec(memory_space=pltpu.VMEM)


SH_ROWS = HALO - 8


def _shifted_copies(buf, shbuf, tm):
    for b in range(1, 8):
        shbuf[b - 1] = buf[b:b + tm + SH_ROWS, :]


def _window(buf, shbuf, off, tm):
    a, b = divmod(off, 8)
    if b == 0:
        return buf[8 * a:8 * a + tm, :]
    return shbuf[b - 1, 8 * a:8 * a + tm, :]


def _head_of_lane(rows):
    return lax.broadcasted_iota(jnp.int32, (rows, DA), 1) >> 6


def _block_pick(r, lane_head):
    out = jnp.zeros((CH, DA), F32)
    for h in range(NH):
        out = jnp.where(lane_head == h, r[h * CH:(h + 1) * CH, :], out)
    return out


def _place():
    x, y, c = lax.axis_index("x"), lax.axis_index("y"), lax.axis_index("c")
    return x, y, c, 4 * x + 2 * y + c


def _dev(t):
    return (t >> 2, (t >> 1) & 1, t & 1)


class _AllGather:
    def __init__(self, w_in, w_out, wss, wrs, lsem):
        x, y, c, idx = _place()
        me, sibling = (x, y, c), (x, y, 1 - c)
        chips = [(1 - x, y), (x, 1 - y), (1 - x, 1 - y)]
        nw = len(w_in)

        def blk(p):
            return 4 * p[0] + 2 * p[1] + p[2]

        def wcopy(a, k, block, to, src=None):
            dst = w_out[a].at[blk(block)]
            return pltpu.make_async_remote_copy(src_ref=dst if src is None else src, dst_ref=dst,
                                                send_sem=wss.at[a, k], recv_sem=wrs.at[a, k],
                                                device_id=to, device_id_type=MESH)

        self.mine = [pltpu.make_async_copy(w_in[a], w_out[a].at[idx], lsem.at[a]) for a in range(nw)]
        self.first = []
        for a in range(nw):
            self.first.append(wcopy(a, 0, me, sibling, src=w_in[a]))
            self.first += [wcopy(a, 1 + j, me, (*chip, c), src=w_in[a]) for j, chip in enumerate(chips)]
        self.landed = [[wcopy(a, 1 + j, (*chip, c), me) for a in range(nw)] for j, chip in enumerate(chips)]
        self.passed = [[wcopy(a, 4 + j, (*chip, c), sibling) for a in range(nw)] for j, chip in enumerate(chips)]
        self.from_sibling = []
        for a in range(nw):
            self.from_sibling.append(wcopy(a, 0, sibling, me))
            self.from_sibling += [wcopy(a, 4 + j, (*chip, 1 - c), me) for j, chip in enumerate(chips)]

    def start(self):
        for cp in self.mine + self.first:
            cp.start()

    def forward(self):
        for land, pas in zip(self.landed, self.passed):
            for l, p in zip(land, pas):
                l.wait_recv()
                p.start()

    def finish(self):
        for cp in self.from_sibling:
            cp.wait_recv()
        for cp in self.first:
            cp.wait_send()
        for pas in self.passed:
            for p in pas:
                p.wait_send()
        for cp in self.mine:
            cp.wait()


AG_SEMS = lambda nw: [pltpu.SemaphoreType.DMA((nw, 7)), pltpu.SemaphoreType.DMA((nw, 7)),
                      pltpu.SemaphoreType.DMA((nw,))]


class _ReduceScatter:
    def __init__(self, g_in, r_out, gss, grs, lsem):
        x, y, c, idx = _place()
        me = (x, y, c)
        nw = len(g_in)
        self.mine = [pltpu.make_async_copy(g_in[a].at[idx], r_out[a].at[0], lsem.at[a]) for a in range(nw)]
        self.sends, self.recvs = [], []
        for k in range(1, NDEV):
            t = idx ^ k
            for a in range(nw):
                self.sends.append(pltpu.make_async_remote_copy(
                    src_ref=g_in[a].at[t], dst_ref=r_out[a].at[k], send_sem=gss.at[a, k - 1],
                    recv_sem=grs.at[a, k - 1], device_id=_dev(t), device_id_type=MESH))
                self.recvs.append(pltpu.make_async_remote_copy(
                    src_ref=g_in[a].at[0], dst_ref=r_out[a].at[k], send_sem=gss.at[a, k - 1],
                    recv_sem=grs.at[a, k - 1], device_id=me, device_id_type=MESH))

    def start(self):
        for cp in self.mine + self.sends:
            cp.start()

    def finish(self):
        for cp in self.recvs:
            cp.wait_recv()
        for cp in self.sends:
            cp.wait_send()
        for cp in self.mine:
            cp.wait()


RS_SEMS = AG_SEMS


def _mix_fwd(x, mod, g1, win, b_in, lng, lnb, wcat, bsf, cw, cb, gng, gnb, oga, ogb, wout, pm, ffn_shards, tm):
    T = x.shape[0]
    nt = T // tm
    nch = tm // CH
    nw = len(ffn_shards)
    fwd_step = (5 * nt) // 8

    def body(x_ref, mod_ref, g1_ref, win_ref, bin_ref, lng_ref, lnb_ref, wcat_ref, bsf_ref, cw_ref, cb_ref,
             gng_ref, gnb_ref, oga_ref, ogb_ref, wout_ref, pm_ref, *rest):
        sh_in = rest[:nw]
        x1_ref, h_ref, z_ref, mixed_ref, yc_ref, y_ref, o_ref = rest[nw:nw + 7]
        sh_out = rest[nw + 7:2 * nw + 7]
        glbuf, shbuf, wss, wrs, lsem = rest[2 * nw + 7:]
        i = pl.program_id(0)

        @pl.when(i == 0)
        def _():
            _AllGather(sh_in, sh_out, wss, wrs, lsem).start()

        xv = x_ref[...]
        shift1 = mod_ref[0:1, :]
        scale1 = mod_ref[1:2, :]
        gate1 = mod_ref[2:3, :]
        h = (xv * _rs(xv) * g1_ref[...]) * (1.0 + scale1) + shift1
        hb = h.astype(BF16)
        h_ref[...] = hb
        z = jnp.concatenate([_dot(hb, win_ref[j]) for j in range(NDEV)], axis=1) + bin_ref[...]
        z_ref[...] = z
        gu, _ = _gelu_parts(z[:, 0:DA])
        gv, _ = _gelu_parts(z[:, DA:2 * DA])
        xc = gv - jnp.mean(gv, axis=-1, keepdims=True)
        vn = xc * lax.rsqrt(jnp.mean(xc * xc, axis=-1, keepdims=True) + EPS) * lng_ref[...] + lnb_ref[...]
        vnb = vn.astype(BF16)
        lane_head = _head_of_lane(CH)
        chunks = []
        for ci in range(nch):
            r = _dot(wcat_ref[...], vnb[ci * CH:(ci + 1) * CH, :])
            chunks.append(_block_pick(r, lane_head) + bsf_ref[...])
        mixed = jnp.concatenate(chunks, axis=0) if nch > 1 else chunks[0]
        mixed_ref[...] = mixed
        ya = gu * mixed
        gl = z[:, 2 * DA:2 * DA + DB] * _sig(z[:, 2 * DA + DB:])

        @pl.when(i == 0)
        def _():
            glbuf[0:HALO, :] = jnp.zeros((HALO, DB), F32)

        glbuf[HALO:HALO + tm, :] = gl
        _shifted_copies(glbuf, shbuf, tm)
        yc = jnp.zeros((tm, DB), F32) + cb_ref[...]
        for k in range(KW):
            yc = yc + cw_ref[k:k + 1, :] * _window(glbuf, shbuf, HALO - (KW - 1) + k, tm)
        glbuf[0:HALO, :] = gl[tm - HALO:, :]
        yc_ref[...] = yc
        pmv = pm_ref[...]
        dc = yc - _grp_mean(yc, pmv)
        yg = dc * lax.rsqrt(_grp_mean(dc * dc, pmv) + EPS) * gng_ref[...] + gnb_ref[...]
        yb = yg * _sig(yg)
        na = ya * _rs(ya) * oga_ref[...]
        nb = yb * _rs(yb) * ogb_ref[...]
        yv = jnp.concatenate([na, nb], axis=1).astype(BF16)
        y_ref[...] = yv
        o = _dot(yv, wout_ref[...])
        o_ref[...] = o
        x1_ref[...] = xv + gate1 * o

        @pl.when(i == fwd_step)
        def _():
            _AllGather(sh_in, sh_out, wss, wrs, lsem).forward()

        @pl.when(i == nt - 1)
        def _():
            _AllGather(sh_in, sh_out, wss, wrs, lsem).finish()

    tile = lambda w: pl.BlockSpec((tm, w), lambda i: (i, 0))
    outs = pl.pallas_call(
        body,
        name="mix_fwd",
        grid=(nt,),
        in_specs=[tile(D), _full((8, D)), _full((1, D)), _resident((NDEV, D, WIN_B)), _full((1, DIN)),
                  _full((1, DA)), _full((1, DA)), _full((NH * CH, CH)), _full((CH, DA)), _full((HALO, DB)),
                  _full((1, DB)), _full((1, DB)), _full((1, DB)), _full((1, DA)), _full((1, DB)),
                  _resident((D, D)), _full((DB, DB))] + [HBM] * nw,
        out_specs=[tile(D), tile(D), tile(DIN), tile(DA), tile(DB), tile(D), tile(D)] + [HBM] * nw,
        out_shape=[jax.ShapeDtypeStruct((T, D), F32), jax.ShapeDtypeStruct((T, D), BF16),
                   jax.ShapeDtypeStruct((T, DIN), F32), jax.ShapeDtypeStruct((T, DA), F32),
                   jax.ShapeDtypeStruct((T, DB), F32), jax.ShapeDtypeStruct((T, D), BF16),
                   jax.ShapeDtypeStruct((T, D), F32)]
                  + [jax.ShapeDtypeStruct((NDEV,) + s.shape, s.dtype) for s in ffn_shards],
        scratch_shapes=[pltpu.VMEM((HALO + tm, DB), F32), pltpu.VMEM((7, tm + SH_ROWS, DB), F32)] + AG_SEMS(nw),
        compiler_params=pltpu.CompilerParams(dimension_semantics=("arbitrary",), vmem_limit_bytes=VMEM_LIMIT),
    )(x, mod, g1, win, b_in, lng, lnb, wcat, bsf, cw, cb, gng, gnb, oga, ogb, wout, pm, *ffn_shards)
    return outs[:7], outs[7:]


def _ffn(x1, tgt, mod, g2, gf, wfi, wfo, tm):
    T = x1.shape[0]
    nt = T // tm

    def body(x1_ref, tgt_ref, mod_ref, g2_ref, gf_ref, wfi_ref, wfo_ref,
             dx1_ref, h2_ref, dgu_ref, act_ref, dxg_ref, acc_ref, g_s, u_s):
        i = pl.program_id(0)

        @pl.when(i == 0)
        def _():
            acc_ref[...] = jnp.zeros((8, D), F32)

        x1 = x1_ref[...]
        shift2 = mod_ref[3:4, :]
        scale2 = mod_ref[4:5, :]
        gate2 = mod_ref[5:6, :]
        shiftf = mod_ref[6:7, :]
        scalef = mod_ref[7:8, :]
        g2v = g2_ref[...]
        gfv = gf_ref[...]
        r2 = _rs(x1)
        xn2 = x1 * r2
        h2b = (xn2 * g2v * (1.0 + scale2) + shift2).astype(BF16)
        h2_ref[...] = h2b
        f = jnp.zeros((tm, D), F32)
        for j in range(NFB):
            g = _dot(h2b, wfi_ref[j])
            u = _dot(h2b, wfi_ref[NFB + j])
            g_s[j] = g
            u_s[j] = u
            actb = (g * _sig(g) * u).astype(BF16)
            act_ref[j] = actb
            f = f + _dot(actb, wfo_ref[j * WFI_B:(j + 1) * WFI_B, :])
        x2 = x1 + gate2 * f
        rf = _rs(x2)
        xnf = x2 * rf
        out = xnf * gfv * (1.0 + scalef) + shiftf
        e = out - tgt_ref[...]
        dout = e * (1.0 / D)
        acc_ref[7:8, :] += _colsum(e * e)
        acc_ref[0:1, :] += _colsum(dout)
        acc_ref[1:2, :] += _colsum(dout * xnf * gfv)
        acc_ref[2:3, :] += _colsum(dout * (1.0 + scalef) * xnf)
        dxnf = dout * (1.0 + scalef) * gfv
        dx2 = rf * (dxnf - xnf * jnp.mean(dxnf * xnf, axis=-1, keepdims=True))
        acc_ref[3:4, :] += _colsum(dx2 * f)
        dxgb = (dx2 * gate2).astype(BF16)
        dxg_ref[...] = dxgb
        dh2 = jnp.zeros((tm, D), F32)
        for j in range(NFB):
            dact = _dot_nt(dxgb, wfo_ref[j * WFI_B:(j + 1) * WFI_B, :])
            g = g_s[j]
            u = u_s[j]
            s = _sig(g)
            dgb = (dact * u * (s * (1.0 + g * (1.0 - s)))).astype(BF16)
            dub = (dact * (g * s)).astype(BF16)
            dgu_ref[j] = dgb
            dgu_ref[NFB + j] = dub
            dh2 = dh2 + _dot_nt(dgb, wfi_ref[j])
            dh2 = dh2 + _dot_nt(dub, wfi_ref[NFB + j])
        acc_ref[4:5, :] += _colsum(dh2)
        acc_ref[5:6, :] += _colsum(dh2 * xn2 * g2v)
        acc_ref[6:7, :] += _colsum(dh2 * (1.0 + scale2) * xn2)
        dxn2 = dh2 * (1.0 + scale2) * g2v
        dx1_ref[...] = dx2 + r2 * (dxn2 - xn2 * jnp.mean(dxn2 * xn2, axis=-1, keepdims=True))

    tile = lambda w: pl.BlockSpec((tm, w), lambda i: (i, 0))
    blocked = lambda n: pl.BlockSpec((n, tm, WFI_B), lambda i: (0, i, 0))
    return pl.pallas_call(
        body,
        name="ffn_fwd_bwd",
        grid=(nt,),
        in_specs=[tile(D), tile(D), _full((8, D)), _full((1, D)), _full((1, D)),
                  _resident((NDEV, D, WFI_B)), _resident((DFF, D))],
        out_specs=[tile(D), tile(D), blocked(NDEV), blocked(NFB), tile(D), _full((8, D))],
        out_shape=[jax.ShapeDtypeStruct((T, D), F32), jax.ShapeDtypeStruct((T, D), BF16),
                   jax.ShapeDtypeStruct((NDEV, T, WFI_B), BF16), jax.ShapeDtypeStruct((NFB, T, WFI_B), BF16),
                   jax.ShapeDtypeStruct((T, D), BF16), jax.ShapeDtypeStruct((8, D), F32)],
        scratch_shapes=[pltpu.VMEM((NFB, tm, WFI_B), F32), pltpu.VMEM((NFB, tm, WFI_B), F32)],
        compiler_params=pltpu.CompilerParams(dimension_semantics=("arbitrary",), vmem_limit_bytes=VMEM_LIMIT),
    )(x1, tgt, mod, g2, gf, wfi, wfo)


def _mix_bwd(dx1, x, z, mixed, yc, o, hb, yb, mod, g1, win, lng, lnb, wcat, wcat_t, cw, gng, gnb, oga, ogb, wout,
             pm, esel, ffn_grads, tm):
    T = x.shape[0]
    nt = T // tm
    nch = tm // CH
    nw = len(ffn_grads)
    WOB = 256

    def body(dx1_ref, x_ref, z_ref, mixed_ref, yc_ref, o_ref, hb_ref, yb_ref, mod_ref, g1_ref, win_ref, lng_ref,
             lnb_ref, wcat_ref, wcatt_ref, cw_ref, gng_ref, gnb_ref, oga_ref, ogb_ref, wout_ref, pm_ref, esel_ref,
             *rest):
        g_in = rest[:nw]
        gx_ref, accv_ref, accb_ref, acca_ref, accbs_ref, accws_ref, acccw_ref = rest[nw:nw + 7]
        r_out = rest[nw + 7:2 * nw + 7]
        gwin_ref, gwout_ref = rest[2 * nw + 7:2 * nw + 9]
        dycbuf, shbuf, bs_s, acc_win, acc_wout, st_win, st_wout, gss, grs, lsem = rest[2 * nw + 9:]
        i = pl.program_id(0)

        @pl.when(i == 0)
        def _():
            _ReduceScatter(g_in, r_out, gss, grs, lsem).start()
            acc_win[...] = jnp.zeros((NDEV, D, WIN_B), F32)
            acc_wout[...] = jnp.zeros((D, D), F32)
            accv_ref[...] = jnp.zeros((8, D), F32)
            accb_ref[...] = jnp.zeros((1, DIN), F32)
            acca_ref[...] = jnp.zeros((8, DA), F32)
            accws_ref[...] = jnp.zeros((NH * CH, CH), F32)
            acccw_ref[...] = jnp.zeros((HALO, DB), F32)
            bs_s[...] = jnp.zeros((CH, DA), F32)
            dycbuf[tm:tm + HALO, :] = jnp.zeros((HALO, DB), F32)

        shift1 = mod_ref[0:1, :]
        scale1 = mod_ref[1:2, :]
        gate1 = mod_ref[2:3, :]
        g1v = g1_ref[...]
        xv = x_ref[...]
        r1 = _rs(xv)
        xn1 = xv * r1
        z = z_ref[...]
        u = z[:, 0:DA]
        v = z[:, DA:2 * DA]
        val = z[:, 2 * DA:2 * DA + DB]
        gate = z[:, 2 * DA + DB:]
        gu, dgelu_u = _gelu_parts(u)
        gv, dgelu_v = _gelu_parts(v)
        xc = gv - jnp.mean(gv, axis=-1, keepdims=True)
        rsl = lax.rsqrt(jnp.mean(xc * xc, axis=-1, keepdims=True) + EPS)
        vhat = xc * rsl
        lngv = lng_ref[...]
        vnb = (vhat * lngv + lnb_ref[...]).astype(BF16)
        mixed = mixed_ref[...]
        ya = gu * mixed
        ra = _rs(ya)
        yan = ya * ra
        sgt = _sig(gate)
        gl = val * sgt
        pmv = pm_ref[...]
        ycv = yc_ref[...]
        dcen = ycv - _grp_mean(ycv, pmv)
        rsg = lax.rsqrt(_grp_mean(dcen * dcen, pmv) + EPS)
        yhat = dcen * rsg
        gngv = gng_ref[...]
        yg = yhat * gngv + gnb_ref[...]
        sgy = _sig(yg)
        yb = yg * sgy
        rb = _rs(yb)
        ybn = yb * rb
        dx1 = dx1_ref[...]
        accv_ref[0:1, :] += _colsum(dx1 * o_ref[...])
        dogb = (dx1 * gate1).astype(BF16)
        acc_wout[...] += _dot_tn(yb_ref[...], dogb)
        dy = _dot_nt(dogb, wout_ref[...])
        dna = dy[:, 0:DA]
        dnb = dy[:, DA:]
        ogav = oga_ref[...]
        ogbv = ogb_ref[...]
        acca_ref[2:3, :] += _colsum(dna * yan)
        acca_ref[3:4, :] += _colsum(dnb * ybn)
        ta = dna * ogav
        dya = ra * (ta - yan * jnp.mean(ta * yan, axis=-1, keepdims=True))
        tb = dnb * ogbv
        dyb = rb * (tb - ybn * jnp.mean(tb * ybn, axis=-1, keepdims=True))
        dgu = dya * mixed
        dm = dya * gu
        lane_head = _head_of_lane(CH)
        dvn_chunks = []
        bs_acc = bs_s[...]
        for ci in range(nch):
            dmc = dm[ci * CH:(ci + 1) * CH, :]
            bs_acc = bs_acc + dmc
            dmcb = dmc.astype(BF16)
            dvn_chunks.append(_block_pick(_dot(wcatt_ref[...], dmcb), lane_head))
            zero = jnp.zeros((CH, DA), BF16)
            stack = jnp.concatenate([jnp.where(lane_head == h, dmcb, zero) for h in range(NH)], axis=0)
            accws_ref[...] += _dot_nt(stack, vnb[ci * CH:(ci + 1) * CH, :])
        bs_s[...] = bs_acc
        dvn = jnp.concatenate(dvn_chunks, axis=0) if nch > 1 else dvn_chunks[0]
        acca_ref[0:1, :] += _colsum(dvn * vhat)
        acca_ref[1:2, :] += _colsum(dvn)
        dvh = dvn * lngv
        dgv = rsl * (dvh - jnp.mean(dvh, axis=-1, keepdims=True)
                     - vhat * jnp.mean(dvh * vhat, axis=-1, keepdims=True))
        du = dgu * dgelu_u
        dv = dgv * dgelu_v
        dyg = dyb * (sgy * (1.0 + yg * (1.0 - sgy)))
        acca_ref[5:6, :] += _colsum(dyg * yhat)
        acca_ref[6:7, :] += _colsum(dyg)
        dyh = dyg * gngv
        dyc = rsg * (dyh - _grp_mean(dyh, pmv) - yhat * _grp_mean(dyh * yhat, pmv))
        acca_ref[4:5, :] += _colsum(dyc)
        dycbuf[0:tm, :] = dyc
        _shifted_copies(dycbuf, shbuf, tm)
        dgl = jnp.zeros((tm, DB), F32)
        for k in range(KW):
            win_k = _window(dycbuf, shbuf, KW - 1 - k, tm)
            dgl = dgl + cw_ref[k:k + 1, :] * win_k
            acccw_ref[k:k + 1, :] += _colsum(win_k * gl)
        dycbuf[tm:tm + HALO, :] = dyc[0:HALO, :]
        dval = dgl * sgt
        dgate = dgl * val * sgt * (1.0 - sgt)
        dz = jnp.concatenate([du, dv, dval, dgate], axis=1)
        accb_ref[...] += _colsum(dz)
        dzb = dz.astype(BF16)
        hbv = hb_ref[...]
        dh = jnp.zeros((tm, D), F32)
        for j in range(NDEV):
            dzj = dzb[:, j * WIN_B:(j + 1) * WIN_B]
            acc_win[j] += _dot_tn(hbv, dzj)
            dh = dh + _dot_nt(dzj, win_ref[j])
        accv_ref[1:2, :] += _colsum(dh)
        accv_ref[2:3, :] += _colsum(dh * xn1 * g1v)
        accv_ref[3:4, :] += _colsum(dh * (1.0 + scale1) * xn1)
        dxn1 = dh * (1.0 + scale1) * g1v
        gx_ref[...] = dx1 + r1 * (dxn1 - xn1 * jnp.mean(dxn1 * xn1, axis=-1, keepdims=True))

        @pl.when(i == nt - 1)
        def _():
            rows = lax.broadcasted_iota(jnp.int32, (NH * CH, CH), 0) & (CH - 1)
            cols = lax.broadcasted_iota(jnp.int32, (NH * CH, CH), 1)
            accws_ref[...] = jnp.where(cols <= rows, accws_ref[...], 0.0)
            bs = bs_s[...]
            hi = bs.astype(BF16)
            r1_ = bs - hi.astype(F32)
            mid = r1_.astype(BF16)
            lo = (r1_ - mid.astype(F32)).astype(BF16)
            ev = esel_ref[...]
            accbs_ref[...] = _dot(hi, ev) + _dot(mid, ev) + _dot(lo, ev)
            for j in range(NDEV):
                st_win[...] = acc_win[j].astype(BF16)
                pltpu.sync_copy(st_win, gwin_ref.at[j])
            for j in range(D // WOB):
                st_wout[...] = acc_wout[j * WOB:(j + 1) * WOB, :].astype(BF16)
                pltpu.sync_copy(st_wout, gwout_ref.at[pl.ds(j * WOB, WOB)])
            _ReduceScatter(g_in, r_out, gss, grs, lsem).finish()

    rev = lambda w: pl.BlockSpec((tm, w), lambda i: (nt - 1 - i, 0))
    outs = pl.pallas_call(
        body,
        name="mix_bwd",
        grid=(nt,),
        in_specs=[rev(D), rev(D), rev(DIN), rev(DA), rev(DB), rev(D), rev(D), rev(D), _full((8, D)), _full((1, D)),
                  _resident((NDEV, D, WIN_B)), _full((1, DA)), _full((1, DA)), _full((NH * CH, CH)),
                  _full((NH * CH, CH)), _full((HALO, DB)), _full((1, DB)), _full((1, DB)), _full((1, DA)),
                  _full((1, DB)), _resident((D, D)), _full((DB, DB)), _full((DA, CH))] + [HBM] * nw,
        out_specs=[rev(D), _full((8, D)), _full((1, DIN)), _full((8, DA)), _full((CH, CH)),
                   _full((NH * CH, CH)), _full((HALO, DB))] + [HBM] * (nw + 2),
        out_shape=[jax.ShapeDtypeStruct((T, D), F32), jax.ShapeDtypeStruct((8, D), F32),
                   jax.ShapeDtypeStruct((1, DIN), F32), jax.ShapeDtypeStruct((8, DA), F32),
                   jax.ShapeDtypeStruct((CH, CH), F32), jax.ShapeDtypeStruct((NH * CH, CH), F32),
                   jax.ShapeDtypeStruct((HALO, DB), F32)]
                  + [jax.ShapeDtypeStruct(g.shape, g.dtype) for g in ffn_grads]
                  + [jax.ShapeDtypeStruct((NDEV, D, WIN_B), BF16), jax.ShapeDtypeStruct((D, D), BF16)],
        scratch_shapes=[pltpu.VMEM((tm + HALO, DB), F32), pltpu.VMEM((7, tm + SH_ROWS, DB), F32),
                        pltpu.VMEM((CH, DA), F32), pltpu.VMEM((NDEV, D, WIN_B), F32), pltpu.VMEM((D, D), F32),
                        pltpu.VMEM((D, WIN_B), BF16), pltpu.VMEM((WOB, D), BF16)] + RS_SEMS(nw),
        compiler_params=pltpu.CompilerParams(dimension_semantics=("arbitrary",), vmem_limit_bytes=VMEM_LIMIT),
    )(dx1, x, z, mixed, yc, o, hb, yb, mod, g1, win, lng, lnb, wcat, wcat_t, cw, gng, gnb, oga, ogb, wout, pm, esel,
      *ffn_grads)
    return outs[:7], outs[7:7 + nw], outs[7 + nw:]


def _wgrad_cols(a, b, nblk, tk, name):
    T, M = a.shape
    bw = b.shape[1] // nblk
    nk = T // tk

    def body(a_ref, b_ref, o_ref, acc):
        k = pl.program_id(0)

        @pl.when(k == 0)
        def _():
            acc[...] = jnp.zeros((nblk, M, bw), F32)

        av = a_ref[...]
        for j in range(nblk):
            acc[j] += _dot_tn(av, b_ref[:, j * bw:(j + 1) * bw])

        @pl.when(k == nk - 1)
        def _():
            o_ref[...] = acc[...].astype(BF16)

    return pl.pallas_call(
        body, name=name, grid=(nk,),
        in_specs=[pl.BlockSpec((tk, M), lambda k: (k, 0)), pl.BlockSpec((tk, nblk * bw), lambda k: (k, 0))],
        out_specs=_full((nblk, M, bw)),
        out_shape=jax.ShapeDtypeStruct((nblk, M, bw), BF16),
        scratch_shapes=[pltpu.VMEM((nblk, M, bw), F32)],
        compiler_params=pltpu.CompilerParams(dimension_semantics=("arbitrary",), vmem_limit_bytes=VMEM_LIMIT),
    )(a, b)


def _wgrad_b_blocked(a, b3, per, tk, name):
    T, M = a.shape
    nb, _, bw = b3.shape
    nk = T // tk

    def body(a_ref, b_ref, o_ref, acc):
        k = pl.program_id(1)

        @pl.when(k == 0)
        def _():
            acc[...] = jnp.zeros((per, M, bw), F32)

        av = a_ref[...]
        for j in range(per):
            acc[j] += _dot_tn(av, b_ref[j])

        @pl.when(k == nk - 1)
        def _():
            o_ref[...] = acc[...].astype(BF16)

    return pl.pallas_call(
        body, name=name, grid=(nb // per, nk),
        in_specs=[pl.BlockSpec((tk, M), lambda j, k: (k, 0)), pl.BlockSpec((per, tk, bw), lambda j, k: (j, k, 0))],
        out_specs=pl.BlockSpec((per, M, bw), lambda j, k: (j, 0, 0)),
        out_shape=jax.ShapeDtypeStruct((nb, M, bw), BF16),
        scratch_shapes=[pltpu.VMEM((per, M, bw), F32)],
        compiler_params=pltpu.CompilerParams(dimension_semantics=("arbitrary", "arbitrary"),
                                             vmem_limit_bytes=VMEM_LIMIT),
    )(a, b3)


def _wgrad_a_blocked(a3, b, tk, name):
    nb, T, bw = a3.shape
    N = b.shape[1]
    nk = T // tk

    def body(a_ref, b_ref, o_ref, acc):
        k = pl.program_id(1)

        @pl.when(k == 0)
        def _():
            acc[...] = jnp.zeros((bw, N), F32)

        acc[...] += _dot_tn(a_ref[0], b_ref[...])

        @pl.when(k == nk - 1)
        def _():
            o_ref[0] = acc[...].astype(BF16)

    return pl.pallas_call(
        body, name=name, grid=(nb, nk),
        in_specs=[pl.BlockSpec((1, tk, bw), lambda j, k: (j, k, 0)), pl.BlockSpec((tk, N), lambda j, k: (k, 0))],
        out_specs=pl.BlockSpec((1, bw, N), lambda j, k: (j, 0, 0)),
        out_shape=jax.ShapeDtypeStruct((nb, bw, N), BF16),
        scratch_shapes=[pltpu.VMEM((bw, N), F32)],
        compiler_params=pltpu.CompilerParams(dimension_semantics=("arbitrary", "arbitrary"),
                                             vmem_limit_bytes=VMEM_LIMIT),
    )(a3, b)


def _small_copy(src, dst, ss, rs, k, to):
    return pltpu.make_async_remote_copy(src_ref=src, dst_ref=dst, send_sem=ss.at[k], recv_sem=rs.at[k],
                                        device_id=to, device_id_type=MESH)


def _gather(c_row, ada_w, ada_b8, ada_f_w, ada_f_b8, conv_s, shards):
    nw = len(shards)

    def body(c_ref, adaw_ref, adab_ref, adafw_ref, adafb_ref, conv_ref, *rest):
        w_in = rest[:nw]
        call_ref, cparts_ref, cfparts_ref, convg_ref = rest[nw:nw + 4]
        w_out = rest[nw + 4:2 * nw + 4]
        part_s, partf_s, wss, wrs, lsem, s1, r1, s2, r2, s3, r3, s4, r4 = rest[2 * nw + 4:]
        x, y, c, idx = _place()
        me = (x, y, c)
        ag = _AllGather(w_in, w_out, wss, wrs, lsem)
        ag.start()
        call_ref[pl.ds(idx, 1), :] = c_ref[...]
        convg_ref[idx] = conv_ref[...]
        ph1 = []
        for k in range(1, NDEV):
            to = _dev(idx ^ k)
            ph1.append(_small_copy(c_ref, call_ref.at[pl.ds(idx, 1)], s1, r1, k - 1, to))
            ph1.append(_small_copy(conv_ref, convg_ref.at[idx], s2, r2, k - 1, to))
        for cp in ph1:
            cp.start()
        for k in range(1, NDEV):
            src_dev = idx ^ k
            _small_copy(c_ref, call_ref.at[pl.ds(src_dev, 1)], s1, r1, k - 1, me).wait_recv()
            _small_copy(conv_ref, convg_ref.at[src_dev], s2, r2, k - 1, me).wait_recv()
        call = call_ref[...]
        cact = (call * _sig(call))
        part_s[...] = jnp.dot(cact, adaw_ref[...], preferred_element_type=F32,
                              precision=lax.Precision.HIGHEST) + adab_ref[pl.ds(idx, 1), :]
        partf_s[...] = jnp.dot(cact, adafw_ref[...], preferred_element_type=F32,
                               precision=lax.Precision.HIGHEST) + adafb_ref[pl.ds(idx, 1), :]
        cparts_ref[pl.ds(idx, 1), :] = part_s[pl.ds(idx, 1), :]
        cfparts_ref[pl.ds(idx, 1), :] = partf_s[pl.ds(idx, 1), :]
        ph2 = []
        for k in range(1, NDEV):
            t = idx ^ k
            ph2.append(_small_copy(part_s.at[pl.ds(t, 1)], cparts_ref.at[pl.ds(idx, 1)], s3, r3, k - 1, _dev(t)))
            ph2.append(_small_copy(partf_s.at[pl.ds(t, 1)], cfparts_ref.at[pl.ds(idx, 1)], s4, r4, k - 1, _dev(t)))
        for cp in ph2:
            cp.start()
        for k in range(1, NDEV):
            src_dev = idx ^ k
            _small_copy(part_s.at[pl.ds(0, 1)], cparts_ref.at[pl.ds(src_dev, 1)], s3, r3, k - 1, me).wait_recv()
            _small_copy(partf_s.at[pl.ds(0, 1)], cfparts_ref.at[pl.ds(src_dev, 1)], s4, r4, k - 1, me).wait_recv()
        for cp in ph1 + ph2:
            cp.wait_send()
        ag.forward()
        ag.finish()

    dma7 = pltpu.SemaphoreType.DMA((NDEV - 1,))
    outs = pl.pallas_call(
        body,
        name="gather_weights",
        in_specs=[VM] * 6 + [HBM] * nw,
        out_specs=[VM] * 4 + [HBM] * nw,
        out_shape=[jax.ShapeDtypeStruct((NDEV, D), F32), jax.ShapeDtypeStruct((NDEV, ada_w.shape[1]), F32),
                   jax.ShapeDtypeStruct((NDEV, ada_f_w.shape[1]), F32),
                   jax.ShapeDtypeStruct((NDEV,) + conv_s.shape, F32)]
                  + [jax.ShapeDtypeStruct((NDEV,) + s.shape, s.dtype) for s in shards],
        scratch_shapes=[pltpu.VMEM((NDEV, ada_w.shape[1]), F32), pltpu.VMEM((NDEV, ada_f_w.shape[1]), F32)]
                       + AG_SEMS(nw) + [dma7] * 8,
        compiler_params=pltpu.CompilerParams(vmem_limit_bytes=VMEM_LIMIT),
    )(c_row, ada_w, ada_b8, ada_f_w, ada_f_b8, conv_s, *shards)
    return outs[0], outs[1], outs[2], outs[3], outs[4:]


_VEC_AT = {
    "norm1_g": (8, 0, D), "a_ln_g": (11, 0, DA), "a_ln_b": (11, DA, DA), "a_spatial_b": (12, 0, D),
    "b_conv_b": (13, 0, DB), "b_gn_g": (13, DB, DB), "b_gn_b": (14, 0, DB), "out_norm_a_g": (14, DB, DA),
    "out_norm_b_g": (15, 0, DB), "norm2_g": (16, 0, D), "norm_f_g": (17, 0, D),
}
_CW_ROW = 24


def _reduce(acc_f, acc_v, acc_b, acc_a, acc_bs, acc_cw, dws, grads):
    nw = len(grads)

    def body(accf_ref, accv_ref, accb_ref, acca_ref, accbs_ref, acccw_ref, dws_ref, *rest):
        g_in = rest[:nw]
        vsum_ref, dcond_ref, wssum_ref = rest[nw:nw + 3]
        r_out = rest[nw + 3:2 * nw + 3]
        vloc, vbuf, wbuf, wown, gss, grs, lsem, s1, r1, s2, r2, s3, r3 = rest[2 * nw + 3:]
        x, y, c, idx = _place()
        me = (x, y, c)
        rs = _ReduceScatter(g_in, r_out, gss, grs, lsem)
        rs.start()
        vloc[...] = jnp.zeros((NVEC, D), F32)
        vloc[0:1, :] = accv_ref[1:2, :]
        vloc[1:2, :] = accv_ref[2:3, :]
        vloc[2:3, :] = accv_ref[0:1, :]
        vloc[3:4, :] = accf_ref[4:5, :]
        vloc[4:5, :] = accf_ref[5:6, :]
        vloc[5:6, :] = accf_ref[3:4, :]
        vloc[6:7, :] = accf_ref[0:1, :]
        vloc[7:8, :] = accf_ref[1:2, :]
        vloc[8:9, :] = accv_ref[3:4, :]
        vloc[9:10, :] = accb_ref[:, 0:D]
        vloc[10:11, :] = accb_ref[:, D:]
        vloc[11:12, 0:DA] = acca_ref[0:1, :]
        vloc[11:12, DA:] = acca_ref[1:2, :]
        bst = accbs_ref[...].T
        for h in range(NH):
            vloc[12:13, h * CH:(h + 1) * CH] = bst[h:h + 1, :]
        vloc[13:14, 0:DB] = acca_ref[4:5, :]
        vloc[13:14, DB:] = acca_ref[5:6, :]
        vloc[14:15, 0:DB] = acca_ref[6:7, :]
        vloc[14:15, DB:] = acca_ref[2:3, :]
        vloc[15:16, 0:DB] = acca_ref[3:4, :]
        vloc[16:17, :] = accf_ref[6:7, :]
        vloc[17:18, :] = accf_ref[2:3, :]
        vloc[_CW_ROW:_CW_ROW + HALO, 0:DB] = acccw_ref[...]
        vbuf[idx] = vloc[...]
        rows_of = lambda t: pl.ds(pl.multiple_of(t * CH, CH), CH)
        wbuf[0] = dws_ref[rows_of(idx), :]
        sm = []
        for k in range(1, NDEV):
            t = idx ^ k
            sm.append(_small_copy(vloc, vbuf.at[idx], s1, r1, k - 1, _dev(t)))
            sm.append(_small_copy(dws_ref.at[rows_of(t)], wbuf.at[k], s2, r2, k - 1, _dev(t)))
        for cp in sm:
            cp.start()
        for k in range(1, NDEV):
            _small_copy(dws_ref.at[rows_of(0)], wbuf.at[k], s2, r2, k - 1, me).wait_recv()
        ws = wbuf[0]
        for k in range(1, NDEV):
            ws = ws + wbuf[k]
        wown[...] = ws
        wssum_ref[rows_of(idx), :] = ws
        ag = [_small_copy(wown, wssum_ref.at[rows_of(idx)], s3, r3, k - 1, _dev(idx ^ k)) for k in range(1, NDEV)]
        for cp in ag:
            cp.start()
        for k in range(1, NDEV):
            _small_copy(vloc, vbuf.at[idx ^ k], s1, r1, k - 1, me).wait_recv()
        vs = vbuf[0]
        for d in range(1, NDEV):
            vs = vs + vbuf[d]
        vsum_ref[...] = vs
        for d in range(NDEV):
            dcond_ref[d] = vbuf[d, 0:8, :]
        for k in range(1, NDEV):
            _small_copy(wown, wssum_ref.at[rows_of(idx ^ k)], s3, r3, k - 1, me).wait_recv()
        for cp in sm + ag:
            cp.wait_send()
        rs.finish()

    dma7 = pltpu.SemaphoreType.DMA((NDEV - 1,))
    outs = pl.pallas_call(
        body,
        name="reduce_grads",
        in_specs=[VM] * 7 + [HBM] * nw,
        out_specs=[VM, VM, VM] + [HBM] * nw,
        out_shape=[jax.ShapeDtypeStruct((NVEC, D), F32), jax.ShapeDtypeStruct((NDEV, 8, D), F32),
                   jax.ShapeDtypeStruct(dws.shape, F32)]
                  + [jax.ShapeDtypeStruct(g.shape, g.dtype) for g in grads],
        scratch_shapes=[pltpu.VMEM((NVEC, D), F32), pltpu.VMEM((NDEV, NVEC, D), F32),
                        pltpu.VMEM((NDEV, CH, CH), F32), pltpu.VMEM((CH, CH), F32)] + RS_SEMS(nw) + [dma7] * 6,
        compiler_params=pltpu.CompilerParams(vmem_limit_bytes=VMEM_LIMIT),
    )(acc_f, acc_v, acc_b, acc_a, acc_bs, acc_cw, dws, *grads)
    return outs[0], outs[1], outs[2], outs[3:]


def _adamw(w, g, m, v):
    m2 = ADAM_B1 * m + (1.0 - ADAM_B1) * g
    v2 = ADAM_B2 * v + (1.0 - ADAM_B2) * (g * g)
    m_hat = m2 / (1.0 - ADAM_B1 ** ADAM_STEP)
    v_hat = v2 / (1.0 - ADAM_B2 ** ADAM_STEP)
    delta = -ADAM_LR * (m_hat / (jnp.sqrt(v_hat) + ADAM_EPS) + ADAM_WD * w)
    return delta, m2, v2


def _adam_big(r, w, m, v, rb, name):
    R, C = w.shape

    def body(r_ref, w_ref, m_ref, v_ref, g_ref, d_ref, m2_ref, v2_ref):
        g = r_ref[0].astype(F32)
        for k in range(1, NDEV):
            g = g + r_ref[k].astype(F32)
        g_ref[...] = g
        d_ref[...], m2_ref[...], v2_ref[...] = _adamw(w_ref[...], g, m_ref[...], v_ref[...])

    t2 = pl.BlockSpec((rb, C), lambda i: (i, 0))
    sd = jax.ShapeDtypeStruct((R, C), F32)
    return pl.pallas_call(
        body, name=name, grid=(R // rb,),
        in_specs=[pl.BlockSpec((NDEV, rb, C), lambda i: (0, i, 0)), t2, t2, t2],
        out_specs=[t2, t2, t2, t2], out_shape=[sd, sd, sd, sd],
        compiler_params=pltpu.CompilerParams(dimension_semantics=("arbitrary",), vmem_limit_bytes=VMEM_LIMIT),
    )(r, w, m, v)


def _adam_ada(cact_t, dcs, w, m, v, rb, name):
    R, C = w.shape

    def body(ct_ref, dc_ref, w_ref, m_ref, v_ref, g_ref, d_ref, m2_ref, v2_ref):
        g = jnp.dot(ct_ref[...], dc_ref[...], preferred_element_type=F32, precision=lax.Precision.HIGHEST)
        g_ref[...] = g
        d_ref[...], m2_ref[...], v2_ref[...] = _adamw(w_ref[...], g, m_ref[...], v_ref[...])

    t2 = pl.BlockSpec((rb, C), lambda i: (i, 0))
    sd = jax.ShapeDtypeStruct((R, C), F32)
    return pl.pallas_call(
        body, name=name, grid=(R // rb,),
        in_specs=[pl.BlockSpec((rb, NDEV), lambda i: (i, 0)), _full((NDEV, C)), t2, t2, t2],
        out_specs=[t2, t2, t2, t2], out_shape=[sd, sd, sd, sd],
        compiler_params=pltpu.CompilerParams(dimension_semantics=("arbitrary",), vmem_limit_bytes=VMEM_LIMIT),
    )(cact_t, dcs, w, m, v)


_SMALL = ["ada_b", "ada_f_b", "norm1_g", "b_in", "a_ln_g", "a_ln_b", "a_spatial_b", "b_conv_b", "b_gn_g", "b_gn_b",
          "out_norm_a_g", "out_norm_b_g", "norm2_g", "norm_f_g", "a_spatial_w", "b_conv_w"]


def _adam_small(vsum, wssum, gcw, params):
    names = _SMALL
    flat = []
    for n in names:
        flat += list(params[n])

    def body(vs_ref, ws_ref, gcw_ref, *rest):
        ins = rest[:3 * len(names)]
        outs = rest[3 * len(names):]
        for pi, n in enumerate(names):
            w_ref, m_ref, v_ref = ins[3 * pi:3 * pi + 3]
            g_ref, d_ref, m2_ref, v2_ref = outs[4 * pi:4 * pi + 4]
            if n in ("ada_b", "ada_f_b", "b_in"):
                row0 = {"ada_b": 0, "ada_f_b": 6, "b_in": 9}[n]
                pieces = [(vs_ref[row0 + r:row0 + r + 1, :], slice(r * D, (r + 1) * D))
                          for r in range(w_ref.shape[1] // D)]
            elif n == "a_spatial_w":
                pieces = [(ws_ref[...], slice(None))]
            elif n == "b_conv_w":
                pieces = [(gcw_ref[...], slice(None))]
            else:
                row, off, width = _VEC_AT[n]
                pieces = [(vs_ref[row:row + 1, off:off + width], slice(None))]
            for g, cs in pieces:
                g_ref[:, cs] = g
                d_ref[:, cs], m2_ref[:, cs], v2_ref[:, cs] = _adamw(w_ref[:, cs], g, m_ref[:, cs], v_ref[:, cs])

    out_shape = []
    for n in names:
        out_shape += [jax.ShapeDtypeStruct(params[n][0].shape, F32)] * 4
    outs = pl.pallas_call(
        body, name="adam_small",
        in_specs=[VM] * (3 + len(flat)), out_specs=[VM] * len(out_shape), out_shape=out_shape,
        compiler_params=pltpu.CompilerParams(vmem_limit_bytes=VMEM_LIMIT),
    )(vsum, wssum, gcw, *flat)
    return {n: outs[4 * pi:4 * pi + 4] for pi, n in enumerate(names)}


def _token_tile(T, want):
    return want if T % want == 0 else T


def kernel(x, c, ada_w, ada_b, norm1_g, w_in, b_in, a_ln_g, a_ln_b, a_spatial_w, a_spatial_b, b_conv_w, b_conv_b, b_gn_g, b_gn_b, out_norm_a_g, out_norm_b_g, w_out, norm2_g, w_ffn_in, w_ffn_out, ada_f_w, ada_f_b, norm_f_g, loss_target, m_ada_w, m_ada_b, m_norm1_g, m_w_in, m_b_in, m_a_ln_g, m_a_ln_b, m_a_spatial_w, m_a_spatial_b, m_b_conv_w, m_b_conv_b, m_b_gn_g, m_b_gn_b, m_out_norm_a_g, m_out_norm_b_g, m_w_out, m_norm2_g, m_w_ffn_in, m_w_ffn_out, m_ada_f_w, m_ada_f_b, m_norm_f_g, v_ada_w, v_ada_b, v_norm1_g, v_w_in, v_b_in, v_a_ln_g, v_a_ln_b, v_a_spatial_w, v_a_spatial_b, v_b_conv_w, v_b_conv_b, v_b_gn_g, v_b_gn_b, v_out_norm_a_g, v_out_norm_b_g, v_w_out, v_norm2_g, v_w_ffn_in, v_w_ffn_out, v_ada_f_w, v_ada_f_b, v_norm_f_g):
    T = x.shape[1]
    idx = 4 * lax.axis_index("x") + 2 * lax.axis_index("y") + lax.axis_index("c")
    x2d = x.reshape(T, D)
    tgt = loss_target.reshape(T, D)

    conv_s = jnp.pad(b_conv_w[0], ((0, HALO - KW), (0, 0)))
    call, cparts, cfparts, convg, (win_g, wout_g) = _gather(
        c, ada_w[0], ada_b.reshape(NDEV, -1), ada_f_w, ada_f_b.reshape(NDEV, -1), conv_s,
        [w_in[0].astype(BF16), w_out[0].astype(BF16)])
    mod = jnp.concatenate([cparts.reshape(6, D), cfparts.reshape(2, D)], axis=0)
    wout = wout_g.reshape(D, D)
    cw = jnp.transpose(convg, (1, 0, 2)).reshape(HALO, DB)

    tril = jnp.tril(jnp.ones((CH, CH), dtype=bool))
    wsm = jnp.where(tril[None], a_spatial_w[0], 0.0).astype(BF16)
    wcat = wsm.reshape(NH * CH, CH)
    wcat_t = jnp.transpose(wsm, (0, 2, 1)).reshape(NH * CH, CH)
    bsf = jnp.repeat(a_spatial_b[0].T, DA // NH, axis=1)
    lane = jnp.arange(DB)
    pm = jnp.where((lane[:, None] >> 6) == (lane[None, :] >> 6), 1.0 / 64.0, 0.0).astype(BF16)
    esel = jnp.where((lane[:, None] >> 6) == jnp.arange(CH)[None, :], 1.0, 0.0).astype(BF16)

    tm = _token_tile(T, 256)
    tk = _token_tile(T, 1024)
    (x1, hb, z, mixed, yc, yb, o), (wfi_g, wfo_g) = _mix_fwd(
        x2d, mod, norm1_g, win_g, b_in, a_ln_g, a_ln_b, wcat, bsf, cw, b_conv_b, b_gn_g, b_gn_b, out_norm_a_g,
        out_norm_b_g, wout, pm, [w_ffn_in[0].astype(BF16), w_ffn_out[0].astype(BF16)], tm)
    dx1, h2b, dgu, act, dxg, acc_f = _ffn(x1, tgt, mod, norm2_g, norm_f_g.reshape(1, D), wfi_g,
                                          wfo_g.reshape(DFF, D), tm)
    g_wfi = _wgrad_b_blocked(h2b, dgu, 2, tk, "wgrad_ffn_in")
    g_wfo = _wgrad_a_blocked(act, dxg, tk, "wgrad_ffn_out").reshape(NDEV, DFF // NDEV, D)
    (gx, acc_v, acc_b, acc_a, acc_bs, acc_ws, acc_cw), (r_wfi, r_wfo), (g_win, g_wout) = _mix_bwd(
        dx1, x2d, z, mixed, yc, o, hb, yb, mod, norm1_g, win_g, a_ln_g, a_ln_b, wcat, wcat_t, cw, b_gn_g, b_gn_b,
        out_norm_a_g, out_norm_b_g, wout, pm, esel, [g_wfi, g_wfo], tm)
    g_wout = g_wout.reshape(NDEV, D // NDEV, D)

    vsum, dcond_all, wssum, (r_win, r_wout) = _reduce(acc_f, acc_v, acc_b, acc_a, acc_bs, acc_cw, acc_ws,
                                                      [g_win, g_wout])

    res = {}
    res["w_in"] = _adam_big(r_win, w_in[0], m_w_in[0], v_w_in[0], 256, "adam_w_in")
    res["w_out"] = _adam_big(r_wout, w_out[0], m_w_out[0], v_w_out[0], D // NDEV, "adam_w_out")
    res["w_ffn_in"] = _adam_big(r_wfi, w_ffn_in[0], m_w_ffn_in[0], v_w_ffn_in[0], 256, "adam_w_ffn_in")
    res["w_ffn_out"] = _adam_big(r_wfo, w_ffn_out[0], m_w_ffn_out[0], v_w_ffn_out[0], DFF // NDEV // 2,
                                 "adam_w_ffn_out")
    cact_t = (call * jax.nn.sigmoid(call)).T
    dcond = dcond_all.reshape(NDEV, 8 * D)
    nada = ada_w.shape[2]
    nadf = ada_f_w.shape[1]
    dcs = lax.dynamic_slice(dcond, (0, idx * nada), (NDEV, nada))
    dcfs = lax.dynamic_slice(dcond, (0, 6 * D + idx * nadf), (NDEV, nadf))
    res["ada_w"] = _adam_ada(cact_t, dcs, ada_w[0], m_ada_w[0], v_ada_w[0], 256, "adam_ada_w")
    res["ada_f_w"] = _adam_ada(cact_t, dcfs, ada_f_w, m_ada_f_w, v_ada_f_w, 256, "adam_ada_f_w")
    ncw = b_conv_w.shape[2]
    gcw = lax.dynamic_slice(vsum, (_CW_ROW, idx * ncw), (KW, ncw))
    two = lambda a: a.reshape(1, -1) if a.ndim == 1 else a.reshape(-1, a.shape[-1])
    small_in = {
        "ada_b": (ada_b, m_ada_b, v_ada_b), "ada_f_b": (ada_f_b, m_ada_f_b, v_ada_f_b),
        "norm1_g": (norm1_g, m_norm1_g, v_norm1_g), "b_in": (b_in, m_b_in, v_b_in),
        "a_ln_g": (a_ln_g, m_a_ln_g, v_a_ln_g), "a_ln_b": (a_ln_b, m_a_ln_b, v_a_ln_b),
        "a_spatial_b": (a_spatial_b.reshape(1, D), m_a_spatial_b.reshape(1, D), v_a_spatial_b.reshape(1, D)),
        "b_conv_b": (b_conv_b, m_b_conv_b, v_b_conv_b), "b_gn_g": (b_gn_g, m_b_gn_g, v_b_gn_g),
        "b_gn_b": (b_gn_b, m_b_gn_b, v_b_gn_b), "out_norm_a_g": (out_norm_a_g, m_out_norm_a_g, v_out_norm_a_g),
        "out_norm_b_g": (out_norm_b_g, m_out_norm_b_g, v_out_norm_b_g),
        "norm2_g": (norm2_g, m_norm2_g, v_norm2_g), "norm_f_g": (norm_f_g, m_norm_f_g, v_norm_f_g),
        "a_spatial_w": (a_spatial_w, m_a_spatial_w, v_a_spatial_w),
        "b_conv_w": (b_conv_w[0], m_b_conv_w[0], v_b_conv_w[0]),
    }
    small_in = {n: tuple(two(a) for a in t) for n, t in small_in.items()}
    res.update(_adam_small(vsum, wssum, gcw, small_in))

    loss = lax.psum(0.5 / D * jnp.sum(acc_f[7]), ("x", "y", "c"))
    shapes = {"ada_w": ada_w, "ada_b": ada_b, "norm1_g": norm1_g, "w_in": w_in, "b_in": b_in, "a_ln_g": a_ln_g,
              "a_ln_b": a_ln_b, "a_spatial_w": a_spatial_w, "a_spatial_b": a_spatial_b, "b_conv_w": b_conv_w,
              "b_conv_b": b_conv_b, "b_gn_g": b_gn_g, "b_gn_b": b_gn_b, "out_norm_a_g": out_norm_a_g,
              "out_norm_b_g": out_norm_b_g, "w_out": w_out, "norm2_g": norm2_g, "w_ffn_in": w_ffn_in,
              "w_ffn_out": w_ffn_out, "ada_f_w": ada_f_w, "ada_f_b": ada_f_b, "norm_f_g": norm_f_g}
    order = list(shapes)
    outs = [loss, gx.reshape(x.shape)]
    for which in range(4):
        outs += [res[n][which].reshape(shapes[n].shape) for n in order]
    return tuple(outs)
```

```python
import math

import jax
import jax.numpy as jnp
from jax import lax
from jax.experimental import pallas as pl
from jax.experimental.pallas import tpu as pltpu

F32 = jnp.float32
BF16 = jnp.bfloat16

D = 1024
DA = 512
DB = 512
DIN = 2048
DFF = 2816
NH = 8
CH = 128
KW = 31
HALO = 32
NDEV = 8
WIN_B = DIN // NDEV
WFI_B = 2 * DFF // NDEV
NFB = DFF // WFI_B
EPS = 1e-6
NVEC = 56
VMEM_LIMIT = 56 * 1024 * 1024

ADAM_LR, ADAM_B1, ADAM_B2, ADAM_EPS, ADAM_WD, ADAM_STEP = 0.001, 0.9, 0.999, 1e-08, 0.01, 10

MESH = pl.DeviceIdType.MESH


def _dot(a, b):
    return jnp.dot(a, b, preferred_element_type=F32)


def _dot_nt(a, b):
    return lax.dot_general(a, b, (((1,), (1,)), ((), ())), preferred_element_type=F32)


def _dot_tn(a, b):
    return lax.dot_general(a, b, (((0,), (0,)), ((), ())), preferred_element_type=F32)


def _rs(v):
    return lax.rsqrt(jnp.mean(v * v, axis=-1, keepdims=True) + EPS)


def _sig(v):
    return 1.0 / (1.0 + jnp.exp(-v))


_INV_SQRT2 = 1.0 / math.sqrt(2.0)
_INV_SQRT2PI = 1.0 / math.sqrt(2.0 * math.pi)


def _gelu_parts(v):
    cdf = 0.5 * (1.0 + lax.erf(v * _INV_SQRT2))
    pdf = jnp.exp(-0.5 * v * v) * _INV_SQRT2PI
    return v * cdf, cdf + v * pdf


def _grp_mean(v, pm):
    hi = v.astype(BF16)
    lo = (v - hi.astype(F32)).astype(BF16)
    return _dot(hi, pm) + _dot(lo, pm)


def _colsum(v):
    return jnp.sum(v, axis=0, keepdims=True)


def _full(shape):
    nd = len(shape)
    return pl.BlockSpec(shape, lambda *_: (0,) * nd)


def _resident(shape):
    nd = len(shape)
    return pl.BlockSpec(shape, lambda *_: (0,) * nd, pipeline_mode=pl.Buffered(1))


HBM = pl.BlockSpec(memory_space=pl.ANY)
VM = pl.BlockSpec(memory_space=pltpu.VMEM)


SH_ROWS = HALO - 8


def _shifted_copies(buf, shbuf, tm):
    for b in range(1, 8):
        shbuf[b - 1] = buf[b:b + tm + SH_ROWS, :]


def _window(buf, shbuf, off, tm):
    a, b = divmod(off, 8)
    if b == 0:
        return buf[8 * a:8 * a + tm, :]
    return shbuf[b - 1, 8 * a:8 * a + tm, :]


def _head_of_lane(rows):
    return lax.broadcasted_iota(jnp.int32, (rows, DA), 1) >> 6


def _block_pick(r, lane_head):
    out = jnp.zeros((CH, DA), F32)
    for h in range(NH):
        out = jnp.where(lane_head == h, r[h * CH:(h + 1) * CH, :], out)
    return out


def _place():
    x, y, c = lax.axis_index("x"), lax.axis_index("y"), lax.axis_index("c")
    return x, y, c, 4 * x + 2 * y + c


def _dev(t):
    return (t >> 2, (t >> 1) & 1, t & 1)


class _AllGather:
    def __init__(self, w_in, w_out, wss, wrs, lsem):
        x, y, c, idx = _place()
        me, sibling = (x, y, c), (x, y, 1 - c)
        chips = [(1 - x, y), (x, 1 - y), (1 - x, 1 - y)]
        nw = len(w_in)

        def blk(p):
            return 4 * p[0] + 2 * p[1] + p[2]

        def wcopy(a, k, block, to, src=None):
            dst = w_out[a].at[blk(block)]
            return pltpu.make_async_remote_copy(src_ref=dst if src is None else src, dst_ref=dst,
                                                send_sem=wss.at[a, k], recv_sem=wrs.at[a, k],
                                                device_id=to, device_id_type=MESH)

        self.mine = [pltpu.make_async_copy(w_in[a], w_out[a].at[idx], lsem.at[a]) for a in range(nw)]
        self.first = []
        for a in range(nw):
            self.first.append(wcopy(a, 0, me, sibling, src=w_in[a]))
            self.first += [wcopy(a, 1 + j, me, (*chip, c), src=w_in[a]) for j, chip in enumerate(chips)]
        self.landed = [[wcopy(a, 1 + j, (*chip, c), me) for a in range(nw)] for j, chip in enumerate(chips)]
        self.passed = [[wcopy(a, 4 + j, (*chip, c), sibling) for a in range(nw)] for j, chip in enumerate(chips)]
        self.from_sibling = []
        for a in range(nw):
            self.from_sibling.append(wcopy(a, 0, sibling, me))
            self.from_sibling += [wcopy(a, 4 + j, (*chip, 1 - c), me) for j, chip in enumerate(chips)]

    def start(self):
        for cp in self.mine + self.first:
            cp.start()

    def forward(self):
        for land, pas in zip(self.landed, self.passed):
            for l, p in zip(land, pas):
                l.wait_recv()
                p.start()

    def finish(self):
        for cp in self.from_sibling:
            cp.wait_recv()
        for cp in self.first:
            cp.wait_send()
        for pas in self.passed:
            for p in pas:
                p.wait_send()
        for cp in self.mine:
            cp.wait()


AG_SEMS = lambda nw: [pltpu.SemaphoreType.DMA((nw, 7)), pltpu.SemaphoreType.DMA((nw, 7)),
                      pltpu.SemaphoreType.DMA((nw,))]


class _ReduceScatter:
    def __init__(self, g_in, r_out, gss, grs, lsem):
        x, y, c, idx = _place()
        me = (x, y, c)
        nw = len(g_in)
        self.mine = [pltpu.make_async_copy(g_in[a].at[idx], r_out[a].at[0], lsem.at[a]) for a in range(nw)]
        self.sends, self.recvs = [], []
        for k in range(1, NDEV):
            t = idx ^ k
            for a in range(nw):
                self.sends.append(pltpu.make_async_remote_copy(
                    src_ref=g_in[a].at[t], dst_ref=r_out[a].at[k], send_sem=gss.at[a, k - 1],
                    recv_sem=grs.at[a, k - 1], device_id=_dev(t), device_id_type=MESH))
                self.recvs.append(pltpu.make_async_remote_copy(
                    src_ref=g_in[a].at[0], dst_ref=r_out[a].at[k], send_sem=gss.at[a, k - 1],
                    recv_sem=grs.at[a, k - 1], device_id=me, device_id_type=MESH))

    def start(self):
        for cp in self.mine + self.sends:
            cp.start()

    def finish(self):
        for cp in self.recvs:
            cp.wait_recv()
        for cp in self.sends:
            cp.wait_send()
        for cp in self.mine:
            cp.wait()


RS_SEMS = AG_SEMS


def _mix_fwd(x, mod, g1, win, b_in, lng, lnb, wcat, bsf, cw, cb, gng, gnb, oga, ogb, wout, pm, ffn_shards, tm):
    T = x.shape[0]
    nt = T // tm
    nch = tm // CH
    nw = len(ffn_shards)
    fwd_step = (5 * nt) // 8
    saved = [(D, F32), (D, BF16), (2 * DB, F32), (DA, F32), (D, BF16), (D, F32), (DA, F32), (DA, F32), (DA, F32),
             (DA, F32), (CH, F32), (DB, F32), (DB, F32)]
    NSAVE = len(saved)

    def body(x_ref, mod_ref, g1_ref, win_ref, bin_ref, lng_ref, lnb_ref, wcat_ref, bsf_ref, cw_ref, cb_ref,
             gng_ref, gnb_ref, oga_ref, ogb_ref, wout_ref, pm_ref, *rest):
        sh_in = rest[:nw]
        (x1_ref, h_ref, zvg_ref, mixed_ref, y_ref, o_ref, gu_ref, dgu_ref, dgv_ref, vhat_ref, rsl_ref, yhat_ref,
         rsg_ref) = rest[nw:nw + NSAVE]
        sh_out = rest[nw + NSAVE:2 * nw + NSAVE]
        glbuf, shbuf, wss, wrs, lsem = rest[2 * nw + NSAVE:]
        i = pl.program_id(0)

        @pl.when(i == 0)
        def _():
            _AllGather(sh_in, sh_out, wss, wrs, lsem).start()

        xv = x_ref[...]
        shift1 = mod_ref[0:1, :]
        scale1 = mod_ref[1:2, :]
        gate1 = mod_ref[2:3, :]
        h = (xv * _rs(xv) * g1_ref[...]) * (1.0 + scale1) + shift1
        hb = h.astype(BF16)
        h_ref[...] = hb
        z = jnp.concatenate([_dot(hb, win_ref[j]) for j in range(NDEV)], axis=1) + bin_ref[...]
        zvg_ref[...] = z[:, 2 * DA:]
        gu, dgelu_u = _gelu_parts(z[:, 0:DA])
        gv, dgelu_v = _gelu_parts(z[:, DA:2 * DA])
        gu_ref[...] = gu
        dgu_ref[...] = dgelu_u
        dgv_ref[...] = dgelu_v
        xc = gv - jnp.mean(gv, axis=-1, keepdims=True)
        rsl = lax.rsqrt(jnp.mean(xc * xc, axis=-1, keepdims=True) + EPS)
        vhat = xc * rsl
        vhat_ref[...] = vhat
        rsl_ref[...] = jnp.broadcast_to(rsl, (tm, CH))
        vnb = (vhat * lng_ref[...] + lnb_ref[...]).astype(BF16)
        lane_head = _head_of_lane(CH)
        chunks = []
        for ci in range(nch):
            r = _dot(wcat_ref[...], vnb[ci * CH:(ci + 1) * CH, :])
            chunks.append(_block_pick(r, lane_head) + bsf_ref[...])
        mixed = jnp.concatenate(chunks, axis=0) if nch > 1 else chunks[0]
        mixed_ref[...] = mixed
        ya = gu * mixed
        gl = z[:, 2 * DA:2 * DA + DB] * _sig(z[:, 2 * DA + DB:])

        @pl.when(i == 0)
        def _():
            glbuf[0:HALO, :] = jnp.zeros((HALO, DB), F32)

        glbuf[HALO:HALO + tm, :] = gl
        _shifted_copies(glbuf, shbuf, tm)
        yc = jnp.zeros((tm, DB), F32) + cb_ref[...]
        for k in range(KW):
            yc = yc + cw_ref[k:k + 1, :] * _window(glbuf, shbuf, HALO - (KW - 1) + k, tm)
        glbuf[0:HALO, :] = gl[tm - HALO:, :]
        pmv = pm_ref[...]
        dc = yc - _grp_mean(yc, pmv)
        rsg = lax.rsqrt(_grp_mean(dc * dc, pmv) + EPS)
        yhat = dc * rsg
        yhat_ref[...] = yhat
        rsg_ref[...] = rsg
        yg = yhat * gng_ref[...] + gnb_ref[...]
        yb = yg * _sig(yg)
        na = ya * _rs(ya) * oga_ref[...]
        nb = yb * _rs(yb) * ogb_ref[...]
        yv = jnp.concatenate([na, nb], axis=1).astype(BF16)
        y_ref[...] = yv
        o = _dot(yv, wout_ref[...])
        o_ref[...] = o
        x1_ref[...] = xv + gate1 * o

        @pl.when(i == fwd_step)
        def _():
            _AllGather(sh_in, sh_out, wss, wrs, lsem).forward()

        @pl.when(i == nt - 1)
        def _():
            _AllGather(sh_in, sh_out, wss, wrs, lsem).finish()

    tile = lambda w: pl.BlockSpec((tm, w), lambda i: (i, 0))
    outs = pl.pallas_call(
        body,
        name="mix_fwd",
        grid=(nt,),
        in_specs=[tile(D), _full((8, D)), _full((1, D)), _resident((NDEV, D, WIN_B)), _full((1, DIN)),
                  _full((1, DA)), _full((1, DA)), _full((NH * CH, CH)), _full((CH, DA)), _full((HALO, DB)),
                  _full((1, DB)), _full((1, DB)), _full((1, DB)), _full((1, DA)), _full((1, DB)),
                  _resident((D, D)), _full((DB, DB))] + [HBM] * nw,
        out_specs=[tile(w) for w, _ in saved] + [HBM] * nw,
        out_shape=[jax.ShapeDtypeStruct((T, w), dt) for w, dt in saved]
                  + [jax.ShapeDtypeStruct((NDEV,) + s.shape, s.dtype) for s in ffn_shards],
        scratch_shapes=[pltpu.VMEM((HALO + tm, DB), F32), pltpu.VMEM((7, tm + SH_ROWS, DB), F32)] + AG_SEMS(nw),
        compiler_params=pltpu.CompilerParams(dimension_semantics=("arbitrary",), vmem_limit_bytes=VMEM_LIMIT),
    )(x, mod, g1, win, b_in, lng, lnb, wcat, bsf, cw, cb, gng, gnb, oga, ogb, wout, pm, *ffn_shards)
    return outs[:NSAVE], outs[NSAVE:]


def _ffn(x1, tgt, mod, g2, gf, wfi, wfo, tm):
    T = x1.shape[0]
    nt = T // tm

    def body(x1_ref, tgt_ref, mod_ref, g2_ref, gf_ref, wfi_ref, wfo_ref,
             dx1_ref, h2_ref, dgu_ref, act_ref, dxg_ref, acc_ref, g_s, u_s):
        i = pl.program_id(0)

        @pl.when(i == 0)
        def _():
            acc_ref[...] = jnp.zeros((8, D), F32)

        x1 = x1_ref[...]
        shift2 = mod_ref[3:4, :]
        scale2 = mod_ref[4:5, :]
        gate2 = mod_ref[5:6, :]
        shiftf = mod_ref[6:7, :]
        scalef = mod_ref[7:8, :]
        g2v = g2_ref[...]
        gfv = gf_ref[...]
        r2 = _rs(x1)
        xn2 = x1 * r2
        h2b = (xn2 * g2v * (1.0 + scale2) + shift2).astype(BF16)
        h2_ref[...] = h2b
        f = jnp.zeros((tm, D), F32)
        for j in range(NFB):
            g = _dot(h2b, wfi_ref[j])
            u = _dot(h2b, wfi_ref[NFB + j])
            g_s[j] = g
            u_s[j] = u
            actb = (g * _sig(g) * u).astype(BF16)
            act_ref[j] = actb
            f = f + _dot(actb, wfo_ref[j * WFI_B:(j + 1) * WFI_B, :])
        x2 = x1 + gate2 * f
        rf = _rs(x2)
        xnf = x2 * rf
        out = xnf * gfv * (1.0 + scalef) + shiftf
        e = out - tgt_ref[...]
        dout = e * (1.0 / D)
        acc_ref[7:8, :] += _colsum(e * e)
        acc_ref[0:1, :] += _colsum(dout)
        acc_ref[1:2, :] += _colsum(dout * xnf * gfv)
        acc_ref[2:3, :] += _colsum(dout * (1.0 + scalef) * xnf)
        dxnf = dout * (1.0 + scalef) * gfv
        dx2 = rf * (dxnf - xnf * jnp.mean(dxnf * xnf, axis=-1, keepdims=True))
        acc_ref[3:4, :] += _colsum(dx2 * f)
        dxgb = (dx2 * gate2).astype(BF16)
        dxg_ref[...] = dxgb
        dh2 = jnp.zeros((tm, D), F32)
        for j in range(NFB):
            dact = _dot_nt(dxgb, wfo_ref[j * WFI_B:(j + 1) * WFI_B, :])
            g = g_s[j]
            u = u_s[j]
            s = _sig(g)
            dgb = (dact * u * (s * (1.0 + g * (1.0 - s)))).astype(BF16)
            dub = (dact * (g * s)).astype(BF16)
            dgu_ref[j] = dgb
            dgu_ref[NFB + j] = dub
            dh2 = dh2 + _dot_nt(dgb, wfi_ref[j])
            dh2 = dh2 + _dot_nt(dub, wfi_ref[NFB + j])
        acc_ref[4:5, :] += _colsum(dh2)
        acc_ref[5:6, :] += _colsum(dh2 * xn2 * g2v)
        acc_ref[6:7, :] += _colsum(dh2 * (1.0 + scale2) * xn2)
        dxn2 = dh2 * (1.0 + scale2) * g2v
        dx1_ref[...] = dx2 + r2 * (dxn2 - xn2 * jnp.mean(dxn2 * xn2, axis=-1, keepdims=True))

    tile = lambda w: pl.BlockSpec((tm, w), lambda i: (i, 0))
    blocked = lambda n: pl.BlockSpec((n, tm, WFI_B), lambda i: (0, i, 0))
    return pl.pallas_call(
        body,
        name="ffn_fwd_bwd",
        grid=(nt,),
        in_specs=[tile(D), tile(D), _full((8, D)), _full((1, D)), _full((1, D)),
                  _resident((NDEV, D, WFI_B)), _resident((DFF, D))],
        out_specs=[tile(D), tile(D), blocked(NDEV), blocked(NFB), tile(D), _full((8, D))],
        out_shape=[jax.ShapeDtypeStruct((T, D), F32), jax.ShapeDtypeStruct((T, D), BF16),
                   jax.ShapeDtypeStruct((NDEV, T, WFI_B), BF16), jax.ShapeDtypeStruct((NFB, T, WFI_B), BF16),
                   jax.ShapeDtypeStruct((T, D), BF16), jax.ShapeDtypeStruct((8, D), F32)],
        scratch_shapes=[pltpu.VMEM((NFB, tm, WFI_B), F32), pltpu.VMEM((NFB, tm, WFI_B), F32)],
        compiler_params=pltpu.CompilerParams(dimension_semantics=("arbitrary",), vmem_limit_bytes=VMEM_LIMIT),
    )(x1, tgt, mod, g2, gf, wfi, wfo)


def _mix_bwd(dx1, x, zvg, mixed, o, hb, yb, gu, dgu, dgv, vhat, rslb, yhat, rsg, mod, g1, win, lng, lnb, wcat, wcat_t,
             cw, gng, gnb, oga, ogb, wout, pm, esel, ffn_grads, tm):
    T = x.shape[0]
    nt = T // tm
    nch = tm // CH
    nw = len(ffn_grads)
    WOB = 256

    def body(dx1_ref, x_ref, zvg_ref, mixed_ref, o_ref, hb_ref, yb_ref, gu_ref, dgu_ref, dgv_ref, vhat_ref, rsl_ref,
             yhat_ref, rsg_ref, mod_ref, g1_ref, win_ref, lng_ref, lnb_ref, wcat_ref, wcatt_ref, cw_ref, gng_ref,
             gnb_ref, oga_ref, ogb_ref, wout_ref, pm_ref, esel_ref, *rest):
        g_in = rest[:nw]
        gx_ref, accv_ref, accb_ref, acca_ref, accbs_ref, accws_ref, acccw_ref = rest[nw:nw + 7]
        r_out = rest[nw + 7:2 * nw + 7]
        gwin_ref, gwout_ref = rest[2 * nw + 7:2 * nw + 9]
        dycbuf, shbuf, bs_s, acc_win, acc_wout, st_win, st_wout, gss, grs, lsem = rest[2 * nw + 9:]
        i = pl.program_id(0)

        @pl.when(i == 0)
        def _():
            _ReduceScatter(g_in, r_out, gss, grs, lsem).start()
            acc_win[...] = jnp.zeros((NDEV, D, WIN_B), F32)
            acc_wout[...] = jnp.zeros((D, D), F32)
            accv_ref[...] = jnp.zeros((8, D), F32)
            accb_ref[...] = jnp.zeros((1, DIN), F32)
            acca_ref[...] = jnp.zeros((8, DA), F32)
            accws_ref[...] = jnp.zeros((NH * CH, CH), F32)
            acccw_ref[...] = jnp.zeros((HALO, DB), F32)
            bs_s[...] = jnp.zeros((CH, DA), F32)
            dycbuf[tm:tm + HALO, :] = jnp.zeros((HALO, DB), F32)

        shift1 = mod_ref[0:1, :]
        scale1 = mod_ref[1:2, :]
        gate1 = mod_ref[2:3, :]
        g1v = g1_ref[...]
        xv = x_ref[...]
        r1 = _rs(xv)
        xn1 = xv * r1
        val = zvg_ref[:, 0:DB]
        gate = zvg_ref[:, DB:]
        gu = gu_ref[...]
        dgelu_u = dgu_ref[...]
        dgelu_v = dgv_ref[...]
        vhat = vhat_ref[...]
        rsl = rsl_ref[:, 0:1]
        lngv = lng_ref[...]
        vnb = (vhat * lngv + lnb_ref[...]).astype(BF16)
        mixed = mixed_ref[...]
        ya = gu * mixed
        ra = _rs(ya)
        yan = ya * ra
        sgt = _sig(gate)
        gl = val * sgt
        pmv = pm_ref[...]
        rsg = rsg_ref[...]
        yhat = yhat_ref[...]
        gngv = gng_ref[...]
        yg = yhat * gngv + gnb_ref[...]
        sgy = _sig(yg)
        yb = yg * sgy
        rb = _rs(yb)
        ybn = yb * rb
        dx1 = dx1_ref[...]
        accv_ref[0:1, :] += _colsum(dx1 * o_ref[...])
        dogb = (dx1 * gate1).astype(BF16)
        acc_wout[...] += _dot_tn(yb_ref[...], dogb)
        dy = _dot_nt(dogb, wout_ref[...])
        dna = dy[:, 0:DA]
        dnb = dy[:, DA:]
        ogav = oga_ref[...]
        ogbv = ogb_ref[...]
        acca_ref[2:3, :] += _colsum(dna * yan)
        acca_ref[3:4, :] += _colsum(dnb * ybn)
        ta = dna * ogav
        dya = ra * (ta - yan * jnp.mean(ta * yan, axis=-1, keepdims=True))
        tb = dnb * ogbv
        dyb = rb * (tb - ybn * jnp.mean(tb * ybn, axis=-1, keepdims=True))
        dgu = dya * mixed
        dm = dya * gu
        lane_head = _head_of_lane(CH)
        dvn_chunks = []
        bs_acc = bs_s[...]
        for ci in range(nch):
            dmc = dm[ci * CH:(ci + 1) * CH, :]
            bs_acc = bs_acc + dmc
            dmcb = dmc.astype(BF16)
            dvn_chunks.append(_block_pick(_dot(wcatt_ref[...], dmcb), lane_head))
            zero = jnp.zeros((CH, DA), BF16)
            stack = jnp.concatenate([jnp.where(lane_head == h, dmcb, zero) for h in range(NH)], axis=0)
            accws_ref[...] += _dot_nt(stack, vnb[ci * CH:(ci + 1) * CH, :])
        bs_s[...] = bs_acc
        dvn = jnp.concatenate(dvn_chunks, axis=0) if nch > 1 else dvn_chunks[0]
        acca_ref[0:1, :] += _colsum(dvn * vhat)
        acca_ref[1:2, :] += _colsum(dvn)
        dvh = dvn * lngv
        dgv = rsl * (dvh - jnp.mean(dvh, axis=-1, keepdims=True)
                     - vhat * jnp.mean(dvh * vhat, axis=-1, keepdims=True))
        du = dgu * dgelu_u
        dv = dgv * dgelu_v
        dyg = dyb * (sgy * (1.0 + yg * (1.0 - sgy)))
        acca_ref[5:6, :] += _colsum(dyg * yhat)
        acca_ref[6:7, :] += _colsum(dyg)
        dyh = dyg * gngv
        dyc = rsg * (dyh - _grp_mean(dyh, pmv) - yhat * _grp_mean(dyh * yhat, pmv))
        acca_ref[4:5, :] += _colsum(dyc)
        dycbuf[0:tm, :] = dyc
        _shifted_copies(dycbuf, shbuf, tm)
        dgl = jnp.zeros((tm, DB), F32)
        for k in range(KW):
            win_k = _window(dycbuf, shbuf, KW - 1 - k, tm)
            dgl = dgl + cw_ref[k:k + 1, :] * win_k
            acccw_ref[k:k + 1, :] += _colsum(win_k * gl)
        dycbuf[tm:tm + HALO, :] = dyc[0:HALO, :]
        dval = dgl * sgt
        dgate = dgl * val * sgt * (1.0 - sgt)
        dz = jnp.concatenate([du, dv, dval, dgate], axis=1)
        accb_ref[...] += _colsum(dz)
        dzb = dz.astype(BF16)
        hbv = hb_ref[...]
        dh = jnp.zeros((tm, D), F32)
        for j in range(NDEV):
            dzj = dzb[:, j * WIN_B:(j + 1) * WIN_B]
            acc_win[j] += _dot_tn(hbv, dzj)
            dh = dh + _dot_nt(dzj, win_ref[j])
        accv_ref[1:2, :] += _colsum(dh)
        accv_ref[2:3, :] += _colsum(dh * xn1 * g1v)
        accv_ref[3:4, :] += _colsum(dh * (1.0 + scale1) * xn1)
        dxn1 = dh * (1.0 + scale1) * g1v
        gx_ref[...] = dx1 + r1 * (dxn1 - xn1 * jnp.mean(dxn1 * xn1, axis=-1, keepdims=True))

        @pl.when(i == nt - 1)
        def _():
            rows = lax.broadcasted_iota(jnp.int32, (NH * CH, CH), 0) & (CH - 1)
            cols = lax.broadcasted_iota(jnp.int32, (NH * CH, CH), 1)
            accws_ref[...] = jnp.where(cols <= rows, accws_ref[...], 0.0)
            bs = bs_s[...]
            hi = bs.astype(BF16)
            r1_ = bs - hi.astype(F32)
            mid = r1_.astype(BF16)
            lo = (r1_ - mid.astype(F32)).astype(BF16)
            ev = esel_ref[...]
            accbs_ref[...] = _dot(hi, ev) + _dot(mid, ev) + _dot(lo, ev)
            for j in range(NDEV):
                st_win[...] = acc_win[j].astype(BF16)
                pltpu.sync_copy(st_win, gwin_ref.at[j])
            for j in range(D // WOB):
                st_wout[...] = acc_wout[j * WOB:(j + 1) * WOB, :].astype(BF16)
                pltpu.sync_copy(st_wout, gwout_ref.at[pl.ds(j * WOB, WOB)])
            _ReduceScatter(g_in, r_out, gss, grs, lsem).finish()

    rev = lambda w: pl.BlockSpec((tm, w), lambda i: (nt - 1 - i, 0))
    outs = pl.pallas_call(
        body,
        name="mix_bwd",
        grid=(nt,),
        in_specs=[rev(D), rev(D), rev(2 * DB), rev(DA), rev(D), rev(D), rev(D), rev(DA), rev(DA), rev(DA), rev(DA),
                  rev(CH), rev(DB), rev(DB), _full((8, D)), _full((1, D)),
                  _resident((NDEV, D, WIN_B)), _full((1, DA)), _full((1, DA)), _full((NH * CH, CH)),
                  _full((NH * CH, CH)), _full((HALO, DB)), _full((1, DB)), _full((1, DB)), _full((1, DA)),
                  _full((1, DB)), _resident((D, D)), _full((DB, DB)), _full((DA, CH))] + [HBM] * nw,
        out_specs=[rev(D), _full((8, D)), _full((1, DIN)), _full((8, DA)), _full((CH, CH)),
                   _full((NH * CH, CH)), _full((HALO, DB))] + [HBM] * (nw + 2),
        out_shape=[jax.ShapeDtypeStruct((T, D), F32), jax.ShapeDtypeStruct((8, D), F32),
                   jax.ShapeDtypeStruct((1, DIN), F32), jax.ShapeDtypeStruct((8, DA), F32),
                   jax.ShapeDtypeStruct((CH, CH), F32), jax.ShapeDtypeStruct((NH * CH, CH), F32),
                   jax.ShapeDtypeStruct((HALO, DB), F32)]
                  + [jax.ShapeDtypeStruct(g.shape, g.dtype) for g in ffn_grads]
                  + [jax.ShapeDtypeStruct((NDEV, D, WIN_B), BF16), jax.ShapeDtypeStruct((D, D), BF16)],
        scratch_shapes=[pltpu.VMEM((tm + HALO, DB), F32), pltpu.VMEM((7, tm + SH_ROWS, DB), F32),
                        pltpu.VMEM((CH, DA), F32), pltpu.VMEM((NDEV, D, WIN_B), F32), pltpu.VMEM((D, D), F32),
                        pltpu.VMEM((D, WIN_B), BF16), pltpu.VMEM((WOB, D), BF16)] + RS_SEMS(nw),
        compiler_params=pltpu.CompilerParams(dimension_semantics=("arbitrary",), vmem_limit_bytes=VMEM_LIMIT),
    )(dx1, x, zvg, mixed, o, hb, yb, gu, dgu, dgv, vhat, rslb, yhat, rsg, mod, g1, win, lng, lnb, wcat, wcat_t, cw,
      gng, gnb, oga, ogb, wout, pm, esel, *ffn_grads)
    return outs[:7], outs[7:7 + nw], outs[7 + nw:]


def _wgrad_cols(a, b, nblk, tk, name):
    T, M = a.shape
    bw = b.shape[1] // nblk
    nk = T // tk

    def body(a_ref, b_ref, o_ref, acc):
        k = pl.program_id(0)

        @pl.when(k == 0)
        def _():
            acc[...] = jnp.zeros((nblk, M, bw), F32)

        av = a_ref[...]
        for j in range(nblk):
            acc[j] += _dot_tn(av, b_ref[:, j * bw:(j + 1) * bw])

        @pl.when(k == nk - 1)
        def _():
            o_ref[...] = acc[...].astype(BF16)

    return pl.pallas_call(
        body, name=name, grid=(nk,),
        in_specs=[pl.BlockSpec((tk, M), lambda k: (k, 0)), pl.BlockSpec((tk, nblk * bw), lambda k: (k, 0))],
        out_specs=_full((nblk, M, bw)),
        out_shape=jax.ShapeDtypeStruct((nblk, M, bw), BF16),
        scratch_shapes=[pltpu.VMEM((nblk, M, bw), F32)],
        compiler_params=pltpu.CompilerParams(dimension_semantics=("arbitrary",), vmem_limit_bytes=VMEM_LIMIT),
    )(a, b)


def _wgrad_b_blocked(a, b3, per, tk, name):
    T, M = a.shape
    nb, _, bw = b3.shape
    nk = T // tk

    def body(a_ref, b_ref, o_ref, acc):
        k = pl.program_id(1)

        @pl.when(k == 0)
        def _():
            acc[...] = jnp.zeros((per, M, bw), F32)

        av = a_ref[...]
        for j in range(per):
            acc[j] += _dot_tn(av, b_ref[j])

        @pl.when(k == nk - 1)
        def _():
            o_ref[...] = acc[...].astype(BF16)

    return pl.pallas_call(
        body, name=name, grid=(nb // per, nk),
        in_specs=[pl.BlockSpec((tk, M), lambda j, k: (k, 0)), pl.BlockSpec((per, tk, bw), lambda j, k: (j, k, 0))],
        out_specs=pl.BlockSpec((per, M, bw), lambda j, k: (j, 0, 0)),
        out_shape=jax.ShapeDtypeStruct((nb, M, bw), BF16),
        scratch_shapes=[pltpu.VMEM((per, M, bw), F32)],
        compiler_params=pltpu.CompilerParams(dimension_semantics=("arbitrary", "arbitrary"),
                                             vmem_limit_bytes=VMEM_LIMIT),
    )(a, b3)


def _wgrad_a_blocked(a3, b, tk, name):
    nb, T, bw = a3.shape
    N = b.shape[1]
    nk = T // tk

    def body(a_ref, b_ref, o_ref, acc):
        k = pl.program_id(1)

        @pl.when(k == 0)
        def _():
            acc[...] = jnp.zeros((bw, N), F32)

        acc[...] += _dot_tn(a_ref[0], b_ref[...])

        @pl.when(k == nk - 1)
        def _():
            o_ref[0] = acc[...].astype(BF16)

    return pl.pallas_call(
        body, name=name, grid=(nb, nk),
        in_specs=[pl.BlockSpec((1, tk, bw), lambda j, k: (j, k, 0)), pl.BlockSpec((tk, N), lambda j, k: (k, 0))],
        out_specs=pl.BlockSpec((1, bw, N), lambda j, k: (j, 0, 0)),
        out_shape=jax.ShapeDtypeStruct((nb, bw, N), BF16),
        scratch_shapes=[pltpu.VMEM((bw, N), F32)],
        compiler_params=pltpu.CompilerParams(dimension_semantics=("arbitrary", "arbitrary"),
                                             vmem_limit_bytes=VMEM_LIMIT),
    )(a3, b)


def _small_copy(src, dst, ss, rs, k, to):
    return pltpu.make_async_remote_copy(src_ref=src, dst_ref=dst, send_sem=ss.at[k], recv_sem=rs.at[k],
                                        device_id=to, device_id_type=MESH)


def _gather(c_row, ada_w, ada_b8, ada_f_w, ada_f_b8, conv_s, shards):
    nw = len(shards)

    def body(c_ref, adaw_ref, adab_ref, adafw_ref, adafb_ref, conv_ref, *rest):
        w_in = rest[:nw]
        call_ref, cparts_ref, cfparts_ref, convg_ref = rest[nw:nw + 4]
        w_out = rest[nw + 4:2 * nw + 4]
        part_s, partf_s, wss, wrs, lsem, s1, r1, s2, r2, s3, r3, s4, r4 = rest[2 * nw + 4:]
        x, y, c, idx = _place()
        me = (x, y, c)
        ag = _AllGather(w_in, w_out, wss, wrs, lsem)
        ag.start()
        call_ref[pl.ds(idx, 1), :] = c_ref[...]
        convg_ref[idx] = conv_ref[...]
        ph1 = []
        for k in range(1, NDEV):
            to = _dev(idx ^ k)
            ph1.append(_small_copy(c_ref, call_ref.at[pl.ds(idx, 1)], s1, r1, k - 1, to))
            ph1.append(_small_copy(conv_ref, convg_ref.at[idx], s2, r2, k - 1, to))
        for cp in ph1:
            cp.start()
        for k in range(1, NDEV):
            src_dev = idx ^ k
            _small_copy(c_ref, call_ref.at[pl.ds(src_dev, 1)], s1, r1, k - 1, me).wait_recv()
            _small_copy(conv_ref, convg_ref.at[src_dev], s2, r2, k - 1, me).wait_recv()
        call = call_ref[...]
        cact = (call * _sig(call))
        part_s[...] = jnp.dot(cact, adaw_ref[...], preferred_element_type=F32,
                              precision=lax.Precision.HIGHEST) + adab_ref[pl.ds(idx, 1), :]
        partf_s[...] = jnp.dot(cact, adafw_ref[...], preferred_element_type=F32,
                               precision=lax.Precision.HIGHEST) + adafb_ref[pl.ds(idx, 1), :]
        cparts_ref[pl.ds(idx, 1), :] = part_s[pl.ds(idx, 1), :]
        cfparts_ref[pl.ds(idx, 1), :] = partf_s[pl.ds(idx, 1), :]
        ph2 = []
        for k in range(1, NDEV):
            t = idx ^ k
            ph2.append(_small_copy(part_s.at[pl.ds(t, 1)], cparts_ref.at[pl.ds(idx, 1)], s3, r3, k - 1, _dev(t)))
            ph2.append(_small_copy(partf_s.at[pl.ds(t, 1)], cfparts_ref.at[pl.ds(idx, 1)], s4, r4, k - 1, _dev(t)))
        for cp in ph2:
            cp.start()
        for k in range(1, NDEV):
            src_dev = idx ^ k
            _small_copy(part_s.at[pl.ds(0, 1)], cparts_ref.at[pl.ds(src_dev, 1)], s3, r3, k - 1, me).wait_recv()
            _small_copy(partf_s.at[pl.ds(0, 1)], cfparts_ref.at[pl.ds(src_dev, 1)], s4, r4, k - 1, me).wait_recv()
        for cp in ph1 + ph2:
            cp.wait_send()
        ag.forward()
        ag.finish()

    dma7 = pltpu.SemaphoreType.DMA((NDEV - 1,))
    outs = pl.pallas_call(
        body,
        name="gather_weights",
        in_specs=[VM] * 6 + [HBM] * nw,
        out_specs=[VM] * 4 + [HBM] * nw,
        out_shape=[jax.ShapeDtypeStruct((NDEV, D), F32), jax.ShapeDtypeStruct((NDEV, ada_w.shape[1]), F32),
                   jax.ShapeDtypeStruct((NDEV, ada_f_w.shape[1]), F32),
                   jax.ShapeDtypeStruct((NDEV,) + conv_s.shape, F32)]
                  + [jax.ShapeDtypeStruct((NDEV,) + s.shape, s.dtype) for s in shards],
        scratch_shapes=[pltpu.VMEM((NDEV, ada_w.shape[1]), F32), pltpu.VMEM((NDEV, ada_f_w.shape[1]), F32)]
                       + AG_SEMS(nw) + [dma7] * 8,
        compiler_params=pltpu.CompilerParams(vmem_limit_bytes=VMEM_LIMIT),
    )(c_row, ada_w, ada_b8, ada_f_w, ada_f_b8, conv_s, *shards)
    return outs[0], outs[1], outs[2], outs[3], outs[4:]


_VEC_AT = {
    "norm1_g": (8, 0, D), "a_ln_g": (11, 0, DA), "a_ln_b": (11, DA, DA), "a_spatial_b": (12, 0, D),
    "b_conv_b": (13, 0, DB), "b_gn_g": (13, DB, DB), "b_gn_b": (14, 0, DB), "out_norm_a_g": (14, DB, DA),
    "out_norm_b_g": (15, 0, DB), "norm2_g": (16, 0, D), "norm_f_g": (17, 0, D),
}
_LOSS_ROW = 18
_CW_ROW = 24


def _reduce(acc_f, acc_v, acc_b, acc_a, acc_bs, acc_cw, dws, grads):
    nw = len(grads)

    def body(accf_ref, accv_ref, accb_ref, acca_ref, accbs_ref, acccw_ref, dws_ref, *rest):
        g_in = rest[:nw]
        vsum_ref, dcond_ref, wssum_ref = rest[nw:nw + 3]
        r_out = rest[nw + 3:2 * nw + 3]
        vloc, vbuf, wbuf, wown, gss, grs, lsem, s1, r1, s2, r2, s3, r3 = rest[2 * nw + 3:]
        x, y, c, idx = _place()
        me = (x, y, c)
        rs = _ReduceScatter(g_in, r_out, gss, grs, lsem)
        rs.start()
        vloc[...] = jnp.zeros((NVEC, D), F32)
        vloc[0:1, :] = accv_ref[1:2, :]
        vloc[1:2, :] = accv_ref[2:3, :]
        vloc[2:3, :] = accv_ref[0:1, :]
        vloc[3:4, :] = accf_ref[4:5, :]
        vloc[4:5, :] = accf_ref[5:6, :]
        vloc[5:6, :] = accf_ref[3:4, :]
        vloc[6:7, :] = accf_ref[0:1, :]
        vloc[7:8, :] = accf_ref[1:2, :]
        vloc[8:9, :] = accv_ref[3:4, :]
        vloc[9:10, :] = accb_ref[:, 0:D]
        vloc[10:11, :] = accb_ref[:, D:]
        vloc[11:12, 0:DA] = acca_ref[0:1, :]
        vloc[11:12, DA:] = acca_ref[1:2, :]
        bst = accbs_ref[...].T
        for h in range(NH):
            vloc[12:13, h * CH:(h + 1) * CH] = bst[h:h + 1, :]
        vloc[13:14, 0:DB] = acca_ref[4:5, :]
        vloc[13:14, DB:] = acca_ref[5:6, :]
        vloc[14:15, 0:DB] = acca_ref[6:7, :]
        vloc[14:15, DB:] = acca_ref[2:3, :]
        vloc[15:16, 0:DB] = acca_ref[3:4, :]
        vloc[16:17, :] = accf_ref[6:7, :]
        vloc[17:18, :] = accf_ref[2:3, :]
        vloc[_LOSS_ROW:_LOSS_ROW + 1, :] = accf_ref[7:8, :]
        vloc[_CW_ROW:_CW_ROW + HALO, 0:DB] = acccw_ref[...]
        vbuf[idx] = vloc[...]
        rows_of = lambda t: pl.ds(pl.multiple_of(t * CH, CH), CH)
        wbuf[0] = dws_ref[rows_of(idx), :]
        sm = []
        for k in range(1, NDEV):
            t = idx ^ k
            sm.append(_small_copy(vloc, vbuf.at[idx], s1, r1, k - 1, _dev(t)))
            sm.append(_small_copy(dws_ref.at[rows_of(t)], wbuf.at[k], s2, r2, k - 1, _dev(t)))
        for cp in sm:
            cp.start()
        for k in range(1, NDEV):
            _small_copy(dws_ref.at[rows_of(0)], wbuf.at[k], s2, r2, k - 1, me).wait_recv()
        ws = wbuf[0]
        for k in range(1, NDEV):
            ws = ws + wbuf[k]
        wown[...] = ws
        wssum_ref[rows_of(idx), :] = ws
        ag = [_small_copy(wown, wssum_ref.at[rows_of(idx)], s3, r3, k - 1, _dev(idx ^ k)) for k in range(1, NDEV)]
        for cp in ag:
            cp.start()
        for k in range(1, NDEV):
            _small_copy(vloc, vbuf.at[idx ^ k], s1, r1, k - 1, me).wait_recv()
        vs = vbuf[0]
        for d in range(1, NDEV):
            vs = vs + vbuf[d]
        vsum_ref[...] = vs
        for d in range(NDEV):
            dcond_ref[d] = vbuf[d, 0:8, :]
        for k in range(1, NDEV):
            _small_copy(wown, wssum_ref.at[rows_of(idx ^ k)], s3, r3, k - 1, me).wait_recv()
        for cp in sm + ag:
            cp.wait_send()
        rs.finish()

    dma7 = pltpu.SemaphoreType.DMA((NDEV - 1,))
    outs = pl.pallas_call(
        body,
        name="reduce_grads",
        in_specs=[VM] * 7 + [HBM] * nw,
        out_specs=[VM, VM, VM] + [HBM] * nw,
        out_shape=[jax.ShapeDtypeStruct((NVEC, D), F32), jax.ShapeDtypeStruct((NDEV, 8, D), F32),
                   jax.ShapeDtypeStruct(dws.shape, F32)]
                  + [jax.ShapeDtypeStruct(g.shape, g.dtype) for g in grads],
        scratch_shapes=[pltpu.VMEM((NVEC, D), F32), pltpu.VMEM((NDEV, NVEC, D), F32),
                        pltpu.VMEM((NDEV, CH, CH), F32), pltpu.VMEM((CH, CH), F32)] + RS_SEMS(nw) + [dma7] * 6,
        compiler_params=pltpu.CompilerParams(vmem_limit_bytes=VMEM_LIMIT),
    )(acc_f, acc_v, acc_b, acc_a, acc_bs, acc_cw, dws, *grads)
    return outs[0], outs[1], outs[2], outs[3:]


def _adamw(w, g, m, v):
    m2 = ADAM_B1 * m + (1.0 - ADAM_B1) * g
    v2 = ADAM_B2 * v + (1.0 - ADAM_B2) * (g * g)
    m_hat = m2 / (1.0 - ADAM_B1 ** ADAM_STEP)
    v_hat = v2 / (1.0 - ADAM_B2 ** ADAM_STEP)
    delta = -ADAM_LR * (m_hat / (jnp.sqrt(v_hat) + ADAM_EPS) + ADAM_WD * w)
    return delta, m2, v2


def _adam_big(r, w, m, v, rb, name):
    R, C = w.shape

    def body(r_ref, w_ref, m_ref, v_ref, g_ref, d_ref, m2_ref, v2_ref):
        g = r_ref[0].astype(F32)
        for k in range(1, NDEV):
            g = g + r_ref[k].astype(F32)
        g_ref[...] = g
        d_ref[...], m2_ref[...], v2_ref[...] = _adamw(w_ref[...], g, m_ref[...], v_ref[...])

    t2 = pl.BlockSpec((rb, C), lambda i: (i, 0))
    sd = jax.ShapeDtypeStruct((R, C), F32)
    return pl.pallas_call(
        body, name=name, grid=(R // rb,),
        in_specs=[pl.BlockSpec((NDEV, rb, C), lambda i: (0, i, 0)), t2, t2, t2],
        out_specs=[t2, t2, t2, t2], out_shape=[sd, sd, sd, sd],
        compiler_params=pltpu.CompilerParams(dimension_semantics=("arbitrary",), vmem_limit_bytes=VMEM_LIMIT),
    )(r, w, m, v)


def _adam_ada(cact_t, dcs, w, m, v, rb, name):
    R, C = w.shape

    def body(ct_ref, dc_ref, w_ref, m_ref, v_ref, g_ref, d_ref, m2_ref, v2_ref):
        g = jnp.dot(ct_ref[...], dc_ref[...], preferred_element_type=F32, precision=lax.Precision.HIGHEST)
        g_ref[...] = g
        d_ref[...], m2_ref[...], v2_ref[...] = _adamw(w_ref[...], g, m_ref[...], v_ref[...])

    t2 = pl.BlockSpec((rb, C), lambda i: (i, 0))
    sd = jax.ShapeDtypeStruct((R, C), F32)
    return pl.pallas_call(
        body, name=name, grid=(R // rb,),
        in_specs=[pl.BlockSpec((rb, NDEV), lambda i: (i, 0)), _full((NDEV, C)), t2, t2, t2],
        out_specs=[t2, t2, t2, t2], out_shape=[sd, sd, sd, sd],
        compiler_params=pltpu.CompilerParams(dimension_semantics=("arbitrary",), vmem_limit_bytes=VMEM_LIMIT),
    )(cact_t, dcs, w, m, v)


_SMALL = ["ada_b", "ada_f_b", "norm1_g", "b_in", "a_ln_g", "a_ln_b", "a_spatial_b", "b_conv_b", "b_gn_g", "b_gn_b",
          "out_norm_a_g", "out_norm_b_g", "norm2_g", "norm_f_g", "a_spatial_w", "b_conv_w"]


def _adam_small(vsum, wssum, gcw, params):
    names = _SMALL
    flat = []
    for n in names:
        flat += list(params[n])

    def body(vs_ref, ws_ref, gcw_ref, *rest):
        ins = rest[:3 * len(names)]
        outs = rest[3 * len(names):]
        for pi, n in enumerate(names):
            w_ref, m_ref, v_ref = ins[3 * pi:3 * pi + 3]
            g_ref, d_ref, m2_ref, v2_ref = outs[4 * pi:4 * pi + 4]
            if n in ("ada_b", "ada_f_b", "b_in"):
                row0 = {"ada_b": 0, "ada_f_b": 6, "b_in": 9}[n]
                pieces = [(vs_ref[row0 + r:row0 + r + 1, :], slice(r * D, (r + 1) * D))
                          for r in range(w_ref.shape[1] // D)]
            elif n == "a_spatial_w":
                pieces = [(ws_ref[...], slice(None))]
            elif n == "b_conv_w":
                pieces = [(gcw_ref[...], slice(None))]
            else:
                row, off, width = _VEC_AT[n]
                pieces = [(vs_ref[row:row + 1, off:off + width], slice(None))]
            for g, cs in pieces:
                g_ref[:, cs] = g
                d_ref[:, cs], m2_ref[:, cs], v2_ref[:, cs] = _adamw(w_ref[:, cs], g, m_ref[:, cs], v_ref[:, cs])

    out_shape = []
    for n in names:
        out_shape += [jax.ShapeDtypeStruct(params[n][0].shape, F32)] * 4
    outs = pl.pallas_call(
        body, name="adam_small",
        in_specs=[VM] * (3 + len(flat)), out_specs=[VM] * len(out_shape), out_shape=out_shape,
        compiler_params=pltpu.CompilerParams(vmem_limit_bytes=VMEM_LIMIT),
    )(vsum, wssum, gcw, *flat)
    return {n: outs[4 * pi:4 * pi + 4] for pi, n in enumerate(names)}


def _token_tile(T, want):
    return want if T % want == 0 else T


def kernel(x, c, ada_w, ada_b, norm1_g, w_in, b_in, a_ln_g, a_ln_b, a_spatial_w, a_spatial_b, b_conv_w, b_conv_b, b_gn_g, b_gn_b, out_norm_a_g, out_norm_b_g, w_out, norm2_g, w_ffn_in, w_ffn_out, ada_f_w, ada_f_b, norm_f_g, loss_target, m_ada_w, m_ada_b, m_norm1_g, m_w_in, m_b_in, m_a_ln_g, m_a_ln_b, m_a_spatial_w, m_a_spatial_b, m_b_conv_w, m_b_conv_b, m_b_gn_g, m_b_gn_b, m_out_norm_a_g, m_out_norm_b_g, m_w_out, m_norm2_g, m_w_ffn_in, m_w_ffn_out, m_ada_f_w, m_ada_f_b, m_norm_f_g, v_ada_w, v_ada_b, v_norm1_g, v_w_in, v_b_in, v_a_ln_g, v_a_ln_b, v_a_spatial_w, v_a_spatial_b, v_b_conv_w, v_b_conv_b, v_b_gn_g, v_b_gn_b, v_out_norm_a_g, v_out_norm_b_g, v_w_out, v_norm2_g, v_w_ffn_in, v_w_ffn_out, v_ada_f_w, v_ada_f_b, v_norm_f_g):
    T = x.shape[1]
    idx = 4 * lax.axis_index("x") + 2 * lax.axis_index("y") + lax.axis_index("c")
    x2d = x.reshape(T, D)
    tgt = loss_target.reshape(T, D)

    conv_s = jnp.pad(b_conv_w[0], ((0, HALO - KW), (0, 0)))
    call, cparts, cfparts, convg, (win_g, wout_g) = _gather(
        c, ada_w[0], ada_b.reshape(NDEV, -1), ada_f_w, ada_f_b.reshape(NDEV, -1), conv_s,
        [w_in[0].astype(BF16), w_out[0].astype(BF16)])
    mod = jnp.concatenate([cparts.reshape(6, D), cfparts.reshape(2, D)], axis=0)
    wout = wout_g.reshape(D, D)
    cw = jnp.transpose(convg, (1, 0, 2)).reshape(HALO, DB)

    tril = jnp.tril(jnp.ones((CH, CH), dtype=bool))
    wsm = jnp.where(tril[None], a_spatial_w[0], 0.0).astype(BF16)
    wcat = wsm.reshape(NH * CH, CH)
    wcat_t = jnp.transpose(wsm, (0, 2, 1)).reshape(NH * CH, CH)
    bsf = jnp.repeat(a_spatial_b[0].T, DA // NH, axis=1)
    lane = jnp.arange(DB)
    pm = jnp.where((lane[:, None] >> 6) == (lane[None, :] >> 6), 1.0 / 64.0, 0.0).astype(BF16)
    esel = jnp.where((lane[:, None] >> 6) == jnp.arange(CH)[None, :], 1.0, 0.0).astype(BF16)

    tm = _token_tile(T, 256)
    tk = _token_tile(T, 1024)
    (x1, hb, zvg, mixed, yb, o, gu, dgelu_u, dgelu_v, vhat, rslb, yhat, rsg), (wfi_g, wfo_g) = _mix_fwd(
        x2d, mod, norm1_g, win_g, b_in, a_ln_g, a_ln_b, wcat, bsf, cw, b_conv_b, b_gn_g, b_gn_b, out_norm_a_g,
        out_norm_b_g, wout, pm, [w_ffn_in[0].astype(BF16), w_ffn_out[0].astype(BF16)], _token_tile(T, 512))
    dx1, h2b, dgu, act, dxg, acc_f = _ffn(x1, tgt, mod, norm2_g, norm_f_g.reshape(1, D), wfi_g,
                                          wfo_g.reshape(DFF, D), tm)
    g_wfi = _wgrad_b_blocked(h2b, dgu, 2, tk, "wgrad_ffn_in")
    g_wfo = _wgrad_a_blocked(act, dxg, tk, "wgrad_ffn_out").reshape(NDEV, DFF // NDEV, D)
    (gx, acc_v, acc_b, acc_a, acc_bs, acc_ws, acc_cw), (r_wfi, r_wfo), (g_win, g_wout) = _mix_bwd(
        dx1, x2d, zvg, mixed, o, hb, yb, gu, dgelu_u, dgelu_v, vhat, rslb, yhat, rsg, mod, norm1_g, win_g, a_ln_g,
        a_ln_b, wcat, wcat_t, cw, b_gn_g, b_gn_b, out_norm_a_g, out_norm_b_g, wout, pm, esel, [g_wfi, g_wfo], tm)
    g_wout = g_wout.reshape(NDEV, D // NDEV, D)

    vsum, dcond_all, wssum, (r_win, r_wout) = _reduce(acc_f, acc_v, acc_b, acc_a, acc_bs, acc_cw, acc_ws,
                                                      [g_win, g_wout])

    res = {}
    res["w_in"] = _adam_big(r_win, w_in[0], m_w_in[0], v_w_in[0], 256, "adam_w_in")
    res["w_out"] = _adam_big(r_wout, w_out[0], m_w_out[0], v_w_out[0], D // NDEV, "adam_w_out")
    res["w_ffn_in"] = _adam_big(r_wfi, w_ffn_in[0], m_w_ffn_in[0], v_w_ffn_in[0], 256, "adam_w_ffn_in")
    res["w_ffn_out"] = _adam_big(r_wfo, w_ffn_out[0], m_w_ffn_out[0], v_w_ffn_out[0], DFF // NDEV // 2,
                                 "adam_w_ffn_out")
    cact_t = (call * jax.nn.sigmoid(call)).T
    dcond = dcond_all.reshape(NDEV, 8 * D)
    nada = ada_w.shape[2]
    nadf = ada_f_w.shape[1]
    dcs = lax.dynamic_slice(dcond, (0, idx * nada), (NDEV, nada))
    dcfs = lax.dynamic_slice(dcond, (0, 6 * D + idx * nadf), (NDEV, nadf))
    res["ada_w"] = _adam_ada(cact_t, dcs, ada_w[0], m_ada_w[0], v_ada_w[0], 256, "adam_ada_w")
    res["ada_f_w"] = _adam_ada(cact_t, dcfs, ada_f_w, m_ada_f_w, v_ada_f_w, 256, "adam_ada_f_w")
    ncw = b_conv_w.shape[2]
    gcw = lax.dynamic_slice(vsum, (_CW_ROW, idx * ncw), (KW, ncw))
    two = lambda a: a.reshape(1, -1) if a.ndim == 1 else a.reshape(-1, a.shape[-1])
    small_in = {
        "ada_b": (ada_b, m_ada_b, v_ada_b), "ada_f_b": (ada_f_b, m_ada_f_b, v_ada_f_b),
        "norm1_g": (norm1_g, m_norm1_g, v_norm1_g), "b_in": (b_in, m_b_in, v_b_in),
        "a_ln_g": (a_ln_g, m_a_ln_g, v_a_ln_g), "a_ln_b": (a_ln_b, m_a_ln_b, v_a_ln_b),
        "a_spatial_b": (a_spatial_b.reshape(1, D), m_a_spatial_b.reshape(1, D), v_a_spatial_b.reshape(1, D)),
        "b_conv_b": (b_conv_b, m_b_conv_b, v_b_conv_b), "b_gn_g": (b_gn_g, m_b_gn_g, v_b_gn_g),
        "b_gn_b": (b_gn_b, m_b_gn_b, v_b_gn_b), "out_norm_a_g": (out_norm_a_g, m_out_norm_a_g, v_out_norm_a_g),
        "out_norm_b_g": (out_norm_b_g, m_out_norm_b_g, v_out_norm_b_g),
        "norm2_g": (norm2_g, m_norm2_g, v_norm2_g), "norm_f_g": (norm_f_g, m_norm_f_g, v_norm_f_g),
        "a_spatial_w": (a_spatial_w, m_a_spatial_w, v_a_spatial_w),
        "b_conv_w": (b_conv_w[0], m_b_conv_w[0], v_b_conv_w[0]),
    }
    small_in = {n: tuple(two(a) for a in t) for n, t in small_in.items()}
    res.update(_adam_small(vsum, wssum, gcw, small_in))

    loss = 0.5 / D * jnp.sum(vsum[_LOSS_ROW])
    shapes = {"ada_w": ada_w, "ada_b": ada_b, "norm1_g": norm1_g, "w_in": w_in, "b_in": b_in, "a_ln_g": a_ln_g,
              "a_ln_b": a_ln_b, "a_spatial_w": a_spatial_w, "a_spatial_b": a_spatial_b, "b_conv_w": b_conv_w,
              "b_conv_b": b_conv_b, "b_gn_g": b_gn_g, "b_gn_b": b_gn_b, "out_norm_a_g": out_norm_a_g,
              "out_norm_b_g": out_norm_b_g, "w_out": w_out, "norm2_g": norm2_g, "w_ffn_in": w_ffn_in,
              "w_ffn_out": w_ffn_out, "ada_f_w": ada_f_w, "ada_f_b": ada_f_b, "norm_f_g": norm_f_g}
    order = list(shapes)
    outs = [loss, gx.reshape(x.shape)]
    for which in range(4):
        outs += [res[n][which].reshape(shapes[n].shape) for n in order]
    return tuple(outs)
```

```python
import math

import jax
import jax.numpy as jnp
from jax import lax
from jax.experimental import pallas as pl
from jax.experimental.pallas import tpu as pltpu

F32 = jnp.float32
BF16 = jnp.bfloat16

D = 1024
DA = 512
DB = 512
DIN = 2048
DFF = 2816
NH = 8
CH = 128
KW = 31
HALO = 32
NDEV = 8
WIN_B = DIN // NDEV
WFI_B = 2 * DFF // NDEV
NFB = DFF // WFI_B
EPS = 1e-6
NVEC = 56
VMEM_LIMIT = 56 * 1024 * 1024

ADAM_LR, ADAM_B1, ADAM_B2, ADAM_EPS, ADAM_WD, ADAM_STEP = 0.001, 0.9, 0.999, 1e-08, 0.01, 10

MESH = pl.DeviceIdType.MESH


def _dot(a, b):
    return jnp.dot(a, b, preferred_element_type=F32)


def _dot_nt(a, b):
    return lax.dot_general(a, b, (((1,), (1,)), ((), ())), preferred_element_type=F32)


def _dot_tn(a, b):
    return lax.dot_general(a, b, (((0,), (0,)), ((), ())), preferred_element_type=F32)


def _rs(v):
    return lax.rsqrt(jnp.mean(v * v, axis=-1, keepdims=True) + EPS)


def _sig(v):
    return 1.0 / (1.0 + jnp.exp(-v))


_INV_SQRT2 = 1.0 / math.sqrt(2.0)
_INV_SQRT2PI = 1.0 / math.sqrt(2.0 * math.pi)


def _gelu_parts(v):
    cdf = 0.5 * (1.0 + lax.erf(v * _INV_SQRT2))
    pdf = jnp.exp(-0.5 * v * v) * _INV_SQRT2PI
    return v * cdf, cdf + v * pdf


def _grp_mean(v, pm):
    hi = v.astype(BF16)
    lo = (v - hi.astype(F32)).astype(BF16)
    return _dot(hi, pm) + _dot(lo, pm)


def _colsum(v):
    return jnp.sum(v, axis=0, keepdims=True)


def _full(shape):
    nd = len(shape)
    return pl.BlockSpec(shape, lambda *_: (0,) * nd)


def _resident(shape):
    nd = len(shape)
    return pl.BlockSpec(shape, lambda *_: (0,) * nd, pipeline_mode=pl.Buffered(1))


HBM = pl.BlockSpec(memory_space=pl.ANY)
VM = pl.BlockSpec(memory_space=pltpu.VMEM)


SH_ROWS = HALO - 8


def _shifted_copies(buf, shbuf, tm):
    for b in range(1, 8):
        shbuf[b - 1] = buf[b:b + tm + SH_ROWS, :]


def _window(buf, shbuf, off, tm):
    a, b = divmod(off, 8)
    if b == 0:
        return buf[8 * a:8 * a + tm, :]
    return shbuf[b - 1, 8 * a:8 * a + tm, :]


def _head_of_lane(rows):
    return lax.broadcasted_iota(jnp.int32, (rows, DA), 1) >> 6


def _block_pick(r, lane_head):
    out = jnp.zeros((CH, DA), F32)
    for h in range(NH):
        out = jnp.where(lane_head == h, r[h * CH:(h + 1) * CH, :], out)
    return out


def _place():
    x, y, c = lax.axis_index("x"), lax.axis_index("y"), lax.axis_index("c")
    return x, y, c, 4 * x + 2 * y + c


def _dev(t):
    return (t >> 2, (t >> 1) & 1, t & 1)


class _AllGather:
    def __init__(self, w_in, w_out, wss, wrs, lsem):
        x, y, c, idx = _place()
        me, sibling = (x, y, c), (x, y, 1 - c)
        chips = [(1 - x, y), (x, 1 - y), (1 - x, 1 - y)]
        nw = len(w_in)

        def blk(p):
            return 4 * p[0] + 2 * p[1] + p[2]

        def wcopy(a, k, block, to, src=None):
            dst = w_out[a].at[blk(block)]
            return pltpu.make_async_remote_copy(src_ref=dst if src is None else src, dst_ref=dst,
                                                send_sem=wss.at[a, k], recv_sem=wrs.at[a, k],
                                                device_id=to, device_id_type=MESH)

        self.mine = [pltpu.make_async_copy(w_in[a], w_out[a].at[idx], lsem.at[a]) for a in range(nw)]
        self.first = []
        for a in range(nw):
            self.first.append(wcopy(a, 0, me, sibling, src=w_in[a]))
            self.first += [wcopy(a, 1 + j, me, (*chip, c), src=w_in[a]) for j, chip in enumerate(chips)]
        self.landed = [[wcopy(a, 1 + j, (*chip, c), me) for a in range(nw)] for j, chip in enumerate(chips)]
        self.passed = [[wcopy(a, 4 + j, (*chip, c), sibling) for a in range(nw)] for j, chip in enumerate(chips)]
        self.from_sibling = []
        for a in range(nw):
            self.from_sibling.append(wcopy(a, 0, sibling, me))
            self.from_sibling += [wcopy(a, 4 + j, (*chip, 1 - c), me) for j, chip in enumerate(chips)]

    def start(self):
        for cp in self.mine + self.first:
            cp.start()

    def forward(self):
        for land, pas in zip(self.landed, self.passed):
            for l, p in zip(land, pas):
                l.wait_recv()
                p.start()

    def finish(self):
        for cp in self.from_sibling:
            cp.wait_recv()
        for cp in self.first:
            cp.wait_send()
        for pas in self.passed:
            for p in pas:
                p.wait_send()
        for cp in self.mine:
            cp.wait()


AG_SEMS = lambda nw: [pltpu.SemaphoreType.DMA((nw, 7)), pltpu.SemaphoreType.DMA((nw, 7)),
                      pltpu.SemaphoreType.DMA((nw,))]


class _ReduceScatter:
    def __init__(self, g_in, r_out, gss, grs, lsem):
        x, y, c, idx = _place()
        me = (x, y, c)
        nw = len(g_in)
        self.mine = [pltpu.make_async_copy(g_in[a].at[idx], r_out[a].at[0], lsem.at[a]) for a in range(nw)]
        self.sends, self.recvs = [], []
        for k in range(1, NDEV):
            t = idx ^ k
            for a in range(nw):
                self.sends.append(pltpu.make_async_remote_copy(
                    src_ref=g_in[a].at[t], dst_ref=r_out[a].at[k], send_sem=gss.at[a, k - 1],
                    recv_sem=grs.at[a, k - 1], device_id=_dev(t), device_id_type=MESH))
                self.recvs.append(pltpu.make_async_remote_copy(
                    src_ref=g_in[a].at[0], dst_ref=r_out[a].at[k], send_sem=gss.at[a, k - 1],
                    recv_sem=grs.at[a, k - 1], device_id=me, device_id_type=MESH))

    def start(self):
        for cp in self.mine + self.sends:
            cp.start()

    def finish(self):
        for cp in self.recvs:
            cp.wait_recv()
        for cp in self.sends:
            cp.wait_send()
        for cp in self.mine:
            cp.wait()


RS_SEMS = AG_SEMS


def _mix_fwd(x, mod, g1, win, b_in, lng, lnb, wcat, bsf, cw, cb, gng, gnb, oga, ogb, wout, pm, ffn_shards, tm):
    T = x.shape[0]
    nt = T // tm
    nch = tm // CH
    nw = len(ffn_shards)
    fwd_step = (5 * nt) // 8
    saved = [(D, F32), (D, BF16), (2 * DB, F32), (DA, F32), (D, BF16), (D, F32), (DA, F32), (DA, F32), (DA, F32),
             (DA, F32), (CH, F32), (DB, F32), (DB, F32)]
    NSAVE = len(saved)

    def body(x_ref, mod_ref, g1_ref, win_ref, bin_ref, lng_ref, lnb_ref, wcat_ref, bsf_ref, cw_ref, cb_ref,
             gng_ref, gnb_ref, oga_ref, ogb_ref, wout_ref, pm_ref, *rest):
        sh_in = rest[:nw]
        (x1_ref, h_ref, zvg_ref, mixed_ref, y_ref, o_ref, gu_ref, dgu_ref, dgv_ref, vhat_ref, rsl_ref, yhat_ref,
         rsg_ref) = rest[nw:nw + NSAVE]
        sh_out = rest[nw + NSAVE:2 * nw + NSAVE]
        glbuf, shbuf, wss, wrs, lsem = rest[2 * nw + NSAVE:]
        i = pl.program_id(0)

        @pl.when(i == 0)
        def _():
            _AllGather(sh_in, sh_out, wss, wrs, lsem).start()

        xv = x_ref[...]
        shift1 = mod_ref[0:1, :]
        scale1 = mod_ref[1:2, :]
        gate1 = mod_ref[2:3, :]
        h = (xv * _rs(xv) * g1_ref[...]) * (1.0 + scale1) + shift1
        hb = h.astype(BF16)
        h_ref[...] = hb
        z = jnp.concatenate([_dot(hb, win_ref[j]) for j in range(NDEV)], axis=1) + bin_ref[...]
        zvg_ref[...] = z[:, 2 * DA:]
        gu, dgelu_u = _gelu_parts(z[:, 0:DA])
        gv, dgelu_v = _gelu_parts(z[:, DA:2 * DA])
        gu_ref[...] = gu
        dgu_ref[...] = dgelu_u
        dgv_ref[...] = dgelu_v
        xc = gv - jnp.mean(gv, axis=-1, keepdims=True)
        rsl = lax.rsqrt(jnp.mean(xc * xc, axis=-1, keepdims=True) + EPS)
        vhat = xc * rsl
        vhat_ref[...] = vhat
        rsl_ref[...] = jnp.broadcast_to(rsl, (tm, CH))
        vnb = (vhat * lng_ref[...] + lnb_ref[...]).astype(BF16)
        lane_head = _head_of_lane(CH)
        chunks = []
        for ci in range(nch):
            r = _dot(wcat_ref[...], vnb[ci * CH:(ci + 1) * CH, :])
            chunks.append(_block_pick(r, lane_head) + bsf_ref[...])
        mixed = jnp.concatenate(chunks, axis=0) if nch > 1 else chunks[0]
        mixed_ref[...] = mixed
        ya = gu * mixed
        gl = z[:, 2 * DA:2 * DA + DB] * _sig(z[:, 2 * DA + DB:])

        @pl.when(i == 0)
        def _():
            glbuf[0:HALO, :] = jnp.zeros((HALO, DB), F32)

        glbuf[HALO:HALO + tm, :] = gl
        _shifted_copies(glbuf, shbuf, tm)
        yc = jnp.zeros((tm, DB), F32) + cb_ref[...]
        for k in range(KW):
            yc = yc + cw_ref[k:k + 1, :] * _window(glbuf, shbuf, HALO - (KW - 1) + k, tm)
        glbuf[0:HALO, :] = gl[tm - HALO:, :]
        pmv = pm_ref[...]
        dc = yc - _grp_mean(yc, pmv)
        rsg = lax.rsqrt(_grp_mean(dc * dc, pmv) + EPS)
        yhat = dc * rsg
        yhat_ref[...] = yhat
        rsg_ref[...] = rsg
        yg = yhat * gng_ref[...] + gnb_ref[...]
        yb = yg * _sig(yg)
        na = ya * _rs(ya) * oga_ref[...]
        nb = yb * _rs(yb) * ogb_ref[...]
        yv = jnp.concatenate([na, nb], axis=1).astype(BF16)
        y_ref[...] = yv
        o = _dot(yv, wout_ref[...])
        o_ref[...] = o
        x1_ref[...] = xv + gate1 * o

        @pl.when(i == fwd_step)
        def _():
            _AllGather(sh_in, sh_out, wss, wrs, lsem).forward()

        @pl.when(i == nt - 1)
        def _():
            _AllGather(sh_in, sh_out, wss, wrs, lsem).finish()

    tile = lambda w: pl.BlockSpec((tm, w), lambda i: (i, 0))
    outs = pl.pallas_call(
        body,
        name="mix_fwd",
        grid=(nt,),
        in_specs=[tile(D), _full((8, D)), _full((1, D)), _resident((NDEV, D, WIN_B)), _full((1, DIN)),
                  _full((1, DA)), _full((1, DA)), _full((NH * CH, CH)), _full((CH, DA)), _full((HALO, DB)),
                  _full((1, DB)), _full((1, DB)), _full((1, DB)), _full((1, DA)), _full((1, DB)),
                  _resident((D, D)), _full((DB, DB))] + [HBM] * nw,
        out_specs=[tile(w) for w, _ in saved] + [HBM] * nw,
        out_shape=[jax.ShapeDtypeStruct((T, w), dt) for w, dt in saved]
                  + [jax.ShapeDtypeStruct((NDEV,) + s.shape, s.dtype) for s in ffn_shards],
        scratch_shapes=[pltpu.VMEM((HALO + tm, DB), F32), pltpu.VMEM((7, tm + SH_ROWS, DB), F32)] + AG_SEMS(nw),
        compiler_params=pltpu.CompilerParams(dimension_semantics=("arbitrary",), vmem_limit_bytes=VMEM_LIMIT),
    )(x, mod, g1, win, b_in, lng, lnb, wcat, bsf, cw, cb, gng, gnb, oga, ogb, wout, pm, *ffn_shards)
    return outs[:NSAVE], outs[NSAVE:]


def _ffn(x1, tgt, mod, g2, gf, wfi, wfo, tm):
    T = x1.shape[0]
    nt = T // tm

    def body(x1_ref, tgt_ref, mod_ref, g2_ref, gf_ref, wfi_ref, wfo_ref,
             dx1_ref, h2_ref, dgu_ref, act_ref, dxg_ref, acc_ref, g_s, u_s):
        i = pl.program_id(0)

        @pl.when(i == 0)
        def _():
            acc_ref[...] = jnp.zeros((8, D), F32)

        x1 = x1_ref[...]
        shift2 = mod_ref[3:4, :]
        scale2 = mod_ref[4:5, :]
        gate2 = mod_ref[5:6, :]
        shiftf = mod_ref[6:7, :]
        scalef = mod_ref[7:8, :]
        g2v = g2_ref[...]
        gfv = gf_ref[...]
        r2 = _rs(x1)
        xn2 = x1 * r2
        h2b = (xn2 * g2v * (1.0 + scale2) + shift2).astype(BF16)
        h2_ref[...] = h2b
        f = jnp.zeros((tm, D), F32)
        for j in range(NFB):
            g = _dot(h2b, wfi_ref[j])
            u = _dot(h2b, wfi_ref[NFB + j])
            g_s[j] = g
            u_s[j] = u
            actb = (g * _sig(g) * u).astype(BF16)
            act_ref[j] = actb
            f = f + _dot(actb, wfo_ref[j * WFI_B:(j + 1) * WFI_B, :])
        x2 = x1 + gate2 * f
        rf = _rs(x2)
        xnf = x2 * rf
        out = xnf * gfv * (1.0 + scalef) + shiftf
        e = out - tgt_ref[...]
        dout = e * (1.0 / D)
        acc_ref[7:8, :] += _colsum(e * e)
        acc_ref[0:1, :] += _colsum(dout)
        acc_ref[1:2, :] += _colsum(dout * xnf * gfv)
        acc_ref[2:3, :] += _colsum(dout * (1.0 + scalef) * xnf)
        dxnf = dout * (1.0 + scalef) * gfv
        dx2 = rf * (dxnf - xnf * jnp.mean(dxnf * xnf, axis=-1, keepdims=True))
        acc_ref[3:4, :] += _colsum(dx2 * f)
        dxgb = (dx2 * gate2).astype(BF16)
        dxg_ref[...] = dxgb
        dh2 = jnp.zeros((tm, D), F32)
        for j in range(NFB):
            dact = _dot_nt(dxgb, wfo_ref[j * WFI_B:(j + 1) * WFI_B, :])
            g = g_s[j]
            u = u_s[j]
            s = _sig(g)
            dgb = (dact * u * (s * (1.0 + g * (1.0 - s)))).astype(BF16)
            dub = (dact * (g * s)).astype(BF16)
            dgu_ref[j] = dgb
            dgu_ref[NFB + j] = dub
            dh2 = dh2 + _dot_nt(dgb, wfi_ref[j])
            dh2 = dh2 + _dot_nt(dub, wfi_ref[NFB + j])
        acc_ref[4:5, :] += _colsum(dh2)
        acc_ref[5:6, :] += _colsum(dh2 * xn2 * g2v)
        acc_ref[6:7, :] += _colsum(dh2 * (1.0 + scale2) * xn2)
        dxn2 = dh2 * (1.0 + scale2) * g2v
        dx1_ref[...] = dx2 + r2 * (dxn2 - xn2 * jnp.mean(dxn2 * xn2, axis=-1, keepdims=True))

    tile = lambda w: pl.BlockSpec((tm, w), lambda i: (i, 0))
    blocked = lambda n: pl.BlockSpec((n, tm, WFI_B), lambda i: (0, i, 0))
    return pl.pallas_call(
        body,
        name="ffn_fwd_bwd",
        grid=(nt,),
        in_specs=[tile(D), tile(D), _full((8, D)), _full((1, D)), _full((1, D)),
                  _resident((NDEV, D, WFI_B)), _resident((DFF, D))],
        out_specs=[tile(D), tile(D), blocked(NDEV), blocked(NFB), tile(D), _full((8, D))],
        out_shape=[jax.ShapeDtypeStruct((T, D), F32), jax.ShapeDtypeStruct((T, D), BF16),
                   jax.ShapeDtypeStruct((NDEV, T, WFI_B), BF16), jax.ShapeDtypeStruct((NFB, T, WFI_B), BF16),
                   jax.ShapeDtypeStruct((T, D), BF16), jax.ShapeDtypeStruct((8, D), F32)],
        scratch_shapes=[pltpu.VMEM((NFB, tm, WFI_B), F32), pltpu.VMEM((NFB, tm, WFI_B), F32)],
        compiler_params=pltpu.CompilerParams(dimension_semantics=("arbitrary",), vmem_limit_bytes=VMEM_LIMIT),
    )(x1, tgt, mod, g2, gf, wfi, wfo)


def _mix_bwd(dx1, x, zvg, mixed, o, hb, yb, gu, dgu, dgv, vhat, rslb, yhat, rsg, mod, g1, win, lng, lnb, wcat, wcat_t,
             cw, gng, gnb, oga, ogb, wout, pm, esel, ffn_grads, tm):
    T = x.shape[0]
    nt = T // tm
    nch = tm // CH
    nw = len(ffn_grads)
    WOB = 256

    def body(dx1_ref, x_ref, zvg_ref, mixed_ref, o_ref, hb_ref, yb_ref, gu_ref, dgu_ref, dgv_ref, vhat_ref, rsl_ref,
             yhat_ref, rsg_ref, mod_ref, g1_ref, win_ref, lng_ref, lnb_ref, wcat_ref, wcatt_ref, cw_ref, gng_ref,
             gnb_ref, oga_ref, ogb_ref, wout_ref, pm_ref, esel_ref, *rest):
        g_in = rest[:nw]
        gx_ref, accv_ref, accb_ref, acca_ref, accbs_ref, accws_ref, acccw_ref = rest[nw:nw + 7]
        r_out = rest[nw + 7:2 * nw + 7]
        gwin_ref, gwout_ref = rest[2 * nw + 7:2 * nw + 9]
        dycbuf, shbuf, bs_s, acc_win, acc_wout, st_win, st_wout, gss, grs, lsem = rest[2 * nw + 9:]
        i = pl.program_id(0)

        @pl.when(i == 0)
        def _():
            _ReduceScatter(g_in, r_out, gss, grs, lsem).start()
            acc_win[...] = jnp.zeros((NDEV, D, WIN_B), F32)
            acc_wout[...] = jnp.zeros((D, D), F32)
            accv_ref[...] = jnp.zeros((8, D), F32)
            accb_ref[...] = jnp.zeros((1, DIN), F32)
            acca_ref[...] = jnp.zeros((8, DA), F32)
            accws_ref[...] = jnp.zeros((NH * CH, CH), F32)
            acccw_ref[...] = jnp.zeros((HALO, DB), F32)
            bs_s[...] = jnp.zeros((CH, DA), F32)
            dycbuf[tm:tm + HALO, :] = jnp.zeros((HALO, DB), F32)

        shift1 = mod_ref[0:1, :]
        scale1 = mod_ref[1:2, :]
        gate1 = mod_ref[2:3, :]
        g1v = g1_ref[...]
        xv = x_ref[...]
        r1 = _rs(xv)
        xn1 = xv * r1
        val = zvg_ref[:, 0:DB]
        gate = zvg_ref[:, DB:]
        gu = gu_ref[...]
        dgelu_u = dgu_ref[...]
        dgelu_v = dgv_ref[...]
        vhat = vhat_ref[...]
        rsl = rsl_ref[:, 0:1]
        lngv = lng_ref[...]
        vnb = (vhat * lngv + lnb_ref[...]).astype(BF16)
        mixed = mixed_ref[...]
        ya = gu * mixed
        ra = _rs(ya)
        yan = ya * ra
        sgt = _sig(gate)
        gl = val * sgt
        pmv = pm_ref[...]
        rsg = rsg_ref[...]
        yhat = yhat_ref[...]
        gngv = gng_ref[...]
        yg = yhat * gngv + gnb_ref[...]
        sgy = _sig(yg)
        yb = yg * sgy
        rb = _rs(yb)
        ybn = yb * rb
        dx1 = dx1_ref[...]
        accv_ref[0:1, :] += _colsum(dx1 * o_ref[...])
        dogb = (dx1 * gate1).astype(BF16)
        acc_wout[...] += _dot_tn(yb_ref[...], dogb)
        dy = _dot_nt(dogb, wout_ref[...])
        dna = dy[:, 0:DA]
        dnb = dy[:, DA:]
        ogav = oga_ref[...]
        ogbv = ogb_ref[...]
        acca_ref[2:3, :] += _colsum(dna * yan)
        acca_ref[3:4, :] += _colsum(dnb * ybn)
        ta = dna * ogav
        dya = ra * (ta - yan * jnp.mean(ta * yan, axis=-1, keepdims=True))
        tb = dnb * ogbv
        dyb = rb * (tb - ybn * jnp.mean(tb * ybn, axis=-1, keepdims=True))
        dgu = dya * mixed
        dm = dya * gu
        lane_head = _head_of_lane(CH)
        dvn_chunks = []
        bs_acc = bs_s[...]
        for ci in range(nch):
            dmc = dm[ci * CH:(ci + 1) * CH, :]
            bs_acc = bs_acc + dmc
            dmcb = dmc.astype(BF16)
            dvn_chunks.append(_block_pick(_dot(wcatt_ref[...], dmcb), lane_head))
            zero = jnp.zeros((CH, DA), BF16)
            stack = jnp.concatenate([jnp.where(lane_head == h, dmcb, zero) for h in range(NH)], axis=0)
            accws_ref[...] += _dot_nt(stack, vnb[ci * CH:(ci + 1) * CH, :])
        bs_s[...] = bs_acc
        dvn = jnp.concatenate(dvn_chunks, axis=0) if nch > 1 else dvn_chunks[0]
        acca_ref[0:1, :] += _colsum(dvn * vhat)
        acca_ref[1:2, :] += _colsum(dvn)
        dvh = dvn * lngv
        dgv = rsl * (dvh - jnp.mean(dvh, axis=-1, keepdims=True)
                     - vhat * jnp.mean(dvh * vhat, axis=-1, keepdims=True))
        du = dgu * dgelu_u
        dv = dgv * dgelu_v
        dyg = dyb * (sgy * (1.0 + yg * (1.0 - sgy)))
        acca_ref[5:6, :] += _colsum(dyg * yhat)
        acca_ref[6:7, :] += _colsum(dyg)
        dyh = dyg * gngv
        dyc = rsg * (dyh - _grp_mean(dyh, pmv) - yhat * _grp_mean(dyh * yhat, pmv))
        acca_ref[4:5, :] += _colsum(dyc)
        dycbuf[0:tm, :] = dyc
        _shifted_copies(dycbuf, shbuf, tm)
        dgl = jnp.zeros((tm, DB), F32)
        for k in range(KW):
            win_k = _window(dycbuf, shbuf, KW - 1 - k, tm)
            dgl = dgl + cw_ref[k:k + 1, :] * win_k
            acccw_ref[k:k + 1, :] += _colsum(win_k * gl)
        dycbuf[tm:tm + HALO, :] = dyc[0:HALO, :]
        dval = dgl * sgt
        dgate = dgl * val * sgt * (1.0 - sgt)
        dz = jnp.concatenate([du, dv, dval, dgate], axis=1)
        accb_ref[...] += _colsum(dz)
        dzb = dz.astype(BF16)
        hbv = hb_ref[...]
        dh = jnp.zeros((tm, D), F32)
        for j in range(NDEV):
            dzj = dzb[:, j * WIN_B:(j + 1) * WIN_B]
            acc_win[j] += _dot_tn(hbv, dzj)
            dh = dh + _dot_nt(dzj, win_ref[j])
        accv_ref[1:2, :] += _colsum(dh)
        accv_ref[2:3, :] += _colsum(dh * xn1 * g1v)
        accv_ref[3:4, :] += _colsum(dh * (1.0 + scale1) * xn1)
        dxn1 = dh * (1.0 + scale1) * g1v
        gx_ref[...] = dx1 + r1 * (dxn1 - xn1 * jnp.mean(dxn1 * xn1, axis=-1, keepdims=True))

        @pl.when(i == nt - 1)
        def _():
            rows = lax.broadcasted_iota(jnp.int32, (NH * CH, CH), 0) & (CH - 1)
            cols = lax.broadcasted_iota(jnp.int32, (NH * CH, CH), 1)
            accws_ref[...] = jnp.where(cols <= rows, accws_ref[...], 0.0)
            bs = bs_s[...]
            hi = bs.astype(BF16)
            r1_ = bs - hi.astype(F32)
            mid = r1_.astype(BF16)
            lo = (r1_ - mid.astype(F32)).astype(BF16)
            ev = esel_ref[...]
            accbs_ref[...] = _dot(hi, ev) + _dot(mid, ev) + _dot(lo, ev)
            for j in range(NDEV):
                st_win[...] = acc_win[j].astype(BF16)
                pltpu.sync_copy(st_win, gwin_ref.at[j])
            for j in range(D // WOB):
                st_wout[...] = acc_wout[j * WOB:(j + 1) * WOB, :].astype(BF16)
                pltpu.sync_copy(st_wout, gwout_ref.at[pl.ds(j * WOB, WOB)])
            _ReduceScatter(g_in, r_out, gss, grs, lsem).finish()

    rev = lambda w: pl.BlockSpec((tm, w), lambda i: (nt - 1 - i, 0))
    outs = pl.pallas_call(
        body,
        name="mix_bwd",
        grid=(nt,),
        in_specs=[rev(D), rev(D), rev(2 * DB), rev(DA), rev(D), rev(D), rev(D), rev(DA), rev(DA), rev(DA), rev(DA),
                  rev(CH), rev(DB), rev(DB), _full((8, D)), _full((1, D)),
                  _resident((NDEV, D, WIN_B)), _full((1, DA)), _full((1, DA)), _full((NH * CH, CH)),
                  _full((NH * CH, CH)), _full((HALO, DB)), _full((1, DB)), _full((1, DB)), _full((1, DA)),
                  _full((1, DB)), _resident((D, D)), _full((DB, DB)), _full((DA, CH))] + [HBM] * nw,
        out_specs=[rev(D), _full((8, D)), _full((1, DIN)), _full((8, DA)), _full((CH, CH)),
                   _full((NH * CH, CH)), _full((HALO, DB))] + [HBM] * (nw + 2),
        out_shape=[jax.ShapeDtypeStruct((T, D), F32), jax.ShapeDtypeStruct((8, D), F32),
                   jax.ShapeDtypeStruct((1, DIN), F32), jax.ShapeDtypeStruct((8, DA), F32),
                   jax.ShapeDtypeStruct((CH, CH), F32), jax.ShapeDtypeStruct((NH * CH, CH), F32),
                   jax.ShapeDtypeStruct((HALO, DB), F32)]
                  + [jax.ShapeDtypeStruct(g.shape, g.dtype) for g in ffn_grads]
                  + [jax.ShapeDtypeStruct((NDEV, D, WIN_B), BF16), jax.ShapeDtypeStruct((D, D), BF16)],
        scratch_shapes=[pltpu.VMEM((tm + HALO, DB), F32), pltpu.VMEM((7, tm + SH_ROWS, DB), F32),
                        pltpu.VMEM((CH, DA), F32), pltpu.VMEM((NDEV, D, WIN_B), F32), pltpu.VMEM((D, D), F32),
                        pltpu.VMEM((D, WIN_B), BF16), pltpu.VMEM((WOB, D), BF16)] + RS_SEMS(nw),
        compiler_params=pltpu.CompilerParams(dimension_semantics=("arbitrary",), vmem_limit_bytes=VMEM_LIMIT),
    )(dx1, x, zvg, mixed, o, hb, yb, gu, dgu, dgv, vhat, rslb, yhat, rsg, mod, g1, win, lng, lnb, wcat, wcat_t, cw,
      gng, gnb, oga, ogb, wout, pm, esel, *ffn_grads)
    return outs[:7], outs[7:7 + nw], outs[7 + nw:]


def _wgrad_cols(a, b, nblk, tk, name):
    T, M = a.shape
    bw = b.shape[1] // nblk
    nk = T // tk

    def body(a_ref, b_ref, o_ref, acc):
        k = pl.program_id(0)

        @pl.when(k == 0)
        def _():
            acc[...] = jnp.zeros((nblk, M, bw), F32)

        av = a_ref[...]
        for j in range(nblk):
            acc[j] += _dot_tn(av, b_ref[:, j * bw:(j + 1) * bw])

        @pl.when(k == nk - 1)
        def _():
            o_ref[...] = acc[...].astype(BF16)

    return pl.pallas_call(
        body, name=name, grid=(nk,),
        in_specs=[pl.BlockSpec((tk, M), lambda k: (k, 0)), pl.BlockSpec((tk, nblk * bw), lambda k: (k, 0))],
        out_specs=_full((nblk, M, bw)),
        out_shape=jax.ShapeDtypeStruct((nblk, M, bw), BF16),
        scratch_shapes=[pltpu.VMEM((nblk, M, bw), F32)],
        compiler_params=pltpu.CompilerParams(dimension_semantics=("arbitrary",), vmem_limit_bytes=VMEM_LIMIT),
    )(a, b)


def _wgrad_b_blocked(a, b3, per, tk, name):
    T, M = a.shape
    nb, _, bw = b3.shape
    nk = T // tk

    def body(a_ref, b_ref, o_ref, acc):
        k = pl.program_id(1)

        @pl.when(k == 0)
        def _():
            acc[...] = jnp.zeros((per, M, bw), F32)

        av = a_ref[...]
        for j in range(per):
            acc[j] += _dot_tn(av, b_ref[j])

        @pl.when(k == nk - 1)
        def _():
            o_ref[...] = acc[...].astype(BF16)

    return pl.pallas_call(
        body, name=name, grid=(nb // per, nk),
        in_specs=[pl.BlockSpec((tk, M), lambda j, k: (k, 0)), pl.BlockSpec((per, tk, bw), lambda j, k: (j, k, 0))],
        out_specs=pl.BlockSpec((per, M, bw), lambda j, k: (j, 0, 0)),
        out_shape=jax.ShapeDtypeStruct((nb, M, bw), BF16),
        scratch_shapes=[pltpu.VMEM((per, M, bw), F32)],
        compiler_params=pltpu.CompilerParams(dimension_semantics=("arbitrary", "arbitrary"),
                                             vmem_limit_bytes=VMEM_LIMIT),
    )(a, b3)


def _wgrad_a_blocked(a3, b, tk, name):
    nb, T, bw = a3.shape
    N = b.shape[1]
    nk = T // tk

    def body(a_ref, b_ref, o_ref, acc):
        k = pl.program_id(1)

        @pl.when(k == 0)
        def _():
            acc[...] = jnp.zeros((bw, N), F32)

        acc[...] += _dot_tn(a_ref[0], b_ref[...])

        @pl.when(k == nk - 1)
        def _():
            o_ref[0] = acc[...].astype(BF16)

    return pl.pallas_call(
        body, name=name, grid=(nb, nk),
        in_specs=[pl.BlockSpec((1, tk, bw), lambda j, k: (j, k, 0)), pl.BlockSpec((tk, N), lambda j, k: (k, 0))],
        out_specs=pl.BlockSpec((1, bw, N), lambda j, k: (j, 0, 0)),
        out_shape=jax.ShapeDtypeStruct((nb, bw, N), BF16),
        scratch_shapes=[pltpu.VMEM((bw, N), F32)],
        compiler_params=pltpu.CompilerParams(dimension_semantics=("arbitrary", "arbitrary"),
                                             vmem_limit_bytes=VMEM_LIMIT),
    )(a3, b)


def _small_copy(src, dst, ss, rs, k, to):
    return pltpu.make_async_remote_copy(src_ref=src, dst_ref=dst, send_sem=ss.at[k], recv_sem=rs.at[k],
                                        device_id=to, device_id_type=MESH)


def _gather(c_row, ada_w, ada_b8, ada_f_w, ada_f_b8, conv_s, shards):
    nw = len(shards)

    def body(c_ref, adaw_ref, adab_ref, adafw_ref, adafb_ref, conv_ref, *rest):
        w_in = rest[:nw]
        call_ref, cparts_ref, cfparts_ref, convg_ref = rest[nw:nw + 4]
        w_out = rest[nw + 4:2 * nw + 4]
        part_s, partf_s, wss, wrs, lsem, s1, r1, s2, r2, s3, r3, s4, r4 = rest[2 * nw + 4:]
        x, y, c, idx = _place()
        me = (x, y, c)
        ag = _AllGather(w_in, w_out, wss, wrs, lsem)
        ag.start()
        call_ref[pl.ds(idx, 1), :] = c_ref[...]
        convg_ref[idx] = conv_ref[...]
        ph1 = []
        for k in range(1, NDEV):
            to = _dev(idx ^ k)
            ph1.append(_small_copy(c_ref, call_ref.at[pl.ds(idx, 1)], s1, r1, k - 1, to))
            ph1.append(_small_copy(conv_ref, convg_ref.at[idx], s2, r2, k - 1, to))
        for cp in ph1:
            cp.start()
        for k in range(1, NDEV):
            src_dev = idx ^ k
            _small_copy(c_ref, call_ref.at[pl.ds(src_dev, 1)], s1, r1, k - 1, me).wait_recv()
            _small_copy(conv_ref, convg_ref.at[src_dev], s2, r2, k - 1, me).wait_recv()
        call = call_ref[...]
        cact = (call * _sig(call))
        part_s[...] = jnp.dot(cact, adaw_ref[...], preferred_element_type=F32,
                              precision=lax.Precision.HIGHEST) + adab_ref[pl.ds(idx, 1), :]
        partf_s[...] = jnp.dot(cact, adafw_ref[...], preferred_element_type=F32,
                               precision=lax.Precision.HIGHEST) + adafb_ref[pl.ds(idx, 1), :]
        cparts_ref[pl.ds(idx, 1), :] = part_s[pl.ds(idx, 1), :]
        cfparts_ref[pl.ds(idx, 1), :] = partf_s[pl.ds(idx, 1), :]
        ph2 = []
        for k in range(1, NDEV):
            t = idx ^ k
            ph2.append(_small_copy(part_s.at[pl.ds(t, 1)], cparts_ref.at[pl.ds(idx, 1)], s3, r3, k - 1, _dev(t)))
            ph2.append(_small_copy(partf_s.at[pl.ds(t, 1)], cfparts_ref.at[pl.ds(idx, 1)], s4, r4, k - 1, _dev(t)))
        for cp in ph2:
            cp.start()
        for k in range(1, NDEV):
            src_dev = idx ^ k
            _small_copy(part_s.at[pl.ds(0, 1)], cparts_ref.at[pl.ds(src_dev, 1)], s3, r3, k - 1, me).wait_recv()
            _small_copy(partf_s.at[pl.ds(0, 1)], cfparts_ref.at[pl.ds(src_dev, 1)], s4, r4, k - 1, me).wait_recv()
        for cp in ph1 + ph2:
            cp.wait_send()
        ag.forward()
        ag.finish()

    dma7 = pltpu.SemaphoreType.DMA((NDEV - 1,))
    outs = pl.pallas_call(
        body,
        name="gather_weights",
        in_specs=[VM] * 6 + [HBM] * nw,
        out_specs=[VM] * 4 + [HBM] * nw,
        out_shape=[jax.ShapeDtypeStruct((NDEV, D), F32), jax.ShapeDtypeStruct((NDEV, ada_w.shape[1]), F32),
                   jax.ShapeDtypeStruct((NDEV, ada_f_w.shape[1]), F32),
                   jax.ShapeDtypeStruct((NDEV,) + conv_s.shape, F32)]
                  + [jax.ShapeDtypeStruct((NDEV,) + s.shape, s.dtype) for s in shards],
        scratch_shapes=[pltpu.VMEM((NDEV, ada_w.shape[1]), F32), pltpu.VMEM((NDEV, ada_f_w.shape[1]), F32)]
                       + AG_SEMS(nw) + [dma7] * 8,
        compiler_params=pltpu.CompilerParams(vmem_limit_bytes=VMEM_LIMIT),
    )(c_row, ada_w, ada_b8, ada_f_w, ada_f_b8, conv_s, *shards)
    return outs[0], outs[1], outs[2], outs[3], outs[4:]


_VEC_AT = {
    "norm1_g": (8, 0, D), "a_ln_g": (11, 0, DA), "a_ln_b": (11, DA, DA), "a_spatial_b": (12, 0, D),
    "b_conv_b": (13, 0, DB), "b_gn_g": (13, DB, DB), "b_gn_b": (14, 0, DB), "out_norm_a_g": (14, DB, DA),
    "out_norm_b_g": (15, 0, DB), "norm2_g": (16, 0, D), "norm_f_g": (17, 0, D),
}
_LOSS_ROW = 18
_CW_ROW = 24


def _reduce_small(acc_f, acc_v, acc_b, acc_a, acc_bs, acc_cw, dws, after):
    def body(accf_ref, accv_ref, accb_ref, acca_ref, accbs_ref, acccw_ref, dws_ref, after_ref,
             vsum_ref, dcond_ref, wssum_ref, vloc, vbuf, wbuf, wown, s1, r1, s2, r2, s3, r3):
        x, y, c, idx = _place()
        me = (x, y, c)
        vloc[...] = jnp.zeros((NVEC, D), F32)
        vloc[0:1, :] = accv_ref[1:2, :]
        vloc[1:2, :] = accv_ref[2:3, :]
        vloc[2:3, :] = accv_ref[0:1, :]
        vloc[3:4, :] = accf_ref[4:5, :]
        vloc[4:5, :] = accf_ref[5:6, :]
        vloc[5:6, :] = accf_ref[3:4, :]
        vloc[6:7, :] = accf_ref[0:1, :]
        vloc[7:8, :] = accf_ref[1:2, :]
        vloc[8:9, :] = accv_ref[3:4, :]
        vloc[9:10, :] = accb_ref[:, 0:D]
        vloc[10:11, :] = accb_ref[:, D:]
        vloc[11:12, 0:DA] = acca_ref[0:1, :]
        vloc[11:12, DA:] = acca_ref[1:2, :]
        bst = accbs_ref[...].T
        for h in range(NH):
            vloc[12:13, h * CH:(h + 1) * CH] = bst[h:h + 1, :]
        vloc[13:14, 0:DB] = acca_ref[4:5, :]
        vloc[13:14, DB:] = acca_ref[5:6, :]
        vloc[14:15, 0:DB] = acca_ref[6:7, :]
        vloc[14:15, DB:] = acca_ref[2:3, :]
        vloc[15:16, 0:DB] = acca_ref[3:4, :]
        vloc[16:17, :] = accf_ref[6:7, :]
        vloc[17:18, :] = accf_ref[2:3, :]
        vloc[_LOSS_ROW:_LOSS_ROW + 1, :] = accf_ref[7:8, :]
        vloc[_CW_ROW:_CW_ROW + HALO, 0:DB] = acccw_ref[...]
        vbuf[idx] = vloc[...]
        rows_of = lambda t: pl.ds(pl.multiple_of(t * CH, CH), CH)
        wbuf[0] = dws_ref[rows_of(idx), :]
        sm = []
        for k in range(1, NDEV):
            t = idx ^ k
            sm.append(_small_copy(vloc, vbuf.at[idx], s1, r1, k - 1, _dev(t)))
            sm.append(_small_copy(dws_ref.at[rows_of(t)], wbuf.at[k], s2, r2, k - 1, _dev(t)))
        for cp in sm:
            cp.start()
        for k in range(1, NDEV):
            _small_copy(dws_ref.at[rows_of(0)], wbuf.at[k], s2, r2, k - 1, me).wait_recv()
        ws = wbuf[0]
        for k in range(1, NDEV):
            ws = ws + wbuf[k]
        wown[...] = ws
        wssum_ref[rows_of(idx), :] = ws
        ag = [_small_copy(wown, wssum_ref.at[rows_of(idx)], s3, r3, k - 1, _dev(idx ^ k)) for k in range(1, NDEV)]
        for cp in ag:
            cp.start()
        for k in range(1, NDEV):
            _small_copy(vloc, vbuf.at[idx ^ k], s1, r1, k - 1, me).wait_recv()
        vs = vbuf[0]
        for d in range(1, NDEV):
            vs = vs + vbuf[d]
        vsum_ref[...] = vs
        for d in range(NDEV):
            dcond_ref[d] = vbuf[d, 0:8, :]
        for k in range(1, NDEV):
            _small_copy(wown, wssum_ref.at[rows_of(idx ^ k)], s3, r3, k - 1, me).wait_recv()
        for cp in sm + ag:
            cp.wait_send()

    dma7 = pltpu.SemaphoreType.DMA((NDEV - 1,))
    return pl.pallas_call(
        body,
        name="reduce_small",
        in_specs=[VM] * 7 + [HBM],
        out_specs=[VM, VM, VM],
        out_shape=[jax.ShapeDtypeStruct((NVEC, D), F32), jax.ShapeDtypeStruct((NDEV, 8, D), F32),
                   jax.ShapeDtypeStruct(dws.shape, F32)],
        scratch_shapes=[pltpu.VMEM((NVEC, D), F32), pltpu.VMEM((NDEV, NVEC, D), F32),
                        pltpu.VMEM((NDEV, CH, CH), F32), pltpu.VMEM((CH, CH), F32)] + [dma7] * 6,
        compiler_params=pltpu.CompilerParams(vmem_limit_bytes=VMEM_LIMIT),
    )(acc_f, acc_v, acc_b, acc_a, acc_bs, acc_cw, dws, after)


HBM_ONLY = pl.BlockSpec(memory_space=pltpu.HBM)
SEM = pl.BlockSpec(memory_space=pltpu.SEMAPHORE)
EFFECT = pltpu.SideEffectType.DATAFLOW_SIDE_EFFECTING


def _rs_copies(g_refs, land_refs, sems):
    x, y, c, idx = _place()
    cps = []
    for k in range(1, NDEV):
        t = idx ^ k
        for a in range(len(g_refs)):
            n = len(cps)
            cps.append(pltpu.make_async_remote_copy(
                src_ref=g_refs[a].at[t], dst_ref=land_refs[a].at[k - 1], send_sem=sems[2 * n],
                recv_sem=sems[2 * n + 1], device_id=_dev(t), device_id_type=MESH))
    return cps


def _rs_start(grads):
    nw = len(grads)
    nsem = 2 * nw * (NDEV - 1)
    lands = [lax.empty((NDEV - 1,) + g.shape[1:], g.dtype) for g in grads]

    def body(*refs):
        g_refs, land_refs = refs[:nw], refs[nw:2 * nw]
        sems = refs[2 * nw:2 * nw + nsem]
        token = refs[-1]
        for cp in _rs_copies(g_refs, land_refs, sems):
            cp.start()
        token[...] = jnp.zeros_like(token)

    outs = pl.pallas_call(
        body, name="rs_start",
        out_shape=(*[pltpu.SemaphoreType.DMA(())] * nsem,
                   *[pltpu.HBM(g.shape, g.dtype) for g in grads], *[pltpu.HBM(l.shape, l.dtype) for l in lands],
                   jax.ShapeDtypeStruct((8, CH), F32)),
        in_specs=[HBM_ONLY] * (2 * nw),
        out_specs=(*[SEM] * nsem, *[HBM_ONLY] * (2 * nw), VM),
        input_output_aliases={i: nsem + i for i in range(2 * nw)},
        compiler_params=pltpu.CompilerParams(has_side_effects=EFFECT),
    )(*[pltpu.with_memory_space_constraint(g, pltpu.HBM) for g in grads],
      *[pltpu.with_memory_space_constraint(l, pltpu.HBM) for l in lands])
    return outs[:nsem], outs[nsem:nsem + nw], outs[nsem + nw:nsem + 2 * nw], outs[-1]


def _rs_wait(sems, g_thru, land_thru, after):
    nw = len(g_thru)
    nsem = len(sems)

    def body(*refs):
        g_refs, land_refs = refs[:nw], refs[nw:2 * nw]
        for cp in _rs_copies(g_refs, land_refs, refs[2 * nw:2 * nw + nsem]):
            cp.wait_send()
            cp.wait_recv()

    outs = pl.pallas_call(
        body, name="rs_wait",
        out_shape=tuple(pltpu.HBM(a.shape, a.dtype) for a in list(g_thru) + list(land_thru)),
        in_specs=[HBM_ONLY] * (2 * nw) + [SEM] * nsem + [HBM] * len(after),
        out_specs=tuple([HBM_ONLY] * (2 * nw)),
        input_output_aliases={i: i for i in range(2 * nw)},
        compiler_params=pltpu.CompilerParams(has_side_effects=EFFECT),
    )(*g_thru, *land_thru, *sems, *after)
    return outs[:nw], outs[nw:]


def _adamw(w, g, m, v):
    m2 = ADAM_B1 * m + (1.0 - ADAM_B1) * g
    v2 = ADAM_B2 * v + (1.0 - ADAM_B2) * (g * g)
    m_hat = m2 / (1.0 - ADAM_B1 ** ADAM_STEP)
    v_hat = v2 / (1.0 - ADAM_B2 ** ADAM_STEP)
    delta = -ADAM_LR * (m_hat / (jnp.sqrt(v_hat) + ADAM_EPS) + ADAM_WD * w)
    return delta, m2, v2


def _adam_big(r, w, m, v, rb, name, own=None, after=None):
    R, C = w.shape
    ns = r.shape[0]

    def body(*refs):
        r_ref = refs[0]
        own_ref = refs[1] if own is not None else None
        w_ref, m_ref, v_ref, g_ref, d_ref, m2_ref, v2_ref = refs[len(refs) - 7:]
        g = r_ref[0].astype(F32) if own is None else own_ref[...].astype(F32) + r_ref[0].astype(F32)
        for k in range(1, ns):
            g = g + r_ref[k].astype(F32)
        g_ref[...] = g
        d_ref[...], m2_ref[...], v2_ref[...] = _adamw(w_ref[...], g, m_ref[...], v_ref[...])

    t2 = pl.BlockSpec((rb, C), lambda i: (i, 0))
    sd = jax.ShapeDtypeStruct((R, C), F32)
    extra_specs = ([t2] if own is not None else []) + ([HBM] if after is not None else [])
    extra = ([own] if own is not None else []) + ([after] if after is not None else [])
    return pl.pallas_call(
        body, name=name, grid=(R // rb,),
        in_specs=[pl.BlockSpec((ns, rb, C), lambda i: (0, i, 0))] + extra_specs + [t2, t2, t2],
        out_specs=[t2, t2, t2, t2], out_shape=[sd, sd, sd, sd],
        compiler_params=pltpu.CompilerParams(dimension_semantics=("arbitrary",), vmem_limit_bytes=VMEM_LIMIT),
    )(r, *extra, w, m, v)


def _adam_ada(cact_t, dcs, w, m, v, rb, name):
    R, C = w.shape

    def body(ct_ref, dc_ref, w_ref, m_ref, v_ref, g_ref, d_ref, m2_ref, v2_ref):
        g = jnp.dot(ct_ref[...], dc_ref[...], preferred_element_type=F32, precision=lax.Precision.HIGHEST)
        g_ref[...] = g
        d_ref[...], m2_ref[...], v2_ref[...] = _adamw(w_ref[...], g, m_ref[...], v_ref[...])

    t2 = pl.BlockSpec((rb, C), lambda i: (i, 0))
    sd = jax.ShapeDtypeStruct((R, C), F32)
    return pl.pallas_call(
        body, name=name, grid=(R // rb,),
        in_specs=[pl.BlockSpec((rb, NDEV), lambda i: (i, 0)), _full((NDEV, C)), t2, t2, t2],
        out_specs=[t2, t2, t2, t2], out_shape=[sd, sd, sd, sd],
        compiler_params=pltpu.CompilerParams(dimension_semantics=("arbitrary",), vmem_limit_bytes=VMEM_LIMIT),
    )(cact_t, dcs, w, m, v)


_SMALL = ["ada_b", "ada_f_b", "norm1_g", "b_in", "a_ln_g", "a_ln_b", "a_spatial_b", "b_conv_b", "b_gn_g", "b_gn_b",
          "out_norm_a_g", "out_norm_b_g", "norm2_g", "norm_f_g", "a_spatial_w", "b_conv_w"]


def _adam_small(vsum, wssum, gcw, params):
    names = _SMALL
    flat = []
    for n in names:
        flat += list(params[n])

    def body(vs_ref, ws_ref, gcw_ref, *rest):
        ins = rest[:3 * len(names)]
        outs = rest[3 * len(names):]
        for pi, n in enumerate(names):
            w_ref, m_ref, v_ref = ins[3 * pi:3 * pi + 3]
            g_ref, d_ref, m2_ref, v2_ref = outs[4 * pi:4 * pi + 4]
            if n in ("ada_b", "ada_f_b", "b_in"):
                row0 = {"ada_b": 0, "ada_f_b": 6, "b_in": 9}[n]
                pieces = [(vs_ref[row0 + r:row0 + r + 1, :], slice(r * D, (r + 1) * D))
                          for r in range(w_ref.shape[1] // D)]
            elif n == "a_spatial_w":
                pieces = [(ws_ref[...], slice(None))]
            elif n == "b_conv_w":
                pieces = [(gcw_ref[...], slice(None))]
            else:
                row, off, width = _VEC_AT[n]
                pieces = [(vs_ref[row:row + 1, off:off + width], slice(None))]
            for g, cs in pieces:
                g_ref[:, cs] = g
                d_ref[:, cs], m2_ref[:, cs], v2_ref[:, cs] = _adamw(w_ref[:, cs], g, m_ref[:, cs], v_ref[:, cs])

    out_shape = []
    for n in names:
        out_shape += [jax.ShapeDtypeStruct(params[n][0].shape, F32)] * 4
    outs = pl.pallas_call(
        body, name="adam_small",
        in_specs=[VM] * (3 + len(flat)), out_specs=[VM] * len(out_shape), out_shape=out_shape,
        compiler_params=pltpu.CompilerParams(vmem_limit_bytes=VMEM_LIMIT),
    )(vsum, wssum, gcw, *flat)
    return {n: outs[4 * pi:4 * pi + 4] for pi, n in enumerate(names)}


def _token_tile(T, want):
    return want if T % want == 0 else T


def kernel(x, c, ada_w, ada_b, norm1_g, w_in, b_in, a_ln_g, a_ln_b, a_spatial_w, a_spatial_b, b_conv_w, b_conv_b, b_gn_g, b_gn_b, out_norm_a_g, out_norm_b_g, w_out, norm2_g, w_ffn_in, w_ffn_out, ada_f_w, ada_f_b, norm_f_g, loss_target, m_ada_w, m_ada_b, m_norm1_g, m_w_in, m_b_in, m_a_ln_g, m_a_ln_b, m_a_spatial_w, m_a_spatial_b, m_b_conv_w, m_b_conv_b, m_b_gn_g, m_b_gn_b, m_out_norm_a_g, m_out_norm_b_g, m_w_out, m_norm2_g, m_w_ffn_in, m_w_ffn_out, m_ada_f_w, m_ada_f_b, m_norm_f_g, v_ada_w, v_ada_b, v_norm1_g, v_w_in, v_b_in, v_a_ln_g, v_a_ln_b, v_a_spatial_w, v_a_spatial_b, v_b_conv_w, v_b_conv_b, v_b_gn_g, v_b_gn_b, v_out_norm_a_g, v_out_norm_b_g, v_w_out, v_norm2_g, v_w_ffn_in, v_w_ffn_out, v_ada_f_w, v_ada_f_b, v_norm_f_g):
    T = x.shape[1]
    idx = 4 * lax.axis_index("x") + 2 * lax.axis_index("y") + lax.axis_index("c")
    x2d = x.reshape(T, D)
    tgt = loss_target.reshape(T, D)

    conv_s = jnp.pad(b_conv_w[0], ((0, HALO - KW), (0, 0)))
    call, cparts, cfparts, convg, (win_g, wout_g) = _gather(
        c, ada_w[0], ada_b.reshape(NDEV, -1), ada_f_w, ada_f_b.reshape(NDEV, -1), conv_s,
        [w_in[0].astype(BF16), w_out[0].astype(BF16)])
    mod = jnp.concatenate([cparts.reshape(6, D), cfparts.reshape(2, D)], axis=0)
    wout = wout_g.reshape(D, D)
    cw = jnp.transpose(convg, (1, 0, 2)).reshape(HALO, DB)

    tril = jnp.tril(jnp.ones((CH, CH), dtype=bool))
    wsm = jnp.where(tril[None], a_spatial_w[0], 0.0).astype(BF16)
    wcat = wsm.reshape(NH * CH, CH)
    wcat_t = jnp.transpose(wsm, (0, 2, 1)).reshape(NH * CH, CH)
    bsf = jnp.repeat(a_spatial_b[0].T, DA // NH, axis=1)
    lane = jnp.arange(DB)
    pm = jnp.where((lane[:, None] >> 6) == (lane[None, :] >> 6), 1.0 / 64.0, 0.0).astype(BF16)
    esel = jnp.where((lane[:, None] >> 6) == jnp.arange(CH)[None, :], 1.0, 0.0).astype(BF16)

    tm = _token_tile(T, 256)
    tk = _token_tile(T, 1024)
    (x1, hb, zvg, mixed, yb, o, gu, dgelu_u, dgelu_v, vhat, rslb, yhat, rsg), (wfi_g, wfo_g) = _mix_fwd(
        x2d, mod, norm1_g, win_g, b_in, a_ln_g, a_ln_b, wcat, bsf, cw, b_conv_b, b_gn_g, b_gn_b, out_norm_a_g,
        out_norm_b_g, wout, pm, [w_ffn_in[0].astype(BF16), w_ffn_out[0].astype(BF16)], _token_tile(T, 512))
    dx1, h2b, dgu, act, dxg, acc_f = _ffn(x1, tgt, mod, norm2_g, norm_f_g.reshape(1, D), wfi_g,
                                          wfo_g.reshape(DFF, D), tm)
    g_wfi = _wgrad_b_blocked(h2b, dgu, 2, tk, "wgrad_ffn_in")
    g_wfo = _wgrad_a_blocked(act, dxg, tk, "wgrad_ffn_out").reshape(NDEV, DFF // NDEV, D)
    (gx, acc_v, acc_b, acc_a, acc_bs, acc_ws, acc_cw), (r_wfi, r_wfo), (g_win, g_wout) = _mix_bwd(
        dx1, x2d, zvg, mixed, o, hb, yb, gu, dgelu_u, dgelu_v, vhat, rslb, yhat, rsg, mod, norm1_g, win_g, a_ln_g,
        a_ln_b, wcat, wcat_t, cw, b_gn_g, b_gn_b, out_norm_a_g, out_norm_b_g, wout, pm, esel, [g_wfi, g_wfo], tm)
    g_wout = g_wout.reshape(NDEV, D // NDEV, D)

    sems, g_thru, land_thru, token = _rs_start([g_win, g_wout])
    vsum, dcond_all, wssum = _reduce_small(acc_f, acc_v, acc_b, acc_a, acc_bs, acc_cw, acc_ws, token)

    res = {}
    res["w_ffn_in"] = _adam_big(r_wfi, w_ffn_in[0], m_w_ffn_in[0], v_w_ffn_in[0], 256, "adam_w_ffn_in", after=token)
    res["w_ffn_out"] = _adam_big(r_wfo, w_ffn_out[0], m_w_ffn_out[0], v_w_ffn_out[0], DFF // NDEV // 2,
                                 "adam_w_ffn_out", after=token)
    cact_t = (call * jax.nn.sigmoid(call)).T
    dcond = dcond_all.reshape(NDEV, 8 * D)
    nada = ada_w.shape[2]
    nadf = ada_f_w.shape[1]
    dcs = lax.dynamic_slice(dcond, (0, idx * nada), (NDEV, nada))
    dcfs = lax.dynamic_slice(dcond, (0, 6 * D + idx * nadf), (NDEV, nadf))
    res["ada_w"] = _adam_ada(cact_t, dcs, ada_w[0], m_ada_w[0], v_ada_w[0], 256, "adam_ada_w")
    res["ada_f_w"] = _adam_ada(cact_t, dcfs, ada_f_w, m_ada_f_w, v_ada_f_w, 256, "adam_ada_f_w")
    ncw = b_conv_w.shape[2]
    gcw = lax.dynamic_slice(vsum, (_CW_ROW, idx * ncw), (KW, ncw))
    two = lambda a: a.reshape(1, -1) if a.ndim == 1 else a.reshape(-1, a.shape[-1])
    small_in = {
        "ada_b": (ada_b, m_ada_b, v_ada_b), "ada_f_b": (ada_f_b, m_ada_f_b, v_ada_f_b),
        "norm1_g": (norm1_g, m_norm1_g, v_norm1_g), "b_in": (b_in, m_b_in, v_b_in),
        "a_ln_g": (a_ln_g, m_a_ln_g, v_a_ln_g), "a_ln_b": (a_ln_b, m_a_ln_b, v_a_ln_b),
        "a_spatial_b": (a_spatial_b.reshape(1, D), m_a_spatial_b.reshape(1, D), v_a_spatial_b.reshape(1, D)),
        "b_conv_b": (b_conv_b, m_b_conv_b, v_b_conv_b), "b_gn_g": (b_gn_g, m_b_gn_g, v_b_gn_g),
        "b_gn_b": (b_gn_b, m_b_gn_b, v_b_gn_b), "out_norm_a_g": (out_norm_a_g, m_out_norm_a_g, v_out_norm_a_g),
        "out_norm_b_g": (out_norm_b_g, m_out_norm_b_g, v_out_norm_b_g),
        "norm2_g": (norm2_g, m_norm2_g, v_norm2_g), "norm_f_g": (norm_f_g, m_norm_f_g, v_norm_f_g),
        "a_spatial_w": (a_spatial_w, m_a_spatial_w, v_a_spatial_w),
        "b_conv_w": (b_conv_w[0], m_b_conv_w[0], v_b_conv_w[0]),
    }
    small_in = {n: tuple(two(a) for a in t) for n, t in small_in.items()}
    res.update(_adam_small(vsum, wssum, gcw, small_in))
    (g_win_d, g_wout_d), (r_win, r_wout) = _rs_wait(
        sems, g_thru, land_thru,
        [res["w_ffn_in"][0], res["w_ffn_out"][0], res["ada_w"][0], res["ada_f_w"][0], res["norm_f_g"][0]])
    res["w_in"] = _adam_big(r_win, w_in[0], m_w_in[0], v_w_in[0], 256, "adam_w_in",
                            own=lax.dynamic_index_in_dim(g_win_d, idx, 0, keepdims=False))
    res["w_out"] = _adam_big(r_wout, w_out[0], m_w_out[0], v_w_out[0], D // NDEV, "adam_w_out",
                             own=lax.dynamic_index_in_dim(g_wout_d, idx, 0, keepdims=False))

    loss = 0.5 / D * jnp.sum(vsum[_LOSS_ROW])
    shapes = {"ada_w": ada_w, "ada_b": ada_b, "norm1_g": norm1_g, "w_in": w_in, "b_in": b_in, "a_ln_g": a_ln_g,
              "a_ln_b": a_ln_b, "a_spatial_w": a_spatial_w, "a_spatial_b": a_spatial_b, "b_conv_w": b_conv_w,
              "b_conv_b": b_conv_b, "b_gn_g": b_gn_g, "b_gn_b": b_gn_b, "out_norm_a_g": out_norm_a_g,
              "out_norm_b_g": out_norm_b_g, "w_out": w_out, "norm2_g": norm2_g, "w_ffn_in": w_ffn_in,
              "w_ffn_out": w_ffn_out, "ada_f_w": ada_f_w, "ada_f_b": ada_f_b, "norm_f_g": norm_f_g}
    order = list(shapes)
    outs = [loss, gx.reshape(x.shape)]
    for which in range(4):
        outs += [res[n][which].reshape(shapes[n].shape) for n in order]
    return tuple(outs)
```

```python
import math

import jax
import jax.numpy as jnp
from jax import lax
from jax.experimental import pallas as pl
from jax.experimental.pallas import tpu as pltpu

F32 = jnp.float32
BF16 = jnp.bfloat16

D = 1024
DA = 512
DB = 512
DIN = 2048
DFF = 2816
NH = 8
CH = 128
KW = 31
HALO = 32
NDEV = 8
WIN_B = DIN // NDEV
WFI_B = 2 * DFF // NDEV
NFB = DFF // WFI_B
EPS = 1e-6
NVEC = 56
VMEM_LIMIT = 56 * 1024 * 1024

ADAM_LR, ADAM_B1, ADAM_B2, ADAM_EPS, ADAM_WD, ADAM_STEP = 0.001, 0.9, 0.999, 1e-08, 0.01, 10

MESH = pl.DeviceIdType.MESH


def _dot(a, b):
    return jnp.dot(a, b, preferred_element_type=F32)


def _dot_nt(a, b):
    return lax.dot_general(a, b, (((1,), (1,)), ((), ())), preferred_element_type=F32)


def _dot_tn(a, b):
    return lax.dot_general(a, b, (((0,), (0,)), ((), ())), preferred_element_type=F32)


def _rs(v):
    return lax.rsqrt(jnp.mean(v * v, axis=-1, keepdims=True) + EPS)


def _sig(v):
    return 1.0 / (1.0 + jnp.exp(-v))


_INV_SQRT2 = 1.0 / math.sqrt(2.0)
_INV_SQRT2PI = 1.0 / math.sqrt(2.0 * math.pi)


def _gelu_parts(v):
    cdf = 0.5 * (1.0 + lax.erf(v * _INV_SQRT2))
    pdf = jnp.exp(-0.5 * v * v) * _INV_SQRT2PI
    return v * cdf, cdf + v * pdf


def _grp_mean(v, pm):
    hi = v.astype(BF16)
    lo = (v - hi.astype(F32)).astype(BF16)
    return _dot(hi, pm) + _dot(lo, pm)


def _colsum(v):
    return jnp.sum(v, axis=0, keepdims=True)


def _full(shape):
    nd = len(shape)
    return pl.BlockSpec(shape, lambda *_: (0,) * nd)


def _resident(shape):
    nd = len(shape)
    return pl.BlockSpec(shape, lambda *_: (0,) * nd, pipeline_mode=pl.Buffered(1))


HBM = pl.BlockSpec(memory_space=pl.ANY)
VM = pl.BlockSpec(memory_space=pltpu.VMEM)


SH_ROWS = HALO - 8


def _shifted_copies(buf, shbuf, tm):
    for b in range(1, 8):
        shbuf[b - 1] = buf[b:b + tm + SH_ROWS, :]


def _window(buf, shbuf, off, tm):
    a, b = divmod(off, 8)
    if b == 0:
        return buf[8 * a:8 * a + tm, :]
    return shbuf[b - 1, 8 * a:8 * a + tm, :]


def _head_of_lane(rows):
    return lax.broadcasted_iota(jnp.int32, (rows, DA), 1) >> 6


def _block_pick(r, lane_head):
    out = jnp.zeros((CH, DA), F32)
    for h in range(NH):
        out = jnp.where(lane_head == h, r[h * CH:(h + 1) * CH, :], out)
    return out


def _place():
    x, y, c = lax.axis_index("x"), lax.axis_index("y"), lax.axis_index("c")
    return x, y, c, 4 * x + 2 * y + c


def _dev(t):
    return (t >> 2, (t >> 1) & 1, t & 1)


class _AllGather:
    def __init__(self, w_in, w_out, wss, wrs, lsem):
        x, y, c, idx = _place()
        me, sibling = (x, y, c), (x, y, 1 - c)
        chips = [(1 - x, y), (x, 1 - y), (1 - x, 1 - y)]
        nw = len(w_in)

        def blk(p):
            return 4 * p[0] + 2 * p[1] + p[2]

        def wcopy(a, k, block, to, src=None):
            dst = w_out[a].at[blk(block)]
            return pltpu.make_async_remote_copy(src_ref=dst if src is None else src, dst_ref=dst,
                                                send_sem=wss.at[a, k], recv_sem=wrs.at[a, k],
                                                device_id=to, device_id_type=MESH)

        self.mine = [pltpu.make_async_copy(w_in[a], w_out[a].at[idx], lsem.at[a]) for a in range(nw)]
        self.first = []
        for a in range(nw):
            self.first.append(wcopy(a, 0, me, sibling, src=w_in[a]))
            self.first += [wcopy(a, 1 + j, me, (*chip, c), src=w_in[a]) for j, chip in enumerate(chips)]
        self.landed = [[wcopy(a, 1 + j, (*chip, c), me) for a in range(nw)] for j, chip in enumerate(chips)]
        self.passed = [[wcopy(a, 4 + j, (*chip, c), sibling) for a in range(nw)] for j, chip in enumerate(chips)]
        self.from_sibling = []
        for a in range(nw):
            self.from_sibling.append(wcopy(a, 0, sibling, me))
            self.from_sibling += [wcopy(a, 4 + j, (*chip, 1 - c), me) for j, chip in enumerate(chips)]

    def start(self):
        for cp in self.mine + self.first:
            cp.start()

    def forward(self):
        for land, pas in zip(self.landed, self.passed):
            for l, p in zip(land, pas):
                l.wait_recv()
                p.start()

    def finish(self):
        for cp in self.from_sibling:
            cp.wait_recv()
        for cp in self.first:
            cp.wait_send()
        for pas in self.passed:
            for p in pas:
                p.wait_send()
        for cp in self.mine:
            cp.wait()


AG_SEMS = lambda nw: [pltpu.SemaphoreType.DMA((nw, 7)), pltpu.SemaphoreType.DMA((nw, 7)),
                      pltpu.SemaphoreType.DMA((nw,))]


class _ReduceScatter:
    def __init__(self, g_in, r_out, gss, grs, lsem):
        x, y, c, idx = _place()
        me = (x, y, c)
        nw = len(g_in)
        self.mine = [pltpu.make_async_copy(g_in[a].at[idx], r_out[a].at[0], lsem.at[a]) for a in range(nw)]
        self.sends, self.recvs = [], []
        for k in range(1, NDEV):
            t = idx ^ k
            for a in range(nw):
                self.sends.append(pltpu.make_async_remote_copy(
                    src_ref=g_in[a].at[t], dst_ref=r_out[a].at[k], send_sem=gss.at[a, k - 1],
                    recv_sem=grs.at[a, k - 1], device_id=_dev(t), device_id_type=MESH))
                self.recvs.append(pltpu.make_async_remote_copy(
                    src_ref=g_in[a].at[0], dst_ref=r_out[a].at[k], send_sem=gss.at[a, k - 1],
                    recv_sem=grs.at[a, k - 1], device_id=me, device_id_type=MESH))

    def start(self):
        for cp in self.mine + self.sends:
            cp.start()

    def finish(self):
        for cp in self.recvs:
            cp.wait_recv()
        for cp in self.sends:
            cp.wait_send()
        for cp in self.mine:
            cp.wait()


RS_SEMS = AG_SEMS


def _mix_fwd(x, mod, g1, win, b_in, lng, lnb, wcat, bsf, cw, cb, gng, gnb, oga, ogb, wout, pm, ffn_shards, tm):
    T = x.shape[0]
    nt = T // tm
    nch = tm // CH
    nw = len(ffn_shards)
    fwd_step = (5 * nt) // 8
    saved = [(D, F32), (D, BF16), (2 * DB, F32), (DA, F32), (D, BF16), (D, F32), (DA, F32), (DA, F32), (DA, F32),
             (DA, F32), (CH, F32), (DB, F32), (DB, F32)]
    NSAVE = len(saved)

    def body(x_ref, mod_ref, g1_ref, win_ref, bin_ref, lng_ref, lnb_ref, wcat_ref, bsf_ref, cw_ref, cb_ref,
             gng_ref, gnb_ref, oga_ref, ogb_ref, wout_ref, pm_ref, *rest):
        sh_in = rest[:nw]
        (x1_ref, h_ref, zvg_ref, mixed_ref, y_ref, o_ref, gu_ref, dgu_ref, dgv_ref, vhat_ref, rsl_ref, yhat_ref,
         rsg_ref) = rest[nw:nw + NSAVE]
        sh_out = rest[nw + NSAVE:2 * nw + NSAVE]
        glbuf, shbuf, wss, wrs, lsem = rest[2 * nw + NSAVE:]
        i = pl.program_id(0)

        @pl.when(i == 0)
        def _():
            _AllGather(sh_in, sh_out, wss, wrs, lsem).start()

        xv = x_ref[...]
        shift1 = mod_ref[0:1, :]
        scale1 = mod_ref[1:2, :]
        gate1 = mod_ref[2:3, :]
        h = (xv * _rs(xv) * g1_ref[...]) * (1.0 + scale1) + shift1
        hb = h.astype(BF16)
        h_ref[...] = hb
        z = jnp.concatenate([_dot(hb, win_ref[j]) for j in range(NDEV)], axis=1) + bin_ref[...]
        zvg_ref[...] = z[:, 2 * DA:]
        gu, dgelu_u = _gelu_parts(z[:, 0:DA])
        gv, dgelu_v = _gelu_parts(z[:, DA:2 * DA])
        gu_ref[...] = gu
        dgu_ref[...] = dgelu_u
        dgv_ref[...] = dgelu_v
        xc = gv - jnp.mean(gv, axis=-1, keepdims=True)
        rsl = lax.rsqrt(jnp.mean(xc * xc, axis=-1, keepdims=True) + EPS)
        vhat = xc * rsl
        vhat_ref[...] = vhat
        rsl_ref[...] = jnp.broadcast_to(rsl, (tm, CH))
        vnb = (vhat * lng_ref[...] + lnb_ref[...]).astype(BF16)
        lane_head = _head_of_lane(CH)
        chunks = []
        for ci in range(nch):
            r = _dot(wcat_ref[...], vnb[ci * CH:(ci + 1) * CH, :])
            chunks.append(_block_pick(r, lane_head) + bsf_ref[...])
        mixed = jnp.concatenate(chunks, axis=0) if nch > 1 else chunks[0]
        mixed_ref[...] = mixed
        ya = gu * mixed
        gl = z[:, 2 * DA:2 * DA + DB] * _sig(z[:, 2 * DA + DB:])

        @pl.when(i == 0)
        def _():
            glbuf[0:HALO, :] = jnp.zeros((HALO, DB), F32)

        glbuf[HALO:HALO + tm, :] = gl
        _shifted_copies(glbuf, shbuf, tm)
        yc = jnp.zeros((tm, DB), F32) + cb_ref[...]
        for k in range(KW):
            yc = yc + cw_ref[k:k + 1, :] * _window(glbuf, shbuf, HALO - (KW - 1) + k, tm)
        glbuf[0:HALO, :] = gl[tm - HALO:, :]
        pmv = pm_ref[...]
        dc = yc - _grp_mean(yc, pmv)
        rsg = lax.rsqrt(_grp_mean(dc * dc, pmv) + EPS)
        yhat = dc * rsg
        yhat_ref[...] = yhat
        rsg_ref[...] = rsg
        yg = yhat * gng_ref[...] + gnb_ref[...]
        yb = yg * _sig(yg)
        na = ya * _rs(ya) * oga_ref[...]
        nb = yb * _rs(yb) * ogb_ref[...]
        yv = jnp.concatenate([na, nb], axis=1).astype(BF16)
        y_ref[...] = yv
        o = _dot(yv, wout_ref[...])
        o_ref[...] = o
        x1_ref[...] = xv + gate1 * o

        @pl.when(i == fwd_step)
        def _():
            _AllGather(sh_in, sh_out, wss, wrs, lsem).forward()

        @pl.when(i == nt - 1)
        def _():
            _AllGather(sh_in, sh_out, wss, wrs, lsem).finish()

    tile = lambda w: pl.BlockSpec((tm, w), lambda i: (i, 0))
    outs = pl.pallas_call(
        body,
        name="mix_fwd",
        grid=(nt,),
        in_specs=[tile(D), _full((8, D)), _full((1, D)), _resident((NDEV, D, WIN_B)), _full((1, DIN)),
                  _full((1, DA)), _full((1, DA)), _full((NH * CH, CH)), _full((CH, DA)), _full((HALO, DB)),
                  _full((1, DB)), _full((1, DB)), _full((1, DB)), _full((1, DA)), _full((1, DB)),
                  _resident((D, D)), _full((DB, DB))] + [HBM] * nw,
        out_specs=[tile(w) for w, _ in saved] + [HBM] * nw,
        out_shape=[jax.ShapeDtypeStruct((T, w), dt) for w, dt in saved]
                  + [jax.ShapeDtypeStruct((NDEV,) + s.shape, s.dtype) for s in ffn_shards],
        scratch_shapes=[pltpu.VMEM((HALO + tm, DB), F32), pltpu.VMEM((7, tm + SH_ROWS, DB), F32)] + AG_SEMS(nw),
        compiler_params=pltpu.CompilerParams(dimension_semantics=("arbitrary",), vmem_limit_bytes=VMEM_LIMIT),
    )(x, mod, g1, win, b_in, lng, lnb, wcat, bsf, cw, cb, gng, gnb, oga, ogb, wout, pm, *ffn_shards)
    return outs[:NSAVE], outs[NSAVE:]


def _ffn(x1, tgt, mod, g2, gf, wfi, wfo, tm):
    T = x1.shape[0]
    nt = T // tm

    def body(x1_ref, tgt_ref, mod_ref, g2_ref, gf_ref, wfi_ref, wfo_ref,
             dx1_ref, h2_ref, dgu_ref, act_ref, dxg_ref, acc_ref, g_s, u_s):
        i = pl.program_id(0)

        @pl.when(i == 0)
        def _():
            acc_ref[...] = jnp.zeros((8, D), F32)

        x1 = x1_ref[...]
        shift2 = mod_ref[3:4, :]
        scale2 = mod_ref[4:5, :]
        gate2 = mod_ref[5:6, :]
        shiftf = mod_ref[6:7, :]
        scalef = mod_ref[7:8, :]
        g2v = g2_ref[...]
        gfv = gf_ref[...]
        r2 = _rs(x1)
        xn2 = x1 * r2
        h2b = (xn2 * g2v * (1.0 + scale2) + shift2).astype(BF16)
        h2_ref[...] = h2b
        f = jnp.zeros((tm, D), F32)
        for j in range(NFB):
            g = _dot(h2b, wfi_ref[j])
            u = _dot(h2b, wfi_ref[NFB + j])
            g_s[j] = g
            u_s[j] = u
            actb = (g * _sig(g) * u).astype(BF16)
            act_ref[j] = actb
            f = f + _dot(actb, wfo_ref[j * WFI_B:(j + 1) * WFI_B, :])
        x2 = x1 + gate2 * f
        rf = _rs(x2)
        xnf = x2 * rf
        out = xnf * gfv * (1.0 + scalef) + shiftf
        e = out - tgt_ref[...]
        dout = e * (1.0 / D)
        acc_ref[7:8, :] += _colsum(e * e)
        acc_ref[0:1, :] += _colsum(dout)
        acc_ref[1:2, :] += _colsum(dout * xnf * gfv)
        acc_ref[2:3, :] += _colsum(dout * (1.0 + scalef) * xnf)
        dxnf = dout * (1.0 + scalef) * gfv
        dx2 = rf * (dxnf - xnf * jnp.mean(dxnf * xnf, axis=-1, keepdims=True))
        acc_ref[3:4, :] += _colsum(dx2 * f)
        dxgb = (dx2 * gate2).astype(BF16)
        dxg_ref[...] = dxgb
        dh2 = jnp.zeros((tm, D), F32)
        for j in range(NFB):
            dact = _dot_nt(dxgb, wfo_ref[j * WFI_B:(j + 1) * WFI_B, :])
            g = g_s[j]
            u = u_s[j]
            s = _sig(g)
            dgb = (dact * u * (s * (1.0 + g * (1.0 - s)))).astype(BF16)
            dub = (dact * (g * s)).astype(BF16)
            dgu_ref[j] = dgb
            dgu_ref[NFB + j] = dub
            dh2 = dh2 + _dot_nt(dgb, wfi_ref[j])
            dh2 = dh2 + _dot_nt(dub, wfi_ref[NFB + j])
        acc_ref[4:5, :] += _colsum(dh2)
        acc_ref[5:6, :] += _colsum(dh2 * xn2 * g2v)
        acc_ref[6:7, :] += _colsum(dh2 * (1.0 + scale2) * xn2)
        dxn2 = dh2 * (1.0 + scale2) * g2v
        dx1_ref[...] = dx2 + r2 * (dxn2 - xn2 * jnp.mean(dxn2 * xn2, axis=-1, keepdims=True))

    tile = lambda w: pl.BlockSpec((tm, w), lambda i: (i, 0))
    blocked = lambda n: pl.BlockSpec((n, tm, WFI_B), lambda i: (0, i, 0))
    return pl.pallas_call(
        body,
        name="ffn_fwd_bwd",
        grid=(nt,),
        in_specs=[tile(D), tile(D), _full((8, D)), _full((1, D)), _full((1, D)),
                  _resident((NDEV, D, WFI_B)), _resident((DFF, D))],
        out_specs=[tile(D), tile(D), blocked(NDEV), blocked(NFB), tile(D), _full((8, D))],
        out_shape=[jax.ShapeDtypeStruct((T, D), F32), jax.ShapeDtypeStruct((T, D), BF16),
                   jax.ShapeDtypeStruct((NDEV, T, WFI_B), BF16), jax.ShapeDtypeStruct((NFB, T, WFI_B), BF16),
                   jax.ShapeDtypeStruct((T, D), BF16), jax.ShapeDtypeStruct((8, D), F32)],
        scratch_shapes=[pltpu.VMEM((NFB, tm, WFI_B), F32), pltpu.VMEM((NFB, tm, WFI_B), F32)],
        compiler_params=pltpu.CompilerParams(dimension_semantics=("arbitrary",), vmem_limit_bytes=VMEM_LIMIT),
    )(x1, tgt, mod, g2, gf, wfi, wfo)


def _mix_bwd(dx1, x, zvg, mixed, o, hb, yb, gu, dgu, dgv, vhat, rslb, yhat, rsg, mod, g1, win, lng, lnb, wcat, wcat_t,
             cw, gng, gnb, oga, ogb, wout, pm, esel, after, tm):
    T = x.shape[0]
    nt = T // tm
    nch = tm // CH
    WOB = 256

    def body(dx1_ref, x_ref, zvg_ref, mixed_ref, o_ref, hb_ref, yb_ref, gu_ref, dgu_ref, dgv_ref, vhat_ref, rsl_ref,
             yhat_ref, rsg_ref, mod_ref, g1_ref, win_ref, lng_ref, lnb_ref, wcat_ref, wcatt_ref, cw_ref, gng_ref,
             gnb_ref, oga_ref, ogb_ref, wout_ref, pm_ref, esel_ref, after_ref,
             gx_ref, accv_ref, accb_ref, acca_ref, accbs_ref, accws_ref, acccw_ref, gwin_ref, gwout_ref,
             dycbuf, shbuf, bs_s, acc_win, acc_wout, st_win, st_wout):
        i = pl.program_id(0)

        @pl.when(i == 0)
        def _():
            acc_win[...] = jnp.zeros((NDEV, D, WIN_B), F32)
            acc_wout[...] = jnp.zeros((D, D), F32)
            accv_ref[...] = jnp.zeros((8, D), F32)
            accb_ref[...] = jnp.zeros((1, DIN), F32)
            acca_ref[...] = jnp.zeros((8, DA), F32)
            accws_ref[...] = jnp.zeros((NH * CH, CH), F32)
            acccw_ref[...] = jnp.zeros((HALO, DB), F32)
            bs_s[...] = jnp.zeros((CH, DA), F32)
            dycbuf[tm:tm + HALO, :] = jnp.zeros((HALO, DB), F32)

        shift1 = mod_ref[0:1, :]
        scale1 = mod_ref[1:2, :]
        gate1 = mod_ref[2:3, :]
        g1v = g1_ref[...]
        xv = x_ref[...]
        r1 = _rs(xv)
        xn1 = xv * r1
        val = zvg_ref[:, 0:DB]
        gate = zvg_ref[:, DB:]
        gu = gu_ref[...]
        dgelu_u = dgu_ref[...]
        dgelu_v = dgv_ref[...]
        vhat = vhat_ref[...]
        rsl = rsl_ref[:, 0:1]
        lngv = lng_ref[...]
        vnb = (vhat * lngv + lnb_ref[...]).astype(BF16)
        mixed = mixed_ref[...]
        ya = gu * mixed
        ra = _rs(ya)
        yan = ya * ra
        sgt = _sig(gate)
        gl = val * sgt
        pmv = pm_ref[...]
        rsg = rsg_ref[...]
        yhat = yhat_ref[...]
        gngv = gng_ref[...]
        yg = yhat * gngv + gnb_ref[...]
        sgy = _sig(yg)
        yb = yg * sgy
        rb = _rs(yb)
        ybn = yb * rb
        dx1 = dx1_ref[...]
        accv_ref[0:1, :] += _colsum(dx1 * o_ref[...])
        dogb = (dx1 * gate1).astype(BF16)
        acc_wout[...] += _dot_tn(yb_ref[...], dogb)
        dy = _dot_nt(dogb, wout_ref[...])
        dna = dy[:, 0:DA]
        dnb = dy[:, DA:]
        ogav = oga_ref[...]
        ogbv = ogb_ref[...]
        acca_ref[2:3, :] += _colsum(dna * yan)
        acca_ref[3:4, :] += _colsum(dnb * ybn)
        ta = dna * ogav
        dya = ra * (ta - yan * jnp.mean(ta * yan, axis=-1, keepdims=True))
        tb = dnb * ogbv
        dyb = rb * (tb - ybn * jnp.mean(tb * ybn, axis=-1, keepdims=True))
        dgu = dya * mixed
        dm = dya * gu
        lane_head = _head_of_lane(CH)
        dvn_chunks = []
        bs_acc = bs_s[...]
        for ci in range(nch):
            dmc = dm[ci * CH:(ci + 1) * CH, :]
            bs_acc = bs_acc + dmc
            dmcb = dmc.astype(BF16)
            dvn_chunks.append(_block_pick(_dot(wcatt_ref[...], dmcb), lane_head))
            zero = jnp.zeros((CH, DA), BF16)
            stack = jnp.concatenate([jnp.where(lane_head == h, dmcb, zero) for h in range(NH)], axis=0)
            accws_ref[...] += _dot_nt(stack, vnb[ci * CH:(ci + 1) * CH, :])
        bs_s[...] = bs_acc
        dvn = jnp.concatenate(dvn_chunks, axis=0) if nch > 1 else dvn_chunks[0]
        acca_ref[0:1, :] += _colsum(dvn * vhat)
        acca_ref[1:2, :] += _colsum(dvn)
        dvh = dvn * lngv
        dgv = rsl * (dvh - jnp.mean(dvh, axis=-1, keepdims=True)
                     - vhat * jnp.mean(dvh * vhat, axis=-1, keepdims=True))
        du = dgu * dgelu_u
        dv = dgv * dgelu_v
        dyg = dyb * (sgy * (1.0 + yg * (1.0 - sgy)))
        acca_ref[5:6, :] += _colsum(dyg * yhat)
        acca_ref[6:7, :] += _colsum(dyg)
        dyh = dyg * gngv
        dyc = rsg * (dyh - _grp_mean(dyh, pmv) - yhat * _grp_mean(dyh * yhat, pmv))
        acca_ref[4:5, :] += _colsum(dyc)
        dycbuf[0:tm, :] = dyc
        _shifted_copies(dycbuf, shbuf, tm)
        dgl = jnp.zeros((tm, DB), F32)
        for k in range(KW):
            win_k = _window(dycbuf, shbuf, KW - 1 - k, tm)
            dgl = dgl + cw_ref[k:k + 1, :] * win_k
            acccw_ref[k:k + 1, :] += _colsum(win_k * gl)
        dycbuf[tm:tm + HALO, :] = dyc[0:HALO, :]
        dval = dgl * sgt
        dgate = dgl * val * sgt * (1.0 - sgt)
        dz = jnp.concatenate([du, dv, dval, dgate], axis=1)
        accb_ref[...] += _colsum(dz)
        dzb = dz.astype(BF16)
        hbv = hb_ref[...]
        dh = jnp.zeros((tm, D), F32)
        for j in range(NDEV):
            dzj = dzb[:, j * WIN_B:(j + 1) * WIN_B]
            acc_win[j] += _dot_tn(hbv, dzj)
            dh = dh + _dot_nt(dzj, win_ref[j])
        accv_ref[1:2, :] += _colsum(dh)
        accv_ref[2:3, :] += _colsum(dh * xn1 * g1v)
        accv_ref[3:4, :] += _colsum(dh * (1.0 + scale1) * xn1)
        dxn1 = dh * (1.0 + scale1) * g1v
        gx_ref[...] = dx1 + r1 * (dxn1 - xn1 * jnp.mean(dxn1 * xn1, axis=-1, keepdims=True))

        @pl.when(i == nt - 1)
        def _():
            rows = lax.broadcasted_iota(jnp.int32, (NH * CH, CH), 0) & (CH - 1)
            cols = lax.broadcasted_iota(jnp.int32, (NH * CH, CH), 1)
            accws_ref[...] = jnp.where(cols <= rows, accws_ref[...], 0.0)
            bs = bs_s[...]
            hi = bs.astype(BF16)
            r1_ = bs - hi.astype(F32)
            mid = r1_.astype(BF16)
            lo = (r1_ - mid.astype(F32)).astype(BF16)
            ev = esel_ref[...]
            accbs_ref[...] = _dot(hi, ev) + _dot(mid, ev) + _dot(lo, ev)
            for j in range(NDEV):
                st_win[...] = acc_win[j].astype(BF16)
                pltpu.sync_copy(st_win, gwin_ref.at[j])
            for j in range(D // WOB):
                st_wout[...] = acc_wout[j * WOB:(j + 1) * WOB, :].astype(BF16)
                pltpu.sync_copy(st_wout, gwout_ref.at[pl.ds(j * WOB, WOB)])

    rev = lambda w: pl.BlockSpec((tm, w), lambda i: (nt - 1 - i, 0))
    outs = pl.pallas_call(
        body,
        name="mix_bwd",
        grid=(nt,),
        in_specs=[rev(D), rev(D), rev(2 * DB), rev(DA), rev(D), rev(D), rev(D), rev(DA), rev(DA), rev(DA), rev(DA),
                  rev(CH), rev(DB), rev(DB), _full((8, D)), _full((1, D)),
                  _resident((NDEV, D, WIN_B)), _full((1, DA)), _full((1, DA)), _full((NH * CH, CH)),
                  _full((NH * CH, CH)), _full((HALO, DB)), _full((1, DB)), _full((1, DB)), _full((1, DA)),
                  _full((1, DB)), _resident((D, D)), _full((DB, DB)), _full((DA, CH)), HBM],
        out_specs=[rev(D), _full((8, D)), _full((1, DIN)), _full((8, DA)), _full((CH, CH)),
                   _full((NH * CH, CH)), _full((HALO, DB)), HBM, HBM],
        out_shape=[jax.ShapeDtypeStruct((T, D), F32), jax.ShapeDtypeStruct((8, D), F32),
                   jax.ShapeDtypeStruct((1, DIN), F32), jax.ShapeDtypeStruct((8, DA), F32),
                   jax.ShapeDtypeStruct((CH, CH), F32), jax.ShapeDtypeStruct((NH * CH, CH), F32),
                   jax.ShapeDtypeStruct((HALO, DB), F32),
                   jax.ShapeDtypeStruct((NDEV, D, WIN_B), BF16), jax.ShapeDtypeStruct((D, D), BF16)],
        scratch_shapes=[pltpu.VMEM((tm + HALO, DB), F32), pltpu.VMEM((7, tm + SH_ROWS, DB), F32),
                        pltpu.VMEM((CH, DA), F32), pltpu.VMEM((NDEV, D, WIN_B), F32), pltpu.VMEM((D, D), F32),
                        pltpu.VMEM((D, WIN_B), BF16), pltpu.VMEM((WOB, D), BF16)],
        compiler_params=pltpu.CompilerParams(dimension_semantics=("arbitrary",), vmem_limit_bytes=VMEM_LIMIT),
    )(dx1, x, zvg, mixed, o, hb, yb, gu, dgu, dgv, vhat, rslb, yhat, rsg, mod, g1, win, lng, lnb, wcat, wcat_t, cw,
      gng, gnb, oga, ogb, wout, pm, esel, after)
    return outs[:7], outs[7:]


def _wgrad_cols(a, b, nblk, tk, name):
    T, M = a.shape
    bw = b.shape[1] // nblk
    nk = T // tk

    def body(a_ref, b_ref, o_ref, acc):
        k = pl.program_id(0)

        @pl.when(k == 0)
        def _():
            acc[...] = jnp.zeros((nblk, M, bw), F32)

        av = a_ref[...]
        for j in range(nblk):
            acc[j] += _dot_tn(av, b_ref[:, j * bw:(j + 1) * bw])

        @pl.when(k == nk - 1)
        def _():
            o_ref[...] = acc[...].astype(BF16)

    return pl.pallas_call(
        body, name=name, grid=(nk,),
        in_specs=[pl.BlockSpec((tk, M), lambda k: (k, 0)), pl.BlockSpec((tk, nblk * bw), lambda k: (k, 0))],
        out_specs=_full((nblk, M, bw)),
        out_shape=jax.ShapeDtypeStruct((nblk, M, bw), BF16),
        scratch_shapes=[pltpu.VMEM((nblk, M, bw), F32)],
        compiler_params=pltpu.CompilerParams(dimension_semantics=("arbitrary",), vmem_limit_bytes=VMEM_LIMIT),
    )(a, b)


def _wgrad_b_blocked(a, b3, per, tk, name):
    T, M = a.shape
    nb, _, bw = b3.shape
    nk = T // tk

    def body(a_ref, b_ref, o_ref, acc):
        k = pl.program_id(1)

        @pl.when(k == 0)
        def _():
            acc[...] = jnp.zeros((per, M, bw), F32)

        av = a_ref[...]
        for j in range(per):
            acc[j] += _dot_tn(av, b_ref[j])

        @pl.when(k == nk - 1)
        def _():
            o_ref[...] = acc[...].astype(BF16)

    return pl.pallas_call(
        body, name=name, grid=(nb // per, nk),
        in_specs=[pl.BlockSpec((tk, M), lambda j, k: (k, 0)), pl.BlockSpec((per, tk, bw), lambda j, k: (j, k, 0))],
        out_specs=pl.BlockSpec((per, M, bw), lambda j, k: (j, 0, 0)),
        out_shape=jax.ShapeDtypeStruct((nb, M, bw), BF16),
        scratch_shapes=[pltpu.VMEM((per, M, bw), F32)],
        compiler_params=pltpu.CompilerParams(dimension_semantics=("arbitrary", "arbitrary"),
                                             vmem_limit_bytes=VMEM_LIMIT),
    )(a, b3)


def _wgrad_a_blocked(a3, b, tk, name):
    nb, T, bw = a3.shape
    N = b.shape[1]
    nk = T // tk

    def body(a_ref, b_ref, o_ref, acc):
        k = pl.program_id(1)

        @pl.when(k == 0)
        def _():
            acc[...] = jnp.zeros((bw, N), F32)

        acc[...] += _dot_tn(a_ref[0], b_ref[...])

        @pl.when(k == nk - 1)
        def _():
            o_ref[0] = acc[...].astype(BF16)

    return pl.pallas_call(
        body, name=name, grid=(nb, nk),
        in_specs=[pl.BlockSpec((1, tk, bw), lambda j, k: (j, k, 0)), pl.BlockSpec((tk, N), lambda j, k: (k, 0))],
        out_specs=pl.BlockSpec((1, bw, N), lambda j, k: (j, 0, 0)),
        out_shape=jax.ShapeDtypeStruct((nb, bw, N), BF16),
        scratch_shapes=[pltpu.VMEM((bw, N), F32)],
        compiler_params=pltpu.CompilerParams(dimension_semantics=("arbitrary", "arbitrary"),
                                             vmem_limit_bytes=VMEM_LIMIT),
    )(a3, b)


def _small_copy(src, dst, ss, rs, k, to):
    return pltpu.make_async_remote_copy(src_ref=src, dst_ref=dst, send_sem=ss.at[k], recv_sem=rs.at[k],
                                        device_id=to, device_id_type=MESH)


def _gather(c_row, ada_w, ada_b8, ada_f_w, ada_f_b8, conv_s, shards):
    nw = len(shards)

    def body(c_ref, adaw_ref, adab_ref, adafw_ref, adafb_ref, conv_ref, *rest):
        w_in = rest[:nw]
        call_ref, cparts_ref, cfparts_ref, convg_ref = rest[nw:nw + 4]
        w_out = rest[nw + 4:2 * nw + 4]
        part_s, partf_s, wss, wrs, lsem, s1, r1, s2, r2, s3, r3, s4, r4 = rest[2 * nw + 4:]
        x, y, c, idx = _place()
        me = (x, y, c)
        ag = _AllGather(w_in, w_out, wss, wrs, lsem)
        ag.start()
        call_ref[pl.ds(idx, 1), :] = c_ref[...]
        convg_ref[idx] = conv_ref[...]
        ph1 = []
        for k in range(1, NDEV):
            to = _dev(idx ^ k)
            ph1.append(_small_copy(c_ref, call_ref.at[pl.ds(idx, 1)], s1, r1, k - 1, to))
            ph1.append(_small_copy(conv_ref, convg_ref.at[idx], s2, r2, k - 1, to))
        for cp in ph1:
            cp.start()
        for k in range(1, NDEV):
            src_dev = idx ^ k
            _small_copy(c_ref, call_ref.at[pl.ds(src_dev, 1)], s1, r1, k - 1, me).wait_recv()
            _small_copy(conv_ref, convg_ref.at[src_dev], s2, r2, k - 1, me).wait_recv()
        call = call_ref[...]
        cact = (call * _sig(call))
        part_s[...] = jnp.dot(cact, adaw_ref[...], preferred_element_type=F32,
                              precision=lax.Precision.HIGHEST) + adab_ref[pl.ds(idx, 1), :]
        partf_s[...] = jnp.dot(cact, adafw_ref[...], preferred_element_type=F32,
                               precision=lax.Precision.HIGHEST) + adafb_ref[pl.ds(idx, 1), :]
        cparts_ref[pl.ds(idx, 1), :] = part_s[pl.ds(idx, 1), :]
        cfparts_ref[pl.ds(idx, 1), :] = partf_s[pl.ds(idx, 1), :]
        ph2 = []
        for k in range(1, NDEV):
            t = idx ^ k
            ph2.append(_small_copy(part_s.at[pl.ds(t, 1)], cparts_ref.at[pl.ds(idx, 1)], s3, r3, k - 1, _dev(t)))
            ph2.append(_small_copy(partf_s.at[pl.ds(t, 1)], cfparts_ref.at[pl.ds(idx, 1)], s4, r4, k - 1, _dev(t)))
        for cp in ph2:
            cp.start()
        for k in range(1, NDEV):
            src_dev = idx ^ k
            _small_copy(part_s.at[pl.ds(0, 1)], cparts_ref.at[pl.ds(src_dev, 1)], s3, r3, k - 1, me).wait_recv()
            _small_copy(partf_s.at[pl.ds(0, 1)], cfparts_ref.at[pl.ds(src_dev, 1)], s4, r4, k - 1, me).wait_recv()
        for cp in ph1 + ph2:
            cp.wait_send()
        ag.forward()
        ag.finish()

    dma7 = pltpu.SemaphoreType.DMA((NDEV - 1,))
    outs = pl.pallas_call(
        body,
        name="gather_weights",
        in_specs=[VM] * 6 + [HBM] * nw,
        out_specs=[VM] * 4 + [HBM] * nw,
        out_shape=[jax.ShapeDtypeStruct((NDEV, D), F32), jax.ShapeDtypeStruct((NDEV, ada_w.shape[1]), F32),
                   jax.ShapeDtypeStruct((NDEV, ada_f_w.shape[1]), F32),
                   jax.ShapeDtypeStruct((NDEV,) + conv_s.shape, F32)]
                  + [jax.ShapeDtypeStruct((NDEV,) + s.shape, s.dtype) for s in shards],
        scratch_shapes=[pltpu.VMEM((NDEV, ada_w.shape[1]), F32), pltpu.VMEM((NDEV, ada_f_w.shape[1]), F32)]
                       + AG_SEMS(nw) + [dma7] * 8,
        compiler_params=pltpu.CompilerParams(vmem_limit_bytes=VMEM_LIMIT),
    )(c_row, ada_w, ada_b8, ada_f_w, ada_f_b8, conv_s, *shards)
    return outs[0], outs[1], outs[2], outs[3], outs[4:]


_VEC_AT = {
    "norm1_g": (8, 0, D), "a_ln_g": (11, 0, DA), "a_ln_b": (11, DA, DA), "a_spatial_b": (12, 0, D),
    "b_conv_b": (13, 0, DB), "b_gn_g": (13, DB, DB), "b_gn_b": (14, 0, DB), "out_norm_a_g": (14, DB, DA),
    "out_norm_b_g": (15, 0, DB), "norm2_g": (16, 0, D), "norm_f_g": (17, 0, D),
}
_LOSS_ROW = 18
_CW_ROW = 24


def _reduce_small(acc_f, acc_v, acc_b, acc_a, acc_bs, acc_cw, dws, after):
    def body(accf_ref, accv_ref, accb_ref, acca_ref, accbs_ref, acccw_ref, dws_ref, after_ref,
             vsum_ref, dcond_ref, wssum_ref, vloc, vbuf, wbuf, wown, s1, r1, s2, r2, s3, r3):
        x, y, c, idx = _place()
        me = (x, y, c)
        vloc[...] = jnp.zeros((NVEC, D), F32)
        vloc[0:1, :] = accv_ref[1:2, :]
        vloc[1:2, :] = accv_ref[2:3, :]
        vloc[2:3, :] = accv_ref[0:1, :]
        vloc[3:4, :] = accf_ref[4:5, :]
        vloc[4:5, :] = accf_ref[5:6, :]
        vloc[5:6, :] = accf_ref[3:4, :]
        vloc[6:7, :] = accf_ref[0:1, :]
        vloc[7:8, :] = accf_ref[1:2, :]
        vloc[8:9, :] = accv_ref[3:4, :]
        vloc[9:10, :] = accb_ref[:, 0:D]
        vloc[10:11, :] = accb_ref[:, D:]
        vloc[11:12, 0:DA] = acca_ref[0:1, :]
        vloc[11:12, DA:] = acca_ref[1:2, :]
        bst = accbs_ref[...].T
        for h in range(NH):
            vloc[12:13, h * CH:(h + 1) * CH] = bst[h:h + 1, :]
        vloc[13:14, 0:DB] = acca_ref[4:5, :]
        vloc[13:14, DB:] = acca_ref[5:6, :]
        vloc[14:15, 0:DB] = acca_ref[6:7, :]
        vloc[14:15, DB:] = acca_ref[2:3, :]
        vloc[15:16, 0:DB] = acca_ref[3:4, :]
        vloc[16:17, :] = accf_ref[6:7, :]
        vloc[17:18, :] = accf_ref[2:3, :]
        vloc[_LOSS_ROW:_LOSS_ROW + 1, :] = accf_ref[7:8, :]
        vloc[_CW_ROW:_CW_ROW + HALO, 0:DB] = acccw_ref[...]
        vbuf[idx] = vloc[...]
        rows_of = lambda t: pl.ds(pl.multiple_of(t * CH, CH), CH)
        wbuf[0] = dws_ref[rows_of(idx), :]
        sm = []
        for k in range(1, NDEV):
            t = idx ^ k
            sm.append(_small_copy(vloc, vbuf.at[idx], s1, r1, k - 1, _dev(t)))
            sm.append(_small_copy(dws_ref.at[rows_of(t)], wbuf.at[k], s2, r2, k - 1, _dev(t)))
        for cp in sm:
            cp.start()
        for k in range(1, NDEV):
            _small_copy(dws_ref.at[rows_of(0)], wbuf.at[k], s2, r2, k - 1, me).wait_recv()
        ws = wbuf[0]
        for k in range(1, NDEV):
            ws = ws + wbuf[k]
        wown[...] = ws
        wssum_ref[rows_of(idx), :] = ws
        ag = [_small_copy(wown, wssum_ref.at[rows_of(idx)], s3, r3, k - 1, _dev(idx ^ k)) for k in range(1, NDEV)]
        for cp in ag:
            cp.start()
        for k in range(1, NDEV):
            _small_copy(vloc, vbuf.at[idx ^ k], s1, r1, k - 1, me).wait_recv()
        vs = vbuf[0]
        for d in range(1, NDEV):
            vs = vs + vbuf[d]
        vsum_ref[...] = vs
        for d in range(NDEV):
            dcond_ref[d] = vbuf[d, 0:8, :]
        for k in range(1, NDEV):
            _small_copy(wown, wssum_ref.at[rows_of(idx ^ k)], s3, r3, k - 1, me).wait_recv()
        for cp in sm + ag:
            cp.wait_send()

    dma7 = pltpu.SemaphoreType.DMA((NDEV - 1,))
    return pl.pallas_call(
        body,
        name="reduce_small",
        in_specs=[VM] * 7 + [HBM],
        out_specs=[VM, VM, VM],
        out_shape=[jax.ShapeDtypeStruct((NVEC, D), F32), jax.ShapeDtypeStruct((NDEV, 8, D), F32),
                   jax.ShapeDtypeStruct(dws.shape, F32)],
        scratch_shapes=[pltpu.VMEM((NVEC, D), F32), pltpu.VMEM((NDEV, NVEC, D), F32),
                        pltpu.VMEM((NDEV, CH, CH), F32), pltpu.VMEM((CH, CH), F32)] + [dma7] * 6,
        compiler_params=pltpu.CompilerParams(vmem_limit_bytes=VMEM_LIMIT),
    )(acc_f, acc_v, acc_b, acc_a, acc_bs, acc_cw, dws, after)


HBM_ONLY = pl.BlockSpec(memory_space=pltpu.HBM)
SEM = pl.BlockSpec(memory_space=pltpu.SEMAPHORE)
EFFECT = pltpu.SideEffectType.DATAFLOW_SIDE_EFFECTING


def _rs_copies(g_refs, land_refs, sems):
    x, y, c, idx = _place()
    cps = []
    for k in range(1, NDEV):
        t = idx ^ k
        for a in range(len(g_refs)):
            n = len(cps)
            cps.append(pltpu.make_async_remote_copy(
                src_ref=g_refs[a].at[t], dst_ref=land_refs[a].at[k - 1], send_sem=sems[2 * n],
                recv_sem=sems[2 * n + 1], device_id=_dev(t), device_id_type=MESH))
    return cps


def _rs_start(grads, name):
    nw = len(grads)
    nsem = 2 * nw * (NDEV - 1)
    lands = [lax.empty((NDEV - 1,) + g.shape[1:], g.dtype) for g in grads]

    def body(*refs):
        g_refs, land_refs = refs[:nw], refs[nw:2 * nw]
        sems = refs[2 * nw:2 * nw + nsem]
        token = refs[-1]
        for cp in _rs_copies(g_refs, land_refs, sems):
            cp.start()
        token[...] = jnp.zeros_like(token)

    outs = pl.pallas_call(
        body, name=name,
        out_shape=(*[pltpu.SemaphoreType.DMA(())] * nsem,
                   *[pltpu.HBM(g.shape, g.dtype) for g in grads], *[pltpu.HBM(l.shape, l.dtype) for l in lands],
                   jax.ShapeDtypeStruct((8, CH), F32)),
        in_specs=[HBM_ONLY] * (2 * nw),
        out_specs=(*[SEM] * nsem, *[HBM_ONLY] * (2 * nw), VM),
        input_output_aliases={i: nsem + i for i in range(2 * nw)},
        compiler_params=pltpu.CompilerParams(has_side_effects=EFFECT),
    )(*[pltpu.with_memory_space_constraint(g, pltpu.HBM) for g in grads],
      *[pltpu.with_memory_space_constraint(l, pltpu.HBM) for l in lands])
    return outs[:nsem], outs[nsem:nsem + nw], outs[nsem + nw:nsem + 2 * nw], outs[-1]


def _rs_wait(sems, g_thru, land_thru, after, name):
    nw = len(g_thru)
    nsem = len(sems)

    def body(*refs):
        g_refs, land_refs = refs[:nw], refs[nw:2 * nw]
        for cp in _rs_copies(g_refs, land_refs, refs[2 * nw:2 * nw + nsem]):
            cp.wait_send()
            cp.wait_recv()

    outs = pl.pallas_call(
        body, name=name,
        out_shape=tuple(pltpu.HBM(a.shape, a.dtype) for a in list(g_thru) + list(land_thru)),
        in_specs=[HBM_ONLY] * (2 * nw) + [SEM] * nsem + [HBM] * len(after),
        out_specs=tuple([HBM_ONLY] * (2 * nw)),
        input_output_aliases={i: i for i in range(2 * nw)},
        compiler_params=pltpu.CompilerParams(has_side_effects=EFFECT),
    )(*g_thru, *land_thru, *sems, *after)
    return outs[:nw], outs[nw:]


def _adamw(w, g, m, v):
    m2 = ADAM_B1 * m + (1.0 - ADAM_B1) * g
    v2 = ADAM_B2 * v + (1.0 - ADAM_B2) * (g * g)
    m_hat = m2 / (1.0 - ADAM_B1 ** ADAM_STEP)
    v_hat = v2 / (1.0 - ADAM_B2 ** ADAM_STEP)
    delta = -ADAM_LR * (m_hat / (jnp.sqrt(v_hat) + ADAM_EPS) + ADAM_WD * w)
    return delta, m2, v2


def _adam_big(r, w, m, v, rb, name, own=None, after=None):
    R, C = w.shape
    ns = r.shape[0]

    def body(*refs):
        r_ref = refs[0]
        own_ref = refs[1] if own is not None else None
        w_ref, m_ref, v_ref, g_ref, d_ref, m2_ref, v2_ref = refs[len(refs) - 7:]
        g = r_ref[0].astype(F32) if own is None else own_ref[...].astype(F32) + r_ref[0].astype(F32)
        for k in range(1, ns):
            g = g + r_ref[k].astype(F32)
        g_ref[...] = g
        d_ref[...], m2_ref[...], v2_ref[...] = _adamw(w_ref[...], g, m_ref[...], v_ref[...])

    t2 = pl.BlockSpec((rb, C), lambda i: (i, 0))
    sd = jax.ShapeDtypeStruct((R, C), F32)
    extra_specs = ([t2] if own is not None else []) + ([HBM] if after is not None else [])
    extra = ([own] if own is not None else []) + ([after] if after is not None else [])
    return pl.pallas_call(
        body, name=name, grid=(R // rb,),
        in_specs=[pl.BlockSpec((ns, rb, C), lambda i: (0, i, 0))] + extra_specs + [t2, t2, t2],
        out_specs=[t2, t2, t2, t2], out_shape=[sd, sd, sd, sd],
        compiler_params=pltpu.CompilerParams(dimension_semantics=("arbitrary",), vmem_limit_bytes=VMEM_LIMIT),
    )(r, *extra, w, m, v)


def _adam_ada(cact_t, dcs, w, m, v, rb, name):
    R, C = w.shape

    def body(ct_ref, dc_ref, w_ref, m_ref, v_ref, g_ref, d_ref, m2_ref, v2_ref):
        g = jnp.dot(ct_ref[...], dc_ref[...], preferred_element_type=F32, precision=lax.Precision.HIGHEST)
        g_ref[...] = g
        d_ref[...], m2_ref[...], v2_ref[...] = _adamw(w_ref[...], g, m_ref[...], v_ref[...])

    t2 = pl.BlockSpec((rb, C), lambda i: (i, 0))
    sd = jax.ShapeDtypeStruct((R, C), F32)
    return pl.pallas_call(
        body, name=name, grid=(R // rb,),
        in_specs=[pl.BlockSpec((rb, NDEV), lambda i: (i, 0)), _full((NDEV, C)), t2, t2, t2],
        out_specs=[t2, t2, t2, t2], out_shape=[sd, sd, sd, sd],
        compiler_params=pltpu.CompilerParams(dimension_semantics=("arbitrary",), vmem_limit_bytes=VMEM_LIMIT),
    )(cact_t, dcs, w, m, v)


_SMALL = ["ada_b", "ada_f_b", "norm1_g", "b_in", "a_ln_g", "a_ln_b", "a_spatial_b", "b_conv_b", "b_gn_g", "b_gn_b",
          "out_norm_a_g", "out_norm_b_g", "norm2_g", "norm_f_g", "a_spatial_w", "b_conv_w"]


def _adam_small(vsum, wssum, gcw, params):
    names = _SMALL
    flat = []
    for n in names:
        flat += list(params[n])

    def body(vs_ref, ws_ref, gcw_ref, *rest):
        ins = rest[:3 * len(names)]
        outs = rest[3 * len(names):]
        for pi, n in enumerate(names):
            w_ref, m_ref, v_ref = ins[3 * pi:3 * pi + 3]
            g_ref, d_ref, m2_ref, v2_ref = outs[4 * pi:4 * pi + 4]
            if n in ("ada_b", "ada_f_b", "b_in"):
                row0 = {"ada_b": 0, "ada_f_b": 6, "b_in": 9}[n]
                pieces = [(vs_ref[row0 + r:row0 + r + 1, :], slice(r * D, (r + 1) * D))
                          for r in range(w_ref.shape[1] // D)]
            elif n == "a_spatial_w":
                pieces = [(ws_ref[...], slice(None))]
            elif n == "b_conv_w":
                pieces = [(gcw_ref[...], slice(None))]
            else:
                row, off, width = _VEC_AT[n]
                pieces = [(vs_ref[row:row + 1, off:off + width], slice(None))]
            for g, cs in pieces:
                g_ref[:, cs] = g
                d_ref[:, cs], m2_ref[:, cs], v2_ref[:, cs] = _adamw(w_ref[:, cs], g, m_ref[:, cs], v_ref[:, cs])

    out_shape = []
    for n in names:
        out_shape += [jax.ShapeDtypeStruct(params[n][0].shape, F32)] * 4
    outs = pl.pallas_call(
        body, name="adam_small",
        in_specs=[VM] * (3 + len(flat)), out_specs=[VM] * len(out_shape), out_shape=out_shape,
        compiler_params=pltpu.CompilerParams(vmem_limit_bytes=VMEM_LIMIT),
    )(vsum, wssum, gcw, *flat)
    return {n: outs[4 * pi:4 * pi + 4] for pi, n in enumerate(names)}


def _token_tile(T, want):
    return want if T % want == 0 else T


def kernel(x, c, ada_w, ada_b, norm1_g, w_in, b_in, a_ln_g, a_ln_b, a_spatial_w, a_spatial_b, b_conv_w, b_conv_b, b_gn_g, b_gn_b, out_norm_a_g, out_norm_b_g, w_out, norm2_g, w_ffn_in, w_ffn_out, ada_f_w, ada_f_b, norm_f_g, loss_target, m_ada_w, m_ada_b, m_norm1_g, m_w_in, m_b_in, m_a_ln_g, m_a_ln_b, m_a_spatial_w, m_a_spatial_b, m_b_conv_w, m_b_conv_b, m_b_gn_g, m_b_gn_b, m_out_norm_a_g, m_out_norm_b_g, m_w_out, m_norm2_g, m_w_ffn_in, m_w_ffn_out, m_ada_f_w, m_ada_f_b, m_norm_f_g, v_ada_w, v_ada_b, v_norm1_g, v_w_in, v_b_in, v_a_ln_g, v_a_ln_b, v_a_spatial_w, v_a_spatial_b, v_b_conv_w, v_b_conv_b, v_b_gn_g, v_b_gn_b, v_out_norm_a_g, v_out_norm_b_g, v_w_out, v_norm2_g, v_w_ffn_in, v_w_ffn_out, v_ada_f_w, v_ada_f_b, v_norm_f_g):
    T = x.shape[1]
    idx = 4 * lax.axis_index("x") + 2 * lax.axis_index("y") + lax.axis_index("c")
    x2d = x.reshape(T, D)
    tgt = loss_target.reshape(T, D)

    conv_s = jnp.pad(b_conv_w[0], ((0, HALO - KW), (0, 0)))
    call, cparts, cfparts, convg, (win_g, wout_g) = _gather(
        c, ada_w[0], ada_b.reshape(NDEV, -1), ada_f_w, ada_f_b.reshape(NDEV, -1), conv_s,
        [w_in[0].astype(BF16), w_out[0].astype(BF16)])
    mod = jnp.concatenate([cparts.reshape(6, D), cfparts.reshape(2, D)], axis=0)
    wout = wout_g.reshape(D, D)
    cw = jnp.transpose(convg, (1, 0, 2)).reshape(HALO, DB)

    tril = jnp.tril(jnp.ones((CH, CH), dtype=bool))
    wsm = jnp.where(tril[None], a_spatial_w[0], 0.0).astype(BF16)
    wcat = wsm.reshape(NH * CH, CH)
    wcat_t = jnp.transpose(wsm, (0, 2, 1)).reshape(NH * CH, CH)
    bsf = jnp.repeat(a_spatial_b[0].T, DA // NH, axis=1)
    lane = jnp.arange(DB)
    pm = jnp.where((lane[:, None] >> 6) == (lane[None, :] >> 6), 1.0 / 64.0, 0.0).astype(BF16)
    esel = jnp.where((lane[:, None] >> 6) == jnp.arange(CH)[None, :], 1.0, 0.0).astype(BF16)

    tm = _token_tile(T, 256)
    tk = _token_tile(T, 1024)
    (x1, hb, zvg, mixed, yb, o, gu, dgelu_u, dgelu_v, vhat, rslb, yhat, rsg), (wfi_g, wfo_g) = _mix_fwd(
        x2d, mod, norm1_g, win_g, b_in, a_ln_g, a_ln_b, wcat, bsf, cw, b_conv_b, b_gn_g, b_gn_b, out_norm_a_g,
        out_norm_b_g, wout, pm, [w_ffn_in[0].astype(BF16), w_ffn_out[0].astype(BF16)], _token_tile(T, 512))
    dx1, h2b, dgu, act, dxg, acc_f = _ffn(x1, tgt, mod, norm2_g, norm_f_g.reshape(1, D), wfi_g,
                                          wfo_g.reshape(DFF, D), tm)
    g_wfi = _wgrad_b_blocked(h2b, dgu, 2, tk, "wgrad_ffn_in")
    g_wfo = _wgrad_a_blocked(act, dxg, tk, "wgrad_ffn_out").reshape(NDEV, DFF // NDEV, D)
    f_sems, f_thru, f_land, f_token = _rs_start([g_wfi, g_wfo], "rs_ffn_start")
    (gx, acc_v, acc_b, acc_a, acc_bs, acc_ws, acc_cw), (g_win, g_wout) = _mix_bwd(
        dx1, x2d, zvg, mixed, o, hb, yb, gu, dgelu_u, dgelu_v, vhat, rslb, yhat, rsg, mod, norm1_g, win_g, a_ln_g,
        a_ln_b, wcat, wcat_t, cw, b_gn_g, b_gn_b, out_norm_a_g, out_norm_b_g, wout, pm, esel, f_token, tm)
    (g_wfi_d, g_wfo_d), (r_wfi, r_wfo) = _rs_wait(f_sems, f_thru, f_land, [acc_v], "rs_ffn_wait")
    g_wout = g_wout.reshape(NDEV, D // NDEV, D)

    sems, g_thru, land_thru, token = _rs_start([g_win, g_wout], "rs_mix_start")
    vsum, dcond_all, wssum = _reduce_small(acc_f, acc_v, acc_b, acc_a, acc_bs, acc_cw, acc_ws, token)

    own = lambda g: lax.dynamic_index_in_dim(g, idx, 0, keepdims=False)
    res = {}
    res["w_ffn_in"] = _adam_big(r_wfi, w_ffn_in[0], m_w_ffn_in[0], v_w_ffn_in[0], 256, "adam_w_ffn_in",
                                own=own(g_wfi_d), after=token)
    res["w_ffn_out"] = _adam_big(r_wfo, w_ffn_out[0], m_w_ffn_out[0], v_w_ffn_out[0], DFF // NDEV // 2,
                                 "adam_w_ffn_out", own=own(g_wfo_d), after=token)
    cact_t = (call * jax.nn.sigmoid(call)).T
    dcond = dcond_all.reshape(NDEV, 8 * D)
    nada = ada_w.shape[2]
    nadf = ada_f_w.shape[1]
    dcs = lax.dynamic_slice(dcond, (0, idx * nada), (NDEV, nada))
    dcfs = lax.dynamic_slice(dcond, (0, 6 * D + idx * nadf), (NDEV, nadf))
    res["ada_w"] = _adam_ada(cact_t, dcs, ada_w[0], m_ada_w[0], v_ada_w[0], 256, "adam_ada_w")
    res["ada_f_w"] = _adam_ada(cact_t, dcfs, ada_f_w, m_ada_f_w, v_ada_f_w, 256, "adam_ada_f_w")
    ncw = b_conv_w.shape[2]
    gcw = lax.dynamic_slice(vsum, (_CW_ROW, idx * ncw), (KW, ncw))
    two = lambda a: a.reshape(1, -1) if a.ndim == 1 else a.reshape(-1, a.shape[-1])
    small_in = {
        "ada_b": (ada_b, m_ada_b, v_ada_b), "ada_f_b": (ada_f_b, m_ada_f_b, v_ada_f_b),
        "norm1_g": (norm1_g, m_norm1_g, v_norm1_g), "b_in": (b_in, m_b_in, v_b_in),
        "a_ln_g": (a_ln_g, m_a_ln_g, v_a_ln_g), "a_ln_b": (a_ln_b, m_a_ln_b, v_a_ln_b),
        "a_spatial_b": (a_spatial_b.reshape(1, D), m_a_spatial_b.reshape(1, D), v_a_spatial_b.reshape(1, D)),
        "b_conv_b": (b_conv_b, m_b_conv_b, v_b_conv_b), "b_gn_g": (b_gn_g, m_b_gn_g, v_b_gn_g),
        "b_gn_b": (b_gn_b, m_b_gn_b, v_b_gn_b), "out_norm_a_g": (out_norm_a_g, m_out_norm_a_g, v_out_norm_a_g),
        "out_norm_b_g": (out_norm_b_g, m_out_norm_b_g, v_out_norm_b_g),
        "norm2_g": (norm2_g, m_norm2_g, v_norm2_g), "norm_f_g": (norm_f_g, m_norm_f_g, v_norm_f_g),
        "a_spatial_w": (a_spatial_w, m_a_spatial_w, v_a_spatial_w),
        "b_conv_w": (b_conv_w[0], m_b_conv_w[0], v_b_conv_w[0]),
    }
    small_in = {n: tuple(two(a) for a in t) for n, t in small_in.items()}
    res.update(_adam_small(vsum, wssum, gcw, small_in))
    (g_win_d, g_wout_d), (r_win, r_wout) = _rs_wait(
        sems, g_thru, land_thru,
        [res["w_ffn_in"][0], res["w_ffn_out"][0], res["ada_w"][0], res["ada_f_w"][0], res["norm_f_g"][0]],
        "rs_mix_wait")
    res["w_in"] = _adam_big(r_win, w_in[0], m_w_in[0], v_w_in[0], 256, "adam_w_in", own=own(g_win_d))
    res["w_out"] = _adam_big(r_wout, w_out[0], m_w_out[0], v_w_out[0], D // NDEV, "adam_w_out", own=own(g_wout_d))

    loss = 0.5 / D * jnp.sum(vsum[_LOSS_ROW])
    shapes = {"ada_w": ada_w, "ada_b": ada_b, "norm1_g": norm1_g, "w_in": w_in, "b_in": b_in, "a_ln_g": a_ln_g,
              "a_ln_b": a_ln_b, "a_spatial_w": a_spatial_w, "a_spatial_b": a_spatial_b, "b_conv_w": b_conv_w,
              "b_conv_b": b_conv_b, "b_gn_g": b_gn_g, "b_gn_b": b_gn_b, "out_norm_a_g": out_norm_a_g,
              "out_norm_b_g": out_norm_b_g, "w_out": w_out, "norm2_g": norm2_g, "w_ffn_in": w_ffn_in,
              "w_ffn_out": w_ffn_out, "ada_f_w": ada_f_w, "ada_f_b": ada_f_b, "norm_f_g": norm_f_g}
    order = list(shapes)
    outs = [loss, gx.reshape(x.shape)]
    for which in range(4):
        outs += [res[n][which].reshape(shapes[n].shape) for n in order]
    return tuple(outs)
```

```python
import math

import jax
import jax.numpy as jnp
from jax import lax
from jax.experimental import pallas as pl
from jax.experimental.pallas import tpu as pltpu

F32 = jnp.float32
BF16 = jnp.bfloat16

D = 1024
DA = 512
DB = 512
DIN = 2048
DFF = 2816
NH = 8
CH = 128
KW = 31
HALO = 32
NDEV = 8
WIN_B = DIN // NDEV
WFI_B = 2 * DFF // NDEV
NFB = DFF // WFI_B
EPS = 1e-6
NVEC = 40
VMEM_LIMIT = 56 * 1024 * 1024
FFN_VMEM_LIMIT = 62 * 1024 * 1024

ADAM_LR, ADAM_B1, ADAM_B2, ADAM_EPS, ADAM_WD, ADAM_STEP = 0.001, 0.9, 0.999, 1e-08, 0.01, 10

MESH = pl.DeviceIdType.MESH


def _dot(a, b):
    return jnp.dot(a, b, preferred_element_type=F32)


def _dot_nt(a, b):
    return lax.dot_general(a, b, (((1,), (1,)), ((), ())), preferred_element_type=F32)


def _dot_tn(a, b):
    return lax.dot_general(a, b, (((0,), (0,)), ((), ())), preferred_element_type=F32)


def _rs(v):
    return lax.rsqrt(jnp.mean(v * v, axis=-1, keepdims=True) + EPS)


def _sig(v):
    return 1.0 / (1.0 + jnp.exp(-v))


_INV_SQRT2 = 1.0 / math.sqrt(2.0)
_INV_SQRT2PI = 1.0 / math.sqrt(2.0 * math.pi)


def _gelu_parts(v):
    cdf = 0.5 * (1.0 + lax.erf(v * _INV_SQRT2))
    pdf = jnp.exp(-0.5 * v * v) * _INV_SQRT2PI
    return v * cdf, cdf + v * pdf


def _grp_mean(v, pm):
    hi = v.astype(BF16)
    lo = (v - hi.astype(F32)).astype(BF16)
    return _dot(hi, pm) + _dot(lo, pm)


def _colsum(v):
    return jnp.sum(v, axis=0, keepdims=True)


def _full(shape):
    nd = len(shape)
    return pl.BlockSpec(shape, lambda *_: (0,) * nd)


def _resident(shape):
    nd = len(shape)
    return pl.BlockSpec(shape, lambda *_: (0,) * nd, pipeline_mode=pl.Buffered(1))


HBM = pl.BlockSpec(memory_space=pl.ANY)
VM = pl.BlockSpec(memory_space=pltpu.VMEM)


SH_ROWS = HALO - 8


def _shifted_copies(buf, shbuf, tm):
    for b in range(1, 8):
        shbuf[b - 1] = buf[b:b + tm + SH_ROWS, :]


def _window(buf, shbuf, off, tm):
    a, b = divmod(off, 8)
    if b == 0:
        return buf[8 * a:8 * a + tm, :]
    return shbuf[b - 1, 8 * a:8 * a + tm, :]


def _head_of_lane(rows):
    return lax.broadcasted_iota(jnp.int32, (rows, DA), 1) >> 6


def _block_pick(r, lane_head):
    out = jnp.zeros((CH, DA), F32)
    for h in range(NH):
        out = jnp.where(lane_head == h, r[h * CH:(h + 1) * CH, :], out)
    return out


def _place():
    x, y, c = lax.axis_index("x"), lax.axis_index("y"), lax.axis_index("c")
    return x, y, c, 4 * x + 2 * y + c


def _dev(t):
    return (t >> 2, (t >> 1) & 1, t & 1)


class _AllGather:
    def __init__(self, w_in, w_out, wss, wrs, lsem):
        x, y, c, idx = _place()
        me, sibling = (x, y, c), (x, y, 1 - c)
        chips = [(1 - x, y), (x, 1 - y), (1 - x, 1 - y)]
        nw = len(w_in)

        def blk(p):
            return 4 * p[0] + 2 * p[1] + p[2]

        def wcopy(a, k, block, to, src=None):
            dst = w_out[a].at[blk(block)]
            return pltpu.make_async_remote_copy(src_ref=dst if src is None else src, dst_ref=dst,
                                                send_sem=wss.at[a, k], recv_sem=wrs.at[a, k],
                                                device_id=to, device_id_type=MESH)

        self.mine = [pltpu.make_async_copy(w_in[a], w_out[a].at[idx], lsem.at[a]) for a in range(nw)]
        self.first = []
        for a in range(nw):
            self.first.append(wcopy(a, 0, me, sibling, src=w_in[a]))
            self.first += [wcopy(a, 1 + j, me, (*chip, c), src=w_in[a]) for j, chip in enumerate(chips)]
        self.landed = [[wcopy(a, 1 + j, (*chip, c), me) for a in range(nw)] for j, chip in enumerate(chips)]
        self.passed = [[wcopy(a, 4 + j, (*chip, c), sibling) for a in range(nw)] for j, chip in enumerate(chips)]
        self.from_sibling = []
        for a in range(nw):
            self.from_sibling.append(wcopy(a, 0, sibling, me))
            self.from_sibling += [wcopy(a, 4 + j, (*chip, 1 - c), me) for j, chip in enumerate(chips)]

    def start(self):
        for cp in self.mine + self.first:
            cp.start()

    def forward(self):
        for land, pas in zip(self.landed, self.passed):
            for l, p in zip(land, pas):
                l.wait_recv()
                p.start()

    def finish(self):
        for cp in self.from_sibling:
            cp.wait_recv()
        for cp in self.first:
            cp.wait_send()
        for pas in self.passed:
            for p in pas:
                p.wait_send()
        for cp in self.mine:
            cp.wait()


AG_SEMS = lambda nw: [pltpu.SemaphoreType.DMA((nw, 7)), pltpu.SemaphoreType.DMA((nw, 7)),
                      pltpu.SemaphoreType.DMA((nw,))]


class _ReduceScatter:
    def __init__(self, g_in, r_out, gss, grs, lsem):
        x, y, c, idx = _place()
        me = (x, y, c)
        nw = len(g_in)
        self.mine = [pltpu.make_async_copy(g_in[a].at[idx], r_out[a].at[0], lsem.at[a]) for a in range(nw)]
        self.sends, self.recvs = [], []
        for k in range(1, NDEV):
            t = idx ^ k
            for a in range(nw):
                self.sends.append(pltpu.make_async_remote_copy(
                    src_ref=g_in[a].at[t], dst_ref=r_out[a].at[k], send_sem=gss.at[a, k - 1],
                    recv_sem=grs.at[a, k - 1], device_id=_dev(t), device_id_type=MESH))
                self.recvs.append(pltpu.make_async_remote_copy(
                    src_ref=g_in[a].at[0], dst_ref=r_out[a].at[k], send_sem=gss.at[a, k - 1],
                    recv_sem=grs.at[a, k - 1], device_id=me, device_id_type=MESH))

    def start(self):
        for cp in self.mine + self.sends:
            cp.start()

    def finish(self):
        for cp in self.recvs:
            cp.wait_recv()
        for cp in self.sends:
            cp.wait_send()
        for cp in self.mine:
            cp.wait()


RS_SEMS = AG_SEMS


def _mix_fwd(x, mod, g1, win, b_in, lng, lnb, wcat, bsf, cw, cb, gng, gnb, oga, ogb, wout, pm, ffn_shards, tm):
    T = x.shape[0]
    nt = T // tm
    nch = tm // CH
    nw = len(ffn_shards)
    fwd_step = (5 * nt) // 8
    saved = [(D, F32), (D, BF16), (2 * DB, F32), (DA, F32), (D, BF16), (D, F32), (DA, F32), (DA, F32), (DA, F32),
             (DA, F32), (CH, F32), (DB, F32), (DB, F32)]
    NSAVE = len(saved)

    def body(x_ref, mod_ref, g1_ref, win_ref, bin_ref, lng_ref, lnb_ref, wcat_ref, bsf_ref, cw_ref, cb_ref,
             gng_ref, gnb_ref, oga_ref, ogb_ref, wout_ref, pm_ref, *rest):
        sh_in = rest[:nw]
        (x1_ref, h_ref, zvg_ref, mixed_ref, y_ref, o_ref, gu_ref, dgu_ref, dgv_ref, vhat_ref, rsl_ref, yhat_ref,
         rsg_ref) = rest[nw:nw + NSAVE]
        sh_out = rest[nw + NSAVE:2 * nw + NSAVE]
        glbuf, shbuf, wss, wrs, lsem = rest[2 * nw + NSAVE:]
        i = pl.program_id(0)

        @pl.when(i == 0)
        def _():
            _AllGather(sh_in, sh_out, wss, wrs, lsem).start()

        xv = x_ref[...]
        shift1 = mod_ref[0:1, :]
        scale1 = mod_ref[1:2, :]
        gate1 = mod_ref[2:3, :]
        h = (xv * _rs(xv) * g1_ref[...]) * (1.0 + scale1) + shift1
        hb = h.astype(BF16)
        h_ref[...] = hb
        z = jnp.concatenate([_dot(hb, win_ref[j]) for j in range(NDEV)], axis=1) + bin_ref[...]
        zvg_ref[...] = z[:, 2 * DA:]
        gu, dgelu_u = _gelu_parts(z[:, 0:DA])
        gv, dgelu_v = _gelu_parts(z[:, DA:2 * DA])
        gu_ref[...] = gu
        dgu_ref[...] = dgelu_u
        dgv_ref[...] = dgelu_v
        xc = gv - jnp.mean(gv, axis=-1, keepdims=True)
        rsl = lax.rsqrt(jnp.mean(xc * xc, axis=-1, keepdims=True) + EPS)
        vhat = xc * rsl
        vhat_ref[...] = vhat
        rsl_ref[...] = jnp.broadcast_to(rsl, (tm, CH))
        vnb = (vhat * lng_ref[...] + lnb_ref[...]).astype(BF16)
        lane_head = _head_of_lane(CH)
        chunks = []
        for ci in range(nch):
            r = _dot(wcat_ref[...], vnb[ci * CH:(ci + 1) * CH, :])
            chunks.append(_block_pick(r, lane_head) + bsf_ref[...])
        mixed = jnp.concatenate(chunks, axis=0) if nch > 1 else chunks[0]
        mixed_ref[...] = mixed
        ya = gu * mixed
        gl = z[:, 2 * DA:2 * DA + DB] * _sig(z[:, 2 * DA + DB:])

        @pl.when(i == 0)
        def _():
            glbuf[0:HALO, :] = jnp.zeros((HALO, DB), F32)

        glbuf[HALO:HALO + tm, :] = gl
        _shifted_copies(glbuf, shbuf, tm)
        yc = jnp.zeros((tm, DB), F32) + cb_ref[...]
        for k in range(KW):
            yc = yc + cw_ref[k:k + 1, :] * _window(glbuf, shbuf, HALO - (KW - 1) + k, tm)
        glbuf[0:HALO, :] = gl[tm - HALO:, :]
        pmv = pm_ref[...]
        dc = yc - _grp_mean(yc, pmv)
        rsg = lax.rsqrt(_grp_mean(dc * dc, pmv) + EPS)
        yhat = dc * rsg
        yhat_ref[...] = yhat
        rsg_ref[...] = rsg
        yg = yhat * gng_ref[...] + gnb_ref[...]
        yb = yg * _sig(yg)
        na = ya * _rs(ya) * oga_ref[...]
        nb = yb * _rs(yb) * ogb_ref[...]
        yv = jnp.concatenate([na, nb], axis=1).astype(BF16)
        y_ref[...] = yv
        o = _dot(yv, wout_ref[...])
        o_ref[...] = o
        x1_ref[...] = xv + gate1 * o

        @pl.when(i == fwd_step)
        def _():
            _AllGather(sh_in, sh_out, wss, wrs, lsem).forward()

        @pl.when(i == nt - 1)
        def _():
            _AllGather(sh_in, sh_out, wss, wrs, lsem).finish()

    tile = lambda w: pl.BlockSpec((tm, w), lambda i: (i, 0))
    outs = pl.pallas_call(
        body,
        name="mix_fwd",
        grid=(nt,),
        in_specs=[tile(D), _full((8, D)), _full((1, D)), _resident((NDEV, D, WIN_B)), _full((1, DIN)),
                  _full((1, DA)), _full((1, DA)), _full((NH * CH, CH)), _full((CH, DA)), _full((HALO, DB)),
                  _full((1, DB)), _full((1, DB)), _full((1, DB)), _full((1, DA)), _full((1, DB)),
                  _resident((D, D)), _full((DB, DB))] + [HBM] * nw,
        out_specs=[tile(w) for w, _ in saved] + [HBM] * nw,
        out_shape=[jax.ShapeDtypeStruct((T, w), dt) for w, dt in saved]
                  + [jax.ShapeDtypeStruct((NDEV,) + s.shape, s.dtype) for s in ffn_shards],
        scratch_shapes=[pltpu.VMEM((HALO + tm, DB), F32), pltpu.VMEM((7, tm + SH_ROWS, DB), F32)] + AG_SEMS(nw),
        compiler_params=pltpu.CompilerParams(dimension_semantics=("arbitrary",), vmem_limit_bytes=VMEM_LIMIT),
    )(x, mod, g1, win, b_in, lng, lnb, wcat, bsf, cw, cb, gng, gnb, oga, ogb, wout, pm, *ffn_shards)
    return outs[:NSAVE], outs[NSAVE:]


def _ffn(x1, tgt, mod, g2, gf, wfi, wfo, tm):
    T = x1.shape[0]
    nt = T // tm

    def body(x1_ref, tgt_ref, mod_ref, g2_ref, gf_ref, wfi_ref, wfo_ref,
             dx1_ref, h2_ref, dgu_ref, acc_ref, gwfo_ref, g_s, u_s, acc_wfo, act_s, dxg_s, st_wfo):
        i = pl.program_id(0)
        cur = i % 2
        prv = 1 - cur

        @pl.when(i == 0)
        def _():
            acc_ref[...] = jnp.zeros((8, D), F32)
            acc_wfo[...] = jnp.zeros((NFB, WFI_B, D), F32)
            act_s[1] = jnp.zeros((NFB, tm, WFI_B), BF16)
            dxg_s[1] = jnp.zeros((tm, D), BF16)

        x1 = x1_ref[...]
        shift2 = mod_ref[3:4, :]
        scale2 = mod_ref[4:5, :]
        gate2 = mod_ref[5:6, :]
        shiftf = mod_ref[6:7, :]
        scalef = mod_ref[7:8, :]
        g2v = g2_ref[...]
        gfv = gf_ref[...]
        r2 = _rs(x1)
        xn2 = x1 * r2
        h2b = (xn2 * g2v * (1.0 + scale2) + shift2).astype(BF16)
        h2_ref[...] = h2b
        f = jnp.zeros((tm, D), F32)
        for j in range(NFB):
            g = _dot(h2b, wfi_ref[j])
            u = _dot(h2b, wfi_ref[NFB + j])
            g_s[j] = g
            u_s[j] = u
            actb = (g * _sig(g) * u).astype(BF16)
            act_s[cur, j] = actb
            f = f + _dot(actb, wfo_ref[j * WFI_B:(j + 1) * WFI_B, :])
        dxg_prev = dxg_s[prv]
        for j in range(NFB):
            acc_wfo[j] += _dot_tn(act_s[prv, j], dxg_prev)
        x2 = x1 + gate2 * f
        rf = _rs(x2)
        xnf = x2 * rf
        out = xnf * gfv * (1.0 + scalef) + shiftf
        e = out - tgt_ref[...]
        dout = e * (1.0 / D)
        acc_ref[7:8, :] += _colsum(e * e)
        acc_ref[0:1, :] += _colsum(dout)
        acc_ref[1:2, :] += _colsum(dout * xnf * gfv)
        acc_ref[2:3, :] += _colsum(dout * (1.0 + scalef) * xnf)
        dxnf = dout * (1.0 + scalef) * gfv
        dx2 = rf * (dxnf - xnf * jnp.mean(dxnf * xnf, axis=-1, keepdims=True))
        acc_ref[3:4, :] += _colsum(dx2 * f)
        dxgb = (dx2 * gate2).astype(BF16)
        dxg_s[cur] = dxgb
        dh2 = jnp.zeros((tm, D), F32)
        for j in range(NFB):
            dact = _dot_nt(dxgb, wfo_ref[j * WFI_B:(j + 1) * WFI_B, :])
            g = g_s[j]
            u = u_s[j]
            s = _sig(g)
            dgb = (dact * u * (s * (1.0 + g * (1.0 - s)))).astype(BF16)
            dub = (dact * (g * s)).astype(BF16)
            dgu_ref[j] = dgb
            dgu_ref[NFB + j] = dub
            dh2 = dh2 + _dot_nt(dgb, wfi_ref[j])
            dh2 = dh2 + _dot_nt(dub, wfi_ref[NFB + j])
        acc_ref[4:5, :] += _colsum(dh2)
        acc_ref[5:6, :] += _colsum(dh2 * xn2 * g2v)
        acc_ref[6:7, :] += _colsum(dh2 * (1.0 + scale2) * xn2)
        dxn2 = dh2 * (1.0 + scale2) * g2v
        dx1_ref[...] = dx2 + r2 * (dxn2 - xn2 * jnp.mean(dxn2 * xn2, axis=-1, keepdims=True))

        @pl.when(i == nt - 1)
        def _():
            for j in range(NFB):
                st_wfo[...] = (acc_wfo[j] + _dot_tn(act_s[cur, j], dxgb)).astype(BF16)
                pltpu.sync_copy(st_wfo, gwfo_ref.at[j])

    tile = lambda w: pl.BlockSpec((tm, w), lambda i: (i, 0))
    blocked = lambda n: pl.BlockSpec((n, tm, WFI_B), lambda i: (0, i, 0))
    return pl.pallas_call(
        body,
        name="ffn_fwd_bwd",
        grid=(nt,),
        in_specs=[tile(D), tile(D), _full((8, D)), _full((1, D)), _full((1, D)),
                  _resident((NDEV, D, WFI_B)), _resident((DFF, D))],
        out_specs=[tile(D), tile(D), blocked(NDEV), _full((8, D)), HBM],
        out_shape=[jax.ShapeDtypeStruct((T, D), F32), jax.ShapeDtypeStruct((T, D), BF16),
                   jax.ShapeDtypeStruct((NDEV, T, WFI_B), BF16), jax.ShapeDtypeStruct((8, D), F32),
                   jax.ShapeDtypeStruct((NFB, WFI_B, D), BF16)],
        scratch_shapes=[pltpu.VMEM((NFB, tm, WFI_B), F32), pltpu.VMEM((NFB, tm, WFI_B), F32),
                        pltpu.VMEM((NFB, WFI_B, D), F32), pltpu.VMEM((2, NFB, tm, WFI_B), BF16),
                        pltpu.VMEM((2, tm, D), BF16), pltpu.VMEM((WFI_B, D), BF16)],
        compiler_params=pltpu.CompilerParams(dimension_semantics=("arbitrary",), vmem_limit_bytes=FFN_VMEM_LIMIT),
    )(x1, tgt, mod, g2, gf, wfi, wfo)


def _mix_bwd(dx1, x, zvg, mixed, o, hb, yb, gu, dgu, dgv, vhat, rslb, yhat, rsg, mod, g1, win, lng, lnb, wcat, wcat_t,
             cw, gng, gnb, oga, ogb, wout, pm, esel, after, tm):
    T = x.shape[0]
    nt = T // tm
    nch = tm // CH
    WOB = 256

    def body(dx1_ref, x_ref, zvg_ref, mixed_ref, o_ref, hb_ref, yb_ref, gu_ref, dgu_ref, dgv_ref, vhat_ref, rsl_ref,
             yhat_ref, rsg_ref, mod_ref, g1_ref, win_ref, lng_ref, lnb_ref, wcat_ref, wcatt_ref, cw_ref, gng_ref,
             gnb_ref, oga_ref, ogb_ref, wout_ref, pm_ref, esel_ref, after_ref,
             gx_ref, accv_ref, accb_ref, acca_ref, accbs_ref, accws_ref, acccw_ref, gwin_ref, gwout_ref,
             dycbuf, shbuf, bs_s, acc_win, acc_wout, st_win, st_wout):
        i = pl.program_id(0)

        @pl.when(i == 0)
        def _():
            acc_win[...] = jnp.zeros((NDEV, D, WIN_B), F32)
            acc_wout[...] = jnp.zeros((D, D), F32)
            accv_ref[...] = jnp.zeros((8, D), F32)
            accb_ref[...] = jnp.zeros((1, DIN), F32)
            acca_ref[...] = jnp.zeros((8, DA), F32)
            accws_ref[...] = jnp.zeros((NH * CH, CH), F32)
            acccw_ref[...] = jnp.zeros((HALO, DB), F32)
            bs_s[...] = jnp.zeros((CH, DA), F32)
            dycbuf[tm:tm + HALO, :] = jnp.zeros((HALO, DB), F32)

        shift1 = mod_ref[0:1, :]
        scale1 = mod_ref[1:2, :]
        gate1 = mod_ref[2:3, :]
        g1v = g1_ref[...]
        xv = x_ref[...]
        r1 = _rs(xv)
        xn1 = xv * r1
        val = zvg_ref[:, 0:DB]
        gate = zvg_ref[:, DB:]
        gu = gu_ref[...]
        dgelu_u = dgu_ref[...]
        dgelu_v = dgv_ref[...]
        vhat = vhat_ref[...]
        rsl = rsl_ref[:, 0:1]
        lngv = lng_ref[...]
        vnb = (vhat * lngv + lnb_ref[...]).astype(BF16)
        mixed = mixed_ref[...]
        ya = gu * mixed
        ra = _rs(ya)
        yan = ya * ra
        sgt = _sig(gate)
        gl = val * sgt
        pmv = pm_ref[...]
        rsg = rsg_ref[...]
        yhat = yhat_ref[...]
        gngv = gng_ref[...]
        yg = yhat * gngv + gnb_ref[...]
        sgy = _sig(yg)
        yb = yg * sgy
        rb = _rs(yb)
        ybn = yb * rb
        dx1 = dx1_ref[...]
        accv_ref[0:1, :] += _colsum(dx1 * o_ref[...])
        dogb = (dx1 * gate1).astype(BF16)
        acc_wout[...] += _dot_tn(yb_ref[...], dogb)
        dy = _dot_nt(dogb, wout_ref[...])
        dna = dy[:, 0:DA]
        dnb = dy[:, DA:]
        ogav = oga_ref[...]
        ogbv = ogb_ref[...]
        acca_ref[2:3, :] += _colsum(dna * yan)
        acca_ref[3:4, :] += _colsum(dnb * ybn)
        ta = dna * ogav
        dya = ra * (ta - yan * jnp.mean(ta * yan, axis=-1, keepdims=True))
        tb = dnb * ogbv
        dyb = rb * (tb - ybn * jnp.mean(tb * ybn, axis=-1, keepdims=True))
        dgu = dya * mixed
        dm = dya * gu
        lane_head = _head_of_lane(CH)
        dvn_chunks = []
        bs_acc = bs_s[...]
        for ci in range(nch):
            dmc = dm[ci * CH:(ci + 1) * CH, :]
            bs_acc = bs_acc + dmc
            dmcb = dmc.astype(BF16)
            dvn_chunks.append(_block_pick(_dot(wcatt_ref[...], dmcb), lane_head))
            zero = jnp.zeros((CH, DA), BF16)
            stack = jnp.concatenate([jnp.where(lane_head == h, dmcb, zero) for h in range(NH)], axis=0)
            accws_ref[...] += _dot_nt(stack, vnb[ci * CH:(ci + 1) * CH, :])
        bs_s[...] = bs_acc
        dvn = jnp.concatenate(dvn_chunks, axis=0) if nch > 1 else dvn_chunks[0]
        acca_ref[0:1, :] += _colsum(dvn * vhat)
        acca_ref[1:2, :] += _colsum(dvn)
        dvh = dvn * lngv
        dgv = rsl * (dvh - jnp.mean(dvh, axis=-1, keepdims=True)
                     - vhat * jnp.mean(dvh * vhat, axis=-1, keepdims=True))
        du = dgu * dgelu_u
        dv = dgv * dgelu_v
        dyg = dyb * (sgy * (1.0 + yg * (1.0 - sgy)))
        acca_ref[5:6, :] += _colsum(dyg * yhat)
        acca_ref[6:7, :] += _colsum(dyg)
        dyh = dyg * gngv
        dyc = rsg * (dyh - _grp_mean(dyh, pmv) - yhat * _grp_mean(dyh * yhat, pmv))
        acca_ref[4:5, :] += _colsum(dyc)
        dycbuf[0:tm, :] = dyc
        _shifted_copies(dycbuf, shbuf, tm)
        dgl = jnp.zeros((tm, DB), F32)
        for k in range(KW):
            win_k = _window(dycbuf, shbuf, KW - 1 - k, tm)
            dgl = dgl + cw_ref[k:k + 1, :] * win_k
            acccw_ref[k:k + 1, :] += _colsum(win_k * gl)
        dycbuf[tm:tm + HALO, :] = dyc[0:HALO, :]
        dval = dgl * sgt
        dgate = dgl * val * sgt * (1.0 - sgt)
        dz = jnp.concatenate([du, dv, dval, dgate], axis=1)
        accb_ref[...] += _colsum(dz)
        dzb = dz.astype(BF16)
        hbv = hb_ref[...]
        dh = jnp.zeros((tm, D), F32)
        for j in range(NDEV):
            dzj = dzb[:, j * WIN_B:(j + 1) * WIN_B]
            acc_win[j] += _dot_tn(hbv, dzj)
            dh = dh + _dot_nt(dzj, win_ref[j])
        accv_ref[1:2, :] += _colsum(dh)
        accv_ref[2:3, :] += _colsum(dh * xn1 * g1v)
        accv_ref[3:4, :] += _colsum(dh * (1.0 + scale1) * xn1)
        dxn1 = dh * (1.0 + scale1) * g1v
        gx_ref[...] = dx1 + r1 * (dxn1 - xn1 * jnp.mean(dxn1 * xn1, axis=-1, keepdims=True))

        @pl.when(i == nt - 1)
        def _():
            rows = lax.broadcasted_iota(jnp.int32, (NH * CH, CH), 0) & (CH - 1)
            cols = lax.broadcasted_iota(jnp.int32, (NH * CH, CH), 1)
            accws_ref[...] = jnp.where(cols <= rows, accws_ref[...], 0.0)
            bs = bs_s[...]
            hi = bs.astype(BF16)
            r1_ = bs - hi.astype(F32)
            mid = r1_.astype(BF16)
            lo = (r1_ - mid.astype(F32)).astype(BF16)
            ev = esel_ref[...]
            accbs_ref[...] = _dot(hi, ev) + _dot(mid, ev) + _dot(lo, ev)
            for j in range(NDEV):
                st_win[...] = acc_win[j].astype(BF16)
                pltpu.sync_copy(st_win, gwin_ref.at[j])
            for j in range(D // WOB):
                st_wout[...] = acc_wout[j * WOB:(j + 1) * WOB, :].astype(BF16)
                pltpu.sync_copy(st_wout, gwout_ref.at[pl.ds(j * WOB, WOB)])

    rev = lambda w: pl.BlockSpec((tm, w), lambda i: (nt - 1 - i, 0))
    outs = pl.pallas_call(
        body,
        name="mix_bwd",
        grid=(nt,),
        in_specs=[rev(D), rev(D), rev(2 * DB), rev(DA), rev(D), rev(D), rev(D), rev(DA), rev(DA), rev(DA), rev(DA),
                  rev(CH), rev(DB), rev(DB), _full((8, D)), _full((1, D)),
                  _resident((NDEV, D, WIN_B)), _full((1, DA)), _full((1, DA)), _full((NH * CH, CH)),
                  _full((NH * CH, CH)), _full((HALO, DB)), _full((1, DB)), _full((1, DB)), _full((1, DA)),
                  _full((1, DB)), _resident((D, D)), _full((DB, DB)), _full((DA, CH)), HBM],
        out_specs=[rev(D), _full((8, D)), _full((1, DIN)), _full((8, DA)), _full((CH, CH)),
                   _full((NH * CH, CH)), _full((HALO, DB)), HBM, HBM],
        out_shape=[jax.ShapeDtypeStruct((T, D), F32), jax.ShapeDtypeStruct((8, D), F32),
                   jax.ShapeDtypeStruct((1, DIN), F32), jax.ShapeDtypeStruct((8, DA), F32),
                   jax.ShapeDtypeStruct((CH, CH), F32), jax.ShapeDtypeStruct((NH * CH, CH), F32),
                   jax.ShapeDtypeStruct((HALO, DB), F32),
                   jax.ShapeDtypeStruct((NDEV, D, WIN_B), BF16), jax.ShapeDtypeStruct((D, D), BF16)],
        scratch_shapes=[pltpu.VMEM((tm + HALO, DB), F32), pltpu.VMEM((7, tm + SH_ROWS, DB), F32),
                        pltpu.VMEM((CH, DA), F32), pltpu.VMEM((NDEV, D, WIN_B), F32), pltpu.VMEM((D, D), F32),
                        pltpu.VMEM((D, WIN_B), BF16), pltpu.VMEM((WOB, D), BF16)],
        compiler_params=pltpu.CompilerParams(dimension_semantics=("arbitrary",), vmem_limit_bytes=VMEM_LIMIT),
    )(dx1, x, zvg, mixed, o, hb, yb, gu, dgu, dgv, vhat, rslb, yhat, rsg, mod, g1, win, lng, lnb, wcat, wcat_t, cw,
      gng, gnb, oga, ogb, wout, pm, esel, after)
    return outs[:7], outs[7:]


def _wgrad_cols(a, b, nblk, tk, name):
    T, M = a.shape
    bw = b.shape[1] // nblk
    nk = T // tk

    def body(a_ref, b_ref, o_ref, acc):
        k = pl.program_id(0)

        @pl.when(k == 0)
        def _():
            acc[...] = jnp.zeros((nblk, M, bw), F32)

        av = a_ref[...]
        for j in range(nblk):
            acc[j] += _dot_tn(av, b_ref[:, j * bw:(j + 1) * bw])

        @pl.when(k == nk - 1)
        def _():
            o_ref[...] = acc[...].astype(BF16)

    return pl.pallas_call(
        body, name=name, grid=(nk,),
        in_specs=[pl.BlockSpec((tk, M), lambda k: (k, 0)), pl.BlockSpec((tk, nblk * bw), lambda k: (k, 0))],
        out_specs=_full((nblk, M, bw)),
        out_shape=jax.ShapeDtypeStruct((nblk, M, bw), BF16),
        scratch_shapes=[pltpu.VMEM((nblk, M, bw), F32)],
        compiler_params=pltpu.CompilerParams(dimension_semantics=("arbitrary",), vmem_limit_bytes=VMEM_LIMIT),
    )(a, b)


def _wgrad_b_blocked(a, b3, per, tk, name):
    T, M = a.shape
    nb, _, bw = b3.shape
    nk = T // tk

    def body(a_ref, b_ref, o_ref, acc):
        k = pl.program_id(1)

        @pl.when(k == 0)
        def _():
            acc[...] = jnp.zeros((per, M, bw), F32)

        av = a_ref[...]
        for j in range(per):
            acc[j] += _dot_tn(av, b_ref[j])

        @pl.when(k == nk - 1)
        def _():
            o_ref[...] = acc[...].astype(BF16)

    return pl.pallas_call(
        body, name=name, grid=(nb // per, nk),
        in_specs=[pl.BlockSpec((tk, M), lambda j, k: (k, 0)), pl.BlockSpec((per, tk, bw), lambda j, k: (j, k, 0))],
        out_specs=pl.BlockSpec((per, M, bw), lambda j, k: (j, 0, 0)),
        out_shape=jax.ShapeDtypeStruct((nb, M, bw), BF16),
        scratch_shapes=[pltpu.VMEM((per, M, bw), F32)],
        compiler_params=pltpu.CompilerParams(dimension_semantics=("arbitrary", "arbitrary"),
                                             vmem_limit_bytes=VMEM_LIMIT),
    )(a, b3)


def _wgrad_a_blocked(a3, b, tk, name):
    nb, T, bw = a3.shape
    N = b.shape[1]
    nk = T // tk

    def body(a_ref, b_ref, o_ref, acc):
        k = pl.program_id(1)

        @pl.when(k == 0)
        def _():
            acc[...] = jnp.zeros((bw, N), F32)

        acc[...] += _dot_tn(a_ref[0], b_ref[...])

        @pl.when(k == nk - 1)
        def _():
            o_ref[0] = acc[...].astype(BF16)

    return pl.pallas_call(
        body, name=name, grid=(nb, nk),
        in_specs=[pl.BlockSpec((1, tk, bw), lambda j, k: (j, k, 0)), pl.BlockSpec((tk, N), lambda j, k: (k, 0))],
        out_specs=pl.BlockSpec((1, bw, N), lambda j, k: (j, 0, 0)),
        out_shape=jax.ShapeDtypeStruct((nb, bw, N), BF16),
        scratch_shapes=[pltpu.VMEM((bw, N), F32)],
        compiler_params=pltpu.CompilerParams(dimension_semantics=("arbitrary", "arbitrary"),
                                             vmem_limit_bytes=VMEM_LIMIT),
    )(a3, b)


def _small_copy(src, dst, ss, rs, k, to):
    return pltpu.make_async_remote_copy(src_ref=src, dst_ref=dst, send_sem=ss.at[k], recv_sem=rs.at[k],
                                        device_id=to, device_id_type=MESH)


def _gather(c_row, ada_w, ada_b8, ada_f_w, ada_f_b8, conv_s, shards):
    nw = len(shards)

    def body(c_ref, adaw_ref, adab_ref, adafw_ref, adafb_ref, conv_ref, *rest):
        w_in = rest[:nw]
        call_ref, cparts_ref, cfparts_ref, convg_ref = rest[nw:nw + 4]
        w_out = rest[nw + 4:2 * nw + 4]
        part_s, partf_s, wss, wrs, lsem, s1, r1, s2, r2, s3, r3, s4, r4 = rest[2 * nw + 4:]
        x, y, c, idx = _place()
        me = (x, y, c)
        ag = _AllGather(w_in, w_out, wss, wrs, lsem)
        ag.start()
        call_ref[pl.ds(idx, 1), :] = c_ref[...]
        convg_ref[idx] = conv_ref[...]
        ph1 = []
        for k in range(1, NDEV):
            to = _dev(idx ^ k)
            ph1.append(_small_copy(c_ref, call_ref.at[pl.ds(idx, 1)], s1, r1, k - 1, to))
            ph1.append(_small_copy(conv_ref, convg_ref.at[idx], s2, r2, k - 1, to))
        for cp in ph1:
            cp.start()
        for k in range(1, NDEV):
            src_dev = idx ^ k
            _small_copy(c_ref, call_ref.at[pl.ds(src_dev, 1)], s1, r1, k - 1, me).wait_recv()
            _small_copy(conv_ref, convg_ref.at[src_dev], s2, r2, k - 1, me).wait_recv()
        call = call_ref[...]
        cact = (call * _sig(call))
        part_s[...] = jnp.dot(cact, adaw_ref[...], preferred_element_type=F32,
                              precision=lax.Precision.HIGHEST) + adab_ref[pl.ds(idx, 1), :]
        partf_s[...] = jnp.dot(cact, adafw_ref[...], preferred_element_type=F32,
                               precision=lax.Precision.HIGHEST) + adafb_ref[pl.ds(idx, 1), :]
        cparts_ref[pl.ds(idx, 1), :] = part_s[pl.ds(idx, 1), :]
        cfparts_ref[pl.ds(idx, 1), :] = partf_s[pl.ds(idx, 1), :]
        ph2 = []
        for k in range(1, NDEV):
            t = idx ^ k
            ph2.append(_small_copy(part_s.at[pl.ds(t, 1)], cparts_ref.at[pl.ds(idx, 1)], s3, r3, k - 1, _dev(t)))
            ph2.append(_small_copy(partf_s.at[pl.ds(t, 1)], cfparts_ref.at[pl.ds(idx, 1)], s4, r4, k - 1, _dev(t)))
        for cp in ph2:
            cp.start()
        for k in range(1, NDEV):
            src_dev = idx ^ k
            _small_copy(part_s.at[pl.ds(0, 1)], cparts_ref.at[pl.ds(src_dev, 1)], s3, r3, k - 1, me).wait_recv()
            _small_copy(partf_s.at[pl.ds(0, 1)], cfparts_ref.at[pl.ds(src_dev, 1)], s4, r4, k - 1, me).wait_recv()
        for cp in ph1 + ph2:
            cp.wait_send()
        ag.forward()
        ag.finish()

    dma7 = pltpu.SemaphoreType.DMA((NDEV - 1,))
    outs = pl.pallas_call(
        body,
        name="gather_weights",
        in_specs=[VM] * 6 + [HBM] * nw,
        out_specs=[VM] * 4 + [HBM] * nw,
        out_shape=[jax.ShapeDtypeStruct((NDEV, D), F32), jax.ShapeDtypeStruct((NDEV, ada_w.shape[1]), F32),
                   jax.ShapeDtypeStruct((NDEV, ada_f_w.shape[1]), F32),
                   jax.ShapeDtypeStruct((NDEV,) + conv_s.shape, F32)]
                  + [jax.ShapeDtypeStruct((NDEV,) + s.shape, s.dtype) for s in shards],
        scratch_shapes=[pltpu.VMEM((NDEV, ada_w.shape[1]), F32), pltpu.VMEM((NDEV, ada_f_w.shape[1]), F32)]
                       + AG_SEMS(nw) + [dma7] * 8,
        compiler_params=pltpu.CompilerParams(vmem_limit_bytes=VMEM_LIMIT),
    )(c_row, ada_w, ada_b8, ada_f_w, ada_f_b8, conv_s, *shards)
    return outs[0], outs[1], outs[2], outs[3], outs[4:]


_VEC_AT = {
    "norm1_g": (8, 0, D), "a_ln_g": (11, 0, DA), "a_ln_b": (11, DA, DA), "a_spatial_b": (12, 0, D),
    "b_conv_b": (13, 0, DB), "b_gn_g": (13, DB, DB), "b_gn_b": (14, 0, DB), "out_norm_a_g": (14, DB, DA),
    "out_norm_b_g": (15, 0, DB), "norm2_g": (16, 0, D), "norm_f_g": (17, 0, D),
}
_LOSS_ROW = 18
_CW_ROW = 24


def _reduce_small(acc_f, acc_v, acc_b, acc_a, acc_bs, acc_cw, dws, after):
    def body(accf_ref, accv_ref, accb_ref, acca_ref, accbs_ref, acccw_ref, dws_ref, after_ref,
             vsum_ref, dcond_ref, wssum_ref, vloc, vbuf, wbuf, wown, s1, r1, s2, r2, s3, r3):
        x, y, c, idx = _place()
        me = (x, y, c)
        vloc[...] = jnp.zeros((NVEC, D), F32)
        vloc[0:1, :] = accv_ref[1:2, :]
        vloc[1:2, :] = accv_ref[2:3, :]
        vloc[2:3, :] = accv_ref[0:1, :]
        vloc[3:4, :] = accf_ref[4:5, :]
        vloc[4:5, :] = accf_ref[5:6, :]
        vloc[5:6, :] = accf_ref[3:4, :]
        vloc[6:7, :] = accf_ref[0:1, :]
        vloc[7:8, :] = accf_ref[1:2, :]
        vloc[8:9, :] = accv_ref[3:4, :]
        vloc[9:10, :] = accb_ref[:, 0:D]
        vloc[10:11, :] = accb_ref[:, D:]
        vloc[11:12, 0:DA] = acca_ref[0:1, :]
        vloc[11:12, DA:] = acca_ref[1:2, :]
        bst = accbs_ref[...].T
        for h in range(NH):
            vloc[12:13, h * CH:(h + 1) * CH] = bst[h:h + 1, :]
        vloc[13:14, 0:DB] = acca_ref[4:5, :]
        vloc[13:14, DB:] = acca_ref[5:6, :]
        vloc[14:15, 0:DB] = acca_ref[6:7, :]
        vloc[14:15, DB:] = acca_ref[2:3, :]
        vloc[15:16, 0:DB] = acca_ref[3:4, :]
        vloc[16:17, :] = accf_ref[6:7, :]
        vloc[17:18, :] = accf_ref[2:3, :]
        vloc[_LOSS_ROW:_LOSS_ROW + 1, :] = accf_ref[7:8, :]
        vloc[_CW_ROW:_CW_ROW + HALO // 2, 0:DB] = acccw_ref[0:HALO // 2, :]
        vloc[_CW_ROW:_CW_ROW + HALO // 2, DB:] = acccw_ref[HALO // 2:, :]
        vbuf[idx] = vloc[...]
        rows_of = lambda t: pl.ds(pl.multiple_of(t * CH, CH), CH)
        wbuf[0] = dws_ref[rows_of(idx), :]
        sm = []
        for k in range(1, NDEV):
            t = idx ^ k
            sm.append(_small_copy(vloc, vbuf.at[idx], s1, r1, k - 1, _dev(t)))
            sm.append(_small_copy(dws_ref.at[rows_of(t)], wbuf.at[k], s2, r2, k - 1, _dev(t)))
        for cp in sm:
            cp.start()
        for k in range(1, NDEV):
            _small_copy(dws_ref.at[rows_of(0)], wbuf.at[k], s2, r2, k - 1, me).wait_recv()
        ws = wbuf[0]
        for k in range(1, NDEV):
            ws = ws + wbuf[k]
        wown[...] = ws
        wssum_ref[rows_of(idx), :] = ws
        ag = [_small_copy(wown, wssum_ref.at[rows_of(idx)], s3, r3, k - 1, _dev(idx ^ k)) for k in range(1, NDEV)]
        for cp in ag:
            cp.start()
        for k in range(1, NDEV):
            _small_copy(vloc, vbuf.at[idx ^ k], s1, r1, k - 1, me).wait_recv()
        vs = vbuf[0]
        for d in range(1, NDEV):
            vs = vs + vbuf[d]
        vsum_ref[...] = vs
        for d in range(NDEV):
            dcond_ref[d] = vbuf[d, 0:8, :]
        for k in range(1, NDEV):
            _small_copy(wown, wssum_ref.at[rows_of(idx ^ k)], s3, r3, k - 1, me).wait_recv()
        for cp in sm + ag:
            cp.wait_send()

    dma7 = pltpu.SemaphoreType.DMA((NDEV - 1,))
    return pl.pallas_call(
        body,
        name="reduce_small",
        in_specs=[VM] * 7 + [HBM],
        out_specs=[VM, VM, VM],
        out_shape=[jax.ShapeDtypeStruct((NVEC, D), F32), jax.ShapeDtypeStruct((NDEV, 8, D), F32),
                   jax.ShapeDtypeStruct(dws.shape, F32)],
        scratch_shapes=[pltpu.VMEM((NVEC, D), F32), pltpu.VMEM((NDEV, NVEC, D), F32),
                        pltpu.VMEM((NDEV, CH, CH), F32), pltpu.VMEM((CH, CH), F32)] + [dma7] * 6,
        compiler_params=pltpu.CompilerParams(vmem_limit_bytes=VMEM_LIMIT),
    )(acc_f, acc_v, acc_b, acc_a, acc_bs, acc_cw, dws, after)


HBM_ONLY = pl.BlockSpec(memory_space=pltpu.HBM)
SEM = pl.BlockSpec(memory_space=pltpu.SEMAPHORE)
EFFECT = pltpu.SideEffectType.DATAFLOW_SIDE_EFFECTING


def _rs_copies(g_refs, land_refs, sems):
    x, y, c, idx = _place()
    cps = []
    for k in range(1, NDEV):
        t = idx ^ k
        for a in range(len(g_refs)):
            n = len(cps)
            cps.append(pltpu.make_async_remote_copy(
                src_ref=g_refs[a].at[t], dst_ref=land_refs[a].at[k - 1], send_sem=sems[2 * n],
                recv_sem=sems[2 * n + 1], device_id=_dev(t), device_id_type=MESH))
    return cps


def _rs_start(grads, name, after=()):
    nw = len(grads)
    nsem = 2 * nw * (NDEV - 1)
    lands = [lax.empty((NDEV - 1,) + g.shape[1:], g.dtype) for g in grads]

    def body(*refs):
        g_refs, land_refs = refs[:nw], refs[nw:2 * nw]
        sems = refs[2 * nw + len(after):2 * nw + len(after) + nsem]
        token = refs[-1]
        for cp in _rs_copies(g_refs, land_refs, sems):
            cp.start()
        token[...] = jnp.zeros_like(token)

    outs = pl.pallas_call(
        body, name=name,
        out_shape=(*[pltpu.SemaphoreType.DMA(())] * nsem,
                   *[pltpu.HBM(g.shape, g.dtype) for g in grads], *[pltpu.HBM(l.shape, l.dtype) for l in lands],
                   jax.ShapeDtypeStruct((8, CH), F32)),
        in_specs=[HBM_ONLY] * (2 * nw) + [HBM] * len(after),
        out_specs=(*[SEM] * nsem, *[HBM_ONLY] * (2 * nw), VM),
        input_output_aliases={i: nsem + i for i in range(2 * nw)},
        compiler_params=pltpu.CompilerParams(has_side_effects=EFFECT),
    )(*[pltpu.with_memory_space_constraint(g, pltpu.HBM) for g in grads],
      *[pltpu.with_memory_space_constraint(l, pltpu.HBM) for l in lands], *after)
    return outs[:nsem], outs[nsem:nsem + nw], outs[nsem + nw:nsem + 2 * nw], outs[-1]


def _rs_wait(sems, g_thru, land_thru, after, name):
    nw = len(g_thru)
    nsem = len(sems)

    def body(*refs):
        g_refs, land_refs = refs[:nw], refs[nw:2 * nw]
        for cp in _rs_copies(g_refs, land_refs, refs[2 * nw:2 * nw + nsem]):
            cp.wait_send()
            cp.wait_recv()

    outs = pl.pallas_call(
        body, name=name,
        out_shape=tuple(pltpu.HBM(a.shape, a.dtype) for a in list(g_thru) + list(land_thru)),
        in_specs=[HBM_ONLY] * (2 * nw) + [SEM] * nsem + [HBM] * len(after),
        out_specs=tuple([HBM_ONLY] * (2 * nw)),
        input_output_aliases={i: i for i in range(2 * nw)},
        compiler_params=pltpu.CompilerParams(has_side_effects=EFFECT),
    )(*g_thru, *land_thru, *sems, *after)
    return outs[:nw], outs[nw:]


def _adamw(w, g, m, v):
    m2 = ADAM_B1 * m + (1.0 - ADAM_B1) * g
    v2 = ADAM_B2 * v + (1.0 - ADAM_B2) * (g * g)
    m_hat = m2 / (1.0 - ADAM_B1 ** ADAM_STEP)
    v_hat = v2 / (1.0 - ADAM_B2 ** ADAM_STEP)
    delta = -ADAM_LR * (m_hat / (jnp.sqrt(v_hat) + ADAM_EPS) + ADAM_WD * w)
    return delta, m2, v2


def _adam_big(r, w, m, v, rb, name, own=None, after=None):
    R, C = w.shape
    ns = r.shape[0]

    def body(*refs):
        r_ref = refs[0]
        own_ref = refs[1] if own is not None else None
        w_ref, m_ref, v_ref, g_ref, d_ref, m2_ref, v2_ref = refs[len(refs) - 7:]
        g = r_ref[0].astype(F32) if own is None else own_ref[...].astype(F32) + r_ref[0].astype(F32)
        for k in range(1, ns):
            g = g + r_ref[k].astype(F32)
        g_ref[...] = g
        d_ref[...], m2_ref[...], v2_ref[...] = _adamw(w_ref[...], g, m_ref[...], v_ref[...])

    t2 = pl.BlockSpec((rb, C), lambda i: (i, 0))
    sd = jax.ShapeDtypeStruct((R, C), F32)
    extra_specs = ([t2] if own is not None else []) + ([HBM] if after is not None else [])
    extra = ([own] if own is not None else []) + ([after] if after is not None else [])
    return pl.pallas_call(
        body, name=name, grid=(R // rb,),
        in_specs=[pl.BlockSpec((ns, rb, C), lambda i: (0, i, 0))] + extra_specs + [t2, t2, t2],
        out_specs=[t2, t2, t2, t2], out_shape=[sd, sd, sd, sd],
        compiler_params=pltpu.CompilerParams(dimension_semantics=("arbitrary",), vmem_limit_bytes=VMEM_LIMIT),
    )(r, *extra, w, m, v)


def _adam_ada(cact_t, dcs, w, m, v, rb, name):
    R, C = w.shape

    def body(ct_ref, dc_ref, w_ref, m_ref, v_ref, g_ref, d_ref, m2_ref, v2_ref):
        g = jnp.dot(ct_ref[...], dc_ref[...], preferred_element_type=F32, precision=lax.Precision.HIGHEST)
        g_ref[...] = g
        d_ref[...], m2_ref[...], v2_ref[...] = _adamw(w_ref[...], g, m_ref[...], v_ref[...])

    t2 = pl.BlockSpec((rb, C), lambda i: (i, 0))
    sd = jax.ShapeDtypeStruct((R, C), F32)
    return pl.pallas_call(
        body, name=name, grid=(R // rb,),
        in_specs=[pl.BlockSpec((rb, NDEV), lambda i: (i, 0)), _full((NDEV, C)), t2, t2, t2],
        out_specs=[t2, t2, t2, t2], out_shape=[sd, sd, sd, sd],
        compiler_params=pltpu.CompilerParams(dimension_semantics=("arbitrary",), vmem_limit_bytes=VMEM_LIMIT),
    )(cact_t, dcs, w, m, v)


_SMALL = ["ada_b", "ada_f_b", "norm1_g", "b_in", "a_ln_g", "a_ln_b", "a_spatial_b", "b_conv_b", "b_gn_g", "b_gn_b",
          "out_norm_a_g", "out_norm_b_g", "norm2_g", "norm_f_g", "a_spatial_w", "b_conv_w"]


def _adam_small(vsum, wssum, gcw, params):
    names = _SMALL
    flat = []
    for n in names:
        flat += list(params[n])

    def body(vs_ref, ws_ref, gcw_ref, *rest):
        ins = rest[:3 * len(names)]
        outs = rest[3 * len(names):]
        for pi, n in enumerate(names):
            w_ref, m_ref, v_ref = ins[3 * pi:3 * pi + 3]
            g_ref, d_ref, m2_ref, v2_ref = outs[4 * pi:4 * pi + 4]
            if n in ("ada_b", "ada_f_b", "b_in"):
                row0 = {"ada_b": 0, "ada_f_b": 6, "b_in": 9}[n]
                pieces = [(vs_ref[row0 + r:row0 + r + 1, :], slice(r * D, (r + 1) * D))
                          for r in range(w_ref.shape[1] // D)]
            elif n == "a_spatial_w":
                pieces = [(ws_ref[...], slice(None))]
            elif n == "b_conv_w":
                pieces = [(gcw_ref[...], slice(None))]
            else:
                row, off, width = _VEC_AT[n]
                pieces = [(vs_ref[row:row + 1, off:off + width], slice(None))]
            for g, cs in pieces:
                g_ref[:, cs] = g
                d_ref[:, cs], m2_ref[:, cs], v2_ref[:, cs] = _adamw(w_ref[:, cs], g, m_ref[:, cs], v_ref[:, cs])

    out_shape = []
    for n in names:
        out_shape += [jax.ShapeDtypeStruct(params[n][0].shape, F32)] * 4
    outs = pl.pallas_call(
        body, name="adam_small",
        in_specs=[VM] * (3 + len(flat)), out_specs=[VM] * len(out_shape), out_shape=out_shape,
        compiler_params=pltpu.CompilerParams(vmem_limit_bytes=VMEM_LIMIT),
    )(vsum, wssum, gcw, *flat)
    return {n: outs[4 * pi:4 * pi + 4] for pi, n in enumerate(names)}


def _token_tile(T, want):
    return want if T % want == 0 else T


def kernel(x, c, ada_w, ada_b, norm1_g, w_in, b_in, a_ln_g, a_ln_b, a_spatial_w, a_spatial_b, b_conv_w, b_conv_b, b_gn_g, b_gn_b, out_norm_a_g, out_norm_b_g, w_out, norm2_g, w_ffn_in, w_ffn_out, ada_f_w, ada_f_b, norm_f_g, loss_target, m_ada_w, m_ada_b, m_norm1_g, m_w_in, m_b_in, m_a_ln_g, m_a_ln_b, m_a_spatial_w, m_a_spatial_b, m_b_conv_w, m_b_conv_b, m_b_gn_g, m_b_gn_b, m_out_norm_a_g, m_out_norm_b_g, m_w_out, m_norm2_g, m_w_ffn_in, m_w_ffn_out, m_ada_f_w, m_ada_f_b, m_norm_f_g, v_ada_w, v_ada_b, v_norm1_g, v_w_in, v_b_in, v_a_ln_g, v_a_ln_b, v_a_spatial_w, v_a_spatial_b, v_b_conv_w, v_b_conv_b, v_b_gn_g, v_b_gn_b, v_out_norm_a_g, v_out_norm_b_g, v_w_out, v_norm2_g, v_w_ffn_in, v_w_ffn_out, v_ada_f_w, v_ada_f_b, v_norm_f_g):
    T = x.shape[1]
    idx = 4 * lax.axis_index("x") + 2 * lax.axis_index("y") + lax.axis_index("c")
    x2d = x.reshape(T, D)
    tgt = loss_target.reshape(T, D)

    conv_s = jnp.pad(b_conv_w[0], ((0, HALO - KW), (0, 0)))
    call, cparts, cfparts, convg, (win_g, wout_g) = _gather(
        c, ada_w[0], ada_b.reshape(NDEV, -1), ada_f_w, ada_f_b.reshape(NDEV, -1), conv_s,
        [w_in[0].astype(BF16), w_out[0].astype(BF16)])
    mod = jnp.concatenate([cparts.reshape(6, D), cfparts.reshape(2, D)], axis=0)
    wout = wout_g.reshape(D, D)
    cw = jnp.transpose(convg, (1, 0, 2)).reshape(HALO, DB)

    tril = jnp.tril(jnp.ones((CH, CH), dtype=bool))
    wsm = jnp.where(tril[None], a_spatial_w[0], 0.0).astype(BF16)
    wcat = wsm.reshape(NH * CH, CH)
    wcat_t = jnp.transpose(wsm, (0, 2, 1)).reshape(NH * CH, CH)
    bsf = jnp.repeat(a_spatial_b[0].T, DA // NH, axis=1)
    lane = jnp.arange(DB)
    pm = jnp.where((lane[:, None] >> 6) == (lane[None, :] >> 6), 1.0 / 64.0, 0.0).astype(BF16)
    esel = jnp.where((lane[:, None] >> 6) == jnp.arange(CH)[None, :], 1.0, 0.0).astype(BF16)

    tm = _token_tile(T, 256)
    tk = _token_tile(T, 1024)
    (x1, hb, zvg, mixed, yb, o, gu, dgelu_u, dgelu_v, vhat, rslb, yhat, rsg), (wfi_g, wfo_g) = _mix_fwd(
        x2d, mod, norm1_g, win_g, b_in, a_ln_g, a_ln_b, wcat, bsf, cw, b_conv_b, b_gn_g, b_gn_b, out_norm_a_g,
        out_norm_b_g, wout, pm, [w_ffn_in[0].astype(BF16), w_ffn_out[0].astype(BF16)], _token_tile(T, 512))
    dx1, h2b, dgu, acc_f, g_wfo = _ffn(x1, tgt, mod, norm2_g, norm_f_g.reshape(1, D), wfi_g,
                                       wfo_g.reshape(DFF, D), tm)
    g_wfi = _wgrad_b_blocked(h2b, dgu, 2, tk, "wgrad_ffn_in")
    g_wfo = g_wfo.reshape(NDEV, DFF // NDEV, D)
    f_sems, f_thru, f_land, f_token = _rs_start([g_wfi, g_wfo], "rs_ffn_start")
    (gx, acc_v, acc_b, acc_a, acc_bs, acc_ws, acc_cw), (g_win, g_wout) = _mix_bwd(
        dx1, x2d, zvg, mixed, o, hb, yb, gu, dgelu_u, dgelu_v, vhat, rslb, yhat, rsg, mod, norm1_g, win_g, a_ln_g,
        a_ln_b, wcat, wcat_t, cw, b_gn_g, b_gn_b, out_norm_a_g, out_norm_b_g, wout, pm, esel, f_token, tm)
    (g_wfi_d, g_wfo_d), (r_wfi, r_wfo) = _rs_wait(f_sems, f_thru, f_land, [acc_v], "rs_ffn_wait")
    g_wout = g_wout.reshape(NDEV, D // NDEV, D)

    vsum, dcond_all, wssum = _reduce_small(acc_f, acc_v, acc_b, acc_a, acc_bs, acc_cw, acc_ws, g_wfi_d)
    sems, g_thru, land_thru, token = _rs_start([g_win, g_wout], "rs_mix_start", after=(vsum,))

    own = lambda g: lax.dynamic_index_in_dim(g, idx, 0, keepdims=False)
    res = {}
    res["w_ffn_in"] = _adam_big(r_wfi, w_ffn_in[0], m_w_ffn_in[0], v_w_ffn_in[0], 256, "adam_w_ffn_in",
                                own=own(g_wfi_d), after=token)
    res["w_ffn_out"] = _adam_big(r_wfo, w_ffn_out[0], m_w_ffn_out[0], v_w_ffn_out[0], DFF // NDEV // 2,
                                 "adam_w_ffn_out", own=own(g_wfo_d), after=token)
    cact_t = (call * jax.nn.sigmoid(call)).T
    dcond = dcond_all.reshape(NDEV, 8 * D)
    nada = ada_w.shape[2]
    nadf = ada_f_w.shape[1]
    dcs = lax.dynamic_slice(dcond, (0, idx * nada), (NDEV, nada))
    dcfs = lax.dynamic_slice(dcond, (0, 6 * D + idx * nadf), (NDEV, nadf))
    res["ada_w"] = _adam_ada(cact_t, dcs, ada_w[0], m_ada_w[0], v_ada_w[0], 256, "adam_ada_w")
    res["ada_f_w"] = _adam_ada(cact_t, dcfs, ada_f_w, m_ada_f_w, v_ada_f_w, 256, "adam_ada_f_w")
    ncw = b_conv_w.shape[2]
    gcw = jnp.concatenate([lax.dynamic_slice(vsum, (_CW_ROW, idx * ncw), (HALO // 2, ncw)),
                           lax.dynamic_slice(vsum, (_CW_ROW, DB + idx * ncw), (HALO // 2, ncw))], axis=0)[:KW]
    two = lambda a: a.reshape(1, -1) if a.ndim == 1 else a.reshape(-1, a.shape[-1])
    small_in = {
        "ada_b": (ada_b, m_ada_b, v_ada_b), "ada_f_b": (ada_f_b, m_ada_f_b, v_ada_f_b),
        "norm1_g": (norm1_g, m_norm1_g, v_norm1_g), "b_in": (b_in, m_b_in, v_b_in),
        "a_ln_g": (a_ln_g, m_a_ln_g, v_a_ln_g), "a_ln_b": (a_ln_b, m_a_ln_b, v_a_ln_b),
        "a_spatial_b": (a_spatial_b.reshape(1, D), m_a_spatial_b.reshape(1, D), v_a_spatial_b.reshape(1, D)),
        "b_conv_b": (b_conv_b, m_b_conv_b, v_b_conv_b), "b_gn_g": (b_gn_g, m_b_gn_g, v_b_gn_g),
        "b_gn_b": (b_gn_b, m_b_gn_b, v_b_gn_b), "out_norm_a_g": (out_norm_a_g, m_out_norm_a_g, v_out_norm_a_g),
        "out_norm_b_g": (out_norm_b_g, m_out_norm_b_g, v_out_norm_b_g),
        "norm2_g": (norm2_g, m_norm2_g, v_norm2_g), "norm_f_g": (norm_f_g, m_norm_f_g, v_norm_f_g),
        "a_spatial_w": (a_spatial_w, m_a_spatial_w, v_a_spatial_w),
        "b_conv_w": (b_conv_w[0], m_b_conv_w[0], v_b_conv_w[0]),
    }
    small_in = {n: tuple(two(a) for a in t) for n, t in small_in.items()}
    res.update(_adam_small(vsum, wssum, gcw, small_in))
    (g_win_d, g_wout_d), (r_win, r_wout) = _rs_wait(
        sems, g_thru, land_thru,
        [res["w_ffn_in"][0], res["w_ffn_out"][0], res["ada_w"][0], res["ada_f_w"][0], res["norm_f_g"][0]],
        "rs_mix_wait")
    res["w_in"] = _adam_big(r_win, w_in[0], m_w_in[0], v_w_in[0], 256, "adam_w_in", own=own(g_win_d))
    res["w_out"] = _adam_big(r_wout, w_out[0], m_w_out[0], v_w_out[0], D // NDEV, "adam_w_out", own=own(g_wout_d))

    loss = 0.5 / D * jnp.sum(vsum[_LOSS_ROW])
    shapes = {"ada_w": ada_w, "ada_b": ada_b, "norm1_g": norm1_g, "w_in": w_in, "b_in": b_in, "a_ln_g": a_ln_g,
              "a_ln_b": a_ln_b, "a_spatial_w": a_spatial_w, "a_spatial_b": a_spatial_b, "b_conv_w": b_conv_w,
              "b_conv_b": b_conv_b, "b_gn_g": b_gn_g, "b_gn_b": b_gn_b, "out_norm_a_g": out_norm_a_g,
              "out_norm_b_g": out_norm_b_g, "w_out": w_out, "norm2_g": norm2_g, "w_ffn_in": w_ffn_in,
              "w_ffn_out": w_ffn_out, "ada_f_w": ada_f_w, "ada_f_b": ada_f_b, "norm_f_g": norm_f_g}
    order = list(shapes)
    outs = [loss, gx.reshape(x.shape)]
    for which in range(4):
        outs += [res[n][which].reshape(shapes[n].shape) for n in order]
    return tuple(outs)
```

```python
import math

import jax
import jax.numpy as jnp
from jax import lax
from jax.experimental import pallas as pl
from jax.experimental.pallas import tpu as pltpu

F32 = jnp.float32
BF16 = jnp.bfloat16

D = 1024
DA = 512
DB = 512
DIN = 2048
DFF = 2816
NH = 8
CH = 128
KW = 31
HALO = 32
NDEV = 8
WIN_B = DIN // NDEV
WFI_B = 2 * DFF // NDEV
NFB = DFF // WFI_B
EPS = 1e-6
NVEC = 40
VMEM_LIMIT = 56 * 1024 * 1024
FFN_VMEM_LIMIT = 62 * 1024 * 1024

ADAM_LR, ADAM_B1, ADAM_B2, ADAM_EPS, ADAM_WD, ADAM_STEP = 0.001, 0.9, 0.999, 1e-08, 0.01, 10

MESH = pl.DeviceIdType.MESH


def _dot(a, b):
    return jnp.dot(a, b, preferred_element_type=F32)


def _dot_nt(a, b):
    return lax.dot_general(a, b, (((1,), (1,)), ((), ())), preferred_element_type=F32)


def _dot_tn(a, b):
    return lax.dot_general(a, b, (((0,), (0,)), ((), ())), preferred_element_type=F32)


def _rs(v):
    return lax.rsqrt(jnp.mean(v * v, axis=-1, keepdims=True) + EPS)


def _sig(v):
    return 1.0 / (1.0 + jnp.exp(-v))


_INV_SQRT2 = 1.0 / math.sqrt(2.0)
_INV_SQRT2PI = 1.0 / math.sqrt(2.0 * math.pi)


def _gelu_parts(v):
    cdf = 0.5 * (1.0 + lax.erf(v * _INV_SQRT2))
    pdf = jnp.exp(-0.5 * v * v) * _INV_SQRT2PI
    return v * cdf, cdf + v * pdf


def _grp_mean(v, pm):
    hi = v.astype(BF16)
    lo = (v - hi.astype(F32)).astype(BF16)
    return _dot(hi, pm) + _dot(lo, pm)


def _colsum(v):
    return jnp.sum(v, axis=0, keepdims=True)


def _full(shape):
    nd = len(shape)
    return pl.BlockSpec(shape, lambda *_: (0,) * nd)


def _resident(shape):
    nd = len(shape)
    return pl.BlockSpec(shape, lambda *_: (0,) * nd, pipeline_mode=pl.Buffered(1))


HBM = pl.BlockSpec(memory_space=pl.ANY)
VM = pl.BlockSpec(memory_space=pltpu.VMEM)


SH_ROWS = HALO - 8


def _shifted_copies(buf, shbuf, tm):
    for b in range(1, 8):
        shbuf[b - 1] = buf[b:b + tm + SH_ROWS, :]


def _window(buf, shbuf, off, tm):
    a, b = divmod(off, 8)
    if b == 0:
        return buf[8 * a:8 * a + tm, :]
    return shbuf[b - 1, 8 * a:8 * a + tm, :]


def _first_head_lanes():
    return lax.broadcasted_iota(jnp.int32, (CH, CH), 1) < (DA // NH)


def _mix_heads(w_ref, vb, first):
    outs = []
    for p in range(NH // 2):
        v = vb[:, p * CH:(p + 1) * CH]
        a = _dot(w_ref[(2 * p) * CH:(2 * p + 1) * CH, :], v)
        b = _dot(w_ref[(2 * p + 1) * CH:(2 * p + 2) * CH, :], v)
        outs.append(jnp.where(first, a, b))
    return jnp.concatenate(outs, axis=1)


def _place():
    x, y, c = lax.axis_index("x"), lax.axis_index("y"), lax.axis_index("c")
    return x, y, c, 4 * x + 2 * y + c


def _dev(t):
    return (t >> 2, (t >> 1) & 1, t & 1)


class _AllGather:
    def __init__(self, w_in, w_out, wss, wrs, lsem):
        x, y, c, idx = _place()
        me, sibling = (x, y, c), (x, y, 1 - c)
        chips = [(1 - x, y), (x, 1 - y), (1 - x, 1 - y)]
        nw = len(w_in)

        def blk(p):
            return 4 * p[0] + 2 * p[1] + p[2]

        def wcopy(a, k, block, to, src=None):
            dst = w_out[a].at[blk(block)]
            return pltpu.make_async_remote_copy(src_ref=dst if src is None else src, dst_ref=dst,
                                                send_sem=wss.at[a, k], recv_sem=wrs.at[a, k],
                                                device_id=to, device_id_type=MESH)

        self.mine = [pltpu.make_async_copy(w_in[a], w_out[a].at[idx], lsem.at[a]) for a in range(nw)]
        self.first = []
        for a in range(nw):
            self.first.append(wcopy(a, 0, me, sibling, src=w_in[a]))
            self.first += [wcopy(a, 1 + j, me, (*chip, c), src=w_in[a]) for j, chip in enumerate(chips)]
        self.landed = [[wcopy(a, 1 + j, (*chip, c), me) for a in range(nw)] for j, chip in enumerate(chips)]
        self.passed = [[wcopy(a, 4 + j, (*chip, c), sibling) for a in range(nw)] for j, chip in enumerate(chips)]
        self.from_sibling = []
        for a in range(nw):
            self.from_sibling.append(wcopy(a, 0, sibling, me))
            self.from_sibling += [wcopy(a, 4 + j, (*chip, 1 - c), me) for j, chip in enumerate(chips)]

    def start(self):
        for cp in self.mine + self.first:
            cp.start()

    def forward(self):
        for land, pas in zip(self.landed, self.passed):
            for l, p in zip(land, pas):
                l.wait_recv()
                p.start()

    def finish(self):
        for cp in self.from_sibling:
            cp.wait_recv()
        for cp in self.first:
            cp.wait_send()
        for pas in self.passed:
            for p in pas:
                p.wait_send()
        for cp in self.mine:
            cp.wait()


AG_SEMS = lambda nw: [pltpu.SemaphoreType.DMA((nw, 7)), pltpu.SemaphoreType.DMA((nw, 7)),
                      pltpu.SemaphoreType.DMA((nw,))]


class _ReduceScatter:
    def __init__(self, g_in, r_out, gss, grs, lsem):
        x, y, c, idx = _place()
        me = (x, y, c)
        nw = len(g_in)
        self.mine = [pltpu.make_async_copy(g_in[a].at[idx], r_out[a].at[0], lsem.at[a]) for a in range(nw)]
        self.sends, self.recvs = [], []
        for k in range(1, NDEV):
            t = idx ^ k
            for a in range(nw):
                self.sends.append(pltpu.make_async_remote_copy(
                    src_ref=g_in[a].at[t], dst_ref=r_out[a].at[k], send_sem=gss.at[a, k - 1],
                    recv_sem=grs.at[a, k - 1], device_id=_dev(t), device_id_type=MESH))
                self.recvs.append(pltpu.make_async_remote_copy(
                    src_ref=g_in[a].at[0], dst_ref=r_out[a].at[k], send_sem=gss.at[a, k - 1],
                    recv_sem=grs.at[a, k - 1], device_id=me, device_id_type=MESH))

    def start(self):
        for cp in self.mine + self.sends:
            cp.start()

    def finish(self):
        for cp in self.recvs:
            cp.wait_recv()
        for cp in self.sends:
            cp.wait_send()
        for cp in self.mine:
            cp.wait()


RS_SEMS = AG_SEMS


def _mix_fwd(x, mod, g1, win, b_in, lng, lnb, wcat, bsf, cw, cb, gng, gnb, oga, ogb, wout, pm, ffn_shards, tm):
    T = x.shape[0]
    nt = T // tm
    nch = tm // CH
    nw = len(ffn_shards)
    fwd_step = (5 * nt) // 8
    saved = [(D, F32), (D, BF16), (2 * DB, F32), (DA, F32), (D, BF16), (D, F32), (DA, F32), (DA, F32), (DA, F32),
             (DA, F32), (CH, F32), (DB, F32), (DB, F32)]
    NSAVE = len(saved)

    def body(x_ref, mod_ref, g1_ref, win_ref, bin_ref, lng_ref, lnb_ref, wcat_ref, bsf_ref, cw_ref, cb_ref,
             gng_ref, gnb_ref, oga_ref, ogb_ref, wout_ref, pm_ref, *rest):
        sh_in = rest[:nw]
        (x1_ref, h_ref, zvg_ref, mixed_ref, y_ref, o_ref, gu_ref, dgu_ref, dgv_ref, vhat_ref, rsl_ref, yhat_ref,
         rsg_ref) = rest[nw:nw + NSAVE]
        sh_out = rest[nw + NSAVE:2 * nw + NSAVE]
        glbuf, shbuf, wss, wrs, lsem = rest[2 * nw + NSAVE:]
        i = pl.program_id(0)

        @pl.when(i == 0)
        def _():
            _AllGather(sh_in, sh_out, wss, wrs, lsem).start()

        xv = x_ref[...]
        shift1 = mod_ref[0:1, :]
        scale1 = mod_ref[1:2, :]
        gate1 = mod_ref[2:3, :]
        h = (xv * _rs(xv) * g1_ref[...]) * (1.0 + scale1) + shift1
        hb = h.astype(BF16)
        h_ref[...] = hb
        z = jnp.concatenate([_dot(hb, win_ref[j]) for j in range(NDEV)], axis=1) + bin_ref[...]
        zvg_ref[...] = z[:, 2 * DA:]
        gu, dgelu_u = _gelu_parts(z[:, 0:DA])
        gv, dgelu_v = _gelu_parts(z[:, DA:2 * DA])
        gu_ref[...] = gu
        dgu_ref[...] = dgelu_u
        dgv_ref[...] = dgelu_v
        xc = gv - jnp.mean(gv, axis=-1, keepdims=True)
        rsl = lax.rsqrt(jnp.mean(xc * xc, axis=-1, keepdims=True) + EPS)
        vhat = xc * rsl
        vhat_ref[...] = vhat
        rsl_ref[...] = jnp.broadcast_to(rsl, (tm, CH))
        vnb = (vhat * lng_ref[...] + lnb_ref[...]).astype(BF16)
        first = _first_head_lanes()
        chunks = []
        for ci in range(nch):
            chunks.append(_mix_heads(wcat_ref, vnb[ci * CH:(ci + 1) * CH, :], first) + bsf_ref[...])
        mixed = jnp.concatenate(chunks, axis=0) if nch > 1 else chunks[0]
        mixed_ref[...] = mixed
        ya = gu * mixed
        gl = z[:, 2 * DA:2 * DA + DB] * _sig(z[:, 2 * DA + DB:])

        @pl.when(i == 0)
        def _():
            glbuf[0:HALO, :] = jnp.zeros((HALO, DB), F32)

        glbuf[HALO:HALO + tm, :] = gl
        _shifted_copies(glbuf, shbuf, tm)
        yc = jnp.zeros((tm, DB), F32) + cb_ref[...]
        for k in range(KW):
            yc = yc + cw_ref[k:k + 1, :] * _window(glbuf, shbuf, HALO - (KW - 1) + k, tm)
        glbuf[0:HALO, :] = gl[tm - HALO:, :]
        pmv = pm_ref[...]
        dc = yc - _grp_mean(yc, pmv)
        rsg = lax.rsqrt(_grp_mean(dc * dc, pmv) + EPS)
        yhat = dc * rsg
        yhat_ref[...] = yhat
        rsg_ref[...] = rsg
        yg = yhat * gng_ref[...] + gnb_ref[...]
        yb = yg * _sig(yg)
        na = ya * _rs(ya) * oga_ref[...]
        nb = yb * _rs(yb) * ogb_ref[...]
        yv = jnp.concatenate([na, nb], axis=1).astype(BF16)
        y_ref[...] = yv
        o = _dot(yv, wout_ref[...])
        o_ref[...] = o
        x1_ref[...] = xv + gate1 * o

        @pl.when(i == fwd_step)
        def _():
            _AllGather(sh_in, sh_out, wss, wrs, lsem).forward()

        @pl.when(i == nt - 1)
        def _():
            _AllGather(sh_in, sh_out, wss, wrs, lsem).finish()

    tile = lambda w: pl.BlockSpec((tm, w), lambda i: (i, 0))
    outs = pl.pallas_call(
        body,
        name="mix_fwd",
        grid=(nt,),
        in_specs=[tile(D), _full((8, D)), _full((1, D)), _resident((NDEV, D, WIN_B)), _full((1, DIN)),
                  _full((1, DA)), _full((1, DA)), _full((NH * CH, CH)), _full((CH, DA)), _full((HALO, DB)),
                  _full((1, DB)), _full((1, DB)), _full((1, DB)), _full((1, DA)), _full((1, DB)),
                  _resident((D, D)), _full((DB, DB))] + [HBM] * nw,
        out_specs=[tile(w) for w, _ in saved] + [HBM] * nw,
        out_shape=[jax.ShapeDtypeStruct((T, w), dt) for w, dt in saved]
                  + [jax.ShapeDtypeStruct((NDEV,) + s.shape, s.dtype) for s in ffn_shards],
        scratch_shapes=[pltpu.VMEM((HALO + tm, DB), F32), pltpu.VMEM((7, tm + SH_ROWS, DB), F32)] + AG_SEMS(nw),
        compiler_params=pltpu.CompilerParams(dimension_semantics=("arbitrary",), vmem_limit_bytes=VMEM_LIMIT),
    )(x, mod, g1, win, b_in, lng, lnb, wcat, bsf, cw, cb, gng, gnb, oga, ogb, wout, pm, *ffn_shards)
    return outs[:NSAVE], outs[NSAVE:]


def _ffn(x1, tgt, mod, g2, gf, wfi, wfo, tm):
    T = x1.shape[0]
    nt = T // tm

    def body(x1_ref, tgt_ref, mod_ref, g2_ref, gf_ref, wfi_ref, wfo_ref,
             dx1_ref, h2_ref, dgu_ref, acc_ref, gwfo_ref, g_s, u_s, acc_wfo, act_s, dxg_s, st_wfo):
        i = pl.program_id(0)
        cur = i % 2
        prv = 1 - cur

        @pl.when(i == 0)
        def _():
            acc_ref[...] = jnp.zeros((8, D), F32)
            acc_wfo[...] = jnp.zeros((NFB, WFI_B, D), F32)
            act_s[1] = jnp.zeros((NFB, tm, WFI_B), BF16)
            dxg_s[1] = jnp.zeros((tm, D), BF16)

        x1 = x1_ref[...]
        shift2 = mod_ref[3:4, :]
        scale2 = mod_ref[4:5, :]
        gate2 = mod_ref[5:6, :]
        shiftf = mod_ref[6:7, :]
        scalef = mod_ref[7:8, :]
        g2v = g2_ref[...]
        gfv = gf_ref[...]
        r2 = _rs(x1)
        xn2 = x1 * r2
        h2b = (xn2 * g2v * (1.0 + scale2) + shift2).astype(BF16)
        h2_ref[...] = h2b
        f = jnp.zeros((tm, D), F32)
        for j in range(NFB):
            g = _dot_nt(h2b, wfi_ref[j])
            u = _dot_nt(h2b, wfi_ref[NFB + j])
            g_s[j] = g
            u_s[j] = u
            actb = (g * _sig(g) * u).astype(BF16)
            act_s[cur, j] = actb
            f = f + _dot(actb, wfo_ref[j * WFI_B:(j + 1) * WFI_B, :])
        dxg_prev = dxg_s[prv]
        for j in range(NFB):
            acc_wfo[j] += _dot_tn(act_s[prv, j], dxg_prev)
        x2 = x1 + gate2 * f
        rf = _rs(x2)
        xnf = x2 * rf
        out = xnf * gfv * (1.0 + scalef) + shiftf
        e = out - tgt_ref[...]
        dout = e * (1.0 / D)
        acc_ref[7:8, :] += _colsum(e * e)
        acc_ref[0:1, :] += _colsum(dout)
        acc_ref[1:2, :] += _colsum(dout * xnf * gfv)
        acc_ref[2:3, :] += _colsum(dout * (1.0 + scalef) * xnf)
        dxnf = dout * (1.0 + scalef) * gfv
        dx2 = rf * (dxnf - xnf * jnp.mean(dxnf * xnf, axis=-1, keepdims=True))
        acc_ref[3:4, :] += _colsum(dx2 * f)
        dxgb = (dx2 * gate2).astype(BF16)
        dxg_s[cur] = dxgb
        dh2 = jnp.zeros((tm, D), F32)
        for j in range(NFB):
            dact = _dot_nt(dxgb, wfo_ref[j * WFI_B:(j + 1) * WFI_B, :])
            g = g_s[j]
            u = u_s[j]
            s = _sig(g)
            dgb = (dact * u * (s * (1.0 + g * (1.0 - s)))).astype(BF16)
            dub = (dact * (g * s)).astype(BF16)
            dgu_ref[j] = dgb
            dgu_ref[NFB + j] = dub
            dh2 = dh2 + _dot(dgb, wfi_ref[j])
            dh2 = dh2 + _dot(dub, wfi_ref[NFB + j])
        acc_ref[4:5, :] += _colsum(dh2)
        acc_ref[5:6, :] += _colsum(dh2 * xn2 * g2v)
        acc_ref[6:7, :] += _colsum(dh2 * (1.0 + scale2) * xn2)
        dxn2 = dh2 * (1.0 + scale2) * g2v
        dx1_ref[...] = dx2 + r2 * (dxn2 - xn2 * jnp.mean(dxn2 * xn2, axis=-1, keepdims=True))

        @pl.when(i == nt - 1)
        def _():
            for j in range(NFB):
                st_wfo[...] = (acc_wfo[j] + _dot_tn(act_s[cur, j], dxgb)).astype(BF16)
                pltpu.sync_copy(st_wfo, gwfo_ref.at[j])

    tile = lambda w: pl.BlockSpec((tm, w), lambda i: (i, 0))
    blocked = lambda n: pl.BlockSpec((n, tm, WFI_B), lambda i: (0, i, 0))
    return pl.pallas_call(
        body,
        name="ffn_fwd_bwd",
        grid=(nt,),
        in_specs=[tile(D), tile(D), _full((8, D)), _full((1, D)), _full((1, D)),
                  _resident((NDEV, WFI_B, D)), _resident((DFF, D))],
        out_specs=[tile(D), tile(D), blocked(NDEV), _full((8, D)), HBM],
        out_shape=[jax.ShapeDtypeStruct((T, D), F32), jax.ShapeDtypeStruct((T, D), BF16),
                   jax.ShapeDtypeStruct((NDEV, T, WFI_B), BF16), jax.ShapeDtypeStruct((8, D), F32),
                   jax.ShapeDtypeStruct((NFB, WFI_B, D), BF16)],
        scratch_shapes=[pltpu.VMEM((NFB, tm, WFI_B), F32), pltpu.VMEM((NFB, tm, WFI_B), F32),
                        pltpu.VMEM((NFB, WFI_B, D), F32), pltpu.VMEM((2, NFB, tm, WFI_B), BF16),
                        pltpu.VMEM((2, tm, D), BF16), pltpu.VMEM((WFI_B, D), BF16)],
        compiler_params=pltpu.CompilerParams(dimension_semantics=("arbitrary",), vmem_limit_bytes=FFN_VMEM_LIMIT),
    )(x1, tgt, mod, g2, gf, wfi, wfo)


def _mix_bwd(dx1, x, zvg, mixed, o, hb, yb, gu, dgu, dgv, vhat, rslb, yhat, rsg, mod, g1, win, lng, lnb, wcat, wcat_t,
             cw, gng, gnb, oga, ogb, wout, pm, esel, after, tm):
    T = x.shape[0]
    nt = T // tm
    nch = tm // CH
    WOB = 256

    def body(dx1_ref, x_ref, zvg_ref, mixed_ref, o_ref, hb_ref, yb_ref, gu_ref, dgu_ref, dgv_ref, vhat_ref, rsl_ref,
             yhat_ref, rsg_ref, mod_ref, g1_ref, win_ref, lng_ref, lnb_ref, wcat_ref, wcatt_ref, cw_ref, gng_ref,
             gnb_ref, oga_ref, ogb_ref, wout_ref, pm_ref, esel_ref, after_ref,
             gx_ref, accv_ref, accb_ref, acca_ref, accbs_ref, accws_ref, acccw_ref, gwin_ref, gwout_ref,
             dycbuf, shbuf, bs_s, acc_win, acc_wout, st_win, st_wout):
        i = pl.program_id(0)

        @pl.when(i == 0)
        def _():
            acc_win[...] = jnp.zeros((NDEV, D, WIN_B), F32)
            acc_wout[...] = jnp.zeros((D, D), F32)
            accv_ref[...] = jnp.zeros((8, D), F32)
            accb_ref[...] = jnp.zeros((1, DIN), F32)
            acca_ref[...] = jnp.zeros((8, DA), F32)
            accws_ref[...] = jnp.zeros((NH * CH, CH), F32)
            acccw_ref[...] = jnp.zeros((HALO, DB), F32)
            bs_s[...] = jnp.zeros((CH, DA), F32)
            dycbuf[tm:tm + HALO, :] = jnp.zeros((HALO, DB), F32)

        shift1 = mod_ref[0:1, :]
        scale1 = mod_ref[1:2, :]
        gate1 = mod_ref[2:3, :]
        g1v = g1_ref[...]
        xv = x_ref[...]
        r1 = _rs(xv)
        xn1 = xv * r1
        val = zvg_ref[:, 0:DB]
        gate = zvg_ref[:, DB:]
        gu = gu_ref[...]
        dgelu_u = dgu_ref[...]
        dgelu_v = dgv_ref[...]
        vhat = vhat_ref[...]
        rsl = rsl_ref[:, 0:1]
        lngv = lng_ref[...]
        vnb = (vhat * lngv + lnb_ref[...]).astype(BF16)
        mixed = mixed_ref[...]
        ya = gu * mixed
        ra = _rs(ya)
        yan = ya * ra
        sgt = _sig(gate)
        gl = val * sgt
        pmv = pm_ref[...]
        rsg = rsg_ref[...]
        yhat = yhat_ref[...]
        gngv = gng_ref[...]
        yg = yhat * gngv + gnb_ref[...]
        sgy = _sig(yg)
        yb = yg * sgy
        rb = _rs(yb)
        ybn = yb * rb
        dx1 = dx1_ref[...]
        accv_ref[0:1, :] += _colsum(dx1 * o_ref[...])
        dogb = (dx1 * gate1).astype(BF16)
        acc_wout[...] += _dot_tn(yb_ref[...], dogb)
        dy = _dot_nt(dogb, wout_ref[...])
        dna = dy[:, 0:DA]
        dnb = dy[:, DA:]
        ogav = oga_ref[...]
        ogbv = ogb_ref[...]
        acca_ref[2:3, :] += _colsum(dna * yan)
        acca_ref[3:4, :] += _colsum(dnb * ybn)
        ta = dna * ogav
        dya = ra * (ta - yan * jnp.mean(ta * yan, axis=-1, keepdims=True))
        tb = dnb * ogbv
        dyb = rb * (tb - ybn * jnp.mean(tb * ybn, axis=-1, keepdims=True))
        dgu = dya * mixed
        dm = dya * gu
        first = _first_head_lanes()
        zero = jnp.zeros((CH, CH), BF16)
        dvn_chunks = []
        bs_acc = bs_s[...]
        for ci in range(nch):
            dmc = dm[ci * CH:(ci + 1) * CH, :]
            bs_acc = bs_acc + dmc
            dmcb = dmc.astype(BF16)
            dvn_chunks.append(_mix_heads(wcatt_ref, dmcb, first))
            vc = vnb[ci * CH:(ci + 1) * CH, :]
            for p in range(NH // 2):
                xt = dmcb[:, p * CH:(p + 1) * CH]
                vt = vc[:, p * CH:(p + 1) * CH]
                accws_ref[(2 * p) * CH:(2 * p + 1) * CH, :] += _dot_nt(jnp.where(first, xt, zero), vt)
                accws_ref[(2 * p + 1) * CH:(2 * p + 2) * CH, :] += _dot_nt(jnp.where(first, zero, xt), vt)
        bs_s[...] = bs_acc
        dvn = jnp.concatenate(dvn_chunks, axis=0) if nch > 1 else dvn_chunks[0]
        acca_ref[0:1, :] += _colsum(dvn * vhat)
        acca_ref[1:2, :] += _colsum(dvn)
        dvh = dvn * lngv
        dgv = rsl * (dvh - jnp.mean(dvh, axis=-1, keepdims=True)
                     - vhat * jnp.mean(dvh * vhat, axis=-1, keepdims=True))
        du = dgu * dgelu_u
        dv = dgv * dgelu_v
        dyg = dyb * (sgy * (1.0 + yg * (1.0 - sgy)))
        acca_ref[5:6, :] += _colsum(dyg * yhat)
        acca_ref[6:7, :] += _colsum(dyg)
        dyh = dyg * gngv
        dyc = rsg * (dyh - _grp_mean(dyh, pmv) - yhat * _grp_mean(dyh * yhat, pmv))
        acca_ref[4:5, :] += _colsum(dyc)
        dycbuf[0:tm, :] = dyc
        _shifted_copies(dycbuf, shbuf, tm)
        dgl = jnp.zeros((tm, DB), F32)
        for k in range(KW):
            win_k = _window(dycbuf, shbuf, KW - 1 - k, tm)
            dgl = dgl + cw_ref[k:k + 1, :] * win_k
            acccw_ref[k:k + 1, :] += _colsum(win_k * gl)
        dycbuf[tm:tm + HALO, :] = dyc[0:HALO, :]
        dval = dgl * sgt
        dgate = dgl * val * sgt * (1.0 - sgt)
        dz = jnp.concatenate([du, dv, dval, dgate], axis=1)
        accb_ref[...] += _colsum(dz)
        dzb = dz.astype(BF16)
        hbv = hb_ref[...]
        dh = jnp.zeros((tm, D), F32)
        for j in range(NDEV):
            dzj = dzb[:, j * WIN_B:(j + 1) * WIN_B]
            acc_win[j] += _dot_tn(hbv, dzj)
            dh = dh + _dot_nt(dzj, win_ref[j])
        accv_ref[1:2, :] += _colsum(dh)
        accv_ref[2:3, :] += _colsum(dh * xn1 * g1v)
        accv_ref[3:4, :] += _colsum(dh * (1.0 + scale1) * xn1)
        dxn1 = dh * (1.0 + scale1) * g1v
        gx_ref[...] = dx1 + r1 * (dxn1 - xn1 * jnp.mean(dxn1 * xn1, axis=-1, keepdims=True))

        @pl.when(i == nt - 1)
        def _():
            rows = lax.broadcasted_iota(jnp.int32, (NH * CH, CH), 0) & (CH - 1)
            cols = lax.broadcasted_iota(jnp.int32, (NH * CH, CH), 1)
            accws_ref[...] = jnp.where(cols <= rows, accws_ref[...], 0.0)
            bs = bs_s[...]
            hi = bs.astype(BF16)
            r1_ = bs - hi.astype(F32)
            mid = r1_.astype(BF16)
            lo = (r1_ - mid.astype(F32)).astype(BF16)
            ev = esel_ref[...]
            accbs_ref[...] = _dot(hi, ev) + _dot(mid, ev) + _dot(lo, ev)
            for j in range(NDEV):
                st_win[...] = acc_win[j].astype(BF16)
                pltpu.sync_copy(st_win, gwin_ref.at[j])
            for j in range(D // WOB):
                st_wout[...] = acc_wout[j * WOB:(j + 1) * WOB, :].astype(BF16)
                pltpu.sync_copy(st_wout, gwout_ref.at[pl.ds(j * WOB, WOB)])

    rev = lambda w: pl.BlockSpec((tm, w), lambda i: (nt - 1 - i, 0))
    outs = pl.pallas_call(
        body,
        name="mix_bwd",
        grid=(nt,),
        in_specs=[rev(D), rev(D), rev(2 * DB), rev(DA), rev(D), rev(D), rev(D), rev(DA), rev(DA), rev(DA), rev(DA),
                  rev(CH), rev(DB), rev(DB), _full((8, D)), _full((1, D)),
                  _resident((NDEV, D, WIN_B)), _full((1, DA)), _full((1, DA)), _full((NH * CH, CH)),
                  _full((NH * CH, CH)), _full((HALO, DB)), _full((1, DB)), _full((1, DB)), _full((1, DA)),
                  _full((1, DB)), _resident((D, D)), _full((DB, DB)), _full((DA, CH)), HBM],
        out_specs=[rev(D), _full((8, D)), _full((1, DIN)), _full((8, DA)), _full((CH, CH)),
                   _full((NH * CH, CH)), _full((HALO, DB)), HBM, HBM],
        out_shape=[jax.ShapeDtypeStruct((T, D), F32), jax.ShapeDtypeStruct((8, D), F32),
                   jax.ShapeDtypeStruct((1, DIN), F32), jax.ShapeDtypeStruct((8, DA), F32),
                   jax.ShapeDtypeStruct((CH, CH), F32), jax.ShapeDtypeStruct((NH * CH, CH), F32),
                   jax.ShapeDtypeStruct((HALO, DB), F32),
                   jax.ShapeDtypeStruct((NDEV, D, WIN_B), BF16), jax.ShapeDtypeStruct((D, D), BF16)],
        scratch_shapes=[pltpu.VMEM((tm + HALO, DB), F32), pltpu.VMEM((7, tm + SH_ROWS, DB), F32),
                        pltpu.VMEM((CH, DA), F32), pltpu.VMEM((NDEV, D, WIN_B), F32), pltpu.VMEM((D, D), F32),
                        pltpu.VMEM((D, WIN_B), BF16), pltpu.VMEM((WOB, D), BF16)],
        compiler_params=pltpu.CompilerParams(dimension_semantics=("arbitrary",), vmem_limit_bytes=VMEM_LIMIT),
    )(dx1, x, zvg, mixed, o, hb, yb, gu, dgu, dgv, vhat, rslb, yhat, rsg, mod, g1, win, lng, lnb, wcat, wcat_t, cw,
      gng, gnb, oga, ogb, wout, pm, esel, after)
    return outs[:7], outs[7:]


def _wgrad_cols(a, b, nblk, tk, name):
    T, M = a.shape
    bw = b.shape[1] // nblk
    nk = T // tk

    def body(a_ref, b_ref, o_ref, acc):
        k = pl.program_id(0)

        @pl.when(k == 0)
        def _():
            acc[...] = jnp.zeros((nblk, M, bw), F32)

        av = a_ref[...]
        for j in range(nblk):
            acc[j] += _dot_tn(av, b_ref[:, j * bw:(j + 1) * bw])

        @pl.when(k == nk - 1)
        def _():
            o_ref[...] = acc[...].astype(BF16)

    return pl.pallas_call(
        body, name=name, grid=(nk,),
        in_specs=[pl.BlockSpec((tk, M), lambda k: (k, 0)), pl.BlockSpec((tk, nblk * bw), lambda k: (k, 0))],
        out_specs=_full((nblk, M, bw)),
        out_shape=jax.ShapeDtypeStruct((nblk, M, bw), BF16),
        scratch_shapes=[pltpu.VMEM((nblk, M, bw), F32)],
        compiler_params=pltpu.CompilerParams(dimension_semantics=("arbitrary",), vmem_limit_bytes=VMEM_LIMIT),
    )(a, b)


def _wgrad_b_blocked(a, b3, per, tk, name):
    T, M = a.shape
    nb, _, bw = b3.shape
    nk = T // tk

    def body(a_ref, b_ref, o_ref, acc):
        k = pl.program_id(1)

        @pl.when(k == 0)
        def _():
            acc[...] = jnp.zeros((per, bw, M), F32)

        av = a_ref[...]
        for j in range(per):
            acc[j] += _dot_tn(b_ref[j], av)

        @pl.when(k == nk - 1)
        def _():
            o_ref[...] = acc[...].astype(BF16)

    return pl.pallas_call(
        body, name=name, grid=(nb // per, nk),
        in_specs=[pl.BlockSpec((tk, M), lambda j, k: (k, 0)), pl.BlockSpec((per, tk, bw), lambda j, k: (j, k, 0))],
        out_specs=pl.BlockSpec((per, bw, M), lambda j, k: (j, 0, 0)),
        out_shape=jax.ShapeDtypeStruct((nb, bw, M), BF16),
        scratch_shapes=[pltpu.VMEM((per, bw, M), F32)],
        compiler_params=pltpu.CompilerParams(dimension_semantics=("arbitrary", "arbitrary"),
                                             vmem_limit_bytes=VMEM_LIMIT),
    )(a, b3)


def _wgrad_a_blocked(a3, b, tk, name):
    nb, T, bw = a3.shape
    N = b.shape[1]
    nk = T // tk

    def body(a_ref, b_ref, o_ref, acc):
        k = pl.program_id(1)

        @pl.when(k == 0)
        def _():
            acc[...] = jnp.zeros((bw, N), F32)

        acc[...] += _dot_tn(a_ref[0], b_ref[...])

        @pl.when(k == nk - 1)
        def _():
            o_ref[0] = acc[...].astype(BF16)

    return pl.pallas_call(
        body, name=name, grid=(nb, nk),
        in_specs=[pl.BlockSpec((1, tk, bw), lambda j, k: (j, k, 0)), pl.BlockSpec((tk, N), lambda j, k: (k, 0))],
        out_specs=pl.BlockSpec((1, bw, N), lambda j, k: (j, 0, 0)),
        out_shape=jax.ShapeDtypeStruct((nb, bw, N), BF16),
        scratch_shapes=[pltpu.VMEM((bw, N), F32)],
        compiler_params=pltpu.CompilerParams(dimension_semantics=("arbitrary", "arbitrary"),
                                             vmem_limit_bytes=VMEM_LIMIT),
    )(a3, b)


def _small_copy(src, dst, ss, rs, k, to):
    return pltpu.make_async_remote_copy(src_ref=src, dst_ref=dst, send_sem=ss.at[k], recv_sem=rs.at[k],
                                        device_id=to, device_id_type=MESH)


def _gather(c_row, ada_w, ada_b8, ada_f_w, ada_f_b8, conv_s, shards):
    nw = len(shards)

    def body(c_ref, adaw_ref, adab_ref, adafw_ref, adafb_ref, conv_ref, *rest):
        w_in = rest[:nw]
        call_ref, cparts_ref, cfparts_ref, convg_ref = rest[nw:nw + 4]
        w_out = rest[nw + 4:2 * nw + 4]
        part_s, partf_s, wss, wrs, lsem, s1, r1, s2, r2, s3, r3, s4, r4 = rest[2 * nw + 4:]
        x, y, c, idx = _place()
        me = (x, y, c)
        ag = _AllGather(w_in, w_out, wss, wrs, lsem)
        ag.start()
        call_ref[pl.ds(idx, 1), :] = c_ref[...]
        convg_ref[idx] = conv_ref[...]
        ph1 = []
        for k in range(1, NDEV):
            to = _dev(idx ^ k)
            ph1.append(_small_copy(c_ref, call_ref.at[pl.ds(idx, 1)], s1, r1, k - 1, to))
            ph1.append(_small_copy(conv_ref, convg_ref.at[idx], s2, r2, k - 1, to))
        for cp in ph1:
            cp.start()
        for k in range(1, NDEV):
            src_dev = idx ^ k
            _small_copy(c_ref, call_ref.at[pl.ds(src_dev, 1)], s1, r1, k - 1, me).wait_recv()
            _small_copy(conv_ref, convg_ref.at[src_dev], s2, r2, k - 1, me).wait_recv()
        call = call_ref[...]
        cact = (call * _sig(call))
        part_s[...] = jnp.dot(cact, adaw_ref[...], preferred_element_type=F32,
                              precision=lax.Precision.HIGHEST) + adab_ref[pl.ds(idx, 1), :]
        partf_s[...] = jnp.dot(cact, adafw_ref[...], preferred_element_type=F32,
                               precision=lax.Precision.HIGHEST) + adafb_ref[pl.ds(idx, 1), :]
        cparts_ref[pl.ds(idx, 1), :] = part_s[pl.ds(idx, 1), :]
        cfparts_ref[pl.ds(idx, 1), :] = partf_s[pl.ds(idx, 1), :]
        ph2 = []
        for k in range(1, NDEV):
            t = idx ^ k
            ph2.append(_small_copy(part_s.at[pl.ds(t, 1)], cparts_ref.at[pl.ds(idx, 1)], s3, r3, k - 1, _dev(t)))
            ph2.append(_small_copy(partf_s.at[pl.ds(t, 1)], cfparts_ref.at[pl.ds(idx, 1)], s4, r4, k - 1, _dev(t)))
        for cp in ph2:
            cp.start()
        for k in range(1, NDEV):
            src_dev = idx ^ k
            _small_copy(part_s.at[pl.ds(0, 1)], cparts_ref.at[pl.ds(src_dev, 1)], s3, r3, k - 1, me).wait_recv()
            _small_copy(partf_s.at[pl.ds(0, 1)], cfparts_ref.at[pl.ds(src_dev, 1)], s4, r4, k - 1, me).wait_recv()
        for cp in ph1 + ph2:
            cp.wait_send()
        ag.forward()
        ag.finish()

    dma7 = pltpu.SemaphoreType.DMA((NDEV - 1,))
    outs = pl.pallas_call(
        body,
        name="gather_weights",
        in_specs=[VM] * 6 + [HBM] * nw,
        out_specs=[VM] * 4 + [HBM] * nw,
        out_shape=[jax.ShapeDtypeStruct((NDEV, D), F32), jax.ShapeDtypeStruct((NDEV, ada_w.shape[1]), F32),
                   jax.ShapeDtypeStruct((NDEV, ada_f_w.shape[1]), F32),
                   jax.ShapeDtypeStruct((NDEV,) + conv_s.shape, F32)]
                  + [jax.ShapeDtypeStruct((NDEV,) + s.shape, s.dtype) for s in shards],
        scratch_shapes=[pltpu.VMEM((NDEV, ada_w.shape[1]), F32), pltpu.VMEM((NDEV, ada_f_w.shape[1]), F32)]
                       + AG_SEMS(nw) + [dma7] * 8,
        compiler_params=pltpu.CompilerParams(vmem_limit_bytes=VMEM_LIMIT),
    )(c_row, ada_w, ada_b8, ada_f_w, ada_f_b8, conv_s, *shards)
    return outs[0], outs[1], outs[2], outs[3], outs[4:]


_VEC_AT = {
    "norm1_g": (8, 0, D), "a_ln_g": (11, 0, DA), "a_ln_b": (11, DA, DA), "a_spatial_b": (12, 0, D),
    "b_conv_b": (13, 0, DB), "b_gn_g": (13, DB, DB), "b_gn_b": (14, 0, DB), "out_norm_a_g": (14, DB, DA),
    "out_norm_b_g": (15, 0, DB), "norm2_g": (16, 0, D), "norm_f_g": (17, 0, D),
}
_LOSS_ROW = 18
_CW_ROW = 24


def _reduce_small(acc_f, acc_v, acc_b, acc_a, acc_bs, acc_cw, dws, after):
    def body(accf_ref, accv_ref, accb_ref, acca_ref, accbs_ref, acccw_ref, dws_ref, after_ref,
             vsum_ref, dcond_ref, wssum_ref, vloc, vbuf, wbuf, wown, s1, r1, s2, r2, s3, r3):
        x, y, c, idx = _place()
        me = (x, y, c)
        vloc[...] = jnp.zeros((NVEC, D), F32)
        vloc[0:1, :] = accv_ref[1:2, :]
        vloc[1:2, :] = accv_ref[2:3, :]
        vloc[2:3, :] = accv_ref[0:1, :]
        vloc[3:4, :] = accf_ref[4:5, :]
        vloc[4:5, :] = accf_ref[5:6, :]
        vloc[5:6, :] = accf_ref[3:4, :]
        vloc[6:7, :] = accf_ref[0:1, :]
        vloc[7:8, :] = accf_ref[1:2, :]
        vloc[8:9, :] = accv_ref[3:4, :]
        vloc[9:10, :] = accb_ref[:, 0:D]
        vloc[10:11, :] = accb_ref[:, D:]
        vloc[11:12, 0:DA] = acca_ref[0:1, :]
        vloc[11:12, DA:] = acca_ref[1:2, :]
        bst = accbs_ref[...].T
        for h in range(NH):
            vloc[12:13, h * CH:(h + 1) * CH] = bst[h:h + 1, :]
        vloc[13:14, 0:DB] = acca_ref[4:5, :]
        vloc[13:14, DB:] = acca_ref[5:6, :]
        vloc[14:15, 0:DB] = acca_ref[6:7, :]
        vloc[14:15, DB:] = acca_ref[2:3, :]
        vloc[15:16, 0:DB] = acca_ref[3:4, :]
        vloc[16:17, :] = accf_ref[6:7, :]
        vloc[17:18, :] = accf_ref[2:3, :]
        vloc[_LOSS_ROW:_LOSS_ROW + 1, :] = accf_ref[7:8, :]
        vloc[_CW_ROW:_CW_ROW + HALO // 2, 0:DB] = acccw_ref[0:HALO // 2, :]
        vloc[_CW_ROW:_CW_ROW + HALO // 2, DB:] = acccw_ref[HALO // 2:, :]
        vbuf[idx] = vloc[...]
        rows_of = lambda t: pl.ds(pl.multiple_of(t * CH, CH), CH)
        wbuf[0] = dws_ref[rows_of(idx), :]
        sm = []
        for k in range(1, NDEV):
            t = idx ^ k
            sm.append(_small_copy(vloc, vbuf.at[idx], s1, r1, k - 1, _dev(t)))
            sm.append(_small_copy(dws_ref.at[rows_of(t)], wbuf.at[k], s2, r2, k - 1, _dev(t)))
        for cp in sm:
            cp.start()
        for k in range(1, NDEV):
            _small_copy(dws_ref.at[rows_of(0)], wbuf.at[k], s2, r2, k - 1, me).wait_recv()
        ws = wbuf[0]
        for k in range(1, NDEV):
            ws = ws + wbuf[k]
        wown[...] = ws
        wssum_ref[rows_of(idx), :] = ws
        ag = [_small_copy(wown, wssum_ref.at[rows_of(idx)], s3, r3, k - 1, _dev(idx ^ k)) for k in range(1, NDEV)]
        for cp in ag:
            cp.start()
        for k in range(1, NDEV):
            _small_copy(vloc, vbuf.at[idx ^ k], s1, r1, k - 1, me).wait_recv()
        vs = vbuf[0]
        for d in range(1, NDEV):
            vs = vs + vbuf[d]
        vsum_ref[...] = vs
        for d in range(NDEV):
            dcond_ref[d] = vbuf[d, 0:8, :]
        for k in range(1, NDEV):
            _small_copy(wown, wssum_ref.at[rows_of(idx ^ k)], s3, r3, k - 1, me).wait_recv()
        for cp in sm + ag:
            cp.wait_send()

    dma7 = pltpu.SemaphoreType.DMA((NDEV - 1,))
    return pl.pallas_call(
        body,
        name="reduce_small",
        in_specs=[VM] * 7 + [HBM],
        out_specs=[VM, VM, VM],
        out_shape=[jax.ShapeDtypeStruct((NVEC, D), F32), jax.ShapeDtypeStruct((NDEV, 8, D), F32),
                   jax.ShapeDtypeStruct(dws.shape, F32)],
        scratch_shapes=[pltpu.VMEM((NVEC, D), F32), pltpu.VMEM((NDEV, NVEC, D), F32),
                        pltpu.VMEM((NDEV, CH, CH), F32), pltpu.VMEM((CH, CH), F32)] + [dma7] * 6,
        compiler_params=pltpu.CompilerParams(vmem_limit_bytes=VMEM_LIMIT),
    )(acc_f, acc_v, acc_b, acc_a, acc_bs, acc_cw, dws, after)


HBM_ONLY = pl.BlockSpec(memory_space=pltpu.HBM)
SEM = pl.BlockSpec(memory_space=pltpu.SEMAPHORE)
EFFECT = pltpu.SideEffectType.DATAFLOW_SIDE_EFFECTING


def _rs_copies(g_refs, land_refs, sems):
    x, y, c, idx = _place()
    cps = []
    for k in range(1, NDEV):
        t = idx ^ k
        for a in range(len(g_refs)):
            n = len(cps)
            cps.append(pltpu.make_async_remote_copy(
                src_ref=g_refs[a].at[t], dst_ref=land_refs[a].at[k - 1], send_sem=sems[2 * n],
                recv_sem=sems[2 * n + 1], device_id=_dev(t), device_id_type=MESH))
    return cps


def _rs_start(grads, name, after=()):
    nw = len(grads)
    nsem = 2 * nw * (NDEV - 1)
    lands = [lax.empty((NDEV - 1,) + g.shape[1:], g.dtype) for g in grads]

    def body(*refs):
        g_refs, land_refs = refs[:nw], refs[nw:2 * nw]
        sems = refs[2 * nw + len(after):2 * nw + len(after) + nsem]
        token = refs[-1]
        for cp in _rs_copies(g_refs, land_refs, sems):
            cp.start()
        token[...] = jnp.zeros_like(token)

    outs = pl.pallas_call(
        body, name=name,
        out_shape=(*[pltpu.SemaphoreType.DMA(())] * nsem,
                   *[pltpu.HBM(g.shape, g.dtype) for g in grads], *[pltpu.HBM(l.shape, l.dtype) for l in lands],
                   jax.ShapeDtypeStruct((8, CH), F32)),
        in_specs=[HBM_ONLY] * (2 * nw) + [HBM] * len(after),
        out_specs=(*[SEM] * nsem, *[HBM_ONLY] * (2 * nw), VM),
        input_output_aliases={i: nsem + i for i in range(2 * nw)},
        compiler_params=pltpu.CompilerParams(has_side_effects=EFFECT),
    )(*[pltpu.with_memory_space_constraint(g, pltpu.HBM) for g in grads],
      *[pltpu.with_memory_space_constraint(l, pltpu.HBM) for l in lands], *after)
    return outs[:nsem], outs[nsem:nsem + nw], outs[nsem + nw:nsem + 2 * nw], outs[-1]


def _rs_wait(sems, g_thru, land_thru, after, name):
    nw = len(g_thru)
    nsem = len(sems)

    def body(*refs):
        g_refs, land_refs = refs[:nw], refs[nw:2 * nw]
        for cp in _rs_copies(g_refs, land_refs, refs[2 * nw:2 * nw + nsem]):
            cp.wait_send()
            cp.wait_recv()

    outs = pl.pallas_call(
        body, name=name,
        out_shape=tuple(pltpu.HBM(a.shape, a.dtype) for a in list(g_thru) + list(land_thru)),
        in_specs=[HBM_ONLY] * (2 * nw) + [SEM] * nsem + [HBM] * len(after),
        out_specs=tuple([HBM_ONLY] * (2 * nw)),
        input_output_aliases={i: i for i in range(2 * nw)},
        compiler_params=pltpu.CompilerParams(has_side_effects=EFFECT),
    )(*g_thru, *land_thru, *sems, *after)
    return outs[:nw], outs[nw:]


def _adamw(w, g, m, v):
    m2 = ADAM_B1 * m + (1.0 - ADAM_B1) * g
    v2 = ADAM_B2 * v + (1.0 - ADAM_B2) * (g * g)
    m_hat = m2 / (1.0 - ADAM_B1 ** ADAM_STEP)
    v_hat = v2 / (1.0 - ADAM_B2 ** ADAM_STEP)
    delta = -ADAM_LR * (m_hat / (jnp.sqrt(v_hat) + ADAM_EPS) + ADAM_WD * w)
    return delta, m2, v2


def _adam_big(r, w, m, v, rb, name, own=None, after=None):
    R, C = w.shape
    ns = r.shape[0]

    def body(*refs):
        r_ref = refs[0]
        own_ref = refs[1] if own is not None else None
        w_ref, m_ref, v_ref, g_ref, d_ref, m2_ref, v2_ref = refs[len(refs) - 7:]
        g = r_ref[0].astype(F32) if own is None else own_ref[...].astype(F32) + r_ref[0].astype(F32)
        for k in range(1, ns):
            g = g + r_ref[k].astype(F32)
        g_ref[...] = g
        d_ref[...], m2_ref[...], v2_ref[...] = _adamw(w_ref[...], g, m_ref[...], v_ref[...])

    t2 = pl.BlockSpec((rb, C), lambda i: (i, 0))
    sd = jax.ShapeDtypeStruct((R, C), F32)
    extra_specs = ([t2] if own is not None else []) + ([HBM] if after is not None else [])
    extra = ([own] if own is not None else []) + ([after] if after is not None else [])
    return pl.pallas_call(
        body, name=name, grid=(R // rb,),
        in_specs=[pl.BlockSpec((ns, rb, C), lambda i: (0, i, 0))] + extra_specs + [t2, t2, t2],
        out_specs=[t2, t2, t2, t2], out_shape=[sd, sd, sd, sd],
        compiler_params=pltpu.CompilerParams(dimension_semantics=("arbitrary",), vmem_limit_bytes=VMEM_LIMIT),
    )(r, *extra, w, m, v)


def _adam_ada(cact_t, dcs, w, m, v, rb, name):
    R, C = w.shape

    def body(ct_ref, dc_ref, w_ref, m_ref, v_ref, g_ref, d_ref, m2_ref, v2_ref):
        g = jnp.dot(ct_ref[...], dc_ref[...], preferred_element_type=F32, precision=lax.Precision.HIGHEST)
        g_ref[...] = g
        d_ref[...], m2_ref[...], v2_ref[...] = _adamw(w_ref[...], g, m_ref[...], v_ref[...])

    t2 = pl.BlockSpec((rb, C), lambda i: (i, 0))
    sd = jax.ShapeDtypeStruct((R, C), F32)
    return pl.pallas_call(
        body, name=name, grid=(R // rb,),
        in_specs=[pl.BlockSpec((rb, NDEV), lambda i: (i, 0)), _full((NDEV, C)), t2, t2, t2],
        out_specs=[t2, t2, t2, t2], out_shape=[sd, sd, sd, sd],
        compiler_params=pltpu.CompilerParams(dimension_semantics=("arbitrary",), vmem_limit_bytes=VMEM_LIMIT),
    )(cact_t, dcs, w, m, v)


_SMALL = ["ada_b", "ada_f_b", "norm1_g", "b_in", "a_ln_g", "a_ln_b", "a_spatial_b", "b_conv_b", "b_gn_g", "b_gn_b",
          "out_norm_a_g", "out_norm_b_g", "norm2_g", "norm_f_g", "a_spatial_w", "b_conv_w"]


def _adam_small(vsum, wssum, gcw, params):
    names = _SMALL
    flat = []
    for n in names:
        flat += list(params[n])

    def body(vs_ref, ws_ref, gcw_ref, *rest):
        ins = rest[:3 * len(names)]
        outs = rest[3 * len(names):]
        for pi, n in enumerate(names):
            w_ref, m_ref, v_ref = ins[3 * pi:3 * pi + 3]
            g_ref, d_ref, m2_ref, v2_ref = outs[4 * pi:4 * pi + 4]
            if n in ("ada_b", "ada_f_b", "b_in"):
                row0 = {"ada_b": 0, "ada_f_b": 6, "b_in": 9}[n]
                pieces = [(vs_ref[row0 + r:row0 + r + 1, :], slice(r * D, (r + 1) * D))
                          for r in range(w_ref.shape[1] // D)]
            elif n == "a_spatial_w":
                pieces = [(ws_ref[...], slice(None))]
            elif n == "b_conv_w":
                pieces = [(gcw_ref[...], slice(None))]
            else:
                row, off, width = _VEC_AT[n]
                pieces = [(vs_ref[row:row + 1, off:off + width], slice(None))]
            for g, cs in pieces:
                g_ref[:, cs] = g
                d_ref[:, cs], m2_ref[:, cs], v2_ref[:, cs] = _adamw(w_ref[:, cs], g, m_ref[:, cs], v_ref[:, cs])

    out_shape = []
    for n in names:
        out_shape += [jax.ShapeDtypeStruct(params[n][0].shape, F32)] * 4
    outs = pl.pallas_call(
        body, name="adam_small",
        in_specs=[VM] * (3 + len(flat)), out_specs=[VM] * len(out_shape), out_shape=out_shape,
        compiler_params=pltpu.CompilerParams(vmem_limit_bytes=VMEM_LIMIT),
    )(vsum, wssum, gcw, *flat)
    return {n: outs[4 * pi:4 * pi + 4] for pi, n in enumerate(names)}


def _token_tile(T, want):
    return want if T % want == 0 else T


def kernel(x, c, ada_w, ada_b, norm1_g, w_in, b_in, a_ln_g, a_ln_b, a_spatial_w, a_spatial_b, b_conv_w, b_conv_b, b_gn_g, b_gn_b, out_norm_a_g, out_norm_b_g, w_out, norm2_g, w_ffn_in, w_ffn_out, ada_f_w, ada_f_b, norm_f_g, loss_target, m_ada_w, m_ada_b, m_norm1_g, m_w_in, m_b_in, m_a_ln_g, m_a_ln_b, m_a_spatial_w, m_a_spatial_b, m_b_conv_w, m_b_conv_b, m_b_gn_g, m_b_gn_b, m_out_norm_a_g, m_out_norm_b_g, m_w_out, m_norm2_g, m_w_ffn_in, m_w_ffn_out, m_ada_f_w, m_ada_f_b, m_norm_f_g, v_ada_w, v_ada_b, v_norm1_g, v_w_in, v_b_in, v_a_ln_g, v_a_ln_b, v_a_spatial_w, v_a_spatial_b, v_b_conv_w, v_b_conv_b, v_b_gn_g, v_b_gn_b, v_out_norm_a_g, v_out_norm_b_g, v_w_out, v_norm2_g, v_w_ffn_in, v_w_ffn_out, v_ada_f_w, v_ada_f_b, v_norm_f_g):
    T = x.shape[1]
    idx = 4 * lax.axis_index("x") + 2 * lax.axis_index("y") + lax.axis_index("c")
    x2d = x.reshape(T, D)
    tgt = loss_target.reshape(T, D)

    conv_s = jnp.pad(b_conv_w[0], ((0, HALO - KW), (0, 0)))
    call, cparts, cfparts, convg, (win_g, wout_g) = _gather(
        c, ada_w[0], ada_b.reshape(NDEV, -1), ada_f_w, ada_f_b.reshape(NDEV, -1), conv_s,
        [w_in[0].astype(BF16), w_out[0].astype(BF16)])
    mod = jnp.concatenate([cparts.reshape(6, D), cfparts.reshape(2, D)], axis=0)
    wout = wout_g.reshape(D, D)
    cw = jnp.transpose(convg, (1, 0, 2)).reshape(HALO, DB)

    tril = jnp.tril(jnp.ones((CH, CH), dtype=bool))
    wsm = jnp.where(tril[None], a_spatial_w[0], 0.0).astype(BF16)
    wcat = wsm.reshape(NH * CH, CH)
    wcat_t = jnp.transpose(wsm, (0, 2, 1)).reshape(NH * CH, CH)
    bsf = jnp.repeat(a_spatial_b[0].T, DA // NH, axis=1)
    lane = jnp.arange(DB)
    pm = jnp.where((lane[:, None] >> 6) == (lane[None, :] >> 6), 1.0 / 64.0, 0.0).astype(BF16)
    esel = jnp.where((lane[:, None] >> 6) == jnp.arange(CH)[None, :], 1.0, 0.0).astype(BF16)

    tm = _token_tile(T, 256)
    tk = _token_tile(T, 1024)
    (x1, hb, zvg, mixed, yb, o, gu, dgelu_u, dgelu_v, vhat, rslb, yhat, rsg), (wfi_g, wfo_g) = _mix_fwd(
        x2d, mod, norm1_g, win_g, b_in, a_ln_g, a_ln_b, wcat, bsf, cw, b_conv_b, b_gn_g, b_gn_b, out_norm_a_g,
        out_norm_b_g, wout, pm, [w_ffn_in[0].T.astype(BF16), w_ffn_out[0].astype(BF16)], _token_tile(T, 512))
    dx1, h2b, dgu, acc_f, g_wfo = _ffn(x1, tgt, mod, norm2_g, norm_f_g.reshape(1, D), wfi_g,
                                       wfo_g.reshape(DFF, D), tm)
    g_wfi = _wgrad_b_blocked(h2b, dgu, 2, tk, "wgrad_ffn_in")
    g_wfo = g_wfo.reshape(NDEV, DFF // NDEV, D)
    f_sems, f_thru, f_land, f_token = _rs_start([g_wfi, g_wfo], "rs_ffn_start")
    (gx, acc_v, acc_b, acc_a, acc_bs, acc_ws, acc_cw), (g_win, g_wout) = _mix_bwd(
        dx1, x2d, zvg, mixed, o, hb, yb, gu, dgelu_u, dgelu_v, vhat, rslb, yhat, rsg, mod, norm1_g, win_g, a_ln_g,
        a_ln_b, wcat, wcat_t, cw, b_gn_g, b_gn_b, out_norm_a_g, out_norm_b_g, wout, pm, esel, f_token, tm)
    (g_wfi_d, g_wfo_d), (r_wfi, r_wfo) = _rs_wait(f_sems, f_thru, f_land, [acc_v], "rs_ffn_wait")
    g_wout = g_wout.reshape(NDEV, D // NDEV, D)

    vsum, dcond_all, wssum = _reduce_small(acc_f, acc_v, acc_b, acc_a, acc_bs, acc_cw, acc_ws, g_wfi_d)
    sems, g_thru, land_thru, token = _rs_start([g_win, g_wout], "rs_mix_start", after=(vsum,))

    own = lambda g: lax.dynamic_index_in_dim(g, idx, 0, keepdims=False)
    res = {}
    res["w_ffn_in"] = tuple(a.T for a in _adam_big(r_wfi, w_ffn_in[0].T, m_w_ffn_in[0].T, v_w_ffn_in[0].T, WFI_B // 4,
                                                   "adam_w_ffn_in", own=own(g_wfi_d), after=token))
    res["w_ffn_out"] = _adam_big(r_wfo, w_ffn_out[0], m_w_ffn_out[0], v_w_ffn_out[0], DFF // NDEV // 2,
                                 "adam_w_ffn_out", own=own(g_wfo_d), after=token)
    cact_t = (call * jax.nn.sigmoid(call)).T
    dcond = dcond_all.reshape(NDEV, 8 * D)
    nada = ada_w.shape[2]
    nadf = ada_f_w.shape[1]
    dcs = lax.dynamic_slice(dcond, (0, idx * nada), (NDEV, nada))
    dcfs = lax.dynamic_slice(dcond, (0, 6 * D + idx * nadf), (NDEV, nadf))
    res["ada_w"] = _adam_ada(cact_t, dcs, ada_w[0], m_ada_w[0], v_ada_w[0], 256, "adam_ada_w")
    res["ada_f_w"] = _adam_ada(cact_t, dcfs, ada_f_w, m_ada_f_w, v_ada_f_w, 256, "adam_ada_f_w")
    ncw = b_conv_w.shape[2]
    gcw = jnp.concatenate([lax.dynamic_slice(vsum, (_CW_ROW, idx * ncw), (HALO // 2, ncw)),
                           lax.dynamic_slice(vsum, (_CW_ROW, DB + idx * ncw), (HALO // 2, ncw))], axis=0)[:KW]
    two = lambda a: a.reshape(1, -1) if a.ndim == 1 else a.reshape(-1, a.shape[-1])
    small_in = {
        "ada_b": (ada_b, m_ada_b, v_ada_b), "ada_f_b": (ada_f_b, m_ada_f_b, v_ada_f_b),
        "norm1_g": (norm1_g, m_norm1_g, v_norm1_g), "b_in": (b_in, m_b_in, v_b_in),
        "a_ln_g": (a_ln_g, m_a_ln_g, v_a_ln_g), "a_ln_b": (a_ln_b, m_a_ln_b, v_a_ln_b),
        "a_spatial_b": (a_spatial_b.reshape(1, D), m_a_spatial_b.reshape(1, D), v_a_spatial_b.reshape(1, D)),
        "b_conv_b": (b_conv_b, m_b_conv_b, v_b_conv_b), "b_gn_g": (b_gn_g, m_b_gn_g, v_b_gn_g),
        "b_gn_b": (b_gn_b, m_b_gn_b, v_b_gn_b), "out_norm_a_g": (out_norm_a_g, m_out_norm_a_g, v_out_norm_a_g),
        "out_norm_b_g": (out_norm_b_g, m_out_norm_b_g, v_out_norm_b_g),
        "norm2_g": (norm2_g, m_norm2_g, v_norm2_g), "norm_f_g": (norm_f_g, m_norm_f_g, v_norm_f_g),
        "a_spatial_w": (a_spatial_w, m_a_spatial_w, v_a_spatial_w),
        "b_conv_w": (b_conv_w[0], m_b_conv_w[0], v_b_conv_w[0]),
    }
    small_in = {n: tuple(two(a) for a in t) for n, t in small_in.items()}
    res.update(_adam_small(vsum, wssum, gcw, small_in))
    (g_win_d, g_wout_d), (r_win, r_wout) = _rs_wait(
        sems, g_thru, land_thru,
        [res["w_ffn_in"][0], res["w_ffn_out"][0], res["ada_w"][0], res["ada_f_w"][0], res["norm_f_g"][0]],
        "rs_mix_wait")
    res["w_in"] = _adam_big(r_win, w_in[0], m_w_in[0], v_w_in[0], 256, "adam_w_in", own=own(g_win_d))
    res["w_out"] = _adam_big(r_wout, w_out[0], m_w_out[0], v_w_out[0], D // NDEV, "adam_w_out", own=own(g_wout_d))

    loss = 0.5 / D * jnp.sum(vsum[_LOSS_ROW])
    shapes = {"ada_w": ada_w, "ada_b": ada_b, "norm1_g": norm1_g, "w_in": w_in, "b_in": b_in, "a_ln_g": a_ln_g,
              "a_ln_b": a_ln_b, "a_spatial_w": a_spatial_w, "a_spatial_b": a_spatial_b, "b_conv_w": b_conv_w,
              "b_conv_b": b_conv_b, "b_gn_g": b_gn_g, "b_gn_b": b_gn_b, "out_norm_a_g": out_norm_a_g,
              "out_norm_b_g": out_norm_b_g, "w_out": w_out, "norm2_g": norm2_g, "w_ffn_in": w_ffn_in,
              "w_ffn_out": w_ffn_out, "ada_f_w": ada_f_w, "ada_f_b": ada_f_b, "norm_f_g": norm_f_g}
    order = list(shapes)
    outs = [loss, gx.reshape(x.shape)]
    for which in range(4):
        outs += [res[n][which].reshape(shapes[n].shape) for n in order]
    return tuple(outs)
```

```python
import math

import jax
import jax.numpy as jnp
from jax import lax
from jax.experimental import pallas as pl
from jax.experimental.pallas import tpu as pltpu

F32 = jnp.float32
BF16 = jnp.bfloat16

D = 1024
DA = 512
DB = 512
DIN = 2048
DFF = 2816
NH = 8
CH = 128
KW = 31
HALO = 32
NDEV = 8
WIN_B = DIN // NDEV
WFI_B = 2 * DFF // NDEV
EPS = 1e-6
NVEC = 40
VMEM_LIMIT = 56 * 1024 * 1024

ADAM_LR, ADAM_B1, ADAM_B2, ADAM_EPS, ADAM_WD, ADAM_STEP = 0.001, 0.9, 0.999, 1e-08, 0.01, 10

MESH = pl.DeviceIdType.MESH


def _dot(a, b):
    return jnp.dot(a, b, preferred_element_type=F32)


def _dot_nt(a, b):
    return lax.dot_general(a, b, (((1,), (1,)), ((), ())), preferred_element_type=F32)


def _dot_tn(a, b):
    return lax.dot_general(a, b, (((0,), (0,)), ((), ())), preferred_element_type=F32)


def _rs(v):
    return lax.rsqrt(jnp.mean(v * v, axis=-1, keepdims=True) + EPS)


def _sig(v):
    return 1.0 / (1.0 + jnp.exp(-v))


_INV_SQRT2 = 1.0 / math.sqrt(2.0)
_INV_SQRT2PI = 1.0 / math.sqrt(2.0 * math.pi)


def _gelu_parts(v):
    cdf = 0.5 * (1.0 + lax.erf(v * _INV_SQRT2))
    pdf = jnp.exp(-0.5 * v * v) * _INV_SQRT2PI
    return v * cdf, cdf + v * pdf


def _grp_mean(v, pm):
    hi = v.astype(BF16)
    lo = (v - hi.astype(F32)).astype(BF16)
    return _dot(hi, pm) + _dot(lo, pm)


def _colsum(v):
    return jnp.sum(v, axis=0, keepdims=True)


def _full(shape):
    nd = len(shape)
    return pl.BlockSpec(shape, lambda *_: (0,) * nd)


def _resident(shape):
    nd = len(shape)
    return pl.BlockSpec(shape, lambda *_: (0,) * nd, pipeline_mode=pl.Buffered(1))


HBM = pl.BlockSpec(memory_space=pl.ANY)
VM = pl.BlockSpec(memory_space=pltpu.VMEM)


SH_ROWS = HALO - 8


def _shifted_copies(buf, shbuf, tm):
    for b in range(1, 8):
        shbuf[b - 1] = buf[b:b + tm + SH_ROWS, :]


def _window(buf, shbuf, off, tm):
    a, b = divmod(off, 8)
    if b == 0:
        return buf[8 * a:8 * a + tm, :]
    return shbuf[b - 1, 8 * a:8 * a + tm, :]


def _first_head_lanes():
    return lax.broadcasted_iota(jnp.int32, (CH, CH), 1) < (DA // NH)


def _mix_heads(w_ref, vb, first):
    outs = []
    for p in range(NH // 2):
        v = vb[:, p * CH:(p + 1) * CH]
        a = _dot(w_ref[(2 * p) * CH:(2 * p + 1) * CH, :], v)
        b = _dot(w_ref[(2 * p + 1) * CH:(2 * p + 2) * CH, :], v)
        outs.append(jnp.where(first, a, b))
    return jnp.concatenate(outs, axis=1)


def _place():
    x, y, c = lax.axis_index("x"), lax.axis_index("y"), lax.axis_index("c")
    return x, y, c, 4 * x + 2 * y + c


def _dev(t):
    return (t >> 2, (t >> 1) & 1, t & 1)


class _AllGather:
    def __init__(self, w_in, w_out, wss, wrs, lsem):
        x, y, c, idx = _place()
        me, sibling = (x, y, c), (x, y, 1 - c)
        chips = [(1 - x, y), (x, 1 - y), (1 - x, 1 - y)]
        nw = len(w_in)

        def blk(p):
            return 4 * p[0] + 2 * p[1] + p[2]

        def wcopy(a, k, block, to, src=None):
            dst = w_out[a].at[blk(block)]
            return pltpu.make_async_remote_copy(src_ref=dst if src is None else src, dst_ref=dst,
                                                send_sem=wss.at[a, k], recv_sem=wrs.at[a, k],
                                                device_id=to, device_id_type=MESH)

        self.mine = [pltpu.make_async_copy(w_in[a], w_out[a].at[idx], lsem.at[a]) for a in range(nw)]
        self.first = []
        for a in range(nw):
            self.first.append(wcopy(a, 0, me, sibling, src=w_in[a]))
            self.first += [wcopy(a, 1 + j, me, (*chip, c), src=w_in[a]) for j, chip in enumerate(chips)]
        self.landed = [[wcopy(a, 1 + j, (*chip, c), me) for a in range(nw)] for j, chip in enumerate(chips)]
        self.passed = [[wcopy(a, 4 + j, (*chip, c), sibling) for a in range(nw)] for j, chip in enumerate(chips)]
        self.from_sibling = []
        for a in range(nw):
            self.from_sibling.append(wcopy(a, 0, sibling, me))
            self.from_sibling += [wcopy(a, 4 + j, (*chip, 1 - c), me) for j, chip in enumerate(chips)]

    def start(self):
        for cp in self.mine + self.first:
            cp.start()

    def forward(self):
        for land, pas in zip(self.landed, self.passed):
            for l, p in zip(land, pas):
                l.wait_recv()
                p.start()

    def finish(self):
        for cp in self.from_sibling:
            cp.wait_recv()
        for cp in self.first:
            cp.wait_send()
        for pas in self.passed:
            for p in pas:
                p.wait_send()
        for cp in self.mine:
            cp.wait()


AG_SEMS = lambda nw: [pltpu.SemaphoreType.DMA((nw, 7)), pltpu.SemaphoreType.DMA((nw, 7)),
                      pltpu.SemaphoreType.DMA((nw,))]


class _ReduceScatter:
    def __init__(self, g_in, r_out, gss, grs, lsem):
        x, y, c, idx = _place()
        me = (x, y, c)
        nw = len(g_in)
        self.mine = [pltpu.make_async_copy(g_in[a].at[idx], r_out[a].at[0], lsem.at[a]) for a in range(nw)]
        self.sends, self.recvs = [], []
        for k in range(1, NDEV):
            t = idx ^ k
            for a in range(nw):
                self.sends.append(pltpu.make_async_remote_copy(
                    src_ref=g_in[a].at[t], dst_ref=r_out[a].at[k], send_sem=gss.at[a, k - 1],
                    recv_sem=grs.at[a, k - 1], device_id=_dev(t), device_id_type=MESH))
                self.recvs.append(pltpu.make_async_remote_copy(
                    src_ref=g_in[a].at[0], dst_ref=r_out[a].at[k], send_sem=gss.at[a, k - 1],
                    recv_sem=grs.at[a, k - 1], device_id=me, device_id_type=MESH))

    def start(self):
        for cp in self.mine + self.sends:
            cp.start()

    def finish(self):
        for cp in self.recvs:
            cp.wait_recv()
        for cp in self.sends:
            cp.wait_send()
        for cp in self.mine:
            cp.wait()


RS_SEMS = AG_SEMS


def _mix_fwd(x, mod, g1, win, b_in, lng, lnb, wcat, bsf, cw, cb, gng, gnb, oga, ogb, wout, pm, ffn_shards, tm):
    T = x.shape[0]
    nt = T // tm
    nch = tm // CH
    nw = len(ffn_shards)
    fwd_step = (5 * nt) // 8
    saved = [(D, F32), (D, BF16), (2 * DB, F32), (DA, F32), (D, BF16), (D, F32), (DA, F32), (DA, F32), (DA, F32),
             (DA, F32), (CH, F32), (DB, F32), (DB, F32)]
    NSAVE = len(saved)

    def body(x_ref, mod_ref, g1_ref, win_ref, bin_ref, lng_ref, lnb_ref, wcat_ref, bsf_ref, cw_ref, cb_ref,
             gng_ref, gnb_ref, oga_ref, ogb_ref, wout_ref, pm_ref, *rest):
        sh_in = rest[:nw]
        (x1_ref, h_ref, zvg_ref, mixed_ref, y_ref, o_ref, gu_ref, dgu_ref, dgv_ref, vhat_ref, rsl_ref, yhat_ref,
         rsg_ref) = rest[nw:nw + NSAVE]
        sh_out = rest[nw + NSAVE:2 * nw + NSAVE]
        glbuf, shbuf, wss, wrs, lsem = rest[2 * nw + NSAVE:]
        i = pl.program_id(0)

        @pl.when(i == 0)
        def _():
            _AllGather(sh_in, sh_out, wss, wrs, lsem).start()

        xv = x_ref[...]
        shift1 = mod_ref[0:1, :]
        scale1 = mod_ref[1:2, :]
        gate1 = mod_ref[2:3, :]
        h = (xv * _rs(xv) * g1_ref[...]) * (1.0 + scale1) + shift1
        hb = h.astype(BF16)
        h_ref[...] = hb
        z = jnp.concatenate([_dot(hb, win_ref[j]) for j in range(NDEV)], axis=1) + bin_ref[...]
        zvg_ref[...] = z[:, 2 * DA:]
        gu, dgelu_u = _gelu_parts(z[:, 0:DA])
        gv, dgelu_v = _gelu_parts(z[:, DA:2 * DA])
        gu_ref[...] = gu
        dgu_ref[...] = dgelu_u
        dgv_ref[...] = dgelu_v
        xc = gv - jnp.mean(gv, axis=-1, keepdims=True)
        rsl = lax.rsqrt(jnp.mean(xc * xc, axis=-1, keepdims=True) + EPS)
        vhat = xc * rsl
        vhat_ref[...] = vhat
        rsl_ref[...] = jnp.broadcast_to(rsl, (tm, CH))
        vnb = (vhat * lng_ref[...] + lnb_ref[...]).astype(BF16)
        first = _first_head_lanes()
        chunks = []
        for ci in range(nch):
            chunks.append(_mix_heads(wcat_ref, vnb[ci * CH:(ci + 1) * CH, :], first) + bsf_ref[...])
        mixed = jnp.concatenate(chunks, axis=0) if nch > 1 else chunks[0]
        mixed_ref[...] = mixed
        ya = gu * mixed
        gl = z[:, 2 * DA:2 * DA + DB] * _sig(z[:, 2 * DA + DB:])

        @pl.when(i == 0)
        def _():
            glbuf[0:HALO, :] = jnp.zeros((HALO, DB), F32)

        glbuf[HALO:HALO + tm, :] = gl
        _shifted_copies(glbuf, shbuf, tm)
        yc = jnp.zeros((tm, DB), F32) + cb_ref[...]
        for k in range(KW):
            yc = yc + cw_ref[k:k + 1, :] * _window(glbuf, shbuf, HALO - (KW - 1) + k, tm)
        glbuf[0:HALO, :] = gl[tm - HALO:, :]
        pmv = pm_ref[...]
        dc = yc - _grp_mean(yc, pmv)
        rsg = lax.rsqrt(_grp_mean(dc * dc, pmv) + EPS)
        yhat = dc * rsg
        yhat_ref[...] = yhat
        rsg_ref[...] = rsg
        yg = yhat * gng_ref[...] + gnb_ref[...]
        yb = yg * _sig(yg)
        na = ya * _rs(ya) * oga_ref[...]
        nb = yb * _rs(yb) * ogb_ref[...]
        yv = jnp.concatenate([na, nb], axis=1).astype(BF16)
        y_ref[...] = yv
        o = _dot(yv, wout_ref[...])
        o_ref[...] = o
        x1_ref[...] = xv + gate1 * o

        @pl.when(i == fwd_step)
        def _():
            _AllGather(sh_in, sh_out, wss, wrs, lsem).forward()

        @pl.when(i == nt - 1)
        def _():
            _AllGather(sh_in, sh_out, wss, wrs, lsem).finish()

    tile = lambda w: pl.BlockSpec((tm, w), lambda i: (i, 0))
    outs = pl.pallas_call(
        body,
        name="mix_fwd",
        grid=(nt,),
        in_specs=[tile(D), _full((8, D)), _full((1, D)), _resident((NDEV, D, WIN_B)), _full((1, DIN)),
                  _full((1, DA)), _full((1, DA)), _full((NH * CH, CH)), _full((CH, DA)), _full((HALO, DB)),
                  _full((1, DB)), _full((1, DB)), _full((1, DB)), _full((1, DA)), _full((1, DB)),
                  _resident((D, D)), _full((DB, DB))] + [HBM] * nw,
        out_specs=[tile(w) for w, _ in saved] + [HBM] * nw,
        out_shape=[jax.ShapeDtypeStruct((T, w), dt) for w, dt in saved]
                  + [jax.ShapeDtypeStruct((NDEV,) + s.shape, s.dtype) for s in ffn_shards],
        scratch_shapes=[pltpu.VMEM((HALO + tm, DB), F32), pltpu.VMEM((7, tm + SH_ROWS, DB), F32)] + AG_SEMS(nw),
        compiler_params=pltpu.CompilerParams(dimension_semantics=("arbitrary",), vmem_limit_bytes=VMEM_LIMIT),
    )(x, mod, g1, win, b_in, lng, lnb, wcat, bsf, cw, cb, gng, gnb, oga, ogb, wout, pm, *ffn_shards)
    return outs[:NSAVE], outs[NSAVE:]


FF_BLOCKS = ((0, 1024), (1024, 1024), (2048, 768))


def _ffn(x1, tgt, mod, g2, gf, wfi_t, wfo, tm):
    T = x1.shape[0]
    nt = T // tm

    def body(x1_ref, tgt_ref, mod_ref, g2_ref, gf_ref, wfi_ref, wfo_ref,
             dx1_ref, h2_ref, dgu_ref, act_ref, dxg_ref, acc_ref, g_s, u_s):
        i = pl.program_id(0)

        @pl.when(i == 0)
        def _():
            acc_ref[...] = jnp.zeros((8, D), F32)

        x1 = x1_ref[...]
        shift2 = mod_ref[3:4, :]
        scale2 = mod_ref[4:5, :]
        gate2 = mod_ref[5:6, :]
        shiftf = mod_ref[6:7, :]
        scalef = mod_ref[7:8, :]
        g2v = g2_ref[...]
        gfv = gf_ref[...]
        r2 = _rs(x1)
        xn2 = x1 * r2
        h2b = (xn2 * g2v * (1.0 + scale2) + shift2).astype(BF16)
        h2_ref[...] = h2b
        f = jnp.zeros((tm, D), F32)
        for o, w in FF_BLOCKS:
            g = _dot_nt(h2b, wfi_ref[o:o + w, :])
            u = _dot_nt(h2b, wfi_ref[DFF + o:DFF + o + w, :])
            g_s[:, o:o + w] = g
            u_s[:, o:o + w] = u
            actb = (g * _sig(g) * u).astype(BF16)
            act_ref[:, o:o + w] = actb
            f = f + _dot(actb, wfo_ref[o:o + w, :])
        x2 = x1 + gate2 * f
        rf = _rs(x2)
        xnf = x2 * rf
        out = xnf * gfv * (1.0 + scalef) + shiftf
        e = out - tgt_ref[...]
        dout = e * (1.0 / D)
        acc_ref[7:8, :] += _colsum(e * e)
        acc_ref[0:1, :] += _colsum(dout)
        acc_ref[1:2, :] += _colsum(dout * xnf * gfv)
        acc_ref[2:3, :] += _colsum(dout * (1.0 + scalef) * xnf)
        dxnf = dout * (1.0 + scalef) * gfv
        dx2 = rf * (dxnf - xnf * jnp.mean(dxnf * xnf, axis=-1, keepdims=True))
        acc_ref[3:4, :] += _colsum(dx2 * f)
        dxgb = (dx2 * gate2).astype(BF16)
        dxg_ref[...] = dxgb
        dh2 = jnp.zeros((tm, D), F32)
        for o, w in FF_BLOCKS:
            dact = _dot_nt(dxgb, wfo_ref[o:o + w, :])
            g = g_s[:, o:o + w]
            u = u_s[:, o:o + w]
            s = _sig(g)
            dgb = (dact * u * (s * (1.0 + g * (1.0 - s)))).astype(BF16)
            dub = (dact * (g * s)).astype(BF16)
            dgu_ref[:, o:o + w] = dgb
            dgu_ref[:, DFF + o:DFF + o + w] = dub
            dh2 = dh2 + _dot(dgb, wfi_ref[o:o + w, :])
            dh2 = dh2 + _dot(dub, wfi_ref[DFF + o:DFF + o + w, :])
        acc_ref[4:5, :] += _colsum(dh2)
        acc_ref[5:6, :] += _colsum(dh2 * xn2 * g2v)
        acc_ref[6:7, :] += _colsum(dh2 * (1.0 + scale2) * xn2)
        dxn2 = dh2 * (1.0 + scale2) * g2v
        dx1_ref[...] = dx2 + r2 * (dxn2 - xn2 * jnp.mean(dxn2 * xn2, axis=-1, keepdims=True))

    tile = lambda w: pl.BlockSpec((tm, w), lambda i: (i, 0))
    return pl.pallas_call(
        body,
        name="ffn_fwd_bwd",
        grid=(nt,),
        in_specs=[tile(D), tile(D), _full((8, D)), _full((1, D)), _full((1, D)),
                  _resident((2 * DFF, D)), _resident((DFF, D))],
        out_specs=[tile(D), tile(D), tile(2 * DFF), tile(DFF), tile(D), _full((8, D))],
        out_shape=[jax.ShapeDtypeStruct((T, D), F32), jax.ShapeDtypeStruct((T, D), BF16),
                   jax.ShapeDtypeStruct((T, 2 * DFF), BF16), jax.ShapeDtypeStruct((T, DFF), BF16),
                   jax.ShapeDtypeStruct((T, D), BF16), jax.ShapeDtypeStruct((8, D), F32)],
        scratch_shapes=[pltpu.VMEM((tm, DFF), F32), pltpu.VMEM((tm, DFF), F32)],
        compiler_params=pltpu.CompilerParams(dimension_semantics=("arbitrary",), vmem_limit_bytes=VMEM_LIMIT),
    )(x1, tgt, mod, g2, gf, wfi_t, wfo)


def _mix_bwd(dx1, x, zvg, mixed, o, hb, yb, gu, dgu, dgv, vhat, rslb, yhat, rsg, mod, g1, win, lng, lnb, wcat, wcat_t,
             cw, gng, gnb, oga, ogb, wout, pm, esel, after, tm):
    T = x.shape[0]
    nt = T // tm
    nch = tm // CH
    WOB = 256

    def body(dx1_ref, x_ref, zvg_ref, mixed_ref, o_ref, hb_ref, yb_ref, gu_ref, dgu_ref, dgv_ref, vhat_ref, rsl_ref,
             yhat_ref, rsg_ref, mod_ref, g1_ref, win_ref, lng_ref, lnb_ref, wcat_ref, wcatt_ref, cw_ref, gng_ref,
             gnb_ref, oga_ref, ogb_ref, wout_ref, pm_ref, esel_ref, after_ref,
             gx_ref, accv_ref, accb_ref, acca_ref, accbs_ref, accws_ref, acccw_ref, gwin_ref, gwout_ref,
             dycbuf, shbuf, bs_s, acc_win, acc_wout, st_win, st_wout):
        i = pl.program_id(0)

        @pl.when(i == 0)
        def _():
            acc_win[...] = jnp.zeros((NDEV, D, WIN_B), F32)
            acc_wout[...] = jnp.zeros((D, D), F32)
            accv_ref[...] = jnp.zeros((8, D), F32)
            accb_ref[...] = jnp.zeros((1, DIN), F32)
            acca_ref[...] = jnp.zeros((8, DA), F32)
            accws_ref[...] = jnp.zeros((NH * CH, CH), F32)
            acccw_ref[...] = jnp.zeros((HALO, DB), F32)
            bs_s[...] = jnp.zeros((CH, DA), F32)
            dycbuf[tm:tm + HALO, :] = jnp.zeros((HALO, DB), F32)

        shift1 = mod_ref[0:1, :]
        scale1 = mod_ref[1:2, :]
        gate1 = mod_ref[2:3, :]
        g1v = g1_ref[...]
        xv = x_ref[...]
        r1 = _rs(xv)
        xn1 = xv * r1
        val = zvg_ref[:, 0:DB]
        gate = zvg_ref[:, DB:]
        gu = gu_ref[...]
        dgelu_u = dgu_ref[...]
        dgelu_v = dgv_ref[...]
        vhat = vhat_ref[...]
        rsl = rsl_ref[:, 0:1]
        lngv = lng_ref[...]
        vnb = (vhat * lngv + lnb_ref[...]).astype(BF16)
        mixed = mixed_ref[...]
        ya = gu * mixed
        ra = _rs(ya)
        yan = ya * ra
        sgt = _sig(gate)
        gl = val * sgt
        pmv = pm_ref[...]
        rsg = rsg_ref[...]
        yhat = yhat_ref[...]
        gngv = gng_ref[...]
        yg = yhat * gngv + gnb_ref[...]
        sgy = _sig(yg)
        yb = yg * sgy
        rb = _rs(yb)
        ybn = yb * rb
        dx1 = dx1_ref[...]
        accv_ref[0:1, :] += _colsum(dx1 * o_ref[...])
        dogb = (dx1 * gate1).astype(BF16)
        acc_wout[...] += _dot_tn(yb_ref[...], dogb)
        dy = _dot_nt(dogb, wout_ref[...])
        dna = dy[:, 0:DA]
        dnb = dy[:, DA:]
        ogav = oga_ref[...]
        ogbv = ogb_ref[...]
        acca_ref[2:3, :] += _colsum(dna * yan)
        acca_ref[3:4, :] += _colsum(dnb * ybn)
        ta = dna * ogav
        dya = ra * (ta - yan * jnp.mean(ta * yan, axis=-1, keepdims=True))
        tb = dnb * ogbv
        dyb = rb * (tb - ybn * jnp.mean(tb * ybn, axis=-1, keepdims=True))
        dgu = dya * mixed
        dm = dya * gu
        first = _first_head_lanes()
        zero = jnp.zeros((CH, CH), BF16)
        dvn_chunks = []
        bs_acc = bs_s[...]
        for ci in range(nch):
            dmc = dm[ci * CH:(ci + 1) * CH, :]
            bs_acc = bs_acc + dmc
            dmcb = dmc.astype(BF16)
            dvn_chunks.append(_mix_heads(wcatt_ref, dmcb, first))
            vc = vnb[ci * CH:(ci + 1) * CH, :]
            for p in range(NH // 2):
                xt = dmcb[:, p * CH:(p + 1) * CH]
                vt = vc[:, p * CH:(p + 1) * CH]
                accws_ref[(2 * p) * CH:(2 * p + 1) * CH, :] += _dot_nt(jnp.where(first, xt, zero), vt)
                accws_ref[(2 * p + 1) * CH:(2 * p + 2) * CH, :] += _dot_nt(jnp.where(first, zero, xt), vt)
        bs_s[...] = bs_acc
        dvn = jnp.concatenate(dvn_chunks, axis=0) if nch > 1 else dvn_chunks[0]
        acca_ref[0:1, :] += _colsum(dvn * vhat)
        acca_ref[1:2, :] += _colsum(dvn)
        dvh = dvn * lngv
        dgv = rsl * (dvh - jnp.mean(dvh, axis=-1, keepdims=True)
                     - vhat * jnp.mean(dvh * vhat, axis=-1, keepdims=True))
        du = dgu * dgelu_u
        dv = dgv * dgelu_v
        dyg = dyb * (sgy * (1.0 + yg * (1.0 - sgy)))
        acca_ref[5:6, :] += _colsum(dyg * yhat)
        acca_ref[6:7, :] += _colsum(dyg)
        dyh = dyg * gngv
        dyc = rsg * (dyh - _grp_mean(dyh, pmv) - yhat * _grp_mean(dyh * yhat, pmv))
        acca_ref[4:5, :] += _colsum(dyc)
        dycbuf[0:tm, :] = dyc
        _shifted_copies(dycbuf, shbuf, tm)
        dgl = jnp.zeros((tm, DB), F32)
        for k in range(KW):
            win_k = _window(dycbuf, shbuf, KW - 1 - k, tm)
            dgl = dgl + cw_ref[k:k + 1, :] * win_k
            acccw_ref[k:k + 1, :] += _colsum(win_k * gl)
        dycbuf[tm:tm + HALO, :] = dyc[0:HALO, :]
        dval = dgl * sgt
        dgate = dgl * val * sgt * (1.0 - sgt)
        dz = jnp.concatenate([du, dv, dval, dgate], axis=1)
        accb_ref[...] += _colsum(dz)
        dzb = dz.astype(BF16)
        hbv = hb_ref[...]
        dh = jnp.zeros((tm, D), F32)
        for j in range(NDEV):
            dzj = dzb[:, j * WIN_B:(j + 1) * WIN_B]
            acc_win[j] += _dot_tn(hbv, dzj)
            dh = dh + _dot_nt(dzj, win_ref[j])
        accv_ref[1:2, :] += _colsum(dh)
        accv_ref[2:3, :] += _colsum(dh * xn1 * g1v)
        accv_ref[3:4, :] += _colsum(dh * (1.0 + scale1) * xn1)
        dxn1 = dh * (1.0 + scale1) * g1v
        gx_ref[...] = dx1 + r1 * (dxn1 - xn1 * jnp.mean(dxn1 * xn1, axis=-1, keepdims=True))

        @pl.when(i == nt - 1)
        def _():
            rows = lax.broadcasted_iota(jnp.int32, (NH * CH, CH), 0) & (CH - 1)
            cols = lax.broadcasted_iota(jnp.int32, (NH * CH, CH), 1)
            accws_ref[...] = jnp.where(cols <= rows, accws_ref[...], 0.0)
            bs = bs_s[...]
            hi = bs.astype(BF16)
            r1_ = bs - hi.astype(F32)
            mid = r1_.astype(BF16)
            lo = (r1_ - mid.astype(F32)).astype(BF16)
            ev = esel_ref[...]
            accbs_ref[...] = _dot(hi, ev) + _dot(mid, ev) + _dot(lo, ev)
            for j in range(NDEV):
                st_win[...] = acc_win[j].astype(BF16)
                pltpu.sync_copy(st_win, gwin_ref.at[j])
            for j in range(D // WOB):
                st_wout[...] = acc_wout[j * WOB:(j + 1) * WOB, :].astype(BF16)
                pltpu.sync_copy(st_wout, gwout_ref.at[pl.ds(j * WOB, WOB)])

    rev = lambda w: pl.BlockSpec((tm, w), lambda i: (nt - 1 - i, 0))
    outs = pl.pallas_call(
        body,
        name="mix_bwd",
        grid=(nt,),
        in_specs=[rev(D), rev(D), rev(2 * DB), rev(DA), rev(D), rev(D), rev(D), rev(DA), rev(DA), rev(DA), rev(DA),
                  rev(CH), rev(DB), rev(DB), _full((8, D)), _full((1, D)),
                  _resident((NDEV, D, WIN_B)), _full((1, DA)), _full((1, DA)), _full((NH * CH, CH)),
                  _full((NH * CH, CH)), _full((HALO, DB)), _full((1, DB)), _full((1, DB)), _full((1, DA)),
                  _full((1, DB)), _resident((D, D)), _full((DB, DB)), _full((DA, CH)), HBM],
        out_specs=[rev(D), _full((8, D)), _full((1, DIN)), _full((8, DA)), _full((CH, CH)),
                   _full((NH * CH, CH)), _full((HALO, DB)), HBM, HBM],
        out_shape=[jax.ShapeDtypeStruct((T, D), F32), jax.ShapeDtypeStruct((8, D), F32),
                   jax.ShapeDtypeStruct((1, DIN), F32), jax.ShapeDtypeStruct((8, DA), F32),
                   jax.ShapeDtypeStruct((CH, CH), F32), jax.ShapeDtypeStruct((NH * CH, CH), F32),
                   jax.ShapeDtypeStruct((HALO, DB), F32),
                   jax.ShapeDtypeStruct((NDEV, D, WIN_B), BF16), jax.ShapeDtypeStruct((D, D), BF16)],
        scratch_shapes=[pltpu.VMEM((tm + HALO, DB), F32), pltpu.VMEM((7, tm + SH_ROWS, DB), F32),
                        pltpu.VMEM((CH, DA), F32), pltpu.VMEM((NDEV, D, WIN_B), F32), pltpu.VMEM((D, D), F32),
                        pltpu.VMEM((D, WIN_B), BF16), pltpu.VMEM((WOB, D), BF16)],
        compiler_params=pltpu.CompilerParams(dimension_semantics=("arbitrary",), vmem_limit_bytes=VMEM_LIMIT),
    )(dx1, x, zvg, mixed, o, hb, yb, gu, dgu, dgv, vhat, rslb, yhat, rsg, mod, g1, win, lng, lnb, wcat, wcat_t, cw,
      gng, gnb, oga, ogb, wout, pm, esel, after)
    return outs[:7], outs[7:]


def _wgrad_rows(a, b, bm, tk, name):
    T, M = a.shape
    N = b.shape[1]
    nk = T // tk

    def body(a_ref, b_ref, o_ref, acc):
        k = pl.program_id(1)

        @pl.when(k == 0)
        def _():
            acc[...] = jnp.zeros((bm, N), F32)

        acc[...] += _dot_tn(a_ref[...], b_ref[...])

        @pl.when(k == nk - 1)
        def _():
            o_ref[...] = acc[...].astype(BF16)

    return pl.pallas_call(
        body, name=name, grid=(M // bm, nk),
        in_specs=[pl.BlockSpec((tk, bm), lambda j, k: (k, j)), pl.BlockSpec((tk, N), lambda j, k: (k, 0))],
        out_specs=pl.BlockSpec((bm, N), lambda j, k: (j, 0)),
        out_shape=jax.ShapeDtypeStruct((M, N), BF16),
        scratch_shapes=[pltpu.VMEM((bm, N), F32)],
        compiler_params=pltpu.CompilerParams(dimension_semantics=("arbitrary", "arbitrary"),
                                             vmem_limit_bytes=VMEM_LIMIT),
    )(a, b)


def _small_copy(src, dst, ss, rs, k, to):
    return pltpu.make_async_remote_copy(src_ref=src, dst_ref=dst, send_sem=ss.at[k], recv_sem=rs.at[k],
                                        device_id=to, device_id_type=MESH)


def _gather(c_row, ada_w, ada_b8, ada_f_w, ada_f_b8, conv_s, shards):
    nw = len(shards)

    def body(c_ref, adaw_ref, adab_ref, adafw_ref, adafb_ref, conv_ref, *rest):
        w_in = rest[:nw]
        call_ref, cparts_ref, cfparts_ref, convg_ref = rest[nw:nw + 4]
        w_out = rest[nw + 4:2 * nw + 4]
        part_s, partf_s, wss, wrs, lsem, s1, r1, s2, r2, s3, r3, s4, r4 = rest[2 * nw + 4:]
        x, y, c, idx = _place()
        me = (x, y, c)
        ag = _AllGather(w_in, w_out, wss, wrs, lsem)
        ag.start()
        call_ref[pl.ds(idx, 1), :] = c_ref[...]
        convg_ref[idx] = conv_ref[...]
        ph1 = []
        for k in range(1, NDEV):
            to = _dev(idx ^ k)
            ph1.append(_small_copy(c_ref, call_ref.at[pl.ds(idx, 1)], s1, r1, k - 1, to))
            ph1.append(_small_copy(conv_ref, convg_ref.at[idx], s2, r2, k - 1, to))
        for cp in ph1:
            cp.start()
        for k in range(1, NDEV):
            src_dev = idx ^ k
            _small_copy(c_ref, call_ref.at[pl.ds(src_dev, 1)], s1, r1, k - 1, me).wait_recv()
            _small_copy(conv_ref, convg_ref.at[src_dev], s2, r2, k - 1, me).wait_recv()
        call = call_ref[...]
        cact = (call * _sig(call))
        part_s[...] = jnp.dot(cact, adaw_ref[...], preferred_element_type=F32,
                              precision=lax.Precision.HIGHEST) + adab_ref[pl.ds(idx, 1), :]
        partf_s[...] = jnp.dot(cact, adafw_ref[...], preferred_element_type=F32,
                               precision=lax.Precision.HIGHEST) + adafb_ref[pl.ds(idx, 1), :]
        cparts_ref[pl.ds(idx, 1), :] = part_s[pl.ds(idx, 1), :]
        cfparts_ref[pl.ds(idx, 1), :] = partf_s[pl.ds(idx, 1), :]
        ph2 = []
        for k in range(1, NDEV):
            t = idx ^ k
            ph2.append(_small_copy(part_s.at[pl.ds(t, 1)], cparts_ref.at[pl.ds(idx, 1)], s3, r3, k - 1, _dev(t)))
            ph2.append(_small_copy(partf_s.at[pl.ds(t, 1)], cfparts_ref.at[pl.ds(idx, 1)], s4, r4, k - 1, _dev(t)))
        for cp in ph2:
            cp.start()
        for k in range(1, NDEV):
            src_dev = idx ^ k
            _small_copy(part_s.at[pl.ds(0, 1)], cparts_ref.at[pl.ds(src_dev, 1)], s3, r3, k - 1, me).wait_recv()
            _small_copy(partf_s.at[pl.ds(0, 1)], cfparts_ref.at[pl.ds(src_dev, 1)], s4, r4, k - 1, me).wait_recv()
        for cp in ph1 + ph2:
            cp.wait_send()
        ag.forward()
        ag.finish()

    dma7 = pltpu.SemaphoreType.DMA((NDEV - 1,))
    outs = pl.pallas_call(
        body,
        name="gather_weights",
        in_specs=[VM] * 6 + [HBM] * nw,
        out_specs=[VM] * 4 + [HBM] * nw,
        out_shape=[jax.ShapeDtypeStruct((NDEV, D), F32), jax.ShapeDtypeStruct((NDEV, ada_w.shape[1]), F32),
                   jax.ShapeDtypeStruct((NDEV, ada_f_w.shape[1]), F32),
                   jax.ShapeDtypeStruct((NDEV,) + conv_s.shape, F32)]
                  + [jax.ShapeDtypeStruct((NDEV,) + s.shape, s.dtype) for s in shards],
        scratch_shapes=[pltpu.VMEM((NDEV, ada_w.shape[1]), F32), pltpu.VMEM((NDEV, ada_f_w.shape[1]), F32)]
                       + AG_SEMS(nw) + [dma7] * 8,
        compiler_params=pltpu.CompilerParams(vmem_limit_bytes=VMEM_LIMIT),
    )(c_row, ada_w, ada_b8, ada_f_w, ada_f_b8, conv_s, *shards)
    return outs[0], outs[1], outs[2], outs[3], outs[4:]


_VEC_AT = {
    "norm1_g": (8, 0, D), "a_ln_g": (11, 0, DA), "a_ln_b": (11, DA, DA), "a_spatial_b": (12, 0, D),
    "b_conv_b": (13, 0, DB), "b_gn_g": (13, DB, DB), "b_gn_b": (14, 0, DB), "out_norm_a_g": (14, DB, DA),
    "out_norm_b_g": (15, 0, DB), "norm2_g": (16, 0, D), "norm_f_g": (17, 0, D),
}
_LOSS_ROW = 18
_CW_ROW = 24


def _reduce_small(acc_f, acc_v, acc_b, acc_a, acc_bs, acc_cw, dws, after):
    def body(accf_ref, accv_ref, accb_ref, acca_ref, accbs_ref, acccw_ref, dws_ref, after_ref,
             vsum_ref, dcond_ref, wssum_ref, vloc, vbuf, wbuf, wown, s1, r1, s2, r2, s3, r3):
        x, y, c, idx = _place()
        me = (x, y, c)
        vloc[...] = jnp.zeros((NVEC, D), F32)
        vloc[0:1, :] = accv_ref[1:2, :]
        vloc[1:2, :] = accv_ref[2:3, :]
        vloc[2:3, :] = accv_ref[0:1, :]
        vloc[3:4, :] = accf_ref[4:5, :]
        vloc[4:5, :] = accf_ref[5:6, :]
        vloc[5:6, :] = accf_ref[3:4, :]
        vloc[6:7, :] = accf_ref[0:1, :]
        vloc[7:8, :] = accf_ref[1:2, :]
        vloc[8:9, :] = accv_ref[3:4, :]
        vloc[9:10, :] = accb_ref[:, 0:D]
        vloc[10:11, :] = accb_ref[:, D:]
        vloc[11:12, 0:DA] = acca_ref[0:1, :]
        vloc[11:12, DA:] = acca_ref[1:2, :]
        bst = accbs_ref[...].T
        for h in range(NH):
            vloc[12:13, h * CH:(h + 1) * CH] = bst[h:h + 1, :]
        vloc[13:14, 0:DB] = acca_ref[4:5, :]
        vloc[13:14, DB:] = acca_ref[5:6, :]
        vloc[14:15, 0:DB] = acca_ref[6:7, :]
        vloc[14:15, DB:] = acca_ref[2:3, :]
        vloc[15:16, 0:DB] = acca_ref[3:4, :]
        vloc[16:17, :] = accf_ref[6:7, :]
        vloc[17:18, :] = accf_ref[2:3, :]
        vloc[_LOSS_ROW:_LOSS_ROW + 1, :] = accf_ref[7:8, :]
        vloc[_CW_ROW:_CW_ROW + HALO // 2, 0:DB] = acccw_ref[0:HALO // 2, :]
        vloc[_CW_ROW:_CW_ROW + HALO // 2, DB:] = acccw_ref[HALO // 2:, :]
        vbuf[idx] = vloc[...]
        rows_of = lambda t: pl.ds(pl.multiple_of(t * CH, CH), CH)
        wbuf[0] = dws_ref[rows_of(idx), :]
        sm = []
        for k in range(1, NDEV):
            t = idx ^ k
            sm.append(_small_copy(vloc, vbuf.at[idx], s1, r1, k - 1, _dev(t)))
            sm.append(_small_copy(dws_ref.at[rows_of(t)], wbuf.at[k], s2, r2, k - 1, _dev(t)))
        for cp in sm:
            cp.start()
        for k in range(1, NDEV):
            _small_copy(dws_ref.at[rows_of(0)], wbuf.at[k], s2, r2, k - 1, me).wait_recv()
        ws = wbuf[0]
        for k in range(1, NDEV):
            ws = ws + wbuf[k]
        wown[...] = ws
        wssum_ref[rows_of(idx), :] = ws
        ag = [_small_copy(wown, wssum_ref.at[rows_of(idx)], s3, r3, k - 1, _dev(idx ^ k)) for k in range(1, NDEV)]
        for cp in ag:
            cp.start()
        for k in range(1, NDEV):
            _small_copy(vloc, vbuf.at[idx ^ k], s1, r1, k - 1, me).wait_recv()
        vs = vbuf[0]
        for d in range(1, NDEV):
            vs = vs + vbuf[d]
        vsum_ref[...] = vs
        for d in range(NDEV):
            dcond_ref[d] = vbuf[d, 0:8, :]
        for k in range(1, NDEV):
            _small_copy(wown, wssum_ref.at[rows_of(idx ^ k)], s3, r3, k - 1, me).wait_recv()
        for cp in sm + ag:
            cp.wait_send()

    dma7 = pltpu.SemaphoreType.DMA((NDEV - 1,))
    return pl.pallas_call(
        body,
        name="reduce_small",
        in_specs=[VM] * 7 + [HBM],
        out_specs=[VM, VM, VM],
        out_shape=[jax.ShapeDtypeStruct((NVEC, D), F32), jax.ShapeDtypeStruct((NDEV, 8, D), F32),
                   jax.ShapeDtypeStruct(dws.shape, F32)],
        scratch_shapes=[pltpu.VMEM((NVEC, D), F32), pltpu.VMEM((NDEV, NVEC, D), F32),
                        pltpu.VMEM((NDEV, CH, CH), F32), pltpu.VMEM((CH, CH), F32)] + [dma7] * 6,
        compiler_params=pltpu.CompilerParams(vmem_limit_bytes=VMEM_LIMIT),
    )(acc_f, acc_v, acc_b, acc_a, acc_bs, acc_cw, dws, after)


HBM_ONLY = pl.BlockSpec(memory_space=pltpu.HBM)
SEM = pl.BlockSpec(memory_space=pltpu.SEMAPHORE)
EFFECT = pltpu.SideEffectType.DATAFLOW_SIDE_EFFECTING


def _rs_copies(g_refs, land_refs, sems):
    x, y, c, idx = _place()
    cps = []
    for k in range(1, NDEV):
        t = idx ^ k
        for a in range(len(g_refs)):
            n = len(cps)
            cps.append(pltpu.make_async_remote_copy(
                src_ref=g_refs[a].at[t], dst_ref=land_refs[a].at[k - 1], send_sem=sems[2 * n],
                recv_sem=sems[2 * n + 1], device_id=_dev(t), device_id_type=MESH))
    return cps


def _rs_start(grads, name, after=()):
    nw = len(grads)
    nsem = 2 * nw * (NDEV - 1)
    lands = [lax.empty((NDEV - 1,) + g.shape[1:], g.dtype) for g in grads]

    def body(*refs):
        g_refs, land_refs = refs[:nw], refs[nw:2 * nw]
        sems = refs[2 * nw + len(after):2 * nw + len(after) + nsem]
        token = refs[-1]
        for cp in _rs_copies(g_refs, land_refs, sems):
            cp.start()
        token[...] = jnp.zeros_like(token)

    outs = pl.pallas_call(
        body, name=name,
        out_shape=(*[pltpu.SemaphoreType.DMA(())] * nsem,
                   *[pltpu.HBM(g.shape, g.dtype) for g in grads], *[pltpu.HBM(l.shape, l.dtype) for l in lands],
                   jax.ShapeDtypeStruct((8, CH), F32)),
        in_specs=[HBM_ONLY] * (2 * nw) + [HBM] * len(after),
        out_specs=(*[SEM] * nsem, *[HBM_ONLY] * (2 * nw), VM),
        input_output_aliases={i: nsem + i for i in range(2 * nw)},
        compiler_params=pltpu.CompilerParams(has_side_effects=EFFECT),
    )(*[pltpu.with_memory_space_constraint(g, pltpu.HBM) for g in grads],
      *[pltpu.with_memory_space_constraint(l, pltpu.HBM) for l in lands], *after)
    return outs[:nsem], outs[nsem:nsem + nw], outs[nsem + nw:nsem + 2 * nw], outs[-1]


def _rs_wait(sems, g_thru, land_thru, after, name):
    nw = len(g_thru)
    nsem = len(sems)

    def body(*refs):
        g_refs, land_refs = refs[:nw], refs[nw:2 * nw]
        for cp in _rs_copies(g_refs, land_refs, refs[2 * nw:2 * nw + nsem]):
            cp.wait_send()
            cp.wait_recv()

    outs = pl.pallas_call(
        body, name=name,
        out_shape=tuple(pltpu.HBM(a.shape, a.dtype) for a in list(g_thru) + list(land_thru)),
        in_specs=[HBM_ONLY] * (2 * nw) + [SEM] * nsem + [HBM] * len(after),
        out_specs=tuple([HBM_ONLY] * (2 * nw)),
        input_output_aliases={i: i for i in range(2 * nw)},
        compiler_params=pltpu.CompilerParams(has_side_effects=EFFECT),
    )(*g_thru, *land_thru, *sems, *after)
    return outs[:nw], outs[nw:]


def _adamw(w, g, m, v):
    m2 = ADAM_B1 * m + (1.0 - ADAM_B1) * g
    v2 = ADAM_B2 * v + (1.0 - ADAM_B2) * (g * g)
    m_hat = m2 / (1.0 - ADAM_B1 ** ADAM_STEP)
    v_hat = v2 / (1.0 - ADAM_B2 ** ADAM_STEP)
    delta = -ADAM_LR * (m_hat / (jnp.sqrt(v_hat) + ADAM_EPS) + ADAM_WD * w)
    return delta, m2, v2


def _adam_big(r, w, m, v, rb, name, own=None, after=None):
    R, C = w.shape
    ns = r.shape[0]

    def body(*refs):
        r_ref = refs[0]
        own_ref = refs[1] if own is not None else None
        w_ref, m_ref, v_ref, g_ref, d_ref, m2_ref, v2_ref = refs[len(refs) - 7:]
        g = r_ref[0].astype(F32) if own is None else own_ref[...].astype(F32) + r_ref[0].astype(F32)
        for k in range(1, ns):
            g = g + r_ref[k].astype(F32)
        g_ref[...] = g
        d_ref[...], m2_ref[...], v2_ref[...] = _adamw(w_ref[...], g, m_ref[...], v_ref[...])

    t2 = pl.BlockSpec((rb, C), lambda i: (i, 0))
    sd = jax.ShapeDtypeStruct((R, C), F32)
    extra_specs = ([t2] if own is not None else []) + ([HBM] if after is not None else [])
    extra = ([own] if own is not None else []) + ([after] if after is not None else [])
    return pl.pallas_call(
        body, name=name, grid=(R // rb,),
        in_specs=[pl.BlockSpec((ns, rb, C), lambda i: (0, i, 0))] + extra_specs + [t2, t2, t2],
        out_specs=[t2, t2, t2, t2], out_shape=[sd, sd, sd, sd],
        compiler_params=pltpu.CompilerParams(dimension_semantics=("arbitrary",), vmem_limit_bytes=VMEM_LIMIT),
    )(r, *extra, w, m, v)


def _adam_ada(cact_t, dcs, w, m, v, rb, name):
    R, C = w.shape

    def body(ct_ref, dc_ref, w_ref, m_ref, v_ref, g_ref, d_ref, m2_ref, v2_ref):
        g = jnp.dot(ct_ref[...], dc_ref[...], preferred_element_type=F32, precision=lax.Precision.HIGHEST)
        g_ref[...] = g
        d_ref[...], m2_ref[...], v2_ref[...] = _adamw(w_ref[...], g, m_ref[...], v_ref[...])

    t2 = pl.BlockSpec((rb, C), lambda i: (i, 0))
    sd = jax.ShapeDtypeStruct((R, C), F32)
    return pl.pallas_call(
        body, name=name, grid=(R // rb,),
        in_specs=[pl.BlockSpec((rb, NDEV), lambda i: (i, 0)), _full((NDEV, C)), t2, t2, t2],
        out_specs=[t2, t2, t2, t2], out_shape=[sd, sd, sd, sd],
        compiler_params=pltpu.CompilerParams(dimension_semantics=("arbitrary",), vmem_limit_bytes=VMEM_LIMIT),
    )(cact_t, dcs, w, m, v)


_SMALL = ["ada_b", "ada_f_b", "norm1_g", "b_in", "a_ln_g", "a_ln_b", "a_spatial_b", "b_conv_b", "b_gn_g", "b_gn_b",
          "out_norm_a_g", "out_norm_b_g", "norm2_g", "norm_f_g", "a_spatial_w", "b_conv_w"]


def _adam_small(vsum, wssum, gcw, params):
    names = _SMALL
    flat = []
    for n in names:
        flat += list(params[n])

    def body(vs_ref, ws_ref, gcw_ref, *rest):
        ins = rest[:3 * len(names)]
        outs = rest[3 * len(names):]
        for pi, n in enumerate(names):
            w_ref, m_ref, v_ref = ins[3 * pi:3 * pi + 3]
            g_ref, d_ref, m2_ref, v2_ref = outs[4 * pi:4 * pi + 4]
            if n in ("ada_b", "ada_f_b", "b_in"):
                row0 = {"ada_b": 0, "ada_f_b": 6, "b_in": 9}[n]
                pieces = [(vs_ref[row0 + r:row0 + r + 1, :], slice(r * D, (r + 1) * D))
                          for r in range(w_ref.shape[1] // D)]
            elif n == "a_spatial_w":
                pieces = [(ws_ref[...], slice(None))]
            elif n == "b_conv_w":
                pieces = [(gcw_ref[...], slice(None))]
            else:
                row, off, width = _VEC_AT[n]
                pieces = [(vs_ref[row:row + 1, off:off + width], slice(None))]
            for g, cs in pieces:
                g_ref[:, cs] = g
                d_ref[:, cs], m2_ref[:, cs], v2_ref[:, cs] = _adamw(w_ref[:, cs], g, m_ref[:, cs], v_ref[:, cs])

    out_shape = []
    for n in names:
        out_shape += [jax.ShapeDtypeStruct(params[n][0].shape, F32)] * 4
    outs = pl.pallas_call(
        body, name="adam_small",
        in_specs=[VM] * (3 + len(flat)), out_specs=[VM] * len(out_shape), out_shape=out_shape,
        compiler_params=pltpu.CompilerParams(vmem_limit_bytes=VMEM_LIMIT),
    )(vsum, wssum, gcw, *flat)
    return {n: outs[4 * pi:4 * pi + 4] for pi, n in enumerate(names)}


def _token_tile(T, want):
    return want if T % want == 0 else T


def kernel(x, c, ada_w, ada_b, norm1_g, w_in, b_in, a_ln_g, a_ln_b, a_spatial_w, a_spatial_b, b_conv_w, b_conv_b, b_gn_g, b_gn_b, out_norm_a_g, out_norm_b_g, w_out, norm2_g, w_ffn_in, w_ffn_out, ada_f_w, ada_f_b, norm_f_g, loss_target, m_ada_w, m_ada_b, m_norm1_g, m_w_in, m_b_in, m_a_ln_g, m_a_ln_b, m_a_spatial_w, m_a_spatial_b, m_b_conv_w, m_b_conv_b, m_b_gn_g, m_b_gn_b, m_out_norm_a_g, m_out_norm_b_g, m_w_out, m_norm2_g, m_w_ffn_in, m_w_ffn_out, m_ada_f_w, m_ada_f_b, m_norm_f_g, v_ada_w, v_ada_b, v_norm1_g, v_w_in, v_b_in, v_a_ln_g, v_a_ln_b, v_a_spatial_w, v_a_spatial_b, v_b_conv_w, v_b_conv_b, v_b_gn_g, v_b_gn_b, v_out_norm_a_g, v_out_norm_b_g, v_w_out, v_norm2_g, v_w_ffn_in, v_w_ffn_out, v_ada_f_w, v_ada_f_b, v_norm_f_g):
    T = x.shape[1]
    idx = 4 * lax.axis_index("x") + 2 * lax.axis_index("y") + lax.axis_index("c")
    x2d = x.reshape(T, D)
    tgt = loss_target.reshape(T, D)

    conv_s = jnp.pad(b_conv_w[0], ((0, HALO - KW), (0, 0)))
    call, cparts, cfparts, convg, (win_g, wout_g) = _gather(
        c, ada_w[0], ada_b.reshape(NDEV, -1), ada_f_w, ada_f_b.reshape(NDEV, -1), conv_s,
        [w_in[0].astype(BF16), w_out[0].astype(BF16)])
    mod = jnp.concatenate([cparts.reshape(6, D), cfparts.reshape(2, D)], axis=0)
    wout = wout_g.reshape(D, D)
    cw = jnp.transpose(convg, (1, 0, 2)).reshape(HALO, DB)

    tril = jnp.tril(jnp.ones((CH, CH), dtype=bool))
    wsm = jnp.where(tril[None], a_spatial_w[0], 0.0).astype(BF16)
    wcat = wsm.reshape(NH * CH, CH)
    wcat_t = jnp.transpose(wsm, (0, 2, 1)).reshape(NH * CH, CH)
    bsf = jnp.repeat(a_spatial_b[0].T, DA // NH, axis=1)
    lane = jnp.arange(DB)
    pm = jnp.where((lane[:, None] >> 6) == (lane[None, :] >> 6), 1.0 / 64.0, 0.0).astype(BF16)
    esel = jnp.where((lane[:, None] >> 6) == jnp.arange(CH)[None, :], 1.0, 0.0).astype(BF16)

    tm = _token_tile(T, 256)
    tk = _token_tile(T, 1024)
    (x1, hb, zvg, mixed, yb, o, gu, dgelu_u, dgelu_v, vhat, rslb, yhat, rsg), (wfi_g, wfo_g) = _mix_fwd(
        x2d, mod, norm1_g, win_g, b_in, a_ln_g, a_ln_b, wcat, bsf, cw, b_conv_b, b_gn_g, b_gn_b, out_norm_a_g,
        out_norm_b_g, wout, pm, [w_ffn_in[0].T.astype(BF16), w_ffn_out[0].astype(BF16)], _token_tile(T, 512))
    dx1, h2b, dgu, act, dxg, acc_f = _ffn(x1, tgt, mod, norm2_g, norm_f_g.reshape(1, D),
                                          wfi_g.reshape(2 * DFF, D), wfo_g.reshape(DFF, D), tm)
    g_wfi = _wgrad_rows(dgu, h2b, 2 * WFI_B, tk, "wgrad_ffn_in").reshape(NDEV, WFI_B, D)
    g_wfo = _wgrad_rows(act, dxg, 2 * WFI_B, tk, "wgrad_ffn_out").reshape(NDEV, DFF // NDEV, D)
    f_sems, f_thru, f_land, f_token = _rs_start([g_wfi, g_wfo], "rs_ffn_start")
    (gx, acc_v, acc_b, acc_a, acc_bs, acc_ws, acc_cw), (g_win, g_wout) = _mix_bwd(
        dx1, x2d, zvg, mixed, o, hb, yb, gu, dgelu_u, dgelu_v, vhat, rslb, yhat, rsg, mod, norm1_g, win_g, a_ln_g,
        a_ln_b, wcat, wcat_t, cw, b_gn_g, b_gn_b, out_norm_a_g, out_norm_b_g, wout, pm, esel, f_token, tm)
    (g_wfi_d, g_wfo_d), (r_wfi, r_wfo) = _rs_wait(f_sems, f_thru, f_land, [acc_v], "rs_ffn_wait")
    g_wout = g_wout.reshape(NDEV, D // NDEV, D)

    vsum, dcond_all, wssum = _reduce_small(acc_f, acc_v, acc_b, acc_a, acc_bs, acc_cw, acc_ws, g_wfi_d)
    sems, g_thru, land_thru, token = _rs_start([g_win, g_wout], "rs_mix_start", after=(vsum,))

    own = lambda g: lax.dynamic_index_in_dim(g, idx, 0, keepdims=False)
    res = {}
    res["w_ffn_in"] = tuple(a.T for a in _adam_big(r_wfi, w_ffn_in[0].T, m_w_ffn_in[0].T, v_w_ffn_in[0].T, WFI_B // 4,
                                                   "adam_w_ffn_in", own=own(g_wfi_d), after=token))
    res["w_ffn_out"] = _adam_big(r_wfo, w_ffn_out[0], m_w_ffn_out[0], v_w_ffn_out[0], DFF // NDEV // 2,
                                 "adam_w_ffn_out", own=own(g_wfo_d), after=token)
    cact_t = (call * jax.nn.sigmoid(call)).T
    dcond = dcond_all.reshape(NDEV, 8 * D)
    nada = ada_w.shape[2]
    nadf = ada_f_w.shape[1]
    dcs = lax.dynamic_slice(dcond, (0, idx * nada), (NDEV, nada))
    dcfs = lax.dynamic_slice(dcond, (0, 6 * D + idx * nadf), (NDEV, nadf))
    res["ada_w"] = _adam_ada(cact_t, dcs, ada_w[0], m_ada_w[0], v_ada_w[0], 256, "adam_ada_w")
    res["ada_f_w"] = _adam_ada(cact_t, dcfs, ada_f_w, m_ada_f_w, v_ada_f_w, 256, "adam_ada_f_w")
    ncw = b_conv_w.shape[2]
    gcw = jnp.concatenate([lax.dynamic_slice(vsum, (_CW_ROW, idx * ncw), (HALO // 2, ncw)),
                           lax.dynamic_slice(vsum, (_CW_ROW, DB + idx * ncw), (HALO // 2, ncw))], axis=0)[:KW]
    two = lambda a: a.reshape(1, -1) if a.ndim == 1 else a.reshape(-1, a.shape[-1])
    small_in = {
        "ada_b": (ada_b, m_ada_b, v_ada_b), "ada_f_b": (ada_f_b, m_ada_f_b, v_ada_f_b),
        "norm1_g": (norm1_g, m_norm1_g, v_norm1_g), "b_in": (b_in, m_b_in, v_b_in),
        "a_ln_g": (a_ln_g, m_a_ln_g, v_a_ln_g), "a_ln_b": (a_ln_b, m_a_ln_b, v_a_ln_b),
        "a_spatial_b": (a_spatial_b.reshape(1, D), m_a_spatial_b.reshape(1, D), v_a_spatial_b.reshape(1, D)),
        "b_conv_b": (b_conv_b, m_b_conv_b, v_b_conv_b), "b_gn_g": (b_gn_g, m_b_gn_g, v_b_gn_g),
        "b_gn_b": (b_gn_b, m_b_gn_b, v_b_gn_b), "out_norm_a_g": (out_norm_a_g, m_out_norm_a_g, v_out_norm_a_g),
        "out_norm_b_g": (out_norm_b_g, m_out_norm_b_g, v_out_norm_b_g),
        "norm2_g": (norm2_g, m_norm2_g, v_norm2_g), "norm_f_g": (norm_f_g, m_norm_f_g, v_norm_f_g),
        "a_spatial_w": (a_spatial_w, m_a_spatial_w, v_a_spatial_w),
        "b_conv_w": (b_conv_w[0], m_b_conv_w[0], v_b_conv_w[0]),
    }
    small_in = {n: tuple(two(a) for a in t) for n, t in small_in.items()}
    res.update(_adam_small(vsum, wssum, gcw, small_in))
    (g_win_d, g_wout_d), (r_win, r_wout) = _rs_wait(
        sems, g_thru, land_thru,
        [res["w_ffn_in"][0], res["w_ffn_out"][0], res["ada_w"][0], res["ada_f_w"][0], res["norm_f_g"][0]],
        "rs_mix_wait")
    res["w_in"] = _adam_big(r_win, w_in[0], m_w_in[0], v_w_in[0], 256, "adam_w_in", own=own(g_win_d))
    res["w_out"] = _adam_big(r_wout, w_out[0], m_w_out[0], v_w_out[0], D // NDEV, "adam_w_out", own=own(g_wout_d))

    loss = 0.5 / D * jnp.sum(vsum[_LOSS_ROW])
    shapes = {"ada_w": ada_w, "ada_b": ada_b, "norm1_g": norm1_g, "w_in": w_in, "b_in": b_in, "a_ln_g": a_ln_g,
              "a_ln_b": a_ln_b, "a_spatial_w": a_spatial_w, "a_spatial_b": a_spatial_b, "b_conv_w": b_conv_w,
              "b_conv_b": b_conv_b, "b_gn_g": b_gn_g, "b_gn_b": b_gn_b, "out_norm_a_g": out_norm_a_g,
              "out_norm_b_g": out_norm_b_g, "w_out": w_out, "norm2_g": norm2_g, "w_ffn_in": w_ffn_in,
              "w_ffn_out": w_ffn_out, "ada_f_w": ada_f_w, "ada_f_b": ada_f_b, "norm_f_g": norm_f_g}
    order = list(shapes)
    outs = [loss, gx.reshape(x.shape)]
    for which in range(4):
        outs += [res[n][which].reshape(shapes[n].shape) for n in order]
    return tuple(outs)
```

```python
import math

import jax
import jax.numpy as jnp
from jax import lax
from jax.experimental import pallas as pl
from jax.experimental.pallas import tpu as pltpu

F32 = jnp.float32
BF16 = jnp.bfloat16

D = 1024
DA = 512
DB = 512
DIN = 2048
DFF = 2816
NH = 8
CH = 128
KW = 31
HALO = 32
NDEV = 8
WIN_B = DIN // NDEV
WFI_B = 2 * DFF // NDEV
EPS = 1e-6
NVEC = 40
VMEM_LIMIT = 56 * 1024 * 1024

ADAM_LR, ADAM_B1, ADAM_B2, ADAM_EPS, ADAM_WD, ADAM_STEP = 0.001, 0.9, 0.999, 1e-08, 0.01, 10

MESH = pl.DeviceIdType.MESH


def _dot(a, b):
    return jnp.dot(a, b, preferred_element_type=F32)


def _dot_nt(a, b):
    return lax.dot_general(a, b, (((1,), (1,)), ((), ())), preferred_element_type=F32)


def _dot_tn(a, b):
    return lax.dot_general(a, b, (((0,), (0,)), ((), ())), preferred_element_type=F32)


def _rs(v):
    return lax.rsqrt(jnp.mean(v * v, axis=-1, keepdims=True) + EPS)


def _sig(v):
    return 1.0 / (1.0 + jnp.exp(-v))


_INV_SQRT2 = 1.0 / math.sqrt(2.0)
_INV_SQRT2PI = 1.0 / math.sqrt(2.0 * math.pi)


def _gelu_parts(v):
    cdf = 0.5 * (1.0 + lax.erf(v * _INV_SQRT2))
    pdf = jnp.exp(-0.5 * v * v) * _INV_SQRT2PI
    return v * cdf, cdf + v * pdf


def _grp_mean(v, pm):
    hi = v.astype(BF16)
    lo = (v - hi.astype(F32)).astype(BF16)
    return _dot(hi, pm) + _dot(lo, pm)


def _colsum(v):
    return jnp.sum(v, axis=0, keepdims=True)


def _full(shape):
    nd = len(shape)
    return pl.BlockSpec(shape, lambda *_: (0,) * nd)


def _resident(shape):
    nd = len(shape)
    return pl.BlockSpec(shape, lambda *_: (0,) * nd, pipeline_mode=pl.Buffered(1))


HBM = pl.BlockSpec(memory_space=pl.ANY)
VM = pl.BlockSpec(memory_space=pltpu.VMEM)


SH_ROWS = HALO - 8


def _shifted_copies(buf, shbuf, tm):
    for b in range(1, 8):
        shbuf[b - 1] = buf[b:b + tm + SH_ROWS, :]


def _window(buf, shbuf, off, tm):
    a, b = divmod(off, 8)
    if b == 0:
        return buf[8 * a:8 * a + tm, :]
    return shbuf[b - 1, 8 * a:8 * a + tm, :]


def _first_head_lanes():
    return lax.broadcasted_iota(jnp.int32, (CH, CH), 1) < (DA // NH)


def _mix_heads(w_ref, vb, first):
    outs = []
    for p in range(NH // 2):
        v = vb[:, p * CH:(p + 1) * CH]
        a = _dot(w_ref[(2 * p) * CH:(2 * p + 1) * CH, :], v)
        b = _dot(w_ref[(2 * p + 1) * CH:(2 * p + 2) * CH, :], v)
        outs.append(jnp.where(first, a, b))
    return jnp.concatenate(outs, axis=1)


def _place():
    x, y, c = lax.axis_index("x"), lax.axis_index("y"), lax.axis_index("c")
    return x, y, c, 4 * x + 2 * y + c


def _dev(t):
    return (t >> 2, (t >> 1) & 1, t & 1)


class _AllGather:
    def __init__(self, w_in, w_out, wss, wrs, lsem, slot=lambda ref, b: ref.at[b]):
        x, y, c, idx = _place()
        me, sibling = (x, y, c), (x, y, 1 - c)
        chips = [(1 - x, y), (x, 1 - y), (1 - x, 1 - y)]
        nw = len(w_in)

        def blk(p):
            return 4 * p[0] + 2 * p[1] + p[2]

        def wcopy(a, k, block, to, src=None):
            dst = slot(w_out[a], blk(block))
            return pltpu.make_async_remote_copy(src_ref=dst if src is None else src, dst_ref=dst,
                                                send_sem=wss.at[a, k], recv_sem=wrs.at[a, k],
                                                device_id=to, device_id_type=MESH)

        self.mine = [pltpu.make_async_copy(w_in[a], slot(w_out[a], idx), lsem.at[a]) for a in range(nw)]
        self.first = []
        for a in range(nw):
            self.first.append(wcopy(a, 0, me, sibling, src=w_in[a]))
            self.first += [wcopy(a, 1 + j, me, (*chip, c), src=w_in[a]) for j, chip in enumerate(chips)]
        self.landed = [[wcopy(a, 1 + j, (*chip, c), me) for a in range(nw)] for j, chip in enumerate(chips)]
        self.passed = [[wcopy(a, 4 + j, (*chip, c), sibling) for a in range(nw)] for j, chip in enumerate(chips)]
        self.from_sibling = []
        for a in range(nw):
            self.from_sibling.append(wcopy(a, 0, sibling, me))
            self.from_sibling += [wcopy(a, 4 + j, (*chip, 1 - c), me) for j, chip in enumerate(chips)]

    def start(self):
        for cp in self.mine + self.first:
            cp.start()

    def forward(self):
        for land, pas in zip(self.landed, self.passed):
            for l, p in zip(land, pas):
                l.wait_recv()
                p.start()

    def finish(self):
        for cp in self.from_sibling:
            cp.wait_recv()
        for cp in self.first:
            cp.wait_send()
        for pas in self.passed:
            for p in pas:
                p.wait_send()
        for cp in self.mine:
            cp.wait()


AG_SEMS = lambda nw: [pltpu.SemaphoreType.DMA((nw, 7)), pltpu.SemaphoreType.DMA((nw, 7)),
                      pltpu.SemaphoreType.DMA((nw,))]


class _ReduceScatter:
    def __init__(self, g_in, r_out, gss, grs, lsem):
        x, y, c, idx = _place()
        me = (x, y, c)
        nw = len(g_in)
        self.mine = [pltpu.make_async_copy(g_in[a].at[idx], r_out[a].at[0], lsem.at[a]) for a in range(nw)]
        self.sends, self.recvs = [], []
        for k in range(1, NDEV):
            t = idx ^ k
            for a in range(nw):
                self.sends.append(pltpu.make_async_remote_copy(
                    src_ref=g_in[a].at[t], dst_ref=r_out[a].at[k], send_sem=gss.at[a, k - 1],
                    recv_sem=grs.at[a, k - 1], device_id=_dev(t), device_id_type=MESH))
                self.recvs.append(pltpu.make_async_remote_copy(
                    src_ref=g_in[a].at[0], dst_ref=r_out[a].at[k], send_sem=gss.at[a, k - 1],
                    recv_sem=grs.at[a, k - 1], device_id=me, device_id_type=MESH))

    def start(self):
        for cp in self.mine + self.sends:
            cp.start()

    def finish(self):
        for cp in self.recvs:
            cp.wait_recv()
        for cp in self.sends:
            cp.wait_send()
        for cp in self.mine:
            cp.wait()


RS_SEMS = AG_SEMS


def _wout_rows(ref, b):
    rows = D // NDEV
    return ref.at[pl.ds(pl.multiple_of(b * rows, rows), rows)]


def _mix_fwd(x, mod, g1, win, b_in, lng, lnb, wcat, bsf, cw, cb, gng, gnb, oga, ogb, wout_s, pm, ffn_shards, tm):
    T = x.shape[0]
    nt = T // tm
    nch = tm // CH
    nw = len(ffn_shards)
    fwd_step = (5 * nt) // 8
    saved = [(D, F32), (D, BF16), (2 * DB, F32), (DA, F32), (D, BF16), (D, F32), (DA, F32), (DA, F32), (DA, F32),
             (DA, F32), (CH, F32), (DB, F32), (DB, F32)]
    NSAVE = len(saved)

    def body(x_ref, mod_ref, g1_ref, win_ref, bin_ref, lng_ref, lnb_ref, wcat_ref, bsf_ref, cw_ref, cb_ref,
             gng_ref, gnb_ref, oga_ref, ogb_ref, wouts_ref, pm_ref, *rest):
        sh_in = rest[:nw]
        (x1_ref, h_ref, zvg_ref, mixed_ref, y_ref, o_ref, gu_ref, dgu_ref, dgv_ref, vhat_ref, rsl_ref, yhat_ref,
         rsg_ref) = rest[nw:nw + NSAVE]
        sh_out = rest[nw + NSAVE:2 * nw + NSAVE]
        woutg_ref = rest[2 * nw + NSAVE]
        glbuf, shbuf, wout_v, wss, wrs, lsem, oss, ors, olsem = rest[2 * nw + NSAVE + 1:]
        i = pl.program_id(0)
        gather_wout = lambda: _AllGather([wouts_ref], [wout_v], oss, ors, olsem, slot=_wout_rows)

        @pl.when(i == 0)
        def _():
            gather_wout().start()
            _AllGather(sh_in, sh_out, wss, wrs, lsem).start()

        xv = x_ref[...]
        shift1 = mod_ref[0:1, :]
        scale1 = mod_ref[1:2, :]
        gate1 = mod_ref[2:3, :]
        h = (xv * _rs(xv) * g1_ref[...]) * (1.0 + scale1) + shift1
        hb = h.astype(BF16)
        h_ref[...] = hb
        z = jnp.concatenate([_dot(hb, win_ref[j]) for j in range(NDEV)], axis=1) + bin_ref[...]
        zvg_ref[...] = z[:, 2 * DA:]
        gu, dgelu_u = _gelu_parts(z[:, 0:DA])
        gv, dgelu_v = _gelu_parts(z[:, DA:2 * DA])
        gu_ref[...] = gu
        dgu_ref[...] = dgelu_u
        dgv_ref[...] = dgelu_v
        xc = gv - jnp.mean(gv, axis=-1, keepdims=True)
        rsl = lax.rsqrt(jnp.mean(xc * xc, axis=-1, keepdims=True) + EPS)
        vhat = xc * rsl
        vhat_ref[...] = vhat
        rsl_ref[...] = jnp.broadcast_to(rsl, (tm, CH))
        vnb = (vhat * lng_ref[...] + lnb_ref[...]).astype(BF16)
        first = _first_head_lanes()
        chunks = []
        for ci in range(nch):
            chunks.append(_mix_heads(wcat_ref, vnb[ci * CH:(ci + 1) * CH, :], first) + bsf_ref[...])
        mixed = jnp.concatenate(chunks, axis=0) if nch > 1 else chunks[0]
        mixed_ref[...] = mixed
        ya = gu * mixed
        gl = z[:, 2 * DA:2 * DA + DB] * _sig(z[:, 2 * DA + DB:])

        @pl.when(i == 0)
        def _():
            glbuf[0:HALO, :] = jnp.zeros((HALO, DB), F32)

        glbuf[HALO:HALO + tm, :] = gl
        _shifted_copies(glbuf, shbuf, tm)
        yc = jnp.zeros((tm, DB), F32) + cb_ref[...]
        for k in range(KW):
            yc = yc + cw_ref[k:k + 1, :] * _window(glbuf, shbuf, HALO - (KW - 1) + k, tm)
        glbuf[0:HALO, :] = gl[tm - HALO:, :]
        pmv = pm_ref[...]
        dc = yc - _grp_mean(yc, pmv)
        rsg = lax.rsqrt(_grp_mean(dc * dc, pmv) + EPS)
        yhat = dc * rsg
        yhat_ref[...] = yhat
        rsg_ref[...] = rsg
        yg = yhat * gng_ref[...] + gnb_ref[...]
        yb = yg * _sig(yg)
        na = ya * _rs(ya) * oga_ref[...]
        nb = yb * _rs(yb) * ogb_ref[...]
        yv = jnp.concatenate([na, nb], axis=1).astype(BF16)
        y_ref[...] = yv

        @pl.when(i == 0)
        def _():
            ag = gather_wout()
            ag.forward()
            ag.finish()

        o = _dot(yv, wout_v[...])
        o_ref[...] = o
        x1_ref[...] = xv + gate1 * o

        @pl.when(i == fwd_step)
        def _():
            _AllGather(sh_in, sh_out, wss, wrs, lsem).forward()

        @pl.when(i == nt - 1)
        def _():
            _AllGather(sh_in, sh_out, wss, wrs, lsem).finish()
            pltpu.sync_copy(wout_v, woutg_ref)

    tile = lambda w: pl.BlockSpec((tm, w), lambda i: (i, 0))
    outs = pl.pallas_call(
        body,
        name="mix_fwd",
        grid=(nt,),
        in_specs=[tile(D), _full((8, D)), _full((1, D)), _resident((NDEV, D, WIN_B)), _full((1, DIN)),
                  _full((1, DA)), _full((1, DA)), _full((NH * CH, CH)), _full((CH, DA)), _full((HALO, DB)),
                  _full((1, DB)), _full((1, DB)), _full((1, DB)), _full((1, DA)), _full((1, DB)),
                  _full((D // NDEV, D)), _full((DB, DB))] + [HBM] * nw,
        out_specs=[tile(w) for w, _ in saved] + [HBM] * (nw + 1),
        out_shape=[jax.ShapeDtypeStruct((T, w), dt) for w, dt in saved]
                  + [jax.ShapeDtypeStruct((NDEV,) + s.shape, s.dtype) for s in ffn_shards]
                  + [jax.ShapeDtypeStruct((D, D), BF16)],
        scratch_shapes=[pltpu.VMEM((HALO + tm, DB), F32), pltpu.VMEM((7, tm + SH_ROWS, DB), F32),
                        pltpu.VMEM((D, D), BF16)] + AG_SEMS(nw) + AG_SEMS(1),
        compiler_params=pltpu.CompilerParams(dimension_semantics=("arbitrary",), vmem_limit_bytes=VMEM_LIMIT),
    )(x, mod, g1, win, b_in, lng, lnb, wcat, bsf, cw, cb, gng, gnb, oga, ogb, wout_s, pm, *ffn_shards)
    return outs[:NSAVE], outs[NSAVE:NSAVE + nw], outs[NSAVE + nw]


FF_BLOCKS = ((0, 1024), (1024, 1024), (2048, 768))


def _ffn(x1, tgt, mod, g2, gf, wfi_t, wfo, tm):
    T = x1.shape[0]
    nt = T // tm

    def body(x1_ref, tgt_ref, mod_ref, g2_ref, gf_ref, wfi_ref, wfo_ref,
             dx1_ref, h2_ref, dgu_ref, act_ref, dxg_ref, acc_ref, g_s, u_s):
        i = pl.program_id(0)

        @pl.when(i == 0)
        def _():
            acc_ref[...] = jnp.zeros((8, D), F32)

        x1 = x1_ref[...]
        shift2 = mod_ref[3:4, :]
        scale2 = mod_ref[4:5, :]
        gate2 = mod_ref[5:6, :]
        shiftf = mod_ref[6:7, :]
        scalef = mod_ref[7:8, :]
        g2v = g2_ref[...]
        gfv = gf_ref[...]
        r2 = _rs(x1)
        xn2 = x1 * r2
        h2b = (xn2 * g2v * (1.0 + scale2) + shift2).astype(BF16)
        h2_ref[...] = h2b
        f = jnp.zeros((tm, D), F32)
        for o, w in FF_BLOCKS:
            g = _dot_nt(h2b, wfi_ref[o:o + w, :])
            u = _dot_nt(h2b, wfi_ref[DFF + o:DFF + o + w, :])
            g_s[:, o:o + w] = g
            u_s[:, o:o + w] = u
            actb = (g * _sig(g) * u).astype(BF16)
            act_ref[:, o:o + w] = actb
            f = f + _dot(actb, wfo_ref[o:o + w, :])
        x2 = x1 + gate2 * f
        rf = _rs(x2)
        xnf = x2 * rf
        out = xnf * gfv * (1.0 + scalef) + shiftf
        e = out - tgt_ref[...]
        dout = e * (1.0 / D)
        acc_ref[7:8, :] += _colsum(e * e)
        acc_ref[0:1, :] += _colsum(dout)
        acc_ref[1:2, :] += _colsum(dout * xnf * gfv)
        acc_ref[2:3, :] += _colsum(dout * (1.0 + scalef) * xnf)
        dxnf = dout * (1.0 + scalef) * gfv
        dx2 = rf * (dxnf - xnf * jnp.mean(dxnf * xnf, axis=-1, keepdims=True))
        acc_ref[3:4, :] += _colsum(dx2 * f)
        dxgb = (dx2 * gate2).astype(BF16)
        dxg_ref[...] = dxgb
        dh2 = jnp.zeros((tm, D), F32)
        for o, w in FF_BLOCKS:
            dact = _dot_nt(dxgb, wfo_ref[o:o + w, :])
            g = g_s[:, o:o + w]
            u = u_s[:, o:o + w]
            s = _sig(g)
            dgb = (dact * u * (s * (1.0 + g * (1.0 - s)))).astype(BF16)
            dub = (dact * (g * s)).astype(BF16)
            dgu_ref[:, o:o + w] = dgb
            dgu_ref[:, DFF + o:DFF + o + w] = dub
            dh2 = dh2 + _dot(dgb, wfi_ref[o:o + w, :])
            dh2 = dh2 + _dot(dub, wfi_ref[DFF + o:DFF + o + w, :])
        acc_ref[4:5, :] += _colsum(dh2)
        acc_ref[5:6, :] += _colsum(dh2 * xn2 * g2v)
        acc_ref[6:7, :] += _colsum(dh2 * (1.0 + scale2) * xn2)
        dxn2 = dh2 * (1.0 + scale2) * g2v
        dx1_ref[...] = dx2 + r2 * (dxn2 - xn2 * jnp.mean(dxn2 * xn2, axis=-1, keepdims=True))

    tile = lambda w: pl.BlockSpec((tm, w), lambda i: (i, 0))
    return pl.pallas_call(
        body,
        name="ffn_fwd_bwd",
        grid=(nt,),
        in_specs=[tile(D), tile(D), _full((8, D)), _full((1, D)), _full((1, D)),
                  _resident((2 * DFF, D)), _resident((DFF, D))],
        out_specs=[tile(D), tile(D), tile(2 * DFF), tile(DFF), tile(D), _full((8, D))],
        out_shape=[jax.ShapeDtypeStruct((T, D), F32), jax.ShapeDtypeStruct((T, D), BF16),
                   jax.ShapeDtypeStruct((T, 2 * DFF), BF16), jax.ShapeDtypeStruct((T, DFF), BF16),
                   jax.ShapeDtypeStruct((T, D), BF16), jax.ShapeDtypeStruct((8, D), F32)],
        scratch_shapes=[pltpu.VMEM((tm, DFF), F32), pltpu.VMEM((tm, DFF), F32)],
        compiler_params=pltpu.CompilerParams(dimension_semantics=("arbitrary",), vmem_limit_bytes=VMEM_LIMIT),
    )(x1, tgt, mod, g2, gf, wfi_t, wfo)


def _mix_bwd(dx1, x, zvg, mixed, o, hb, yb, gu, dgu, dgv, vhat, rslb, yhat, rsg, mod, g1, win, lng, lnb, wcat, wcat_t,
             cw, gng, gnb, oga, ogb, wout, pm, esel, after, tm):
    T = x.shape[0]
    nt = T // tm
    nch = tm // CH
    WOB = 256

    def body(dx1_ref, x_ref, zvg_ref, mixed_ref, o_ref, hb_ref, yb_ref, gu_ref, dgu_ref, dgv_ref, vhat_ref, rsl_ref,
             yhat_ref, rsg_ref, mod_ref, g1_ref, win_ref, lng_ref, lnb_ref, wcat_ref, wcatt_ref, cw_ref, gng_ref,
             gnb_ref, oga_ref, ogb_ref, wout_ref, pm_ref, esel_ref, after_ref,
             gx_ref, accv_ref, accb_ref, acca_ref, accbs_ref, accws_ref, acccw_ref, gwin_ref, gwout_ref,
             dycbuf, shbuf, bs_s, acc_win, acc_wout, st_win, st_wout):
        i = pl.program_id(0)

        @pl.when(i == 0)
        def _():
            acc_win[...] = jnp.zeros((NDEV, D, WIN_B), F32)
            acc_wout[...] = jnp.zeros((D, D), F32)
            accv_ref[...] = jnp.zeros((8, D), F32)
            accb_ref[...] = jnp.zeros((1, DIN), F32)
            acca_ref[...] = jnp.zeros((8, DA), F32)
            accws_ref[...] = jnp.zeros((NH * CH, CH), F32)
            acccw_ref[...] = jnp.zeros((HALO, DB), F32)
            bs_s[...] = jnp.zeros((CH, DA), F32)
            dycbuf[tm:tm + HALO, :] = jnp.zeros((HALO, DB), F32)

        shift1 = mod_ref[0:1, :]
        scale1 = mod_ref[1:2, :]
        gate1 = mod_ref[2:3, :]
        g1v = g1_ref[...]
        xv = x_ref[...]
        r1 = _rs(xv)
        xn1 = xv * r1
        val = zvg_ref[:, 0:DB]
        gate = zvg_ref[:, DB:]
        gu = gu_ref[...]
        dgelu_u = dgu_ref[...]
        dgelu_v = dgv_ref[...]
        vhat = vhat_ref[...]
        rsl = rsl_ref[:, 0:1]
        lngv = lng_ref[...]
        vnb = (vhat * lngv + lnb_ref[...]).astype(BF16)
        mixed = mixed_ref[...]
        ya = gu * mixed
        ra = _rs(ya)
        yan = ya * ra
        sgt = _sig(gate)
        gl = val * sgt
        pmv = pm_ref[...]
        rsg = rsg_ref[...]
        yhat = yhat_ref[...]
        gngv = gng_ref[...]
        yg = yhat * gngv + gnb_ref[...]
        sgy = _sig(yg)
        yb = yg * sgy
        rb = _rs(yb)
        ybn = yb * rb
        dx1 = dx1_ref[...]
        accv_ref[0:1, :] += _colsum(dx1 * o_ref[...])
        dogb = (dx1 * gate1).astype(BF16)
        acc_wout[...] += _dot_tn(yb_ref[...], dogb)
        dy = _dot_nt(dogb, wout_ref[...])
        dna = dy[:, 0:DA]
        dnb = dy[:, DA:]
        ogav = oga_ref[...]
        ogbv = ogb_ref[...]
        acca_ref[2:3, :] += _colsum(dna * yan)
        acca_ref[3:4, :] += _colsum(dnb * ybn)
        ta = dna * ogav
        dya = ra * (ta - yan * jnp.mean(ta * yan, axis=-1, keepdims=True))
        tb = dnb * ogbv
        dyb = rb * (tb - ybn * jnp.mean(tb * ybn, axis=-1, keepdims=True))
        dgu = dya * mixed
        dm = dya * gu
        first = _first_head_lanes()
        zero = jnp.zeros((CH, CH), BF16)
        dvn_chunks = []
        bs_acc = bs_s[...]
        for ci in range(nch):
            dmc = dm[ci * CH:(ci + 1) * CH, :]
            bs_acc = bs_acc + dmc
            dmcb = dmc.astype(BF16)
            dvn_chunks.append(_mix_heads(wcatt_ref, dmcb, first))
            vc = vnb[ci * CH:(ci + 1) * CH, :]
            for p in range(NH // 2):
                xt = dmcb[:, p * CH:(p + 1) * CH]
                vt = vc[:, p * CH:(p + 1) * CH]
                accws_ref[(2 * p) * CH:(2 * p + 1) * CH, :] += _dot_nt(jnp.where(first, xt, zero), vt)
                accws_ref[(2 * p + 1) * CH:(2 * p + 2) * CH, :] += _dot_nt(jnp.where(first, zero, xt), vt)
        bs_s[...] = bs_acc
        dvn = jnp.concatenate(dvn_chunks, axis=0) if nch > 1 else dvn_chunks[0]
        acca_ref[0:1, :] += _colsum(dvn * vhat)
        acca_ref[1:2, :] += _colsum(dvn)
        dvh = dvn * lngv
        dgv = rsl * (dvh - jnp.mean(dvh, axis=-1, keepdims=True)
                     - vhat * jnp.mean(dvh * vhat, axis=-1, keepdims=True))
        du = dgu * dgelu_u
        dv = dgv * dgelu_v
        dyg = dyb * (sgy * (1.0 + yg * (1.0 - sgy)))
        acca_ref[5:6, :] += _colsum(dyg * yhat)
        acca_ref[6:7, :] += _colsum(dyg)
        dyh = dyg * gngv
        dyc = rsg * (dyh - _grp_mean(dyh, pmv) - yhat * _grp_mean(dyh * yhat, pmv))
        acca_ref[4:5, :] += _colsum(dyc)
        dycbuf[0:tm, :] = dyc
        _shifted_copies(dycbuf, shbuf, tm)
        dgl = jnp.zeros((tm, DB), F32)
        for k in range(KW):
            win_k = _window(dycbuf, shbuf, KW - 1 - k, tm)
            dgl = dgl + cw_ref[k:k + 1, :] * win_k
            acccw_ref[k:k + 1, :] += _colsum(win_k * gl)
        dycbuf[tm:tm + HALO, :] = dyc[0:HALO, :]
        dval = dgl * sgt
        dgate = dgl * val * sgt * (1.0 - sgt)
        dz = jnp.concatenate([du, dv, dval, dgate], axis=1)
        accb_ref[...] += _colsum(dz)
        dzb = dz.astype(BF16)
        hbv = hb_ref[...]
        dh = jnp.zeros((tm, D), F32)
        for j in range(NDEV):
            dzj = dzb[:, j * WIN_B:(j + 1) * WIN_B]
            acc_win[j] += _dot_tn(hbv, dzj)
            dh = dh + _dot_nt(dzj, win_ref[j])
        accv_ref[1:2, :] += _colsum(dh)
        dh_xn = _colsum(dh * xn1)
        accv_ref[2:3, :] += dh_xn * g1v
        accv_ref[3:4, :] += dh_xn * (1.0 + scale1)
        dxn1 = dh * (1.0 + scale1) * g1v
        gx_ref[...] = dx1 + r1 * (dxn1 - xn1 * jnp.mean(dxn1 * xn1, axis=-1, keepdims=True))

        @pl.when(i == nt - 1)
        def _():
            rows = lax.broadcasted_iota(jnp.int32, (NH * CH, CH), 0) & (CH - 1)
            cols = lax.broadcasted_iota(jnp.int32, (NH * CH, CH), 1)
            accws_ref[...] = jnp.where(cols <= rows, accws_ref[...], 0.0)
            bs = bs_s[...]
            hi = bs.astype(BF16)
            r1_ = bs - hi.astype(F32)
            mid = r1_.astype(BF16)
            lo = (r1_ - mid.astype(F32)).astype(BF16)
            ev = esel_ref[...]
            accbs_ref[...] = _dot(hi, ev) + _dot(mid, ev) + _dot(lo, ev)
            for j in range(NDEV):
                st_win[...] = acc_win[j].astype(BF16)
                pltpu.sync_copy(st_win, gwin_ref.at[j])
            for j in range(D // WOB):
                st_wout[...] = acc_wout[j * WOB:(j + 1) * WOB, :].astype(BF16)
                pltpu.sync_copy(st_wout, gwout_ref.at[pl.ds(j * WOB, WOB)])

    rev = lambda w: pl.BlockSpec((tm, w), lambda i: (nt - 1 - i, 0))
    outs = pl.pallas_call(
        body,
        name="mix_bwd",
        grid=(nt,),
        in_specs=[rev(D), rev(D), rev(2 * DB), rev(DA), rev(D), rev(D), rev(D), rev(DA), rev(DA), rev(DA), rev(DA),
                  rev(CH), rev(DB), rev(DB), _full((8, D)), _full((1, D)),
                  _resident((NDEV, D, WIN_B)), _full((1, DA)), _full((1, DA)), _full((NH * CH, CH)),
                  _full((NH * CH, CH)), _full((HALO, DB)), _full((1, DB)), _full((1, DB)), _full((1, DA)),
                  _full((1, DB)), _resident((D, D)), _full((DB, DB)), _full((DA, CH)), HBM],
        out_specs=[rev(D), _full((8, D)), _full((1, DIN)), _full((8, DA)), _full((CH, CH)),
                   _full((NH * CH, CH)), _full((HALO, DB)), HBM, HBM],
        out_shape=[jax.ShapeDtypeStruct((T, D), F32), jax.ShapeDtypeStruct((8, D), F32),
                   jax.ShapeDtypeStruct((1, DIN), F32), jax.ShapeDtypeStruct((8, DA), F32),
                   jax.ShapeDtypeStruct((CH, CH), F32), jax.ShapeDtypeStruct((NH * CH, CH), F32),
                   jax.ShapeDtypeStruct((HALO, DB), F32),
                   jax.ShapeDtypeStruct((NDEV, D, WIN_B), BF16), jax.ShapeDtypeStruct((D, D), BF16)],
        scratch_shapes=[pltpu.VMEM((tm + HALO, DB), F32), pltpu.VMEM((7, tm + SH_ROWS, DB), F32),
                        pltpu.VMEM((CH, DA), F32), pltpu.VMEM((NDEV, D, WIN_B), F32), pltpu.VMEM((D, D), F32),
                        pltpu.VMEM((D, WIN_B), BF16), pltpu.VMEM((WOB, D), BF16)],
        compiler_params=pltpu.CompilerParams(dimension_semantics=("arbitrary",), vmem_limit_bytes=VMEM_LIMIT),
    )(dx1, x, zvg, mixed, o, hb, yb, gu, dgu, dgv, vhat, rslb, yhat, rsg, mod, g1, win, lng, lnb, wcat, wcat_t, cw,
      gng, gnb, oga, ogb, wout, pm, esel, after)
    return outs[:7], outs[7:]


def _wgrad_rows(a, b, bm, tk, name):
    T, M = a.shape
    N = b.shape[1]
    nk = T // tk

    def body(a_ref, b_ref, o_ref, acc):
        k = pl.program_id(1)

        @pl.when(k == 0)
        def _():
            acc[...] = jnp.zeros((bm, N), F32)

        acc[...] += _dot_tn(a_ref[...], b_ref[...])

        @pl.when(k == nk - 1)
        def _():
            o_ref[...] = acc[...].astype(BF16)

    return pl.pallas_call(
        body, name=name, grid=(M // bm, nk),
        in_specs=[pl.BlockSpec((tk, bm), lambda j, k: (k, j)), pl.BlockSpec((tk, N), lambda j, k: (k, 0))],
        out_specs=pl.BlockSpec((bm, N), lambda j, k: (j, 0)),
        out_shape=jax.ShapeDtypeStruct((M, N), BF16),
        scratch_shapes=[pltpu.VMEM((bm, N), F32)],
        compiler_params=pltpu.CompilerParams(dimension_semantics=("arbitrary", "arbitrary"),
                                             vmem_limit_bytes=VMEM_LIMIT),
    )(a, b)


def _small_copy(src, dst, ss, rs, k, to):
    return pltpu.make_async_remote_copy(src_ref=src, dst_ref=dst, send_sem=ss.at[k], recv_sem=rs.at[k],
                                        device_id=to, device_id_type=MESH)


def _gather(c_row, ada_w, ada_b8, ada_f_w, ada_f_b8, conv_s, shards):
    nw = len(shards)

    def body(c_ref, adaw_ref, adab_ref, adafw_ref, adafb_ref, conv_ref, *rest):
        w_in = rest[:nw]
        call_ref, cparts_ref, cfparts_ref, convg_ref = rest[nw:nw + 4]
        w_out = rest[nw + 4:2 * nw + 4]
        part_s, partf_s, wss, wrs, lsem, s1, r1, s2, r2, s3, r3, s4, r4 = rest[2 * nw + 4:]
        x, y, c, idx = _place()
        me = (x, y, c)
        ag = _AllGather(w_in, w_out, wss, wrs, lsem)
        ag.start()
        call_ref[pl.ds(idx, 1), :] = c_ref[...]
        convg_ref[idx] = conv_ref[...]
        ph1 = []
        for k in range(1, NDEV):
            to = _dev(idx ^ k)
            ph1.append(_small_copy(c_ref, call_ref.at[pl.ds(idx, 1)], s1, r1, k - 1, to))
            ph1.append(_small_copy(conv_ref, convg_ref.at[idx], s2, r2, k - 1, to))
        for cp in ph1:
            cp.start()
        for k in range(1, NDEV):
            src_dev = idx ^ k
            _small_copy(c_ref, call_ref.at[pl.ds(src_dev, 1)], s1, r1, k - 1, me).wait_recv()
            _small_copy(conv_ref, convg_ref.at[src_dev], s2, r2, k - 1, me).wait_recv()
        call = call_ref[...]
        cact = (call * _sig(call))
        part_s[...] = jnp.dot(cact, adaw_ref[...], preferred_element_type=F32,
                              precision=lax.Precision.HIGHEST) + adab_ref[pl.ds(idx, 1), :]
        partf_s[...] = jnp.dot(cact, adafw_ref[...], preferred_element_type=F32,
                               precision=lax.Precision.HIGHEST) + adafb_ref[pl.ds(idx, 1), :]
        cparts_ref[pl.ds(idx, 1), :] = part_s[pl.ds(idx, 1), :]
        cfparts_ref[pl.ds(idx, 1), :] = partf_s[pl.ds(idx, 1), :]
        ph2 = []
        for k in range(1, NDEV):
            t = idx ^ k
            ph2.append(_small_copy(part_s.at[pl.ds(t, 1)], cparts_ref.at[pl.ds(idx, 1)], s3, r3, k - 1, _dev(t)))
            ph2.append(_small_copy(partf_s.at[pl.ds(t, 1)], cfparts_ref.at[pl.ds(idx, 1)], s4, r4, k - 1, _dev(t)))
        for cp in ph2:
            cp.start()
        for k in range(1, NDEV):
            src_dev = idx ^ k
            _small_copy(part_s.at[pl.ds(0, 1)], cparts_ref.at[pl.ds(src_dev, 1)], s3, r3, k - 1, me).wait_recv()
            _small_copy(partf_s.at[pl.ds(0, 1)], cfparts_ref.at[pl.ds(src_dev, 1)], s4, r4, k - 1, me).wait_recv()
        for cp in ph1 + ph2:
            cp.wait_send()
        ag.forward()
        ag.finish()

    dma7 = pltpu.SemaphoreType.DMA((NDEV - 1,))
    outs = pl.pallas_call(
        body,
        name="gather_weights",
        in_specs=[VM] * 6 + [HBM] * nw,
        out_specs=[VM] * 4 + [HBM] * nw,
        out_shape=[jax.ShapeDtypeStruct((NDEV, D), F32), jax.ShapeDtypeStruct((NDEV, ada_w.shape[1]), F32),
                   jax.ShapeDtypeStruct((NDEV, ada_f_w.shape[1]), F32),
                   jax.ShapeDtypeStruct((NDEV,) + conv_s.shape, F32)]
                  + [jax.ShapeDtypeStruct((NDEV,) + s.shape, s.dtype) for s in shards],
        scratch_shapes=[pltpu.VMEM((NDEV, ada_w.shape[1]), F32), pltpu.VMEM((NDEV, ada_f_w.shape[1]), F32)]
                       + AG_SEMS(nw) + [dma7] * 8,
        compiler_params=pltpu.CompilerParams(vmem_limit_bytes=VMEM_LIMIT),
    )(c_row, ada_w, ada_b8, ada_f_w, ada_f_b8, conv_s, *shards)
    return outs[0], outs[1], outs[2], outs[3], outs[4:]


_VEC_AT = {
    "norm1_g": (8, 0, D), "a_ln_g": (11, 0, DA), "a_ln_b": (11, DA, DA), "a_spatial_b": (12, 0, D),
    "b_conv_b": (13, 0, DB), "b_gn_g": (13, DB, DB), "b_gn_b": (14, 0, DB), "out_norm_a_g": (14, DB, DA),
    "out_norm_b_g": (15, 0, DB), "norm2_g": (16, 0, D), "norm_f_g": (17, 0, D),
}
_LOSS_ROW = 18
_CW_ROW = 24


def _reduce_small(acc_f, acc_v, acc_b, acc_a, acc_bs, acc_cw, dws, after):
    def body(accf_ref, accv_ref, accb_ref, acca_ref, accbs_ref, acccw_ref, dws_ref, after_ref,
             vsum_ref, dcond_ref, wssum_ref, vloc, vbuf, wbuf, wown, s1, r1, s2, r2, s3, r3):
        x, y, c, idx = _place()
        me = (x, y, c)
        vloc[...] = jnp.zeros((NVEC, D), F32)
        vloc[0:1, :] = accv_ref[1:2, :]
        vloc[1:2, :] = accv_ref[2:3, :]
        vloc[2:3, :] = accv_ref[0:1, :]
        vloc[3:4, :] = accf_ref[4:5, :]
        vloc[4:5, :] = accf_ref[5:6, :]
        vloc[5:6, :] = accf_ref[3:4, :]
        vloc[6:7, :] = accf_ref[0:1, :]
        vloc[7:8, :] = accf_ref[1:2, :]
        vloc[8:9, :] = accv_ref[3:4, :]
        vloc[9:10, :] = accb_ref[:, 0:D]
        vloc[10:11, :] = accb_ref[:, D:]
        vloc[11:12, 0:DA] = acca_ref[0:1, :]
        vloc[11:12, DA:] = acca_ref[1:2, :]
        bst = accbs_ref[...].T
        for h in range(NH):
            vloc[12:13, h * CH:(h + 1) * CH] = bst[h:h + 1, :]
        vloc[13:14, 0:DB] = acca_ref[4:5, :]
        vloc[13:14, DB:] = acca_ref[5:6, :]
        vloc[14:15, 0:DB] = acca_ref[6:7, :]
        vloc[14:15, DB:] = acca_ref[2:3, :]
        vloc[15:16, 0:DB] = acca_ref[3:4, :]
        vloc[16:17, :] = accf_ref[6:7, :]
        vloc[17:18, :] = accf_ref[2:3, :]
        vloc[_LOSS_ROW:_LOSS_ROW + 1, :] = accf_ref[7:8, :]
        vloc[_CW_ROW:_CW_ROW + HALO // 2, 0:DB] = acccw_ref[0:HALO // 2, :]
        vloc[_CW_ROW:_CW_ROW + HALO // 2, DB:] = acccw_ref[HALO // 2:, :]
        vbuf[idx] = vloc[...]
        rows_of = lambda t: pl.ds(pl.multiple_of(t * CH, CH), CH)
        wbuf[0] = dws_ref[rows_of(idx), :]
        sm = []
        for k in range(1, NDEV):
            t = idx ^ k
            sm.append(_small_copy(vloc, vbuf.at[idx], s1, r1, k - 1, _dev(t)))
            sm.append(_small_copy(dws_ref.at[rows_of(t)], wbuf.at[k], s2, r2, k - 1, _dev(t)))
        for cp in sm:
            cp.start()
        for k in range(1, NDEV):
            _small_copy(dws_ref.at[rows_of(0)], wbuf.at[k], s2, r2, k - 1, me).wait_recv()
        ws = wbuf[0]
        for k in range(1, NDEV):
            ws = ws + wbuf[k]
        wown[...] = ws
        wssum_ref[rows_of(idx), :] = ws
        ag = [_small_copy(wown, wssum_ref.at[rows_of(idx)], s3, r3, k - 1, _dev(idx ^ k)) for k in range(1, NDEV)]
        for cp in ag:
            cp.start()
        for k in range(1, NDEV):
            _small_copy(vloc, vbuf.at[idx ^ k], s1, r1, k - 1, me).wait_recv()
        vs = vbuf[0]
        for d in range(1, NDEV):
            vs = vs + vbuf[d]
        vsum_ref[...] = vs
        for d in range(NDEV):
            dcond_ref[d] = vbuf[d, 0:8, :]
        for k in range(1, NDEV):
            _small_copy(wown, wssum_ref.at[rows_of(idx ^ k)], s3, r3, k - 1, me).wait_recv()
        for cp in sm + ag:
            cp.wait_send()

    dma7 = pltpu.SemaphoreType.DMA((NDEV - 1,))
    return pl.pallas_call(
        body,
        name="reduce_small",
        in_specs=[VM] * 7 + [HBM],
        out_specs=[VM, VM, VM],
        out_shape=[jax.ShapeDtypeStruct((NVEC, D), F32), jax.ShapeDtypeStruct((NDEV, 8, D), F32),
                   jax.ShapeDtypeStruct(dws.shape, F32)],
        scratch_shapes=[pltpu.VMEM((NVEC, D), F32), pltpu.VMEM((NDEV, NVEC, D), F32),
                        pltpu.VMEM((NDEV, CH, CH), F32), pltpu.VMEM((CH, CH), F32)] + [dma7] * 6,
        compiler_params=pltpu.CompilerParams(vmem_limit_bytes=VMEM_LIMIT),
    )(acc_f, acc_v, acc_b, acc_a, acc_bs, acc_cw, dws, after)


HBM_ONLY = pl.BlockSpec(memory_space=pltpu.HBM)
SEM = pl.BlockSpec(memory_space=pltpu.SEMAPHORE)
EFFECT = pltpu.SideEffectType.DATAFLOW_SIDE_EFFECTING


def _rs_copies(g_refs, land_refs, sems):
    x, y, c, idx = _place()
    cps = []
    for k in range(1, NDEV):
        t = idx ^ k
        for a in range(len(g_refs)):
            n = len(cps)
            cps.append(pltpu.make_async_remote_copy(
                src_ref=g_refs[a].at[t], dst_ref=land_refs[a].at[k - 1], send_sem=sems[2 * n],
                recv_sem=sems[2 * n + 1], device_id=_dev(t), device_id_type=MESH))
    return cps


def _rs_start(grads, name, after=()):
    nw = len(grads)
    nsem = 2 * nw * (NDEV - 1)
    lands = [lax.empty((NDEV - 1,) + g.shape[1:], g.dtype) for g in grads]

    def body(*refs):
        g_refs, land_refs = refs[:nw], refs[nw:2 * nw]
        sems = refs[2 * nw + len(after):2 * nw + len(after) + nsem]
        token = refs[-1]
        for cp in _rs_copies(g_refs, land_refs, sems):
            cp.start()
        token[...] = jnp.zeros_like(token)

    outs = pl.pallas_call(
        body, name=name,
        out_shape=(*[pltpu.SemaphoreType.DMA(())] * nsem,
                   *[pltpu.HBM(g.shape, g.dtype) for g in grads], *[pltpu.HBM(l.shape, l.dtype) for l in lands],
                   jax.ShapeDtypeStruct((8, CH), F32)),
        in_specs=[HBM_ONLY] * (2 * nw) + [HBM] * len(after),
        out_specs=(*[SEM] * nsem, *[HBM_ONLY] * (2 * nw), VM),
        input_output_aliases={i: nsem + i for i in range(2 * nw)},
        compiler_params=pltpu.CompilerParams(has_side_effects=EFFECT),
    )(*[pltpu.with_memory_space_constraint(g, pltpu.HBM) for g in grads],
      *[pltpu.with_memory_space_constraint(l, pltpu.HBM) for l in lands], *after)
    return outs[:nsem], outs[nsem:nsem + nw], outs[nsem + nw:nsem + 2 * nw], outs[-1]


def _rs_wait(sems, g_thru, land_thru, after, name):
    nw = len(g_thru)
    nsem = len(sems)

    def body(*refs):
        g_refs, land_refs = refs[:nw], refs[nw:2 * nw]
        for cp in _rs_copies(g_refs, land_refs, refs[2 * nw:2 * nw + nsem]):
            cp.wait_send()
            cp.wait_recv()

    outs = pl.pallas_call(
        body, name=name,
        out_shape=tuple(pltpu.HBM(a.shape, a.dtype) for a in list(g_thru) + list(land_thru)),
        in_specs=[HBM_ONLY] * (2 * nw) + [SEM] * nsem + [HBM] * len(after),
        out_specs=tuple([HBM_ONLY] * (2 * nw)),
        input_output_aliases={i: i for i in range(2 * nw)},
        compiler_params=pltpu.CompilerParams(has_side_effects=EFFECT),
    )(*g_thru, *land_thru, *sems, *after)
    return outs[:nw], outs[nw:]


def _adamw(w, g, m, v):
    m2 = ADAM_B1 * m + (1.0 - ADAM_B1) * g
    v2 = ADAM_B2 * v + (1.0 - ADAM_B2) * (g * g)
    m_hat = m2 / (1.0 - ADAM_B1 ** ADAM_STEP)
    v_hat = v2 / (1.0 - ADAM_B2 ** ADAM_STEP)
    delta = -ADAM_LR * (m_hat / (jnp.sqrt(v_hat) + ADAM_EPS) + ADAM_WD * w)
    return delta, m2, v2


def _adam_big(r, w, m, v, rb, name, own=None, after=None):
    R, C = w.shape
    ns = r.shape[0]

    def body(*refs):
        r_ref = refs[0]
        own_ref = refs[1] if own is not None else None
        w_ref, m_ref, v_ref, g_ref, d_ref, m2_ref, v2_ref = refs[len(refs) - 7:]
        g = r_ref[0].astype(F32) if own is None else own_ref[...].astype(F32) + r_ref[0].astype(F32)
        for k in range(1, ns):
            g = g + r_ref[k].astype(F32)
        g_ref[...] = g
        d_ref[...], m2_ref[...], v2_ref[...] = _adamw(w_ref[...], g, m_ref[...], v_ref[...])

    t2 = pl.BlockSpec((rb, C), lambda i: (i, 0))
    sd = jax.ShapeDtypeStruct((R, C), F32)
    extra_specs = ([t2] if own is not None else []) + ([HBM] if after is not None else [])
    extra = ([own] if own is not None else []) + ([after] if after is not None else [])
    return pl.pallas_call(
        body, name=name, grid=(R // rb,),
        in_specs=[pl.BlockSpec((ns, rb, C), lambda i: (0, i, 0))] + extra_specs + [t2, t2, t2],
        out_specs=[t2, t2, t2, t2], out_shape=[sd, sd, sd, sd],
        compiler_params=pltpu.CompilerParams(dimension_semantics=("arbitrary",), vmem_limit_bytes=VMEM_LIMIT),
    )(r, *extra, w, m, v)


def _adam_ada(cact_t, dcs, w, m, v, rb, name):
    R, C = w.shape

    def body(ct_ref, dc_ref, w_ref, m_ref, v_ref, g_ref, d_ref, m2_ref, v2_ref):
        g = jnp.dot(ct_ref[...], dc_ref[...], preferred_element_type=F32, precision=lax.Precision.HIGHEST)
        g_ref[...] = g
        d_ref[...], m2_ref[...], v2_ref[...] = _adamw(w_ref[...], g, m_ref[...], v_ref[...])

    t2 = pl.BlockSpec((rb, C), lambda i: (i, 0))
    sd = jax.ShapeDtypeStruct((R, C), F32)
    return pl.pallas_call(
        body, name=name, grid=(R // rb,),
        in_specs=[pl.BlockSpec((rb, NDEV), lambda i: (i, 0)), _full((NDEV, C)), t2, t2, t2],
        out_specs=[t2, t2, t2, t2], out_shape=[sd, sd, sd, sd],
        compiler_params=pltpu.CompilerParams(dimension_semantics=("arbitrary",), vmem_limit_bytes=VMEM_LIMIT),
    )(cact_t, dcs, w, m, v)


_SMALL = ["ada_b", "ada_f_b", "norm1_g", "b_in", "a_ln_g", "a_ln_b", "a_spatial_b", "b_conv_b", "b_gn_g", "b_gn_b",
          "out_norm_a_g", "out_norm_b_g", "norm2_g", "norm_f_g", "a_spatial_w", "b_conv_w"]


def _adam_small(vsum, wssum, gcw, params):
    names = _SMALL
    flat = []
    for n in names:
        flat += list(params[n])

    def body(vs_ref, ws_ref, gcw_ref, *rest):
        ins = rest[:3 * len(names)]
        outs = rest[3 * len(names):]
        for pi, n in enumerate(names):
            w_ref, m_ref, v_ref = ins[3 * pi:3 * pi + 3]
            g_ref, d_ref, m2_ref, v2_ref = outs[4 * pi:4 * pi + 4]
            if n in ("ada_b", "ada_f_b", "b_in"):
                row0 = {"ada_b": 0, "ada_f_b": 6, "b_in": 9}[n]
                pieces = [(vs_ref[row0 + r:row0 + r + 1, :], slice(r * D, (r + 1) * D))
                          for r in range(w_ref.shape[1] // D)]
            elif n == "a_spatial_w":
                pieces = [(ws_ref[...], slice(None))]
            elif n == "b_conv_w":
                pieces = [(gcw_ref[...], slice(None))]
            else:
                row, off, width = _VEC_AT[n]
                pieces = [(vs_ref[row:row + 1, off:off + width], slice(None))]
            for g, cs in pieces:
                g_ref[:, cs] = g
                d_ref[:, cs], m2_ref[:, cs], v2_ref[:, cs] = _adamw(w_ref[:, cs], g, m_ref[:, cs], v_ref[:, cs])

    out_shape = []
    for n in names:
        out_shape += [jax.ShapeDtypeStruct(params[n][0].shape, F32)] * 4
    outs = pl.pallas_call(
        body, name="adam_small",
        in_specs=[VM] * (3 + len(flat)), out_specs=[VM] * len(out_shape), out_shape=out_shape,
        compiler_params=pltpu.CompilerParams(vmem_limit_bytes=VMEM_LIMIT),
    )(vsum, wssum, gcw, *flat)
    return {n: outs[4 * pi:4 * pi + 4] for pi, n in enumerate(names)}


def _token_tile(T, want):
    return want if T % want == 0 else T


def kernel(x, c, ada_w, ada_b, norm1_g, w_in, b_in, a_ln_g, a_ln_b, a_spatial_w, a_spatial_b, b_conv_w, b_conv_b, b_gn_g, b_gn_b, out_norm_a_g, out_norm_b_g, w_out, norm2_g, w_ffn_in, w_ffn_out, ada_f_w, ada_f_b, norm_f_g, loss_target, m_ada_w, m_ada_b, m_norm1_g, m_w_in, m_b_in, m_a_ln_g, m_a_ln_b, m_a_spatial_w, m_a_spatial_b, m_b_conv_w, m_b_conv_b, m_b_gn_g, m_b_gn_b, m_out_norm_a_g, m_out_norm_b_g, m_w_out, m_norm2_g, m_w_ffn_in, m_w_ffn_out, m_ada_f_w, m_ada_f_b, m_norm_f_g, v_ada_w, v_ada_b, v_norm1_g, v_w_in, v_b_in, v_a_ln_g, v_a_ln_b, v_a_spatial_w, v_a_spatial_b, v_b_conv_w, v_b_conv_b, v_b_gn_g, v_b_gn_b, v_out_norm_a_g, v_out_norm_b_g, v_w_out, v_norm2_g, v_w_ffn_in, v_w_ffn_out, v_ada_f_w, v_ada_f_b, v_norm_f_g):
    T = x.shape[1]
    idx = 4 * lax.axis_index("x") + 2 * lax.axis_index("y") + lax.axis_index("c")
    x2d = x.reshape(T, D)
    tgt = loss_target.reshape(T, D)

    conv_s = jnp.pad(b_conv_w[0], ((0, HALO - KW), (0, 0)))
    call, cparts, cfparts, convg, (win_g,) = _gather(
        c, ada_w[0], ada_b.reshape(NDEV, -1), ada_f_w, ada_f_b.reshape(NDEV, -1), conv_s, [w_in[0].astype(BF16)])
    mod = jnp.concatenate([cparts.reshape(6, D), cfparts.reshape(2, D)], axis=0)
    cw = jnp.transpose(convg, (1, 0, 2)).reshape(HALO, DB)

    tril = jnp.tril(jnp.ones((CH, CH), dtype=bool))
    wsm = jnp.where(tril[None], a_spatial_w[0], 0.0).astype(BF16)
    wcat = wsm.reshape(NH * CH, CH)
    wcat_t = jnp.transpose(wsm, (0, 2, 1)).reshape(NH * CH, CH)
    bsf = jnp.repeat(a_spatial_b[0].T, DA // NH, axis=1)
    lane = jnp.arange(DB)
    pm = jnp.where((lane[:, None] >> 6) == (lane[None, :] >> 6), 1.0 / 64.0, 0.0).astype(BF16)
    esel = jnp.where((lane[:, None] >> 6) == jnp.arange(CH)[None, :], 1.0, 0.0).astype(BF16)

    tm = _token_tile(T, 256)
    tk = _token_tile(T, 1024)
    (x1, hb, zvg, mixed, yb, o, gu, dgelu_u, dgelu_v, vhat, rslb, yhat, rsg), (wfi_g, wfo_g), wout = _mix_fwd(
        x2d, mod, norm1_g, win_g, b_in, a_ln_g, a_ln_b, wcat, bsf, cw, b_conv_b, b_gn_g, b_gn_b, out_norm_a_g,
        out_norm_b_g, w_out[0].astype(BF16), pm, [w_ffn_in[0].T.astype(BF16), w_ffn_out[0].astype(BF16)],
        _token_tile(T, 512))
    dx1, h2b, dgu, act, dxg, acc_f = _ffn(x1, tgt, mod, norm2_g, norm_f_g.reshape(1, D),
                                          wfi_g.reshape(2 * DFF, D), wfo_g.reshape(DFF, D), tm)
    g_wfi = _wgrad_rows(dgu, h2b, 2 * WFI_B, tk, "wgrad_ffn_in").reshape(NDEV, WFI_B, D)
    g_wfo = _wgrad_rows(act, dxg, 2 * WFI_B, tk, "wgrad_ffn_out").reshape(NDEV, DFF // NDEV, D)
    f_sems, f_thru, f_land, f_token = _rs_start([g_wfi, g_wfo], "rs_ffn_start")
    (gx, acc_v, acc_b, acc_a, acc_bs, acc_ws, acc_cw), (g_win, g_wout) = _mix_bwd(
        dx1, x2d, zvg, mixed, o, hb, yb, gu, dgelu_u, dgelu_v, vhat, rslb, yhat, rsg, mod, norm1_g, win_g, a_ln_g,
        a_ln_b, wcat, wcat_t, cw, b_gn_g, b_gn_b, out_norm_a_g, out_norm_b_g, wout, pm, esel, f_token, tm)
    (g_wfi_d, g_wfo_d), (r_wfi, r_wfo) = _rs_wait(f_sems, f_thru, f_land, [acc_v], "rs_ffn_wait")
    g_wout = g_wout.reshape(NDEV, D // NDEV, D)

    vsum, dcond_all, wssum = _reduce_small(acc_f, acc_v, acc_b, acc_a, acc_bs, acc_cw, acc_ws, g_wfi_d)
    sems, g_thru, land_thru, token = _rs_start([g_win, g_wout], "rs_mix_start", after=(vsum,))

    own = lambda g: lax.dynamic_index_in_dim(g, idx, 0, keepdims=False)
    res = {}
    res["w_ffn_in"] = tuple(a.T for a in _adam_big(r_wfi, w_ffn_in[0].T, m_w_ffn_in[0].T, v_w_ffn_in[0].T, WFI_B // 4,
                                                   "adam_w_ffn_in", own=own(g_wfi_d), after=token))
    res["w_ffn_out"] = _adam_big(r_wfo, w_ffn_out[0], m_w_ffn_out[0], v_w_ffn_out[0], DFF // NDEV // 2,
                                 "adam_w_ffn_out", own=own(g_wfo_d), after=token)
    cact_t = (call * jax.nn.sigmoid(call)).T
    dcond = dcond_all.reshape(NDEV, 8 * D)
    nada = ada_w.shape[2]
    nadf = ada_f_w.shape[1]
    dcs = lax.dynamic_slice(dcond, (0, idx * nada), (NDEV, nada))
    dcfs = lax.dynamic_slice(dcond, (0, 6 * D + idx * nadf), (NDEV, nadf))
    res["ada_w"] = _adam_ada(cact_t, dcs, ada_w[0], m_ada_w[0], v_ada_w[0], 256, "adam_ada_w")
    res["ada_f_w"] = _adam_ada(cact_t, dcfs, ada_f_w, m_ada_f_w, v_ada_f_w, 256, "adam_ada_f_w")
    ncw = b_conv_w.shape[2]
    gcw = jnp.concatenate([lax.dynamic_slice(vsum, (_CW_ROW, idx * ncw), (HALO // 2, ncw)),
                           lax.dynamic_slice(vsum, (_CW_ROW, DB + idx * ncw), (HALO // 2, ncw))], axis=0)[:KW]
    two = lambda a: a.reshape(1, -1) if a.ndim == 1 else a.reshape(-1, a.shape[-1])
    small_in = {
        "ada_b": (ada_b, m_ada_b, v_ada_b), "ada_f_b": (ada_f_b, m_ada_f_b, v_ada_f_b),
        "norm1_g": (norm1_g, m_norm1_g, v_norm1_g), "b_in": (b_in, m_b_in, v_b_in),
        "a_ln_g": (a_ln_g, m_a_ln_g, v_a_ln_g), "a_ln_b": (a_ln_b, m_a_ln_b, v_a_ln_b),
        "a_spatial_b": (a_spatial_b.reshape(1, D), m_a_spatial_b.reshape(1, D), v_a_spatial_b.reshape(1, D)),
        "b_conv_b": (b_conv_b, m_b_conv_b, v_b_conv_b), "b_gn_g": (b_gn_g, m_b_gn_g, v_b_gn_g),
        "b_gn_b": (b_gn_b, m_b_gn_b, v_b_gn_b), "out_norm_a_g": (out_norm_a_g, m_out_norm_a_g, v_out_norm_a_g),
        "out_norm_b_g": (out_norm_b_g, m_out_norm_b_g, v_out_norm_b_g),
        "norm2_g": (norm2_g, m_norm2_g, v_norm2_g), "norm_f_g": (norm_f_g, m_norm_f_g, v_norm_f_g),
        "a_spatial_w": (a_spatial_w, m_a_spatial_w, v_a_spatial_w),
        "b_conv_w": (b_conv_w[0], m_b_conv_w[0], v_b_conv_w[0]),
    }
    small_in = {n: tuple(two(a) for a in t) for n, t in small_in.items()}
    res.update(_adam_small(vsum, wssum, gcw, small_in))
    (g_win_d, g_wout_d), (r_win, r_wout) = _rs_wait(
        sems, g_thru, land_thru,
        [res["w_ffn_in"][0], res["w_ffn_out"][0], res["ada_w"][0], res["ada_f_w"][0], res["norm_f_g"][0]],
        "rs_mix_wait")
    res["w_in"] = _adam_big(r_win, w_in[0], m_w_in[0], v_w_in[0], 256, "adam_w_in", own=own(g_win_d))
    res["w_out"] = _adam_big(r_wout, w_out[0], m_w_out[0], v_w_out[0], D // NDEV, "adam_w_out", own=own(g_wout_d))

    loss = 0.5 / D * jnp.sum(vsum[_LOSS_ROW])
    shapes = {"ada_w": ada_w, "ada_b": ada_b, "norm1_g": norm1_g, "w_in": w_in, "b_in": b_in, "a_ln_g": a_ln_g,
              "a_ln_b": a_ln_b, "a_spatial_w": a_spatial_w, "a_spatial_b": a_spatial_b, "b_conv_w": b_conv_w,
              "b_conv_b": b_conv_b, "b_gn_g": b_gn_g, "b_gn_b": b_gn_b, "out_norm_a_g": out_norm_a_g,
              "out_norm_b_g": out_norm_b_g, "w_out": w_out, "norm2_g": norm2_g, "w_ffn_in": w_ffn_in,
              "w_ffn_out": w_ffn_out, "ada_f_w": ada_f_w, "ada_f_b": ada_f_b, "norm_f_g": norm_f_g}
    order = list(shapes)
    outs = [loss, gx.reshape(x.shape)]
    for which in range(4):
        outs += [res[n][which].reshape(shapes[n].shape) for n in order]
    return tuple(outs)
```

```python
import math

import jax
import jax.numpy as jnp
from jax import lax
from jax.experimental import pallas as pl
from jax.experimental.pallas import tpu as pltpu

F32 = jnp.float32
BF16 = jnp.bfloat16

D = 1024
DA = 512
DB = 512
DIN = 2048
DFF = 2816
NH = 8
CH = 128
KW = 31
HALO = 32
NDEV = 8
WIN_B = DIN // NDEV
WFI_B = 2 * DFF // NDEV
EPS = 1e-6
NVEC = 40
VMEM_LIMIT = 56 * 1024 * 1024

ADAM_LR, ADAM_B1, ADAM_B2, ADAM_EPS, ADAM_WD, ADAM_STEP = 0.001, 0.9, 0.999, 1e-08, 0.01, 10

MESH = pl.DeviceIdType.MESH


def _dot(a, b):
    return jnp.dot(a, b, preferred_element_type=F32)


def _dot_nt(a, b):
    return lax.dot_general(a, b, (((1,), (1,)), ((), ())), preferred_element_type=F32)


def _dot_tn(a, b):
    return lax.dot_general(a, b, (((0,), (0,)), ((), ())), preferred_element_type=F32)


def _rs(v):
    return lax.rsqrt(jnp.mean(v * v, axis=-1, keepdims=True) + EPS)


def _sig(v):
    return 1.0 / (1.0 + jnp.exp(-v))


_INV_SQRT2 = 1.0 / math.sqrt(2.0)
_INV_SQRT2PI = 1.0 / math.sqrt(2.0 * math.pi)


def _gelu_parts(v):
    cdf = 0.5 * (1.0 + lax.erf(v * _INV_SQRT2))
    pdf = jnp.exp(-0.5 * v * v) * _INV_SQRT2PI
    return v * cdf, cdf + v * pdf


def _grp_mean(v, pm):
    hi = v.astype(BF16)
    lo = (v - hi.astype(F32)).astype(BF16)
    return _dot(hi, pm) + _dot(lo, pm)


def _colsum(v):
    return jnp.sum(v, axis=0, keepdims=True)


def _full(shape):
    nd = len(shape)
    return pl.BlockSpec(shape, lambda *_: (0,) * nd)


def _resident(shape):
    nd = len(shape)
    return pl.BlockSpec(shape, lambda *_: (0,) * nd, pipeline_mode=pl.Buffered(1))


HBM = pl.BlockSpec(memory_space=pl.ANY)
VM = pl.BlockSpec(memory_space=pltpu.VMEM)


SH_ROWS = HALO - 8


def _shifted_copies(buf, shbuf, tm):
    for b in range(1, 8):
        shbuf[b - 1] = buf[b:b + tm + SH_ROWS, :]


def _window(buf, shbuf, off, tm):
    a, b = divmod(off, 8)
    if b == 0:
        return buf[8 * a:8 * a + tm, :]
    return shbuf[b - 1, 8 * a:8 * a + tm, :]


def _first_head_lanes():
    return lax.broadcasted_iota(jnp.int32, (CH, CH), 1) < (DA // NH)


def _mix_heads(w_ref, vb, first):
    outs = []
    for p in range(NH // 2):
        v = vb[:, p * CH:(p + 1) * CH]
        a = _dot(w_ref[(2 * p) * CH:(2 * p + 1) * CH, :], v)
        b = _dot(w_ref[(2 * p + 1) * CH:(2 * p + 2) * CH, :], v)
        outs.append(jnp.where(first, a, b))
    return jnp.concatenate(outs, axis=1)


def _place():
    x, y, c = lax.axis_index("x"), lax.axis_index("y"), lax.axis_index("c")
    return x, y, c, 4 * x + 2 * y + c


def _dev(t):
    return (t >> 2, (t >> 1) & 1, t & 1)


class _AllGather:
    def __init__(self, w_in, w_out, wss, wrs, lsem, slot=lambda ref, b: ref.at[b]):
        x, y, c, idx = _place()
        me, sibling = (x, y, c), (x, y, 1 - c)
        chips = [(1 - x, y), (x, 1 - y), (1 - x, 1 - y)]
        nw = len(w_in)

        def blk(p):
            return 4 * p[0] + 2 * p[1] + p[2]

        def wcopy(a, k, block, to, src=None):
            dst = slot(w_out[a], blk(block))
            return pltpu.make_async_remote_copy(src_ref=dst if src is None else src, dst_ref=dst,
                                                send_sem=wss.at[a, k], recv_sem=wrs.at[a, k],
                                                device_id=to, device_id_type=MESH)

        self.mine = [pltpu.make_async_copy(w_in[a], slot(w_out[a], idx), lsem.at[a]) for a in range(nw)]
        self.first = []
        for a in range(nw):
            self.first.append(wcopy(a, 0, me, sibling, src=w_in[a]))
            self.first += [wcopy(a, 1 + j, me, (*chip, c), src=w_in[a]) for j, chip in enumerate(chips)]
        self.landed = [[wcopy(a, 1 + j, (*chip, c), me) for a in range(nw)] for j, chip in enumerate(chips)]
        self.passed = [[wcopy(a, 4 + j, (*chip, c), sibling) for a in range(nw)] for j, chip in enumerate(chips)]
        self.from_sibling = []
        for a in range(nw):
            self.from_sibling.append(wcopy(a, 0, sibling, me))
            self.from_sibling += [wcopy(a, 4 + j, (*chip, 1 - c), me) for j, chip in enumerate(chips)]

    def start(self):
        for cp in self.mine + self.first:
            cp.start()

    def forward(self):
        for land, pas in zip(self.landed, self.passed):
            for l, p in zip(land, pas):
                l.wait_recv()
                p.start()

    def finish(self):
        for cp in self.from_sibling:
            cp.wait_recv()
        for cp in self.first:
            cp.wait_send()
        for pas in self.passed:
            for p in pas:
                p.wait_send()
        for cp in self.mine:
            cp.wait()


AG_SEMS = lambda nw: [pltpu.SemaphoreType.DMA((nw, 7)), pltpu.SemaphoreType.DMA((nw, 7)),
                      pltpu.SemaphoreType.DMA((nw,))]


class _ReduceScatter:
    def __init__(self, g_in, r_out, gss, grs, lsem):
        x, y, c, idx = _place()
        me = (x, y, c)
        nw = len(g_in)
        self.mine = [pltpu.make_async_copy(g_in[a].at[idx], r_out[a].at[0], lsem.at[a]) for a in range(nw)]
        self.sends, self.recvs = [], []
        for k in range(1, NDEV):
            t = idx ^ k
            for a in range(nw):
                self.sends.append(pltpu.make_async_remote_copy(
                    src_ref=g_in[a].at[t], dst_ref=r_out[a].at[k], send_sem=gss.at[a, k - 1],
                    recv_sem=grs.at[a, k - 1], device_id=_dev(t), device_id_type=MESH))
                self.recvs.append(pltpu.make_async_remote_copy(
                    src_ref=g_in[a].at[0], dst_ref=r_out[a].at[k], send_sem=gss.at[a, k - 1],
                    recv_sem=grs.at[a, k - 1], device_id=me, device_id_type=MESH))

    def start(self):
        for cp in self.mine + self.sends:
            cp.start()

    def finish(self):
        for cp in self.recvs:
            cp.wait_recv()
        for cp in self.sends:
            cp.wait_send()
        for cp in self.mine:
            cp.wait()


RS_SEMS = AG_SEMS


def _wout_rows(ref, b):
    rows = D // NDEV
    return ref.at[pl.ds(pl.multiple_of(b * rows, rows), rows)]


def _mix_fwd(x, mod, g1, win, b_in, lng, lnb, wcat, bsf, cw, cb, gng, gnb, oga, ogb, wout_s, pm, ffn_shards, tm):
    T = x.shape[0]
    nt = T // tm
    nch = tm // CH
    nw = len(ffn_shards)
    fwd_step = (5 * nt) // 8
    saved = [(D, F32), (D, BF16), (2 * DB, F32), (DA, F32), (D, BF16), (D, F32), (DA, F32), (DA, F32), (DA, F32),
             (DA, F32), (CH, F32), (DB, F32), (DB, F32)]
    NSAVE = len(saved)

    def body(x_ref, mod_ref, g1_ref, win_ref, bin_ref, lng_ref, lnb_ref, wcat_ref, bsf_ref, cw_ref, cb_ref,
             gng_ref, gnb_ref, oga_ref, ogb_ref, wouts_ref, pm_ref, *rest):
        sh_in = rest[:nw]
        (x1_ref, h_ref, zvg_ref, mixed_ref, y_ref, o_ref, gu_ref, dgu_ref, dgv_ref, vhat_ref, rsl_ref, yhat_ref,
         rsg_ref) = rest[nw:nw + NSAVE]
        sh_out = rest[nw + NSAVE:2 * nw + NSAVE]
        woutg_ref = rest[2 * nw + NSAVE]
        glbuf, shbuf, wout_v, wss, wrs, lsem, oss, ors, olsem = rest[2 * nw + NSAVE + 1:]
        i = pl.program_id(0)
        gather_wout = lambda: _AllGather([wouts_ref], [wout_v], oss, ors, olsem, slot=_wout_rows)

        @pl.when(i == 0)
        def _():
            gather_wout().start()

        xv = x_ref[...]
        shift1 = mod_ref[0:1, :]
        scale1 = mod_ref[1:2, :]
        gate1 = mod_ref[2:3, :]
        h = (xv * _rs(xv) * g1_ref[...]) * (1.0 + scale1) + shift1
        hb = h.astype(BF16)
        h_ref[...] = hb
        z = jnp.concatenate([_dot(hb, win_ref[j]) for j in range(NDEV)], axis=1) + bin_ref[...]
        zvg_ref[...] = z[:, 2 * DA:]
        gu, dgelu_u = _gelu_parts(z[:, 0:DA])
        gv, dgelu_v = _gelu_parts(z[:, DA:2 * DA])
        gu_ref[...] = gu
        dgu_ref[...] = dgelu_u
        dgv_ref[...] = dgelu_v
        xc = gv - jnp.mean(gv, axis=-1, keepdims=True)
        rsl = lax.rsqrt(jnp.mean(xc * xc, axis=-1, keepdims=True) + EPS)
        vhat = xc * rsl
        vhat_ref[...] = vhat
        rsl_ref[...] = jnp.broadcast_to(rsl, (tm, CH))
        vnb = (vhat * lng_ref[...] + lnb_ref[...]).astype(BF16)
        first = _first_head_lanes()
        chunks = []
        for ci in range(nch):
            chunks.append(_mix_heads(wcat_ref, vnb[ci * CH:(ci + 1) * CH, :], first) + bsf_ref[...])
        mixed = jnp.concatenate(chunks, axis=0) if nch > 1 else chunks[0]
        mixed_ref[...] = mixed
        ya = gu * mixed
        gl = z[:, 2 * DA:2 * DA + DB] * _sig(z[:, 2 * DA + DB:])

        @pl.when(i == 0)
        def _():
            glbuf[0:HALO, :] = jnp.zeros((HALO, DB), F32)

        glbuf[HALO:HALO + tm, :] = gl
        _shifted_copies(glbuf, shbuf, tm)
        yc = jnp.zeros((tm, DB), F32) + cb_ref[...]
        for k in range(KW):
            yc = yc + cw_ref[k:k + 1, :] * _window(glbuf, shbuf, HALO - (KW - 1) + k, tm)
        glbuf[0:HALO, :] = gl[tm - HALO:, :]
        pmv = pm_ref[...]
        dc = yc - _grp_mean(yc, pmv)
        rsg = lax.rsqrt(_grp_mean(dc * dc, pmv) + EPS)
        yhat = dc * rsg
        yhat_ref[...] = yhat
        rsg_ref[...] = rsg
        yg = yhat * gng_ref[...] + gnb_ref[...]
        yb = yg * _sig(yg)
        na = ya * _rs(ya) * oga_ref[...]
        nb = yb * _rs(yb) * ogb_ref[...]
        yv = jnp.concatenate([na, nb], axis=1).astype(BF16)
        y_ref[...] = yv

        @pl.when(i == 0)
        def _():
            ag = gather_wout()
            ag.forward()
            ag.finish()
            _AllGather(sh_in, sh_out, wss, wrs, lsem).start()

        o = _dot(yv, wout_v[...])
        o_ref[...] = o
        x1_ref[...] = xv + gate1 * o

        @pl.when(i == fwd_step)
        def _():
            _AllGather(sh_in, sh_out, wss, wrs, lsem).forward()

        @pl.when(i == nt - 1)
        def _():
            _AllGather(sh_in, sh_out, wss, wrs, lsem).finish()
            pltpu.sync_copy(wout_v, woutg_ref)

    tile = lambda w: pl.BlockSpec((tm, w), lambda i: (i, 0))
    outs = pl.pallas_call(
        body,
        name="mix_fwd",
        grid=(nt,),
        in_specs=[tile(D), _full((8, D)), _full((1, D)), _resident((NDEV, D, WIN_B)), _full((1, DIN)),
                  _full((1, DA)), _full((1, DA)), _full((NH * CH, CH)), _full((CH, DA)), _full((HALO, DB)),
                  _full((1, DB)), _full((1, DB)), _full((1, DB)), _full((1, DA)), _full((1, DB)),
                  _full((D // NDEV, D)), _full((DB, DB))] + [HBM] * nw,
        out_specs=[tile(w) for w, _ in saved] + [HBM] * (nw + 1),
        out_shape=[jax.ShapeDtypeStruct((T, w), dt) for w, dt in saved]
                  + [jax.ShapeDtypeStruct((NDEV,) + s.shape, s.dtype) for s in ffn_shards]
                  + [jax.ShapeDtypeStruct((D, D), BF16)],
        scratch_shapes=[pltpu.VMEM((HALO + tm, DB), F32), pltpu.VMEM((7, tm + SH_ROWS, DB), F32),
                        pltpu.VMEM((D, D), BF16)] + AG_SEMS(nw) + AG_SEMS(1),
        compiler_params=pltpu.CompilerParams(dimension_semantics=("arbitrary",), vmem_limit_bytes=VMEM_LIMIT),
    )(x, mod, g1, win, b_in, lng, lnb, wcat, bsf, cw, cb, gng, gnb, oga, ogb, wout_s, pm, *ffn_shards)
    return outs[:NSAVE], outs[NSAVE:NSAVE + nw], outs[NSAVE + nw]


FF_BLOCKS = ((0, 1024), (1024, 1024), (2048, 768))


def _ffn(x1, tgt, mod, g2, gf, wfi_t, wfo, tm):
    T = x1.shape[0]
    nt = T // tm

    def body(x1_ref, tgt_ref, mod_ref, g2_ref, gf_ref, wfi_ref, wfo_ref,
             dx1_ref, h2_ref, dgu_ref, act_ref, dxg_ref, acc_ref, g_s, u_s):
        i = pl.program_id(0)

        @pl.when(i == 0)
        def _():
            acc_ref[...] = jnp.zeros((8, D), F32)

        x1 = x1_ref[...]
        shift2 = mod_ref[3:4, :]
        scale2 = mod_ref[4:5, :]
        gate2 = mod_ref[5:6, :]
        shiftf = mod_ref[6:7, :]
        scalef = mod_ref[7:8, :]
        g2v = g2_ref[...]
        gfv = gf_ref[...]
        r2 = _rs(x1)
        xn2 = x1 * r2
        h2b = (xn2 * g2v * (1.0 + scale2) + shift2).astype(BF16)
        h2_ref[...] = h2b
        f = jnp.zeros((tm, D), F32)
        for o, w in FF_BLOCKS:
            g = _dot_nt(h2b, wfi_ref[o:o + w, :])
            u = _dot_nt(h2b, wfi_ref[DFF + o:DFF + o + w, :])
            g_s[:, o:o + w] = g
            u_s[:, o:o + w] = u
            actb = (g * _sig(g) * u).astype(BF16)
            act_ref[:, o:o + w] = actb
            f = f + _dot(actb, wfo_ref[o:o + w, :])
        x2 = x1 + gate2 * f
        rf = _rs(x2)
        xnf = x2 * rf
        out = xnf * gfv * (1.0 + scalef) + shiftf
        e = out - tgt_ref[...]
        dout = e * (1.0 / D)
        acc_ref[7:8, :] += _colsum(e * e)
        acc_ref[0:1, :] += _colsum(dout)
        acc_ref[1:2, :] += _colsum(dout * xnf * gfv)
        acc_ref[2:3, :] += _colsum(dout * (1.0 + scalef) * xnf)
        dxnf = dout * (1.0 + scalef) * gfv
        dx2 = rf * (dxnf - xnf * jnp.mean(dxnf * xnf, axis=-1, keepdims=True))
        acc_ref[3:4, :] += _colsum(dx2 * f)
        dxgb = (dx2 * gate2).astype(BF16)
        dxg_ref[...] = dxgb
        dh2 = jnp.zeros((tm, D), F32)
        for o, w in FF_BLOCKS:
            dact = _dot_nt(dxgb, wfo_ref[o:o + w, :])
            g = g_s[:, o:o + w]
            u = u_s[:, o:o + w]
            s = _sig(g)
            dgb = (dact * u * (s * (1.0 + g * (1.0 - s)))).astype(BF16)
            dub = (dact * (g * s)).astype(BF16)
            dgu_ref[:, o:o + w] = dgb
            dgu_ref[:, DFF + o:DFF + o + w] = dub
            dh2 = dh2 + _dot(dgb, wfi_ref[o:o + w, :])
            dh2 = dh2 + _dot(dub, wfi_ref[DFF + o:DFF + o + w, :])
        acc_ref[4:5, :] += _colsum(dh2)
        acc_ref[5:6, :] += _colsum(dh2 * xn2 * g2v)
        acc_ref[6:7, :] += _colsum(dh2 * (1.0 + scale2) * xn2)
        dxn2 = dh2 * (1.0 + scale2) * g2v
        dx1_ref[...] = dx2 + r2 * (dxn2 - xn2 * jnp.mean(dxn2 * xn2, axis=-1, keepdims=True))

    tile = lambda w: pl.BlockSpec((tm, w), lambda i: (i, 0))
    return pl.pallas_call(
        body,
        name="ffn_fwd_bwd",
        grid=(nt,),
        in_specs=[tile(D), tile(D), _full((8, D)), _full((1, D)), _full((1, D)),
                  _resident((2 * DFF, D)), _resident((DFF, D))],
        out_specs=[tile(D), tile(D), tile(2 * DFF), tile(DFF), tile(D), _full((8, D))],
        out_shape=[jax.ShapeDtypeStruct((T, D), F32), jax.ShapeDtypeStruct((T, D), BF16),
                   jax.ShapeDtypeStruct((T, 2 * DFF), BF16), jax.ShapeDtypeStruct((T, DFF), BF16),
                   jax.ShapeDtypeStruct((T, D), BF16), jax.ShapeDtypeStruct((8, D), F32)],
        scratch_shapes=[pltpu.VMEM((tm, DFF), F32), pltpu.VMEM((tm, DFF), F32)],
        compiler_params=pltpu.CompilerParams(dimension_semantics=("arbitrary",), vmem_limit_bytes=VMEM_LIMIT),
    )(x1, tgt, mod, g2, gf, wfi_t, wfo)


def _mix_bwd(dx1, x, zvg, mixed, o, hb, yb, gu, dgu, dgv, vhat, rslb, yhat, rsg, mod, g1, win, lng, lnb, wcat, wcat_t,
             cw, gng, gnb, oga, ogb, wout, pm, esel, after, tm):
    T = x.shape[0]
    nt = T // tm
    nch = tm // CH
    WOB = 256

    def body(dx1_ref, x_ref, zvg_ref, mixed_ref, o_ref, hb_ref, yb_ref, gu_ref, dgu_ref, dgv_ref, vhat_ref, rsl_ref,
             yhat_ref, rsg_ref, mod_ref, g1_ref, win_ref, lng_ref, lnb_ref, wcat_ref, wcatt_ref, cw_ref, gng_ref,
             gnb_ref, oga_ref, ogb_ref, wout_ref, pm_ref, esel_ref, after_ref,
             gx_ref, accv_ref, accb_ref, acca_ref, accbs_ref, accws_ref, acccw_ref, gwin_ref, gwout_ref,
             dycbuf, shbuf, bs_s, acc_win, acc_wout, st_win, st_wout):
        i = pl.program_id(0)

        @pl.when(i == 0)
        def _():
            acc_win[...] = jnp.zeros((NDEV, D, WIN_B), F32)
            acc_wout[...] = jnp.zeros((D, D), F32)
            accv_ref[...] = jnp.zeros((8, D), F32)
            accb_ref[...] = jnp.zeros((1, DIN), F32)
            acca_ref[...] = jnp.zeros((8, DA), F32)
            accws_ref[...] = jnp.zeros((NH * CH, CH), F32)
            acccw_ref[...] = jnp.zeros((HALO, DB), F32)
            bs_s[...] = jnp.zeros((CH, DA), F32)
            dycbuf[tm:tm + HALO, :] = jnp.zeros((HALO, DB), F32)

        shift1 = mod_ref[0:1, :]
        scale1 = mod_ref[1:2, :]
        gate1 = mod_ref[2:3, :]
        g1v = g1_ref[...]
        xv = x_ref[...]
        r1 = _rs(xv)
        xn1 = xv * r1
        val = zvg_ref[:, 0:DB]
        gate = zvg_ref[:, DB:]
        gu = gu_ref[...]
        dgelu_u = dgu_ref[...]
        dgelu_v = dgv_ref[...]
        vhat = vhat_ref[...]
        rsl = rsl_ref[:, 0:1]
        lngv = lng_ref[...]
        vnb = (vhat * lngv + lnb_ref[...]).astype(BF16)
        mixed = mixed_ref[...]
        ya = gu * mixed
        ra = _rs(ya)
        yan = ya * ra
        sgt = _sig(gate)
        gl = val * sgt
        pmv = pm_ref[...]
        rsg = rsg_ref[...]
        yhat = yhat_ref[...]
        gngv = gng_ref[...]
        yg = yhat * gngv + gnb_ref[...]
        sgy = _sig(yg)
        yb = yg * sgy
        rb = _rs(yb)
        ybn = yb * rb
        dx1 = dx1_ref[...]
        accv_ref[0:1, :] += _colsum(dx1 * o_ref[...])
        dogb = (dx1 * gate1).astype(BF16)
        acc_wout[...] += _dot_tn(yb_ref[...], dogb)
        dy = _dot_nt(dogb, wout_ref[...])
        dna = dy[:, 0:DA]
        dnb = dy[:, DA:]
        ogav = oga_ref[...]
        ogbv = ogb_ref[...]
        acca_ref[2:3, :] += _colsum(dna * yan)
        acca_ref[3:4, :] += _colsum(dnb * ybn)
        ta = dna * ogav
        dya = ra * (ta - yan * jnp.mean(ta * yan, axis=-1, keepdims=True))
        tb = dnb * ogbv
        dyb = rb * (tb - ybn * jnp.mean(tb * ybn, axis=-1, keepdims=True))
        dgu = dya * mixed
        dm = dya * gu
        first = _first_head_lanes()
        zero = jnp.zeros((CH, CH), BF16)
        dvn_chunks = []
        bs_acc = bs_s[...]
        for ci in range(nch):
            dmc = dm[ci * CH:(ci + 1) * CH, :]
            bs_acc = bs_acc + dmc
            dmcb = dmc.astype(BF16)
            dvn_chunks.append(_mix_heads(wcatt_ref, dmcb, first))
            vc = vnb[ci * CH:(ci + 1) * CH, :]
            for p in range(NH // 2):
                xt = dmcb[:, p * CH:(p + 1) * CH]
                vt = vc[:, p * CH:(p + 1) * CH]
                accws_ref[(2 * p) * CH:(2 * p + 1) * CH, :] += _dot_nt(jnp.where(first, xt, zero), vt)
                accws_ref[(2 * p + 1) * CH:(2 * p + 2) * CH, :] += _dot_nt(jnp.where(first, zero, xt), vt)
        bs_s[...] = bs_acc
        dvn = jnp.concatenate(dvn_chunks, axis=0) if nch > 1 else dvn_chunks[0]
        acca_ref[0:1, :] += _colsum(dvn * vhat)
        acca_ref[1:2, :] += _colsum(dvn)
        dvh = dvn * lngv
        dgv = rsl * (dvh - jnp.mean(dvh, axis=-1, keepdims=True)
                     - vhat * jnp.mean(dvh * vhat, axis=-1, keepdims=True))
        du = dgu * dgelu_u
        dv = dgv * dgelu_v
        dyg = dyb * (sgy * (1.0 + yg * (1.0 - sgy)))
        acca_ref[5:6, :] += _colsum(dyg * yhat)
        acca_ref[6:7, :] += _colsum(dyg)
        dyh = dyg * gngv
        dyc = rsg * (dyh - _grp_mean(dyh, pmv) - yhat * _grp_mean(dyh * yhat, pmv))
        acca_ref[4:5, :] += _colsum(dyc)
        dycbuf[0:tm, :] = dyc
        _shifted_copies(dycbuf, shbuf, tm)
        dgl = jnp.zeros((tm, DB), F32)
        for k in range(KW):
            win_k = _window(dycbuf, shbuf, KW - 1 - k, tm)
            dgl = dgl + cw_ref[k:k + 1, :] * win_k
            acccw_ref[k:k + 1, :] += _colsum(win_k * gl)
        dycbuf[tm:tm + HALO, :] = dyc[0:HALO, :]
        dval = dgl * sgt
        dgate = dgl * val * sgt * (1.0 - sgt)
        dz = jnp.concatenate([du, dv, dval, dgate], axis=1)
        accb_ref[...] += _colsum(dz)
        dzb = dz.astype(BF16)
        hbv = hb_ref[...]
        dh = jnp.zeros((tm, D), F32)
        for j in range(NDEV):
            dzj = dzb[:, j * WIN_B:(j + 1) * WIN_B]
            acc_win[j] += _dot_tn(hbv, dzj)
            dh = dh + _dot_nt(dzj, win_ref[j])
        accv_ref[1:2, :] += _colsum(dh)
        dh_xn = _colsum(dh * xn1)
        accv_ref[2:3, :] += dh_xn * g1v
        accv_ref[3:4, :] += dh_xn * (1.0 + scale1)
        dxn1 = dh * (1.0 + scale1) * g1v
        gx_ref[...] = dx1 + r1 * (dxn1 - xn1 * jnp.mean(dxn1 * xn1, axis=-1, keepdims=True))

        @pl.when(i == nt - 1)
        def _():
            rows = lax.broadcasted_iota(jnp.int32, (NH * CH, CH), 0) & (CH - 1)
            cols = lax.broadcasted_iota(jnp.int32, (NH * CH, CH), 1)
            accws_ref[...] = jnp.where(cols <= rows, accws_ref[...], 0.0)
            bs = bs_s[...]
            hi = bs.astype(BF16)
            r1_ = bs - hi.astype(F32)
            mid = r1_.astype(BF16)
            lo = (r1_ - mid.astype(F32)).astype(BF16)
            ev = esel_ref[...]
            accbs_ref[...] = _dot(hi, ev) + _dot(mid, ev) + _dot(lo, ev)
            for j in range(NDEV):
                st_win[...] = acc_win[j].astype(BF16)
                pltpu.sync_copy(st_win, gwin_ref.at[j])
            for j in range(D // WOB):
                st_wout[...] = acc_wout[j * WOB:(j + 1) * WOB, :].astype(BF16)
                pltpu.sync_copy(st_wout, gwout_ref.at[pl.ds(j * WOB, WOB)])

    rev = lambda w: pl.BlockSpec((tm, w), lambda i: (nt - 1 - i, 0))
    outs = pl.pallas_call(
        body,
        name="mix_bwd",
        grid=(nt,),
        in_specs=[rev(D), rev(D), rev(2 * DB), rev(DA), rev(D), rev(D), rev(D), rev(DA), rev(DA), rev(DA), rev(DA),
                  rev(CH), rev(DB), rev(DB), _full((8, D)), _full((1, D)),
                  _resident((NDEV, D, WIN_B)), _full((1, DA)), _full((1, DA)), _full((NH * CH, CH)),
                  _full((NH * CH, CH)), _full((HALO, DB)), _full((1, DB)), _full((1, DB)), _full((1, DA)),
                  _full((1, DB)), _resident((D, D)), _full((DB, DB)), _full((DA, CH)), HBM],
        out_specs=[rev(D), _full((8, D)), _full((1, DIN)), _full((8, DA)), _full((CH, CH)),
                   _full((NH * CH, CH)), _full((HALO, DB)), HBM, HBM],
        out_shape=[jax.ShapeDtypeStruct((T, D), F32), jax.ShapeDtypeStruct((8, D), F32),
                   jax.ShapeDtypeStruct((1, DIN), F32), jax.ShapeDtypeStruct((8, DA), F32),
                   jax.ShapeDtypeStruct((CH, CH), F32), jax.ShapeDtypeStruct((NH * CH, CH), F32),
                   jax.ShapeDtypeStruct((HALO, DB), F32),
                   jax.ShapeDtypeStruct((NDEV, D, WIN_B), BF16), jax.ShapeDtypeStruct((D, D), BF16)],
        scratch_shapes=[pltpu.VMEM((tm + HALO, DB), F32), pltpu.VMEM((7, tm + SH_ROWS, DB), F32),
                        pltpu.VMEM((CH, DA), F32), pltpu.VMEM((NDEV, D, WIN_B), F32), pltpu.VMEM((D, D), F32),
                        pltpu.VMEM((D, WIN_B), BF16), pltpu.VMEM((WOB, D), BF16)],
        compiler_params=pltpu.CompilerParams(dimension_semantics=("arbitrary",), vmem_limit_bytes=VMEM_LIMIT),
    )(dx1, x, zvg, mixed, o, hb, yb, gu, dgu, dgv, vhat, rslb, yhat, rsg, mod, g1, win, lng, lnb, wcat, wcat_t, cw,
      gng, gnb, oga, ogb, wout, pm, esel, after)
    return outs[:7], outs[7:]


def _wgrad_rows(a, b, bm, tk, name):
    T, M = a.shape
    N = b.shape[1]
    nk = T // tk

    def body(a_ref, b_ref, o_ref, acc):
        k = pl.program_id(1)

        @pl.when(k == 0)
        def _():
            acc[...] = jnp.zeros((bm, N), F32)

        acc[...] += _dot_tn(a_ref[...], b_ref[...])

        @pl.when(k == nk - 1)
        def _():
            o_ref[...] = acc[...].astype(BF16)

    return pl.pallas_call(
        body, name=name, grid=(M // bm, nk),
        in_specs=[pl.BlockSpec((tk, bm), lambda j, k: (k, j)), pl.BlockSpec((tk, N), lambda j, k: (k, 0))],
        out_specs=pl.BlockSpec((bm, N), lambda j, k: (j, 0)),
        out_shape=jax.ShapeDtypeStruct((M, N), BF16),
        scratch_shapes=[pltpu.VMEM((bm, N), F32)],
        compiler_params=pltpu.CompilerParams(dimension_semantics=("arbitrary", "arbitrary"),
                                             vmem_limit_bytes=VMEM_LIMIT),
    )(a, b)


def _small_copy(src, dst, ss, rs, k, to):
    return pltpu.make_async_remote_copy(src_ref=src, dst_ref=dst, send_sem=ss.at[k], recv_sem=rs.at[k],
                                        device_id=to, device_id_type=MESH)


def _gather(c_row, ada_w, ada_b8, ada_f_w, ada_f_b8, conv_s, shards):
    nw = len(shards)

    def body(c_ref, adaw_ref, adab_ref, adafw_ref, adafb_ref, conv_ref, *rest):
        w_in = rest[:nw]
        call_ref, cparts_ref, cfparts_ref, convg_ref = rest[nw:nw + 4]
        w_out = rest[nw + 4:2 * nw + 4]
        part_s, partf_s, wss, wrs, lsem, s1, r1, s2, r2, s3, r3, s4, r4 = rest[2 * nw + 4:]
        x, y, c, idx = _place()
        me = (x, y, c)
        ag = _AllGather(w_in, w_out, wss, wrs, lsem)
        ag.start()
        call_ref[pl.ds(idx, 1), :] = c_ref[...]
        convg_ref[idx] = conv_ref[...]
        ph1 = []
        for k in range(1, NDEV):
            to = _dev(idx ^ k)
            ph1.append(_small_copy(c_ref, call_ref.at[pl.ds(idx, 1)], s1, r1, k - 1, to))
            ph1.append(_small_copy(conv_ref, convg_ref.at[idx], s2, r2, k - 1, to))
        for cp in ph1:
            cp.start()
        for k in range(1, NDEV):
            src_dev = idx ^ k
            _small_copy(c_ref, call_ref.at[pl.ds(src_dev, 1)], s1, r1, k - 1, me).wait_recv()
            _small_copy(conv_ref, convg_ref.at[src_dev], s2, r2, k - 1, me).wait_recv()
        call = call_ref[...]
        cact = (call * _sig(call))
        part_s[...] = jnp.dot(cact, adaw_ref[...], preferred_element_type=F32,
                              precision=lax.Precision.HIGHEST) + adab_ref[pl.ds(idx, 1), :]
        partf_s[...] = jnp.dot(cact, adafw_ref[...], preferred_element_type=F32,
                               precision=lax.Precision.HIGHEST) + adafb_ref[pl.ds(idx, 1), :]
        cparts_ref[pl.ds(idx, 1), :] = part_s[pl.ds(idx, 1), :]
        cfparts_ref[pl.ds(idx, 1), :] = partf_s[pl.ds(idx, 1), :]
        ph2 = []
        for k in range(1, NDEV):
            t = idx ^ k
            ph2.append(_small_copy(part_s.at[pl.ds(t, 1)], cparts_ref.at[pl.ds(idx, 1)], s3, r3, k - 1, _dev(t)))
            ph2.append(_small_copy(partf_s.at[pl.ds(t, 1)], cfparts_ref.at[pl.ds(idx, 1)], s4, r4, k - 1, _dev(t)))
        for cp in ph2:
            cp.start()
        for k in range(1, NDEV):
            src_dev = idx ^ k
            _small_copy(part_s.at[pl.ds(0, 1)], cparts_ref.at[pl.ds(src_dev, 1)], s3, r3, k - 1, me).wait_recv()
            _small_copy(partf_s.at[pl.ds(0, 1)], cfparts_ref.at[pl.ds(src_dev, 1)], s4, r4, k - 1, me).wait_recv()
        for cp in ph1 + ph2:
            cp.wait_send()
        ag.forward()
        ag.finish()

    dma7 = pltpu.SemaphoreType.DMA((NDEV - 1,))
    outs = pl.pallas_call(
        body,
        name="gather_weights",
        in_specs=[VM] * 6 + [HBM] * nw,
        out_specs=[VM] * 4 + [HBM] * nw,
        out_shape=[jax.ShapeDtypeStruct((NDEV, D), F32), jax.ShapeDtypeStruct((NDEV, ada_w.shape[1]), F32),
                   jax.ShapeDtypeStruct((NDEV, ada_f_w.shape[1]), F32),
                   jax.ShapeDtypeStruct((NDEV,) + conv_s.shape, F32)]
                  + [jax.ShapeDtypeStruct((NDEV,) + s.shape, s.dtype) for s in shards],
        scratch_shapes=[pltpu.VMEM((NDEV, ada_w.shape[1]), F32), pltpu.VMEM((NDEV, ada_f_w.shape[1]), F32)]
                       + AG_SEMS(nw) + [dma7] * 8,
        compiler_params=pltpu.CompilerParams(vmem_limit_bytes=VMEM_LIMIT),
    )(c_row, ada_w, ada_b8, ada_f_w, ada_f_b8, conv_s, *shards)
    return outs[0], outs[1], outs[2], outs[3], outs[4:]


_VEC_AT = {
    "norm1_g": (8, 0, D), "a_ln_g": (11, 0, DA), "a_ln_b": (11, DA, DA), "a_spatial_b": (12, 0, D),
    "b_conv_b": (13, 0, DB), "b_gn_g": (13, DB, DB), "b_gn_b": (14, 0, DB), "out_norm_a_g": (14, DB, DA),
    "out_norm_b_g": (15, 0, DB), "norm2_g": (16, 0, D), "norm_f_g": (17, 0, D),
}
_LOSS_ROW = 18
_CW_ROW = 24


def _reduce_small(acc_f, acc_v, acc_b, acc_a, acc_bs, acc_cw, dws, after):
    def body(accf_ref, accv_ref, accb_ref, acca_ref, accbs_ref, acccw_ref, dws_ref, after_ref,
             vsum_ref, dcond_ref, wssum_ref, vloc, vbuf, wbuf, wown, s1, r1, s2, r2, s3, r3):
        x, y, c, idx = _place()
        me = (x, y, c)
        vloc[...] = jnp.zeros((NVEC, D), F32)
        vloc[0:1, :] = accv_ref[1:2, :]
        vloc[1:2, :] = accv_ref[2:3, :]
        vloc[2:3, :] = accv_ref[0:1, :]
        vloc[3:4, :] = accf_ref[4:5, :]
        vloc[4:5, :] = accf_ref[5:6, :]
        vloc[5:6, :] = accf_ref[3:4, :]
        vloc[6:7, :] = accf_ref[0:1, :]
        vloc[7:8, :] = accf_ref[1:2, :]
        vloc[8:9, :] = accv_ref[3:4, :]
        vloc[9:10, :] = accb_ref[:, 0:D]
        vloc[10:11, :] = accb_ref[:, D:]
        vloc[11:12, 0:DA] = acca_ref[0:1, :]
        vloc[11:12, DA:] = acca_ref[1:2, :]
        bst = accbs_ref[...].T
        for h in range(NH):
            vloc[12:13, h * CH:(h + 1) * CH] = bst[h:h + 1, :]
        vloc[13:14, 0:DB] = acca_ref[4:5, :]
        vloc[13:14, DB:] = acca_ref[5:6, :]
        vloc[14:15, 0:DB] = acca_ref[6:7, :]
        vloc[14:15, DB:] = acca_ref[2:3, :]
        vloc[15:16, 0:DB] = acca_ref[3:4, :]
        vloc[16:17, :] = accf_ref[6:7, :]
        vloc[17:18, :] = accf_ref[2:3, :]
        vloc[_LOSS_ROW:_LOSS_ROW + 1, :] = accf_ref[7:8, :]
        vloc[_CW_ROW:_CW_ROW + HALO // 2, 0:DB] = acccw_ref[0:HALO // 2, :]
        vloc[_CW_ROW:_CW_ROW + HALO // 2, DB:] = acccw_ref[HALO // 2:, :]
        vbuf[idx] = vloc[...]
        rows_of = lambda t: pl.ds(pl.multiple_of(t * CH, CH), CH)
        wbuf[0] = dws_ref[rows_of(idx), :]
        sm = []
        for k in range(1, NDEV):
            t = idx ^ k
            sm.append(_small_copy(vloc, vbuf.at[idx], s1, r1, k - 1, _dev(t)))
            sm.append(_small_copy(dws_ref.at[rows_of(t)], wbuf.at[k], s2, r2, k - 1, _dev(t)))
        for cp in sm:
            cp.start()
        for k in range(1, NDEV):
            _small_copy(dws_ref.at[rows_of(0)], wbuf.at[k], s2, r2, k - 1, me).wait_recv()
        ws = wbuf[0]
        for k in range(1, NDEV):
            ws = ws + wbuf[k]
        wown[...] = ws
        wssum_ref[rows_of(idx), :] = ws
        ag = [_small_copy(wown, wssum_ref.at[rows_of(idx)], s3, r3, k - 1, _dev(idx ^ k)) for k in range(1, NDEV)]
        for cp in ag:
            cp.start()
        for k in range(1, NDEV):
            _small_copy(vloc, vbuf.at[idx ^ k], s1, r1, k - 1, me).wait_recv()
        vs = vbuf[0]
        for d in range(1, NDEV):
            vs = vs + vbuf[d]
        vsum_ref[...] = vs
        for d in range(NDEV):
            dcond_ref[d] = vbuf[d, 0:8, :]
        for k in range(1, NDEV):
            _small_copy(wown, wssum_ref.at[rows_of(idx ^ k)], s3, r3, k - 1, me).wait_recv()
        for cp in sm + ag:
            cp.wait_send()

    dma7 = pltpu.SemaphoreType.DMA((NDEV - 1,))
    return pl.pallas_call(
        body,
        name="reduce_small",
        in_specs=[VM] * 7 + [HBM],
        out_specs=[VM, VM, VM],
        out_shape=[jax.ShapeDtypeStruct((NVEC, D), F32), jax.ShapeDtypeStruct((NDEV, 8, D), F32),
                   jax.ShapeDtypeStruct(dws.shape, F32)],
        scratch_shapes=[pltpu.VMEM((NVEC, D), F32), pltpu.VMEM((NDEV, NVEC, D), F32),
                        pltpu.VMEM((NDEV, CH, CH), F32), pltpu.VMEM((CH, CH), F32)] + [dma7] * 6,
        compiler_params=pltpu.CompilerParams(vmem_limit_bytes=VMEM_LIMIT),
    )(acc_f, acc_v, acc_b, acc_a, acc_bs, acc_cw, dws, after)


HBM_ONLY = pl.BlockSpec(memory_space=pltpu.HBM)
SEM = pl.BlockSpec(memory_space=pltpu.SEMAPHORE)
EFFECT = pltpu.SideEffectType.DATAFLOW_SIDE_EFFECTING


def _rs_copies(g_refs, land_refs, sems):
    x, y, c, idx = _place()
    cps = []
    for k in range(1, NDEV):
        t = idx ^ k
        for a in range(len(g_refs)):
            n = len(cps)
            cps.append(pltpu.make_async_remote_copy(
                src_ref=g_refs[a].at[t], dst_ref=land_refs[a].at[k - 1], send_sem=sems[2 * n],
                recv_sem=sems[2 * n + 1], device_id=_dev(t), device_id_type=MESH))
    return cps


def _rs_start(grads, name, after=()):
    nw = len(grads)
    nsem = 2 * nw * (NDEV - 1)
    lands = [lax.empty((NDEV - 1,) + g.shape[1:], g.dtype) for g in grads]

    def body(*refs):
        g_refs, land_refs = refs[:nw], refs[nw:2 * nw]
        sems = refs[2 * nw + len(after):2 * nw + len(after) + nsem]
        token = refs[-1]
        for cp in _rs_copies(g_refs, land_refs, sems):
            cp.start()
        token[...] = jnp.zeros_like(token)

    outs = pl.pallas_call(
        body, name=name,
        out_shape=(*[pltpu.SemaphoreType.DMA(())] * nsem,
                   *[pltpu.HBM(g.shape, g.dtype) for g in grads], *[pltpu.HBM(l.shape, l.dtype) for l in lands],
                   jax.ShapeDtypeStruct((8, CH), F32)),
        in_specs=[HBM_ONLY] * (2 * nw) + [HBM] * len(after),
        out_specs=(*[SEM] * nsem, *[HBM_ONLY] * (2 * nw), VM),
        input_output_aliases={i: nsem + i for i in range(2 * nw)},
        compiler_params=pltpu.CompilerParams(has_side_effects=EFFECT),
    )(*[pltpu.with_memory_space_constraint(g, pltpu.HBM) for g in grads],
      *[pltpu.with_memory_space_constraint(l, pltpu.HBM) for l in lands], *after)
    return outs[:nsem], outs[nsem:nsem + nw], outs[nsem + nw:nsem + 2 * nw], outs[-1]


def _rs_wait(sems, g_thru, land_thru, after, name):
    nw = len(g_thru)
    nsem = len(sems)

    def body(*refs):
        g_refs, land_refs = refs[:nw], refs[nw:2 * nw]
        for cp in _rs_copies(g_refs, land_refs, refs[2 * nw:2 * nw + nsem]):
            cp.wait_send()
            cp.wait_recv()

    outs = pl.pallas_call(
        body, name=name,
        out_shape=tuple(pltpu.HBM(a.shape, a.dtype) for a in list(g_thru) + list(land_thru)),
        in_specs=[HBM_ONLY] * (2 * nw) + [SEM] * nsem + [HBM] * len(after),
        out_specs=tuple([HBM_ONLY] * (2 * nw)),
        input_output_aliases={i: i for i in range(2 * nw)},
        compiler_params=pltpu.CompilerParams(has_side_effects=EFFECT),
    )(*g_thru, *land_thru, *sems, *after)
    return outs[:nw], outs[nw:]


def _adamw(w, g, m, v):
    m2 = ADAM_B1 * m + (1.0 - ADAM_B1) * g
    v2 = ADAM_B2 * v + (1.0 - ADAM_B2) * (g * g)
    m_hat = m2 / (1.0 - ADAM_B1 ** ADAM_STEP)
    v_hat = v2 / (1.0 - ADAM_B2 ** ADAM_STEP)
    delta = -ADAM_LR * (m_hat / (jnp.sqrt(v_hat) + ADAM_EPS) + ADAM_WD * w)
    return delta, m2, v2


def _adam_big(r, w, m, v, rb, name, own=None, after=None):
    R, C = w.shape
    ns = r.shape[0]
    g_all, idx1 = own

    def body(idx_ref, r_ref, own_ref, *refs):
        w_ref, m_ref, v_ref, g_ref, d_ref, m2_ref, v2_ref = refs[len(refs) - 7:]
        g = own_ref[0].astype(F32)
        for k in range(ns):
            g = g + r_ref[k].astype(F32)
        g_ref[...] = g
        d_ref[...], m2_ref[...], v2_ref[...] = _adamw(w_ref[...], g, m_ref[...], v_ref[...])

    t2 = pl.BlockSpec((rb, C), lambda i, idx_ref: (i, 0))
    sd = jax.ShapeDtypeStruct((R, C), F32)
    extra_specs = [HBM] if after is not None else []
    extra = [after] if after is not None else []
    return pl.pallas_call(
        body, name=name,
        grid_spec=pltpu.PrefetchScalarGridSpec(
            num_scalar_prefetch=1, grid=(R // rb,),
            in_specs=[pl.BlockSpec((ns, rb, C), lambda i, idx_ref: (0, i, 0)),
                      pl.BlockSpec((1, rb, C), lambda i, idx_ref: (idx_ref[0], i, 0))] + extra_specs + [t2, t2, t2],
            out_specs=[t2, t2, t2, t2]),
        out_shape=[sd, sd, sd, sd],
        compiler_params=pltpu.CompilerParams(dimension_semantics=("arbitrary",), vmem_limit_bytes=VMEM_LIMIT),
    )(idx1, r, g_all, *extra, w, m, v)


def _adam_ada(cact_t, dcs, w, m, v, rb, name):
    R, C = w.shape

    def body(ct_ref, dc_ref, w_ref, m_ref, v_ref, g_ref, d_ref, m2_ref, v2_ref):
        g = jnp.dot(ct_ref[...], dc_ref[...], preferred_element_type=F32, precision=lax.Precision.HIGHEST)
        g_ref[...] = g
        d_ref[...], m2_ref[...], v2_ref[...] = _adamw(w_ref[...], g, m_ref[...], v_ref[...])

    t2 = pl.BlockSpec((rb, C), lambda i: (i, 0))
    sd = jax.ShapeDtypeStruct((R, C), F32)
    return pl.pallas_call(
        body, name=name, grid=(R // rb,),
        in_specs=[pl.BlockSpec((rb, NDEV), lambda i: (i, 0)), _full((NDEV, C)), t2, t2, t2],
        out_specs=[t2, t2, t2, t2], out_shape=[sd, sd, sd, sd],
        compiler_params=pltpu.CompilerParams(dimension_semantics=("arbitrary",), vmem_limit_bytes=VMEM_LIMIT),
    )(cact_t, dcs, w, m, v)


_SMALL = ["ada_b", "ada_f_b", "norm1_g", "b_in", "a_ln_g", "a_ln_b", "a_spatial_b", "b_conv_b", "b_gn_g", "b_gn_b",
          "out_norm_a_g", "out_norm_b_g", "norm2_g", "norm_f_g", "a_spatial_w", "b_conv_w"]


def _adam_small(vsum, wssum, gcw, params):
    names = _SMALL
    flat = []
    for n in names:
        flat += list(params[n])

    def body(vs_ref, ws_ref, gcw_ref, *rest):
        ins = rest[:3 * len(names)]
        outs = rest[3 * len(names):]
        for pi, n in enumerate(names):
            w_ref, m_ref, v_ref = ins[3 * pi:3 * pi + 3]
            g_ref, d_ref, m2_ref, v2_ref = outs[4 * pi:4 * pi + 4]
            if n in ("ada_b", "ada_f_b", "b_in"):
                row0 = {"ada_b": 0, "ada_f_b": 6, "b_in": 9}[n]
                pieces = [(vs_ref[row0 + r:row0 + r + 1, :], slice(r * D, (r + 1) * D))
                          for r in range(w_ref.shape[1] // D)]
            elif n == "a_spatial_w":
                pieces = [(ws_ref[...], slice(None))]
            elif n == "b_conv_w":
                pieces = [(gcw_ref[...], slice(None))]
            else:
                row, off, width = _VEC_AT[n]
                pieces = [(vs_ref[row:row + 1, off:off + width], slice(None))]
            for g, cs in pieces:
                g_ref[:, cs] = g
                d_ref[:, cs], m2_ref[:, cs], v2_ref[:, cs] = _adamw(w_ref[:, cs], g, m_ref[:, cs], v_ref[:, cs])

    out_shape = []
    for n in names:
        out_shape += [jax.ShapeDtypeStruct(params[n][0].shape, F32)] * 4
    outs = pl.pallas_call(
        body, name="adam_small",
        in_specs=[VM] * (3 + len(flat)), out_specs=[VM] * len(out_shape), out_shape=out_shape,
        compiler_params=pltpu.CompilerParams(vmem_limit_bytes=VMEM_LIMIT),
    )(vsum, wssum, gcw, *flat)
    return {n: outs[4 * pi:4 * pi + 4] for pi, n in enumerate(names)}


def _token_tile(T, want):
    return want if T % want == 0 else T


def kernel(x, c, ada_w, ada_b, norm1_g, w_in, b_in, a_ln_g, a_ln_b, a_spatial_w, a_spatial_b, b_conv_w, b_conv_b, b_gn_g, b_gn_b, out_norm_a_g, out_norm_b_g, w_out, norm2_g, w_ffn_in, w_ffn_out, ada_f_w, ada_f_b, norm_f_g, loss_target, m_ada_w, m_ada_b, m_norm1_g, m_w_in, m_b_in, m_a_ln_g, m_a_ln_b, m_a_spatial_w, m_a_spatial_b, m_b_conv_w, m_b_conv_b, m_b_gn_g, m_b_gn_b, m_out_norm_a_g, m_out_norm_b_g, m_w_out, m_norm2_g, m_w_ffn_in, m_w_ffn_out, m_ada_f_w, m_ada_f_b, m_norm_f_g, v_ada_w, v_ada_b, v_norm1_g, v_w_in, v_b_in, v_a_ln_g, v_a_ln_b, v_a_spatial_w, v_a_spatial_b, v_b_conv_w, v_b_conv_b, v_b_gn_g, v_b_gn_b, v_out_norm_a_g, v_out_norm_b_g, v_w_out, v_norm2_g, v_w_ffn_in, v_w_ffn_out, v_ada_f_w, v_ada_f_b, v_norm_f_g):
    T = x.shape[1]
    idx = 4 * lax.axis_index("x") + 2 * lax.axis_index("y") + lax.axis_index("c")
    x2d = x.reshape(T, D)
    tgt = loss_target.reshape(T, D)

    conv_s = jnp.pad(b_conv_w[0], ((0, HALO - KW), (0, 0)))
    call, cparts, cfparts, convg, (win_g,) = _gather(
        c, ada_w[0], ada_b.reshape(NDEV, -1), ada_f_w, ada_f_b.reshape(NDEV, -1), conv_s, [w_in[0].astype(BF16)])
    mod = jnp.concatenate([cparts.reshape(6, D), cfparts.reshape(2, D)], axis=0)
    cw = jnp.transpose(convg, (1, 0, 2)).reshape(HALO, DB)

    tril = jnp.tril(jnp.ones((CH, CH), dtype=bool))
    wsm = jnp.where(tril[None], a_spatial_w[0], 0.0).astype(BF16)
    wcat = wsm.reshape(NH * CH, CH)
    wcat_t = jnp.transpose(wsm, (0, 2, 1)).reshape(NH * CH, CH)
    bsf = jnp.repeat(a_spatial_b[0].T, DA // NH, axis=1)
    lane = jnp.arange(DB)
    pm = jnp.where((lane[:, None] >> 6) == (lane[None, :] >> 6), 1.0 / 64.0, 0.0).astype(BF16)
    esel = jnp.where((lane[:, None] >> 6) == jnp.arange(CH)[None, :], 1.0, 0.0).astype(BF16)

    tm = _token_tile(T, 256)
    tk = _token_tile(T, 1024)
    (x1, hb, zvg, mixed, yb, o, gu, dgelu_u, dgelu_v, vhat, rslb, yhat, rsg), (wfi_g, wfo_g), wout = _mix_fwd(
        x2d, mod, norm1_g, win_g, b_in, a_ln_g, a_ln_b, wcat, bsf, cw, b_conv_b, b_gn_g, b_gn_b, out_norm_a_g,
        out_norm_b_g, w_out[0].astype(BF16), pm, [w_ffn_in[0].T.astype(BF16), w_ffn_out[0].astype(BF16)],
        _token_tile(T, 512))
    dx1, h2b, dgu, act, dxg, acc_f = _ffn(x1, tgt, mod, norm2_g, norm_f_g.reshape(1, D),
                                          wfi_g.reshape(2 * DFF, D), wfo_g.reshape(DFF, D), tm)
    g_wfi = _wgrad_rows(dgu, h2b, 2 * WFI_B, tk, "wgrad_ffn_in").reshape(NDEV, WFI_B, D)
    g_wfo = _wgrad_rows(act, dxg, 2 * WFI_B, tk, "wgrad_ffn_out").reshape(NDEV, DFF // NDEV, D)
    f_sems, f_thru, f_land, f_token = _rs_start([g_wfi, g_wfo], "rs_ffn_start")
    (gx, acc_v, acc_b, acc_a, acc_bs, acc_ws, acc_cw), (g_win, g_wout) = _mix_bwd(
        dx1, x2d, zvg, mixed, o, hb, yb, gu, dgelu_u, dgelu_v, vhat, rslb, yhat, rsg, mod, norm1_g, win_g, a_ln_g,
        a_ln_b, wcat, wcat_t, cw, b_gn_g, b_gn_b, out_norm_a_g, out_norm_b_g, wout, pm, esel, f_token, tm)
    (g_wfi_d, g_wfo_d), (r_wfi, r_wfo) = _rs_wait(f_sems, f_thru, f_land, [acc_v], "rs_ffn_wait")
    g_wout = g_wout.reshape(NDEV, D // NDEV, D)

    vsum, dcond_all, wssum = _reduce_small(acc_f, acc_v, acc_b, acc_a, acc_bs, acc_cw, acc_ws, g_wfi_d)
    sems, g_thru, land_thru, token = _rs_start([g_win, g_wout], "rs_mix_start", after=(vsum,))

    own = lambda g: (g, jnp.reshape(idx, (1,)).astype(jnp.int32))
    res = {}
    res["w_ffn_in"] = tuple(a.T for a in _adam_big(r_wfi, w_ffn_in[0].T, m_w_ffn_in[0].T, v_w_ffn_in[0].T, WFI_B // 4,
                                                   "adam_w_ffn_in", own=own(g_wfi_d), after=token))
    res["w_ffn_out"] = _adam_big(r_wfo, w_ffn_out[0], m_w_ffn_out[0], v_w_ffn_out[0], DFF // NDEV // 2,
                                 "adam_w_ffn_out", own=own(g_wfo_d), after=token)
    cact_t = (call * jax.nn.sigmoid(call)).T
    dcond = dcond_all.reshape(NDEV, 8 * D)
    nada = ada_w.shape[2]
    nadf = ada_f_w.shape[1]
    dcs = lax.dynamic_slice(dcond, (0, idx * nada), (NDEV, nada))
    dcfs = lax.dynamic_slice(dcond, (0, 6 * D + idx * nadf), (NDEV, nadf))
    res["ada_w"] = _adam_ada(cact_t, dcs, ada_w[0], m_ada_w[0], v_ada_w[0], 256, "adam_ada_w")
    res["ada_f_w"] = _adam_ada(cact_t, dcfs, ada_f_w, m_ada_f_w, v_ada_f_w, 256, "adam_ada_f_w")
    ncw = b_conv_w.shape[2]
    gcw = jnp.concatenate([lax.dynamic_slice(vsum, (_CW_ROW, idx * ncw), (HALO // 2, ncw)),
                           lax.dynamic_slice(vsum, (_CW_ROW, DB + idx * ncw), (HALO // 2, ncw))], axis=0)[:KW]
    two = lambda a: a.reshape(1, -1) if a.ndim == 1 else a.reshape(-1, a.shape[-1])
    small_in = {
        "ada_b": (ada_b, m_ada_b, v_ada_b), "ada_f_b": (ada_f_b, m_ada_f_b, v_ada_f_b),
        "norm1_g": (norm1_g, m_norm1_g, v_norm1_g), "b_in": (b_in, m_b_in, v_b_in),
        "a_ln_g": (a_ln_g, m_a_ln_g, v_a_ln_g), "a_ln_b": (a_ln_b, m_a_ln_b, v_a_ln_b),
        "a_spatial_b": (a_spatial_b.reshape(1, D), m_a_spatial_b.reshape(1, D), v_a_spatial_b.reshape(1, D)),
        "b_conv_b": (b_conv_b, m_b_conv_b, v_b_conv_b), "b_gn_g": (b_gn_g, m_b_gn_g, v_b_gn_g),
        "b_gn_b": (b_gn_b, m_b_gn_b, v_b_gn_b), "out_norm_a_g": (out_norm_a_g, m_out_norm_a_g, v_out_norm_a_g),
        "out_norm_b_g": (out_norm_b_g, m_out_norm_b_g, v_out_norm_b_g),
        "norm2_g": (norm2_g, m_norm2_g, v_norm2_g), "norm_f_g": (norm_f_g, m_norm_f_g, v_norm_f_g),
        "a_spatial_w": (a_spatial_w, m_a_spatial_w, v_a_spatial_w),
        "b_conv_w": (b_conv_w[0], m_b_conv_w[0], v_b_conv_w[0]),
    }
    small_in = {n: tuple(two(a) for a in t) for n, t in small_in.items()}
    res.update(_adam_small(vsum, wssum, gcw, small_in))
    (g_win_d, g_wout_d), (r_win, r_wout) = _rs_wait(
        sems, g_thru, land_thru,
        [res["w_ffn_in"][0], res["w_ffn_out"][0], res["ada_w"][0], res["ada_f_w"][0], res["norm_f_g"][0]],
        "rs_mix_wait")
    res["w_in"] = _adam_big(r_win, w_in[0], m_w_in[0], v_w_in[0], 256, "adam_w_in", own=own(g_win_d))
    res["w_out"] = _adam_big(r_wout, w_out[0], m_w_out[0], v_w_out[0], D // NDEV, "adam_w_out", own=own(g_wout_d))

    loss = 0.5 / D * jnp.sum(vsum[_LOSS_ROW])
    shapes = {"ada_w": ada_w, "ada_b": ada_b, "norm1_g": norm1_g, "w_in": w_in, "b_in": b_in, "a_ln_g": a_ln_g,
              "a_ln_b": a_ln_b, "a_spatial_w": a_spatial_w, "a_spatial_b": a_spatial_b, "b_conv_w": b_conv_w,
              "b_conv_b": b_conv_b, "b_gn_g": b_gn_g, "b_gn_b": b_gn_b, "out_norm_a_g": out_norm_a_g,
              "out_norm_b_g": out_norm_b_g, "w_out": w_out, "norm2_g": norm2_g, "w_ffn_in": w_ffn_in,
              "w_ffn_out": w_ffn_out, "ada_f_w": ada_f_w, "ada_f_b": ada_f_b, "norm_f_g": norm_f_g}
    order = list(shapes)
    outs = [loss, gx.reshape(x.shape)]
    for which in range(4):
        outs += [res[n][which].reshape(shapes[n].shape) for n in order]
    return tuple(outs)
```

```python
import math

import jax
import jax.numpy as jnp
from jax import lax
from jax.experimental import pallas as pl
from jax.experimental.pallas import tpu as pltpu

F32 = jnp.float32
BF16 = jnp.bfloat16

D = 1024
DA = 512
DB = 512
DIN = 2048
DFF = 2816
NH = 8
CH = 128
KW = 31
HALO = 32
NDEV = 8
WIN_B = DIN // NDEV
WFI_B = 2 * DFF // NDEV
EPS = 1e-6
NVEC = 40
VMEM_LIMIT = 56 * 1024 * 1024

ADAM_LR, ADAM_B1, ADAM_B2, ADAM_EPS, ADAM_WD, ADAM_STEP = 0.001, 0.9, 0.999, 1e-08, 0.01, 10

MESH = pl.DeviceIdType.MESH


def _dot(a, b):
    return jnp.dot(a, b, preferred_element_type=F32)


def _dot_nt(a, b):
    return lax.dot_general(a, b, (((1,), (1,)), ((), ())), preferred_element_type=F32)


def _dot_tn(a, b):
    return lax.dot_general(a, b, (((0,), (0,)), ((), ())), preferred_element_type=F32)


def _rs(v):
    return lax.rsqrt(jnp.mean(v * v, axis=-1, keepdims=True) + EPS)


def _sig(v):
    return 1.0 / (1.0 + jnp.exp(-v))


_INV_SQRT2 = 1.0 / math.sqrt(2.0)
_INV_SQRT2PI = 1.0 / math.sqrt(2.0 * math.pi)


def _gelu_parts(v):
    cdf = 0.5 * (1.0 + lax.erf(v * _INV_SQRT2))
    pdf = jnp.exp(-0.5 * v * v) * _INV_SQRT2PI
    return v * cdf, cdf + v * pdf


def _grp_mean(v, pm):
    hi = v.astype(BF16)
    lo = (v - hi.astype(F32)).astype(BF16)
    return _dot(hi, pm) + _dot(lo, pm)


def _colsum(v):
    return jnp.sum(v, axis=0, keepdims=True)


def _full(shape):
    nd = len(shape)
    return pl.BlockSpec(shape, lambda *_: (0,) * nd)


def _resident(shape):
    nd = len(shape)
    return pl.BlockSpec(shape, lambda *_: (0,) * nd, pipeline_mode=pl.Buffered(1))


HBM = pl.BlockSpec(memory_space=pl.ANY)
VM = pl.BlockSpec(memory_space=pltpu.VMEM)


SH_ROWS = HALO - 8


def _shifted_copies(buf, shbuf, tm):
    for b in range(1, 8):
        shbuf[b - 1] = buf[b:b + tm + SH_ROWS, :]


def _window(buf, shbuf, off, tm):
    a, b = divmod(off, 8)
    if b == 0:
        return buf[8 * a:8 * a + tm, :]
    return shbuf[b - 1, 8 * a:8 * a + tm, :]


def _first_head_lanes():
    return lax.broadcasted_iota(jnp.int32, (CH, CH), 1) < (DA // NH)


def _mix_heads(w_ref, vb, first):
    outs = []
    for p in range(NH // 2):
        v = vb[:, p * CH:(p + 1) * CH]
        a = _dot(w_ref[(2 * p) * CH:(2 * p + 1) * CH, :], v)
        b = _dot(w_ref[(2 * p + 1) * CH:(2 * p + 2) * CH, :], v)
        outs.append(jnp.where(first, a, b))
    return jnp.concatenate(outs, axis=1)


def _place():
    x, y, c = lax.axis_index("x"), lax.axis_index("y"), lax.axis_index("c")
    return x, y, c, 4 * x + 2 * y + c


def _dev(t):
    return (t >> 2, (t >> 1) & 1, t & 1)


class _AllGather:
    def __init__(self, w_in, w_out, wss, wrs, lsem, slot=lambda ref, b: ref.at[b]):
        x, y, c, idx = _place()
        me, sibling = (x, y, c), (x, y, 1 - c)
        chips = [(1 - x, y), (x, 1 - y), (1 - x, 1 - y)]
        nw = len(w_in)

        def blk(p):
            return 4 * p[0] + 2 * p[1] + p[2]

        def wcopy(a, k, block, to, src=None):
            dst = slot(w_out[a], blk(block))
            return pltpu.make_async_remote_copy(src_ref=dst if src is None else src, dst_ref=dst,
                                                send_sem=wss.at[a, k], recv_sem=wrs.at[a, k],
                                                device_id=to, device_id_type=MESH)

        self.mine = [pltpu.make_async_copy(w_in[a], slot(w_out[a], idx), lsem.at[a]) for a in range(nw)]
        self.first = []
        for a in range(nw):
            self.first.append(wcopy(a, 0, me, sibling, src=w_in[a]))
            self.first += [wcopy(a, 1 + j, me, (*chip, c), src=w_in[a]) for j, chip in enumerate(chips)]
        self.landed = [[wcopy(a, 1 + j, (*chip, c), me) for a in range(nw)] for j, chip in enumerate(chips)]
        self.passed = [[wcopy(a, 4 + j, (*chip, c), sibling) for a in range(nw)] for j, chip in enumerate(chips)]
        self.from_sibling = []
        for a in range(nw):
            self.from_sibling.append(wcopy(a, 0, sibling, me))
            self.from_sibling += [wcopy(a, 4 + j, (*chip, 1 - c), me) for j, chip in enumerate(chips)]

    def start(self):
        for cp in self.mine + self.first:
            cp.start()

    def forward(self):
        for land, pas in zip(self.landed, self.passed):
            for l, p in zip(land, pas):
                l.wait_recv()
                p.start()

    def finish(self):
        for cp in self.from_sibling:
            cp.wait_recv()
        for cp in self.first:
            cp.wait_send()
        for pas in self.passed:
            for p in pas:
                p.wait_send()
        for cp in self.mine:
            cp.wait()


AG_SEMS = lambda nw: [pltpu.SemaphoreType.DMA((nw, 7)), pltpu.SemaphoreType.DMA((nw, 7)),
                      pltpu.SemaphoreType.DMA((nw,))]


def _wout_rows(ref, b):
    rows = D // NDEV
    return ref.at[pl.ds(pl.multiple_of(b * rows, rows), rows)]


def _mix_fwd(x, mod, g1, win, b_in, lng, lnb, wcat, bsf, cw, cb, gng, gnb, oga, ogb, wout_s, pm, ffn_shards, tm):
    T = x.shape[0]
    nt = T // tm
    nch = tm // CH
    nw = len(ffn_shards)
    fwd_step = (5 * nt) // 8
    saved = [(D, F32), (D, BF16), (2 * DB, F32), (DA, F32), (D, BF16), (D, F32), (DA, F32), (DA, F32), (DA, F32),
             (DA, F32), (CH, F32), (DB, F32), (DB, F32)]
    NSAVE = len(saved)

    def body(x_ref, mod_ref, g1_ref, win_ref, bin_ref, lng_ref, lnb_ref, wcat_ref, bsf_ref, cw_ref, cb_ref,
             gng_ref, gnb_ref, oga_ref, ogb_ref, wouts_ref, pm_ref, *rest):
        sh_in = rest[:nw]
        (x1_ref, h_ref, zvg_ref, mixed_ref, y_ref, o_ref, gu_ref, dgu_ref, dgv_ref, vhat_ref, rsl_ref, yhat_ref,
         rsg_ref) = rest[nw:nw + NSAVE]
        sh_out = rest[nw + NSAVE:2 * nw + NSAVE]
        woutg_ref = rest[2 * nw + NSAVE]
        glbuf, shbuf, wout_v, wss, wrs, lsem, oss, ors, olsem = rest[2 * nw + NSAVE + 1:]
        i = pl.program_id(0)
        gather_wout = lambda: _AllGather([wouts_ref], [wout_v], oss, ors, olsem, slot=_wout_rows)

        @pl.when(i == 0)
        def _():
            gather_wout().start()

        xv = x_ref[...]
        shift1 = mod_ref[0:1, :]
        scale1 = mod_ref[1:2, :]
        gate1 = mod_ref[2:3, :]
        h = (xv * _rs(xv) * g1_ref[...]) * (1.0 + scale1) + shift1
        hb = h.astype(BF16)
        h_ref[...] = hb
        z = jnp.concatenate([_dot(hb, win_ref[j]) for j in range(NDEV)], axis=1) + bin_ref[...]
        zvg_ref[...] = z[:, 2 * DA:]
        gu, dgelu_u = _gelu_parts(z[:, 0:DA])
        gv, dgelu_v = _gelu_parts(z[:, DA:2 * DA])
        gu_ref[...] = gu
        dgu_ref[...] = dgelu_u
        dgv_ref[...] = dgelu_v
        xc = gv - jnp.mean(gv, axis=-1, keepdims=True)
        rsl = lax.rsqrt(jnp.mean(xc * xc, axis=-1, keepdims=True) + EPS)
        vhat = xc * rsl
        vhat_ref[...] = vhat
        rsl_ref[...] = jnp.broadcast_to(rsl, (tm, CH))
        vnb = (vhat * lng_ref[...] + lnb_ref[...]).astype(BF16)
        first = _first_head_lanes()
        chunks = []
        for ci in range(nch):
            chunks.append(_mix_heads(wcat_ref, vnb[ci * CH:(ci + 1) * CH, :], first) + bsf_ref[...])
        mixed = jnp.concatenate(chunks, axis=0) if nch > 1 else chunks[0]
        mixed_ref[...] = mixed
        ya = gu * mixed
        gl = z[:, 2 * DA:2 * DA + DB] * _sig(z[:, 2 * DA + DB:])

        @pl.when(i == 0)
        def _():
            glbuf[0:HALO, :] = jnp.zeros((HALO, DB), F32)

        glbuf[HALO:HALO + tm, :] = gl
        _shifted_copies(glbuf, shbuf, tm)
        yc = jnp.zeros((tm, DB), F32) + cb_ref[...]
        for k in range(KW):
            yc = yc + cw_ref[k:k + 1, :] * _window(glbuf, shbuf, HALO - (KW - 1) + k, tm)
        glbuf[0:HALO, :] = gl[tm - HALO:, :]
        pmv = pm_ref[...]
        dc = yc - _grp_mean(yc, pmv)
        rsg = lax.rsqrt(_grp_mean(dc * dc, pmv) + EPS)
        yhat = dc * rsg
        yhat_ref[...] = yhat
        rsg_ref[...] = rsg
        yg = yhat * gng_ref[...] + gnb_ref[...]
        yb = yg * _sig(yg)
        na = ya * _rs(ya) * oga_ref[...]
        nb = yb * _rs(yb) * ogb_ref[...]
        yv = jnp.concatenate([na, nb], axis=1).astype(BF16)
        y_ref[...] = yv

        @pl.when(i == 0)
        def _():
            ag = gather_wout()
            ag.forward()
            ag.finish()
            _AllGather(sh_in, sh_out, wss, wrs, lsem).start()

        o = _dot(yv, wout_v[...])
        o_ref[...] = o
        x1_ref[...] = xv + gate1 * o

        @pl.when(i == fwd_step)
        def _():
            _AllGather(sh_in, sh_out, wss, wrs, lsem).forward()

        @pl.when(i == nt - 1)
        def _():
            _AllGather(sh_in, sh_out, wss, wrs, lsem).finish()
            pltpu.sync_copy(wout_v, woutg_ref)

    tile = lambda w: pl.BlockSpec((tm, w), lambda i: (i, 0))
    outs = pl.pallas_call(
        body,
        name="mix_fwd",
        grid=(nt,),
        in_specs=[tile(D), _full((8, D)), _full((1, D)), _resident((NDEV, D, WIN_B)), _full((1, DIN)),
                  _full((1, DA)), _full((1, DA)), _full((NH * CH, CH)), _full((CH, DA)), _full((HALO, DB)),
                  _full((1, DB)), _full((1, DB)), _full((1, DB)), _full((1, DA)), _full((1, DB)),
                  _full((D // NDEV, D)), _full((DB, DB))] + [HBM] * nw,
        out_specs=[tile(w) for w, _ in saved] + [HBM] * (nw + 1),
        out_shape=[jax.ShapeDtypeStruct((T, w), dt) for w, dt in saved]
                  + [jax.ShapeDtypeStruct((NDEV,) + s.shape, s.dtype) for s in ffn_shards]
                  + [jax.ShapeDtypeStruct((D, D), BF16)],
        scratch_shapes=[pltpu.VMEM((HALO + tm, DB), F32), pltpu.VMEM((7, tm + SH_ROWS, DB), F32),
                        pltpu.VMEM((D, D), BF16)] + AG_SEMS(nw) + AG_SEMS(1),
        compiler_params=pltpu.CompilerParams(dimension_semantics=("arbitrary",), vmem_limit_bytes=VMEM_LIMIT),
    )(x, mod, g1, win, b_in, lng, lnb, wcat, bsf, cw, cb, gng, gnb, oga, ogb, wout_s, pm, *ffn_shards)
    return outs[:NSAVE], outs[NSAVE:NSAVE + nw], outs[NSAVE + nw]


FF_BLOCKS = ((0, 1024), (1024, 1024), (2048, 768))


def _ffn(x1, tgt, mod, g2, gf, wfi_t, wfo, tm):
    T = x1.shape[0]
    nt = T // tm

    def body(x1_ref, tgt_ref, mod_ref, g2_ref, gf_ref, wfi_ref, wfo_ref,
             dx1_ref, h2_ref, dgu_ref, act_ref, dxg_ref, acc_ref, g_s, u_s):
        i = pl.program_id(0)

        @pl.when(i == 0)
        def _():
            acc_ref[...] = jnp.zeros((8, D), F32)

        x1 = x1_ref[...]
        shift2 = mod_ref[3:4, :]
        scale2 = mod_ref[4:5, :]
        gate2 = mod_ref[5:6, :]
        shiftf = mod_ref[6:7, :]
        scalef = mod_ref[7:8, :]
        g2v = g2_ref[...]
        gfv = gf_ref[...]
        r2 = _rs(x1)
        xn2 = x1 * r2
        h2b = (xn2 * g2v * (1.0 + scale2) + shift2).astype(BF16)
        h2_ref[...] = h2b
        f = jnp.zeros((tm, D), F32)
        for o, w in FF_BLOCKS:
            g = _dot_nt(h2b, wfi_ref[o:o + w, :])
            u = _dot_nt(h2b, wfi_ref[DFF + o:DFF + o + w, :])
            g_s[:, o:o + w] = g
            u_s[:, o:o + w] = u
            actb = (g * _sig(g) * u).astype(BF16)
            act_ref[:, o:o + w] = actb
            f = f + _dot(actb, wfo_ref[o:o + w, :])
        x2 = x1 + gate2 * f
        rf = _rs(x2)
        xnf = x2 * rf
        out = xnf * gfv * (1.0 + scalef) + shiftf
        e = out - tgt_ref[...]
        dout = e * (1.0 / D)
        acc_ref[7:8, :] += _colsum(e * e)
        acc_ref[0:1, :] += _colsum(dout)
        acc_ref[1:2, :] += _colsum(dout * xnf * gfv)
        acc_ref[2:3, :] += _colsum(dout * (1.0 + scalef) * xnf)
        dxnf = dout * (1.0 + scalef) * gfv
        dx2 = rf * (dxnf - xnf * jnp.mean(dxnf * xnf, axis=-1, keepdims=True))
        acc_ref[3:4, :] += _colsum(dx2 * f)
        dxgb = (dx2 * gate2).astype(BF16)
        dxg_ref[...] = dxgb
        dh2 = jnp.zeros((tm, D), F32)
        for o, w in FF_BLOCKS:
            dact = _dot_nt(dxgb, wfo_ref[o:o + w, :])
            g = g_s[:, o:o + w]
            u = u_s[:, o:o + w]
            s = _sig(g)
            dgb = (dact * u * (s * (1.0 + g * (1.0 - s)))).astype(BF16)
            dub = (dact * (g * s)).astype(BF16)
            dgu_ref[:, o:o + w] = dgb
            dgu_ref[:, DFF + o:DFF + o + w] = dub
            dh2 = dh2 + _dot(dgb, wfi_ref[o:o + w, :])
            dh2 = dh2 + _dot(dub, wfi_ref[DFF + o:DFF + o + w, :])
        acc_ref[4:5, :] += _colsum(dh2)
        acc_ref[5:6, :] += _colsum(dh2 * xn2 * g2v)
        acc_ref[6:7, :] += _colsum(dh2 * (1.0 + scale2) * xn2)
        dxn2 = dh2 * (1.0 + scale2) * g2v
        dx1_ref[...] = dx2 + r2 * (dxn2 - xn2 * jnp.mean(dxn2 * xn2, axis=-1, keepdims=True))

    tile = lambda w: pl.BlockSpec((tm, w), lambda i: (i, 0))
    return pl.pallas_call(
        body,
        name="ffn_fwd_bwd",
        grid=(nt,),
        in_specs=[tile(D), tile(D), _full((8, D)), _full((1, D)), _full((1, D)),
                  _resident((2 * DFF, D)), _resident((DFF, D))],
        out_specs=[tile(D), tile(D), tile(2 * DFF), tile(DFF), tile(D), _full((8, D))],
        out_shape=[jax.ShapeDtypeStruct((T, D), F32), jax.ShapeDtypeStruct((T, D), BF16),
                   jax.ShapeDtypeStruct((T, 2 * DFF), BF16), jax.ShapeDtypeStruct((T, DFF), BF16),
                   jax.ShapeDtypeStruct((T, D), BF16), jax.ShapeDtypeStruct((8, D), F32)],
        scratch_shapes=[pltpu.VMEM((tm, DFF), F32), pltpu.VMEM((tm, DFF), F32)],
        compiler_params=pltpu.CompilerParams(dimension_semantics=("arbitrary",), vmem_limit_bytes=VMEM_LIMIT),
    )(x1, tgt, mod, g2, gf, wfi_t, wfo)


def _mix_bwd(dx1, x, zvg, mixed, o, hb, yb, gu, dgu, dgv, vhat, rslb, yhat, rsg, mod, g1, win, lng, lnb, wcat, wcat_t,
             cw, gng, gnb, oga, ogb, wout, pm, esel, after, tm):
    T = x.shape[0]
    nt = T // tm
    nch = tm // CH
    WOB = 256

    def body(dx1_ref, x_ref, zvg_ref, mixed_ref, o_ref, hb_ref, yb_ref, gu_ref, dgu_ref, dgv_ref, vhat_ref, rsl_ref,
             yhat_ref, rsg_ref, mod_ref, g1_ref, win_ref, lng_ref, lnb_ref, wcat_ref, wcatt_ref, cw_ref, gng_ref,
             gnb_ref, oga_ref, ogb_ref, wout_ref, pm_ref, esel_ref, after_ref,
             gx_ref, accv_ref, accb_ref, acca_ref, accbs_ref, accws_ref, acccw_ref, gwin_ref, gwout_ref,
             dycbuf, shbuf, bs_s, acc_win, acc_wout, st_win, st_wout):
        i = pl.program_id(0)

        @pl.when(i == 0)
        def _():
            acc_win[...] = jnp.zeros((NDEV, D, WIN_B), F32)
            acc_wout[...] = jnp.zeros((D, D), F32)
            accv_ref[...] = jnp.zeros((8, D), F32)
            accb_ref[...] = jnp.zeros((1, DIN), F32)
            acca_ref[...] = jnp.zeros((8, DA), F32)
            accws_ref[...] = jnp.zeros((NH * CH, CH), F32)
            acccw_ref[...] = jnp.zeros((HALO, DB), F32)
            bs_s[...] = jnp.zeros((CH, DA), F32)
            dycbuf[tm:tm + HALO, :] = jnp.zeros((HALO, DB), F32)

        shift1 = mod_ref[0:1, :]
        scale1 = mod_ref[1:2, :]
        gate1 = mod_ref[2:3, :]
        g1v = g1_ref[...]
        xv = x_ref[...]
        r1 = _rs(xv)
        xn1 = xv * r1
        val = zvg_ref[:, 0:DB]
        gate = zvg_ref[:, DB:]
        gu = gu_ref[...]
        dgelu_u = dgu_ref[...]
        dgelu_v = dgv_ref[...]
        vhat = vhat_ref[...]
        rsl = rsl_ref[:, 0:1]
        lngv = lng_ref[...]
        vnb = (vhat * lngv + lnb_ref[...]).astype(BF16)
        mixed = mixed_ref[...]
        ya = gu * mixed
        ra = _rs(ya)
        yan = ya * ra
        sgt = _sig(gate)
        gl = val * sgt
        pmv = pm_ref[...]
        rsg = rsg_ref[...]
        yhat = yhat_ref[...]
        gngv = gng_ref[...]
        yg = yhat * gngv + gnb_ref[...]
        sgy = _sig(yg)
        yb = yg * sgy
        rb = _rs(yb)
        ybn = yb * rb
        dx1 = dx1_ref[...]
        accv_ref[0:1, :] += _colsum(dx1 * o_ref[...])
        dogb = (dx1 * gate1).astype(BF16)
        acc_wout[...] += _dot_tn(yb_ref[...], dogb)
        dy = _dot_nt(dogb, wout_ref[...])
        dna = dy[:, 0:DA]
        dnb = dy[:, DA:]
        ogav = oga_ref[...]
        ogbv = ogb_ref[...]
        acca_ref[2:3, :] += _colsum(dna * yan)
        acca_ref[3:4, :] += _colsum(dnb * ybn)
        ta = dna * ogav
        dya = ra * (ta - yan * jnp.mean(ta * yan, axis=-1, keepdims=True))
        tb = dnb * ogbv
        dyb = rb * (tb - ybn * jnp.mean(tb * ybn, axis=-1, keepdims=True))
        dgu = dya * mixed
        dm = dya * gu
        first = _first_head_lanes()
        zero = jnp.zeros((CH, CH), BF16)
        dvn_chunks = []
        bs_acc = bs_s[...]
        for ci in range(nch):
            dmc = dm[ci * CH:(ci + 1) * CH, :]
            bs_acc = bs_acc + dmc
            dmcb = dmc.astype(BF16)
            dvn_chunks.append(_mix_heads(wcatt_ref, dmcb, first))
            vc = vnb[ci * CH:(ci + 1) * CH, :]
            for p in range(NH // 2):
                xt = dmcb[:, p * CH:(p + 1) * CH]
                vt = vc[:, p * CH:(p + 1) * CH]
                accws_ref[(2 * p) * CH:(2 * p + 1) * CH, :] += _dot_nt(jnp.where(first, xt, zero), vt)
                accws_ref[(2 * p + 1) * CH:(2 * p + 2) * CH, :] += _dot_nt(jnp.where(first, zero, xt), vt)
        bs_s[...] = bs_acc
        dvn = jnp.concatenate(dvn_chunks, axis=0) if nch > 1 else dvn_chunks[0]
        acca_ref[0:1, :] += _colsum(dvn * vhat)
        acca_ref[1:2, :] += _colsum(dvn)
        dvh = dvn * lngv
        dgv = rsl * (dvh - jnp.mean(dvh, axis=-1, keepdims=True)
                     - vhat * jnp.mean(dvh * vhat, axis=-1, keepdims=True))
        du = dgu * dgelu_u
        dv = dgv * dgelu_v
        dyg = dyb * (sgy * (1.0 + yg * (1.0 - sgy)))
        acca_ref[5:6, :] += _colsum(dyg * yhat)
        acca_ref[6:7, :] += _colsum(dyg)
        dyh = dyg * gngv
        dyc = rsg * (dyh - _grp_mean(dyh, pmv) - yhat * _grp_mean(dyh * yhat, pmv))
        acca_ref[4:5, :] += _colsum(dyc)
        dycbuf[0:tm, :] = dyc
        _shifted_copies(dycbuf, shbuf, tm)
        dgl = jnp.zeros((tm, DB), F32)
        for k in range(KW):
            win_k = _window(dycbuf, shbuf, KW - 1 - k, tm)
            dgl = dgl + cw_ref[k:k + 1, :] * win_k
            acccw_ref[k:k + 1, :] += _colsum(win_k * gl)
        dycbuf[tm:tm + HALO, :] = dyc[0:HALO, :]
        dval = dgl * sgt
        dgate = dgl * val * sgt * (1.0 - sgt)
        dz = jnp.concatenate([du, dv, dval, dgate], axis=1)
        accb_ref[...] += _colsum(dz)
        dzb = dz.astype(BF16)
        hbv = hb_ref[...]
        dh = jnp.zeros((tm, D), F32)
        for j in range(NDEV):
            dzj = dzb[:, j * WIN_B:(j + 1) * WIN_B]
            acc_win[j] += _dot_tn(hbv, dzj)
            dh = dh + _dot_nt(dzj, win_ref[j])
        accv_ref[1:2, :] += _colsum(dh)
        dh_xn = _colsum(dh * xn1)
        accv_ref[2:3, :] += dh_xn * g1v
        accv_ref[3:4, :] += dh_xn * (1.0 + scale1)
        dxn1 = dh * (1.0 + scale1) * g1v
        gx_ref[...] = dx1 + r1 * (dxn1 - xn1 * jnp.mean(dxn1 * xn1, axis=-1, keepdims=True))

        @pl.when(i == nt - 1)
        def _():
            rows = lax.broadcasted_iota(jnp.int32, (NH * CH, CH), 0) & (CH - 1)
            cols = lax.broadcasted_iota(jnp.int32, (NH * CH, CH), 1)
            accws_ref[...] = jnp.where(cols <= rows, accws_ref[...], 0.0)
            bs = bs_s[...]
            hi = bs.astype(BF16)
            r1_ = bs - hi.astype(F32)
            mid = r1_.astype(BF16)
            lo = (r1_ - mid.astype(F32)).astype(BF16)
            ev = esel_ref[...]
            accbs_ref[...] = _dot(hi, ev) + _dot(mid, ev) + _dot(lo, ev)
            for j in range(NDEV):
                st_win[...] = acc_win[j].astype(BF16)
                pltpu.sync_copy(st_win, gwin_ref.at[j])
            for j in range(D // WOB):
                st_wout[...] = acc_wout[j * WOB:(j + 1) * WOB, :].astype(BF16)
                pltpu.sync_copy(st_wout, gwout_ref.at[pl.ds(j * WOB, WOB)])

    rev = lambda w: pl.BlockSpec((tm, w), lambda i: (nt - 1 - i, 0))
    outs = pl.pallas_call(
        body,
        name="mix_bwd",
        grid=(nt,),
        in_specs=[rev(D), rev(D), rev(2 * DB), rev(DA), rev(D), rev(D), rev(D), rev(DA), rev(DA), rev(DA), rev(DA),
                  rev(CH), rev(DB), rev(DB), _full((8, D)), _full((1, D)),
                  _resident((NDEV, D, WIN_B)), _full((1, DA)), _full((1, DA)), _full((NH * CH, CH)),
                  _full((NH * CH, CH)), _full((HALO, DB)), _full((1, DB)), _full((1, DB)), _full((1, DA)),
                  _full((1, DB)), _resident((D, D)), _full((DB, DB)), _full((DA, CH)), HBM],
        out_specs=[rev(D), _full((8, D)), _full((1, DIN)), _full((8, DA)), _full((CH, CH)),
                   _full((NH * CH, CH)), _full((HALO, DB)), HBM, HBM],
        out_shape=[jax.ShapeDtypeStruct((T, D), F32), jax.ShapeDtypeStruct((8, D), F32),
                   jax.ShapeDtypeStruct((1, DIN), F32), jax.ShapeDtypeStruct((8, DA), F32),
                   jax.ShapeDtypeStruct((CH, CH), F32), jax.ShapeDtypeStruct((NH * CH, CH), F32),
                   jax.ShapeDtypeStruct((HALO, DB), F32),
                   jax.ShapeDtypeStruct((NDEV, D, WIN_B), BF16), jax.ShapeDtypeStruct((D, D), BF16)],
        scratch_shapes=[pltpu.VMEM((tm + HALO, DB), F32), pltpu.VMEM((7, tm + SH_ROWS, DB), F32),
                        pltpu.VMEM((CH, DA), F32), pltpu.VMEM((NDEV, D, WIN_B), F32), pltpu.VMEM((D, D), F32),
                        pltpu.VMEM((D, WIN_B), BF16), pltpu.VMEM((WOB, D), BF16)],
        compiler_params=pltpu.CompilerParams(dimension_semantics=("arbitrary",), vmem_limit_bytes=VMEM_LIMIT),
    )(dx1, x, zvg, mixed, o, hb, yb, gu, dgu, dgv, vhat, rslb, yhat, rsg, mod, g1, win, lng, lnb, wcat, wcat_t, cw,
      gng, gnb, oga, ogb, wout, pm, esel, after)
    return outs[:7], outs[7:]


def _wgrad_rows(a, b, bm, tk, name):
    T, M = a.shape
    N = b.shape[1]
    nk = T // tk

    def body(a_ref, b_ref, o_ref, acc):
        k = pl.program_id(1)

        @pl.when(k == 0)
        def _():
            acc[...] = jnp.zeros((bm, N), F32)

        acc[...] += _dot_tn(a_ref[...], b_ref[...])

        @pl.when(k == nk - 1)
        def _():
            o_ref[...] = acc[...].astype(BF16)

    return pl.pallas_call(
        body, name=name, grid=(M // bm, nk),
        in_specs=[pl.BlockSpec((tk, bm), lambda j, k: (k, j)), pl.BlockSpec((tk, N), lambda j, k: (k, 0))],
        out_specs=pl.BlockSpec((bm, N), lambda j, k: (j, 0)),
        out_shape=jax.ShapeDtypeStruct((M, N), BF16),
        scratch_shapes=[pltpu.VMEM((bm, N), F32)],
        compiler_params=pltpu.CompilerParams(dimension_semantics=("arbitrary", "arbitrary"),
                                             vmem_limit_bytes=VMEM_LIMIT),
    )(a, b)


def _small_copy(src, dst, ss, rs, k, to):
    return pltpu.make_async_remote_copy(src_ref=src, dst_ref=dst, send_sem=ss.at[k], recv_sem=rs.at[k],
                                        device_id=to, device_id_type=MESH)


def _gather(c_row, ada_w, ada_b8, ada_f_w, ada_f_b8, conv_s, shards):
    nw = len(shards)

    def body(c_ref, adaw_ref, adab_ref, adafw_ref, adafb_ref, conv_ref, *rest):
        w_in = rest[:nw]
        call_ref, cparts_ref, cfparts_ref, convg_ref = rest[nw:nw + 4]
        w_out = rest[nw + 4:2 * nw + 4]
        part_s, partf_s, wss, wrs, lsem, s1, r1, s2, r2, s3, r3, s4, r4 = rest[2 * nw + 4:]
        x, y, c, idx = _place()
        me = (x, y, c)
        ag = _AllGather(w_in, w_out, wss, wrs, lsem)
        ag.start()
        call_ref[pl.ds(idx, 1), :] = c_ref[...]
        convg_ref[idx] = conv_ref[...]
        ph1 = []
        for k in range(1, NDEV):
            to = _dev(idx ^ k)
            ph1.append(_small_copy(c_ref, call_ref.at[pl.ds(idx, 1)], s1, r1, k - 1, to))
            ph1.append(_small_copy(conv_ref, convg_ref.at[idx], s2, r2, k - 1, to))
        for cp in ph1:
            cp.start()
        for k in range(1, NDEV):
            src_dev = idx ^ k
            _small_copy(c_ref, call_ref.at[pl.ds(src_dev, 1)], s1, r1, k - 1, me).wait_recv()
            _small_copy(conv_ref, convg_ref.at[src_dev], s2, r2, k - 1, me).wait_recv()
        call = call_ref[...]
        cact = (call * _sig(call))
        part_s[...] = jnp.dot(cact, adaw_ref[...], preferred_element_type=F32,
                              precision=lax.Precision.HIGHEST) + adab_ref[pl.ds(idx, 1), :]
        partf_s[...] = jnp.dot(cact, adafw_ref[...], preferred_element_type=F32,
                               precision=lax.Precision.HIGHEST) + adafb_ref[pl.ds(idx, 1), :]
        cparts_ref[pl.ds(idx, 1), :] = part_s[pl.ds(idx, 1), :]
        cfparts_ref[pl.ds(idx, 1), :] = partf_s[pl.ds(idx, 1), :]
        ph2 = []
        for k in range(1, NDEV):
            t = idx ^ k
            ph2.append(_small_copy(part_s.at[pl.ds(t, 1)], cparts_ref.at[pl.ds(idx, 1)], s3, r3, k - 1, _dev(t)))
            ph2.append(_small_copy(partf_s.at[pl.ds(t, 1)], cfparts_ref.at[pl.ds(idx, 1)], s4, r4, k - 1, _dev(t)))
        for cp in ph2:
            cp.start()
        for k in range(1, NDEV):
            src_dev = idx ^ k
            _small_copy(part_s.at[pl.ds(0, 1)], cparts_ref.at[pl.ds(src_dev, 1)], s3, r3, k - 1, me).wait_recv()
            _small_copy(partf_s.at[pl.ds(0, 1)], cfparts_ref.at[pl.ds(src_dev, 1)], s4, r4, k - 1, me).wait_recv()
        for cp in ph1 + ph2:
            cp.wait_send()
        ag.forward()
        ag.finish()

    dma7 = pltpu.SemaphoreType.DMA((NDEV - 1,))
    outs = pl.pallas_call(
        body,
        name="gather_weights",
        in_specs=[VM] * 6 + [HBM] * nw,
        out_specs=[VM] * 4 + [HBM] * nw,
        out_shape=[jax.ShapeDtypeStruct((NDEV, D), F32), jax.ShapeDtypeStruct((NDEV, ada_w.shape[1]), F32),
                   jax.ShapeDtypeStruct((NDEV, ada_f_w.shape[1]), F32),
                   jax.ShapeDtypeStruct((NDEV,) + conv_s.shape, F32)]
                  + [jax.ShapeDtypeStruct((NDEV,) + s.shape, s.dtype) for s in shards],
        scratch_shapes=[pltpu.VMEM((NDEV, ada_w.shape[1]), F32), pltpu.VMEM((NDEV, ada_f_w.shape[1]), F32)]
                       + AG_SEMS(nw) + [dma7] * 8,
        compiler_params=pltpu.CompilerParams(vmem_limit_bytes=VMEM_LIMIT),
    )(c_row, ada_w, ada_b8, ada_f_w, ada_f_b8, conv_s, *shards)
    return outs[0], outs[1], outs[2], outs[3], outs[4:]


_VEC_AT = {
    "norm1_g": (8, 0, D), "a_ln_g": (11, 0, DA), "a_ln_b": (11, DA, DA), "a_spatial_b": (12, 0, D),
    "b_conv_b": (13, 0, DB), "b_gn_g": (13, DB, DB), "b_gn_b": (14, 0, DB), "out_norm_a_g": (14, DB, DA),
    "out_norm_b_g": (15, 0, DB), "norm2_g": (16, 0, D), "norm_f_g": (17, 0, D),
}
_LOSS_ROW = 18
_CW_ROW = 24


def _reduce_small(acc_f, acc_v, acc_b, acc_a, acc_bs, acc_cw, dws, after):
    def body(accf_ref, accv_ref, accb_ref, acca_ref, accbs_ref, acccw_ref, dws_ref, after_ref,
             vsum_ref, dcond_ref, wssum_ref, vloc, vbuf, wbuf, wown, s1, r1, s2, r2, s3, r3):
        x, y, c, idx = _place()
        me = (x, y, c)
        vloc[...] = jnp.zeros((NVEC, D), F32)
        vloc[0:1, :] = accv_ref[1:2, :]
        vloc[1:2, :] = accv_ref[2:3, :]
        vloc[2:3, :] = accv_ref[0:1, :]
        vloc[3:4, :] = accf_ref[4:5, :]
        vloc[4:5, :] = accf_ref[5:6, :]
        vloc[5:6, :] = accf_ref[3:4, :]
        vloc[6:7, :] = accf_ref[0:1, :]
        vloc[7:8, :] = accf_ref[1:2, :]
        vloc[8:9, :] = accv_ref[3:4, :]
        vloc[9:10, :] = accb_ref[:, 0:D]
        vloc[10:11, :] = accb_ref[:, D:]
        vloc[11:12, 0:DA] = acca_ref[0:1, :]
        vloc[11:12, DA:] = acca_ref[1:2, :]
        bst = accbs_ref[...].T
        for h in range(NH):
            vloc[12:13, h * CH:(h + 1) * CH] = bst[h:h + 1, :]
        vloc[13:14, 0:DB] = acca_ref[4:5, :]
        vloc[13:14, DB:] = acca_ref[5:6, :]
        vloc[14:15, 0:DB] = acca_ref[6:7, :]
        vloc[14:15, DB:] = acca_ref[2:3, :]
        vloc[15:16, 0:DB] = acca_ref[3:4, :]
        vloc[16:17, :] = accf_ref[6:7, :]
        vloc[17:18, :] = accf_ref[2:3, :]
        vloc[_LOSS_ROW:_LOSS_ROW + 1, :] = accf_ref[7:8, :]
        vloc[_CW_ROW:_CW_ROW + HALO // 2, 0:DB] = acccw_ref[0:HALO // 2, :]
        vloc[_CW_ROW:_CW_ROW + HALO // 2, DB:] = acccw_ref[HALO // 2:, :]
        vbuf[idx] = vloc[...]
        rows_of = lambda t: pl.ds(pl.multiple_of(t * CH, CH), CH)
        wbuf[0] = dws_ref[rows_of(idx), :]
        sm = []
        for k in range(1, NDEV):
            t = idx ^ k
            sm.append(_small_copy(vloc, vbuf.at[idx], s1, r1, k - 1, _dev(t)))
            sm.append(_small_copy(dws_ref.at[rows_of(t)], wbuf.at[k], s2, r2, k - 1, _dev(t)))
        for cp in sm:
            cp.start()
        for k in range(1, NDEV):
            _small_copy(dws_ref.at[rows_of(0)], wbuf.at[k], s2, r2, k - 1, me).wait_recv()
        ws = wbuf[0]
        for k in range(1, NDEV):
            ws = ws + wbuf[k]
        wown[...] = ws
        wssum_ref[rows_of(idx), :] = ws
        ag = [_small_copy(wown, wssum_ref.at[rows_of(idx)], s3, r3, k - 1, _dev(idx ^ k)) for k in range(1, NDEV)]
        for cp in ag:
            cp.start()
        for k in range(1, NDEV):
            _small_copy(vloc, vbuf.at[idx ^ k], s1, r1, k - 1, me).wait_recv()
        vs = vbuf[0]
        for d in range(1, NDEV):
            vs = vs + vbuf[d]
        vsum_ref[...] = vs
        for d in range(NDEV):
            dcond_ref[d] = vbuf[d, 0:8, :]
        for k in range(1, NDEV):
            _small_copy(wown, wssum_ref.at[rows_of(idx ^ k)], s3, r3, k - 1, me).wait_recv()
        for cp in sm + ag:
            cp.wait_send()

    dma7 = pltpu.SemaphoreType.DMA((NDEV - 1,))
    return pl.pallas_call(
        body,
        name="reduce_small",
        in_specs=[VM] * 7 + [HBM],
        out_specs=[VM, VM, VM],
        out_shape=[jax.ShapeDtypeStruct((NVEC, D), F32), jax.ShapeDtypeStruct((NDEV, 8, D), F32),
                   jax.ShapeDtypeStruct(dws.shape, F32)],
        scratch_shapes=[pltpu.VMEM((NVEC, D), F32), pltpu.VMEM((NDEV, NVEC, D), F32),
                        pltpu.VMEM((NDEV, CH, CH), F32), pltpu.VMEM((CH, CH), F32)] + [dma7] * 6,
        compiler_params=pltpu.CompilerParams(vmem_limit_bytes=VMEM_LIMIT),
    )(acc_f, acc_v, acc_b, acc_a, acc_bs, acc_cw, dws, after)


HBM_ONLY = pl.BlockSpec(memory_space=pltpu.HBM)
SEM = pl.BlockSpec(memory_space=pltpu.SEMAPHORE)
EFFECT = pltpu.SideEffectType.DATAFLOW_SIDE_EFFECTING


def _rs_copies(g_refs, land_refs, sems):
    x, y, c, idx = _place()
    cps = []
    for k in range(1, NDEV):
        t = idx ^ k
        for a in range(len(g_refs)):
            n = len(cps)
            cps.append(pltpu.make_async_remote_copy(
                src_ref=g_refs[a].at[t], dst_ref=land_refs[a].at[k - 1], send_sem=sems[2 * n],
                recv_sem=sems[2 * n + 1], device_id=_dev(t), device_id_type=MESH))
    return cps


def _rs_start(grads, name, after=()):
    nw = len(grads)
    nsem = 2 * nw * (NDEV - 1)
    lands = [lax.empty((NDEV - 1,) + g.shape[1:], g.dtype) for g in grads]

    def body(*refs):
        g_refs, land_refs = refs[:nw], refs[nw:2 * nw]
        sems = refs[2 * nw + len(after):2 * nw + len(after) + nsem]
        token = refs[-1]
        for cp in _rs_copies(g_refs, land_refs, sems):
            cp.start()
        token[...] = jnp.zeros_like(token)

    outs = pl.pallas_call(
        body, name=name,
        out_shape=(*[pltpu.SemaphoreType.DMA(())] * nsem,
                   *[pltpu.HBM(g.shape, g.dtype) for g in grads], *[pltpu.HBM(l.shape, l.dtype) for l in lands],
                   jax.ShapeDtypeStruct((8, CH), F32)),
        in_specs=[HBM_ONLY] * (2 * nw) + [HBM] * len(after),
        out_specs=(*[SEM] * nsem, *[HBM_ONLY] * (2 * nw), VM),
        input_output_aliases={i: nsem + i for i in range(2 * nw)},
        compiler_params=pltpu.CompilerParams(has_side_effects=EFFECT),
    )(*[pltpu.with_memory_space_constraint(g, pltpu.HBM) for g in grads],
      *[pltpu.with_memory_space_constraint(l, pltpu.HBM) for l in lands], *after)
    return outs[:nsem], outs[nsem:nsem + nw], outs[nsem + nw:nsem + 2 * nw], outs[-1]


def _rs_wait(sems, g_thru, land_thru, after, name):
    nw = len(g_thru)
    nsem = len(sems)

    def body(*refs):
        g_refs, land_refs = refs[:nw], refs[nw:2 * nw]
        for cp in _rs_copies(g_refs, land_refs, refs[2 * nw:2 * nw + nsem]):
            cp.wait_send()
            cp.wait_recv()

    outs = pl.pallas_call(
        body, name=name,
        out_shape=tuple(pltpu.HBM(a.shape, a.dtype) for a in list(g_thru) + list(land_thru)),
        in_specs=[HBM_ONLY] * (2 * nw) + [SEM] * nsem + [HBM] * len(after),
        out_specs=tuple([HBM_ONLY] * (2 * nw)),
        input_output_aliases={i: i for i in range(2 * nw)},
        compiler_params=pltpu.CompilerParams(has_side_effects=EFFECT),
    )(*g_thru, *land_thru, *sems, *after)
    return outs[:nw], outs[nw:]


def _adamw(w, g, m, v):
    m2 = ADAM_B1 * m + (1.0 - ADAM_B1) * g
    v2 = ADAM_B2 * v + (1.0 - ADAM_B2) * (g * g)
    m_hat = m2 / (1.0 - ADAM_B1 ** ADAM_STEP)
    v_hat = v2 / (1.0 - ADAM_B2 ** ADAM_STEP)
    delta = -ADAM_LR * (m_hat / (jnp.sqrt(v_hat) + ADAM_EPS) + ADAM_WD * w)
    return delta, m2, v2


def _adam_big(r, w, m, v, rb, name, own=None, after=None):
    R, C = w.shape
    ns = r.shape[0]
    g_all, idx1 = own

    def body(idx_ref, r_ref, own_ref, *refs):
        w_ref, m_ref, v_ref, g_ref, d_ref, m2_ref, v2_ref = refs[len(refs) - 7:]
        g = own_ref[0].astype(F32)
        for k in range(ns):
            g = g + r_ref[k].astype(F32)
        g_ref[...] = g
        d_ref[...], m2_ref[...], v2_ref[...] = _adamw(w_ref[...], g, m_ref[...], v_ref[...])

    t2 = pl.BlockSpec((rb, C), lambda i, idx_ref: (i, 0))
    sd = jax.ShapeDtypeStruct((R, C), F32)
    extra_specs = [HBM] if after is not None else []
    extra = [after] if after is not None else []
    return pl.pallas_call(
        body, name=name,
        grid_spec=pltpu.PrefetchScalarGridSpec(
            num_scalar_prefetch=1, grid=(R // rb,),
            in_specs=[pl.BlockSpec((ns, rb, C), lambda i, idx_ref: (0, i, 0)),
                      pl.BlockSpec((1, rb, C), lambda i, idx_ref: (idx_ref[0], i, 0))] + extra_specs + [t2, t2, t2],
            out_specs=[t2, t2, t2, t2]),
        out_shape=[sd, sd, sd, sd],
        compiler_params=pltpu.CompilerParams(dimension_semantics=("arbitrary",), vmem_limit_bytes=VMEM_LIMIT),
    )(idx1, r, g_all, *extra, w, m, v)


def _adam_ada(cact_t, dcs, w, m, v, rb, name):
    R, C = w.shape

    def body(ct_ref, dc_ref, w_ref, m_ref, v_ref, g_ref, d_ref, m2_ref, v2_ref):
        g = jnp.dot(ct_ref[...], dc_ref[...], preferred_element_type=F32, precision=lax.Precision.HIGHEST)
        g_ref[...] = g
        d_ref[...], m2_ref[...], v2_ref[...] = _adamw(w_ref[...], g, m_ref[...], v_ref[...])

    t2 = pl.BlockSpec((rb, C), lambda i: (i, 0))
    sd = jax.ShapeDtypeStruct((R, C), F32)
    return pl.pallas_call(
        body, name=name, grid=(R // rb,),
        in_specs=[pl.BlockSpec((rb, NDEV), lambda i: (i, 0)), _full((NDEV, C)), t2, t2, t2],
        out_specs=[t2, t2, t2, t2], out_shape=[sd, sd, sd, sd],
        compiler_params=pltpu.CompilerParams(dimension_semantics=("arbitrary",), vmem_limit_bytes=VMEM_LIMIT),
    )(cact_t, dcs, w, m, v)


_SMALL = ["ada_b", "ada_f_b", "norm1_g", "b_in", "a_ln_g", "a_ln_b", "a_spatial_b", "b_conv_b", "b_gn_g", "b_gn_b",
          "out_norm_a_g", "out_norm_b_g", "norm2_g", "norm_f_g", "a_spatial_w", "b_conv_w"]


def _adam_small(vsum, wssum, gcw, params):
    names = _SMALL
    flat = []
    for n in names:
        flat += list(params[n])

    def body(vs_ref, ws_ref, gcw_ref, *rest):
        ins = rest[:3 * len(names)]
        outs = rest[3 * len(names):]
        for pi, n in enumerate(names):
            w_ref, m_ref, v_ref = ins[3 * pi:3 * pi + 3]
            g_ref, d_ref, m2_ref, v2_ref = outs[4 * pi:4 * pi + 4]
            if n in ("ada_b", "ada_f_b", "b_in"):
                row0 = {"ada_b": 0, "ada_f_b": 6, "b_in": 9}[n]
                pieces = [(vs_ref[row0 + r:row0 + r + 1, :], slice(r * D, (r + 1) * D))
                          for r in range(w_ref.shape[1] // D)]
            elif n == "a_spatial_w":
                pieces = [(ws_ref[...], slice(None))]
            elif n == "b_conv_w":
                pieces = [(gcw_ref[...], slice(None))]
            else:
                row, off, width = _VEC_AT[n]
                pieces = [(vs_ref[row:row + 1, off:off + width], slice(None))]
            for g, cs in pieces:
                g_ref[:, cs] = g
                d_ref[:, cs], m2_ref[:, cs], v2_ref[:, cs] = _adamw(w_ref[:, cs], g, m_ref[:, cs], v_ref[:, cs])

    out_shape = []
    for n in names:
        out_shape += [jax.ShapeDtypeStruct(params[n][0].shape, F32)] * 4
    outs = pl.pallas_call(
        body, name="adam_small",
        in_specs=[VM] * (3 + len(flat)), out_specs=[VM] * len(out_shape), out_shape=out_shape,
        compiler_params=pltpu.CompilerParams(vmem_limit_bytes=VMEM_LIMIT),
    )(vsum, wssum, gcw, *flat)
    return {n: outs[4 * pi:4 * pi + 4] for pi, n in enumerate(names)}


def _token_tile(T, want):
    return want if T % want == 0 else T


def kernel(x, c, ada_w, ada_b, norm1_g, w_in, b_in, a_ln_g, a_ln_b, a_spatial_w, a_spatial_b, b_conv_w, b_conv_b, b_gn_g, b_gn_b, out_norm_a_g, out_norm_b_g, w_out, norm2_g, w_ffn_in, w_ffn_out, ada_f_w, ada_f_b, norm_f_g, loss_target, m_ada_w, m_ada_b, m_norm1_g, m_w_in, m_b_in, m_a_ln_g, m_a_ln_b, m_a_spatial_w, m_a_spatial_b, m_b_conv_w, m_b_conv_b, m_b_gn_g, m_b_gn_b, m_out_norm_a_g, m_out_norm_b_g, m_w_out, m_norm2_g, m_w_ffn_in, m_w_ffn_out, m_ada_f_w, m_ada_f_b, m_norm_f_g, v_ada_w, v_ada_b, v_norm1_g, v_w_in, v_b_in, v_a_ln_g, v_a_ln_b, v_a_spatial_w, v_a_spatial_b, v_b_conv_w, v_b_conv_b, v_b_gn_g, v_b_gn_b, v_out_norm_a_g, v_out_norm_b_g, v_w_out, v_norm2_g, v_w_ffn_in, v_w_ffn_out, v_ada_f_w, v_ada_f_b, v_norm_f_g):
    T = x.shape[1]
    idx = 4 * lax.axis_index("x") + 2 * lax.axis_index("y") + lax.axis_index("c")
    x2d = x.reshape(T, D)
    tgt = loss_target.reshape(T, D)

    conv_s = jnp.pad(b_conv_w[0], ((0, HALO - KW), (0, 0)))
    call, cparts, cfparts, convg, (win_g,) = _gather(
        c, ada_w[0], ada_b.reshape(NDEV, -1), ada_f_w, ada_f_b.reshape(NDEV, -1), conv_s, [w_in[0].astype(BF16)])
    mod = jnp.concatenate([cparts.reshape(6, D), cfparts.reshape(2, D)], axis=0)
    cw = jnp.transpose(convg, (1, 0, 2)).reshape(HALO, DB)

    tril = jnp.tril(jnp.ones((CH, CH), dtype=bool))
    wsm = jnp.where(tril[None], a_spatial_w[0], 0.0).astype(BF16)
    wcat = wsm.reshape(NH * CH, CH)
    wcat_t = jnp.transpose(wsm, (0, 2, 1)).reshape(NH * CH, CH)
    bsf = jnp.repeat(a_spatial_b[0].T, DA // NH, axis=1)
    lane = jnp.arange(DB)
    pm = jnp.where((lane[:, None] >> 6) == (lane[None, :] >> 6), 1.0 / 64.0, 0.0).astype(BF16)
    esel = jnp.where((lane[:, None] >> 6) == jnp.arange(CH)[None, :], 1.0, 0.0).astype(BF16)

    tm = _token_tile(T, 256)
    tk = _token_tile(T, 2048)
    (x1, hb, zvg, mixed, yb, o, gu, dgelu_u, dgelu_v, vhat, rslb, yhat, rsg), (wfi_g, wfo_g), wout = _mix_fwd(
        x2d, mod, norm1_g, win_g, b_in, a_ln_g, a_ln_b, wcat, bsf, cw, b_conv_b, b_gn_g, b_gn_b, out_norm_a_g,
        out_norm_b_g, w_out[0].astype(BF16), pm, [w_ffn_in[0].T.astype(BF16), w_ffn_out[0].astype(BF16)],
        _token_tile(T, 512))
    dx1, h2b, dgu, act, dxg, acc_f = _ffn(x1, tgt, mod, norm2_g, norm_f_g.reshape(1, D),
                                          wfi_g.reshape(2 * DFF, D), wfo_g.reshape(DFF, D), tm)
    g_wfi = _wgrad_rows(dgu, h2b, 2 * WFI_B, tk, "wgrad_ffn_in").reshape(NDEV, WFI_B, D)
    g_wfo = _wgrad_rows(act, dxg, 2 * WFI_B, tk, "wgrad_ffn_out").reshape(NDEV, DFF // NDEV, D)
    f_sems, f_thru, f_land, f_token = _rs_start([g_wfi, g_wfo], "rs_ffn_start")
    (gx, acc_v, acc_b, acc_a, acc_bs, acc_ws, acc_cw), (g_win, g_wout) = _mix_bwd(
        dx1, x2d, zvg, mixed, o, hb, yb, gu, dgelu_u, dgelu_v, vhat, rslb, yhat, rsg, mod, norm1_g, win_g, a_ln_g,
        a_ln_b, wcat, wcat_t, cw, b_gn_g, b_gn_b, out_norm_a_g, out_norm_b_g, wout, pm, esel, f_token, tm)
    (g_wfi_d, g_wfo_d), (r_wfi, r_wfo) = _rs_wait(f_sems, f_thru, f_land, [acc_v], "rs_ffn_wait")
    g_wout = g_wout.reshape(NDEV, D // NDEV, D)

    vsum, dcond_all, wssum = _reduce_small(acc_f, acc_v, acc_b, acc_a, acc_bs, acc_cw, acc_ws, g_wfi_d)
    sems, g_thru, land_thru, token = _rs_start([g_win, g_wout], "rs_mix_start", after=(vsum,))

    own = lambda g: (g, jnp.reshape(idx, (1,)).astype(jnp.int32))
    res = {}
    res["w_ffn_in"] = tuple(a.T for a in _adam_big(r_wfi, w_ffn_in[0].T, m_w_ffn_in[0].T, v_w_ffn_in[0].T, WFI_B // 4,
                                                   "adam_w_ffn_in", own=own(g_wfi_d), after=token))
    res["w_ffn_out"] = _adam_big(r_wfo, w_ffn_out[0], m_w_ffn_out[0], v_w_ffn_out[0], DFF // NDEV // 2,
                                 "adam_w_ffn_out", own=own(g_wfo_d), after=token)
    cact_t = (call * jax.nn.sigmoid(call)).T
    dcond = dcond_all.reshape(NDEV, 8 * D)
    nada = ada_w.shape[2]
    nadf = ada_f_w.shape[1]
    dcs = lax.dynamic_slice(dcond, (0, idx * nada), (NDEV, nada))
    dcfs = lax.dynamic_slice(dcond, (0, 6 * D + idx * nadf), (NDEV, nadf))
    res["ada_w"] = _adam_ada(cact_t, dcs, ada_w[0], m_ada_w[0], v_ada_w[0], 512, "adam_ada_w")
    res["ada_f_w"] = _adam_ada(cact_t, dcfs, ada_f_w, m_ada_f_w, v_ada_f_w, 512, "adam_ada_f_w")
    ncw = b_conv_w.shape[2]
    gcw = jnp.concatenate([lax.dynamic_slice(vsum, (_CW_ROW, idx * ncw), (HALO // 2, ncw)),
                           lax.dynamic_slice(vsum, (_CW_ROW, DB + idx * ncw), (HALO // 2, ncw))], axis=0)[:KW]
    two = lambda a: a.reshape(1, -1) if a.ndim == 1 else a.reshape(-1, a.shape[-1])
    small_in = {
        "ada_b": (ada_b, m_ada_b, v_ada_b), "ada_f_b": (ada_f_b, m_ada_f_b, v_ada_f_b),
        "norm1_g": (norm1_g, m_norm1_g, v_norm1_g), "b_in": (b_in, m_b_in, v_b_in),
        "a_ln_g": (a_ln_g, m_a_ln_g, v_a_ln_g), "a_ln_b": (a_ln_b, m_a_ln_b, v_a_ln_b),
        "a_spatial_b": (a_spatial_b.reshape(1, D), m_a_spatial_b.reshape(1, D), v_a_spatial_b.reshape(1, D)),
        "b_conv_b": (b_conv_b, m_b_conv_b, v_b_conv_b), "b_gn_g": (b_gn_g, m_b_gn_g, v_b_gn_g),
        "b_gn_b": (b_gn_b, m_b_gn_b, v_b_gn_b), "out_norm_a_g": (out_norm_a_g, m_out_norm_a_g, v_out_norm_a_g),
        "out_norm_b_g": (out_norm_b_g, m_out_norm_b_g, v_out_norm_b_g),
        "norm2_g": (norm2_g, m_norm2_g, v_norm2_g), "norm_f_g": (norm_f_g, m_norm_f_g, v_norm_f_g),
        "a_spatial_w": (a_spatial_w, m_a_spatial_w, v_a_spatial_w),
        "b_conv_w": (b_conv_w[0], m_b_conv_w[0], v_b_conv_w[0]),
    }
    small_in = {n: tuple(two(a) for a in t) for n, t in small_in.items()}
    res.update(_adam_small(vsum, wssum, gcw, small_in))
    (g_win_d, g_wout_d), (r_win, r_wout) = _rs_wait(
        sems, g_thru, land_thru,
        [res["w_ffn_in"][0], res["w_ffn_out"][0], res["ada_w"][0], res["ada_f_w"][0], res["norm_f_g"][0]],
        "rs_mix_wait")
    res["w_in"] = _adam_big(r_win, w_in[0], m_w_in[0], v_w_in[0], 256, "adam_w_in", own=own(g_win_d))
    res["w_out"] = _adam_big(r_wout, w_out[0], m_w_out[0], v_w_out[0], D // NDEV, "adam_w_out", own=own(g_wout_d))

    loss = 0.5 / D * jnp.sum(vsum[_LOSS_ROW])
    shapes = {"ada_w": ada_w, "ada_b": ada_b, "norm1_g": norm1_g, "w_in": w_in, "b_in": b_in, "a_ln_g": a_ln_g,
              "a_ln_b": a_ln_b, "a_spatial_w": a_spatial_w, "a_spatial_b": a_spatial_b, "b_conv_w": b_conv_w,
              "b_conv_b": b_conv_b, "b_gn_g": b_gn_g, "b_gn_b": b_gn_b, "out_norm_a_g": out_norm_a_g,
              "out_norm_b_g": out_norm_b_g, "w_out": w_out, "norm2_g": norm2_g, "w_ffn_in": w_ffn_in,
              "w_ffn_out": w_ffn_out, "ada_f_w": ada_f_w, "ada_f_b": ada_f_b, "norm_f_g": norm_f_g}
    order = list(shapes)
    outs = [loss, gx.reshape(x.shape)]
    for which in range(4):
        outs += [res[n][which].reshape(shapes[n].shape) for n in order]
    return tuple(outs)
```

```python
import math

import jax
import jax.numpy as jnp
from jax import lax
from jax.experimental import pallas as pl
from jax.experimental.pallas import tpu as pltpu

F32 = jnp.float32
BF16 = jnp.bfloat16

D = 1024
DA = 512
DB = 512
DIN = 2048
DFF = 2816
NH = 8
CH = 128
KW = 31
HALO = 32
NDEV = 8
WIN_B = DIN // NDEV
WFI_B = 2 * DFF // NDEV
EPS = 1e-6
NVEC = 40
VMEM_LIMIT = 56 * 1024 * 1024

ADAM_LR, ADAM_B1, ADAM_B2, ADAM_EPS, ADAM_WD, ADAM_STEP = 0.001, 0.9, 0.999, 1e-08, 0.01, 10

MESH = pl.DeviceIdType.MESH


def _dot(a, b):
    return jnp.dot(a, b, preferred_element_type=F32)


def _dot_nt(a, b):
    return lax.dot_general(a, b, (((1,), (1,)), ((), ())), preferred_element_type=F32)


def _dot_tn(a, b):
    return lax.dot_general(a, b, (((0,), (0,)), ((), ())), preferred_element_type=F32)


def _rs(v):
    return lax.rsqrt(jnp.mean(v * v, axis=-1, keepdims=True) + EPS)


def _sig(v):
    return 1.0 / (1.0 + jnp.exp(-v))


_INV_SQRT2 = 1.0 / math.sqrt(2.0)
_INV_SQRT2PI = 1.0 / math.sqrt(2.0 * math.pi)


def _gelu_parts(v):
    cdf = 0.5 * (1.0 + lax.erf(v * _INV_SQRT2))
    pdf = jnp.exp(-0.5 * v * v) * _INV_SQRT2PI
    return v * cdf, cdf + v * pdf


def _grp_mean(v, pm):
    hi = v.astype(BF16)
    lo = (v - hi.astype(F32)).astype(BF16)
    return _dot(hi, pm) + _dot(lo, pm)


def _colsum(v):
    return jnp.sum(v, axis=0, keepdims=True)


def _full(shape):
    nd = len(shape)
    return pl.BlockSpec(shape, lambda *_: (0,) * nd)


def _resident(shape):
    nd = len(shape)
    return pl.BlockSpec(shape, lambda *_: (0,) * nd, pipeline_mode=pl.Buffered(1))


HBM = pl.BlockSpec(memory_space=pl.ANY)
VM = pl.BlockSpec(memory_space=pltpu.VMEM)


SH_ROWS = HALO - 8


def _shifted_copies(buf, shbuf, tm):
    for b in range(1, 8):
        shbuf[b - 1] = buf[b:b + tm + SH_ROWS, :]


def _window(buf, shbuf, off, tm):
    a, b = divmod(off, 8)
    if b == 0:
        return buf[8 * a:8 * a + tm, :]
    return shbuf[b - 1, 8 * a:8 * a + tm, :]


def _first_head_lanes():
    return lax.broadcasted_iota(jnp.int32, (CH, CH), 1) < (DA // NH)


def _mix_heads(w_ref, vb, first):
    outs = []
    for p in range(NH // 2):
        v = vb[:, p * CH:(p + 1) * CH]
        a = _dot(w_ref[(2 * p) * CH:(2 * p + 1) * CH, :], v)
        b = _dot(w_ref[(2 * p + 1) * CH:(2 * p + 2) * CH, :], v)
        outs.append(jnp.where(first, a, b))
    return jnp.concatenate(outs, axis=1)


def _place():
    x, y, c = lax.axis_index("x"), lax.axis_index("y"), lax.axis_index("c")
    return x, y, c, 4 * x + 2 * y + c


def _dev(t):
    return (t >> 2, (t >> 1) & 1, t & 1)


class _AllGather:
    def __init__(self, w_in, w_out, wss, wrs, lsem, slot=lambda ref, b: ref.at[b]):
        x, y, c, idx = _place()
        me, sibling = (x, y, c), (x, y, 1 - c)
        chips = [(1 - x, y), (x, 1 - y), (1 - x, 1 - y)]
        nw = len(w_in)

        def blk(p):
            return 4 * p[0] + 2 * p[1] + p[2]

        def wcopy(a, k, block, to, src=None):
            dst = slot(w_out[a], blk(block))
            return pltpu.make_async_remote_copy(src_ref=dst if src is None else src, dst_ref=dst,
                                                send_sem=wss.at[a, k], recv_sem=wrs.at[a, k],
                                                device_id=to, device_id_type=MESH)

        self.mine = [pltpu.make_async_copy(w_in[a], slot(w_out[a], idx), lsem.at[a]) for a in range(nw)]
        self.first = []
        for a in range(nw):
            self.first.append(wcopy(a, 0, me, sibling, src=w_in[a]))
            self.first += [wcopy(a, 1 + j, me, (*chip, c), src=w_in[a]) for j, chip in enumerate(chips)]
        self.landed = [[wcopy(a, 1 + j, (*chip, c), me) for a in range(nw)] for j, chip in enumerate(chips)]
        self.passed = [[wcopy(a, 4 + j, (*chip, c), sibling) for a in range(nw)] for j, chip in enumerate(chips)]
        self.from_sibling = []
        for a in range(nw):
            self.from_sibling.append(wcopy(a, 0, sibling, me))
            self.from_sibling += [wcopy(a, 4 + j, (*chip, 1 - c), me) for j, chip in enumerate(chips)]

    def start(self):
        for cp in self.mine + self.first:
            cp.start()

    def forward(self):
        for land, pas in zip(self.landed, self.passed):
            for l, p in zip(land, pas):
                l.wait_recv()
                p.start()

    def finish(self):
        for cp in self.from_sibling:
            cp.wait_recv()
        for cp in self.first:
            cp.wait_send()
        for pas in self.passed:
            for p in pas:
                p.wait_send()
        for cp in self.mine:
            cp.wait()


AG_SEMS = lambda nw: [pltpu.SemaphoreType.DMA((nw, 7)), pltpu.SemaphoreType.DMA((nw, 7)),
                      pltpu.SemaphoreType.DMA((nw,))]


def _wout_rows(ref, b):
    rows = D // NDEV
    return ref.at[pl.ds(pl.multiple_of(b * rows, rows), rows)]


def _mix_fwd(x, mod, g1, win, b_in, lng, lnb, wcat, bsf, cw, cb, gng, gnb, oga, ogb, wout_s, pm, ffn_shards, tm):
    T = x.shape[0]
    nt = T // tm
    nch = tm // CH
    nw = len(ffn_shards)
    fwd_step = (5 * nt) // 8
    saved = [(D, F32), (D, BF16), (2 * DB, F32), (DA, F32), (D, BF16), (D, F32), (DA, F32), (DA, F32), (DA, F32),
             (DA, F32), (CH, F32), (DB, F32), (DB, F32)]
    NSAVE = len(saved)

    def body(x_ref, mod_ref, g1_ref, win_ref, bin_ref, lng_ref, lnb_ref, wcat_ref, bsf_ref, cw_ref, cb_ref,
             gng_ref, gnb_ref, oga_ref, ogb_ref, wouts_ref, pm_ref, *rest):
        sh_in = rest[:nw]
        (x1_ref, h_ref, zvg_ref, mixed_ref, y_ref, o_ref, gu_ref, dgu_ref, dgv_ref, vhat_ref, rsl_ref, yhat_ref,
         rsg_ref) = rest[nw:nw + NSAVE]
        sh_out = rest[nw + NSAVE:2 * nw + NSAVE]
        woutg_ref = rest[2 * nw + NSAVE]
        glbuf, shbuf, wout_v, wss, wrs, lsem, oss, ors, olsem = rest[2 * nw + NSAVE + 1:]
        i = pl.program_id(0)
        gather_wout = lambda: _AllGather([wouts_ref], [wout_v], oss, ors, olsem, slot=_wout_rows)

        @pl.when(i == 0)
        def _():
            gather_wout().start()

        xv = x_ref[...]
        shift1 = mod_ref[0:1, :]
        scale1 = mod_ref[1:2, :]
        gate1 = mod_ref[2:3, :]
        h = (xv * _rs(xv) * g1_ref[...]) * (1.0 + scale1) + shift1
        hb = h.astype(BF16)
        h_ref[...] = hb
        z = jnp.concatenate([_dot(hb, win_ref[j]) for j in range(NDEV)], axis=1) + bin_ref[...]
        zvg_ref[...] = z[:, 2 * DA:]
        gu, dgelu_u = _gelu_parts(z[:, 0:DA])
        gv, dgelu_v = _gelu_parts(z[:, DA:2 * DA])
        gu_ref[...] = gu
        dgu_ref[...] = dgelu_u
        dgv_ref[...] = dgelu_v
        xc = gv - jnp.mean(gv, axis=-1, keepdims=True)
        rsl = lax.rsqrt(jnp.mean(xc * xc, axis=-1, keepdims=True) + EPS)
        vhat = xc * rsl
        vhat_ref[...] = vhat
        rsl_ref[...] = jnp.broadcast_to(rsl, (tm, CH))
        vnb = (vhat * lng_ref[...] + lnb_ref[...]).astype(BF16)
        first = _first_head_lanes()
        chunks = []
        for ci in range(nch):
            chunks.append(_mix_heads(wcat_ref, vnb[ci * CH:(ci + 1) * CH, :], first) + bsf_ref[...])
        mixed = jnp.concatenate(chunks, axis=0) if nch > 1 else chunks[0]
        mixed_ref[...] = mixed
        ya = gu * mixed
        gl = z[:, 2 * DA:2 * DA + DB] * _sig(z[:, 2 * DA + DB:])

        @pl.when(i == 0)
        def _():
            glbuf[0:HALO, :] = jnp.zeros((HALO, DB), F32)

        glbuf[HALO:HALO + tm, :] = gl
        _shifted_copies(glbuf, shbuf, tm)
        yc = jnp.zeros((tm, DB), F32) + cb_ref[...]
        for k in range(KW):
            yc = yc + cw_ref[k:k + 1, :] * _window(glbuf, shbuf, HALO - (KW - 1) + k, tm)
        glbuf[0:HALO, :] = gl[tm - HALO:, :]
        pmv = pm_ref[...]
        dc = yc - _grp_mean(yc, pmv)
        rsg = lax.rsqrt(_grp_mean(dc * dc, pmv) + EPS)
        yhat = dc * rsg
        yhat_ref[...] = yhat
        rsg_ref[...] = rsg
        yg = yhat * gng_ref[...] + gnb_ref[...]
        yb = yg * _sig(yg)
        na = ya * _rs(ya) * oga_ref[...]
        nb = yb * _rs(yb) * ogb_ref[...]
        yv = jnp.concatenate([na, nb], axis=1).astype(BF16)
        y_ref[...] = yv

        @pl.when(i == 0)
        def _():
            ag = gather_wout()
            ag.forward()
            ag.finish()
            _AllGather(sh_in, sh_out, wss, wrs, lsem).start()

        o = _dot(yv, wout_v[...])
        o_ref[...] = o
        x1_ref[...] = xv + gate1 * o

        @pl.when(i == fwd_step)
        def _():
            _AllGather(sh_in, sh_out, wss, wrs, lsem).forward()

        @pl.when(i == nt - 1)
        def _():
            _AllGather(sh_in, sh_out, wss, wrs, lsem).finish()
            pltpu.sync_copy(wout_v, woutg_ref)

    tile = lambda w: pl.BlockSpec((tm, w), lambda i: (i, 0))
    outs = pl.pallas_call(
        body,
        name="mix_fwd",
        grid=(nt,),
        in_specs=[tile(D), _full((8, D)), _full((1, D)), _resident((NDEV, D, WIN_B)), _full((1, DIN)),
                  _full((1, DA)), _full((1, DA)), _full((NH * CH, CH)), _full((CH, DA)), _full((HALO, DB)),
                  _full((1, DB)), _full((1, DB)), _full((1, DB)), _full((1, DA)), _full((1, DB)),
                  _full((D // NDEV, D)), _full((DB, DB))] + [HBM] * nw,
        out_specs=[tile(w) for w, _ in saved] + [HBM] * (nw + 1),
        out_shape=[jax.ShapeDtypeStruct((T, w), dt) for w, dt in saved]
                  + [jax.ShapeDtypeStruct((NDEV,) + s.shape, s.dtype) for s in ffn_shards]
                  + [jax.ShapeDtypeStruct((D, D), BF16)],
        scratch_shapes=[pltpu.VMEM((HALO + tm, DB), F32), pltpu.VMEM((7, tm + SH_ROWS, DB), F32),
                        pltpu.VMEM((D, D), BF16)] + AG_SEMS(nw) + AG_SEMS(1),
        compiler_params=pltpu.CompilerParams(dimension_semantics=("arbitrary",), vmem_limit_bytes=VMEM_LIMIT),
    )(x, mod, g1, win, b_in, lng, lnb, wcat, bsf, cw, cb, gng, gnb, oga, ogb, wout_s, pm, *ffn_shards)
    return outs[:NSAVE], outs[NSAVE:NSAVE + nw], outs[NSAVE + nw]


FF_BLOCKS = ((0, 1024), (1024, 1024), (2048, 768))


def _ffn(x1, tgt, mod, g2, gf, wfi_t, wfo, tm):
    T = x1.shape[0]
    nt = T // tm

    def body(x1_ref, tgt_ref, mod_ref, g2_ref, gf_ref, wfi_ref, wfo_ref,
             dx1_ref, h2_ref, dgu_ref, act_ref, dxg_ref, acc_ref, g_s, u_s):
        i = pl.program_id(0)

        @pl.when(i == 0)
        def _():
            acc_ref[...] = jnp.zeros((8, D), F32)

        x1 = x1_ref[...]
        shift2 = mod_ref[3:4, :]
        scale2 = mod_ref[4:5, :]
        gate2 = mod_ref[5:6, :]
        shiftf = mod_ref[6:7, :]
        scalef = mod_ref[7:8, :]
        g2v = g2_ref[...]
        gfv = gf_ref[...]
        r2 = _rs(x1)
        xn2 = x1 * r2
        h2b = (xn2 * g2v * (1.0 + scale2) + shift2).astype(BF16)
        h2_ref[...] = h2b
        f = jnp.zeros((tm, D), F32)
        for o, w in FF_BLOCKS:
            g = _dot_nt(h2b, wfi_ref[o:o + w, :])
            u = _dot_nt(h2b, wfi_ref[DFF + o:DFF + o + w, :])
            g_s[:, o:o + w] = g
            u_s[:, o:o + w] = u
            actb = (g * _sig(g) * u).astype(BF16)
            act_ref[:, o:o + w] = actb
            f = f + _dot(actb, wfo_ref[o:o + w, :])
        x2 = x1 + gate2 * f
        rf = _rs(x2)
        xnf = x2 * rf
        out = xnf * gfv * (1.0 + scalef) + shiftf
        e = out - tgt_ref[...]
        dout = e * (1.0 / D)
        acc_ref[7:8, :] += _colsum(e * e)
        acc_ref[0:1, :] += _colsum(dout)
        acc_ref[1:2, :] += _colsum(dout * xnf * gfv)
        acc_ref[2:3, :] += _colsum(dout * (1.0 + scalef) * xnf)
        dxnf = dout * (1.0 + scalef) * gfv
        dx2 = rf * (dxnf - xnf * jnp.mean(dxnf * xnf, axis=-1, keepdims=True))
        acc_ref[3:4, :] += _colsum(dx2 * f)
        dxgb = (dx2 * gate2).astype(BF16)
        dxg_ref[...] = dxgb
        dh2 = jnp.zeros((tm, D), F32)
        for o, w in FF_BLOCKS:
            dact = _dot_nt(dxgb, wfo_ref[o:o + w, :])
            g = g_s[:, o:o + w]
            u = u_s[:, o:o + w]
            s = _sig(g)
            dgb = (dact * u * (s * (1.0 + g * (1.0 - s)))).astype(BF16)
            dub = (dact * (g * s)).astype(BF16)
            dgu_ref[:, o:o + w] = dgb
            dgu_ref[:, DFF + o:DFF + o + w] = dub
            dh2 = dh2 + _dot(dgb, wfi_ref[o:o + w, :])
            dh2 = dh2 + _dot(dub, wfi_ref[DFF + o:DFF + o + w, :])
        acc_ref[4:5, :] += _colsum(dh2)
        acc_ref[5:6, :] += _colsum(dh2 * xn2 * g2v)
        acc_ref[6:7, :] += _colsum(dh2 * (1.0 + scale2) * xn2)
        dxn2 = dh2 * (1.0 + scale2) * g2v
        dx1_ref[...] = dx2 + r2 * (dxn2 - xn2 * jnp.mean(dxn2 * xn2, axis=-1, keepdims=True))

    tile = lambda w: pl.BlockSpec((tm, w), lambda i: (i, 0))
    return pl.pallas_call(
        body,
        name="ffn_fwd_bwd",
        grid=(nt,),
        in_specs=[tile(D), tile(D), _full((8, D)), _full((1, D)), _full((1, D)),
                  _resident((2 * DFF, D)), _resident((DFF, D))],
        out_specs=[tile(D), tile(D), tile(2 * DFF), tile(DFF), tile(D), _full((8, D))],
        out_shape=[jax.ShapeDtypeStruct((T, D), F32), jax.ShapeDtypeStruct((T, D), BF16),
                   jax.ShapeDtypeStruct((T, 2 * DFF), BF16), jax.ShapeDtypeStruct((T, DFF), BF16),
                   jax.ShapeDtypeStruct((T, D), BF16), jax.ShapeDtypeStruct((8, D), F32)],
        scratch_shapes=[pltpu.VMEM((tm, DFF), F32), pltpu.VMEM((tm, DFF), F32)],
        compiler_params=pltpu.CompilerParams(dimension_semantics=("arbitrary",), vmem_limit_bytes=VMEM_LIMIT),
    )(x1, tgt, mod, g2, gf, wfi_t, wfo)


def _mix_bwd(dx1, x, zvg, mixed, o, hb, yb, gu, dgu, dgv, vhat, rslb, yhat, rsg, mod, g1, win, lng, lnb, wcat, wcat_t,
             cw, gng, gnb, oga, ogb, wout, pm, esel, after, tm):
    T = x.shape[0]
    nt = T // tm
    nch = tm // CH
    WOB = 256

    def body(dx1_ref, x_ref, zvg_ref, mixed_ref, o_ref, hb_ref, yb_ref, gu_ref, dgu_ref, dgv_ref, vhat_ref, rsl_ref,
             yhat_ref, rsg_ref, mod_ref, g1_ref, win_ref, lng_ref, lnb_ref, wcat_ref, wcatt_ref, cw_ref, gng_ref,
             gnb_ref, oga_ref, ogb_ref, wout_ref, pm_ref, esel_ref, after_ref,
             gx_ref, accv_ref, accb_ref, acca_ref, accbs_ref, accws_ref, acccw_ref, gwin_ref, gwout_ref,
             dycbuf, shbuf, bs_s, acc_win, acc_wout, st_win, st_wout):
        i = pl.program_id(0)

        @pl.when(i == 0)
        def _():
            acc_win[...] = jnp.zeros((NDEV, D, WIN_B), F32)
            acc_wout[...] = jnp.zeros((D, D), F32)
            accv_ref[...] = jnp.zeros((8, D), F32)
            accb_ref[...] = jnp.zeros((1, DIN), F32)
            acca_ref[...] = jnp.zeros((8, DA), F32)
            accws_ref[...] = jnp.zeros((NH * CH, CH), F32)
            acccw_ref[...] = jnp.zeros((HALO, DB), F32)
            bs_s[...] = jnp.zeros((CH, DA), F32)
            dycbuf[tm:tm + HALO, :] = jnp.zeros((HALO, DB), F32)

        shift1 = mod_ref[0:1, :]
        scale1 = mod_ref[1:2, :]
        gate1 = mod_ref[2:3, :]
        g1v = g1_ref[...]
        xv = x_ref[...]
        r1 = _rs(xv)
        xn1 = xv * r1
        val = zvg_ref[:, 0:DB]
        gate = zvg_ref[:, DB:]
        gu = gu_ref[...]
        dgelu_u = dgu_ref[...]
        dgelu_v = dgv_ref[...]
        vhat = vhat_ref[...]
        rsl = rsl_ref[:, 0:1]
        lngv = lng_ref[...]
        vnb = (vhat * lngv + lnb_ref[...]).astype(BF16)
        mixed = mixed_ref[...]
        ya = gu * mixed
        ra = _rs(ya)
        yan = ya * ra
        sgt = _sig(gate)
        gl = val * sgt
        pmv = pm_ref[...]
        rsg = rsg_ref[...]
        yhat = yhat_ref[...]
        gngv = gng_ref[...]
        yg = yhat * gngv + gnb_ref[...]
        sgy = _sig(yg)
        yb = yg * sgy
        rb = _rs(yb)
        ybn = yb * rb
        dx1 = dx1_ref[...]
        accv_ref[0:1, :] += _colsum(dx1 * o_ref[...])
        dogb = (dx1 * gate1).astype(BF16)
        acc_wout[...] += _dot_tn(yb_ref[...], dogb)
        dy = _dot_nt(dogb, wout_ref[...])
        dna = dy[:, 0:DA]
        dnb = dy[:, DA:]
        ogav = oga_ref[...]
        ogbv = ogb_ref[...]
        acca_ref[2:3, :] += _colsum(dna * yan)
        acca_ref[3:4, :] += _colsum(dnb * ybn)
        ta = dna * ogav
        dya = ra * (ta - yan * jnp.mean(ta * yan, axis=-1, keepdims=True))
        tb = dnb * ogbv
        dyb = rb * (tb - ybn * jnp.mean(tb * ybn, axis=-1, keepdims=True))
        dgu = dya * mixed
        dm = dya * gu
        first = _first_head_lanes()
        zero = jnp.zeros((CH, CH), BF16)
        dvn_chunks = []
        bs_acc = bs_s[...]
        for ci in range(nch):
            dmc = dm[ci * CH:(ci + 1) * CH, :]
            bs_acc = bs_acc + dmc
            dmcb = dmc.astype(BF16)
            dvn_chunks.append(_mix_heads(wcatt_ref, dmcb, first))
            vc = vnb[ci * CH:(ci + 1) * CH, :]
            for p in range(NH // 2):
                xt = dmcb[:, p * CH:(p + 1) * CH]
                vt = vc[:, p * CH:(p + 1) * CH]
                accws_ref[(2 * p) * CH:(2 * p + 1) * CH, :] += _dot_nt(jnp.where(first, xt, zero), vt)
                accws_ref[(2 * p + 1) * CH:(2 * p + 2) * CH, :] += _dot_nt(jnp.where(first, zero, xt), vt)
        bs_s[...] = bs_acc
        dvn = jnp.concatenate(dvn_chunks, axis=0) if nch > 1 else dvn_chunks[0]
        acca_ref[0:1, :] += _colsum(dvn * vhat)
        acca_ref[1:2, :] += _colsum(dvn)
        dvh = dvn * lngv
        dgv = rsl * (dvh - jnp.mean(dvh, axis=-1, keepdims=True)
                     - vhat * jnp.mean(dvh * vhat, axis=-1, keepdims=True))
        du = dgu * dgelu_u
        dv = dgv * dgelu_v
        dyg = dyb * (sgy * (1.0 + yg * (1.0 - sgy)))
        acca_ref[5:6, :] += _colsum(dyg * yhat)
        acca_ref[6:7, :] += _colsum(dyg)
        dyh = dyg * gngv
        dyc = rsg * (dyh - _grp_mean(dyh, pmv) - yhat * _grp_mean(dyh * yhat, pmv))
        acca_ref[4:5, :] += _colsum(dyc)
        dycbuf[0:tm, :] = dyc
        _shifted_copies(dycbuf, shbuf, tm)
        dgl = jnp.zeros((tm, DB), F32)
        for k in range(KW):
            win_k = _window(dycbuf, shbuf, KW - 1 - k, tm)
            dgl = dgl + cw_ref[k:k + 1, :] * win_k
            acccw_ref[k:k + 1, :] += _colsum(win_k * gl)
        dycbuf[tm:tm + HALO, :] = dyc[0:HALO, :]
        dval = dgl * sgt
        dgate = dgl * val * sgt * (1.0 - sgt)
        dz = jnp.concatenate([du, dv, dval, dgate], axis=1)
        accb_ref[...] += _colsum(dz)
        dzb = dz.astype(BF16)
        hbv = hb_ref[...]
        dh = jnp.zeros((tm, D), F32)
        for j in range(NDEV):
            dzj = dzb[:, j * WIN_B:(j + 1) * WIN_B]
            acc_win[j] += _dot_tn(hbv, dzj)
            dh = dh + _dot_nt(dzj, win_ref[j])
        accv_ref[1:2, :] += _colsum(dh)
        dh_xn = _colsum(dh * xn1)
        accv_ref[2:3, :] += dh_xn * g1v
        accv_ref[3:4, :] += dh_xn * (1.0 + scale1)
        dxn1 = dh * (1.0 + scale1) * g1v
        gx_ref[...] = dx1 + r1 * (dxn1 - xn1 * jnp.mean(dxn1 * xn1, axis=-1, keepdims=True))

        @pl.when(i == nt - 1)
        def _():
            rows = lax.broadcasted_iota(jnp.int32, (NH * CH, CH), 0) & (CH - 1)
            cols = lax.broadcasted_iota(jnp.int32, (NH * CH, CH), 1)
            accws_ref[...] = jnp.where(cols <= rows, accws_ref[...], 0.0)
            bs = bs_s[...]
            hi = bs.astype(BF16)
            r1_ = bs - hi.astype(F32)
            mid = r1_.astype(BF16)
            lo = (r1_ - mid.astype(F32)).astype(BF16)
            ev = esel_ref[...]
            accbs_ref[...] = _dot(hi, ev) + _dot(mid, ev) + _dot(lo, ev)
            for j in range(NDEV):
                st_win[...] = acc_win[j].astype(BF16)
                pltpu.sync_copy(st_win, gwin_ref.at[j])
            for j in range(D // WOB):
                st_wout[...] = acc_wout[j * WOB:(j + 1) * WOB, :].astype(BF16)
                pltpu.sync_copy(st_wout, gwout_ref.at[pl.ds(j * WOB, WOB)])

    rev = lambda w: pl.BlockSpec((tm, w), lambda i: (nt - 1 - i, 0))
    outs = pl.pallas_call(
        body,
        name="mix_bwd",
        grid=(nt,),
        in_specs=[rev(D), rev(D), rev(2 * DB), rev(DA), rev(D), rev(D), rev(D), rev(DA), rev(DA), rev(DA), rev(DA),
                  rev(CH), rev(DB), rev(DB), _full((8, D)), _full((1, D)),
                  _resident((NDEV, D, WIN_B)), _full((1, DA)), _full((1, DA)), _full((NH * CH, CH)),
                  _full((NH * CH, CH)), _full((HALO, DB)), _full((1, DB)), _full((1, DB)), _full((1, DA)),
                  _full((1, DB)), _resident((D, D)), _full((DB, DB)), _full((DA, CH)), HBM],
        out_specs=[rev(D), _full((8, D)), _full((1, DIN)), _full((8, DA)), _full((CH, CH)),
                   _full((NH * CH, CH)), _full((HALO, DB)), HBM, HBM],
        out_shape=[jax.ShapeDtypeStruct((T, D), F32), jax.ShapeDtypeStruct((8, D), F32),
                   jax.ShapeDtypeStruct((1, DIN), F32), jax.ShapeDtypeStruct((8, DA), F32),
                   jax.ShapeDtypeStruct((CH, CH), F32), jax.ShapeDtypeStruct((NH * CH, CH), F32),
                   jax.ShapeDtypeStruct((HALO, DB), F32),
                   jax.ShapeDtypeStruct((NDEV, D, WIN_B), BF16), jax.ShapeDtypeStruct((D, D), BF16)],
        scratch_shapes=[pltpu.VMEM((tm + HALO, DB), F32), pltpu.VMEM((7, tm + SH_ROWS, DB), F32),
                        pltpu.VMEM((CH, DA), F32), pltpu.VMEM((NDEV, D, WIN_B), F32), pltpu.VMEM((D, D), F32),
                        pltpu.VMEM((D, WIN_B), BF16), pltpu.VMEM((WOB, D), BF16)],
        compiler_params=pltpu.CompilerParams(dimension_semantics=("arbitrary",), vmem_limit_bytes=VMEM_LIMIT),
    )(dx1, x, zvg, mixed, o, hb, yb, gu, dgu, dgv, vhat, rslb, yhat, rsg, mod, g1, win, lng, lnb, wcat, wcat_t, cw,
      gng, gnb, oga, ogb, wout, pm, esel, after)
    return outs[:7], outs[7:]


def _wgrad_rows(a, b, bm, tk, name):
    T, M = a.shape
    N = b.shape[1]
    nk = T // tk

    def body(a_ref, b_ref, o_ref, acc):
        k = pl.program_id(1)

        @pl.when(k == 0)
        def _():
            acc[...] = jnp.zeros((bm, N), F32)

        acc[...] += _dot_tn(a_ref[...], b_ref[...])

        @pl.when(k == nk - 1)
        def _():
            o_ref[...] = acc[...].astype(BF16)

    return pl.pallas_call(
        body, name=name, grid=(M // bm, nk),
        in_specs=[pl.BlockSpec((tk, bm), lambda j, k: (k, j)), pl.BlockSpec((tk, N), lambda j, k: (k, 0))],
        out_specs=pl.BlockSpec((bm, N), lambda j, k: (j, 0)),
        out_shape=jax.ShapeDtypeStruct((M, N), BF16),
        scratch_shapes=[pltpu.VMEM((bm, N), F32)],
        compiler_params=pltpu.CompilerParams(dimension_semantics=("arbitrary", "arbitrary"),
                                             vmem_limit_bytes=VMEM_LIMIT),
    )(a, b)


def _small_copy(src, dst, ss, rs, k, to):
    return pltpu.make_async_remote_copy(src_ref=src, dst_ref=dst, send_sem=ss.at[k], recv_sem=rs.at[k],
                                        device_id=to, device_id_type=MESH)


def _gather(c_row, ada_w, ada_b8, ada_f_w, ada_f_b8, conv_s, shards):
    nw = len(shards)

    def body(c_ref, adaw_ref, adab_ref, adafw_ref, adafb_ref, conv_ref, *rest):
        w_in = rest[:nw]
        call_ref, cparts_ref, cfparts_ref, convg_ref = rest[nw:nw + 4]
        w_out = rest[nw + 4:2 * nw + 4]
        part_s, partf_s, wss, wrs, lsem, s1, r1, s2, r2, s3, r3, s4, r4 = rest[2 * nw + 4:]
        x, y, c, idx = _place()
        me = (x, y, c)
        ag = _AllGather(w_in, w_out, wss, wrs, lsem)
        ag.start()
        call_ref[pl.ds(idx, 1), :] = c_ref[...]
        convg_ref[idx] = conv_ref[...]
        ph1 = []
        for k in range(1, NDEV):
            to = _dev(idx ^ k)
            ph1.append(_small_copy(c_ref, call_ref.at[pl.ds(idx, 1)], s1, r1, k - 1, to))
            ph1.append(_small_copy(conv_ref, convg_ref.at[idx], s2, r2, k - 1, to))
        for cp in ph1:
            cp.start()
        for k in range(1, NDEV):
            src_dev = idx ^ k
            _small_copy(c_ref, call_ref.at[pl.ds(src_dev, 1)], s1, r1, k - 1, me).wait_recv()
            _small_copy(conv_ref, convg_ref.at[src_dev], s2, r2, k - 1, me).wait_recv()
        call = call_ref[...]
        cact = (call * _sig(call))
        part_s[...] = jnp.dot(cact, adaw_ref[...], preferred_element_type=F32,
                              precision=lax.Precision.HIGHEST) + adab_ref[pl.ds(idx, 1), :]
        partf_s[...] = jnp.dot(cact, adafw_ref[...], preferred_element_type=F32,
                               precision=lax.Precision.HIGHEST) + adafb_ref[pl.ds(idx, 1), :]
        cparts_ref[pl.ds(idx, 1), :] = part_s[pl.ds(idx, 1), :]
        cfparts_ref[pl.ds(idx, 1), :] = partf_s[pl.ds(idx, 1), :]
        ph2 = []
        for k in range(1, NDEV):
            t = idx ^ k
            ph2.append(_small_copy(part_s.at[pl.ds(t, 1)], cparts_ref.at[pl.ds(idx, 1)], s3, r3, k - 1, _dev(t)))
            ph2.append(_small_copy(partf_s.at[pl.ds(t, 1)], cfparts_ref.at[pl.ds(idx, 1)], s4, r4, k - 1, _dev(t)))
        for cp in ph2:
            cp.start()
        for k in range(1, NDEV):
            src_dev = idx ^ k
            _small_copy(part_s.at[pl.ds(0, 1)], cparts_ref.at[pl.ds(src_dev, 1)], s3, r3, k - 1, me).wait_recv()
            _small_copy(partf_s.at[pl.ds(0, 1)], cfparts_ref.at[pl.ds(src_dev, 1)], s4, r4, k - 1, me).wait_recv()
        for cp in ph1 + ph2:
            cp.wait_send()
        ag.forward()
        ag.finish()

    dma7 = pltpu.SemaphoreType.DMA((NDEV - 1,))
    outs = pl.pallas_call(
        body,
        name="gather_weights",
        in_specs=[VM] * 6 + [HBM] * nw,
        out_specs=[VM] * 4 + [HBM] * nw,
        out_shape=[jax.ShapeDtypeStruct((NDEV, D), F32), jax.ShapeDtypeStruct((NDEV, ada_w.shape[1]), F32),
                   jax.ShapeDtypeStruct((NDEV, ada_f_w.shape[1]), F32),
                   jax.ShapeDtypeStruct((NDEV,) + conv_s.shape, F32)]
                  + [jax.ShapeDtypeStruct((NDEV,) + s.shape, s.dtype) for s in shards],
        scratch_shapes=[pltpu.VMEM((NDEV, ada_w.shape[1]), F32), pltpu.VMEM((NDEV, ada_f_w.shape[1]), F32)]
                       + AG_SEMS(nw) + [dma7] * 8,
        compiler_params=pltpu.CompilerParams(vmem_limit_bytes=VMEM_LIMIT),
    )(c_row, ada_w, ada_b8, ada_f_w, ada_f_b8, conv_s, *shards)
    return outs[0], outs[1], outs[2], outs[3], outs[4:]


_VEC_AT = {
    "norm1_g": (8, 0, D), "a_ln_g": (11, 0, DA), "a_ln_b": (11, DA, DA), "a_spatial_b": (12, 0, D),
    "b_conv_b": (13, 0, DB), "b_gn_g": (13, DB, DB), "b_gn_b": (14, 0, DB), "out_norm_a_g": (14, DB, DA),
    "out_norm_b_g": (15, 0, DB), "norm2_g": (16, 0, D), "norm_f_g": (17, 0, D),
}
_LOSS_ROW = 18
_CW_ROW = 24


def _reduce_small(acc_f, acc_v, acc_b, acc_a, acc_bs, acc_cw, dws, after, pair_grads):
    npg = len(pair_grads)

    def body(accf_ref, accv_ref, accb_ref, acca_ref, accbs_ref, acccw_ref, dws_ref, after_ref, *rest):
        pg = rest[:npg]
        vsum_ref, dcond_ref, wssum_ref = rest[npg:npg + 3]
        pq = rest[npg + 3:2 * npg + 3]
        vloc, vbuf, wbuf, wown, s1, r1, s2, r2, s3, r3 = rest[2 * npg + 3:2 * npg + 13]
        pland = rest[2 * npg + 13:3 * npg + 13]
        pstage = rest[3 * npg + 13:4 * npg + 13]
        ps, pr, pls, pss = rest[4 * npg + 13:]
        x, y, c, idx = _place()
        me = (x, y, c)
        sibling = (x, y, 1 - c)
        chips = [(1 - x, y), (x, 1 - y), (1 - x, 1 - y)]
        blk = lambda p: 4 * p[0] + 2 * p[1] + p[2]
        give = [blk((*ch, 1 - c)) for ch in chips] + [blk(sibling)]
        pair = [pltpu.make_async_remote_copy(src_ref=pg[a].at[b], dst_ref=pland[a].at[j], send_sem=ps.at[a, j],
                                             recv_sem=pr.at[a, j], device_id=sibling, device_id_type=MESH)
                for a in range(npg) for j, b in enumerate(give)]
        loads = [pltpu.make_async_copy(pg[a].at[blk((*ch, c))], pstage[a].at[j], pls.at[a, j])
                 for a in range(npg) for j, ch in enumerate(chips)]
        for cp in pair + loads:
            cp.start()
        vloc[...] = jnp.zeros((NVEC, D), F32)
        vloc[0:1, :] = accv_ref[1:2, :]
        vloc[1:2, :] = accv_ref[2:3, :]
        vloc[2:3, :] = accv_ref[0:1, :]
        vloc[3:4, :] = accf_ref[4:5, :]
        vloc[4:5, :] = accf_ref[5:6, :]
        vloc[5:6, :] = accf_ref[3:4, :]
        vloc[6:7, :] = accf_ref[0:1, :]
        vloc[7:8, :] = accf_ref[1:2, :]
        vloc[8:9, :] = accv_ref[3:4, :]
        vloc[9:10, :] = accb_ref[:, 0:D]
        vloc[10:11, :] = accb_ref[:, D:]
        vloc[11:12, 0:DA] = acca_ref[0:1, :]
        vloc[11:12, DA:] = acca_ref[1:2, :]
        bst = accbs_ref[...].T
        for h in range(NH):
            vloc[12:13, h * CH:(h + 1) * CH] = bst[h:h + 1, :]
        vloc[13:14, 0:DB] = acca_ref[4:5, :]
        vloc[13:14, DB:] = acca_ref[5:6, :]
        vloc[14:15, 0:DB] = acca_ref[6:7, :]
        vloc[14:15, DB:] = acca_ref[2:3, :]
        vloc[15:16, 0:DB] = acca_ref[3:4, :]
        vloc[16:17, :] = accf_ref[6:7, :]
        vloc[17:18, :] = accf_ref[2:3, :]
        vloc[_LOSS_ROW:_LOSS_ROW + 1, :] = accf_ref[7:8, :]
        vloc[_CW_ROW:_CW_ROW + HALO // 2, 0:DB] = acccw_ref[0:HALO // 2, :]
        vloc[_CW_ROW:_CW_ROW + HALO // 2, DB:] = acccw_ref[HALO // 2:, :]
        vbuf[idx] = vloc[...]
        rows_of = lambda t: pl.ds(pl.multiple_of(t * CH, CH), CH)
        wbuf[0] = dws_ref[rows_of(idx), :]
        sm = []
        for k in range(1, NDEV):
            t = idx ^ k
            sm.append(_small_copy(vloc, vbuf.at[idx], s1, r1, k - 1, _dev(t)))
            sm.append(_small_copy(dws_ref.at[rows_of(t)], wbuf.at[k], s2, r2, k - 1, _dev(t)))
        for cp in sm:
            cp.start()
        for k in range(1, NDEV):
            _small_copy(dws_ref.at[rows_of(0)], wbuf.at[k], s2, r2, k - 1, me).wait_recv()
        ws = wbuf[0]
        for k in range(1, NDEV):
            ws = ws + wbuf[k]
        wown[...] = ws
        wssum_ref[rows_of(idx), :] = ws
        ag = [_small_copy(wown, wssum_ref.at[rows_of(idx)], s3, r3, k - 1, _dev(idx ^ k)) for k in range(1, NDEV)]
        for cp in ag:
            cp.start()
        for cp in loads:
            cp.wait()
        for cp in pair:
            cp.wait_recv()
        stores = []
        for a in range(npg):
            for j in range(3):
                pstage[a][j] = (pstage[a][j].astype(F32) + pland[a][j].astype(F32)).astype(BF16)
                stores.append(pltpu.make_async_copy(pstage[a].at[j], pq[a].at[j], pss.at[a, j]))
            stores.append(pltpu.make_async_copy(pland[a].at[3], pq[a].at[3], pss.at[a, 3]))
        for cp in stores:
            cp.start()
        for k in range(1, NDEV):
            _small_copy(vloc, vbuf.at[idx ^ k], s1, r1, k - 1, me).wait_recv()
        vs = vbuf[0]
        for d in range(1, NDEV):
            vs = vs + vbuf[d]
        vsum_ref[...] = vs
        for d in range(NDEV):
            dcond_ref[d] = vbuf[d, 0:8, :]
        for k in range(1, NDEV):
            _small_copy(wown, wssum_ref.at[rows_of(idx ^ k)], s3, r3, k - 1, me).wait_recv()
        for cp in sm + ag:
            cp.wait_send()
        for cp in stores:
            cp.wait()
        for cp in pair:
            cp.wait_send()

    dma7 = pltpu.SemaphoreType.DMA((NDEV - 1,))
    dma4 = pltpu.SemaphoreType.DMA((npg, 4))
    outs = pl.pallas_call(
        body,
        name="reduce_small",
        in_specs=[VM] * 7 + [HBM] + [HBM] * npg,
        out_specs=[VM, VM, VM] + [HBM] * npg,
        out_shape=[jax.ShapeDtypeStruct((NVEC, D), F32), jax.ShapeDtypeStruct((NDEV, 8, D), F32),
                   jax.ShapeDtypeStruct(dws.shape, F32)]
                  + [jax.ShapeDtypeStruct((4,) + g.shape[1:], g.dtype) for g in pair_grads],
        scratch_shapes=[pltpu.VMEM((NVEC, D), F32), pltpu.VMEM((NDEV, NVEC, D), F32),
                        pltpu.VMEM((NDEV, CH, CH), F32), pltpu.VMEM((CH, CH), F32)] + [dma7] * 6
                       + [pltpu.VMEM((4,) + g.shape[1:], g.dtype) for g in pair_grads]
                       + [pltpu.VMEM((3,) + g.shape[1:], g.dtype) for g in pair_grads] + [dma4] * 4,
        compiler_params=pltpu.CompilerParams(vmem_limit_bytes=VMEM_LIMIT),
    )(acc_f, acc_v, acc_b, acc_a, acc_bs, acc_cw, dws, after, *pair_grads)
    return outs[0], outs[1], outs[2], outs[3:]


HBM_ONLY = pl.BlockSpec(memory_space=pltpu.HBM)
SEM = pl.BlockSpec(memory_space=pltpu.SEMAPHORE)
EFFECT = pltpu.SideEffectType.DATAFLOW_SIDE_EFFECTING


def _rs_copies(g_refs, land_refs, sems, chips):
    x, y, c, idx = _place()
    if chips:
        routes = [(j, j, (*ch, c)) for j, ch in enumerate([(1 - x, y), (x, 1 - y), (1 - x, 1 - y)])]
    else:
        routes = [(idx ^ k, k - 1, _dev(idx ^ k)) for k in range(1, NDEV)]
    cps = []
    for src, dst, to in routes:
        for a in range(len(g_refs)):
            n = len(cps)
            cps.append(pltpu.make_async_remote_copy(
                src_ref=g_refs[a].at[src], dst_ref=land_refs[a].at[dst], send_sem=sems[2 * n],
                recv_sem=sems[2 * n + 1], device_id=to, device_id_type=MESH))
    return cps


def _rs_start(grads, name, after=(), chips=False):
    nw = len(grads)
    npeer = 3 if chips else NDEV - 1
    nsem = 2 * nw * npeer
    lands = [lax.empty((npeer,) + g.shape[1:], g.dtype) for g in grads]

    def body(*refs):
        g_refs, land_refs = refs[:nw], refs[nw:2 * nw]
        sems = refs[2 * nw + len(after):2 * nw + len(after) + nsem]
        token = refs[-1]
        for cp in _rs_copies(g_refs, land_refs, sems, chips):
            cp.start()
        token[...] = jnp.zeros_like(token)

    outs = pl.pallas_call(
        body, name=name,
        out_shape=(*[pltpu.SemaphoreType.DMA(())] * nsem,
                   *[pltpu.HBM(g.shape, g.dtype) for g in grads], *[pltpu.HBM(l.shape, l.dtype) for l in lands],
                   jax.ShapeDtypeStruct((8, CH), F32)),
        in_specs=[HBM_ONLY] * (2 * nw) + [HBM] * len(after),
        out_specs=(*[SEM] * nsem, *[HBM_ONLY] * (2 * nw), VM),
        input_output_aliases={i: nsem + i for i in range(2 * nw)},
        compiler_params=pltpu.CompilerParams(has_side_effects=EFFECT),
    )(*[pltpu.with_memory_space_constraint(g, pltpu.HBM) for g in grads],
      *[pltpu.with_memory_space_constraint(l, pltpu.HBM) for l in lands], *after)
    return outs[:nsem], outs[nsem:nsem + nw], outs[nsem + nw:nsem + 2 * nw], outs[-1]


def _rs_wait(sems, g_thru, land_thru, after, name, chips=False):
    nw = len(g_thru)
    nsem = len(sems)

    def body(*refs):
        g_refs, land_refs = refs[:nw], refs[nw:2 * nw]
        for cp in _rs_copies(g_refs, land_refs, refs[2 * nw:2 * nw + nsem], chips):
            cp.wait_send()
            cp.wait_recv()

    outs = pl.pallas_call(
        body, name=name,
        out_shape=tuple(pltpu.HBM(a.shape, a.dtype) for a in list(g_thru) + list(land_thru)),
        in_specs=[HBM_ONLY] * (2 * nw) + [SEM] * nsem + [HBM] * len(after),
        out_specs=tuple([HBM_ONLY] * (2 * nw)),
        input_output_aliases={i: i for i in range(2 * nw)},
        compiler_params=pltpu.CompilerParams(has_side_effects=EFFECT),
    )(*g_thru, *land_thru, *sems, *after)
    return outs[:nw], outs[nw:]


def _adamw(w, g, m, v):
    m2 = ADAM_B1 * m + (1.0 - ADAM_B1) * g
    v2 = ADAM_B2 * v + (1.0 - ADAM_B2) * (g * g)
    m_hat = m2 / (1.0 - ADAM_B1 ** ADAM_STEP)
    v_hat = v2 / (1.0 - ADAM_B2 ** ADAM_STEP)
    delta = -ADAM_LR * (m_hat / (jnp.sqrt(v_hat) + ADAM_EPS) + ADAM_WD * w)
    return delta, m2, v2


def _adam_big(r, w, m, v, rb, name, own=None, after=None, sib=None):
    R, C = w.shape
    ns = r.shape[0]
    g_all, idx1 = own

    def body(idx_ref, r_ref, own_ref, *refs):
        w_ref, m_ref, v_ref, g_ref, d_ref, m2_ref, v2_ref = refs[len(refs) - 7:]
        g = own_ref[0].astype(F32)
        if sib is not None:
            g = g + refs[0][0].astype(F32)
        for k in range(ns):
            g = g + r_ref[k].astype(F32)
        g_ref[...] = g
        d_ref[...], m2_ref[...], v2_ref[...] = _adamw(w_ref[...], g, m_ref[...], v_ref[...])

    t2 = pl.BlockSpec((rb, C), lambda i, idx_ref: (i, 0))
    sd = jax.ShapeDtypeStruct((R, C), F32)
    extra_specs = ([pl.BlockSpec((1, rb, C), lambda i, idx_ref: (3, i, 0))] if sib is not None else []) \
        + ([HBM] if after is not None else [])
    extra = ([sib] if sib is not None else []) + ([after] if after is not None else [])
    return pl.pallas_call(
        body, name=name,
        grid_spec=pltpu.PrefetchScalarGridSpec(
            num_scalar_prefetch=1, grid=(R // rb,),
            in_specs=[pl.BlockSpec((ns, rb, C), lambda i, idx_ref: (0, i, 0)),
                      pl.BlockSpec((1, rb, C), lambda i, idx_ref: (idx_ref[0], i, 0))] + extra_specs + [t2, t2, t2],
            out_specs=[t2, t2, t2, t2]),
        out_shape=[sd, sd, sd, sd],
        compiler_params=pltpu.CompilerParams(dimension_semantics=("arbitrary",), vmem_limit_bytes=VMEM_LIMIT),
    )(idx1, r, g_all, *extra, w, m, v)


def _adam_ada(cact_t, dcs, w, m, v, rb, name):
    R, C = w.shape

    def body(ct_ref, dc_ref, w_ref, m_ref, v_ref, g_ref, d_ref, m2_ref, v2_ref):
        g = jnp.dot(ct_ref[...], dc_ref[...], preferred_element_type=F32, precision=lax.Precision.HIGHEST)
        g_ref[...] = g
        d_ref[...], m2_ref[...], v2_ref[...] = _adamw(w_ref[...], g, m_ref[...], v_ref[...])

    t2 = pl.BlockSpec((rb, C), lambda i: (i, 0))
    sd = jax.ShapeDtypeStruct((R, C), F32)
    return pl.pallas_call(
        body, name=name, grid=(R // rb,),
        in_specs=[pl.BlockSpec((rb, NDEV), lambda i: (i, 0)), _full((NDEV, C)), t2, t2, t2],
        out_specs=[t2, t2, t2, t2], out_shape=[sd, sd, sd, sd],
        compiler_params=pltpu.CompilerParams(dimension_semantics=("arbitrary",), vmem_limit_bytes=VMEM_LIMIT),
    )(cact_t, dcs, w, m, v)


_SMALL = ["ada_b", "ada_f_b", "norm1_g", "b_in", "a_ln_g", "a_ln_b", "a_spatial_b", "b_conv_b", "b_gn_g", "b_gn_b",
          "out_norm_a_g", "out_norm_b_g", "norm2_g", "norm_f_g", "a_spatial_w", "b_conv_w"]


def _adam_small(vsum, wssum, gcw, params):
    names = _SMALL
    flat = []
    for n in names:
        flat += list(params[n])

    def body(vs_ref, ws_ref, gcw_ref, *rest):
        ins = rest[:3 * len(names)]
        outs = rest[3 * len(names):]
        for pi, n in enumerate(names):
            w_ref, m_ref, v_ref = ins[3 * pi:3 * pi + 3]
            g_ref, d_ref, m2_ref, v2_ref = outs[4 * pi:4 * pi + 4]
            if n in ("ada_b", "ada_f_b", "b_in"):
                row0 = {"ada_b": 0, "ada_f_b": 6, "b_in": 9}[n]
                pieces = [(vs_ref[row0 + r:row0 + r + 1, :], slice(r * D, (r + 1) * D))
                          for r in range(w_ref.shape[1] // D)]
            elif n == "a_spatial_w":
                pieces = [(ws_ref[...], slice(None))]
            elif n == "b_conv_w":
                pieces = [(gcw_ref[...], slice(None))]
            else:
                row, off, width = _VEC_AT[n]
                pieces = [(vs_ref[row:row + 1, off:off + width], slice(None))]
            for g, cs in pieces:
                g_ref[:, cs] = g
                d_ref[:, cs], m2_ref[:, cs], v2_ref[:, cs] = _adamw(w_ref[:, cs], g, m_ref[:, cs], v_ref[:, cs])

    out_shape = []
    for n in names:
        out_shape += [jax.ShapeDtypeStruct(params[n][0].shape, F32)] * 4
    outs = pl.pallas_call(
        body, name="adam_small",
        in_specs=[VM] * (3 + len(flat)), out_specs=[VM] * len(out_shape), out_shape=out_shape,
        compiler_params=pltpu.CompilerParams(vmem_limit_bytes=VMEM_LIMIT),
    )(vsum, wssum, gcw, *flat)
    return {n: outs[4 * pi:4 * pi + 4] for pi, n in enumerate(names)}


def _token_tile(T, want):
    return want if T % want == 0 else T


def kernel(x, c, ada_w, ada_b, norm1_g, w_in, b_in, a_ln_g, a_ln_b, a_spatial_w, a_spatial_b, b_conv_w, b_conv_b, b_gn_g, b_gn_b, out_norm_a_g, out_norm_b_g, w_out, norm2_g, w_ffn_in, w_ffn_out, ada_f_w, ada_f_b, norm_f_g, loss_target, m_ada_w, m_ada_b, m_norm1_g, m_w_in, m_b_in, m_a_ln_g, m_a_ln_b, m_a_spatial_w, m_a_spatial_b, m_b_conv_w, m_b_conv_b, m_b_gn_g, m_b_gn_b, m_out_norm_a_g, m_out_norm_b_g, m_w_out, m_norm2_g, m_w_ffn_in, m_w_ffn_out, m_ada_f_w, m_ada_f_b, m_norm_f_g, v_ada_w, v_ada_b, v_norm1_g, v_w_in, v_b_in, v_a_ln_g, v_a_ln_b, v_a_spatial_w, v_a_spatial_b, v_b_conv_w, v_b_conv_b, v_b_gn_g, v_b_gn_b, v_out_norm_a_g, v_out_norm_b_g, v_w_out, v_norm2_g, v_w_ffn_in, v_w_ffn_out, v_ada_f_w, v_ada_f_b, v_norm_f_g):
    T = x.shape[1]
    idx = 4 * lax.axis_index("x") + 2 * lax.axis_index("y") + lax.axis_index("c")
    x2d = x.reshape(T, D)
    tgt = loss_target.reshape(T, D)

    conv_s = jnp.pad(b_conv_w[0], ((0, HALO - KW), (0, 0)))
    call, cparts, cfparts, convg, (win_g,) = _gather(
        c, ada_w[0], ada_b.reshape(NDEV, -1), ada_f_w, ada_f_b.reshape(NDEV, -1), conv_s, [w_in[0].astype(BF16)])
    mod = jnp.concatenate([cparts.reshape(6, D), cfparts.reshape(2, D)], axis=0)
    cw = jnp.transpose(convg, (1, 0, 2)).reshape(HALO, DB)

    tril = jnp.tril(jnp.ones((CH, CH), dtype=bool))
    wsm = jnp.where(tril[None], a_spatial_w[0], 0.0).astype(BF16)
    wcat = wsm.reshape(NH * CH, CH)
    wcat_t = jnp.transpose(wsm, (0, 2, 1)).reshape(NH * CH, CH)
    bsf = jnp.repeat(a_spatial_b[0].T, DA // NH, axis=1)
    lane = jnp.arange(DB)
    pm = jnp.where((lane[:, None] >> 6) == (lane[None, :] >> 6), 1.0 / 64.0, 0.0).astype(BF16)
    esel = jnp.where((lane[:, None] >> 6) == jnp.arange(CH)[None, :], 1.0, 0.0).astype(BF16)

    tm = _token_tile(T, 256)
    tk = _token_tile(T, 2048)
    (x1, hb, zvg, mixed, yb, o, gu, dgelu_u, dgelu_v, vhat, rslb, yhat, rsg), (wfi_g, wfo_g), wout = _mix_fwd(
        x2d, mod, norm1_g, win_g, b_in, a_ln_g, a_ln_b, wcat, bsf, cw, b_conv_b, b_gn_g, b_gn_b, out_norm_a_g,
        out_norm_b_g, w_out[0].astype(BF16), pm, [w_ffn_in[0].T.astype(BF16), w_ffn_out[0].astype(BF16)],
        _token_tile(T, 512))
    dx1, h2b, dgu, act, dxg, acc_f = _ffn(x1, tgt, mod, norm2_g, norm_f_g.reshape(1, D),
                                          wfi_g.reshape(2 * DFF, D), wfo_g.reshape(DFF, D), tm)
    g_wfi = _wgrad_rows(dgu, h2b, 2 * WFI_B, tk, "wgrad_ffn_in").reshape(NDEV, WFI_B, D)
    g_wfo = _wgrad_rows(act, dxg, 2 * WFI_B, tk, "wgrad_ffn_out").reshape(NDEV, DFF // NDEV, D)
    f_sems, f_thru, f_land, f_token = _rs_start([g_wfi, g_wfo], "rs_ffn_start")
    (gx, acc_v, acc_b, acc_a, acc_bs, acc_ws, acc_cw), (g_win, g_wout) = _mix_bwd(
        dx1, x2d, zvg, mixed, o, hb, yb, gu, dgelu_u, dgelu_v, vhat, rslb, yhat, rsg, mod, norm1_g, win_g, a_ln_g,
        a_ln_b, wcat, wcat_t, cw, b_gn_g, b_gn_b, out_norm_a_g, out_norm_b_g, wout, pm, esel, f_token, tm)
    (g_wfi_d, g_wfo_d), (r_wfi, r_wfo) = _rs_wait(f_sems, f_thru, f_land, [acc_v], "rs_ffn_wait")
    g_wout = g_wout.reshape(NDEV, D // NDEV, D)

    vsum, dcond_all, wssum, (q_win, q_wout) = _reduce_small(acc_f, acc_v, acc_b, acc_a, acc_bs, acc_cw, acc_ws,
                                                            g_wfi_d, [g_win, g_wout])
    sems, g_thru, land_thru, token = _rs_start([q_win, q_wout], "rs_mix_start", after=(vsum,), chips=True)

    own = lambda g: (g, jnp.reshape(idx, (1,)).astype(jnp.int32))
    res = {}
    res["w_ffn_in"] = tuple(a.T for a in _adam_big(r_wfi, w_ffn_in[0].T, m_w_ffn_in[0].T, v_w_ffn_in[0].T, WFI_B // 4,
                                                   "adam_w_ffn_in", own=own(g_wfi_d), after=token))
    res["w_ffn_out"] = _adam_big(r_wfo, w_ffn_out[0], m_w_ffn_out[0], v_w_ffn_out[0], DFF // NDEV // 2,
                                 "adam_w_ffn_out", own=own(g_wfo_d), after=token)
    cact_t = (call * jax.nn.sigmoid(call)).T
    dcond = dcond_all.reshape(NDEV, 8 * D)
    nada = ada_w.shape[2]
    nadf = ada_f_w.shape[1]
    dcs = lax.dynamic_slice(dcond, (0, idx * nada), (NDEV, nada))
    dcfs = lax.dynamic_slice(dcond, (0, 6 * D + idx * nadf), (NDEV, nadf))
    res["ada_w"] = _adam_ada(cact_t, dcs, ada_w[0], m_ada_w[0], v_ada_w[0], 512, "adam_ada_w")
    res["ada_f_w"] = _adam_ada(cact_t, dcfs, ada_f_w, m_ada_f_w, v_ada_f_w, 512, "adam_ada_f_w")
    ncw = b_conv_w.shape[2]
    gcw = jnp.concatenate([lax.dynamic_slice(vsum, (_CW_ROW, idx * ncw), (HALO // 2, ncw)),
                           lax.dynamic_slice(vsum, (_CW_ROW, DB + idx * ncw), (HALO // 2, ncw))], axis=0)[:KW]
    two = lambda a: a.reshape(1, -1) if a.ndim == 1 else a.reshape(-1, a.shape[-1])
    small_in = {
        "ada_b": (ada_b, m_ada_b, v_ada_b), "ada_f_b": (ada_f_b, m_ada_f_b, v_ada_f_b),
        "norm1_g": (norm1_g, m_norm1_g, v_norm1_g), "b_in": (b_in, m_b_in, v_b_in),
        "a_ln_g": (a_ln_g, m_a_ln_g, v_a_ln_g), "a_ln_b": (a_ln_b, m_a_ln_b, v_a_ln_b),
        "a_spatial_b": (a_spatial_b.reshape(1, D), m_a_spatial_b.reshape(1, D), v_a_spatial_b.reshape(1, D)),
        "b_conv_b": (b_conv_b, m_b_conv_b, v_b_conv_b), "b_gn_g": (b_gn_g, m_b_gn_g, v_b_gn_g),
        "b_gn_b": (b_gn_b, m_b_gn_b, v_b_gn_b), "out_norm_a_g": (out_norm_a_g, m_out_norm_a_g, v_out_norm_a_g),
        "out_norm_b_g": (out_norm_b_g, m_out_norm_b_g, v_out_norm_b_g),
        "norm2_g": (norm2_g, m_norm2_g, v_norm2_g), "norm_f_g": (norm_f_g, m_norm_f_g, v_norm_f_g),
        "a_spatial_w": (a_spatial_w, m_a_spatial_w, v_a_spatial_w),
        "b_conv_w": (b_conv_w[0], m_b_conv_w[0], v_b_conv_w[0]),
    }
    small_in = {n: tuple(two(a) for a in t) for n, t in small_in.items()}
    res.update(_adam_small(vsum, wssum, gcw, small_in))
    (q_win_d, q_wout_d), (r_win, r_wout) = _rs_wait(
        sems, g_thru, land_thru,
        [res["w_ffn_in"][0], res["w_ffn_out"][0], res["ada_w"][0], res["ada_f_w"][0], res["norm_f_g"][0]],
        "rs_mix_wait", chips=True)
    res["w_in"] = _adam_big(r_win, w_in[0], m_w_in[0], v_w_in[0], 256, "adam_w_in", own=own(g_win), sib=q_win_d)
    res["w_out"] = _adam_big(r_wout, w_out[0], m_w_out[0], v_w_out[0], D // NDEV, "adam_w_out", own=own(g_wout),
                             sib=q_wout_d)

    loss = 0.5 / D * jnp.sum(vsum[_LOSS_ROW])
    shapes = {"ada_w": ada_w, "ada_b": ada_b, "norm1_g": norm1_g, "w_in": w_in, "b_in": b_in, "a_ln_g": a_ln_g,
              "a_ln_b": a_ln_b, "a_spatial_w": a_spatial_w, "a_spatial_b": a_spatial_b, "b_conv_w": b_conv_w,
              "b_conv_b": b_conv_b, "b_gn_g": b_gn_g, "b_gn_b": b_gn_b, "out_norm_a_g": out_norm_a_g,
              "out_norm_b_g": out_norm_b_g, "w_out": w_out, "norm2_g": norm2_g, "w_ffn_in": w_ffn_in,
              "w_ffn_out": w_ffn_out, "ada_f_w": ada_f_w, "ada_f_b": ada_f_b, "norm_f_g": norm_f_g}
    order = list(shapes)
    outs = [loss, gx.reshape(x.shape)]
    for which in range(4):
        outs += [res[n][which].reshape(shapes[n].shape) for n in order]
    return tuple(outs)
```

```python
import math

import jax
import jax.numpy as jnp
from jax import lax
from jax.experimental import pallas as pl
from jax.experimental.pallas import tpu as pltpu

F32 = jnp.float32
BF16 = jnp.bfloat16

D = 1024
DA = 512
DB = 512
DIN = 2048
DFF = 2816
NH = 8
CH = 128
KW = 31
HALO = 32
NDEV = 8
WIN_B = DIN // NDEV
WFI_B = 2 * DFF // NDEV
EPS = 1e-6
NVEC = 40
VMEM_LIMIT = 56 * 1024 * 1024

ADAM_LR, ADAM_B1, ADAM_B2, ADAM_EPS, ADAM_WD, ADAM_STEP = 0.001, 0.9, 0.999, 1e-08, 0.01, 10

MESH = pl.DeviceIdType.MESH


def _dot(a, b):
    return jnp.dot(a, b, preferred_element_type=F32)


def _dot_nt(a, b):
    return lax.dot_general(a, b, (((1,), (1,)), ((), ())), preferred_element_type=F32)


def _dot_tn(a, b):
    return lax.dot_general(a, b, (((0,), (0,)), ((), ())), preferred_element_type=F32)


def _rs(v):
    return lax.rsqrt(jnp.mean(v * v, axis=-1, keepdims=True) + EPS)


def _sig(v):
    return 1.0 / (1.0 + jnp.exp(-v))


_INV_SQRT2 = 1.0 / math.sqrt(2.0)
_INV_SQRT2PI = 1.0 / math.sqrt(2.0 * math.pi)


def _gelu_parts(v):
    cdf = 0.5 * (1.0 + lax.erf(v * _INV_SQRT2))
    pdf = jnp.exp(-0.5 * v * v) * _INV_SQRT2PI
    return v * cdf, cdf + v * pdf


def _grp_mean(v, pm):
    hi = v.astype(BF16)
    lo = (v - hi.astype(F32)).astype(BF16)
    return _dot(hi, pm) + _dot(lo, pm)


def _colsum(v):
    return jnp.sum(v, axis=0, keepdims=True)


def _full(shape):
    nd = len(shape)
    return pl.BlockSpec(shape, lambda *_: (0,) * nd)


def _resident(shape):
    nd = len(shape)
    return pl.BlockSpec(shape, lambda *_: (0,) * nd, pipeline_mode=pl.Buffered(1))


HBM = pl.BlockSpec(memory_space=pl.ANY)
VM = pl.BlockSpec(memory_space=pltpu.VMEM)


SH_ROWS = HALO - 8


def _shifted_copies(buf, shbuf, tm):
    for b in range(1, 8):
        shbuf[b - 1] = buf[b:b + tm + SH_ROWS, :]


def _window(buf, shbuf, off, tm):
    a, b = divmod(off, 8)
    if b == 0:
        return buf[8 * a:8 * a + tm, :]
    return shbuf[b - 1, 8 * a:8 * a + tm, :]


def _first_head_lanes():
    return lax.broadcasted_iota(jnp.int32, (CH, CH), 1) < (DA // NH)


def _mix_heads(w_ref, vb, first):
    outs = []
    for p in range(NH // 2):
        v = vb[:, p * CH:(p + 1) * CH]
        a = _dot(w_ref[(2 * p) * CH:(2 * p + 1) * CH, :], v)
        b = _dot(w_ref[(2 * p + 1) * CH:(2 * p + 2) * CH, :], v)
        outs.append(jnp.where(first, a, b))
    return jnp.concatenate(outs, axis=1)


def _place():
    x, y, c = lax.axis_index("x"), lax.axis_index("y"), lax.axis_index("c")
    return x, y, c, 4 * x + 2 * y + c


def _dev(t):
    return (t >> 2, (t >> 1) & 1, t & 1)


class _AllGather:
    def __init__(self, w_in, w_out, wss, wrs, lsem):
        x, y, c, idx = _place()
        me, sibling = (x, y, c), (x, y, 1 - c)
        chips = [(1 - x, y), (x, 1 - y), (1 - x, 1 - y)]
        nw = len(w_in)

        def blk(p):
            return 4 * p[0] + 2 * p[1] + p[2]

        def wcopy(a, k, block, to, src=None):
            dst = w_out[a].at[blk(block)]
            return pltpu.make_async_remote_copy(src_ref=dst if src is None else src, dst_ref=dst,
                                                send_sem=wss.at[a, k], recv_sem=wrs.at[a, k],
                                                device_id=to, device_id_type=MESH)

        self.mine = [pltpu.make_async_copy(w_in[a], w_out[a].at[idx], lsem.at[a]) for a in range(nw)]
        self.first = []
        for a in range(nw):
            self.first.append(wcopy(a, 0, me, sibling, src=w_in[a]))
            self.first += [wcopy(a, 1 + j, me, (*chip, c), src=w_in[a]) for j, chip in enumerate(chips)]
        self.landed = [[wcopy(a, 1 + j, (*chip, c), me) for a in range(nw)] for j, chip in enumerate(chips)]
        self.passed = [[wcopy(a, 4 + j, (*chip, c), sibling) for a in range(nw)] for j, chip in enumerate(chips)]
        self.from_sibling = []
        for a in range(nw):
            self.from_sibling.append(wcopy(a, 0, sibling, me))
            self.from_sibling += [wcopy(a, 4 + j, (*chip, 1 - c), me) for j, chip in enumerate(chips)]

    def start(self):
        for cp in self.mine + self.first:
            cp.start()

    def forward(self):
        for land, pas in zip(self.landed, self.passed):
            for l, p in zip(land, pas):
                l.wait_recv()
                p.start()

    def finish(self):
        for cp in self.from_sibling:
            cp.wait_recv()
        for cp in self.first:
            cp.wait_send()
        for pas in self.passed:
            for p in pas:
                p.wait_send()
        for cp in self.mine:
            cp.wait()


AG_SEMS = lambda nw: [pltpu.SemaphoreType.DMA((nw, 7)), pltpu.SemaphoreType.DMA((nw, 7)),
                      pltpu.SemaphoreType.DMA((nw,))]


def _mix_fwd(x, mod, g1, win, b_in, lng, lnb, wcat, bsf, cw, cb, gng, gnb, oga, ogb, wout, pm, ffn_shards, tm):
    T = x.shape[0]
    nt = T // tm
    nch = tm // CH
    nw = len(ffn_shards)
    fwd_step = (5 * nt) // 8
    saved = [(D, F32), (D, BF16), (2 * DB, F32), (DA, F32), (D, BF16), (D, F32), (DA, F32), (DA, F32), (DA, F32),
             (DA, F32), (CH, F32), (DB, F32), (DB, F32)]
    NSAVE = len(saved)

    def body(x_ref, mod_ref, g1_ref, win_ref, bin_ref, lng_ref, lnb_ref, wcat_ref, bsf_ref, cw_ref, cb_ref,
             gng_ref, gnb_ref, oga_ref, ogb_ref, wout_ref, pm_ref, *rest):
        sh_in = rest[:nw]
        (x1_ref, h_ref, zvg_ref, mixed_ref, y_ref, o_ref, gu_ref, dgu_ref, dgv_ref, vhat_ref, rsl_ref, yhat_ref,
         rsg_ref) = rest[nw:nw + NSAVE]
        sh_out = rest[nw + NSAVE:2 * nw + NSAVE]
        glbuf, shbuf, wss, wrs, lsem = rest[2 * nw + NSAVE:]
        i = pl.program_id(0)

        @pl.when(i == 0)
        def _():
            _AllGather(sh_in, sh_out, wss, wrs, lsem).start()

        xv = x_ref[...]
        shift1 = mod_ref[0:1, :]
        scale1 = mod_ref[1:2, :]
        gate1 = mod_ref[2:3, :]
        h = (xv * _rs(xv) * g1_ref[...]) * (1.0 + scale1) + shift1
        hb = h.astype(BF16)
        h_ref[...] = hb
        z = jnp.concatenate([_dot(hb, win_ref[j]) for j in range(NDEV)], axis=1) + bin_ref[...]
        zvg_ref[...] = z[:, 2 * DA:]
        gu, dgelu_u = _gelu_parts(z[:, 0:DA])
        gv, dgelu_v = _gelu_parts(z[:, DA:2 * DA])
        gu_ref[...] = gu
        dgu_ref[...] = dgelu_u
        dgv_ref[...] = dgelu_v
        xc = gv - jnp.mean(gv, axis=-1, keepdims=True)
        rsl = lax.rsqrt(jnp.mean(xc * xc, axis=-1, keepdims=True) + EPS)
        vhat = xc * rsl
        vhat_ref[...] = vhat
        rsl_ref[...] = jnp.broadcast_to(rsl, (tm, CH))
        vnb = (vhat * lng_ref[...] + lnb_ref[...]).astype(BF16)
        first = _first_head_lanes()
        chunks = []
        for ci in range(nch):
            chunks.append(_mix_heads(wcat_ref, vnb[ci * CH:(ci + 1) * CH, :], first) + bsf_ref[...])
        mixed = jnp.concatenate(chunks, axis=0) if nch > 1 else chunks[0]
        mixed_ref[...] = mixed
        ya = gu * mixed
        gl = z[:, 2 * DA:2 * DA + DB] * _sig(z[:, 2 * DA + DB:])

        @pl.when(i == 0)
        def _():
            glbuf[0:HALO, :] = jnp.zeros((HALO, DB), F32)

        glbuf[HALO:HALO + tm, :] = gl
        _shifted_copies(glbuf, shbuf, tm)
        yc = jnp.zeros((tm, DB), F32) + cb_ref[...]
        for k in range(KW):
            yc = yc + cw_ref[k:k + 1, :] * _window(glbuf, shbuf, HALO - (KW - 1) + k, tm)
        glbuf[0:HALO, :] = gl[tm - HALO:, :]
        pmv = pm_ref[...]
        dc = yc - _grp_mean(yc, pmv)
        rsg = lax.rsqrt(_grp_mean(dc * dc, pmv) + EPS)
        yhat = dc * rsg
        yhat_ref[...] = yhat
        rsg_ref[...] = rsg
        yg = yhat * gng_ref[...] + gnb_ref[...]
        yb = yg * _sig(yg)
        na = ya * _rs(ya) * oga_ref[...]
        nb = yb * _rs(yb) * ogb_ref[...]
        yv = jnp.concatenate([na, nb], axis=1).astype(BF16)
        y_ref[...] = yv
        o = _dot(yv, wout_ref[...])
        o_ref[...] = o
        x1_ref[...] = xv + gate1 * o

        @pl.when(i == fwd_step)
        def _():
            _AllGather(sh_in, sh_out, wss, wrs, lsem).forward()

        @pl.when(i == nt - 1)
        def _():
            _AllGather(sh_in, sh_out, wss, wrs, lsem).finish()

    tile = lambda w: pl.BlockSpec((tm, w), lambda i: (i, 0))
    outs = pl.pallas_call(
        body,
        name="mix_fwd",
        grid=(nt,),
        in_specs=[tile(D), _full((8, D)), _full((1, D)), _resident((NDEV, D, WIN_B)), _full((1, DIN)),
                  _full((1, DA)), _full((1, DA)), _full((NH * CH, CH)), _full((CH, DA)), _full((HALO, DB)),
                  _full((1, DB)), _full((1, DB)), _full((1, DB)), _full((1, DA)), _full((1, DB)),
                  _resident((D, D)), _full((DB, DB))] + [HBM] * nw,
        out_specs=[tile(w) for w, _ in saved] + [HBM] * nw,
        out_shape=[jax.ShapeDtypeStruct((T, w), dt) for w, dt in saved]
                  + [jax.ShapeDtypeStruct((NDEV,) + s.shape, s.dtype) for s in ffn_shards],
        scratch_shapes=[pltpu.VMEM((HALO + tm, DB), F32), pltpu.VMEM((7, tm + SH_ROWS, DB), F32)] + AG_SEMS(nw),
        compiler_params=pltpu.CompilerParams(dimension_semantics=("arbitrary",), vmem_limit_bytes=VMEM_LIMIT),
    )(x, mod, g1, win, b_in, lng, lnb, wcat, bsf, cw, cb, gng, gnb, oga, ogb, wout, pm, *ffn_shards)
    return outs[:NSAVE], outs[NSAVE:]


FF_BLOCKS = ((0, 1024), (1024, 1024), (2048, 768))


def _ffn(x1, tgt, mod, g2, gf, wfi_t, wfo, tm):
    T = x1.shape[0]
    nt = T // tm

    def body(x1_ref, tgt_ref, mod_ref, g2_ref, gf_ref, wfi_ref, wfo_ref,
             dx1_ref, h2_ref, dgu_ref, act_ref, dxg_ref, acc_ref, g_s, u_s):
        i = pl.program_id(0)

        @pl.when(i == 0)
        def _():
            acc_ref[...] = jnp.zeros((8, D), F32)

        x1 = x1_ref[...]
        shift2 = mod_ref[3:4, :]
        scale2 = mod_ref[4:5, :]
        gate2 = mod_ref[5:6, :]
        shiftf = mod_ref[6:7, :]
        scalef = mod_ref[7:8, :]
        g2v = g2_ref[...]
        gfv = gf_ref[...]
        r2 = _rs(x1)
        xn2 = x1 * r2
        h2b = (xn2 * g2v * (1.0 + scale2) + shift2).astype(BF16)
        h2_ref[...] = h2b
        f = jnp.zeros((tm, D), F32)
        for o, w in FF_BLOCKS:
            g = _dot_nt(h2b, wfi_ref[o:o + w, :])
            u = _dot_nt(h2b, wfi_ref[DFF + o:DFF + o + w, :])
            g_s[:, o:o + w] = g
            u_s[:, o:o + w] = u
            actb = (g * _sig(g) * u).astype(BF16)
            act_ref[:, o:o + w] = actb
            f = f + _dot(actb, wfo_ref[o:o + w, :])
        x2 = x1 + gate2 * f
        rf = _rs(x2)
        xnf = x2 * rf
        out = xnf * gfv * (1.0 + scalef) + shiftf
        e = out - tgt_ref[...]
        dout = e * (1.0 / D)
        acc_ref[7:8, :] += _colsum(e * e)
        acc_ref[0:1, :] += _colsum(dout)
        acc_ref[1:2, :] += _colsum(dout * xnf * gfv)
        acc_ref[2:3, :] += _colsum(dout * (1.0 + scalef) * xnf)
        dxnf = dout * (1.0 + scalef) * gfv
        dx2 = rf * (dxnf - xnf * jnp.mean(dxnf * xnf, axis=-1, keepdims=True))
        acc_ref[3:4, :] += _colsum(dx2 * f)
        dxgb = (dx2 * gate2).astype(BF16)
        dxg_ref[...] = dxgb
        dh2 = jnp.zeros((tm, D), F32)
        for o, w in FF_BLOCKS:
            dact = _dot_nt(dxgb, wfo_ref[o:o + w, :])
            g = g_s[:, o:o + w]
            u = u_s[:, o:o + w]
            s = _sig(g)
            dgb = (dact * u * (s * (1.0 + g * (1.0 - s)))).astype(BF16)
            dub = (dact * (g * s)).astype(BF16)
            dgu_ref[:, o:o + w] = dgb
            dgu_ref[:, DFF + o:DFF + o + w] = dub
            dh2 = dh2 + _dot(dgb, wfi_ref[o:o + w, :])
            dh2 = dh2 + _dot(dub, wfi_ref[DFF + o:DFF + o + w, :])
        acc_ref[4:5, :] += _colsum(dh2)
        acc_ref[5:6, :] += _colsum(dh2 * xn2 * g2v)
        acc_ref[6:7, :] += _colsum(dh2 * (1.0 + scale2) * xn2)
        dxn2 = dh2 * (1.0 + scale2) * g2v
        dx1_ref[...] = dx2 + r2 * (dxn2 - xn2 * jnp.mean(dxn2 * xn2, axis=-1, keepdims=True))

    tile = lambda w: pl.BlockSpec((tm, w), lambda i: (i, 0))
    return pl.pallas_call(
        body,
        name="ffn_fwd_bwd",
        grid=(nt,),
        in_specs=[tile(D), tile(D), _full((8, D)), _full((1, D)), _full((1, D)),
                  _resident((2 * DFF, D)), _resident((DFF, D))],
        out_specs=[tile(D), tile(D), tile(2 * DFF), tile(DFF), tile(D), _full((8, D))],
        out_shape=[jax.ShapeDtypeStruct((T, D), F32), jax.ShapeDtypeStruct((T, D), BF16),
                   jax.ShapeDtypeStruct((T, 2 * DFF), BF16), jax.ShapeDtypeStruct((T, DFF), BF16),
                   jax.ShapeDtypeStruct((T, D), BF16), jax.ShapeDtypeStruct((8, D), F32)],
        scratch_shapes=[pltpu.VMEM((tm, DFF), F32), pltpu.VMEM((tm, DFF), F32)],
        compiler_params=pltpu.CompilerParams(dimension_semantics=("arbitrary",), vmem_limit_bytes=VMEM_LIMIT),
    )(x1, tgt, mod, g2, gf, wfi_t, wfo)


def _mix_bwd(dx1, x, zvg, mixed, o, hb, yb, gu, dgu, dgv, vhat, rslb, yhat, rsg, mod, g1, win, lng, lnb, wcat, wcat_t,
             cw, gng, gnb, oga, ogb, wout, pm, esel, after, tm):
    T = x.shape[0]
    nt = T // tm
    nch = tm // CH
    WOB = 256

    def body(dx1_ref, x_ref, zvg_ref, mixed_ref, o_ref, hb_ref, yb_ref, gu_ref, dgu_ref, dgv_ref, vhat_ref, rsl_ref,
             yhat_ref, rsg_ref, mod_ref, g1_ref, win_ref, lng_ref, lnb_ref, wcat_ref, wcatt_ref, cw_ref, gng_ref,
             gnb_ref, oga_ref, ogb_ref, wout_ref, pm_ref, esel_ref, after_ref,
             gx_ref, accv_ref, accb_ref, acca_ref, accbs_ref, accws_ref, acccw_ref, gwin_ref, gwout_ref,
             dycbuf, shbuf, bs_s, acc_win, acc_wout, st_win, st_wout):
        i = pl.program_id(0)

        @pl.when(i == 0)
        def _():
            acc_win[...] = jnp.zeros((NDEV, D, WIN_B), F32)
            acc_wout[...] = jnp.zeros((D, D), F32)
            accv_ref[...] = jnp.zeros((8, D), F32)
            accb_ref[...] = jnp.zeros((1, DIN), F32)
            acca_ref[...] = jnp.zeros((8, DA), F32)
            accws_ref[...] = jnp.zeros((NH * CH, CH), F32)
            acccw_ref[...] = jnp.zeros((HALO, DB), F32)
            bs_s[...] = jnp.zeros((CH, DA), F32)
            dycbuf[tm:tm + HALO, :] = jnp.zeros((HALO, DB), F32)

        shift1 = mod_ref[0:1, :]
        scale1 = mod_ref[1:2, :]
        gate1 = mod_ref[2:3, :]
        g1v = g1_ref[...]
        xv = x_ref[...]
        r1 = _rs(xv)
        xn1 = xv * r1
        val = zvg_ref[:, 0:DB]
        gate = zvg_ref[:, DB:]
        gu = gu_ref[...]
        dgelu_u = dgu_ref[...]
        dgelu_v = dgv_ref[...]
        vhat = vhat_ref[...]
        rsl = rsl_ref[:, 0:1]
        lngv = lng_ref[...]
        vnb = (vhat * lngv + lnb_ref[...]).astype(BF16)
        mixed = mixed_ref[...]
        ya = gu * mixed
        ra = _rs(ya)
        yan = ya * ra
        sgt = _sig(gate)
        gl = val * sgt
        pmv = pm_ref[...]
        rsg = rsg_ref[...]
        yhat = yhat_ref[...]
        gngv = gng_ref[...]
        yg = yhat * gngv + gnb_ref[...]
        sgy = _sig(yg)
        yb = yg * sgy
        rb = _rs(yb)
        ybn = yb * rb
        dx1 = dx1_ref[...]
        accv_ref[0:1, :] += _colsum(dx1 * o_ref[...])
        dogb = (dx1 * gate1).astype(BF16)
        acc_wout[...] += _dot_tn(yb_ref[...], dogb)
        dy = _dot_nt(dogb, wout_ref[...])
        dna = dy[:, 0:DA]
        dnb = dy[:, DA:]
        ogav = oga_ref[...]
        ogbv = ogb_ref[...]
        acca_ref[2:3, :] += _colsum(dna * yan)
        acca_ref[3:4, :] += _colsum(dnb * ybn)
        ta = dna * ogav
        dya = ra * (ta - yan * jnp.mean(ta * yan, axis=-1, keepdims=True))
        tb = dnb * ogbv
        dyb = rb * (tb - ybn * jnp.mean(tb * ybn, axis=-1, keepdims=True))
        dgu = dya * mixed
        dm = dya * gu
        first = _first_head_lanes()
        zero = jnp.zeros((CH, CH), BF16)
        dvn_chunks = []
        bs_acc = bs_s[...]
        for ci in range(nch):
            dmc = dm[ci * CH:(ci + 1) * CH, :]
            bs_acc = bs_acc + dmc
            dmcb = dmc.astype(BF16)
            dvn_chunks.append(_mix_heads(wcatt_ref, dmcb, first))
            vc = vnb[ci * CH:(ci + 1) * CH, :]
            for p in range(NH // 2):
                xt = dmcb[:, p * CH:(p + 1) * CH]
                vt = vc[:, p * CH:(p + 1) * CH]
                accws_ref[(2 * p) * CH:(2 * p + 1) * CH, :] += _dot_nt(jnp.where(first, xt, zero), vt)
                accws_ref[(2 * p + 1) * CH:(2 * p + 2) * CH, :] += _dot_nt(jnp.where(first, zero, xt), vt)
        bs_s[...] = bs_acc
        dvn = jnp.concatenate(dvn_chunks, axis=0) if nch > 1 else dvn_chunks[0]
        acca_ref[0:1, :] += _colsum(dvn * vhat)
        acca_ref[1:2, :] += _colsum(dvn)
        dvh = dvn * lngv
        dgv = rsl * (dvh - jnp.mean(dvh, axis=-1, keepdims=True)
                     - vhat * jnp.mean(dvh * vhat, axis=-1, keepdims=True))
        du = dgu * dgelu_u
        dv = dgv * dgelu_v
        dyg = dyb * (sgy * (1.0 + yg * (1.0 - sgy)))
        acca_ref[5:6, :] += _colsum(dyg * yhat)
        acca_ref[6:7, :] += _colsum(dyg)
        dyh = dyg * gngv
        dyc = rsg * (dyh - _grp_mean(dyh, pmv) - yhat * _grp_mean(dyh * yhat, pmv))
        acca_ref[4:5, :] += _colsum(dyc)
        dycbuf[0:tm, :] = dyc
        _shifted_copies(dycbuf, shbuf, tm)
        dgl = jnp.zeros((tm, DB), F32)
        for k in range(KW):
            win_k = _window(dycbuf, shbuf, KW - 1 - k, tm)
            dgl = dgl + cw_ref[k:k + 1, :] * win_k
            acccw_ref[k:k + 1, :] += _colsum(win_k * gl)
        dycbuf[tm:tm + HALO, :] = dyc[0:HALO, :]
        dval = dgl * sgt
        dgate = dgl * val * sgt * (1.0 - sgt)
        dz = jnp.concatenate([du, dv, dval, dgate], axis=1)
        accb_ref[...] += _colsum(dz)
        dzb = dz.astype(BF16)
        hbv = hb_ref[...]
        dh = jnp.zeros((tm, D), F32)
        for j in range(NDEV):
            dzj = dzb[:, j * WIN_B:(j + 1) * WIN_B]
            acc_win[j] += _dot_tn(hbv, dzj)
            dh = dh + _dot_nt(dzj, win_ref[j])
        accv_ref[1:2, :] += _colsum(dh)
        dh_xn = _colsum(dh * xn1)
        accv_ref[2:3, :] += dh_xn * g1v
        accv_ref[3:4, :] += dh_xn * (1.0 + scale1)
        dxn1 = dh * (1.0 + scale1) * g1v
        gx_ref[...] = dx1 + r1 * (dxn1 - xn1 * jnp.mean(dxn1 * xn1, axis=-1, keepdims=True))

        @pl.when(i == nt - 1)
        def _():
            rows = lax.broadcasted_iota(jnp.int32, (NH * CH, CH), 0) & (CH - 1)
            cols = lax.broadcasted_iota(jnp.int32, (NH * CH, CH), 1)
            accws_ref[...] = jnp.where(cols <= rows, accws_ref[...], 0.0)
            bs = bs_s[...]
            hi = bs.astype(BF16)
            r1_ = bs - hi.astype(F32)
            mid = r1_.astype(BF16)
            lo = (r1_ - mid.astype(F32)).astype(BF16)
            ev = esel_ref[...]
            accbs_ref[...] = _dot(hi, ev) + _dot(mid, ev) + _dot(lo, ev)
            for j in range(NDEV):
                st_win[...] = acc_win[j].astype(BF16)
                pltpu.sync_copy(st_win, gwin_ref.at[j])
            for j in range(D // WOB):
                st_wout[...] = acc_wout[j * WOB:(j + 1) * WOB, :].astype(BF16)
                pltpu.sync_copy(st_wout, gwout_ref.at[pl.ds(j * WOB, WOB)])

    rev = lambda w: pl.BlockSpec((tm, w), lambda i: (nt - 1 - i, 0))
    outs = pl.pallas_call(
        body,
        name="mix_bwd",
        grid=(nt,),
        in_specs=[rev(D), rev(D), rev(2 * DB), rev(DA), rev(D), rev(D), rev(D), rev(DA), rev(DA), rev(DA), rev(DA),
                  rev(CH), rev(DB), rev(DB), _full((8, D)), _full((1, D)),
                  _resident((NDEV, D, WIN_B)), _full((1, DA)), _full((1, DA)), _full((NH * CH, CH)),
                  _full((NH * CH, CH)), _full((HALO, DB)), _full((1, DB)), _full((1, DB)), _full((1, DA)),
                  _full((1, DB)), _resident((D, D)), _full((DB, DB)), _full((DA, CH)), HBM],
        out_specs=[rev(D), _full((8, D)), _full((1, DIN)), _full((8, DA)), _full((CH, CH)),
                   _full((NH * CH, CH)), _full((HALO, DB)), HBM, HBM],
        out_shape=[jax.ShapeDtypeStruct((T, D), F32), jax.ShapeDtypeStruct((8, D), F32),
                   jax.ShapeDtypeStruct((1, DIN), F32), jax.ShapeDtypeStruct((8, DA), F32),
                   jax.ShapeDtypeStruct((CH, CH), F32), jax.ShapeDtypeStruct((NH * CH, CH), F32),
                   jax.ShapeDtypeStruct((HALO, DB), F32),
                   jax.ShapeDtypeStruct((NDEV, D, WIN_B), BF16), jax.ShapeDtypeStruct((D, D), BF16)],
        scratch_shapes=[pltpu.VMEM((tm + HALO, DB), F32), pltpu.VMEM((7, tm + SH_ROWS, DB), F32),
                        pltpu.VMEM((CH, DA), F32), pltpu.VMEM((NDEV, D, WIN_B), F32), pltpu.VMEM((D, D), F32),
                        pltpu.VMEM((D, WIN_B), BF16), pltpu.VMEM((WOB, D), BF16)],
        compiler_params=pltpu.CompilerParams(dimension_semantics=("arbitrary",), vmem_limit_bytes=VMEM_LIMIT),
    )(dx1, x, zvg, mixed, o, hb, yb, gu, dgu, dgv, vhat, rslb, yhat, rsg, mod, g1, win, lng, lnb, wcat, wcat_t, cw,
      gng, gnb, oga, ogb, wout, pm, esel, after)
    return outs[:7], outs[7:]


def _wgrad_rows(a, b, bm, tk, name):
    T, M = a.shape
    N = b.shape[1]
    nk = T // tk

    def body(a_ref, b_ref, o_ref, acc):
        k = pl.program_id(1)

        @pl.when(k == 0)
        def _():
            acc[...] = jnp.zeros((bm, N), F32)

        acc[...] += _dot_tn(a_ref[...], b_ref[...])

        @pl.when(k == nk - 1)
        def _():
            o_ref[...] = acc[...].astype(BF16)

    return pl.pallas_call(
        body, name=name, grid=(M // bm, nk),
        in_specs=[pl.BlockSpec((tk, bm), lambda j, k: (k, j)), pl.BlockSpec((tk, N), lambda j, k: (k, 0))],
        out_specs=pl.BlockSpec((bm, N), lambda j, k: (j, 0)),
        out_shape=jax.ShapeDtypeStruct((M, N), BF16),
        scratch_shapes=[pltpu.VMEM((bm, N), F32)],
        compiler_params=pltpu.CompilerParams(dimension_semantics=("arbitrary", "arbitrary"),
                                             vmem_limit_bytes=VMEM_LIMIT),
    )(a, b)


def _small_copy(src, dst, ss, rs, k, to):
    return pltpu.make_async_remote_copy(src_ref=src, dst_ref=dst, send_sem=ss.at[k], recv_sem=rs.at[k],
                                        device_id=to, device_id_type=MESH)


def _gather(c_row, ada_w, ada_b8, ada_f_w, ada_f_b8, conv_s, shards):
    nw = len(shards)

    def body(c_ref, adaw_ref, adab_ref, adafw_ref, adafb_ref, conv_ref, *rest):
        w_in = rest[:nw]
        call_ref, cparts_ref, cfparts_ref, convg_ref = rest[nw:nw + 4]
        w_out = rest[nw + 4:2 * nw + 4]
        part_s, partf_s, wss, wrs, lsem, s1, r1, s2, r2, s3, r3, s4, r4 = rest[2 * nw + 4:]
        x, y, c, idx = _place()
        me = (x, y, c)
        ag = _AllGather(w_in, w_out, wss, wrs, lsem)
        ag.start()
        call_ref[pl.ds(idx, 1), :] = c_ref[...]
        convg_ref[idx] = conv_ref[...]
        ph1 = []
        for k in range(1, NDEV):
            to = _dev(idx ^ k)
            ph1.append(_small_copy(c_ref, call_ref.at[pl.ds(idx, 1)], s1, r1, k - 1, to))
            ph1.append(_small_copy(conv_ref, convg_ref.at[idx], s2, r2, k - 1, to))
        for cp in ph1:
            cp.start()
        for k in range(1, NDEV):
            src_dev = idx ^ k
            _small_copy(c_ref, call_ref.at[pl.ds(src_dev, 1)], s1, r1, k - 1, me).wait_recv()
            _small_copy(conv_ref, convg_ref.at[src_dev], s2, r2, k - 1, me).wait_recv()
        call = call_ref[...]
        cact = (call * _sig(call))
        part_s[...] = jnp.dot(cact, adaw_ref[...], preferred_element_type=F32,
                              precision=lax.Precision.HIGHEST) + adab_ref[pl.ds(idx, 1), :]
        partf_s[...] = jnp.dot(cact, adafw_ref[...], preferred_element_type=F32,
                               precision=lax.Precision.HIGHEST) + adafb_ref[pl.ds(idx, 1), :]
        cparts_ref[pl.ds(idx, 1), :] = part_s[pl.ds(idx, 1), :]
        cfparts_ref[pl.ds(idx, 1), :] = partf_s[pl.ds(idx, 1), :]
        ph2 = []
        for k in range(1, NDEV):
            t = idx ^ k
            ph2.append(_small_copy(part_s.at[pl.ds(t, 1)], cparts_ref.at[pl.ds(idx, 1)], s3, r3, k - 1, _dev(t)))
            ph2.append(_small_copy(partf_s.at[pl.ds(t, 1)], cfparts_ref.at[pl.ds(idx, 1)], s4, r4, k - 1, _dev(t)))
        for cp in ph2:
            cp.start()
        for k in range(1, NDEV):
            src_dev = idx ^ k
            _small_copy(part_s.at[pl.ds(0, 1)], cparts_ref.at[pl.ds(src_dev, 1)], s3, r3, k - 1, me).wait_recv()
            _small_copy(partf_s.at[pl.ds(0, 1)], cfparts_ref.at[pl.ds(src_dev, 1)], s4, r4, k - 1, me).wait_recv()
        for cp in ph1 + ph2:
            cp.wait_send()
        ag.forward()
        ag.finish()

    dma7 = pltpu.SemaphoreType.DMA((NDEV - 1,))
    outs = pl.pallas_call(
        body,
        name="gather_weights",
        in_specs=[VM] * 6 + [HBM] * nw,
        out_specs=[VM] * 4 + [HBM] * nw,
        out_shape=[jax.ShapeDtypeStruct((NDEV, D), F32), jax.ShapeDtypeStruct((NDEV, ada_w.shape[1]), F32),
                   jax.ShapeDtypeStruct((NDEV, ada_f_w.shape[1]), F32),
                   jax.ShapeDtypeStruct((NDEV,) + conv_s.shape, F32)]
                  + [jax.ShapeDtypeStruct((NDEV,) + s.shape, s.dtype) for s in shards],
        scratch_shapes=[pltpu.VMEM((NDEV, ada_w.shape[1]), F32), pltpu.VMEM((NDEV, ada_f_w.shape[1]), F32)]
                       + AG_SEMS(nw) + [dma7] * 8,
        compiler_params=pltpu.CompilerParams(vmem_limit_bytes=VMEM_LIMIT),
    )(c_row, ada_w, ada_b8, ada_f_w, ada_f_b8, conv_s, *shards)
    return outs[0], outs[1], outs[2], outs[3], outs[4:]


_VEC_AT = {
    "norm1_g": (8, 0, D), "a_ln_g": (11, 0, DA), "a_ln_b": (11, DA, DA), "a_spatial_b": (12, 0, D),
    "b_conv_b": (13, 0, DB), "b_gn_g": (13, DB, DB), "b_gn_b": (14, 0, DB), "out_norm_a_g": (14, DB, DA),
    "out_norm_b_g": (15, 0, DB), "norm2_g": (16, 0, D), "norm_f_g": (17, 0, D),
}
_LOSS_ROW = 18
_CW_ROW = 24


def _reduce_small(acc_f, acc_v, acc_b, acc_a, acc_bs, acc_cw, dws, after, pair_grads):
    npg = len(pair_grads)

    def body(accf_ref, accv_ref, accb_ref, acca_ref, accbs_ref, acccw_ref, dws_ref, after_ref, *rest):
        pg = rest[:npg]
        vsum_ref, dcond_ref, wssum_ref = rest[npg:npg + 3]
        pq = rest[npg + 3:2 * npg + 3]
        vloc, vbuf, wbuf, wown, s1, r1, s2, r2, s3, r3 = rest[2 * npg + 3:2 * npg + 13]
        pland = rest[2 * npg + 13:3 * npg + 13]
        pstage = rest[3 * npg + 13:4 * npg + 13]
        ps, pr, pls, pss = rest[4 * npg + 13:]
        x, y, c, idx = _place()
        me = (x, y, c)
        sibling = (x, y, 1 - c)
        chips = [(1 - x, y), (x, 1 - y), (1 - x, 1 - y)]
        blk = lambda p: 4 * p[0] + 2 * p[1] + p[2]
        give = [blk((*ch, 1 - c)) for ch in chips] + [blk(sibling)]
        pair = [pltpu.make_async_remote_copy(src_ref=pg[a].at[b], dst_ref=pland[a].at[j], send_sem=ps.at[a, j],
                                             recv_sem=pr.at[a, j], device_id=sibling, device_id_type=MESH)
                for a in range(npg) for j, b in enumerate(give)]
        loads = [pltpu.make_async_copy(pg[a].at[blk((*ch, c))], pstage[a].at[j], pls.at[a, j])
                 for a in range(npg) for j, ch in enumerate(chips)]
        for cp in pair + loads:
            cp.start()
        vloc[...] = jnp.zeros((NVEC, D), F32)
        vloc[0:1, :] = accv_ref[1:2, :]
        vloc[1:2, :] = accv_ref[2:3, :]
        vloc[2:3, :] = accv_ref[0:1, :]
        vloc[3:4, :] = accf_ref[4:5, :]
        vloc[4:5, :] = accf_ref[5:6, :]
        vloc[5:6, :] = accf_ref[3:4, :]
        vloc[6:7, :] = accf_ref[0:1, :]
        vloc[7:8, :] = accf_ref[1:2, :]
        vloc[8:9, :] = accv_ref[3:4, :]
        vloc[9:10, :] = accb_ref[:, 0:D]
        vloc[10:11, :] = accb_ref[:, D:]
        vloc[11:12, 0:DA] = acca_ref[0:1, :]
        vloc[11:12, DA:] = acca_ref[1:2, :]
        bst = accbs_ref[...].T
        for h in range(NH):
            vloc[12:13, h * CH:(h + 1) * CH] = bst[h:h + 1, :]
        vloc[13:14, 0:DB] = acca_ref[4:5, :]
        vloc[13:14, DB:] = acca_ref[5:6, :]
        vloc[14:15, 0:DB] = acca_ref[6:7, :]
        vloc[14:15, DB:] = acca_ref[2:3, :]
        vloc[15:16, 0:DB] = acca_ref[3:4, :]
        vloc[16:17, :] = accf_ref[6:7, :]
        vloc[17:18, :] = accf_ref[2:3, :]
        vloc[_LOSS_ROW:_LOSS_ROW + 1, :] = accf_ref[7:8, :]
        vloc[_CW_ROW:_CW_ROW + HALO // 2, 0:DB] = acccw_ref[0:HALO // 2, :]
        vloc[_CW_ROW:_CW_ROW + HALO // 2, DB:] = acccw_ref[HALO // 2:, :]
        vbuf[idx] = vloc[...]
        rows_of = lambda t: pl.ds(pl.multiple_of(t * CH, CH), CH)
        wbuf[0] = dws_ref[rows_of(idx), :]
        sm = []
        for k in range(1, NDEV):
            t = idx ^ k
            sm.append(_small_copy(vloc, vbuf.at[idx], s1, r1, k - 1, _dev(t)))
            sm.append(_small_copy(dws_ref.at[rows_of(t)], wbuf.at[k], s2, r2, k - 1, _dev(t)))
        for cp in sm:
            cp.start()
        for k in range(1, NDEV):
            _small_copy(dws_ref.at[rows_of(0)], wbuf.at[k], s2, r2, k - 1, me).wait_recv()
        ws = wbuf[0]
        for k in range(1, NDEV):
            ws = ws + wbuf[k]
        wown[...] = ws
        wssum_ref[rows_of(idx), :] = ws
        ag = [_small_copy(wown, wssum_ref.at[rows_of(idx)], s3, r3, k - 1, _dev(idx ^ k)) for k in range(1, NDEV)]
        for cp in ag:
            cp.start()
        for cp in loads:
            cp.wait()
        for cp in pair:
            cp.wait_recv()
        stores = []
        for a in range(npg):
            for j in range(3):
                pstage[a][j] = (pstage[a][j].astype(F32) + pland[a][j].astype(F32)).astype(BF16)
                stores.append(pltpu.make_async_copy(pstage[a].at[j], pq[a].at[j], pss.at[a, j]))
            stores.append(pltpu.make_async_copy(pland[a].at[3], pq[a].at[3], pss.at[a, 3]))
        for cp in stores:
            cp.start()
        for k in range(1, NDEV):
            _small_copy(vloc, vbuf.at[idx ^ k], s1, r1, k - 1, me).wait_recv()
        vs = vbuf[0]
        for d in range(1, NDEV):
            vs = vs + vbuf[d]
        vsum_ref[...] = vs
        for d in range(NDEV):
            dcond_ref[d] = vbuf[d, 0:8, :]
        for k in range(1, NDEV):
            _small_copy(wown, wssum_ref.at[rows_of(idx ^ k)], s3, r3, k - 1, me).wait_recv()
        for cp in sm + ag:
            cp.wait_send()
        for cp in stores:
            cp.wait()
        for cp in pair:
            cp.wait_send()

    dma7 = pltpu.SemaphoreType.DMA((NDEV - 1,))
    dma4 = pltpu.SemaphoreType.DMA((npg, 4))
    outs = pl.pallas_call(
        body,
        name="reduce_small",
        in_specs=[VM] * 7 + [HBM] + [HBM] * npg,
        out_specs=[VM, VM, VM] + [HBM] * npg,
        out_shape=[jax.ShapeDtypeStruct((NVEC, D), F32), jax.ShapeDtypeStruct((NDEV, 8, D), F32),
                   jax.ShapeDtypeStruct(dws.shape, F32)]
                  + [jax.ShapeDtypeStruct((4,) + g.shape[1:], g.dtype) for g in pair_grads],
        scratch_shapes=[pltpu.VMEM((NVEC, D), F32), pltpu.VMEM((NDEV, NVEC, D), F32),
                        pltpu.VMEM((NDEV, CH, CH), F32), pltpu.VMEM((CH, CH), F32)] + [dma7] * 6
                       + [pltpu.VMEM((4,) + g.shape[1:], g.dtype) for g in pair_grads]
                       + [pltpu.VMEM((3,) + g.shape[1:], g.dtype) for g in pair_grads] + [dma4] * 4,
        compiler_params=pltpu.CompilerParams(vmem_limit_bytes=VMEM_LIMIT),
    )(acc_f, acc_v, acc_b, acc_a, acc_bs, acc_cw, dws, after, *pair_grads)
    return outs[0], outs[1], outs[2], outs[3:]


HBM_ONLY = pl.BlockSpec(memory_space=pltpu.HBM)
SEM = pl.BlockSpec(memory_space=pltpu.SEMAPHORE)
EFFECT = pltpu.SideEffectType.DATAFLOW_SIDE_EFFECTING


def _rs_copies(g_refs, land_refs, sems, chips):
    x, y, c, idx = _place()
    if chips:
        routes = [(j, j, (*ch, c)) for j, ch in enumerate([(1 - x, y), (x, 1 - y), (1 - x, 1 - y)])]
    else:
        routes = [(idx ^ k, k - 1, _dev(idx ^ k)) for k in range(1, NDEV)]
    cps = []
    for src, dst, to in routes:
        for a in range(len(g_refs)):
            n = len(cps)
            cps.append(pltpu.make_async_remote_copy(
                src_ref=g_refs[a].at[src], dst_ref=land_refs[a].at[dst], send_sem=sems[2 * n],
                recv_sem=sems[2 * n + 1], device_id=to, device_id_type=MESH))
    return cps


def _rs_start(grads, name, after=(), chips=False):
    nw = len(grads)
    npeer = 3 if chips else NDEV - 1
    nsem = 2 * nw * npeer
    lands = [lax.empty((npeer,) + g.shape[1:], g.dtype) for g in grads]

    def body(*refs):
        g_refs, land_refs = refs[:nw], refs[nw:2 * nw]
        sems = refs[2 * nw + len(after):2 * nw + len(after) + nsem]
        token = refs[-1]
        for cp in _rs_copies(g_refs, land_refs, sems, chips):
            cp.start()
        token[...] = jnp.zeros_like(token)

    outs = pl.pallas_call(
        body, name=name,
        out_shape=(*[pltpu.SemaphoreType.DMA(())] * nsem,
                   *[pltpu.HBM(g.shape, g.dtype) for g in grads], *[pltpu.HBM(l.shape, l.dtype) for l in lands],
                   jax.ShapeDtypeStruct((8, CH), F32)),
        in_specs=[HBM_ONLY] * (2 * nw) + [HBM] * len(after),
        out_specs=(*[SEM] * nsem, *[HBM_ONLY] * (2 * nw), VM),
        input_output_aliases={i: nsem + i for i in range(2 * nw)},
        compiler_params=pltpu.CompilerParams(has_side_effects=EFFECT),
    )(*[pltpu.with_memory_space_constraint(g, pltpu.HBM) for g in grads],
      *[pltpu.with_memory_space_constraint(l, pltpu.HBM) for l in lands], *after)
    return outs[:nsem], outs[nsem:nsem + nw], outs[nsem + nw:nsem + 2 * nw], outs[-1]


def _rs_wait(sems, g_thru, land_thru, after, name, chips=False):
    nw = len(g_thru)
    nsem = len(sems)

    def body(*refs):
        g_refs, land_refs = refs[:nw], refs[nw:2 * nw]
        for cp in _rs_copies(g_refs, land_refs, refs[2 * nw:2 * nw + nsem], chips):
            cp.wait_send()
            cp.wait_recv()

    outs = pl.pallas_call(
        body, name=name,
        out_shape=tuple(pltpu.HBM(a.shape, a.dtype) for a in list(g_thru) + list(land_thru)),
        in_specs=[HBM_ONLY] * (2 * nw) + [SEM] * nsem + [HBM] * len(after),
        out_specs=tuple([HBM_ONLY] * (2 * nw)),
        input_output_aliases={i: i for i in range(2 * nw)},
        compiler_params=pltpu.CompilerParams(has_side_effects=EFFECT),
    )(*g_thru, *land_thru, *sems, *after)
    return outs[:nw], outs[nw:]


def _adamw(w, g, m, v):
    m2 = ADAM_B1 * m + (1.0 - ADAM_B1) * g
    v2 = ADAM_B2 * v + (1.0 - ADAM_B2) * (g * g)
    m_hat = m2 / (1.0 - ADAM_B1 ** ADAM_STEP)
    v_hat = v2 / (1.0 - ADAM_B2 ** ADAM_STEP)
    delta = -ADAM_LR * (m_hat / (jnp.sqrt(v_hat) + ADAM_EPS) + ADAM_WD * w)
    return delta, m2, v2


def _adam_big(r, w, m, v, rb, name, own, after=None, sib=None):
    R, C = w.shape
    ns = r.shape[0]
    g_all, idx1 = own

    def body(idx_ref, r_ref, own_ref, *refs):
        w_ref, m_ref, v_ref, g_ref, d_ref, m2_ref, v2_ref = refs[len(refs) - 7:]
        g = own_ref[0].astype(F32)
        if sib is not None:
            g = g + refs[0][0].astype(F32)
        for k in range(ns):
            g = g + r_ref[k].astype(F32)
        g_ref[...] = g
        d_ref[...], m2_ref[...], v2_ref[...] = _adamw(w_ref[...], g, m_ref[...], v_ref[...])

    t2 = pl.BlockSpec((rb, C), lambda i, idx_ref: (i, 0))
    sd = jax.ShapeDtypeStruct((R, C), F32)
    extra_specs = ([pl.BlockSpec((1, rb, C), lambda i, idx_ref: (3, i, 0))] if sib is not None else []) \
        + ([HBM] if after is not None else [])
    extra = ([sib] if sib is not None else []) + ([after] if after is not None else [])
    return pl.pallas_call(
        body, name=name,
        grid_spec=pltpu.PrefetchScalarGridSpec(
            num_scalar_prefetch=1, grid=(R // rb,),
            in_specs=[pl.BlockSpec((ns, rb, C), lambda i, idx_ref: (0, i, 0)),
                      pl.BlockSpec((1, rb, C), lambda i, idx_ref: (idx_ref[0], i, 0))] + extra_specs + [t2, t2, t2],
            out_specs=[t2, t2, t2, t2]),
        out_shape=[sd, sd, sd, sd],
        compiler_params=pltpu.CompilerParams(dimension_semantics=("arbitrary",), vmem_limit_bytes=VMEM_LIMIT),
    )(idx1, r, g_all, *extra, w, m, v)


def _adam_ada(cact_t, dcs, w, m, v, rb, name):
    R, C = w.shape

    def body(ct_ref, dc_ref, w_ref, m_ref, v_ref, g_ref, d_ref, m2_ref, v2_ref):
        g = jnp.dot(ct_ref[...], dc_ref[...], preferred_element_type=F32, precision=lax.Precision.HIGHEST)
        g_ref[...] = g
        d_ref[...], m2_ref[...], v2_ref[...] = _adamw(w_ref[...], g, m_ref[...], v_ref[...])

    t2 = pl.BlockSpec((rb, C), lambda i: (i, 0))
    sd = jax.ShapeDtypeStruct((R, C), F32)
    return pl.pallas_call(
        body, name=name, grid=(R // rb,),
        in_specs=[pl.BlockSpec((rb, NDEV), lambda i: (i, 0)), _full((NDEV, C)), t2, t2, t2],
        out_specs=[t2, t2, t2, t2], out_shape=[sd, sd, sd, sd],
        compiler_params=pltpu.CompilerParams(dimension_semantics=("arbitrary",), vmem_limit_bytes=VMEM_LIMIT),
    )(cact_t, dcs, w, m, v)


_SMALL = ["ada_b", "ada_f_b", "norm1_g", "b_in", "a_ln_g", "a_ln_b", "a_spatial_b", "b_conv_b", "b_gn_g", "b_gn_b",
          "out_norm_a_g", "out_norm_b_g", "norm2_g", "norm_f_g", "a_spatial_w", "b_conv_w"]


def _adam_small(vsum, wssum, gcw, params):
    names = _SMALL
    flat = []
    for n in names:
        flat += list(params[n])

    def body(vs_ref, ws_ref, gcw_ref, *rest):
        ins = rest[:3 * len(names)]
        outs = rest[3 * len(names):]
        for pi, n in enumerate(names):
            w_ref, m_ref, v_ref = ins[3 * pi:3 * pi + 3]
            g_ref, d_ref, m2_ref, v2_ref = outs[4 * pi:4 * pi + 4]
            if n in ("ada_b", "ada_f_b", "b_in"):
                row0 = {"ada_b": 0, "ada_f_b": 6, "b_in": 9}[n]
                pieces = [(vs_ref[row0 + r:row0 + r + 1, :], slice(r * D, (r + 1) * D))
                          for r in range(w_ref.shape[1] // D)]
            elif n == "a_spatial_w":
                pieces = [(ws_ref[...], slice(None))]
            elif n == "b_conv_w":
                pieces = [(gcw_ref[...], slice(None))]
            else:
                row, off, width = _VEC_AT[n]
                pieces = [(vs_ref[row:row + 1, off:off + width], slice(None))]
            for g, cs in pieces:
                g_ref[:, cs] = g
                d_ref[:, cs], m2_ref[:, cs], v2_ref[:, cs] = _adamw(w_ref[:, cs], g, m_ref[:, cs], v_ref[:, cs])

    out_shape = []
    for n in names:
        out_shape += [jax.ShapeDtypeStruct(params[n][0].shape, F32)] * 4
    outs = pl.pallas_call(
        body, name="adam_small",
        in_specs=[VM] * (3 + len(flat)), out_specs=[VM] * len(out_shape), out_shape=out_shape,
        compiler_params=pltpu.CompilerParams(vmem_limit_bytes=VMEM_LIMIT),
    )(vsum, wssum, gcw, *flat)
    return {n: outs[4 * pi:4 * pi + 4] for pi, n in enumerate(names)}


def _token_tile(T, want):
    return want if T % want == 0 else T


def kernel(x, c, ada_w, ada_b, norm1_g, w_in, b_in, a_ln_g, a_ln_b, a_spatial_w, a_spatial_b, b_conv_w, b_conv_b, b_gn_g, b_gn_b, out_norm_a_g, out_norm_b_g, w_out, norm2_g, w_ffn_in, w_ffn_out, ada_f_w, ada_f_b, norm_f_g, loss_target, m_ada_w, m_ada_b, m_norm1_g, m_w_in, m_b_in, m_a_ln_g, m_a_ln_b, m_a_spatial_w, m_a_spatial_b, m_b_conv_w, m_b_conv_b, m_b_gn_g, m_b_gn_b, m_out_norm_a_g, m_out_norm_b_g, m_w_out, m_norm2_g, m_w_ffn_in, m_w_ffn_out, m_ada_f_w, m_ada_f_b, m_norm_f_g, v_ada_w, v_ada_b, v_norm1_g, v_w_in, v_b_in, v_a_ln_g, v_a_ln_b, v_a_spatial_w, v_a_spatial_b, v_b_conv_w, v_b_conv_b, v_b_gn_g, v_b_gn_b, v_out_norm_a_g, v_out_norm_b_g, v_w_out, v_norm2_g, v_w_ffn_in, v_w_ffn_out, v_ada_f_w, v_ada_f_b, v_norm_f_g):
    T = x.shape[1]
    idx = 4 * lax.axis_index("x") + 2 * lax.axis_index("y") + lax.axis_index("c")
    x2d = x.reshape(T, D)
    tgt = loss_target.reshape(T, D)

    conv_s = jnp.pad(b_conv_w[0], ((0, HALO - KW), (0, 0)))
    call, cparts, cfparts, convg, (win_g, wout_g) = _gather(
        c, ada_w[0], ada_b.reshape(NDEV, -1), ada_f_w, ada_f_b.reshape(NDEV, -1), conv_s,
        [w_in[0].astype(BF16), w_out[0].astype(BF16)])
    wout = wout_g.reshape(D, D)
    mod = jnp.concatenate([cparts.reshape(6, D), cfparts.reshape(2, D)], axis=0)
    cw = jnp.transpose(convg, (1, 0, 2)).reshape(HALO, DB)

    tril = jnp.tril(jnp.ones((CH, CH), dtype=bool))
    wsm = jnp.where(tril[None], a_spatial_w[0], 0.0).astype(BF16)
    wcat = wsm.reshape(NH * CH, CH)
    wcat_t = jnp.transpose(wsm, (0, 2, 1)).reshape(NH * CH, CH)
    bsf = jnp.repeat(a_spatial_b[0].T, DA // NH, axis=1)
    lane = jnp.arange(DB)
    pm = jnp.where((lane[:, None] >> 6) == (lane[None, :] >> 6), 1.0 / 64.0, 0.0).astype(BF16)
    esel = jnp.where((lane[:, None] >> 6) == jnp.arange(CH)[None, :], 1.0, 0.0).astype(BF16)

    tm = _token_tile(T, 256)
    tk = _token_tile(T, 2048)
    (x1, hb, zvg, mixed, yb, o, gu, dgelu_u, dgelu_v, vhat, rslb, yhat, rsg), (wfi_g, wfo_g) = _mix_fwd(
        x2d, mod, norm1_g, win_g, b_in, a_ln_g, a_ln_b, wcat, bsf, cw, b_conv_b, b_gn_g, b_gn_b, out_norm_a_g,
        out_norm_b_g, wout, pm, [w_ffn_in[0].T.astype(BF16), w_ffn_out[0].astype(BF16)], _token_tile(T, 512))
    dx1, h2b, dgu, act, dxg, acc_f = _ffn(x1, tgt, mod, norm2_g, norm_f_g.reshape(1, D),
                                          wfi_g.reshape(2 * DFF, D), wfo_g.reshape(DFF, D), tm)
    g_wfi = _wgrad_rows(dgu, h2b, 2 * WFI_B, tk, "wgrad_ffn_in").reshape(NDEV, WFI_B, D)
    g_wfo = _wgrad_rows(act, dxg, 2 * WFI_B, tk, "wgrad_ffn_out").reshape(NDEV, DFF // NDEV, D)
    f_sems, f_thru, f_land, f_token = _rs_start([g_wfi, g_wfo], "rs_ffn_start")
    (gx, acc_v, acc_b, acc_a, acc_bs, acc_ws, acc_cw), (g_win, g_wout) = _mix_bwd(
        dx1, x2d, zvg, mixed, o, hb, yb, gu, dgelu_u, dgelu_v, vhat, rslb, yhat, rsg, mod, norm1_g, win_g, a_ln_g,
        a_ln_b, wcat, wcat_t, cw, b_gn_g, b_gn_b, out_norm_a_g, out_norm_b_g, wout, pm, esel, f_token, tm)
    (g_wfi_d, g_wfo_d), (r_wfi, r_wfo) = _rs_wait(f_sems, f_thru, f_land, [acc_v], "rs_ffn_wait")
    g_wout = g_wout.reshape(NDEV, D // NDEV, D)

    vsum, dcond_all, wssum, (q_win, q_wout) = _reduce_small(acc_f, acc_v, acc_b, acc_a, acc_bs, acc_cw, acc_ws,
                                                            g_wfi_d, [g_win, g_wout])
    sems, g_thru, land_thru, token = _rs_start([q_win, q_wout], "rs_mix_start", after=(vsum,), chips=True)

    own = lambda g: (g, jnp.reshape(idx, (1,)).astype(jnp.int32))
    res = {}
    res["w_ffn_in"] = tuple(a.T for a in _adam_big(r_wfi, w_ffn_in[0].T, m_w_ffn_in[0].T, v_w_ffn_in[0].T, WFI_B // 4,
                                                   "adam_w_ffn_in", own=own(g_wfi_d), after=token))
    res["w_ffn_out"] = _adam_big(r_wfo, w_ffn_out[0], m_w_ffn_out[0], v_w_ffn_out[0], DFF // NDEV // 2,
                                 "adam_w_ffn_out", own=own(g_wfo_d), after=token)
    cact_t = (call * jax.nn.sigmoid(call)).T
    dcond = dcond_all.reshape(NDEV, 8 * D)
    nada = ada_w.shape[2]
    nadf = ada_f_w.shape[1]
    dcs = lax.dynamic_slice(dcond, (0, idx * nada), (NDEV, nada))
    dcfs = lax.dynamic_slice(dcond, (0, 6 * D + idx * nadf), (NDEV, nadf))
    res["ada_w"] = _adam_ada(cact_t, dcs, ada_w[0], m_ada_w[0], v_ada_w[0], 512, "adam_ada_w")
    res["ada_f_w"] = _adam_ada(cact_t, dcfs, ada_f_w, m_ada_f_w, v_ada_f_w, 512, "adam_ada_f_w")
    ncw = b_conv_w.shape[2]
    gcw = jnp.concatenate([lax.dynamic_slice(vsum, (_CW_ROW, idx * ncw), (HALO // 2, ncw)),
                           lax.dynamic_slice(vsum, (_CW_ROW, DB + idx * ncw), (HALO // 2, ncw))], axis=0)[:KW]
    two = lambda a: a.reshape(1, -1) if a.ndim == 1 else a.reshape(-1, a.shape[-1])
    small_in = {
        "ada_b": (ada_b, m_ada_b, v_ada_b), "ada_f_b": (ada_f_b, m_ada_f_b, v_ada_f_b),
        "norm1_g": (norm1_g, m_norm1_g, v_norm1_g), "b_in": (b_in, m_b_in, v_b_in),
        "a_ln_g": (a_ln_g, m_a_ln_g, v_a_ln_g), "a_ln_b": (a_ln_b, m_a_ln_b, v_a_ln_b),
        "a_spatial_b": (a_spatial_b.reshape(1, D), m_a_spatial_b.reshape(1, D), v_a_spatial_b.reshape(1, D)),
        "b_conv_b": (b_conv_b, m_b_conv_b, v_b_conv_b), "b_gn_g": (b_gn_g, m_b_gn_g, v_b_gn_g),
        "b_gn_b": (b_gn_b, m_b_gn_b, v_b_gn_b), "out_norm_a_g": (out_norm_a_g, m_out_norm_a_g, v_out_norm_a_g),
        "out_norm_b_g": (out_norm_b_g, m_out_norm_b_g, v_out_norm_b_g),
        "norm2_g": (norm2_g, m_norm2_g, v_norm2_g), "norm_f_g": (norm_f_g, m_norm_f_g, v_norm_f_g),
        "a_spatial_w": (a_spatial_w, m_a_spatial_w, v_a_spatial_w),
        "b_conv_w": (b_conv_w[0], m_b_conv_w[0], v_b_conv_w[0]),
    }
    small_in = {n: tuple(two(a) for a in t) for n, t in small_in.items()}
    res.update(_adam_small(vsum, wssum, gcw, small_in))
    (q_win_d, q_wout_d), (r_win, r_wout) = _rs_wait(
        sems, g_thru, land_thru,
        [res["w_ffn_in"][0], res["w_ffn_out"][0], res["ada_w"][0], res["ada_f_w"][0], res["norm_f_g"][0]],
        "rs_mix_wait", chips=True)
    res["w_in"] = _adam_big(r_win, w_in[0], m_w_in[0], v_w_in[0], 256, "adam_w_in", own=own(g_win), sib=q_win_d)
    res["w_out"] = _adam_big(r_wout, w_out[0], m_w_out[0], v_w_out[0], D // NDEV, "adam_w_out", own=own(g_wout),
                             sib=q_wout_d)

    loss = 0.5 / D * jnp.sum(vsum[_LOSS_ROW])
    shapes = {"ada_w": ada_w, "ada_b": ada_b, "norm1_g": norm1_g, "w_in": w_in, "b_in": b_in, "a_ln_g": a_ln_g,
              "a_ln_b": a_ln_b, "a_spatial_w": a_spatial_w, "a_spatial_b": a_spatial_b, "b_conv_w": b_conv_w,
              "b_conv_b": b_conv_b, "b_gn_g": b_gn_g, "b_gn_b": b_gn_b, "out_norm_a_g": out_norm_a_g,
              "out_norm_b_g": out_norm_b_g, "w_out": w_out, "norm2_g": norm2_g, "w_ffn_in": w_ffn_in,
              "w_ffn_out": w_ffn_out, "ada_f_w": ada_f_w, "ada_f_b": ada_f_b, "norm_f_g": norm_f_g}
    order = list(shapes)
    outs = [loss, gx.reshape(x.shape)]
    for which in range(4):
        outs += [res[n][which].reshape(shapes[n].shape) for n in order]
    return tuple(outs)
```

```python
import math

import jax
import jax.numpy as jnp
from jax import lax
from jax.experimental import pallas as pl
from jax.experimental.pallas import tpu as pltpu

F32 = jnp.float32
BF16 = jnp.bfloat16

D = 1024
DA = 512
DB = 512
DIN = 2048
DFF = 2816
NH = 8
CH = 128
KW = 31
HALO = 32
NDEV = 8
WIN_B = DIN // NDEV
WFI_B = 2 * DFF // NDEV
EPS = 1e-6
NVEC = 40
VMEM_LIMIT = 56 * 1024 * 1024

ADAM_LR, ADAM_B1, ADAM_B2, ADAM_EPS, ADAM_WD, ADAM_STEP = 0.001, 0.9, 0.999, 1e-08, 0.01, 10

MESH = pl.DeviceIdType.MESH


def _dot(a, b):
    return jnp.dot(a, b, preferred_element_type=F32)


def _dot_nt(a, b):
    return lax.dot_general(a, b, (((1,), (1,)), ((), ())), preferred_element_type=F32)


def _dot_tn(a, b):
    return lax.dot_general(a, b, (((0,), (0,)), ((), ())), preferred_element_type=F32)


def _rs(v):
    return lax.rsqrt(jnp.mean(v * v, axis=-1, keepdims=True) + EPS)


def _sig(v):
    return 1.0 / (1.0 + jnp.exp(-v))


_INV_SQRT2 = 1.0 / math.sqrt(2.0)
_INV_SQRT2PI = 1.0 / math.sqrt(2.0 * math.pi)


def _gelu_parts(v):
    cdf = 0.5 * (1.0 + lax.erf(v * _INV_SQRT2))
    pdf = jnp.exp(-0.5 * v * v) * _INV_SQRT2PI
    return v * cdf, cdf + v * pdf


def _grp_mean(v, pm):
    hi = v.astype(BF16)
    lo = (v - hi.astype(F32)).astype(BF16)
    return _dot(hi, pm) + _dot(lo, pm)


def _colsum(v):
    return jnp.sum(v, axis=0, keepdims=True)


def _full(shape):
    nd = len(shape)
    return pl.BlockSpec(shape, lambda *_: (0,) * nd)


def _resident(shape):
    nd = len(shape)
    return pl.BlockSpec(shape, lambda *_: (0,) * nd, pipeline_mode=pl.Buffered(1))


HBM = pl.BlockSpec(memory_space=pl.ANY)
VM = pl.BlockSpec(memory_space=pltpu.VMEM)


SH_ROWS = HALO - 8


def _shifted_copies(buf, shbuf, tm):
    for b in range(1, 8):
        shbuf[b - 1] = buf[b:b + tm + SH_ROWS, :]


def _window(buf, shbuf, off, tm):
    a, b = divmod(off, 8)
    if b == 0:
        return buf[8 * a:8 * a + tm, :]
    return shbuf[b - 1, 8 * a:8 * a + tm, :]


def _first_head_lanes():
    return lax.broadcasted_iota(jnp.int32, (CH, CH), 1) < (DA // NH)


def _mix_heads(w_ref, vb, first):
    outs = []
    for p in range(NH // 2):
        v = vb[:, p * CH:(p + 1) * CH]
        a = _dot(w_ref[(2 * p) * CH:(2 * p + 1) * CH, :], v)
        b = _dot(w_ref[(2 * p + 1) * CH:(2 * p + 2) * CH, :], v)
        outs.append(jnp.where(first, a, b))
    return jnp.concatenate(outs, axis=1)


def _place():
    x, y, c = lax.axis_index("x"), lax.axis_index("y"), lax.axis_index("c")
    return x, y, c, 4 * x + 2 * y + c


def _dev(t):
    return (t >> 2, (t >> 1) & 1, t & 1)


class _AllGather:
    def __init__(self, w_in, w_out, wss, wrs, lsem):
        x, y, c, idx = _place()
        me, sibling = (x, y, c), (x, y, 1 - c)
        chips = [(1 - x, y), (x, 1 - y), (1 - x, 1 - y)]
        nw = len(w_in)

        def blk(p):
            return 4 * p[0] + 2 * p[1] + p[2]

        def wcopy(a, k, block, to, src=None):
            dst = w_out[a].at[blk(block)]
            return pltpu.make_async_remote_copy(src_ref=dst if src is None else src, dst_ref=dst,
                                                send_sem=wss.at[a, k], recv_sem=wrs.at[a, k],
                                                device_id=to, device_id_type=MESH)

        self.mine = [pltpu.make_async_copy(w_in[a], w_out[a].at[idx], lsem.at[a]) for a in range(nw)]
        self.first = []
        for a in range(nw):
            self.first.append(wcopy(a, 0, me, sibling, src=w_in[a]))
            self.first += [wcopy(a, 1 + j, me, (*chip, c), src=w_in[a]) for j, chip in enumerate(chips)]
        self.landed = [[wcopy(a, 1 + j, (*chip, c), me) for a in range(nw)] for j, chip in enumerate(chips)]
        self.passed = [[wcopy(a, 4 + j, (*chip, c), sibling) for a in range(nw)] for j, chip in enumerate(chips)]
        self.from_sibling = []
        for a in range(nw):
            self.from_sibling.append(wcopy(a, 0, sibling, me))
            self.from_sibling += [wcopy(a, 4 + j, (*chip, 1 - c), me) for j, chip in enumerate(chips)]

    def start(self):
        for cp in self.mine + self.first:
            cp.start()

    def forward(self):
        for land, pas in zip(self.landed, self.passed):
            for l, p in zip(land, pas):
                l.wait_recv()
                p.start()

    def finish(self):
        for cp in self.from_sibling:
            cp.wait_recv()
        for cp in self.first:
            cp.wait_send()
        for pas in self.passed:
            for p in pas:
                p.wait_send()
        for cp in self.mine:
            cp.wait()


AG_SEMS = lambda nw: [pltpu.SemaphoreType.DMA((nw, 7)), pltpu.SemaphoreType.DMA((nw, 7)),
                      pltpu.SemaphoreType.DMA((nw,))]


def _mix_fwd(x, mod, g1, win, b_in, lng, lnb, wcat, bsf, cw, cb, gng, gnb, oga, ogb, wout, pm, ffn_shards, tm):
    T = x.shape[0]
    nt = T // tm
    nch = tm // CH
    nw = len(ffn_shards)
    fwd_step = (5 * nt) // 8
    saved = [(D, F32), (D, BF16), (2 * DB, F32), (DA, F32), (D, BF16), (D, F32), (DA, F32), (DA, F32), (DA, F32),
             (DA, F32), (CH, F32), (DB, F32), (DB, F32)]
    NSAVE = len(saved)

    def body(x_ref, mod_ref, g1_ref, win_ref, bin_ref, lng_ref, lnb_ref, wcat_ref, bsf_ref, cw_ref, cb_ref,
             gng_ref, gnb_ref, oga_ref, ogb_ref, wout_ref, pm_ref, *rest):
        sh_in = rest[:nw]
        (x1_ref, h_ref, zvg_ref, mixed_ref, y_ref, o_ref, gu_ref, dgu_ref, dgv_ref, vhat_ref, rsl_ref, yhat_ref,
         rsg_ref) = rest[nw:nw + NSAVE]
        sh_out = rest[nw + NSAVE:2 * nw + NSAVE]
        glbuf, shbuf, wss, wrs, lsem = rest[2 * nw + NSAVE:]
        i = pl.program_id(0)

        @pl.when(i == 0)
        def _():
            _AllGather(sh_in, sh_out, wss, wrs, lsem).start()

        xv = x_ref[...]
        shift1 = mod_ref[0:1, :]
        scale1 = mod_ref[1:2, :]
        gate1 = mod_ref[2:3, :]
        h = (xv * _rs(xv) * g1_ref[...]) * (1.0 + scale1) + shift1
        hb = h.astype(BF16)
        h_ref[...] = hb
        z = jnp.concatenate([_dot(hb, win_ref[j]) for j in range(NDEV)], axis=1) + bin_ref[...]
        zvg_ref[...] = z[:, 2 * DA:]
        gu, dgelu_u = _gelu_parts(z[:, 0:DA])
        gv, dgelu_v = _gelu_parts(z[:, DA:2 * DA])
        gu_ref[...] = gu
        dgu_ref[...] = dgelu_u
        dgv_ref[...] = dgelu_v
        xc = gv - jnp.mean(gv, axis=-1, keepdims=True)
        rsl = lax.rsqrt(jnp.mean(xc * xc, axis=-1, keepdims=True) + EPS)
        vhat = xc * rsl
        vhat_ref[...] = vhat
        rsl_ref[...] = jnp.broadcast_to(rsl, (tm, CH))
        vnb = (vhat * lng_ref[...] + lnb_ref[...]).astype(BF16)
        first = _first_head_lanes()
        chunks = []
        for ci in range(nch):
            chunks.append(_mix_heads(wcat_ref, vnb[ci * CH:(ci + 1) * CH, :], first) + bsf_ref[...])
        mixed = jnp.concatenate(chunks, axis=0) if nch > 1 else chunks[0]
        mixed_ref[...] = mixed
        ya = gu * mixed
        gl = z[:, 2 * DA:2 * DA + DB] * _sig(z[:, 2 * DA + DB:])

        @pl.when(i == 0)
        def _():
            glbuf[0:HALO, :] = jnp.zeros((HALO, DB), F32)

        glbuf[HALO:HALO + tm, :] = gl
        _shifted_copies(glbuf, shbuf, tm)
        yc = jnp.zeros((tm, DB), F32) + cb_ref[...]
        for k in range(KW):
            yc = yc + cw_ref[k:k + 1, :] * _window(glbuf, shbuf, HALO - (KW - 1) + k, tm)
        glbuf[0:HALO, :] = gl[tm - HALO:, :]
        pmv = pm_ref[...]
        dc = yc - _grp_mean(yc, pmv)
        rsg = lax.rsqrt(_grp_mean(dc * dc, pmv) + EPS)
        yhat = dc * rsg
        yhat_ref[...] = yhat
        rsg_ref[...] = rsg
        yg = yhat * gng_ref[...] + gnb_ref[...]
        yb = yg * _sig(yg)
        na = ya * _rs(ya) * oga_ref[...]
        nb = yb * _rs(yb) * ogb_ref[...]
        yv = jnp.concatenate([na, nb], axis=1).astype(BF16)
        y_ref[...] = yv
        o = _dot(yv, wout_ref[...])
        o_ref[...] = o
        x1_ref[...] = xv + gate1 * o

        @pl.when(i == fwd_step)
        def _():
            _AllGather(sh_in, sh_out, wss, wrs, lsem).forward()

        @pl.when(i == nt - 1)
        def _():
            _AllGather(sh_in, sh_out, wss, wrs, lsem).finish()

    tile = lambda w: pl.BlockSpec((tm, w), lambda i: (i, 0))
    outs = pl.pallas_call(
        body,
        name="mix_fwd",
        grid=(nt,),
        in_specs=[tile(D), _full((8, D)), _full((1, D)), _resident((NDEV, D, WIN_B)), _full((1, DIN)),
                  _full((1, DA)), _full((1, DA)), _full((NH * CH, CH)), _full((CH, DA)), _full((HALO, DB)),
                  _full((1, DB)), _full((1, DB)), _full((1, DB)), _full((1, DA)), _full((1, DB)),
                  _resident((D, D)), _full((DB, DB))] + [HBM] * nw,
        out_specs=[tile(w) for w, _ in saved] + [HBM] * nw,
        out_shape=[jax.ShapeDtypeStruct((T, w), dt) for w, dt in saved]
                  + [jax.ShapeDtypeStruct((NDEV,) + s.shape, s.dtype) for s in ffn_shards],
        scratch_shapes=[pltpu.VMEM((HALO + tm, DB), F32), pltpu.VMEM((7, tm + SH_ROWS, DB), F32)] + AG_SEMS(nw),
        compiler_params=pltpu.CompilerParams(dimension_semantics=("arbitrary",), vmem_limit_bytes=VMEM_LIMIT),
    )(x, mod, g1, win, b_in, lng, lnb, wcat, bsf, cw, cb, gng, gnb, oga, ogb, wout, pm, *ffn_shards)
    return outs[:NSAVE], outs[NSAVE:]


FF_BLOCKS = ((0, 1024), (1024, 1024), (2048, 768))


def _ffn(x1, tgt, mod, g2, gf, wfi_t, wfo, tm):
    T = x1.shape[0]
    nt = T // tm

    def body(x1_ref, tgt_ref, mod_ref, g2_ref, gf_ref, wfi_ref, wfo_ref,
             dx1_ref, h2_ref, dgu_ref, act_ref, dxg_ref, acc_ref, g_s, u_s):
        i = pl.program_id(0)

        @pl.when(i == 0)
        def _():
            acc_ref[...] = jnp.zeros((8, D), F32)

        x1 = x1_ref[...]
        shift2 = mod_ref[3:4, :]
        scale2 = mod_ref[4:5, :]
        gate2 = mod_ref[5:6, :]
        shiftf = mod_ref[6:7, :]
        scalef = mod_ref[7:8, :]
        g2v = g2_ref[...]
        gfv = gf_ref[...]
        r2 = _rs(x1)
        xn2 = x1 * r2
        h2b = (xn2 * g2v * (1.0 + scale2) + shift2).astype(BF16)
        h2_ref[...] = h2b
        f = jnp.zeros((tm, D), F32)
        for o, w in FF_BLOCKS:
            g = _dot_nt(h2b, wfi_ref[o:o + w, :])
            u = _dot_nt(h2b, wfi_ref[DFF + o:DFF + o + w, :])
            g_s[:, o:o + w] = g
            u_s[:, o:o + w] = u
            actb = (g * _sig(g) * u).astype(BF16)
            act_ref[:, o:o + w] = actb
            f = f + _dot(actb, wfo_ref[o:o + w, :])
        x2 = x1 + gate2 * f
        rf = _rs(x2)
        xnf = x2 * rf
        out = xnf * gfv * (1.0 + scalef) + shiftf
        e = out - tgt_ref[...]
        dout = e * (1.0 / D)
        acc_ref[7:8, :] += _colsum(e * e)
        acc_ref[0:1, :] += _colsum(dout)
        acc_ref[1:2, :] += _colsum(dout * xnf * gfv)
        acc_ref[2:3, :] += _colsum(dout * (1.0 + scalef) * xnf)
        dxnf = dout * (1.0 + scalef) * gfv
        dx2 = rf * (dxnf - xnf * jnp.mean(dxnf * xnf, axis=-1, keepdims=True))
        acc_ref[3:4, :] += _colsum(dx2 * f)
        dxgb = (dx2 * gate2).astype(BF16)
        dxg_ref[...] = dxgb
        dh2 = jnp.zeros((tm, D), F32)
        for o, w in FF_BLOCKS:
            dact = _dot_nt(dxgb, wfo_ref[o:o + w, :])
            g = g_s[:, o:o + w]
            u = u_s[:, o:o + w]
            s = _sig(g)
            dgb = (dact * u * (s * (1.0 + g * (1.0 - s)))).astype(BF16)
            dub = (dact * (g * s)).astype(BF16)
            dgu_ref[:, o:o + w] = dgb
            dgu_ref[:, DFF + o:DFF + o + w] = dub
            dh2 = dh2 + _dot(dgb, wfi_ref[o:o + w, :])
            dh2 = dh2 + _dot(dub, wfi_ref[DFF + o:DFF + o + w, :])
        acc_ref[4:5, :] += _colsum(dh2)
        acc_ref[5:6, :] += _colsum(dh2 * xn2 * g2v)
        acc_ref[6:7, :] += _colsum(dh2 * (1.0 + scale2) * xn2)
        dxn2 = dh2 * (1.0 + scale2) * g2v
        dx1_ref[...] = dx2 + r2 * (dxn2 - xn2 * jnp.mean(dxn2 * xn2, axis=-1, keepdims=True))

    tile = lambda w: pl.BlockSpec((tm, w), lambda i: (i, 0))
    return pl.pallas_call(
        body,
        name="ffn_fwd_bwd",
        grid=(nt,),
        in_specs=[tile(D), tile(D), _full((8, D)), _full((1, D)), _full((1, D)),
                  _resident((2 * DFF, D)), _resident((DFF, D))],
        out_specs=[tile(D), tile(D), tile(2 * DFF), tile(DFF), tile(D), _full((8, D))],
        out_shape=[jax.ShapeDtypeStruct((T, D), F32), jax.ShapeDtypeStruct((T, D), BF16),
                   jax.ShapeDtypeStruct((T, 2 * DFF), BF16), jax.ShapeDtypeStruct((T, DFF), BF16),
                   jax.ShapeDtypeStruct((T, D), BF16), jax.ShapeDtypeStruct((8, D), F32)],
        scratch_shapes=[pltpu.VMEM((tm, DFF), F32), pltpu.VMEM((tm, DFF), F32)],
        compiler_params=pltpu.CompilerParams(dimension_semantics=("arbitrary",), vmem_limit_bytes=VMEM_LIMIT),
    )(x1, tgt, mod, g2, gf, wfi_t, wfo)


def _mix_bwd(dx1, x, zvg, mixed, o, hb, yb, gu, dgu, dgv, vhat, rslb, yhat, rsg, mod, g1, win, lng, lnb, wcat, wcat_t,
             cw, gng, gnb, oga, ogb, wout, pm, esel, after, tm):
    T = x.shape[0]
    nt = T // tm
    nch = tm // CH
    WOB = 256

    def body(dx1_ref, x_ref, zvg_ref, mixed_ref, o_ref, hb_ref, yb_ref, gu_ref, dgu_ref, dgv_ref, vhat_ref, rsl_ref,
             yhat_ref, rsg_ref, mod_ref, g1_ref, win_ref, lng_ref, lnb_ref, wcat_ref, wcatt_ref, cw_ref, gng_ref,
             gnb_ref, oga_ref, ogb_ref, wout_ref, pm_ref, esel_ref, after_ref,
             gx_ref, accv_ref, accb_ref, acca_ref, accbs_ref, accws_ref, acccw_ref, gwin_ref, gwout_ref,
             dycbuf, shbuf, bs_s, acc_win, acc_wout, st_win, st_wout):
        i = pl.program_id(0)

        @pl.when(i == 0)
        def _():
            acc_win[...] = jnp.zeros((NDEV, D, WIN_B), F32)
            acc_wout[...] = jnp.zeros((D, D), F32)
            accv_ref[...] = jnp.zeros((8, D), F32)
            accb_ref[...] = jnp.zeros((1, DIN), F32)
            acca_ref[...] = jnp.zeros((8, DA), F32)
            accws_ref[...] = jnp.zeros((NH * CH, CH), F32)
            acccw_ref[...] = jnp.zeros((HALO, DB), F32)
            bs_s[...] = jnp.zeros((CH, DA), F32)
            dycbuf[tm:tm + HALO, :] = jnp.zeros((HALO, DB), F32)

        shift1 = mod_ref[0:1, :]
        scale1 = mod_ref[1:2, :]
        gate1 = mod_ref[2:3, :]
        g1v = g1_ref[...]
        xv = x_ref[...]
        r1 = _rs(xv)
        xn1 = xv * r1
        val = zvg_ref[:, 0:DB]
        gate = zvg_ref[:, DB:]
        gu = gu_ref[...]
        dgelu_u = dgu_ref[...]
        dgelu_v = dgv_ref[...]
        vhat = vhat_ref[...]
        rsl = rsl_ref[:, 0:1]
        lngv = lng_ref[...]
        vnb = (vhat * lngv + lnb_ref[...]).astype(BF16)
        mixed = mixed_ref[...]
        ya = gu * mixed
        ra = _rs(ya)
        yan = ya * ra
        sgt = _sig(gate)
        gl = val * sgt
        pmv = pm_ref[...]
        rsg = rsg_ref[...]
        yhat = yhat_ref[...]
        gngv = gng_ref[...]
        yg = yhat * gngv + gnb_ref[...]
        sgy = _sig(yg)
        yb = yg * sgy
        rb = _rs(yb)
        ybn = yb * rb
        dx1 = dx1_ref[...]
        accv_ref[0:1, :] += _colsum(dx1 * o_ref[...])
        dogb = (dx1 * gate1).astype(BF16)
        acc_wout[...] += _dot_tn(yb_ref[...], dogb)
        dy = _dot_nt(dogb, wout_ref[...])
        dna = dy[:, 0:DA]
        dnb = dy[:, DA:]
        ogav = oga_ref[...]
        ogbv = ogb_ref[...]
        acca_ref[2:3, :] += _colsum(dna * yan)
        acca_ref[3:4, :] += _colsum(dnb * ybn)
        ta = dna * ogav
        dya = ra * (ta - yan * jnp.mean(ta * yan, axis=-1, keepdims=True))
        tb = dnb * ogbv
        dyb = rb * (tb - ybn * jnp.mean(tb * ybn, axis=-1, keepdims=True))
        dgu = dya * mixed
        dm = dya * gu
        first = _first_head_lanes()
        zero = jnp.zeros((CH, CH), BF16)
        dvn_chunks = []
        bs_acc = bs_s[...]
        for ci in range(nch):
            dmc = dm[ci * CH:(ci + 1) * CH, :]
            bs_acc = bs_acc + dmc
            dmcb = dmc.astype(BF16)
            dvn_chunks.append(_mix_heads(wcatt_ref, dmcb, first))
            vc = vnb[ci * CH:(ci + 1) * CH, :]
            for p in range(NH // 2):
                xt = dmcb[:, p * CH:(p + 1) * CH]
                vt = vc[:, p * CH:(p + 1) * CH]
                accws_ref[(2 * p) * CH:(2 * p + 1) * CH, :] += _dot_nt(jnp.where(first, xt, zero), vt)
                accws_ref[(2 * p + 1) * CH:(2 * p + 2) * CH, :] += _dot_nt(jnp.where(first, zero, xt), vt)
        bs_s[...] = bs_acc
        dvn = jnp.concatenate(dvn_chunks, axis=0) if nch > 1 else dvn_chunks[0]
        acca_ref[0:1, :] += _colsum(dvn * vhat)
        acca_ref[1:2, :] += _colsum(dvn)
        dvh = dvn * lngv
        dgv = rsl * (dvh - jnp.mean(dvh, axis=-1, keepdims=True)
                     - vhat * jnp.mean(dvh * vhat, axis=-1, keepdims=True))
        du = dgu * dgelu_u
        dv = dgv * dgelu_v
        dyg = dyb * (sgy * (1.0 + yg * (1.0 - sgy)))
        acca_ref[5:6, :] += _colsum(dyg * yhat)
        acca_ref[6:7, :] += _colsum(dyg)
        dyh = dyg * gngv
        dyc = rsg * (dyh - _grp_mean(dyh, pmv) - yhat * _grp_mean(dyh * yhat, pmv))
        acca_ref[4:5, :] += _colsum(dyc)
        dycbuf[0:tm, :] = dyc
        _shifted_copies(dycbuf, shbuf, tm)
        dgl = jnp.zeros((tm, DB), F32)
        for k in range(KW):
            win_k = _window(dycbuf, shbuf, KW - 1 - k, tm)
            dgl = dgl + cw_ref[k:k + 1, :] * win_k
            acccw_ref[k:k + 1, :] += _colsum(win_k * gl)
        dycbuf[tm:tm + HALO, :] = dyc[0:HALO, :]
        dval = dgl * sgt
        dgate = dgl * val * sgt * (1.0 - sgt)
        dz = jnp.concatenate([du, dv, dval, dgate], axis=1)
        accb_ref[...] += _colsum(dz)
        dzb = dz.astype(BF16)
        hbv = hb_ref[...]
        dh = jnp.zeros((tm, D), F32)
        for j in range(NDEV):
            dzj = dzb[:, j * WIN_B:(j + 1) * WIN_B]
            acc_win[j] += _dot_tn(hbv, dzj)
            dh = dh + _dot_nt(dzj, win_ref[j])
        accv_ref[1:2, :] += _colsum(dh)
        dh_xn = _colsum(dh * xn1)
        accv_ref[2:3, :] += dh_xn * g1v
        accv_ref[3:4, :] += dh_xn * (1.0 + scale1)
        dxn1 = dh * (1.0 + scale1) * g1v
        gx_ref[...] = dx1 + r1 * (dxn1 - xn1 * jnp.mean(dxn1 * xn1, axis=-1, keepdims=True))

        @pl.when(i == nt - 1)
        def _():
            rows = lax.broadcasted_iota(jnp.int32, (NH * CH, CH), 0) & (CH - 1)
            cols = lax.broadcasted_iota(jnp.int32, (NH * CH, CH), 1)
            accws_ref[...] = jnp.where(cols <= rows, accws_ref[...], 0.0)
            bs = bs_s[...]
            hi = bs.astype(BF16)
            r1_ = bs - hi.astype(F32)
            mid = r1_.astype(BF16)
            lo = (r1_ - mid.astype(F32)).astype(BF16)
            ev = esel_ref[...]
            accbs_ref[...] = _dot(hi, ev) + _dot(mid, ev) + _dot(lo, ev)
            for j in range(NDEV):
                st_win[...] = acc_win[j].astype(BF16)
                pltpu.sync_copy(st_win, gwin_ref.at[j])
            for j in range(D // WOB):
                st_wout[...] = acc_wout[j * WOB:(j + 1) * WOB, :].astype(BF16)
                pltpu.sync_copy(st_wout, gwout_ref.at[pl.ds(j * WOB, WOB)])

    rev = lambda w: pl.BlockSpec((tm, w), lambda i: (nt - 1 - i, 0))
    outs = pl.pallas_call(
        body,
        name="mix_bwd",
        grid=(nt,),
        in_specs=[rev(D), rev(D), rev(2 * DB), rev(DA), rev(D), rev(D), rev(D), rev(DA), rev(DA), rev(DA), rev(DA),
                  rev(CH), rev(DB), rev(DB), _full((8, D)), _full((1, D)),
                  _resident((NDEV, D, WIN_B)), _full((1, DA)), _full((1, DA)), _full((NH * CH, CH)),
                  _full((NH * CH, CH)), _full((HALO, DB)), _full((1, DB)), _full((1, DB)), _full((1, DA)),
                  _full((1, DB)), _resident((D, D)), _full((DB, DB)), _full((DA, CH)), HBM],
        out_specs=[rev(D), _full((8, D)), _full((1, DIN)), _full((8, DA)), _full((CH, CH)),
                   _full((NH * CH, CH)), _full((HALO, DB)), HBM, HBM],
        out_shape=[jax.ShapeDtypeStruct((T, D), F32), jax.ShapeDtypeStruct((8, D), F32),
                   jax.ShapeDtypeStruct((1, DIN), F32), jax.ShapeDtypeStruct((8, DA), F32),
                   jax.ShapeDtypeStruct((CH, CH), F32), jax.ShapeDtypeStruct((NH * CH, CH), F32),
                   jax.ShapeDtypeStruct((HALO, DB), F32),
                   jax.ShapeDtypeStruct((NDEV, D, WIN_B), BF16), jax.ShapeDtypeStruct((D, D), BF16)],
        scratch_shapes=[pltpu.VMEM((tm + HALO, DB), F32), pltpu.VMEM((7, tm + SH_ROWS, DB), F32),
                        pltpu.VMEM((CH, DA), F32), pltpu.VMEM((NDEV, D, WIN_B), F32), pltpu.VMEM((D, D), F32),
                        pltpu.VMEM((D, WIN_B), BF16), pltpu.VMEM((WOB, D), BF16)],
        compiler_params=pltpu.CompilerParams(dimension_semantics=("arbitrary",), vmem_limit_bytes=VMEM_LIMIT),
    )(dx1, x, zvg, mixed, o, hb, yb, gu, dgu, dgv, vhat, rslb, yhat, rsg, mod, g1, win, lng, lnb, wcat, wcat_t, cw,
      gng, gnb, oga, ogb, wout, pm, esel, after)
    return outs[:7], outs[7:]


def _wgrad_rows(a, b, bm, tk, name):
    T, M = a.shape
    N = b.shape[1]
    nk = T // tk

    def body(a_ref, b_ref, o_ref, acc):
        k = pl.program_id(1)

        @pl.when(k == 0)
        def _():
            acc[...] = jnp.zeros((bm, N), F32)

        acc[...] += _dot_tn(a_ref[...], b_ref[...])

        @pl.when(k == nk - 1)
        def _():
            o_ref[...] = acc[...].astype(BF16)

    return pl.pallas_call(
        body, name=name, grid=(M // bm, nk),
        in_specs=[pl.BlockSpec((tk, bm), lambda j, k: (k, j)), pl.BlockSpec((tk, N), lambda j, k: (k, 0))],
        out_specs=pl.BlockSpec((bm, N), lambda j, k: (j, 0)),
        out_shape=jax.ShapeDtypeStruct((M, N), BF16),
        scratch_shapes=[pltpu.VMEM((bm, N), F32)],
        compiler_params=pltpu.CompilerParams(dimension_semantics=("arbitrary", "arbitrary"),
                                             vmem_limit_bytes=VMEM_LIMIT),
    )(a, b)


def _small_copy(src, dst, ss, rs, k, to):
    return pltpu.make_async_remote_copy(src_ref=src, dst_ref=dst, send_sem=ss.at[k], recv_sem=rs.at[k],
                                        device_id=to, device_id_type=MESH)


def _gather(c_row, ada_w, ada_b8, ada_f_w, ada_f_b8, conv_s, shards):
    nw = len(shards)

    def body(c_ref, adaw_ref, adab_ref, adafw_ref, adafb_ref, conv_ref, *rest):
        w_in = rest[:nw]
        call_ref, cparts_ref, cfparts_ref, convg_ref = rest[nw:nw + 4]
        w_out = rest[nw + 4:2 * nw + 4]
        part_s, partf_s, wss, wrs, lsem, s1, r1, s2, r2, s3, r3, s4, r4 = rest[2 * nw + 4:]
        x, y, c, idx = _place()
        me = (x, y, c)
        ag = _AllGather(w_in, w_out, wss, wrs, lsem)
        ag.start()
        call_ref[pl.ds(idx, 1), :] = c_ref[...]
        convg_ref[idx] = conv_ref[...]
        ph1 = []
        for k in range(1, NDEV):
            to = _dev(idx ^ k)
            ph1.append(_small_copy(c_ref, call_ref.at[pl.ds(idx, 1)], s1, r1, k - 1, to))
            ph1.append(_small_copy(conv_ref, convg_ref.at[idx], s2, r2, k - 1, to))
        for cp in ph1:
            cp.start()
        for k in range(1, NDEV):
            src_dev = idx ^ k
            _small_copy(c_ref, call_ref.at[pl.ds(src_dev, 1)], s1, r1, k - 1, me).wait_recv()
            _small_copy(conv_ref, convg_ref.at[src_dev], s2, r2, k - 1, me).wait_recv()
        call = call_ref[...]
        cact = (call * _sig(call))
        part_s[...] = jnp.dot(cact, adaw_ref[...], preferred_element_type=F32,
                              precision=lax.Precision.HIGHEST) + adab_ref[pl.ds(idx, 1), :]
        partf_s[...] = jnp.dot(cact, adafw_ref[...], preferred_element_type=F32,
                               precision=lax.Precision.HIGHEST) + adafb_ref[pl.ds(idx, 1), :]
        cparts_ref[pl.ds(idx, 1), :] = part_s[pl.ds(idx, 1), :]
        cfparts_ref[pl.ds(idx, 1), :] = partf_s[pl.ds(idx, 1), :]
        ph2 = []
        for k in range(1, NDEV):
            t = idx ^ k
            ph2.append(_small_copy(part_s.at[pl.ds(t, 1)], cparts_ref.at[pl.ds(idx, 1)], s3, r3, k - 1, _dev(t)))
            ph2.append(_small_copy(partf_s.at[pl.ds(t, 1)], cfparts_ref.at[pl.ds(idx, 1)], s4, r4, k - 1, _dev(t)))
        for cp in ph2:
            cp.start()
        for k in range(1, NDEV):
            src_dev = idx ^ k
            _small_copy(part_s.at[pl.ds(0, 1)], cparts_ref.at[pl.ds(src_dev, 1)], s3, r3, k - 1, me).wait_recv()
            _small_copy(partf_s.at[pl.ds(0, 1)], cfparts_ref.at[pl.ds(src_dev, 1)], s4, r4, k - 1, me).wait_recv()
        for cp in ph1 + ph2:
            cp.wait_send()
        ag.forward()
        ag.finish()

    dma7 = pltpu.SemaphoreType.DMA((NDEV - 1,))
    outs = pl.pallas_call(
        body,
        name="gather_weights",
        in_specs=[VM] * 6 + [HBM] * nw,
        out_specs=[VM] * 4 + [HBM] * nw,
        out_shape=[jax.ShapeDtypeStruct((NDEV, D), F32), jax.ShapeDtypeStruct((NDEV, ada_w.shape[1]), F32),
                   jax.ShapeDtypeStruct((NDEV, ada_f_w.shape[1]), F32),
                   jax.ShapeDtypeStruct((NDEV,) + conv_s.shape, F32)]
                  + [jax.ShapeDtypeStruct((NDEV,) + s.shape, s.dtype) for s in shards],
        scratch_shapes=[pltpu.VMEM((NDEV, ada_w.shape[1]), F32), pltpu.VMEM((NDEV, ada_f_w.shape[1]), F32)]
                       + AG_SEMS(nw) + [dma7] * 8,
        compiler_params=pltpu.CompilerParams(vmem_limit_bytes=VMEM_LIMIT),
    )(c_row, ada_w, ada_b8, ada_f_w, ada_f_b8, conv_s, *shards)
    return outs[0], outs[1], outs[2], outs[3], outs[4:]


_VEC_AT = {
    "norm1_g": (8, 0, D), "a_ln_g": (11, 0, DA), "a_ln_b": (11, DA, DA), "a_spatial_b": (12, 0, D),
    "b_conv_b": (13, 0, DB), "b_gn_g": (13, DB, DB), "b_gn_b": (14, 0, DB), "out_norm_a_g": (14, DB, DA),
    "out_norm_b_g": (15, 0, DB), "norm2_g": (16, 0, D), "norm_f_g": (17, 0, D),
}
_LOSS_ROW = 18
_CW_ROW = 24


def _reduce_small(acc_f, acc_v, acc_b, acc_a, acc_bs, acc_cw, dws, after, pair_grads):
    npg = len(pair_grads)

    def body(accf_ref, accv_ref, accb_ref, acca_ref, accbs_ref, acccw_ref, dws_ref, after_ref, *rest):
        pg = rest[:npg]
        vsum_ref, dcond_ref, wssum_ref = rest[npg:npg + 3]
        pq = rest[npg + 3:2 * npg + 3]
        vloc, vbuf, wbuf, wown, s1, r1, s2, r2, s3, r3 = rest[2 * npg + 3:2 * npg + 13]
        pland = rest[2 * npg + 13:3 * npg + 13]
        pstage = rest[3 * npg + 13:4 * npg + 13]
        ps, pr, pls, pss = rest[4 * npg + 13:]
        x, y, c, idx = _place()
        me = (x, y, c)
        sibling = (x, y, 1 - c)
        chips = [(1 - x, y), (x, 1 - y), (1 - x, 1 - y)]
        blk = lambda p: 4 * p[0] + 2 * p[1] + p[2]
        give = [blk((*ch, 1 - c)) for ch in chips] + [blk(sibling)]
        pair = [pltpu.make_async_remote_copy(src_ref=pg[a].at[b], dst_ref=pland[a].at[j], send_sem=ps.at[a, j],
                                             recv_sem=pr.at[a, j], device_id=sibling, device_id_type=MESH)
                for a in range(npg) for j, b in enumerate(give)]
        loads = [pltpu.make_async_copy(pg[a].at[blk((*ch, c))], pstage[a].at[j], pls.at[a, j])
                 for a in range(npg) for j, ch in enumerate(chips)]
        for cp in pair + loads:
            cp.start()
        vloc[...] = jnp.zeros((NVEC, D), F32)
        vloc[0:1, :] = accv_ref[1:2, :]
        vloc[1:2, :] = accv_ref[2:3, :]
        vloc[2:3, :] = accv_ref[0:1, :]
        vloc[3:4, :] = accf_ref[4:5, :]
        vloc[4:5, :] = accf_ref[5:6, :]
        vloc[5:6, :] = accf_ref[3:4, :]
        vloc[6:7, :] = accf_ref[0:1, :]
        vloc[7:8, :] = accf_ref[1:2, :]
        vloc[8:9, :] = accv_ref[3:4, :]
        vloc[9:10, :] = accb_ref[:, 0:D]
        vloc[10:11, :] = accb_ref[:, D:]
        vloc[11:12, 0:DA] = acca_ref[0:1, :]
        vloc[11:12, DA:] = acca_ref[1:2, :]
        bst = accbs_ref[...].T
        for h in range(NH):
            vloc[12:13, h * CH:(h + 1) * CH] = bst[h:h + 1, :]
        vloc[13:14, 0:DB] = acca_ref[4:5, :]
        vloc[13:14, DB:] = acca_ref[5:6, :]
        vloc[14:15, 0:DB] = acca_ref[6:7, :]
        vloc[14:15, DB:] = acca_ref[2:3, :]
        vloc[15:16, 0:DB] = acca_ref[3:4, :]
        vloc[16:17, :] = accf_ref[6:7, :]
        vloc[17:18, :] = accf_ref[2:3, :]
        vloc[_LOSS_ROW:_LOSS_ROW + 1, :] = accf_ref[7:8, :]
        vloc[_CW_ROW:_CW_ROW + HALO // 2, 0:DB] = acccw_ref[0:HALO // 2, :]
        vloc[_CW_ROW:_CW_ROW + HALO // 2, DB:] = acccw_ref[HALO // 2:, :]
        vbuf[idx] = vloc[...]
        rows_of = lambda t: pl.ds(pl.multiple_of(t * CH, CH), CH)
        wbuf[0] = dws_ref[rows_of(idx), :]
        sm = []
        for k in range(1, NDEV):
            t = idx ^ k
            sm.append(_small_copy(vloc, vbuf.at[idx], s1, r1, k - 1, _dev(t)))
            sm.append(_small_copy(dws_ref.at[rows_of(t)], wbuf.at[k], s2, r2, k - 1, _dev(t)))
        for cp in sm:
            cp.start()
        for k in range(1, NDEV):
            _small_copy(dws_ref.at[rows_of(0)], wbuf.at[k], s2, r2, k - 1, me).wait_recv()
        ws = wbuf[0]
        for k in range(1, NDEV):
            ws = ws + wbuf[k]
        wown[...] = ws
        wssum_ref[rows_of(idx), :] = ws
        ag = [_small_copy(wown, wssum_ref.at[rows_of(idx)], s3, r3, k - 1, _dev(idx ^ k)) for k in range(1, NDEV)]
        for cp in ag:
            cp.start()
        for cp in loads:
            cp.wait()
        for cp in pair:
            cp.wait_recv()
        stores = []
        for a in range(npg):
            for j in range(3):
                pstage[a][j] = (pstage[a][j].astype(F32) + pland[a][j].astype(F32)).astype(BF16)
                stores.append(pltpu.make_async_copy(pstage[a].at[j], pq[a].at[j], pss.at[a, j]))
            stores.append(pltpu.make_async_copy(pland[a].at[3], pq[a].at[3], pss.at[a, 3]))
        for cp in stores:
            cp.start()
        for k in range(1, NDEV):
            _small_copy(vloc, vbuf.at[idx ^ k], s1, r1, k - 1, me).wait_recv()
        vs = vbuf[0]
        for d in range(1, NDEV):
            vs = vs + vbuf[d]
        vsum_ref[...] = vs
        for d in range(NDEV):
            dcond_ref[d] = vbuf[d, 0:8, :]
        for k in range(1, NDEV):
            _small_copy(wown, wssum_ref.at[rows_of(idx ^ k)], s3, r3, k - 1, me).wait_recv()
        for cp in sm + ag:
            cp.wait_send()
        for cp in stores:
            cp.wait()
        for cp in pair:
            cp.wait_send()

    dma7 = pltpu.SemaphoreType.DMA((NDEV - 1,))
    dma4 = pltpu.SemaphoreType.DMA((npg, 4))
    outs = pl.pallas_call(
        body,
        name="reduce_small",
        in_specs=[VM] * 7 + [HBM] + [HBM] * npg,
        out_specs=[VM, VM, VM] + [HBM] * npg,
        out_shape=[jax.ShapeDtypeStruct((NVEC, D), F32), jax.ShapeDtypeStruct((NDEV, 8, D), F32),
                   jax.ShapeDtypeStruct(dws.shape, F32)]
                  + [jax.ShapeDtypeStruct((4,) + g.shape[1:], g.dtype) for g in pair_grads],
        scratch_shapes=[pltpu.VMEM((NVEC, D), F32), pltpu.VMEM((NDEV, NVEC, D), F32),
                        pltpu.VMEM((NDEV, CH, CH), F32), pltpu.VMEM((CH, CH), F32)] + [dma7] * 6
                       + [pltpu.VMEM((4,) + g.shape[1:], g.dtype) for g in pair_grads]
                       + [pltpu.VMEM((3,) + g.shape[1:], g.dtype) for g in pair_grads] + [dma4] * 4,
        compiler_params=pltpu.CompilerParams(vmem_limit_bytes=VMEM_LIMIT),
    )(acc_f, acc_v, acc_b, acc_a, acc_bs, acc_cw, dws, after, *pair_grads)
    return outs[0], outs[1], outs[2], outs[3:]


HBM_ONLY = pl.BlockSpec(memory_space=pltpu.HBM)
SEM = pl.BlockSpec(memory_space=pltpu.SEMAPHORE)
EFFECT = pltpu.SideEffectType.DATAFLOW_SIDE_EFFECTING


def _rs_copies(g_refs, land_refs, sems, chips):
    x, y, c, idx = _place()
    if chips:
        routes = [(j, j, (*ch, c)) for j, ch in enumerate([(1 - x, y), (x, 1 - y), (1 - x, 1 - y)])]
    else:
        routes = [(idx ^ k, k - 1, _dev(idx ^ k)) for k in range(1, NDEV)]
    cps = []
    for src, dst, to in routes:
        for a in range(len(g_refs)):
            n = len(cps)
            cps.append(pltpu.make_async_remote_copy(
                src_ref=g_refs[a].at[src], dst_ref=land_refs[a].at[dst], send_sem=sems[2 * n],
                recv_sem=sems[2 * n + 1], device_id=to, device_id_type=MESH))
    return cps


def _rs_start(grads, name, after=(), chips=False):
    nw = len(grads)
    npeer = 3 if chips else NDEV - 1
    nsem = 2 * nw * npeer
    lands = [lax.empty((npeer,) + g.shape[1:], g.dtype) for g in grads]

    def body(*refs):
        g_refs, land_refs = refs[:nw], refs[nw:2 * nw]
        sems = refs[2 * nw + len(after):2 * nw + len(after) + nsem]
        token = refs[-1]
        for cp in _rs_copies(g_refs, land_refs, sems, chips):
            cp.start()
        token[...] = jnp.zeros_like(token)

    outs = pl.pallas_call(
        body, name=name,
        out_shape=(*[pltpu.SemaphoreType.DMA(())] * nsem,
                   *[pltpu.HBM(g.shape, g.dtype) for g in grads], *[pltpu.HBM(l.shape, l.dtype) for l in lands],
                   jax.ShapeDtypeStruct((8, CH), F32)),
        in_specs=[HBM_ONLY] * (2 * nw) + [HBM] * len(after),
        out_specs=(*[SEM] * nsem, *[HBM_ONLY] * (2 * nw), VM),
        input_output_aliases={i: nsem + i for i in range(2 * nw)},
        compiler_params=pltpu.CompilerParams(has_side_effects=EFFECT),
    )(*[pltpu.with_memory_space_constraint(g, pltpu.HBM) for g in grads],
      *[pltpu.with_memory_space_constraint(l, pltpu.HBM) for l in lands], *after)
    return outs[:nsem], outs[nsem:nsem + nw], outs[nsem + nw:nsem + 2 * nw], outs[-1]


def _rs_wait(sems, g_thru, land_thru, after, name, chips=False):
    nw = len(g_thru)
    nsem = len(sems)

    def body(*refs):
        g_refs, land_refs = refs[:nw], refs[nw:2 * nw]
        for cp in _rs_copies(g_refs, land_refs, refs[2 * nw:2 * nw + nsem], chips):
            cp.wait_send()
            cp.wait_recv()

    outs = pl.pallas_call(
        body, name=name,
        out_shape=tuple(pltpu.HBM(a.shape, a.dtype) for a in list(g_thru) + list(land_thru)),
        in_specs=[HBM_ONLY] * (2 * nw) + [SEM] * nsem + [HBM] * len(after),
        out_specs=tuple([HBM_ONLY] * (2 * nw)),
        input_output_aliases={i: i for i in range(2 * nw)},
        compiler_params=pltpu.CompilerParams(has_side_effects=EFFECT),
    )(*g_thru, *land_thru, *sems, *after)
    return outs[:nw], outs[nw:]


def _adamw(w, g, m, v):
    m2 = ADAM_B1 * m + (1.0 - ADAM_B1) * g
    v2 = ADAM_B2 * v + (1.0 - ADAM_B2) * (g * g)
    m_hat = m2 / (1.0 - ADAM_B1 ** ADAM_STEP)
    v_hat = v2 / (1.0 - ADAM_B2 ** ADAM_STEP)
    delta = -ADAM_LR * (m_hat / (jnp.sqrt(v_hat) + ADAM_EPS) + ADAM_WD * w)
    return delta, m2, v2


def _adam_big(r, w, m, v, rb, name, own, after=None, sib=None):
    R, C = w.shape
    ns = r.shape[0]
    g_all, idx1 = own

    def body(idx_ref, r_ref, own_ref, *refs):
        w_ref, m_ref, v_ref, g_ref, d_ref, m2_ref, v2_ref = refs[len(refs) - 7:]
        g = own_ref[0].astype(F32)
        if sib is not None:
            g = g + refs[0][0].astype(F32)
        for k in range(ns):
            g = g + r_ref[k].astype(F32)
        g_ref[...] = g
        d_ref[...], m2_ref[...], v2_ref[...] = _adamw(w_ref[...], g, m_ref[...], v_ref[...])

    t2 = pl.BlockSpec((rb, C), lambda i, idx_ref: (i, 0))
    sd = jax.ShapeDtypeStruct((R, C), F32)
    extra_specs = ([pl.BlockSpec((1, rb, C), lambda i, idx_ref: (3, i, 0))] if sib is not None else []) \
        + ([HBM] if after is not None else [])
    extra = ([sib] if sib is not None else []) + ([after] if after is not None else [])
    return pl.pallas_call(
        body, name=name,
        grid_spec=pltpu.PrefetchScalarGridSpec(
            num_scalar_prefetch=1, grid=(R // rb,),
            in_specs=[pl.BlockSpec((ns, rb, C), lambda i, idx_ref: (0, i, 0)),
                      pl.BlockSpec((1, rb, C), lambda i, idx_ref: (idx_ref[0], i, 0))] + extra_specs + [t2, t2, t2],
            out_specs=[t2, t2, t2, t2]),
        out_shape=[sd, sd, sd, sd],
        compiler_params=pltpu.CompilerParams(dimension_semantics=("arbitrary",), vmem_limit_bytes=VMEM_LIMIT),
    )(idx1, r, g_all, *extra, w, m, v)


def _adam_ada(cact_t, dcs, w, m, v, rb, name):
    R, C = w.shape

    def body(ct_ref, dc_ref, w_ref, m_ref, v_ref, g_ref, d_ref, m2_ref, v2_ref):
        g = jnp.dot(ct_ref[...], dc_ref[...], preferred_element_type=F32, precision=lax.Precision.HIGHEST)
        g_ref[...] = g
        d_ref[...], m2_ref[...], v2_ref[...] = _adamw(w_ref[...], g, m_ref[...], v_ref[...])

    t2 = pl.BlockSpec((rb, C), lambda i: (i, 0))
    sd = jax.ShapeDtypeStruct((R, C), F32)
    return pl.pallas_call(
        body, name=name, grid=(R // rb,),
        in_specs=[pl.BlockSpec((rb, NDEV), lambda i: (i, 0)), _full((NDEV, C)), t2, t2, t2],
        out_specs=[t2, t2, t2, t2], out_shape=[sd, sd, sd, sd],
        compiler_params=pltpu.CompilerParams(dimension_semantics=("arbitrary",), vmem_limit_bytes=VMEM_LIMIT),
    )(cact_t, dcs, w, m, v)


_SMALL = ["ada_b", "ada_f_b", "norm1_g", "b_in", "a_ln_g", "a_ln_b", "a_spatial_b", "b_conv_b", "b_gn_g", "b_gn_b",
          "out_norm_a_g", "out_norm_b_g", "norm2_g", "norm_f_g", "a_spatial_w", "b_conv_w"]


def _adam_small(vsum, wssum, gcw, params):
    names = _SMALL
    flat = []
    for n in names:
        flat += list(params[n])

    def body(vs_ref, ws_ref, gcw_ref, *rest):
        ins = rest[:3 * len(names)]
        outs = rest[3 * len(names):]
        for pi, n in enumerate(names):
            w_ref, m_ref, v_ref = ins[3 * pi:3 * pi + 3]
            g_ref, d_ref, m2_ref, v2_ref = outs[4 * pi:4 * pi + 4]
            if n in ("ada_b", "ada_f_b", "b_in"):
                row0 = {"ada_b": 0, "ada_f_b": 6, "b_in": 9}[n]
                pieces = [(vs_ref[row0 + r:row0 + r + 1, :], slice(r * D, (r + 1) * D))
                          for r in range(w_ref.shape[1] // D)]
            elif n == "a_spatial_w":
                pieces = [(ws_ref[...], slice(None))]
            elif n == "b_conv_w":
                pieces = [(gcw_ref[...], slice(None))]
            else:
                row, off, width = _VEC_AT[n]
                pieces = [(vs_ref[row:row + 1, off:off + width], slice(None))]
            for g, cs in pieces:
                g_ref[:, cs] = g
                d_ref[:, cs], m2_ref[:, cs], v2_ref[:, cs] = _adamw(w_ref[:, cs], g, m_ref[:, cs], v_ref[:, cs])

    out_shape = []
    for n in names:
        out_shape += [jax.ShapeDtypeStruct(params[n][0].shape, F32)] * 4
    outs = pl.pallas_call(
        body, name="adam_small",
        in_specs=[VM] * (3 + len(flat)), out_specs=[VM] * len(out_shape), out_shape=out_shape,
        compiler_params=pltpu.CompilerParams(vmem_limit_bytes=VMEM_LIMIT),
    )(vsum, wssum, gcw, *flat)
    return {n: outs[4 * pi:4 * pi + 4] for pi, n in enumerate(names)}


def _token_tile(T, want):
    return want if T % want == 0 else T


def kernel(x, c, ada_w, ada_b, norm1_g, w_in, b_in, a_ln_g, a_ln_b, a_spatial_w, a_spatial_b, b_conv_w, b_conv_b, b_gn_g, b_gn_b, out_norm_a_g, out_norm_b_g, w_out, norm2_g, w_ffn_in, w_ffn_out, ada_f_w, ada_f_b, norm_f_g, loss_target, m_ada_w, m_ada_b, m_norm1_g, m_w_in, m_b_in, m_a_ln_g, m_a_ln_b, m_a_spatial_w, m_a_spatial_b, m_b_conv_w, m_b_conv_b, m_b_gn_g, m_b_gn_b, m_out_norm_a_g, m_out_norm_b_g, m_w_out, m_norm2_g, m_w_ffn_in, m_w_ffn_out, m_ada_f_w, m_ada_f_b, m_norm_f_g, v_ada_w, v_ada_b, v_norm1_g, v_w_in, v_b_in, v_a_ln_g, v_a_ln_b, v_a_spatial_w, v_a_spatial_b, v_b_conv_w, v_b_conv_b, v_b_gn_g, v_b_gn_b, v_out_norm_a_g, v_out_norm_b_g, v_w_out, v_norm2_g, v_w_ffn_in, v_w_ffn_out, v_ada_f_w, v_ada_f_b, v_norm_f_g):
    T = x.shape[1]
    idx = 4 * lax.axis_index("x") + 2 * lax.axis_index("y") + lax.axis_index("c")
    x2d = x.reshape(T, D)
    tgt = loss_target.reshape(T, D)

    conv_s = jnp.pad(b_conv_w[0], ((0, HALO - KW), (0, 0)))
    call, cparts, cfparts, convg, (win_g, wout_g) = _gather(
        c, ada_w[0], ada_b.reshape(NDEV, -1), ada_f_w, ada_f_b.reshape(NDEV, -1), conv_s,
        [w_in[0].astype(BF16), w_out[0].astype(BF16)])
    wout = wout_g.reshape(D, D)
    mod = jnp.concatenate([cparts.reshape(6, D), cfparts.reshape(2, D)], axis=0)
    cw = jnp.transpose(convg, (1, 0, 2)).reshape(HALO, DB)

    tril = jnp.tril(jnp.ones((CH, CH), dtype=bool))
    wsm = jnp.where(tril[None], a_spatial_w[0], 0.0).astype(BF16)
    wcat = wsm.reshape(NH * CH, CH)
    wcat_t = jnp.transpose(wsm, (0, 2, 1)).reshape(NH * CH, CH)
    bsf = jnp.repeat(a_spatial_b[0].T, DA // NH, axis=1)
    lane = jnp.arange(DB)
    pm = jnp.where((lane[:, None] >> 6) == (lane[None, :] >> 6), 1.0 / 64.0, 0.0).astype(BF16)
    esel = jnp.where((lane[:, None] >> 6) == jnp.arange(CH)[None, :], 1.0, 0.0).astype(BF16)

    tm = _token_tile(T, 256)
    tk = _token_tile(T, 2048)
    (x1, hb, zvg, mixed, yb, o, gu, dgelu_u, dgelu_v, vhat, rslb, yhat, rsg), (wfi_g, wfo_g) = _mix_fwd(
        x2d, mod, norm1_g, win_g, b_in, a_ln_g, a_ln_b, wcat, bsf, cw, b_conv_b, b_gn_g, b_gn_b, out_norm_a_g,
        out_norm_b_g, wout, pm, [w_ffn_in[0].T.astype(BF16), w_ffn_out[0].astype(BF16)], _token_tile(T, 512))
    dx1, h2b, dgu, act, dxg, acc_f = _ffn(x1, tgt, mod, norm2_g, norm_f_g.reshape(1, D),
                                          wfi_g.reshape(2 * DFF, D), wfo_g.reshape(DFF, D), tm)
    g_wfi = _wgrad_rows(dgu, h2b, 2 * WFI_B, tk, "wgrad_ffn_in").reshape(NDEV, WFI_B, D)
    g_wfo = _wgrad_rows(act, dxg, 2 * WFI_B, tk, "wgrad_ffn_out").reshape(NDEV, DFF // NDEV, D)
    f_sems, f_thru, f_land, f_token = _rs_start([g_wfi, g_wfo], "rs_ffn_start")
    (gx, acc_v, acc_b, acc_a, acc_bs, acc_ws, acc_cw), (g_win, g_wout) = _mix_bwd(
        dx1, x2d, zvg, mixed, o, hb, yb, gu, dgelu_u, dgelu_v, vhat, rslb, yhat, rsg, mod, norm1_g, win_g, a_ln_g,
        a_ln_b, wcat, wcat_t, cw, b_gn_g, b_gn_b, out_norm_a_g, out_norm_b_g, wout, pm, esel, f_token, tm)
    (g_wfi_d, g_wfo_d), (r_wfi, r_wfo) = _rs_wait(f_sems, f_thru, f_land, [acc_v], "rs_ffn_wait")
    g_wout = g_wout.reshape(NDEV, D // NDEV, D)

    vsum, dcond_all, wssum, (q_win, q_wout) = _reduce_small(acc_f, acc_v, acc_b, acc_a, acc_bs, acc_cw, acc_ws,
                                                            g_wfi_d, [g_win, g_wout])
    sems, g_thru, land_thru, token = _rs_start([q_win, q_wout], "rs_mix_start", after=(vsum,), chips=True)

    own = lambda g: (g, jnp.reshape(idx, (1,)).astype(jnp.int32))
    res = {}
    res["w_ffn_in"] = tuple(a.T for a in _adam_big(r_wfi, w_ffn_in[0].T, m_w_ffn_in[0].T, v_w_ffn_in[0].T, WFI_B // 2,
                                                   "adam_w_ffn_in", own=own(g_wfi_d), after=token))
    res["w_ffn_out"] = _adam_big(r_wfo, w_ffn_out[0], m_w_ffn_out[0], v_w_ffn_out[0], DFF // NDEV // 2,
                                 "adam_w_ffn_out", own=own(g_wfo_d), after=token)
    cact_t = (call * jax.nn.sigmoid(call)).T
    dcond = dcond_all.reshape(NDEV, 8 * D)
    nada = ada_w.shape[2]
    nadf = ada_f_w.shape[1]
    dcs = lax.dynamic_slice(dcond, (0, idx * nada), (NDEV, nada))
    dcfs = lax.dynamic_slice(dcond, (0, 6 * D + idx * nadf), (NDEV, nadf))
    res["ada_w"] = _adam_ada(cact_t, dcs, ada_w[0], m_ada_w[0], v_ada_w[0], 512, "adam_ada_w")
    res["ada_f_w"] = _adam_ada(cact_t, dcfs, ada_f_w, m_ada_f_w, v_ada_f_w, 512, "adam_ada_f_w")
    ncw = b_conv_w.shape[2]
    gcw = jnp.concatenate([lax.dynamic_slice(vsum, (_CW_ROW, idx * ncw), (HALO // 2, ncw)),
                           lax.dynamic_slice(vsum, (_CW_ROW, DB + idx * ncw), (HALO // 2, ncw))], axis=0)[:KW]
    two = lambda a: a.reshape(1, -1) if a.ndim == 1 else a.reshape(-1, a.shape[-1])
    small_in = {
        "ada_b": (ada_b, m_ada_b, v_ada_b), "ada_f_b": (ada_f_b, m_ada_f_b, v_ada_f_b),
        "norm1_g": (norm1_g, m_norm1_g, v_norm1_g), "b_in": (b_in, m_b_in, v_b_in),
        "a_ln_g": (a_ln_g, m_a_ln_g, v_a_ln_g), "a_ln_b": (a_ln_b, m_a_ln_b, v_a_ln_b),
        "a_spatial_b": (a_spatial_b.reshape(1, D), m_a_spatial_b.reshape(1, D), v_a_spatial_b.reshape(1, D)),
        "b_conv_b": (b_conv_b, m_b_conv_b, v_b_conv_b), "b_gn_g": (b_gn_g, m_b_gn_g, v_b_gn_g),
        "b_gn_b": (b_gn_b, m_b_gn_b, v_b_gn_b), "out_norm_a_g": (out_norm_a_g, m_out_norm_a_g, v_out_norm_a_g),
        "out_norm_b_g": (out_norm_b_g, m_out_norm_b_g, v_out_norm_b_g),
        "norm2_g": (norm2_g, m_norm2_g, v_norm2_g), "norm_f_g": (norm_f_g, m_norm_f_g, v_norm_f_g),
        "a_spatial_w": (a_spatial_w, m_a_spatial_w, v_a_spatial_w),
        "b_conv_w": (b_conv_w[0], m_b_conv_w[0], v_b_conv_w[0]),
    }
    small_in = {n: tuple(two(a) for a in t) for n, t in small_in.items()}
    res.update(_adam_small(vsum, wssum, gcw, small_in))
    (q_win_d, q_wout_d), (r_win, r_wout) = _rs_wait(
        sems, g_thru, land_thru,
        [res["w_ffn_in"][0], res["w_ffn_out"][0], res["ada_w"][0], res["ada_f_w"][0], res["norm_f_g"][0]],
        "rs_mix_wait", chips=True)
    res["w_in"] = _adam_big(r_win, w_in[0], m_w_in[0], v_w_in[0], 512, "adam_w_in", own=own(g_win), sib=q_win_d)
    res["w_out"] = _adam_big(r_wout, w_out[0], m_w_out[0], v_w_out[0], D // NDEV, "adam_w_out", own=own(g_wout),
                             sib=q_wout_d)

    loss = 0.5 / D * jnp.sum(vsum[_LOSS_ROW])
    shapes = {"ada_w": ada_w, "ada_b": ada_b, "norm1_g": norm1_g, "w_in": w_in, "b_in": b_in, "a_ln_g": a_ln_g,
              "a_ln_b": a_ln_b, "a_spatial_w": a_spatial_w, "a_spatial_b": a_spatial_b, "b_conv_w": b_conv_w,
              "b_conv_b": b_conv_b, "b_gn_g": b_gn_g, "b_gn_b": b_gn_b, "out_norm_a_g": out_norm_a_g,
              "out_norm_b_g": out_norm_b_g, "w_out": w_out, "norm2_g": norm2_g, "w_ffn_in": w_ffn_in,
              "w_ffn_out": w_ffn_out, "ada_f_w": ada_f_w, "ada_f_b": ada_f_b, "norm_f_g": norm_f_g}
    order = list(shapes)
    outs = [loss, gx.reshape(x.shape)]
    for which in range(4):
        outs += [res[n][which].reshape(shapes[n].shape) for n in order]
    return tuple(outs)
```

```python
import math

import numpy as np

import jax
import jax.numpy as jnp
from jax import lax
from jax.experimental import pallas as pl
from jax.experimental.pallas import tpu as pltpu

F32 = jnp.float32
BF16 = jnp.bfloat16

D = 1024
DA = 512
DB = 512
DIN = 2048
DFF = 2816
NH = 8
CH = 128
KW = 31
HALO = 32
NDEV = 8
WIN_B = DIN // NDEV
WFI_B = 2 * DFF // NDEV
EPS = 1e-6
NVEC = 40
VMEM_LIMIT = 56 * 1024 * 1024

ADAM_LR, ADAM_B1, ADAM_B2, ADAM_EPS, ADAM_WD, ADAM_STEP = 0.001, 0.9, 0.999, 1e-08, 0.01, 10

MESH = pl.DeviceIdType.MESH

_LANE = np.arange(DB)
_GROUP_MEAN = np.where((_LANE[:, None] >> 6) == (_LANE[None, :] >> 6), 1.0 / 64.0, 0.0).astype(np.float32)
_HEAD_SELECT = np.where((_LANE[:, None] >> 6) == np.arange(CH)[None, :], 1.0, 0.0).astype(np.float32)


def _dot(a, b):
    return jnp.dot(a, b, preferred_element_type=F32)


def _dot_nt(a, b):
    return lax.dot_general(a, b, (((1,), (1,)), ((), ())), preferred_element_type=F32)


def _dot_tn(a, b):
    return lax.dot_general(a, b, (((0,), (0,)), ((), ())), preferred_element_type=F32)


def _rs(v):
    return lax.rsqrt(jnp.mean(v * v, axis=-1, keepdims=True) + EPS)


def _sig(v):
    return 1.0 / (1.0 + jnp.exp(-v))


_INV_SQRT2 = 1.0 / math.sqrt(2.0)
_INV_SQRT2PI = 1.0 / math.sqrt(2.0 * math.pi)


def _gelu_parts(v):
    cdf = 0.5 * (1.0 + lax.erf(v * _INV_SQRT2))
    pdf = jnp.exp(-0.5 * v * v) * _INV_SQRT2PI
    return v * cdf, cdf + v * pdf


def _grp_mean(v, pm):
    hi = v.astype(BF16)
    lo = (v - hi.astype(F32)).astype(BF16)
    return _dot(hi, pm) + _dot(lo, pm)


def _colsum(v):
    return jnp.sum(v, axis=0, keepdims=True)


def _full(shape):
    nd = len(shape)
    return pl.BlockSpec(shape, lambda *_: (0,) * nd)


def _resident(shape):
    nd = len(shape)
    return pl.BlockSpec(shape, lambda *_: (0,) * nd, pipeline_mode=pl.Buffered(1))


HBM = pl.BlockSpec(memory_space=pl.ANY)
VM = pl.BlockSpec(memory_space=pltpu.VMEM)


SH_ROWS = HALO - 8


def _shifted_copies(buf, shbuf, tm):
    for b in range(1, 8):
        shbuf[b - 1] = buf[b:b + tm + SH_ROWS, :]


def _window(buf, shbuf, off, tm):
    a, b = divmod(off, 8)
    if b == 0:
        return buf[8 * a:8 * a + tm, :]
    return shbuf[b - 1, 8 * a:8 * a + tm, :]


def _first_head_lanes():
    return lax.broadcasted_iota(jnp.int32, (CH, CH), 1) < (DA // NH)


def _mix_heads(w_ref, vb, first):
    outs = []
    for p in range(NH // 2):
        v = vb[:, p * CH:(p + 1) * CH]
        a = _dot(w_ref[(2 * p) * CH:(2 * p + 1) * CH, :], v)
        b = _dot(w_ref[(2 * p + 1) * CH:(2 * p + 2) * CH, :], v)
        outs.append(jnp.where(first, a, b))
    return jnp.concatenate(outs, axis=1)


def _place():
    x, y, c = lax.axis_index("x"), lax.axis_index("y"), lax.axis_index("c")
    return x, y, c, 4 * x + 2 * y + c


def _dev(t):
    return (t >> 2, (t >> 1) & 1, t & 1)


class _AllGather:
    def __init__(self, w_in, w_out, wss, wrs, lsem):
        x, y, c, idx = _place()
        me, sibling = (x, y, c), (x, y, 1 - c)
        chips = [(1 - x, y), (x, 1 - y), (1 - x, 1 - y)]
        nw = len(w_in)

        def blk(p):
            return 4 * p[0] + 2 * p[1] + p[2]

        def wcopy(a, k, block, to, src=None):
            dst = w_out[a].at[blk(block)]
            return pltpu.make_async_remote_copy(src_ref=dst if src is None else src, dst_ref=dst,
                                                send_sem=wss.at[a, k], recv_sem=wrs.at[a, k],
                                                device_id=to, device_id_type=MESH)

        self.mine = [pltpu.make_async_copy(w_in[a], w_out[a].at[idx], lsem.at[a]) for a in range(nw)]
        self.first = []
        for a in range(nw):
            self.first.append(wcopy(a, 0, me, sibling, src=w_in[a]))
            self.first += [wcopy(a, 1 + j, me, (*chip, c), src=w_in[a]) for j, chip in enumerate(chips)]
        self.landed = [[wcopy(a, 1 + j, (*chip, c), me) for a in range(nw)] for j, chip in enumerate(chips)]
        self.passed = [[wcopy(a, 4 + j, (*chip, c), sibling) for a in range(nw)] for j, chip in enumerate(chips)]
        self.from_sibling = []
        for a in range(nw):
            self.from_sibling.append(wcopy(a, 0, sibling, me))
            self.from_sibling += [wcopy(a, 4 + j, (*chip, 1 - c), me) for j, chip in enumerate(chips)]

    def start(self):
        for cp in self.mine + self.first:
            cp.start()

    def forward(self):
        for land, pas in zip(self.landed, self.passed):
            for l, p in zip(land, pas):
                l.wait_recv()
                p.start()

    def finish(self):
        for cp in self.from_sibling:
            cp.wait_recv()
        for cp in self.first:
            cp.wait_send()
        for pas in self.passed:
            for p in pas:
                p.wait_send()
        for cp in self.mine:
            cp.wait()


AG_SEMS = lambda nw: [pltpu.SemaphoreType.DMA((nw, 7)), pltpu.SemaphoreType.DMA((nw, 7)),
                      pltpu.SemaphoreType.DMA((nw,))]


def _mix_fwd(x, mod, g1, win, b_in, lng, lnb, wcat, bsf, cw, cb, gng, gnb, oga, ogb, wout, pm, ffn_shards, tm):
    T = x.shape[0]
    nt = T // tm
    nch = tm // CH
    nw = len(ffn_shards)
    fwd_step = (5 * nt) // 8
    saved = [(D, F32), (D, BF16), (2 * DB, F32), (DA, F32), (D, BF16), (D, F32), (DA, F32), (DA, F32), (DA, F32),
             (DA, F32), (CH, F32), (DB, F32), (DB, F32)]
    NSAVE = len(saved)

    def body(x_ref, mod_ref, g1_ref, win_ref, bin_ref, lng_ref, lnb_ref, wcat_ref, bsf_ref, cw_ref, cb_ref,
             gng_ref, gnb_ref, oga_ref, ogb_ref, wout_ref, pm_ref, *rest):
        sh_in = rest[:nw]
        (x1_ref, h_ref, zvg_ref, mixed_ref, y_ref, o_ref, gu_ref, dgu_ref, dgv_ref, vhat_ref, rsl_ref, yhat_ref,
         rsg_ref) = rest[nw:nw + NSAVE]
        sh_out = rest[nw + NSAVE:2 * nw + NSAVE]
        glbuf, shbuf, wss, wrs, lsem = rest[2 * nw + NSAVE:]
        i = pl.program_id(0)

        @pl.when(i == 0)
        def _():
            _AllGather(sh_in, sh_out, wss, wrs, lsem).start()

        xv = x_ref[...]
        shift1 = mod_ref[0:1, :]
        scale1 = mod_ref[1:2, :]
        gate1 = mod_ref[2:3, :]
        h = (xv * _rs(xv) * g1_ref[...]) * (1.0 + scale1) + shift1
        hb = h.astype(BF16)
        h_ref[...] = hb
        z = jnp.concatenate([_dot(hb, win_ref[j]) for j in range(NDEV)], axis=1) + bin_ref[...]
        zvg_ref[...] = z[:, 2 * DA:]
        gu, dgelu_u = _gelu_parts(z[:, 0:DA])
        gv, dgelu_v = _gelu_parts(z[:, DA:2 * DA])
        gu_ref[...] = gu
        dgu_ref[...] = dgelu_u
        dgv_ref[...] = dgelu_v
        xc = gv - jnp.mean(gv, axis=-1, keepdims=True)
        rsl = lax.rsqrt(jnp.mean(xc * xc, axis=-1, keepdims=True) + EPS)
        vhat = xc * rsl
        vhat_ref[...] = vhat
        rsl_ref[...] = jnp.broadcast_to(rsl, (tm, CH))
        vnb = (vhat * lng_ref[...] + lnb_ref[...]).astype(BF16)
        first = _first_head_lanes()
        chunks = []
        for ci in range(nch):
            chunks.append(_mix_heads(wcat_ref, vnb[ci * CH:(ci + 1) * CH, :], first) + bsf_ref[...])
        mixed = jnp.concatenate(chunks, axis=0) if nch > 1 else chunks[0]
        mixed_ref[...] = mixed
        ya = gu * mixed
        gl = z[:, 2 * DA:2 * DA + DB] * _sig(z[:, 2 * DA + DB:])

        @pl.when(i == 0)
        def _():
            glbuf[0:HALO, :] = jnp.zeros((HALO, DB), F32)

        glbuf[HALO:HALO + tm, :] = gl
        _shifted_copies(glbuf, shbuf, tm)
        yc = jnp.zeros((tm, DB), F32) + cb_ref[...]
        for k in range(KW):
            yc = yc + cw_ref[k:k + 1, :] * _window(glbuf, shbuf, HALO - (KW - 1) + k, tm)
        glbuf[0:HALO, :] = gl[tm - HALO:, :]
        pmv = pm_ref[...]
        dc = yc - _grp_mean(yc, pmv)
        rsg = lax.rsqrt(_grp_mean(dc * dc, pmv) + EPS)
        yhat = dc * rsg
        yhat_ref[...] = yhat
        rsg_ref[...] = rsg
        yg = yhat * gng_ref[...] + gnb_ref[...]
        yb = yg * _sig(yg)
        na = ya * _rs(ya) * oga_ref[...]
        nb = yb * _rs(yb) * ogb_ref[...]
        yv = jnp.concatenate([na, nb], axis=1).astype(BF16)
        y_ref[...] = yv
        o = _dot(yv, wout_ref[...])
        o_ref[...] = o
        x1_ref[...] = xv + gate1 * o

        @pl.when(i == fwd_step)
        def _():
            _AllGather(sh_in, sh_out, wss, wrs, lsem).forward()

        @pl.when(i == nt - 1)
        def _():
            _AllGather(sh_in, sh_out, wss, wrs, lsem).finish()

    tile = lambda w: pl.BlockSpec((tm, w), lambda i: (i, 0))
    outs = pl.pallas_call(
        body,
        name="mix_fwd",
        grid=(nt,),
        in_specs=[tile(D), _full((8, D)), _full((1, D)), _resident((NDEV, D, WIN_B)), _full((1, DIN)),
                  _full((1, DA)), _full((1, DA)), _full((NH * CH, CH)), _full((CH, DA)), _full((HALO, DB)),
                  _full((1, DB)), _full((1, DB)), _full((1, DB)), _full((1, DA)), _full((1, DB)),
                  _resident((D, D)), _full((DB, DB))] + [HBM] * nw,
        out_specs=[tile(w) for w, _ in saved] + [HBM] * nw,
        out_shape=[jax.ShapeDtypeStruct((T, w), dt) for w, dt in saved]
                  + [jax.ShapeDtypeStruct((NDEV,) + s.shape, s.dtype) for s in ffn_shards],
        scratch_shapes=[pltpu.VMEM((HALO + tm, DB), F32), pltpu.VMEM((7, tm + SH_ROWS, DB), F32)] + AG_SEMS(nw),
        compiler_params=pltpu.CompilerParams(dimension_semantics=("arbitrary",), vmem_limit_bytes=VMEM_LIMIT),
    )(x, mod, g1, win, b_in, lng, lnb, wcat, bsf, cw, cb, gng, gnb, oga, ogb, wout, pm, *ffn_shards)
    return outs[:NSAVE], outs[NSAVE:]


FF_BLOCKS = ((0, 1024), (1024, 1024), (2048, 768))


def _ffn(x1, tgt, mod, g2, gf, wfi_t, wfo, tm):
    T = x1.shape[0]
    nt = T // tm

    def body(x1_ref, tgt_ref, mod_ref, g2_ref, gf_ref, wfi_ref, wfo_ref,
             dx1_ref, h2_ref, dgu_ref, act_ref, dxg_ref, acc_ref, g_s, u_s):
        i = pl.program_id(0)

        @pl.when(i == 0)
        def _():
            acc_ref[...] = jnp.zeros((8, D), F32)

        x1 = x1_ref[...]
        shift2 = mod_ref[3:4, :]
        scale2 = mod_ref[4:5, :]
        gate2 = mod_ref[5:6, :]
        shiftf = mod_ref[6:7, :]
        scalef = mod_ref[7:8, :]
        g2v = g2_ref[...]
        gfv = gf_ref[...]
        r2 = _rs(x1)
        xn2 = x1 * r2
        h2b = (xn2 * g2v * (1.0 + scale2) + shift2).astype(BF16)
        h2_ref[...] = h2b
        f = jnp.zeros((tm, D), F32)
        for o, w in FF_BLOCKS:
            g = _dot_nt(h2b, wfi_ref[o:o + w, :])
            u = _dot_nt(h2b, wfi_ref[DFF + o:DFF + o + w, :])
            g_s[:, o:o + w] = g
            u_s[:, o:o + w] = u
            actb = (g * _sig(g) * u).astype(BF16)
            act_ref[:, o:o + w] = actb
            f = f + _dot(actb, wfo_ref[o:o + w, :])
        x2 = x1 + gate2 * f
        rf = _rs(x2)
        xnf = x2 * rf
        out = xnf * gfv * (1.0 + scalef) + shiftf
        e = out - tgt_ref[...]
        dout = e * (1.0 / D)
        acc_ref[7:8, :] += _colsum(e * e)
        acc_ref[0:1, :] += _colsum(dout)
        acc_ref[1:2, :] += _colsum(dout * xnf * gfv)
        acc_ref[2:3, :] += _colsum(dout * (1.0 + scalef) * xnf)
        dxnf = dout * (1.0 + scalef) * gfv
        dx2 = rf * (dxnf - xnf * jnp.mean(dxnf * xnf, axis=-1, keepdims=True))
        acc_ref[3:4, :] += _colsum(dx2 * f)
        dxgb = (dx2 * gate2).astype(BF16)
        dxg_ref[...] = dxgb
        dh2 = jnp.zeros((tm, D), F32)
        for o, w in FF_BLOCKS:
            dact = _dot_nt(dxgb, wfo_ref[o:o + w, :])
            g = g_s[:, o:o + w]
            u = u_s[:, o:o + w]
            s = _sig(g)
            dgb = (dact * u * (s * (1.0 + g * (1.0 - s)))).astype(BF16)
            dub = (dact * (g * s)).astype(BF16)
            dgu_ref[:, o:o + w] = dgb
            dgu_ref[:, DFF + o:DFF + o + w] = dub
            dh2 = dh2 + _dot(dgb, wfi_ref[o:o + w, :])
            dh2 = dh2 + _dot(dub, wfi_ref[DFF + o:DFF + o + w, :])
        acc_ref[4:5, :] += _colsum(dh2)
        acc_ref[5:6, :] += _colsum(dh2 * xn2 * g2v)
        acc_ref[6:7, :] += _colsum(dh2 * (1.0 + scale2) * xn2)
        dxn2 = dh2 * (1.0 + scale2) * g2v
        dx1_ref[...] = dx2 + r2 * (dxn2 - xn2 * jnp.mean(dxn2 * xn2, axis=-1, keepdims=True))

    tile = lambda w: pl.BlockSpec((tm, w), lambda i: (i, 0))
    return pl.pallas_call(
        body,
        name="ffn_fwd_bwd",
        grid=(nt,),
        in_specs=[tile(D), tile(D), _full((8, D)), _full((1, D)), _full((1, D)),
                  _resident((2 * DFF, D)), _resident((DFF, D))],
        out_specs=[tile(D), tile(D), tile(2 * DFF), tile(DFF), tile(D), _full((8, D))],
        out_shape=[jax.ShapeDtypeStruct((T, D), F32), jax.ShapeDtypeStruct((T, D), BF16),
                   jax.ShapeDtypeStruct((T, 2 * DFF), BF16), jax.ShapeDtypeStruct((T, DFF), BF16),
                   jax.ShapeDtypeStruct((T, D), BF16), jax.ShapeDtypeStruct((8, D), F32)],
        scratch_shapes=[pltpu.VMEM((tm, DFF), F32), pltpu.VMEM((tm, DFF), F32)],
        compiler_params=pltpu.CompilerParams(dimension_semantics=("arbitrary",), vmem_limit_bytes=VMEM_LIMIT),
    )(x1, tgt, mod, g2, gf, wfi_t, wfo)


def _mix_bwd(dx1, x, zvg, mixed, o, hb, yb, gu, dgu, dgv, vhat, rslb, yhat, rsg, mod, g1, win, lng, lnb, wcat, wcat_t,
             cw, gng, gnb, oga, ogb, wout, pm, esel, after, tm):
    T = x.shape[0]
    nt = T // tm
    nch = tm // CH
    WOB = 256

    def body(dx1_ref, x_ref, zvg_ref, mixed_ref, o_ref, hb_ref, yb_ref, gu_ref, dgu_ref, dgv_ref, vhat_ref, rsl_ref,
             yhat_ref, rsg_ref, mod_ref, g1_ref, win_ref, lng_ref, lnb_ref, wcat_ref, wcatt_ref, cw_ref, gng_ref,
             gnb_ref, oga_ref, ogb_ref, wout_ref, pm_ref, esel_ref, after_ref,
             gx_ref, accv_ref, accb_ref, acca_ref, accbs_ref, accws_ref, acccw_ref, gwin_ref, gwout_ref,
             dycbuf, shbuf, bs_s, acc_win, acc_wout, st_win, st_wout):
        i = pl.program_id(0)

        @pl.when(i == 0)
        def _():
            acc_win[...] = jnp.zeros((NDEV, D, WIN_B), F32)
            acc_wout[...] = jnp.zeros((D, D), F32)
            accv_ref[...] = jnp.zeros((8, D), F32)
            accb_ref[...] = jnp.zeros((1, DIN), F32)
            acca_ref[...] = jnp.zeros((8, DA), F32)
            accws_ref[...] = jnp.zeros((NH * CH, CH), F32)
            acccw_ref[...] = jnp.zeros((HALO, DB), F32)
            bs_s[...] = jnp.zeros((CH, DA), F32)
            dycbuf[tm:tm + HALO, :] = jnp.zeros((HALO, DB), F32)

        shift1 = mod_ref[0:1, :]
        scale1 = mod_ref[1:2, :]
        gate1 = mod_ref[2:3, :]
        g1v = g1_ref[...]
        xv = x_ref[...]
        r1 = _rs(xv)
        xn1 = xv * r1
        val = zvg_ref[:, 0:DB]
        gate = zvg_ref[:, DB:]
        gu = gu_ref[...]
        dgelu_u = dgu_ref[...]
        dgelu_v = dgv_ref[...]
        vhat = vhat_ref[...]
        rsl = rsl_ref[:, 0:1]
        lngv = lng_ref[...]
        vnb = (vhat * lngv + lnb_ref[...]).astype(BF16)
        mixed = mixed_ref[...]
        ya = gu * mixed
        ra = _rs(ya)
        yan = ya * ra
        sgt = _sig(gate)
        gl = val * sgt
        pmv = pm_ref[...]
        rsg = rsg_ref[...]
        yhat = yhat_ref[...]
        gngv = gng_ref[...]
        yg = yhat * gngv + gnb_ref[...]
        sgy = _sig(yg)
        yb = yg * sgy
        rb = _rs(yb)
        ybn = yb * rb
        dx1 = dx1_ref[...]
        accv_ref[0:1, :] += _colsum(dx1 * o_ref[...])
        dogb = (dx1 * gate1).astype(BF16)
        acc_wout[...] += _dot_tn(yb_ref[...], dogb)
        dy = _dot_nt(dogb, wout_ref[...])
        dna = dy[:, 0:DA]
        dnb = dy[:, DA:]
        ogav = oga_ref[...]
        ogbv = ogb_ref[...]
        acca_ref[2:3, :] += _colsum(dna * yan)
        acca_ref[3:4, :] += _colsum(dnb * ybn)
        ta = dna * ogav
        dya = ra * (ta - yan * jnp.mean(ta * yan, axis=-1, keepdims=True))
        tb = dnb * ogbv
        dyb = rb * (tb - ybn * jnp.mean(tb * ybn, axis=-1, keepdims=True))
        dgu = dya * mixed
        dm = dya * gu
        first = _first_head_lanes()
        zero = jnp.zeros((CH, CH), BF16)
        dvn_chunks = []
        bs_acc = bs_s[...]
        for ci in range(nch):
            dmc = dm[ci * CH:(ci + 1) * CH, :]
            bs_acc = bs_acc + dmc
            dmcb = dmc.astype(BF16)
            dvn_chunks.append(_mix_heads(wcatt_ref, dmcb, first))
            vc = vnb[ci * CH:(ci + 1) * CH, :]
            for p in range(NH // 2):
                xt = dmcb[:, p * CH:(p + 1) * CH]
                vt = vc[:, p * CH:(p + 1) * CH]
                accws_ref[(2 * p) * CH:(2 * p + 1) * CH, :] += _dot_nt(jnp.where(first, xt, zero), vt)
                accws_ref[(2 * p + 1) * CH:(2 * p + 2) * CH, :] += _dot_nt(jnp.where(first, zero, xt), vt)
        bs_s[...] = bs_acc
        dvn = jnp.concatenate(dvn_chunks, axis=0) if nch > 1 else dvn_chunks[0]
        acca_ref[0:1, :] += _colsum(dvn * vhat)
        acca_ref[1:2, :] += _colsum(dvn)
        dvh = dvn * lngv
        dgv = rsl * (dvh - jnp.mean(dvh, axis=-1, keepdims=True)
                     - vhat * jnp.mean(dvh * vhat, axis=-1, keepdims=True))
        du = dgu * dgelu_u
        dv = dgv * dgelu_v
        dyg = dyb * (sgy * (1.0 + yg * (1.0 - sgy)))
        acca_ref[5:6, :] += _colsum(dyg * yhat)
        acca_ref[6:7, :] += _colsum(dyg)
        dyh = dyg * gngv
        dyc = rsg * (dyh - _grp_mean(dyh, pmv) - yhat * _grp_mean(dyh * yhat, pmv))
        acca_ref[4:5, :] += _colsum(dyc)
        dycbuf[0:tm, :] = dyc
        _shifted_copies(dycbuf, shbuf, tm)
        dgl = jnp.zeros((tm, DB), F32)
        for k in range(KW):
            win_k = _window(dycbuf, shbuf, KW - 1 - k, tm)
            dgl = dgl + cw_ref[k:k + 1, :] * win_k
            acccw_ref[k:k + 1, :] += _colsum(win_k * gl)
        dycbuf[tm:tm + HALO, :] = dyc[0:HALO, :]
        dval = dgl * sgt
        dgate = dgl * val * sgt * (1.0 - sgt)
        dz = jnp.concatenate([du, dv, dval, dgate], axis=1)
        accb_ref[...] += _colsum(dz)
        dzb = dz.astype(BF16)
        hbv = hb_ref[...]
        dh = jnp.zeros((tm, D), F32)
        for j in range(NDEV):
            dzj = dzb[:, j * WIN_B:(j + 1) * WIN_B]
            acc_win[j] += _dot_tn(hbv, dzj)
            dh = dh + _dot_nt(dzj, win_ref[j])
        accv_ref[1:2, :] += _colsum(dh)
        dh_xn = _colsum(dh * xn1)
        accv_ref[2:3, :] += dh_xn * g1v
        accv_ref[3:4, :] += dh_xn * (1.0 + scale1)
        dxn1 = dh * (1.0 + scale1) * g1v
        gx_ref[...] = dx1 + r1 * (dxn1 - xn1 * jnp.mean(dxn1 * xn1, axis=-1, keepdims=True))

        @pl.when(i == nt - 1)
        def _():
            rows = lax.broadcasted_iota(jnp.int32, (NH * CH, CH), 0) & (CH - 1)
            cols = lax.broadcasted_iota(jnp.int32, (NH * CH, CH), 1)
            accws_ref[...] = jnp.where(cols <= rows, accws_ref[...], 0.0)
            bs = bs_s[...]
            hi = bs.astype(BF16)
            r1_ = bs - hi.astype(F32)
            mid = r1_.astype(BF16)
            lo = (r1_ - mid.astype(F32)).astype(BF16)
            ev = esel_ref[...]
            accbs_ref[...] = _dot(hi, ev) + _dot(mid, ev) + _dot(lo, ev)
            for j in range(NDEV):
                st_win[...] = acc_win[j].astype(BF16)
                pltpu.sync_copy(st_win, gwin_ref.at[j])
            for j in range(D // WOB):
                st_wout[...] = acc_wout[j * WOB:(j + 1) * WOB, :].astype(BF16)
                pltpu.sync_copy(st_wout, gwout_ref.at[pl.ds(j * WOB, WOB)])

    rev = lambda w: pl.BlockSpec((tm, w), lambda i: (nt - 1 - i, 0))
    outs = pl.pallas_call(
        body,
        name="mix_bwd",
        grid=(nt,),
        in_specs=[rev(D), rev(D), rev(2 * DB), rev(DA), rev(D), rev(D), rev(D), rev(DA), rev(DA), rev(DA), rev(DA),
                  rev(CH), rev(DB), rev(DB), _full((8, D)), _full((1, D)),
                  _resident((NDEV, D, WIN_B)), _full((1, DA)), _full((1, DA)), _full((NH * CH, CH)),
                  _full((NH * CH, CH)), _full((HALO, DB)), _full((1, DB)), _full((1, DB)), _full((1, DA)),
                  _full((1, DB)), _resident((D, D)), _full((DB, DB)), _full((DA, CH)), HBM],
        out_specs=[rev(D), _full((8, D)), _full((1, DIN)), _full((8, DA)), _full((CH, CH)),
                   _full((NH * CH, CH)), _full((HALO, DB)), HBM, HBM],
        out_shape=[jax.ShapeDtypeStruct((T, D), F32), jax.ShapeDtypeStruct((8, D), F32),
                   jax.ShapeDtypeStruct((1, DIN), F32), jax.ShapeDtypeStruct((8, DA), F32),
                   jax.ShapeDtypeStruct((CH, CH), F32), jax.ShapeDtypeStruct((NH * CH, CH), F32),
                   jax.ShapeDtypeStruct((HALO, DB), F32),
                   jax.ShapeDtypeStruct((NDEV, D, WIN_B), BF16), jax.ShapeDtypeStruct((D, D), BF16)],
        scratch_shapes=[pltpu.VMEM((tm + HALO, DB), F32), pltpu.VMEM((7, tm + SH_ROWS, DB), F32),
                        pltpu.VMEM((CH, DA), F32), pltpu.VMEM((NDEV, D, WIN_B), F32), pltpu.VMEM((D, D), F32),
                        pltpu.VMEM((D, WIN_B), BF16), pltpu.VMEM((WOB, D), BF16)],
        compiler_params=pltpu.CompilerParams(dimension_semantics=("arbitrary",), vmem_limit_bytes=VMEM_LIMIT),
    )(dx1, x, zvg, mixed, o, hb, yb, gu, dgu, dgv, vhat, rslb, yhat, rsg, mod, g1, win, lng, lnb, wcat, wcat_t, cw,
      gng, gnb, oga, ogb, wout, pm, esel, after)
    return outs[:7], outs[7:]


def _wgrad_rows(a, b, bm, tk, name):
    T, M = a.shape
    N = b.shape[1]
    nk = T // tk

    def body(a_ref, b_ref, o_ref, acc):
        k = pl.program_id(1)

        @pl.when(k == 0)
        def _():
            acc[...] = jnp.zeros((bm, N), F32)

        acc[...] += _dot_tn(a_ref[...], b_ref[...])

        @pl.when(k == nk - 1)
        def _():
            o_ref[...] = acc[...].astype(BF16)

    return pl.pallas_call(
        body, name=name, grid=(M // bm, nk),
        in_specs=[pl.BlockSpec((tk, bm), lambda j, k: (k, j)), pl.BlockSpec((tk, N), lambda j, k: (k, 0))],
        out_specs=pl.BlockSpec((bm, N), lambda j, k: (j, 0)),
        out_shape=jax.ShapeDtypeStruct((M, N), BF16),
        scratch_shapes=[pltpu.VMEM((bm, N), F32)],
        compiler_params=pltpu.CompilerParams(dimension_semantics=("arbitrary", "arbitrary"),
                                             vmem_limit_bytes=VMEM_LIMIT),
    )(a, b)


def _small_copy(src, dst, ss, rs, k, to):
    return pltpu.make_async_remote_copy(src_ref=src, dst_ref=dst, send_sem=ss.at[k], recv_sem=rs.at[k],
                                        device_id=to, device_id_type=MESH)


def _gather(c_row, ada_w, ada_b8, ada_f_w, ada_f_b8, conv_s, shards):
    nw = len(shards)

    def body(c_ref, adaw_ref, adab_ref, adafw_ref, adafb_ref, conv_ref, *rest):
        w_f32 = rest[:nw]
        call_ref, cparts_ref, cfparts_ref, convg_ref = rest[nw:nw + 4]
        w_out = rest[nw + 4:2 * nw + 4]
        part_s, partf_s = rest[2 * nw + 4:2 * nw + 6]
        w_in = rest[2 * nw + 6:3 * nw + 6]
        wss, wrs, lsem, s1, r1, s2, r2, s3, r3, s4, r4 = rest[3 * nw + 6:]
        x, y, c, idx = _place()
        me = (x, y, c)
        for a in range(nw):
            w_in[a][...] = w_f32[a][...].astype(BF16)
        ag = _AllGather(w_in, w_out, wss, wrs, lsem)
        ag.start()
        call_ref[pl.ds(idx, 1), :] = c_ref[...]
        convg_ref[idx] = conv_ref[...]
        ph1 = []
        for k in range(1, NDEV):
            to = _dev(idx ^ k)
            ph1.append(_small_copy(c_ref, call_ref.at[pl.ds(idx, 1)], s1, r1, k - 1, to))
            ph1.append(_small_copy(conv_ref, convg_ref.at[idx], s2, r2, k - 1, to))
        for cp in ph1:
            cp.start()
        for k in range(1, NDEV):
            src_dev = idx ^ k
            _small_copy(c_ref, call_ref.at[pl.ds(src_dev, 1)], s1, r1, k - 1, me).wait_recv()
            _small_copy(conv_ref, convg_ref.at[src_dev], s2, r2, k - 1, me).wait_recv()
        call = call_ref[...]
        cact = (call * _sig(call))
        part_s[...] = jnp.dot(cact, adaw_ref[...], preferred_element_type=F32,
                              precision=lax.Precision.HIGHEST) + adab_ref[pl.ds(idx, 1), :]
        partf_s[...] = jnp.dot(cact, adafw_ref[...], preferred_element_type=F32,
                               precision=lax.Precision.HIGHEST) + adafb_ref[pl.ds(idx, 1), :]
        cparts_ref[pl.ds(idx, 1), :] = part_s[pl.ds(idx, 1), :]
        cfparts_ref[pl.ds(idx, 1), :] = partf_s[pl.ds(idx, 1), :]
        ph2 = []
        for k in range(1, NDEV):
            t = idx ^ k
            ph2.append(_small_copy(part_s.at[pl.ds(t, 1)], cparts_ref.at[pl.ds(idx, 1)], s3, r3, k - 1, _dev(t)))
            ph2.append(_small_copy(partf_s.at[pl.ds(t, 1)], cfparts_ref.at[pl.ds(idx, 1)], s4, r4, k - 1, _dev(t)))
        for cp in ph2:
            cp.start()
        for k in range(1, NDEV):
            src_dev = idx ^ k
            _small_copy(part_s.at[pl.ds(0, 1)], cparts_ref.at[pl.ds(src_dev, 1)], s3, r3, k - 1, me).wait_recv()
            _small_copy(partf_s.at[pl.ds(0, 1)], cfparts_ref.at[pl.ds(src_dev, 1)], s4, r4, k - 1, me).wait_recv()
        for cp in ph1 + ph2:
            cp.wait_send()
        ag.forward()
        ag.finish()

    dma7 = pltpu.SemaphoreType.DMA((NDEV - 1,))
    outs = pl.pallas_call(
        body,
        name="gather_weights",
        in_specs=[VM] * (6 + nw),
        out_specs=[VM] * 4 + [HBM] * nw,
        out_shape=[jax.ShapeDtypeStruct((NDEV, D), F32), jax.ShapeDtypeStruct((NDEV, ada_w.shape[1]), F32),
                   jax.ShapeDtypeStruct((NDEV, ada_f_w.shape[1]), F32),
                   jax.ShapeDtypeStruct((NDEV,) + conv_s.shape, F32)]
                  + [jax.ShapeDtypeStruct((NDEV,) + s.shape, BF16) for s in shards],
        scratch_shapes=[pltpu.VMEM((NDEV, ada_w.shape[1]), F32), pltpu.VMEM((NDEV, ada_f_w.shape[1]), F32)]
                       + [pltpu.VMEM(s.shape, BF16) for s in shards] + AG_SEMS(nw) + [dma7] * 8,
        compiler_params=pltpu.CompilerParams(vmem_limit_bytes=VMEM_LIMIT),
    )(c_row, ada_w, ada_b8, ada_f_w, ada_f_b8, conv_s, *shards)
    return outs[0], outs[1], outs[2], outs[3], outs[4:]


_VEC_AT = {
    "norm1_g": (8, 0, D), "a_ln_g": (11, 0, DA), "a_ln_b": (11, DA, DA), "a_spatial_b": (12, 0, D),
    "b_conv_b": (13, 0, DB), "b_gn_g": (13, DB, DB), "b_gn_b": (14, 0, DB), "out_norm_a_g": (14, DB, DA),
    "out_norm_b_g": (15, 0, DB), "norm2_g": (16, 0, D), "norm_f_g": (17, 0, D),
}
_LOSS_ROW = 18
_CW_ROW = 24


def _reduce_small(acc_f, acc_v, acc_b, acc_a, acc_bs, acc_cw, dws, after, pair_grads):
    npg = len(pair_grads)

    def body(accf_ref, accv_ref, accb_ref, acca_ref, accbs_ref, acccw_ref, dws_ref, after_ref, *rest):
        pg = rest[:npg]
        vsum_ref, dcond_ref, wssum_ref = rest[npg:npg + 3]
        pq = rest[npg + 3:2 * npg + 3]
        vloc, vbuf, wbuf, wown, s1, r1, s2, r2, s3, r3 = rest[2 * npg + 3:2 * npg + 13]
        pland = rest[2 * npg + 13:3 * npg + 13]
        pstage = rest[3 * npg + 13:4 * npg + 13]
        ps, pr, pls, pss = rest[4 * npg + 13:]
        x, y, c, idx = _place()
        me = (x, y, c)
        sibling = (x, y, 1 - c)
        chips = [(1 - x, y), (x, 1 - y), (1 - x, 1 - y)]
        blk = lambda p: 4 * p[0] + 2 * p[1] + p[2]
        give = [blk((*ch, 1 - c)) for ch in chips] + [blk(sibling)]
        pair = [pltpu.make_async_remote_copy(src_ref=pg[a].at[b], dst_ref=pland[a].at[j], send_sem=ps.at[a, j],
                                             recv_sem=pr.at[a, j], device_id=sibling, device_id_type=MESH)
                for a in range(npg) for j, b in enumerate(give)]
        loads = [pltpu.make_async_copy(pg[a].at[blk((*ch, c))], pstage[a].at[j], pls.at[a, j])
                 for a in range(npg) for j, ch in enumerate(chips)]
        for cp in pair + loads:
            cp.start()
        vloc[...] = jnp.zeros((NVEC, D), F32)
        vloc[0:1, :] = accv_ref[1:2, :]
        vloc[1:2, :] = accv_ref[2:3, :]
        vloc[2:3, :] = accv_ref[0:1, :]
        vloc[3:4, :] = accf_ref[4:5, :]
        vloc[4:5, :] = accf_ref[5:6, :]
        vloc[5:6, :] = accf_ref[3:4, :]
        vloc[6:7, :] = accf_ref[0:1, :]
        vloc[7:8, :] = accf_ref[1:2, :]
        vloc[8:9, :] = accv_ref[3:4, :]
        vloc[9:10, :] = accb_ref[:, 0:D]
        vloc[10:11, :] = accb_ref[:, D:]
        vloc[11:12, 0:DA] = acca_ref[0:1, :]
        vloc[11:12, DA:] = acca_ref[1:2, :]
        bst = accbs_ref[...].T
        for h in range(NH):
            vloc[12:13, h * CH:(h + 1) * CH] = bst[h:h + 1, :]
        vloc[13:14, 0:DB] = acca_ref[4:5, :]
        vloc[13:14, DB:] = acca_ref[5:6, :]
        vloc[14:15, 0:DB] = acca_ref[6:7, :]
        vloc[14:15, DB:] = acca_ref[2:3, :]
        vloc[15:16, 0:DB] = acca_ref[3:4, :]
        vloc[16:17, :] = accf_ref[6:7, :]
        vloc[17:18, :] = accf_ref[2:3, :]
        vloc[_LOSS_ROW:_LOSS_ROW + 1, :] = accf_ref[7:8, :]
        vloc[_CW_ROW:_CW_ROW + HALO // 2, 0:DB] = acccw_ref[0:HALO // 2, :]
        vloc[_CW_ROW:_CW_ROW + HALO // 2, DB:] = acccw_ref[HALO // 2:, :]
        vbuf[idx] = vloc[...]
        rows_of = lambda t: pl.ds(pl.multiple_of(t * CH, CH), CH)
        wbuf[0] = dws_ref[rows_of(idx), :]
        sm = []
        for k in range(1, NDEV):
            t = idx ^ k
            sm.append(_small_copy(vloc, vbuf.at[idx], s1, r1, k - 1, _dev(t)))
            sm.append(_small_copy(dws_ref.at[rows_of(t)], wbuf.at[k], s2, r2, k - 1, _dev(t)))
        for cp in sm:
            cp.start()
        for k in range(1, NDEV):
            _small_copy(dws_ref.at[rows_of(0)], wbuf.at[k], s2, r2, k - 1, me).wait_recv()
        ws = wbuf[0]
        for k in range(1, NDEV):
            ws = ws + wbuf[k]
        wown[...] = ws
        wssum_ref[rows_of(idx), :] = ws
        ag = [_small_copy(wown, wssum_ref.at[rows_of(idx)], s3, r3, k - 1, _dev(idx ^ k)) for k in range(1, NDEV)]
        for cp in ag:
            cp.start()
        for cp in loads:
            cp.wait()
        for cp in pair:
            cp.wait_recv()
        stores = []
        for a in range(npg):
            for j in range(3):
                pstage[a][j] = (pstage[a][j].astype(F32) + pland[a][j].astype(F32)).astype(BF16)
                stores.append(pltpu.make_async_copy(pstage[a].at[j], pq[a].at[j], pss.at[a, j]))
            stores.append(pltpu.make_async_copy(pland[a].at[3], pq[a].at[3], pss.at[a, 3]))
        for cp in stores:
            cp.start()
        for k in range(1, NDEV):
            _small_copy(vloc, vbuf.at[idx ^ k], s1, r1, k - 1, me).wait_recv()
        vs = vbuf[0]
        for d in range(1, NDEV):
            vs = vs + vbuf[d]
        vsum_ref[...] = vs
        for d in range(NDEV):
            dcond_ref[d] = vbuf[d, 0:8, :]
        for k in range(1, NDEV):
            _small_copy(wown, wssum_ref.at[rows_of(idx ^ k)], s3, r3, k - 1, me).wait_recv()
        for cp in sm + ag:
            cp.wait_send()
        for cp in stores:
            cp.wait()
        for cp in pair:
            cp.wait_send()

    dma7 = pltpu.SemaphoreType.DMA((NDEV - 1,))
    dma4 = pltpu.SemaphoreType.DMA((npg, 4))
    outs = pl.pallas_call(
        body,
        name="reduce_small",
        in_specs=[VM] * 7 + [HBM] + [HBM] * npg,
        out_specs=[VM, VM, VM] + [HBM] * npg,
        out_shape=[jax.ShapeDtypeStruct((NVEC, D), F32), jax.ShapeDtypeStruct((NDEV, 8, D), F32),
                   jax.ShapeDtypeStruct(dws.shape, F32)]
                  + [jax.ShapeDtypeStruct((4,) + g.shape[1:], g.dtype) for g in pair_grads],
        scratch_shapes=[pltpu.VMEM((NVEC, D), F32), pltpu.VMEM((NDEV, NVEC, D), F32),
                        pltpu.VMEM((NDEV, CH, CH), F32), pltpu.VMEM((CH, CH), F32)] + [dma7] * 6
                       + [pltpu.VMEM((4,) + g.shape[1:], g.dtype) for g in pair_grads]
                       + [pltpu.VMEM((3,) + g.shape[1:], g.dtype) for g in pair_grads] + [dma4] * 4,
        compiler_params=pltpu.CompilerParams(vmem_limit_bytes=VMEM_LIMIT),
    )(acc_f, acc_v, acc_b, acc_a, acc_bs, acc_cw, dws, after, *pair_grads)
    return outs[0], outs[1], outs[2], outs[3:]


HBM_ONLY = pl.BlockSpec(memory_space=pltpu.HBM)
SEM = pl.BlockSpec(memory_space=pltpu.SEMAPHORE)
EFFECT = pltpu.SideEffectType.DATAFLOW_SIDE_EFFECTING


def _rs_copies(g_refs, land_refs, sems, chips):
    x, y, c, idx = _place()
    if chips:
        routes = [(j, j, (*ch, c)) for j, ch in enumerate([(1 - x, y), (x, 1 - y), (1 - x, 1 - y)])]
    else:
        routes = [(idx ^ k, k - 1, _dev(idx ^ k)) for k in range(1, NDEV)]
    cps = []
    for src, dst, to in routes:
        for a in range(len(g_refs)):
            n = len(cps)
            cps.append(pltpu.make_async_remote_copy(
                src_ref=g_refs[a].at[src], dst_ref=land_refs[a].at[dst], send_sem=sems[2 * n],
                recv_sem=sems[2 * n + 1], device_id=to, device_id_type=MESH))
    return cps


def _rs_start(grads, name, after=(), chips=False):
    nw = len(grads)
    npeer = 3 if chips else NDEV - 1
    nsem = 2 * nw * npeer
    lands = [lax.empty((npeer,) + g.shape[1:], g.dtype) for g in grads]

    def body(*refs):
        g_refs, land_refs = refs[:nw], refs[nw:2 * nw]
        sems = refs[2 * nw + len(after):2 * nw + len(after) + nsem]
        token = refs[-1]
        for cp in _rs_copies(g_refs, land_refs, sems, chips):
            cp.start()
        token[...] = jnp.zeros_like(token)

    outs = pl.pallas_call(
        body, name=name,
        out_shape=(*[pltpu.SemaphoreType.DMA(())] * nsem,
                   *[pltpu.HBM(g.shape, g.dtype) for g in grads], *[pltpu.HBM(l.shape, l.dtype) for l in lands],
                   jax.ShapeDtypeStruct((8, CH), F32)),
        in_specs=[HBM_ONLY] * (2 * nw) + [HBM] * len(after),
        out_specs=(*[SEM] * nsem, *[HBM_ONLY] * (2 * nw), VM),
        input_output_aliases={i: nsem + i for i in range(2 * nw)},
        compiler_params=pltpu.CompilerParams(has_side_effects=EFFECT),
    )(*[pltpu.with_memory_space_constraint(g, pltpu.HBM) for g in grads],
      *[pltpu.with_memory_space_constraint(l, pltpu.HBM) for l in lands], *after)
    return outs[:nsem], outs[nsem:nsem + nw], outs[nsem + nw:nsem + 2 * nw], outs[-1]


def _rs_wait(sems, g_thru, land_thru, after, name, chips=False):
    nw = len(g_thru)
    nsem = len(sems)

    def body(*refs):
        g_refs, land_refs = refs[:nw], refs[nw:2 * nw]
        for cp in _rs_copies(g_refs, land_refs, refs[2 * nw:2 * nw + nsem], chips):
            cp.wait_send()
            cp.wait_recv()

    outs = pl.pallas_call(
        body, name=name,
        out_shape=tuple(pltpu.HBM(a.shape, a.dtype) for a in list(g_thru) + list(land_thru)),
        in_specs=[HBM_ONLY] * (2 * nw) + [SEM] * nsem + [HBM] * len(after),
        out_specs=tuple([HBM_ONLY] * (2 * nw)),
        input_output_aliases={i: i for i in range(2 * nw)},
        compiler_params=pltpu.CompilerParams(has_side_effects=EFFECT),
    )(*g_thru, *land_thru, *sems, *after)
    return outs[:nw], outs[nw:]


def _adamw(w, g, m, v):
    m2 = ADAM_B1 * m + (1.0 - ADAM_B1) * g
    v2 = ADAM_B2 * v + (1.0 - ADAM_B2) * (g * g)
    m_hat = m2 / (1.0 - ADAM_B1 ** ADAM_STEP)
    v_hat = v2 / (1.0 - ADAM_B2 ** ADAM_STEP)
    delta = -ADAM_LR * (m_hat / (jnp.sqrt(v_hat) + ADAM_EPS) + ADAM_WD * w)
    return delta, m2, v2


def _adam_big(r, w, m, v, rb, name, own, after=None, sib=None):
    R, C = w.shape
    ns = r.shape[0]
    g_all, idx1 = own

    def body(idx_ref, r_ref, own_ref, *refs):
        w_ref, m_ref, v_ref, g_ref, d_ref, m2_ref, v2_ref = refs[len(refs) - 7:]
        g = own_ref[0].astype(F32)
        if sib is not None:
            g = g + refs[0][0].astype(F32)
        for k in range(ns):
            g = g + r_ref[k].astype(F32)
        g_ref[...] = g
        d_ref[...], m2_ref[...], v2_ref[...] = _adamw(w_ref[...], g, m_ref[...], v_ref[...])

    t2 = pl.BlockSpec((rb, C), lambda i, idx_ref: (i, 0))
    sd = jax.ShapeDtypeStruct((R, C), F32)
    extra_specs = ([pl.BlockSpec((1, rb, C), lambda i, idx_ref: (3, i, 0))] if sib is not None else []) \
        + ([HBM] if after is not None else [])
    extra = ([sib] if sib is not None else []) + ([after] if after is not None else [])
    return pl.pallas_call(
        body, name=name,
        grid_spec=pltpu.PrefetchScalarGridSpec(
            num_scalar_prefetch=1, grid=(R // rb,),
            in_specs=[pl.BlockSpec((ns, rb, C), lambda i, idx_ref: (0, i, 0)),
                      pl.BlockSpec((1, rb, C), lambda i, idx_ref: (idx_ref[0], i, 0))] + extra_specs + [t2, t2, t2],
            out_specs=[t2, t2, t2, t2]),
        out_shape=[sd, sd, sd, sd],
        compiler_params=pltpu.CompilerParams(dimension_semantics=("arbitrary",), vmem_limit_bytes=VMEM_LIMIT),
    )(idx1, r, g_all, *extra, w, m, v)


def _adam_ada(cact_t, dcs, w, m, v, rb, name):
    R, C = w.shape

    def body(ct_ref, dc_ref, w_ref, m_ref, v_ref, g_ref, d_ref, m2_ref, v2_ref):
        g = jnp.dot(ct_ref[...], dc_ref[...], preferred_element_type=F32, precision=lax.Precision.HIGHEST)
        g_ref[...] = g
        d_ref[...], m2_ref[...], v2_ref[...] = _adamw(w_ref[...], g, m_ref[...], v_ref[...])

    t2 = pl.BlockSpec((rb, C), lambda i: (i, 0))
    sd = jax.ShapeDtypeStruct((R, C), F32)
    return pl.pallas_call(
        body, name=name, grid=(R // rb,),
        in_specs=[pl.BlockSpec((rb, NDEV), lambda i: (i, 0)), _full((NDEV, C)), t2, t2, t2],
        out_specs=[t2, t2, t2, t2], out_shape=[sd, sd, sd, sd],
        compiler_params=pltpu.CompilerParams(dimension_semantics=("arbitrary",), vmem_limit_bytes=VMEM_LIMIT),
    )(cact_t, dcs, w, m, v)


_SMALL = ["ada_b", "ada_f_b", "norm1_g", "b_in", "a_ln_g", "a_ln_b", "a_spatial_b", "b_conv_b", "b_gn_g", "b_gn_b",
          "out_norm_a_g", "out_norm_b_g", "norm2_g", "norm_f_g", "a_spatial_w", "b_conv_w"]


def _adam_small(vsum, wssum, gcw, params):
    names = _SMALL
    flat = []
    for n in names:
        flat += list(params[n])

    def body(vs_ref, ws_ref, gcw_ref, *rest):
        ins = rest[:3 * len(names)]
        outs = rest[3 * len(names):]
        for pi, n in enumerate(names):
            w_ref, m_ref, v_ref = ins[3 * pi:3 * pi + 3]
            g_ref, d_ref, m2_ref, v2_ref = outs[4 * pi:4 * pi + 4]
            if n in ("ada_b", "ada_f_b", "b_in"):
                row0 = {"ada_b": 0, "ada_f_b": 6, "b_in": 9}[n]
                pieces = [(vs_ref[row0 + r:row0 + r + 1, :], slice(r * D, (r + 1) * D))
                          for r in range(w_ref.shape[1] // D)]
            elif n == "a_spatial_w":
                pieces = [(ws_ref[...], slice(None))]
            elif n == "b_conv_w":
                pieces = [(gcw_ref[...], slice(None))]
            else:
                row, off, width = _VEC_AT[n]
                pieces = [(vs_ref[row:row + 1, off:off + width], slice(None))]
            for g, cs in pieces:
                g_ref[:, cs] = g
                d_ref[:, cs], m2_ref[:, cs], v2_ref[:, cs] = _adamw(w_ref[:, cs], g, m_ref[:, cs], v_ref[:, cs])

    out_shape = []
    for n in names:
        out_shape += [jax.ShapeDtypeStruct(params[n][0].shape, F32)] * 4
    outs = pl.pallas_call(
        body, name="adam_small",
        in_specs=[VM] * (3 + len(flat)), out_specs=[VM] * len(out_shape), out_shape=out_shape,
        compiler_params=pltpu.CompilerParams(vmem_limit_bytes=VMEM_LIMIT),
    )(vsum, wssum, gcw, *flat)
    return {n: outs[4 * pi:4 * pi + 4] for pi, n in enumerate(names)}


def _token_tile(T, want):
    return want if T % want == 0 else T


def kernel(x, c, ada_w, ada_b, norm1_g, w_in, b_in, a_ln_g, a_ln_b, a_spatial_w, a_spatial_b, b_conv_w, b_conv_b, b_gn_g, b_gn_b, out_norm_a_g, out_norm_b_g, w_out, norm2_g, w_ffn_in, w_ffn_out, ada_f_w, ada_f_b, norm_f_g, loss_target, m_ada_w, m_ada_b, m_norm1_g, m_w_in, m_b_in, m_a_ln_g, m_a_ln_b, m_a_spatial_w, m_a_spatial_b, m_b_conv_w, m_b_conv_b, m_b_gn_g, m_b_gn_b, m_out_norm_a_g, m_out_norm_b_g, m_w_out, m_norm2_g, m_w_ffn_in, m_w_ffn_out, m_ada_f_w, m_ada_f_b, m_norm_f_g, v_ada_w, v_ada_b, v_norm1_g, v_w_in, v_b_in, v_a_ln_g, v_a_ln_b, v_a_spatial_w, v_a_spatial_b, v_b_conv_w, v_b_conv_b, v_b_gn_g, v_b_gn_b, v_out_norm_a_g, v_out_norm_b_g, v_w_out, v_norm2_g, v_w_ffn_in, v_w_ffn_out, v_ada_f_w, v_ada_f_b, v_norm_f_g):
    T = x.shape[1]
    idx = 4 * lax.axis_index("x") + 2 * lax.axis_index("y") + lax.axis_index("c")
    x2d = x.reshape(T, D)
    tgt = loss_target.reshape(T, D)

    conv_s = jnp.pad(b_conv_w[0], ((0, HALO - KW), (0, 0)))
    call, cparts, cfparts, convg, (win_g, wout_g) = _gather(
        c, ada_w[0], ada_b.reshape(NDEV, -1), ada_f_w, ada_f_b.reshape(NDEV, -1), conv_s,
        [w_in[0], w_out[0]])
    wout = wout_g.reshape(D, D)
    mod = jnp.concatenate([cparts.reshape(6, D), cfparts.reshape(2, D)], axis=0)
    cw = jnp.transpose(convg, (1, 0, 2)).reshape(HALO, DB)

    tril = jnp.tril(jnp.ones((CH, CH), dtype=bool))
    wsm = jnp.where(tril[None], a_spatial_w[0], 0.0).astype(BF16)
    wcat = wsm.reshape(NH * CH, CH)
    wcat_t = jnp.transpose(wsm, (0, 2, 1)).reshape(NH * CH, CH)
    bsf = jnp.repeat(a_spatial_b[0].T, DA // NH, axis=1)
    pm = jnp.asarray(_GROUP_MEAN, BF16)
    esel = jnp.asarray(_HEAD_SELECT, BF16)

    tm = _token_tile(T, 256)
    tk = _token_tile(T, 2048)
    (x1, hb, zvg, mixed, yb, o, gu, dgelu_u, dgelu_v, vhat, rslb, yhat, rsg), (wfi_g, wfo_g) = _mix_fwd(
        x2d, mod, norm1_g, win_g, b_in, a_ln_g, a_ln_b, wcat, bsf, cw, b_conv_b, b_gn_g, b_gn_b, out_norm_a_g,
        out_norm_b_g, wout, pm, [w_ffn_in[0].T.astype(BF16), w_ffn_out[0].astype(BF16)], _token_tile(T, 512))
    dx1, h2b, dgu, act, dxg, acc_f = _ffn(x1, tgt, mod, norm2_g, norm_f_g.reshape(1, D),
                                          wfi_g.reshape(2 * DFF, D), wfo_g.reshape(DFF, D), tm)
    g_wfi = _wgrad_rows(dgu, h2b, 2 * WFI_B, tk, "wgrad_ffn_in").reshape(NDEV, WFI_B, D)
    g_wfo = _wgrad_rows(act, dxg, 2 * WFI_B, tk, "wgrad_ffn_out").reshape(NDEV, DFF // NDEV, D)
    f_sems, f_thru, f_land, f_token = _rs_start([g_wfi, g_wfo], "rs_ffn_start")
    (gx, acc_v, acc_b, acc_a, acc_bs, acc_ws, acc_cw), (g_win, g_wout) = _mix_bwd(
        dx1, x2d, zvg, mixed, o, hb, yb, gu, dgelu_u, dgelu_v, vhat, rslb, yhat, rsg, mod, norm1_g, win_g, a_ln_g,
        a_ln_b, wcat, wcat_t, cw, b_gn_g, b_gn_b, out_norm_a_g, out_norm_b_g, wout, pm, esel, f_token, tm)
    (g_wfi_d, g_wfo_d), (r_wfi, r_wfo) = _rs_wait(f_sems, f_thru, f_land, [acc_v], "rs_ffn_wait")
    g_wout = g_wout.reshape(NDEV, D // NDEV, D)

    vsum, dcond_all, wssum, (q_win, q_wout) = _reduce_small(acc_f, acc_v, acc_b, acc_a, acc_bs, acc_cw, acc_ws,
                                                            g_wfi_d, [g_win, g_wout])
    sems, g_thru, land_thru, token = _rs_start([q_win, q_wout], "rs_mix_start", after=(vsum,), chips=True)

    own = lambda g: (g, jnp.reshape(idx, (1,)).astype(jnp.int32))
    res = {}
    res["w_ffn_in"] = tuple(a.T for a in _adam_big(r_wfi, w_ffn_in[0].T, m_w_ffn_in[0].T, v_w_ffn_in[0].T, WFI_B // 2,
                                                   "adam_w_ffn_in", own=own(g_wfi_d), after=token))
    res["w_ffn_out"] = _adam_big(r_wfo, w_ffn_out[0], m_w_ffn_out[0], v_w_ffn_out[0], DFF // NDEV // 2,
                                 "adam_w_ffn_out", own=own(g_wfo_d), after=token)
    cact_t = (call * jax.nn.sigmoid(call)).T
    dcond = dcond_all.reshape(NDEV, 8 * D)
    nada = ada_w.shape[2]
    nadf = ada_f_w.shape[1]
    dcs = lax.dynamic_slice(dcond, (0, idx * nada), (NDEV, nada))
    dcfs = lax.dynamic_slice(dcond, (0, 6 * D + idx * nadf), (NDEV, nadf))
    res["ada_w"] = _adam_ada(cact_t, dcs, ada_w[0], m_ada_w[0], v_ada_w[0], 512, "adam_ada_w")
    res["ada_f_w"] = _adam_ada(cact_t, dcfs, ada_f_w, m_ada_f_w, v_ada_f_w, 512, "adam_ada_f_w")
    ncw = b_conv_w.shape[2]
    gcw = jnp.concatenate([lax.dynamic_slice(vsum, (_CW_ROW, idx * ncw), (HALO // 2, ncw)),
                           lax.dynamic_slice(vsum, (_CW_ROW, DB + idx * ncw), (HALO // 2, ncw))], axis=0)[:KW]
    two = lambda a: a.reshape(1, -1) if a.ndim == 1 else a.reshape(-1, a.shape[-1])
    small_in = {
        "ada_b": (ada_b, m_ada_b, v_ada_b), "ada_f_b": (ada_f_b, m_ada_f_b, v_ada_f_b),
        "norm1_g": (norm1_g, m_norm1_g, v_norm1_g), "b_in": (b_in, m_b_in, v_b_in),
        "a_ln_g": (a_ln_g, m_a_ln_g, v_a_ln_g), "a_ln_b": (a_ln_b, m_a_ln_b, v_a_ln_b),
        "a_spatial_b": (a_spatial_b.reshape(1, D), m_a_spatial_b.reshape(1, D), v_a_spatial_b.reshape(1, D)),
        "b_conv_b": (b_conv_b, m_b_conv_b, v_b_conv_b), "b_gn_g": (b_gn_g, m_b_gn_g, v_b_gn_g),
        "b_gn_b": (b_gn_b, m_b_gn_b, v_b_gn_b), "out_norm_a_g": (out_norm_a_g, m_out_norm_a_g, v_out_norm_a_g),
        "out_norm_b_g": (out_norm_b_g, m_out_norm_b_g, v_out_norm_b_g),
        "norm2_g": (norm2_g, m_norm2_g, v_norm2_g), "norm_f_g": (norm_f_g, m_norm_f_g, v_norm_f_g),
        "a_spatial_w": (a_spatial_w, m_a_spatial_w, v_a_spatial_w),
        "b_conv_w": (b_conv_w[0], m_b_conv_w[0], v_b_conv_w[0]),
    }
    small_in = {n: tuple(two(a) for a in t) for n, t in small_in.items()}
    res.update(_adam_small(vsum, wssum, gcw, small_in))
    (q_win_d, q_wout_d), (r_win, r_wout) = _rs_wait(
        sems, g_thru, land_thru,
        [res["w_ffn_in"][0], res["w_ffn_out"][0], res["ada_w"][0], res["ada_f_w"][0], res["norm_f_g"][0]],
        "rs_mix_wait", chips=True)
    res["w_in"] = _adam_big(r_win, w_in[0], m_w_in[0], v_w_in[0], 512, "adam_w_in", own=own(g_win), sib=q_win_d)
    res["w_out"] = _adam_big(r_wout, w_out[0], m_w_out[0], v_w_out[0], D // NDEV, "adam_w_out", own=own(g_wout),
                             sib=q_wout_d)

    loss = 0.5 / D * jnp.sum(vsum[_LOSS_ROW])
    shapes = {"ada_w": ada_w, "ada_b": ada_b, "norm1_g": norm1_g, "w_in": w_in, "b_in": b_in, "a_ln_g": a_ln_g,
              "a_ln_b": a_ln_b, "a_spatial_w": a_spatial_w, "a_spatial_b": a_spatial_b, "b_conv_w": b_conv_w,
              "b_conv_b": b_conv_b, "b_gn_g": b_gn_g, "b_gn_b": b_gn_b, "out_norm_a_g": out_norm_a_g,
              "out_norm_b_g": out_norm_b_g, "w_out": w_out, "norm2_g": norm2_g, "w_ffn_in": w_ffn_in,
              "w_ffn_out": w_ffn_out, "ada_f_w": ada_f_w, "ada_f_b": ada_f_b, "norm_f_g": norm_f_g}
    order = list(shapes)
    outs = [loss, gx.reshape(x.shape)]
    for which in range(4):
        outs += [res[n][which].reshape(shapes[n].shape) for n in order]
    return tuple(outs)
```

```python
import math

import numpy as np

import jax
import jax.numpy as jnp
from jax import lax
from jax.experimental import pallas as pl
from jax.experimental.pallas import tpu as pltpu

F32 = jnp.float32
BF16 = jnp.bfloat16

D = 1024
DA = 512
DB = 512
DIN = 2048
DFF = 2816
NH = 8
CH = 128
KW = 31
HALO = 32
NDEV = 8
WIN_B = DIN // NDEV
WFI_B = 2 * DFF // NDEV
EPS = 1e-6
NVEC = 40
VMEM_LIMIT = 56 * 1024 * 1024

ADAM_LR, ADAM_B1, ADAM_B2, ADAM_EPS, ADAM_WD, ADAM_STEP = 0.001, 0.9, 0.999, 1e-08, 0.01, 10

MESH = pl.DeviceIdType.MESH

_LANE = np.arange(DB)
_GROUP_MEAN = np.where((_LANE[:, None] >> 6) == (_LANE[None, :] >> 6), 1.0 / 64.0, 0.0).astype(np.float32)
_HEAD_SELECT = np.where((_LANE[:, None] >> 6) == np.arange(CH)[None, :], 1.0, 0.0).astype(np.float32)


def _dot(a, b):
    return jnp.dot(a, b, preferred_element_type=F32)


def _dot_nt(a, b):
    return lax.dot_general(a, b, (((1,), (1,)), ((), ())), preferred_element_type=F32)


def _dot_tn(a, b):
    return lax.dot_general(a, b, (((0,), (0,)), ((), ())), preferred_element_type=F32)


def _rs(v):
    return lax.rsqrt(jnp.mean(v * v, axis=-1, keepdims=True) + EPS)


def _sig(v):
    return 1.0 / (1.0 + jnp.exp(-v))


_INV_SQRT2 = 1.0 / math.sqrt(2.0)
_INV_SQRT2PI = 1.0 / math.sqrt(2.0 * math.pi)


def _gelu_parts(v):
    cdf = 0.5 * (1.0 + lax.erf(v * _INV_SQRT2))
    pdf = jnp.exp(-0.5 * v * v) * _INV_SQRT2PI
    return v * cdf, cdf + v * pdf


def _grp_mean(v, pm):
    hi = v.astype(BF16)
    lo = (v - hi.astype(F32)).astype(BF16)
    return _dot(hi, pm) + _dot(lo, pm)


def _colsum(v):
    return jnp.sum(v, axis=0, keepdims=True)


def _full(shape):
    nd = len(shape)
    return pl.BlockSpec(shape, lambda *_: (0,) * nd)


def _resident(shape):
    nd = len(shape)
    return pl.BlockSpec(shape, lambda *_: (0,) * nd, pipeline_mode=pl.Buffered(1))


HBM = pl.BlockSpec(memory_space=pl.ANY)
VM = pl.BlockSpec(memory_space=pltpu.VMEM)


SH_ROWS = HALO - 8


def _shifted_copies(buf, shbuf, tm):
    for b in range(1, 8):
        shbuf[b - 1] = buf[b:b + tm + SH_ROWS, :]


def _window(buf, shbuf, off, tm):
    a, b = divmod(off, 8)
    if b == 0:
        return buf[8 * a:8 * a + tm, :]
    return shbuf[b - 1, 8 * a:8 * a + tm, :]


def _first_head_lanes():
    return lax.broadcasted_iota(jnp.int32, (CH, CH), 1) < (DA // NH)


def _mix_heads(w_ref, vb, first):
    outs = []
    for p in range(NH // 2):
        v = vb[:, p * CH:(p + 1) * CH]
        a = _dot(w_ref[(2 * p) * CH:(2 * p + 1) * CH, :], v)
        b = _dot(w_ref[(2 * p + 1) * CH:(2 * p + 2) * CH, :], v)
        outs.append(jnp.where(first, a, b))
    return jnp.concatenate(outs, axis=1)


def _place():
    x, y, c = lax.axis_index("x"), lax.axis_index("y"), lax.axis_index("c")
    return x, y, c, 4 * x + 2 * y + c


def _dev(t):
    return (t >> 2, (t >> 1) & 1, t & 1)


class _AllGather:
    def __init__(self, w_in, w_out, wss, wrs, lsem):
        x, y, c, idx = _place()
        me, sibling = (x, y, c), (x, y, 1 - c)
        chips = [(1 - x, y), (x, 1 - y), (1 - x, 1 - y)]
        nw = len(w_in)

        def blk(p):
            return 4 * p[0] + 2 * p[1] + p[2]

        def wcopy(a, k, block, to, src=None):
            dst = w_out[a].at[blk(block)]
            return pltpu.make_async_remote_copy(src_ref=dst if src is None else src, dst_ref=dst,
                                                send_sem=wss.at[a, k], recv_sem=wrs.at[a, k],
                                                device_id=to, device_id_type=MESH)

        self.mine = [pltpu.make_async_copy(w_in[a], w_out[a].at[idx], lsem.at[a]) for a in range(nw)]
        self.first = []
        for a in range(nw):
            self.first.append(wcopy(a, 0, me, sibling, src=w_in[a]))
            self.first += [wcopy(a, 1 + j, me, (*chip, c), src=w_in[a]) for j, chip in enumerate(chips)]
        self.landed = [[wcopy(a, 1 + j, (*chip, c), me) for a in range(nw)] for j, chip in enumerate(chips)]
        self.passed = [[wcopy(a, 4 + j, (*chip, c), sibling) for a in range(nw)] for j, chip in enumerate(chips)]
        self.from_sibling = []
        for a in range(nw):
            self.from_sibling.append(wcopy(a, 0, sibling, me))
            self.from_sibling += [wcopy(a, 4 + j, (*chip, 1 - c), me) for j, chip in enumerate(chips)]

    def start(self):
        for cp in self.mine + self.first:
            cp.start()

    def forward(self):
        for land, pas in zip(self.landed, self.passed):
            for l, p in zip(land, pas):
                l.wait_recv()
                p.start()

    def finish(self):
        for cp in self.from_sibling:
            cp.wait_recv()
        for cp in self.first:
            cp.wait_send()
        for pas in self.passed:
            for p in pas:
                p.wait_send()
        for cp in self.mine:
            cp.wait()


AG_SEMS = lambda nw: [pltpu.SemaphoreType.DMA((nw, 7)), pltpu.SemaphoreType.DMA((nw, 7)),
                      pltpu.SemaphoreType.DMA((nw,))]


def _mix_fwd(x, mod, g1, win, b_in, lng, lnb, wcat, bsf, cw, cb, gng, gnb, oga, ogb, wout, pm, ffn_shards, tm):
    T = x.shape[0]
    nt = T // tm
    nch = tm // CH
    nw = len(ffn_shards)
    fwd_step = (5 * nt) // 8
    saved = [(D, F32), (D, BF16), (2 * DB, F32), (DA, F32), (D, BF16), (D, F32), (DA, F32), (DA, F32), (DA, F32),
             (DA, F32), (CH, F32), (DB, F32), (DB, F32)]
    NSAVE = len(saved)

    def body(x_ref, mod_ref, g1_ref, win_ref, bin_ref, lng_ref, lnb_ref, wcat_ref, bsf_ref, cw_ref, cb_ref,
             gng_ref, gnb_ref, oga_ref, ogb_ref, wout_ref, pm_ref, *rest):
        sh_f32 = rest[:nw]
        (x1_ref, h_ref, zvg_ref, mixed_ref, y_ref, o_ref, gu_ref, dgu_ref, dgv_ref, vhat_ref, rsl_ref, yhat_ref,
         rsg_ref) = rest[nw:nw + NSAVE]
        sh_out = rest[nw + NSAVE:2 * nw + NSAVE]
        glbuf, shbuf = rest[2 * nw + NSAVE:2 * nw + NSAVE + 2]
        sh_in = rest[2 * nw + NSAVE + 2:3 * nw + NSAVE + 2]
        wss, wrs, lsem = rest[3 * nw + NSAVE + 2:]
        i = pl.program_id(0)

        @pl.when(i == 0)
        def _():
            for a in range(nw):
                sh_in[a][...] = sh_f32[a][...].astype(BF16)
            _AllGather(sh_in, sh_out, wss, wrs, lsem).start()

        xv = x_ref[...]
        shift1 = mod_ref[0:1, :]
        scale1 = mod_ref[1:2, :]
        gate1 = mod_ref[2:3, :]
        h = (xv * _rs(xv) * g1_ref[...]) * (1.0 + scale1) + shift1
        hb = h.astype(BF16)
        h_ref[...] = hb
        z = jnp.concatenate([_dot(hb, win_ref[j]) for j in range(NDEV)], axis=1) + bin_ref[...]
        zvg_ref[...] = z[:, 2 * DA:]
        gu, dgelu_u = _gelu_parts(z[:, 0:DA])
        gv, dgelu_v = _gelu_parts(z[:, DA:2 * DA])
        gu_ref[...] = gu
        dgu_ref[...] = dgelu_u
        dgv_ref[...] = dgelu_v
        xc = gv - jnp.mean(gv, axis=-1, keepdims=True)
        rsl = lax.rsqrt(jnp.mean(xc * xc, axis=-1, keepdims=True) + EPS)
        vhat = xc * rsl
        vhat_ref[...] = vhat
        rsl_ref[...] = jnp.broadcast_to(rsl, (tm, CH))
        vnb = (vhat * lng_ref[...] + lnb_ref[...]).astype(BF16)
        first = _first_head_lanes()
        chunks = []
        for ci in range(nch):
            chunks.append(_mix_heads(wcat_ref, vnb[ci * CH:(ci + 1) * CH, :], first) + bsf_ref[...])
        mixed = jnp.concatenate(chunks, axis=0) if nch > 1 else chunks[0]
        mixed_ref[...] = mixed
        ya = gu * mixed
        gl = z[:, 2 * DA:2 * DA + DB] * _sig(z[:, 2 * DA + DB:])

        @pl.when(i == 0)
        def _():
            glbuf[0:HALO, :] = jnp.zeros((HALO, DB), F32)

        glbuf[HALO:HALO + tm, :] = gl
        _shifted_copies(glbuf, shbuf, tm)
        yc = jnp.zeros((tm, DB), F32) + cb_ref[...]
        for k in range(KW):
            yc = yc + cw_ref[k:k + 1, :] * _window(glbuf, shbuf, HALO - (KW - 1) + k, tm)
        glbuf[0:HALO, :] = gl[tm - HALO:, :]
        pmv = pm_ref[...]
        dc = yc - _grp_mean(yc, pmv)
        rsg = lax.rsqrt(_grp_mean(dc * dc, pmv) + EPS)
        yhat = dc * rsg
        yhat_ref[...] = yhat
        rsg_ref[...] = rsg
        yg = yhat * gng_ref[...] + gnb_ref[...]
        yb = yg * _sig(yg)
        na = ya * _rs(ya) * oga_ref[...]
        nb = yb * _rs(yb) * ogb_ref[...]
        yv = jnp.concatenate([na, nb], axis=1).astype(BF16)
        y_ref[...] = yv
        o = _dot(yv, wout_ref[...])
        o_ref[...] = o
        x1_ref[...] = xv + gate1 * o

        @pl.when(i == fwd_step)
        def _():
            _AllGather(sh_in, sh_out, wss, wrs, lsem).forward()

        @pl.when(i == nt - 1)
        def _():
            _AllGather(sh_in, sh_out, wss, wrs, lsem).finish()

    tile = lambda w: pl.BlockSpec((tm, w), lambda i: (i, 0))
    outs = pl.pallas_call(
        body,
        name="mix_fwd",
        grid=(nt,),
        in_specs=[tile(D), _full((8, D)), _full((1, D)), _resident((NDEV, D, WIN_B)), _full((1, DIN)),
                  _full((1, DA)), _full((1, DA)), _full((NH * CH, CH)), _full((CH, DA)), _full((HALO, DB)),
                  _full((1, DB)), _full((1, DB)), _full((1, DB)), _full((1, DA)), _full((1, DB)),
                  _resident((D, D)), _full((DB, DB))] + [_resident(s.shape) for s in ffn_shards],
        out_specs=[tile(w) for w, _ in saved] + [HBM] * nw,
        out_shape=[jax.ShapeDtypeStruct((T, w), dt) for w, dt in saved]
                  + [jax.ShapeDtypeStruct((NDEV,) + s.shape, BF16) for s in ffn_shards],
        scratch_shapes=[pltpu.VMEM((HALO + tm, DB), F32), pltpu.VMEM((7, tm + SH_ROWS, DB), F32)]
                       + [pltpu.VMEM(s.shape, BF16) for s in ffn_shards] + AG_SEMS(nw),
        compiler_params=pltpu.CompilerParams(dimension_semantics=("arbitrary",), vmem_limit_bytes=VMEM_LIMIT),
    )(x, mod, g1, win, b_in, lng, lnb, wcat, bsf, cw, cb, gng, gnb, oga, ogb, wout, pm, *ffn_shards)
    return outs[:NSAVE], outs[NSAVE:]


FF_BLOCKS = ((0, 1024), (1024, 1024), (2048, 768))


def _ffn(x1, tgt, mod, g2, gf, wfi_t, wfo, tm):
    T = x1.shape[0]
    nt = T // tm

    def body(x1_ref, tgt_ref, mod_ref, g2_ref, gf_ref, wfi_ref, wfo_ref,
             dx1_ref, h2_ref, dgu_ref, act_ref, dxg_ref, acc_ref, g_s, u_s):
        i = pl.program_id(0)

        @pl.when(i == 0)
        def _():
            acc_ref[...] = jnp.zeros((8, D), F32)

        x1 = x1_ref[...]
        shift2 = mod_ref[3:4, :]
        scale2 = mod_ref[4:5, :]
        gate2 = mod_ref[5:6, :]
        shiftf = mod_ref[6:7, :]
        scalef = mod_ref[7:8, :]
        g2v = g2_ref[...]
        gfv = gf_ref[...]
        r2 = _rs(x1)
        xn2 = x1 * r2
        h2b = (xn2 * g2v * (1.0 + scale2) + shift2).astype(BF16)
        h2_ref[...] = h2b
        f = jnp.zeros((tm, D), F32)
        for o, w in FF_BLOCKS:
            g = _dot_nt(h2b, wfi_ref[o:o + w, :])
            u = _dot_nt(h2b, wfi_ref[DFF + o:DFF + o + w, :])
            g_s[:, o:o + w] = g
            u_s[:, o:o + w] = u
            actb = (g * _sig(g) * u).astype(BF16)
            act_ref[:, o:o + w] = actb
            f = f + _dot(actb, wfo_ref[o:o + w, :])
        x2 = x1 + gate2 * f
        rf = _rs(x2)
        xnf = x2 * rf
        out = xnf * gfv * (1.0 + scalef) + shiftf
        e = out - tgt_ref[...]
        dout = e * (1.0 / D)
        acc_ref[7:8, :] += _colsum(e * e)
        acc_ref[0:1, :] += _colsum(dout)
        acc_ref[1:2, :] += _colsum(dout * xnf * gfv)
        acc_ref[2:3, :] += _colsum(dout * (1.0 + scalef) * xnf)
        dxnf = dout * (1.0 + scalef) * gfv
        dx2 = rf * (dxnf - xnf * jnp.mean(dxnf * xnf, axis=-1, keepdims=True))
        acc_ref[3:4, :] += _colsum(dx2 * f)
        dxgb = (dx2 * gate2).astype(BF16)
        dxg_ref[...] = dxgb
        dh2 = jnp.zeros((tm, D), F32)
        for o, w in FF_BLOCKS:
            dact = _dot_nt(dxgb, wfo_ref[o:o + w, :])
            g = g_s[:, o:o + w]
            u = u_s[:, o:o + w]
            s = _sig(g)
            dgb = (dact * u * (s * (1.0 + g * (1.0 - s)))).astype(BF16)
            dub = (dact * (g * s)).astype(BF16)
            dgu_ref[:, o:o + w] = dgb
            dgu_ref[:, DFF + o:DFF + o + w] = dub
            dh2 = dh2 + _dot(dgb, wfi_ref[o:o + w, :])
            dh2 = dh2 + _dot(dub, wfi_ref[DFF + o:DFF + o + w, :])
        acc_ref[4:5, :] += _colsum(dh2)
        acc_ref[5:6, :] += _colsum(dh2 * xn2 * g2v)
        acc_ref[6:7, :] += _colsum(dh2 * (1.0 + scale2) * xn2)
        dxn2 = dh2 * (1.0 + scale2) * g2v
        dx1_ref[...] = dx2 + r2 * (dxn2 - xn2 * jnp.mean(dxn2 * xn2, axis=-1, keepdims=True))

    tile = lambda w: pl.BlockSpec((tm, w), lambda i: (i, 0))
    return pl.pallas_call(
        body,
        name="ffn_fwd_bwd",
        grid=(nt,),
        in_specs=[tile(D), tile(D), _full((8, D)), _full((1, D)), _full((1, D)),
                  _resident((2 * DFF, D)), _resident((DFF, D))],
        out_specs=[tile(D), tile(D), tile(2 * DFF), tile(DFF), tile(D), _full((8, D))],
        out_shape=[jax.ShapeDtypeStruct((T, D), F32), jax.ShapeDtypeStruct((T, D), BF16),
                   jax.ShapeDtypeStruct((T, 2 * DFF), BF16), jax.ShapeDtypeStruct((T, DFF), BF16),
                   jax.ShapeDtypeStruct((T, D), BF16), jax.ShapeDtypeStruct((8, D), F32)],
        scratch_shapes=[pltpu.VMEM((tm, DFF), F32), pltpu.VMEM((tm, DFF), F32)],
        compiler_params=pltpu.CompilerParams(dimension_semantics=("arbitrary",), vmem_limit_bytes=VMEM_LIMIT),
    )(x1, tgt, mod, g2, gf, wfi_t, wfo)


def _mix_bwd(dx1, x, zvg, mixed, o, hb, yb, gu, dgu, dgv, vhat, rslb, yhat, rsg, mod, g1, win, lng, lnb, wcat, wcat_t,
             cw, gng, gnb, oga, ogb, wout, pm, esel, after, tm):
    T = x.shape[0]
    nt = T // tm
    nch = tm // CH
    WOB = 256

    def body(dx1_ref, x_ref, zvg_ref, mixed_ref, o_ref, hb_ref, yb_ref, gu_ref, dgu_ref, dgv_ref, vhat_ref, rsl_ref,
             yhat_ref, rsg_ref, mod_ref, g1_ref, win_ref, lng_ref, lnb_ref, wcat_ref, wcatt_ref, cw_ref, gng_ref,
             gnb_ref, oga_ref, ogb_ref, wout_ref, pm_ref, esel_ref, after_ref,
             gx_ref, accv_ref, accb_ref, acca_ref, accbs_ref, accws_ref, acccw_ref, gwin_ref, gwout_ref,
             dycbuf, shbuf, bs_s, acc_win, acc_wout, st_win, st_wout):
        i = pl.program_id(0)

        @pl.when(i == 0)
        def _():
            acc_win[...] = jnp.zeros((NDEV, D, WIN_B), F32)
            acc_wout[...] = jnp.zeros((D, D), F32)
            accv_ref[...] = jnp.zeros((8, D), F32)
            accb_ref[...] = jnp.zeros((1, DIN), F32)
            acca_ref[...] = jnp.zeros((8, DA), F32)
            accws_ref[...] = jnp.zeros((NH * CH, CH), F32)
            acccw_ref[...] = jnp.zeros((HALO, DB), F32)
            bs_s[...] = jnp.zeros((CH, DA), F32)
            dycbuf[tm:tm + HALO, :] = jnp.zeros((HALO, DB), F32)

        shift1 = mod_ref[0:1, :]
        scale1 = mod_ref[1:2, :]
        gate1 = mod_ref[2:3, :]
        g1v = g1_ref[...]
        xv = x_ref[...]
        r1 = _rs(xv)
        xn1 = xv * r1
        val = zvg_ref[:, 0:DB]
        gate = zvg_ref[:, DB:]
        gu = gu_ref[...]
        dgelu_u = dgu_ref[...]
        dgelu_v = dgv_ref[...]
        vhat = vhat_ref[...]
        rsl = rsl_ref[:, 0:1]
        lngv = lng_ref[...]
        vnb = (vhat * lngv + lnb_ref[...]).astype(BF16)
        mixed = mixed_ref[...]
        ya = gu * mixed
        ra = _rs(ya)
        yan = ya * ra
        sgt = _sig(gate)
        gl = val * sgt
        pmv = pm_ref[...]
        rsg = rsg_ref[...]
        yhat = yhat_ref[...]
        gngv = gng_ref[...]
        yg = yhat * gngv + gnb_ref[...]
        sgy = _sig(yg)
        yb = yg * sgy
        rb = _rs(yb)
        ybn = yb * rb
        dx1 = dx1_ref[...]
        accv_ref[0:1, :] += _colsum(dx1 * o_ref[...])
        dogb = (dx1 * gate1).astype(BF16)
        acc_wout[...] += _dot_tn(yb_ref[...], dogb)
        dy = _dot_nt(dogb, wout_ref[...])
        dna = dy[:, 0:DA]
        dnb = dy[:, DA:]
        ogav = oga_ref[...]
        ogbv = ogb_ref[...]
        acca_ref[2:3, :] += _colsum(dna * yan)
        acca_ref[3:4, :] += _colsum(dnb * ybn)
        ta = dna * ogav
        dya = ra * (ta - yan * jnp.mean(ta * yan, axis=-1, keepdims=True))
        tb = dnb * ogbv
        dyb = rb * (tb - ybn * jnp.mean(tb * ybn, axis=-1, keepdims=True))
        dgu = dya * mixed
        dm = dya * gu
        first = _first_head_lanes()
        zero = jnp.zeros((CH, CH), BF16)
        dvn_chunks = []
        bs_acc = bs_s[...]
        for ci in range(nch):
            dmc = dm[ci * CH:(ci + 1) * CH, :]
            bs_acc = bs_acc + dmc
            dmcb = dmc.astype(BF16)
            dvn_chunks.append(_mix_heads(wcatt_ref, dmcb, first))
            vc = vnb[ci * CH:(ci + 1) * CH, :]
            for p in range(NH // 2):
                xt = dmcb[:, p * CH:(p + 1) * CH]
                vt = vc[:, p * CH:(p + 1) * CH]
                accws_ref[(2 * p) * CH:(2 * p + 1) * CH, :] += _dot_nt(jnp.where(first, xt, zero), vt)
                accws_ref[(2 * p + 1) * CH:(2 * p + 2) * CH, :] += _dot_nt(jnp.where(first, zero, xt), vt)
        bs_s[...] = bs_acc
        dvn = jnp.concatenate(dvn_chunks, axis=0) if nch > 1 else dvn_chunks[0]
        acca_ref[0:1, :] += _colsum(dvn * vhat)
        acca_ref[1:2, :] += _colsum(dvn)
        dvh = dvn * lngv
        dgv = rsl * (dvh - jnp.mean(dvh, axis=-1, keepdims=True)
                     - vhat * jnp.mean(dvh * vhat, axis=-1, keepdims=True))
        du = dgu * dgelu_u
        dv = dgv * dgelu_v
        dyg = dyb * (sgy * (1.0 + yg * (1.0 - sgy)))
        acca_ref[5:6, :] += _colsum(dyg * yhat)
        acca_ref[6:7, :] += _colsum(dyg)
        dyh = dyg * gngv
        dyc = rsg * (dyh - _grp_mean(dyh, pmv) - yhat * _grp_mean(dyh * yhat, pmv))
        acca_ref[4:5, :] += _colsum(dyc)
        dycbuf[0:tm, :] = dyc
        _shifted_copies(dycbuf, shbuf, tm)
        dgl = jnp.zeros((tm, DB), F32)
        for k in range(KW):
            win_k = _window(dycbuf, shbuf, KW - 1 - k, tm)
            dgl = dgl + cw_ref[k:k + 1, :] * win_k
            acccw_ref[k:k + 1, :] += _colsum(win_k * gl)
        dycbuf[tm:tm + HALO, :] = dyc[0:HALO, :]
        dval = dgl * sgt
        dgate = dgl * val * sgt * (1.0 - sgt)
        dz = jnp.concatenate([du, dv, dval, dgate], axis=1)
        accb_ref[...] += _colsum(dz)
        dzb = dz.astype(BF16)
        hbv = hb_ref[...]
        dh = jnp.zeros((tm, D), F32)
        for j in range(NDEV):
            dzj = dzb[:, j * WIN_B:(j + 1) * WIN_B]
            acc_win[j] += _dot_tn(hbv, dzj)
            dh = dh + _dot_nt(dzj, win_ref[j])
        accv_ref[1:2, :] += _colsum(dh)
        dh_xn = _colsum(dh * xn1)
        accv_ref[2:3, :] += dh_xn * g1v
        accv_ref[3:4, :] += dh_xn * (1.0 + scale1)
        dxn1 = dh * (1.0 + scale1) * g1v
        gx_ref[...] = dx1 + r1 * (dxn1 - xn1 * jnp.mean(dxn1 * xn1, axis=-1, keepdims=True))

        @pl.when(i == nt - 1)
        def _():
            rows = lax.broadcasted_iota(jnp.int32, (NH * CH, CH), 0) & (CH - 1)
            cols = lax.broadcasted_iota(jnp.int32, (NH * CH, CH), 1)
            accws_ref[...] = jnp.where(cols <= rows, accws_ref[...], 0.0)
            bs = bs_s[...]
            hi = bs.astype(BF16)
            r1_ = bs - hi.astype(F32)
            mid = r1_.astype(BF16)
            lo = (r1_ - mid.astype(F32)).astype(BF16)
            ev = esel_ref[...]
            accbs_ref[...] = _dot(hi, ev) + _dot(mid, ev) + _dot(lo, ev)
            for j in range(NDEV):
                st_win[...] = acc_win[j].astype(BF16)
                pltpu.sync_copy(st_win, gwin_ref.at[j])
            for j in range(D // WOB):
                st_wout[...] = acc_wout[j * WOB:(j + 1) * WOB, :].astype(BF16)
                pltpu.sync_copy(st_wout, gwout_ref.at[pl.ds(j * WOB, WOB)])

    rev = lambda w: pl.BlockSpec((tm, w), lambda i: (nt - 1 - i, 0))
    outs = pl.pallas_call(
        body,
        name="mix_bwd",
        grid=(nt,),
        in_specs=[rev(D), rev(D), rev(2 * DB), rev(DA), rev(D), rev(D), rev(D), rev(DA), rev(DA), rev(DA), rev(DA),
                  rev(CH), rev(DB), rev(DB), _full((8, D)), _full((1, D)),
                  _resident((NDEV, D, WIN_B)), _full((1, DA)), _full((1, DA)), _full((NH * CH, CH)),
                  _full((NH * CH, CH)), _full((HALO, DB)), _full((1, DB)), _full((1, DB)), _full((1, DA)),
                  _full((1, DB)), _resident((D, D)), _full((DB, DB)), _full((DA, CH)), HBM],
        out_specs=[rev(D), _full((8, D)), _full((1, DIN)), _full((8, DA)), _full((CH, CH)),
                   _full((NH * CH, CH)), _full((HALO, DB)), HBM, HBM],
        out_shape=[jax.ShapeDtypeStruct((T, D), F32), jax.ShapeDtypeStruct((8, D), F32),
                   jax.ShapeDtypeStruct((1, DIN), F32), jax.ShapeDtypeStruct((8, DA), F32),
                   jax.ShapeDtypeStruct((CH, CH), F32), jax.ShapeDtypeStruct((NH * CH, CH), F32),
                   jax.ShapeDtypeStruct((HALO, DB), F32),
                   jax.ShapeDtypeStruct((NDEV, D, WIN_B), BF16), jax.ShapeDtypeStruct((D, D), BF16)],
        scratch_shapes=[pltpu.VMEM((tm + HALO, DB), F32), pltpu.VMEM((7, tm + SH_ROWS, DB), F32),
                        pltpu.VMEM((CH, DA), F32), pltpu.VMEM((NDEV, D, WIN_B), F32), pltpu.VMEM((D, D), F32),
                        pltpu.VMEM((D, WIN_B), BF16), pltpu.VMEM((WOB, D), BF16)],
        compiler_params=pltpu.CompilerParams(dimension_semantics=("arbitrary",), vmem_limit_bytes=VMEM_LIMIT),
    )(dx1, x, zvg, mixed, o, hb, yb, gu, dgu, dgv, vhat, rslb, yhat, rsg, mod, g1, win, lng, lnb, wcat, wcat_t, cw,
      gng, gnb, oga, ogb, wout, pm, esel, after)
    return outs[:7], outs[7:]


def _wgrad_rows(a, b, bm, tk, name):
    T, M = a.shape
    N = b.shape[1]
    nk = T // tk

    def body(a_ref, b_ref, o_ref, acc):
        k = pl.program_id(1)

        @pl.when(k == 0)
        def _():
            acc[...] = jnp.zeros((bm, N), F32)

        acc[...] += _dot_tn(a_ref[...], b_ref[...])

        @pl.when(k == nk - 1)
        def _():
            o_ref[...] = acc[...].astype(BF16)

    return pl.pallas_call(
        body, name=name, grid=(M // bm, nk),
        in_specs=[pl.BlockSpec((tk, bm), lambda j, k: (k, j)), pl.BlockSpec((tk, N), lambda j, k: (k, 0))],
        out_specs=pl.BlockSpec((bm, N), lambda j, k: (j, 0)),
        out_shape=jax.ShapeDtypeStruct((M, N), BF16),
        scratch_shapes=[pltpu.VMEM((bm, N), F32)],
        compiler_params=pltpu.CompilerParams(dimension_semantics=("arbitrary", "arbitrary"),
                                             vmem_limit_bytes=VMEM_LIMIT),
    )(a, b)


def _small_copy(src, dst, ss, rs, k, to):
    return pltpu.make_async_remote_copy(src_ref=src, dst_ref=dst, send_sem=ss.at[k], recv_sem=rs.at[k],
                                        device_id=to, device_id_type=MESH)


def _gather(c_row, ada_w, ada_b8, ada_f_w, ada_f_b8, conv_s, shards):
    nw = len(shards)

    def body(c_ref, adaw_ref, adab_ref, adafw_ref, adafb_ref, conv_ref, *rest):
        w_f32 = rest[:nw]
        call_ref, cparts_ref, cfparts_ref, convg_ref = rest[nw:nw + 4]
        w_out = rest[nw + 4:2 * nw + 4]
        part_s, partf_s = rest[2 * nw + 4:2 * nw + 6]
        w_in = rest[2 * nw + 6:3 * nw + 6]
        wss, wrs, lsem, s1, r1, s2, r2, s3, r3, s4, r4 = rest[3 * nw + 6:]
        x, y, c, idx = _place()
        me = (x, y, c)
        for a in range(nw):
            w_in[a][...] = w_f32[a][...].astype(BF16)
        ag = _AllGather(w_in, w_out, wss, wrs, lsem)
        ag.start()
        call_ref[pl.ds(idx, 1), :] = c_ref[...]
        convg_ref[idx] = conv_ref[...]
        ph1 = []
        for k in range(1, NDEV):
            to = _dev(idx ^ k)
            ph1.append(_small_copy(c_ref, call_ref.at[pl.ds(idx, 1)], s1, r1, k - 1, to))
            ph1.append(_small_copy(conv_ref, convg_ref.at[idx], s2, r2, k - 1, to))
        for cp in ph1:
            cp.start()
        for k in range(1, NDEV):
            src_dev = idx ^ k
            _small_copy(c_ref, call_ref.at[pl.ds(src_dev, 1)], s1, r1, k - 1, me).wait_recv()
            _small_copy(conv_ref, convg_ref.at[src_dev], s2, r2, k - 1, me).wait_recv()
        call = call_ref[...]
        cact = (call * _sig(call))
        part_s[...] = jnp.dot(cact, adaw_ref[...], preferred_element_type=F32,
                              precision=lax.Precision.HIGHEST) + adab_ref[pl.ds(idx, 1), :]
        partf_s[...] = jnp.dot(cact, adafw_ref[...], preferred_element_type=F32,
                               precision=lax.Precision.HIGHEST) + adafb_ref[pl.ds(idx, 1), :]
        cparts_ref[pl.ds(idx, 1), :] = part_s[pl.ds(idx, 1), :]
        cfparts_ref[pl.ds(idx, 1), :] = partf_s[pl.ds(idx, 1), :]
        ph2 = []
        for k in range(1, NDEV):
            t = idx ^ k
            ph2.append(_small_copy(part_s.at[pl.ds(t, 1)], cparts_ref.at[pl.ds(idx, 1)], s3, r3, k - 1, _dev(t)))
            ph2.append(_small_copy(partf_s.at[pl.ds(t, 1)], cfparts_ref.at[pl.ds(idx, 1)], s4, r4, k - 1, _dev(t)))
        for cp in ph2:
            cp.start()
        for k in range(1, NDEV):
            src_dev = idx ^ k
            _small_copy(part_s.at[pl.ds(0, 1)], cparts_ref.at[pl.ds(src_dev, 1)], s3, r3, k - 1, me).wait_recv()
            _small_copy(partf_s.at[pl.ds(0, 1)], cfparts_ref.at[pl.ds(src_dev, 1)], s4, r4, k - 1, me).wait_recv()
        for cp in ph1 + ph2:
            cp.wait_send()
        ag.forward()
        ag.finish()

    dma7 = pltpu.SemaphoreType.DMA((NDEV - 1,))
    outs = pl.pallas_call(
        body,
        name="gather_weights",
        in_specs=[VM] * (6 + nw),
        out_specs=[VM] * 4 + [HBM] * nw,
        out_shape=[jax.ShapeDtypeStruct((NDEV, D), F32), jax.ShapeDtypeStruct((NDEV, ada_w.shape[1]), F32),
                   jax.ShapeDtypeStruct((NDEV, ada_f_w.shape[1]), F32),
                   jax.ShapeDtypeStruct((NDEV,) + conv_s.shape, F32)]
                  + [jax.ShapeDtypeStruct((NDEV,) + s.shape, BF16) for s in shards],
        scratch_shapes=[pltpu.VMEM((NDEV, ada_w.shape[1]), F32), pltpu.VMEM((NDEV, ada_f_w.shape[1]), F32)]
                       + [pltpu.VMEM(s.shape, BF16) for s in shards] + AG_SEMS(nw) + [dma7] * 8,
        compiler_params=pltpu.CompilerParams(vmem_limit_bytes=VMEM_LIMIT),
    )(c_row, ada_w, ada_b8, ada_f_w, ada_f_b8, conv_s, *shards)
    return outs[0], outs[1], outs[2], outs[3], outs[4:]


_VEC_AT = {
    "norm1_g": (8, 0, D), "a_ln_g": (11, 0, DA), "a_ln_b": (11, DA, DA), "a_spatial_b": (12, 0, D),
    "b_conv_b": (13, 0, DB), "b_gn_g": (13, DB, DB), "b_gn_b": (14, 0, DB), "out_norm_a_g": (14, DB, DA),
    "out_norm_b_g": (15, 0, DB), "norm2_g": (16, 0, D), "norm_f_g": (17, 0, D),
}
_LOSS_ROW = 18
_CW_ROW = 24


def _reduce_small(acc_f, acc_v, acc_b, acc_a, acc_bs, acc_cw, dws, after, pair_grads):
    npg = len(pair_grads)

    def body(accf_ref, accv_ref, accb_ref, acca_ref, accbs_ref, acccw_ref, dws_ref, after_ref, *rest):
        pg = rest[:npg]
        vsum_ref, dcond_ref, wssum_ref = rest[npg:npg + 3]
        pq = rest[npg + 3:2 * npg + 3]
        vloc, vbuf, wbuf, wown, s1, r1, s2, r2, s3, r3 = rest[2 * npg + 3:2 * npg + 13]
        pland = rest[2 * npg + 13:3 * npg + 13]
        pstage = rest[3 * npg + 13:4 * npg + 13]
        ps, pr, pls, pss = rest[4 * npg + 13:]
        x, y, c, idx = _place()
        me = (x, y, c)
        sibling = (x, y, 1 - c)
        chips = [(1 - x, y), (x, 1 - y), (1 - x, 1 - y)]
        blk = lambda p: 4 * p[0] + 2 * p[1] + p[2]
        give = [blk((*ch, 1 - c)) for ch in chips] + [blk(sibling)]
        pair = [pltpu.make_async_remote_copy(src_ref=pg[a].at[b], dst_ref=pland[a].at[j], send_sem=ps.at[a, j],
                                             recv_sem=pr.at[a, j], device_id=sibling, device_id_type=MESH)
                for a in range(npg) for j, b in enumerate(give)]
        loads = [pltpu.make_async_copy(pg[a].at[blk((*ch, c))], pstage[a].at[j], pls.at[a, j])
                 for a in range(npg) for j, ch in enumerate(chips)]
        for cp in pair + loads:
            cp.start()
        vloc[...] = jnp.zeros((NVEC, D), F32)
        vloc[0:1, :] = accv_ref[1:2, :]
        vloc[1:2, :] = accv_ref[2:3, :]
        vloc[2:3, :] = accv_ref[0:1, :]
        vloc[3:4, :] = accf_ref[4:5, :]
        vloc[4:5, :] = accf_ref[5:6, :]
        vloc[5:6, :] = accf_ref[3:4, :]
        vloc[6:7, :] = accf_ref[0:1, :]
        vloc[7:8, :] = accf_ref[1:2, :]
        vloc[8:9, :] = accv_ref[3:4, :]
        vloc[9:10, :] = accb_ref[:, 0:D]
        vloc[10:11, :] = accb_ref[:, D:]
        vloc[11:12, 0:DA] = acca_ref[0:1, :]
        vloc[11:12, DA:] = acca_ref[1:2, :]
        bst = accbs_ref[...].T
        for h in range(NH):
            vloc[12:13, h * CH:(h + 1) * CH] = bst[h:h + 1, :]
        vloc[13:14, 0:DB] = acca_ref[4:5, :]
        vloc[13:14, DB:] = acca_ref[5:6, :]
        vloc[14:15, 0:DB] = acca_ref[6:7, :]
        vloc[14:15, DB:] = acca_ref[2:3, :]
        vloc[15:16, 0:DB] = acca_ref[3:4, :]
        vloc[16:17, :] = accf_ref[6:7, :]
        vloc[17:18, :] = accf_ref[2:3, :]
        vloc[_LOSS_ROW:_LOSS_ROW + 1, :] = accf_ref[7:8, :]
        vloc[_CW_ROW:_CW_ROW + HALO // 2, 0:DB] = acccw_ref[0:HALO // 2, :]
        vloc[_CW_ROW:_CW_ROW + HALO // 2, DB:] = acccw_ref[HALO // 2:, :]
        vbuf[idx] = vloc[...]
        rows_of = lambda t: pl.ds(pl.multiple_of(t * CH, CH), CH)
        wbuf[0] = dws_ref[rows_of(idx), :]
        sm = []
        for k in range(1, NDEV):
            t = idx ^ k
            sm.append(_small_copy(vloc, vbuf.at[idx], s1, r1, k - 1, _dev(t)))
            sm.append(_small_copy(dws_ref.at[rows_of(t)], wbuf.at[k], s2, r2, k - 1, _dev(t)))
        for cp in sm:
            cp.start()
        for k in range(1, NDEV):
            _small_copy(dws_ref.at[rows_of(0)], wbuf.at[k], s2, r2, k - 1, me).wait_recv()
        ws = wbuf[0]
        for k in range(1, NDEV):
            ws = ws + wbuf[k]
        wown[...] = ws
        wssum_ref[rows_of(idx), :] = ws
        ag = [_small_copy(wown, wssum_ref.at[rows_of(idx)], s3, r3, k - 1, _dev(idx ^ k)) for k in range(1, NDEV)]
        for cp in ag:
            cp.start()
        for cp in loads:
            cp.wait()
        for cp in pair:
            cp.wait_recv()
        stores = []
        for a in range(npg):
            for j in range(3):
                pstage[a][j] = (pstage[a][j].astype(F32) + pland[a][j].astype(F32)).astype(BF16)
                stores.append(pltpu.make_async_copy(pstage[a].at[j], pq[a].at[j], pss.at[a, j]))
            stores.append(pltpu.make_async_copy(pland[a].at[3], pq[a].at[3], pss.at[a, 3]))
        for cp in stores:
            cp.start()
        for k in range(1, NDEV):
            _small_copy(vloc, vbuf.at[idx ^ k], s1, r1, k - 1, me).wait_recv()
        vs = vbuf[0]
        for d in range(1, NDEV):
            vs = vs + vbuf[d]
        vsum_ref[...] = vs
        for d in range(NDEV):
            dcond_ref[d] = vbuf[d, 0:8, :]
        for k in range(1, NDEV):
            _small_copy(wown, wssum_ref.at[rows_of(idx ^ k)], s3, r3, k - 1, me).wait_recv()
        for cp in sm + ag:
            cp.wait_send()
        for cp in stores:
            cp.wait()
        for cp in pair:
            cp.wait_send()

    dma7 = pltpu.SemaphoreType.DMA((NDEV - 1,))
    dma4 = pltpu.SemaphoreType.DMA((npg, 4))
    outs = pl.pallas_call(
        body,
        name="reduce_small",
        in_specs=[VM] * 7 + [HBM] + [HBM] * npg,
        out_specs=[VM, VM, VM] + [HBM] * npg,
        out_shape=[jax.ShapeDtypeStruct((NVEC, D), F32), jax.ShapeDtypeStruct((NDEV, 8, D), F32),
                   jax.ShapeDtypeStruct(dws.shape, F32)]
                  + [jax.ShapeDtypeStruct((4,) + g.shape[1:], g.dtype) for g in pair_grads],
        scratch_shapes=[pltpu.VMEM((NVEC, D), F32), pltpu.VMEM((NDEV, NVEC, D), F32),
                        pltpu.VMEM((NDEV, CH, CH), F32), pltpu.VMEM((CH, CH), F32)] + [dma7] * 6
                       + [pltpu.VMEM((4,) + g.shape[1:], g.dtype) for g in pair_grads]
                       + [pltpu.VMEM((3,) + g.shape[1:], g.dtype) for g in pair_grads] + [dma4] * 4,
        compiler_params=pltpu.CompilerParams(vmem_limit_bytes=VMEM_LIMIT),
    )(acc_f, acc_v, acc_b, acc_a, acc_bs, acc_cw, dws, after, *pair_grads)
    return outs[0], outs[1], outs[2], outs[3:]


HBM_ONLY = pl.BlockSpec(memory_space=pltpu.HBM)
SEM = pl.BlockSpec(memory_space=pltpu.SEMAPHORE)
EFFECT = pltpu.SideEffectType.DATAFLOW_SIDE_EFFECTING


def _rs_copies(g_refs, land_refs, sems, chips):
    x, y, c, idx = _place()
    if chips:
        routes = [(j, j, (*ch, c)) for j, ch in enumerate([(1 - x, y), (x, 1 - y), (1 - x, 1 - y)])]
    else:
        routes = [(idx ^ k, k - 1, _dev(idx ^ k)) for k in range(1, NDEV)]
    cps = []
    for src, dst, to in routes:
        for a in range(len(g_refs)):
            n = len(cps)
            cps.append(pltpu.make_async_remote_copy(
                src_ref=g_refs[a].at[src], dst_ref=land_refs[a].at[dst], send_sem=sems[2 * n],
                recv_sem=sems[2 * n + 1], device_id=to, device_id_type=MESH))
    return cps


def _rs_start(grads, name, after=(), chips=False):
    nw = len(grads)
    npeer = 3 if chips else NDEV - 1
    nsem = 2 * nw * npeer
    lands = [lax.empty((npeer,) + g.shape[1:], g.dtype) for g in grads]

    def body(*refs):
        g_refs, land_refs = refs[:nw], refs[nw:2 * nw]
        sems = refs[2 * nw + len(after):2 * nw + len(after) + nsem]
        token = refs[-1]
        for cp in _rs_copies(g_refs, land_refs, sems, chips):
            cp.start()
        token[...] = jnp.zeros_like(token)

    outs = pl.pallas_call(
        body, name=name,
        out_shape=(*[pltpu.SemaphoreType.DMA(())] * nsem,
                   *[pltpu.HBM(g.shape, g.dtype) for g in grads], *[pltpu.HBM(l.shape, l.dtype) for l in lands],
                   jax.ShapeDtypeStruct((8, CH), F32)),
        in_specs=[HBM_ONLY] * (2 * nw) + [HBM] * len(after),
        out_specs=(*[SEM] * nsem, *[HBM_ONLY] * (2 * nw), VM),
        input_output_aliases={i: nsem + i for i in range(2 * nw)},
        compiler_params=pltpu.CompilerParams(has_side_effects=EFFECT),
    )(*[pltpu.with_memory_space_constraint(g, pltpu.HBM) for g in grads],
      *[pltpu.with_memory_space_constraint(l, pltpu.HBM) for l in lands], *after)
    return outs[:nsem], outs[nsem:nsem + nw], outs[nsem + nw:nsem + 2 * nw], outs[-1]


def _rs_wait(sems, g_thru, land_thru, after, name, chips=False):
    nw = len(g_thru)
    nsem = len(sems)

    def body(*refs):
        g_refs, land_refs = refs[:nw], refs[nw:2 * nw]
        for cp in _rs_copies(g_refs, land_refs, refs[2 * nw:2 * nw + nsem], chips):
            cp.wait_send()
            cp.wait_recv()

    outs = pl.pallas_call(
        body, name=name,
        out_shape=tuple(pltpu.HBM(a.shape, a.dtype) for a in list(g_thru) + list(land_thru)),
        in_specs=[HBM_ONLY] * (2 * nw) + [SEM] * nsem + [HBM] * len(after),
        out_specs=tuple([HBM_ONLY] * (2 * nw)),
        input_output_aliases={i: i for i in range(2 * nw)},
        compiler_params=pltpu.CompilerParams(has_side_effects=EFFECT),
    )(*g_thru, *land_thru, *sems, *after)
    return outs[:nw], outs[nw:]


def _adamw(w, g, m, v):
    m2 = ADAM_B1 * m + (1.0 - ADAM_B1) * g
    v2 = ADAM_B2 * v + (1.0 - ADAM_B2) * (g * g)
    m_hat = m2 / (1.0 - ADAM_B1 ** ADAM_STEP)
    v_hat = v2 / (1.0 - ADAM_B2 ** ADAM_STEP)
    delta = -ADAM_LR * (m_hat / (jnp.sqrt(v_hat) + ADAM_EPS) + ADAM_WD * w)
    return delta, m2, v2


def _adam_big(r, w, m, v, rb, name, own, after=None, sib=None):
    R, C = w.shape
    ns = r.shape[0]
    g_all, idx1 = own

    def body(idx_ref, r_ref, own_ref, *refs):
        w_ref, m_ref, v_ref, g_ref, d_ref, m2_ref, v2_ref = refs[len(refs) - 7:]
        g = own_ref[0].astype(F32)
        if sib is not None:
            g = g + refs[0][0].astype(F32)
        for k in range(ns):
            g = g + r_ref[k].astype(F32)
        g_ref[...] = g
        d_ref[...], m2_ref[...], v2_ref[...] = _adamw(w_ref[...], g, m_ref[...], v_ref[...])

    t2 = pl.BlockSpec((rb, C), lambda i, idx_ref: (i, 0))
    sd = jax.ShapeDtypeStruct((R, C), F32)
    extra_specs = ([pl.BlockSpec((1, rb, C), lambda i, idx_ref: (3, i, 0))] if sib is not None else []) \
        + ([HBM] if after is not None else [])
    extra = ([sib] if sib is not None else []) + ([after] if after is not None else [])
    return pl.pallas_call(
        body, name=name,
        grid_spec=pltpu.PrefetchScalarGridSpec(
            num_scalar_prefetch=1, grid=(R // rb,),
            in_specs=[pl.BlockSpec((ns, rb, C), lambda i, idx_ref: (0, i, 0)),
                      pl.BlockSpec((1, rb, C), lambda i, idx_ref: (idx_ref[0], i, 0))] + extra_specs + [t2, t2, t2],
            out_specs=[t2, t2, t2, t2]),
        out_shape=[sd, sd, sd, sd],
        compiler_params=pltpu.CompilerParams(dimension_semantics=("arbitrary",), vmem_limit_bytes=VMEM_LIMIT),
    )(idx1, r, g_all, *extra, w, m, v)


def _adam_ada(cact_t, dcs, w, m, v, rb, name):
    R, C = w.shape

    def body(ct_ref, dc_ref, w_ref, m_ref, v_ref, g_ref, d_ref, m2_ref, v2_ref):
        g = jnp.dot(ct_ref[...], dc_ref[...], preferred_element_type=F32, precision=lax.Precision.HIGHEST)
        g_ref[...] = g
        d_ref[...], m2_ref[...], v2_ref[...] = _adamw(w_ref[...], g, m_ref[...], v_ref[...])

    t2 = pl.BlockSpec((rb, C), lambda i: (i, 0))
    sd = jax.ShapeDtypeStruct((R, C), F32)
    return pl.pallas_call(
        body, name=name, grid=(R // rb,),
        in_specs=[pl.BlockSpec((rb, NDEV), lambda i: (i, 0)), _full((NDEV, C)), t2, t2, t2],
        out_specs=[t2, t2, t2, t2], out_shape=[sd, sd, sd, sd],
        compiler_params=pltpu.CompilerParams(dimension_semantics=("arbitrary",), vmem_limit_bytes=VMEM_LIMIT),
    )(cact_t, dcs, w, m, v)


_SMALL = ["ada_b", "ada_f_b", "norm1_g", "b_in", "a_ln_g", "a_ln_b", "a_spatial_b", "b_conv_b", "b_gn_g", "b_gn_b",
          "out_norm_a_g", "out_norm_b_g", "norm2_g", "norm_f_g", "a_spatial_w", "b_conv_w"]


def _adam_small(vsum, wssum, gcw, params):
    names = _SMALL
    flat = []
    for n in names:
        flat += list(params[n])

    def body(vs_ref, ws_ref, gcw_ref, *rest):
        ins = rest[:3 * len(names)]
        outs = rest[3 * len(names):]
        for pi, n in enumerate(names):
            w_ref, m_ref, v_ref = ins[3 * pi:3 * pi + 3]
            g_ref, d_ref, m2_ref, v2_ref = outs[4 * pi:4 * pi + 4]
            if n in ("ada_b", "ada_f_b", "b_in"):
                row0 = {"ada_b": 0, "ada_f_b": 6, "b_in": 9}[n]
                pieces = [(vs_ref[row0 + r:row0 + r + 1, :], slice(r * D, (r + 1) * D))
                          for r in range(w_ref.shape[1] // D)]
            elif n == "a_spatial_w":
                pieces = [(ws_ref[...], slice(None))]
            elif n == "b_conv_w":
                pieces = [(gcw_ref[...], slice(None))]
            else:
                row, off, width = _VEC_AT[n]
                pieces = [(vs_ref[row:row + 1, off:off + width], slice(None))]
            for g, cs in pieces:
                g_ref[:, cs] = g
                d_ref[:, cs], m2_ref[:, cs], v2_ref[:, cs] = _adamw(w_ref[:, cs], g, m_ref[:, cs], v_ref[:, cs])

    out_shape = []
    for n in names:
        out_shape += [jax.ShapeDtypeStruct(params[n][0].shape, F32)] * 4
    outs = pl.pallas_call(
        body, name="adam_small",
        in_specs=[VM] * (3 + len(flat)), out_specs=[VM] * len(out_shape), out_shape=out_shape,
        compiler_params=pltpu.CompilerParams(vmem_limit_bytes=VMEM_LIMIT),
    )(vsum, wssum, gcw, *flat)
    return {n: outs[4 * pi:4 * pi + 4] for pi, n in enumerate(names)}


def _token_tile(T, want):
    return want if T % want == 0 else T


def kernel(x, c, ada_w, ada_b, norm1_g, w_in, b_in, a_ln_g, a_ln_b, a_spatial_w, a_spatial_b, b_conv_w, b_conv_b, b_gn_g, b_gn_b, out_norm_a_g, out_norm_b_g, w_out, norm2_g, w_ffn_in, w_ffn_out, ada_f_w, ada_f_b, norm_f_g, loss_target, m_ada_w, m_ada_b, m_norm1_g, m_w_in, m_b_in, m_a_ln_g, m_a_ln_b, m_a_spatial_w, m_a_spatial_b, m_b_conv_w, m_b_conv_b, m_b_gn_g, m_b_gn_b, m_out_norm_a_g, m_out_norm_b_g, m_w_out, m_norm2_g, m_w_ffn_in, m_w_ffn_out, m_ada_f_w, m_ada_f_b, m_norm_f_g, v_ada_w, v_ada_b, v_norm1_g, v_w_in, v_b_in, v_a_ln_g, v_a_ln_b, v_a_spatial_w, v_a_spatial_b, v_b_conv_w, v_b_conv_b, v_b_gn_g, v_b_gn_b, v_out_norm_a_g, v_out_norm_b_g, v_w_out, v_norm2_g, v_w_ffn_in, v_w_ffn_out, v_ada_f_w, v_ada_f_b, v_norm_f_g):
    T = x.shape[1]
    idx = 4 * lax.axis_index("x") + 2 * lax.axis_index("y") + lax.axis_index("c")
    x2d = x.reshape(T, D)
    tgt = loss_target.reshape(T, D)

    conv_s = jnp.pad(b_conv_w[0], ((0, HALO - KW), (0, 0)))
    call, cparts, cfparts, convg, (win_g, wout_g) = _gather(
        c, ada_w[0], ada_b.reshape(NDEV, -1), ada_f_w, ada_f_b.reshape(NDEV, -1), conv_s,
        [w_in[0], w_out[0]])
    wout = wout_g.reshape(D, D)
    mod = jnp.concatenate([cparts.reshape(6, D), cfparts.reshape(2, D)], axis=0)
    cw = jnp.transpose(convg, (1, 0, 2)).reshape(HALO, DB)

    tril = jnp.tril(jnp.ones((CH, CH), dtype=bool))
    wsm = jnp.where(tril[None], a_spatial_w[0], 0.0).astype(BF16)
    wcat = wsm.reshape(NH * CH, CH)
    wcat_t = jnp.transpose(wsm, (0, 2, 1)).reshape(NH * CH, CH)
    bsf = jnp.repeat(a_spatial_b[0].T, DA // NH, axis=1)
    pm = jnp.asarray(_GROUP_MEAN, BF16)
    esel = jnp.asarray(_HEAD_SELECT, BF16)

    tm = _token_tile(T, 256)
    tk = _token_tile(T, 2048)
    (x1, hb, zvg, mixed, yb, o, gu, dgelu_u, dgelu_v, vhat, rslb, yhat, rsg), (wfi_g, wfo_g) = _mix_fwd(
        x2d, mod, norm1_g, win_g, b_in, a_ln_g, a_ln_b, wcat, bsf, cw, b_conv_b, b_gn_g, b_gn_b, out_norm_a_g,
        out_norm_b_g, wout, pm, [w_ffn_in[0].T, w_ffn_out[0]], _token_tile(T, 512))
    dx1, h2b, dgu, act, dxg, acc_f = _ffn(x1, tgt, mod, norm2_g, norm_f_g.reshape(1, D),
                                          wfi_g.reshape(2 * DFF, D), wfo_g.reshape(DFF, D), tm)
    g_wfi = _wgrad_rows(dgu, h2b, 2 * WFI_B, tk, "wgrad_ffn_in").reshape(NDEV, WFI_B, D)
    g_wfo = _wgrad_rows(act, dxg, 2 * WFI_B, tk, "wgrad_ffn_out").reshape(NDEV, DFF // NDEV, D)
    f_sems, f_thru, f_land, f_token = _rs_start([g_wfi, g_wfo], "rs_ffn_start")
    (gx, acc_v, acc_b, acc_a, acc_bs, acc_ws, acc_cw), (g_win, g_wout) = _mix_bwd(
        dx1, x2d, zvg, mixed, o, hb, yb, gu, dgelu_u, dgelu_v, vhat, rslb, yhat, rsg, mod, norm1_g, win_g, a_ln_g,
        a_ln_b, wcat, wcat_t, cw, b_gn_g, b_gn_b, out_norm_a_g, out_norm_b_g, wout, pm, esel, f_token, tm)
    (g_wfi_d, g_wfo_d), (r_wfi, r_wfo) = _rs_wait(f_sems, f_thru, f_land, [acc_v], "rs_ffn_wait")
    g_wout = g_wout.reshape(NDEV, D // NDEV, D)

    vsum, dcond_all, wssum, (q_win, q_wout) = _reduce_small(acc_f, acc_v, acc_b, acc_a, acc_bs, acc_cw, acc_ws,
                                                            g_wfi_d, [g_win, g_wout])
    sems, g_thru, land_thru, token = _rs_start([q_win, q_wout], "rs_mix_start", after=(vsum,), chips=True)

    own = lambda g: (g, jnp.reshape(idx, (1,)).astype(jnp.int32))
    res = {}
    res["w_ffn_in"] = tuple(a.T for a in _adam_big(r_wfi, w_ffn_in[0].T, m_w_ffn_in[0].T, v_w_ffn_in[0].T, WFI_B // 2,
                                                   "adam_w_ffn_in", own=own(g_wfi_d), after=token))
    res["w_ffn_out"] = _adam_big(r_wfo, w_ffn_out[0], m_w_ffn_out[0], v_w_ffn_out[0], DFF // NDEV // 2,
                                 "adam_w_ffn_out", own=own(g_wfo_d), after=token)
    cact_t = (call * jax.nn.sigmoid(call)).T
    dcond = dcond_all.reshape(NDEV, 8 * D)
    nada = ada_w.shape[2]
    nadf = ada_f_w.shape[1]
    dcs = lax.dynamic_slice(dcond, (0, idx * nada), (NDEV, nada))
    dcfs = lax.dynamic_slice(dcond, (0, 6 * D + idx * nadf), (NDEV, nadf))
    res["ada_w"] = _adam_ada(cact_t, dcs, ada_w[0], m_ada_w[0], v_ada_w[0], 512, "adam_ada_w")
    res["ada_f_w"] = _adam_ada(cact_t, dcfs, ada_f_w, m_ada_f_w, v_ada_f_w, 512, "adam_ada_f_w")
    ncw = b_conv_w.shape[2]
    gcw = jnp.concatenate([lax.dynamic_slice(vsum, (_CW_ROW, idx * ncw), (HALO // 2, ncw)),
                           lax.dynamic_slice(vsum, (_CW_ROW, DB + idx * ncw), (HALO // 2, ncw))], axis=0)[:KW]
    two = lambda a: a.reshape(1, -1) if a.ndim == 1 else a.reshape(-1, a.shape[-1])
    small_in = {
        "ada_b": (ada_b, m_ada_b, v_ada_b), "ada_f_b": (ada_f_b, m_ada_f_b, v_ada_f_b),
        "norm1_g": (norm1_g, m_norm1_g, v_norm1_g), "b_in": (b_in, m_b_in, v_b_in),
        "a_ln_g": (a_ln_g, m_a_ln_g, v_a_ln_g), "a_ln_b": (a_ln_b, m_a_ln_b, v_a_ln_b),
        "a_spatial_b": (a_spatial_b.reshape(1, D), m_a_spatial_b.reshape(1, D), v_a_spatial_b.reshape(1, D)),
        "b_conv_b": (b_conv_b, m_b_conv_b, v_b_conv_b), "b_gn_g": (b_gn_g, m_b_gn_g, v_b_gn_g),
        "b_gn_b": (b_gn_b, m_b_gn_b, v_b_gn_b), "out_norm_a_g": (out_norm_a_g, m_out_norm_a_g, v_out_norm_a_g),
        "out_norm_b_g": (out_norm_b_g, m_out_norm_b_g, v_out_norm_b_g),
        "norm2_g": (norm2_g, m_norm2_g, v_norm2_g), "norm_f_g": (norm_f_g, m_norm_f_g, v_norm_f_g),
        "a_spatial_w": (a_spatial_w, m_a_spatial_w, v_a_spatial_w),
        "b_conv_w": (b_conv_w[0], m_b_conv_w[0], v_b_conv_w[0]),
    }
    small_in = {n: tuple(two(a) for a in t) for n, t in small_in.items()}
    res.update(_adam_small(vsum, wssum, gcw, small_in))
    (q_win_d, q_wout_d), (r_win, r_wout) = _rs_wait(
        sems, g_thru, land_thru,
        [res["w_ffn_in"][0], res["w_ffn_out"][0], res["ada_w"][0], res["ada_f_w"][0], res["norm_f_g"][0]],
        "rs_mix_wait", chips=True)
    res["w_in"] = _adam_big(r_win, w_in[0], m_w_in[0], v_w_in[0], 512, "adam_w_in", own=own(g_win), sib=q_win_d)
    res["w_out"] = _adam_big(r_wout, w_out[0], m_w_out[0], v_w_out[0], D // NDEV, "adam_w_out", own=own(g_wout),
                             sib=q_wout_d)

    loss = 0.5 / D * jnp.sum(vsum[_LOSS_ROW])
    shapes = {"ada_w": ada_w, "ada_b": ada_b, "norm1_g": norm1_g, "w_in": w_in, "b_in": b_in, "a_ln_g": a_ln_g,
              "a_ln_b": a_ln_b, "a_spatial_w": a_spatial_w, "a_spatial_b": a_spatial_b, "b_conv_w": b_conv_w,
              "b_conv_b": b_conv_b, "b_gn_g": b_gn_g, "b_gn_b": b_gn_b, "out_norm_a_g": out_norm_a_g,
              "out_norm_b_g": out_norm_b_g, "w_out": w_out, "norm2_g": norm2_g, "w_ffn_in": w_ffn_in,
              "w_ffn_out": w_ffn_out, "ada_f_w": ada_f_w, "ada_f_b": ada_f_b, "norm_f_g": norm_f_g}
    order = list(shapes)
    outs = [loss, gx.reshape(x.shape)]
    for which in range(4):
        outs += [res[n][which].reshape(shapes[n].shape) for n in order]
    return tuple(outs)
```

```python
import math

import numpy as np

import jax
import jax.numpy as jnp
from jax import lax
from jax.experimental import pallas as pl
from jax.experimental.pallas import tpu as pltpu

F32 = jnp.float32
BF16 = jnp.bfloat16

D = 1024
DA = 512
DB = 512
DIN = 2048
DFF = 2816
NH = 8
CH = 128
KW = 31
HALO = 32
NDEV = 8
WIN_B = DIN // NDEV
WFI_B = 2 * DFF // NDEV
EPS = 1e-6
NVEC = 40
VMEM_LIMIT = 56 * 1024 * 1024

ADAM_LR, ADAM_B1, ADAM_B2, ADAM_EPS, ADAM_WD, ADAM_STEP = 0.001, 0.9, 0.999, 1e-08, 0.01, 10

MESH = pl.DeviceIdType.MESH

_LANE = np.arange(DB)
_GROUP_MEAN = np.where((_LANE[:, None] >> 6) == (_LANE[None, :] >> 6), 1.0 / 64.0, 0.0).astype(np.float32)
_HEAD_SELECT = np.where((_LANE[:, None] >> 6) == np.arange(CH)[None, :], 1.0, 0.0).astype(np.float32)


def _dot(a, b):
    return jnp.dot(a, b, preferred_element_type=F32)


def _dot_nt(a, b):
    return lax.dot_general(a, b, (((1,), (1,)), ((), ())), preferred_element_type=F32)


def _dot_tn(a, b):
    return lax.dot_general(a, b, (((0,), (0,)), ((), ())), preferred_element_type=F32)


def _rs(v):
    return lax.rsqrt(jnp.mean(v * v, axis=-1, keepdims=True) + EPS)


def _sig(v):
    return 1.0 / (1.0 + jnp.exp(-v))


_INV_SQRT2 = 1.0 / math.sqrt(2.0)
_INV_SQRT2PI = 1.0 / math.sqrt(2.0 * math.pi)


def _gelu_parts(v):
    cdf = 0.5 * (1.0 + lax.erf(v * _INV_SQRT2))
    pdf = jnp.exp(-0.5 * v * v) * _INV_SQRT2PI
    return v * cdf, cdf + v * pdf


def _grp_mean(v, pm):
    hi = v.astype(BF16)
    lo = (v - hi.astype(F32)).astype(BF16)
    return _dot(hi, pm) + _dot(lo, pm)


def _colsum(v):
    return jnp.sum(v, axis=0, keepdims=True)


def _full(shape):
    nd = len(shape)
    return pl.BlockSpec(shape, lambda *_: (0,) * nd)


def _resident(shape):
    nd = len(shape)
    return pl.BlockSpec(shape, lambda *_: (0,) * nd, pipeline_mode=pl.Buffered(1))


HBM = pl.BlockSpec(memory_space=pl.ANY)
VM = pl.BlockSpec(memory_space=pltpu.VMEM)


SH_ROWS = HALO - 8


def _shifted_copies(buf, shbuf, tm):
    for b in range(1, 8):
        shbuf[b - 1] = buf[b:b + tm + SH_ROWS, :]


def _window(buf, shbuf, off, tm):
    a, b = divmod(off, 8)
    if b == 0:
        return buf[8 * a:8 * a + tm, :]
    return shbuf[b - 1, 8 * a:8 * a + tm, :]


def _first_head_lanes():
    return lax.broadcasted_iota(jnp.int32, (CH, CH), 1) < (DA // NH)


def _mix_heads(w_ref, vb, first):
    outs = []
    for p in range(NH // 2):
        v = vb[:, p * CH:(p + 1) * CH]
        a = _dot(w_ref[(2 * p) * CH:(2 * p + 1) * CH, :], v)
        b = _dot(w_ref[(2 * p + 1) * CH:(2 * p + 2) * CH, :], v)
        outs.append(jnp.where(first, a, b))
    return jnp.concatenate(outs, axis=1)


def _place():
    x, y, c = lax.axis_index("x"), lax.axis_index("y"), lax.axis_index("c")
    return x, y, c, 4 * x + 2 * y + c


def _dev(t):
    return (t >> 2, (t >> 1) & 1, t & 1)


class _AllGather:
    def __init__(self, w_in, w_out, wss, wrs, lsem):
        x, y, c, idx = _place()
        me, sibling = (x, y, c), (x, y, 1 - c)
        chips = [(1 - x, y), (x, 1 - y), (1 - x, 1 - y)]
        nw = len(w_in)

        def blk(p):
            return 4 * p[0] + 2 * p[1] + p[2]

        def wcopy(a, k, block, to, src=None):
            dst = w_out[a].at[blk(block)]
            return pltpu.make_async_remote_copy(src_ref=dst if src is None else src, dst_ref=dst,
                                                send_sem=wss.at[a, k], recv_sem=wrs.at[a, k],
                                                device_id=to, device_id_type=MESH)

        self.mine = [pltpu.make_async_copy(w_in[a], w_out[a].at[idx], lsem.at[a]) for a in range(nw)]
        self.first = []
        for a in range(nw):
            self.first.append(wcopy(a, 0, me, sibling, src=w_in[a]))
            self.first += [wcopy(a, 1 + j, me, (*chip, c), src=w_in[a]) for j, chip in enumerate(chips)]
        self.landed = [[wcopy(a, 1 + j, (*chip, c), me) for a in range(nw)] for j, chip in enumerate(chips)]
        self.passed = [[wcopy(a, 4 + j, (*chip, c), sibling) for a in range(nw)] for j, chip in enumerate(chips)]
        self.from_sibling = []
        for a in range(nw):
            self.from_sibling.append(wcopy(a, 0, sibling, me))
            self.from_sibling += [wcopy(a, 4 + j, (*chip, 1 - c), me) for j, chip in enumerate(chips)]

    def start(self):
        for cp in self.mine + self.first:
            cp.start()

    def forward(self):
        for land, pas in zip(self.landed, self.passed):
            for l, p in zip(land, pas):
                l.wait_recv()
                p.start()

    def finish(self):
        for cp in self.from_sibling:
            cp.wait_recv()
        for cp in self.first:
            cp.wait_send()
        for pas in self.passed:
            for p in pas:
                p.wait_send()
        for cp in self.mine:
            cp.wait()


AG_SEMS = lambda nw: [pltpu.SemaphoreType.DMA((nw, 7)), pltpu.SemaphoreType.DMA((nw, 7)),
                      pltpu.SemaphoreType.DMA((nw,))]


def _mix_fwd(x, mod, g1, win, b_in, lng, lnb, wcat, bsf, cw, cb, gng, gnb, oga, ogb, wout, pm, ffn_shards, tm):
    T = x.shape[0]
    nt = T // tm
    nch = tm // CH
    nw = len(ffn_shards)
    fwd_step = (5 * nt) // 8
    saved = [(D, F32), (D, BF16), (2 * DB, F32), (DA, F32), (D, BF16), (D, F32), (DA, F32), (DA, F32), (DA, F32),
             (DA, F32), (CH, F32), (DB, F32), (DB, F32)]
    NSAVE = len(saved)

    def body(x_ref, mod_ref, g1_ref, win_ref, bin_ref, lng_ref, lnb_ref, wcat_ref, bsf_ref, cw_ref, cb_ref,
             gng_ref, gnb_ref, oga_ref, ogb_ref, wout_ref, pm_ref, *rest):
        sh_f32 = rest[:nw]
        (x1_ref, h_ref, zvg_ref, mixed_ref, y_ref, o_ref, gu_ref, dgu_ref, dgv_ref, vhat_ref, rsl_ref, yhat_ref,
         rsg_ref) = rest[nw:nw + NSAVE]
        sh_out = rest[nw + NSAVE:2 * nw + NSAVE]
        glbuf, shbuf = rest[2 * nw + NSAVE:2 * nw + NSAVE + 2]
        sh_in = rest[2 * nw + NSAVE + 2:3 * nw + NSAVE + 2]
        wss, wrs, lsem = rest[3 * nw + NSAVE + 2:]
        i = pl.program_id(0)

        @pl.when(i == 0)
        def _():
            for a in range(nw):
                sh_in[a][...] = sh_f32[a][...].astype(BF16)
            _AllGather(sh_in, sh_out, wss, wrs, lsem).start()

        xv = x_ref[...]
        shift1 = mod_ref[0:1, :]
        scale1 = mod_ref[1:2, :]
        gate1 = mod_ref[2:3, :]
        h = (xv * _rs(xv) * g1_ref[...]) * (1.0 + scale1) + shift1
        hb = h.astype(BF16)
        h_ref[...] = hb
        z = jnp.concatenate([_dot(hb, win_ref[j]) for j in range(NDEV)], axis=1) + bin_ref[...]
        zvg_ref[...] = z[:, 2 * DA:]
        gu, dgelu_u = _gelu_parts(z[:, 0:DA])
        gv, dgelu_v = _gelu_parts(z[:, DA:2 * DA])
        gu_ref[...] = gu
        dgu_ref[...] = dgelu_u
        dgv_ref[...] = dgelu_v
        xc = gv - jnp.mean(gv, axis=-1, keepdims=True)
        rsl = lax.rsqrt(jnp.mean(xc * xc, axis=-1, keepdims=True) + EPS)
        vhat = xc * rsl
        vhat_ref[...] = vhat
        rsl_ref[...] = jnp.broadcast_to(rsl, (tm, CH))
        vnb = (vhat * lng_ref[...] + lnb_ref[...]).astype(BF16)
        first = _first_head_lanes()
        chunks = []
        for ci in range(nch):
            chunks.append(_mix_heads(wcat_ref, vnb[ci * CH:(ci + 1) * CH, :], first) + bsf_ref[...])
        mixed = jnp.concatenate(chunks, axis=0) if nch > 1 else chunks[0]
        mixed_ref[...] = mixed
        ya = gu * mixed
        gl = z[:, 2 * DA:2 * DA + DB] * _sig(z[:, 2 * DA + DB:])

        @pl.when(i == 0)
        def _():
            glbuf[0:HALO, :] = jnp.zeros((HALO, DB), F32)

        glbuf[HALO:HALO + tm, :] = gl
        _shifted_copies(glbuf, shbuf, tm)
        yc = jnp.zeros((tm, DB), F32) + cb_ref[...]
        for k in range(KW):
            yc = yc + cw_ref[k:k + 1, :] * _window(glbuf, shbuf, HALO - (KW - 1) + k, tm)
        glbuf[0:HALO, :] = gl[tm - HALO:, :]
        pmv = pm_ref[...]
        dc = yc - _grp_mean(yc, pmv)
        rsg = lax.rsqrt(_grp_mean(dc * dc, pmv) + EPS)
        yhat = dc * rsg
        yhat_ref[...] = yhat
        rsg_ref[...] = rsg
        yg = yhat * gng_ref[...] + gnb_ref[...]
        yb = yg * _sig(yg)
        na = ya * _rs(ya) * oga_ref[...]
        nb = yb * _rs(yb) * ogb_ref[...]
        yv = jnp.concatenate([na, nb], axis=1).astype(BF16)
        y_ref[...] = yv
        o = _dot(yv, wout_ref[...])
        o_ref[...] = o
        x1_ref[...] = xv + gate1 * o

        @pl.when(i == fwd_step)
        def _():
            _AllGather(sh_in, sh_out, wss, wrs, lsem).forward()

        @pl.when(i == nt - 1)
        def _():
            _AllGather(sh_in, sh_out, wss, wrs, lsem).finish()

    tile = lambda w: pl.BlockSpec((tm, w), lambda i: (i, 0))
    outs = pl.pallas_call(
        body,
        name="mix_fwd",
        grid=(nt,),
        in_specs=[tile(D), _full((8, D)), _full((1, D)), _resident((NDEV, D, WIN_B)), _full((1, DIN)),
                  _full((1, DA)), _full((1, DA)), _full((NH * CH, CH)), _full((CH, DA)), _full((HALO, DB)),
                  _full((1, DB)), _full((1, DB)), _full((1, DB)), _full((1, DA)), _full((1, DB)),
                  _resident((D, D)), _full((DB, DB))] + [_resident(s.shape) for s in ffn_shards],
        out_specs=[tile(w) for w, _ in saved] + [HBM] * nw,
        out_shape=[jax.ShapeDtypeStruct((T, w), dt) for w, dt in saved]
                  + [jax.ShapeDtypeStruct((NDEV,) + s.shape, BF16) for s in ffn_shards],
        scratch_shapes=[pltpu.VMEM((HALO + tm, DB), F32), pltpu.VMEM((7, tm + SH_ROWS, DB), F32)]
                       + [pltpu.VMEM(s.shape, BF16) for s in ffn_shards] + AG_SEMS(nw),
        compiler_params=pltpu.CompilerParams(dimension_semantics=("arbitrary",), vmem_limit_bytes=VMEM_LIMIT),
    )(x, mod, g1, win, b_in, lng, lnb, wcat, bsf, cw, cb, gng, gnb, oga, ogb, wout, pm, *ffn_shards)
    return outs[:NSAVE], outs[NSAVE:]


FF_BLOCKS = ((0, 1024), (1024, 1024), (2048, 768))


def _ffn(x1, tgt, mod, g2, gf, wfi_t, wfo, tm):
    T = x1.shape[0]
    nt = T // tm

    def body(x1_ref, tgt_ref, mod_ref, g2_ref, gf_ref, wfi_ref, wfo_ref,
             dx1_ref, h2_ref, dgu_ref, act_ref, dxg_ref, acc_ref, g_s, u_s):
        i = pl.program_id(0)

        @pl.when(i == 0)
        def _():
            acc_ref[...] = jnp.zeros((8, D), F32)

        x1 = x1_ref[...]
        shift2 = mod_ref[3:4, :]
        scale2 = mod_ref[4:5, :]
        gate2 = mod_ref[5:6, :]
        shiftf = mod_ref[6:7, :]
        scalef = mod_ref[7:8, :]
        g2v = g2_ref[...]
        gfv = gf_ref[...]
        r2 = _rs(x1)
        xn2 = x1 * r2
        h2b = (xn2 * g2v * (1.0 + scale2) + shift2).astype(BF16)
        h2_ref[...] = h2b
        f = jnp.zeros((tm, D), F32)
        for o, w in FF_BLOCKS:
            g = _dot_nt(h2b, wfi_ref[o:o + w, :])
            u = _dot_nt(h2b, wfi_ref[DFF + o:DFF + o + w, :])
            g_s[:, o:o + w] = g
            u_s[:, o:o + w] = u
            actb = (g * _sig(g) * u).astype(BF16)
            act_ref[:, o:o + w] = actb
            f = f + _dot(actb, wfo_ref[o:o + w, :])
        x2 = x1 + gate2 * f
        rf = _rs(x2)
        xnf = x2 * rf
        out = xnf * gfv * (1.0 + scalef) + shiftf
        e = out - tgt_ref[...]
        dout = e * (1.0 / D)
        acc_ref[7:8, :] += _colsum(e * e)
        acc_ref[0:1, :] += _colsum(dout)
        acc_ref[1:2, :] += _colsum(dout * xnf * gfv)
        acc_ref[2:3, :] += _colsum(dout * (1.0 + scalef) * xnf)
        dxnf = dout * (1.0 + scalef) * gfv
        dx2 = rf * (dxnf - xnf * jnp.mean(dxnf * xnf, axis=-1, keepdims=True))
        acc_ref[3:4, :] += _colsum(dx2 * f)
        dxgb = (dx2 * gate2).astype(BF16)
        dxg_ref[...] = dxgb
        dh2 = jnp.zeros((tm, D), F32)
        for o, w in FF_BLOCKS:
            dact = _dot_nt(dxgb, wfo_ref[o:o + w, :])
            g = g_s[:, o:o + w]
            u = u_s[:, o:o + w]
            s = _sig(g)
            dgb = (dact * u * (s * (1.0 + g * (1.0 - s)))).astype(BF16)
            dub = (dact * (g * s)).astype(BF16)
            dgu_ref[:, o:o + w] = dgb
            dgu_ref[:, DFF + o:DFF + o + w] = dub
            dh2 = dh2 + _dot(dgb, wfi_ref[o:o + w, :])
            dh2 = dh2 + _dot(dub, wfi_ref[DFF + o:DFF + o + w, :])
        acc_ref[4:5, :] += _colsum(dh2)
        acc_ref[5:6, :] += _colsum(dh2 * xn2 * g2v)
        acc_ref[6:7, :] += _colsum(dh2 * (1.0 + scale2) * xn2)
        dxn2 = dh2 * (1.0 + scale2) * g2v
        dx1_ref[...] = dx2 + r2 * (dxn2 - xn2 * jnp.mean(dxn2 * xn2, axis=-1, keepdims=True))

    tile = lambda w: pl.BlockSpec((tm, w), lambda i: (i, 0))
    return pl.pallas_call(
        body,
        name="ffn_fwd_bwd",
        grid=(nt,),
        in_specs=[tile(D), tile(D), _full((8, D)), _full((1, D)), _full((1, D)),
                  _resident((2 * DFF, D)), _resident((DFF, D))],
        out_specs=[tile(D), tile(D), tile(2 * DFF), tile(DFF), tile(D), _full((8, D))],
        out_shape=[jax.ShapeDtypeStruct((T, D), F32), jax.ShapeDtypeStruct((T, D), BF16),
                   jax.ShapeDtypeStruct((T, 2 * DFF), BF16), jax.ShapeDtypeStruct((T, DFF), BF16),
                   jax.ShapeDtypeStruct((T, D), BF16), jax.ShapeDtypeStruct((8, D), F32)],
        scratch_shapes=[pltpu.VMEM((tm, DFF), F32), pltpu.VMEM((tm, DFF), F32)],
        compiler_params=pltpu.CompilerParams(dimension_semantics=("arbitrary",), vmem_limit_bytes=VMEM_LIMIT),
    )(x1, tgt, mod, g2, gf, wfi_t, wfo)


def _mix_bwd(dx1, x, zvg, mixed, o, hb, yb, gu, dgu, dgv, vhat, rslb, yhat, rsg, mod, g1, win, lng, lnb, wcat, wcat_t,
             cw, gng, gnb, oga, ogb, wout, pm, esel, after, tm):
    T = x.shape[0]
    nt = T // tm
    nch = tm // CH
    WOB = 256

    def body(dx1_ref, x_ref, zvg_ref, mixed_ref, o_ref, hb_ref, yb_ref, gu_ref, dgu_ref, dgv_ref, vhat_ref, rsl_ref,
             yhat_ref, rsg_ref, mod_ref, g1_ref, win_ref, lng_ref, lnb_ref, wcat_ref, wcatt_ref, cw_ref, gng_ref,
             gnb_ref, oga_ref, ogb_ref, wout_ref, pm_ref, esel_ref, after_ref,
             gx_ref, accv_ref, accb_ref, acca_ref, accbs_ref, accws_ref, acccw_ref, gwin_ref, gwout_ref,
             dycbuf, shbuf, bs_s, acc_win, acc_wout, st_win, st_wout):
        i = pl.program_id(0)

        @pl.when(i == 0)
        def _():
            acc_win[...] = jnp.zeros((NDEV, D, WIN_B), F32)
            acc_wout[...] = jnp.zeros((D, D), F32)
            accv_ref[...] = jnp.zeros((8, D), F32)
            accb_ref[...] = jnp.zeros((1, DIN), F32)
            acca_ref[...] = jnp.zeros((8, DA), F32)
            accws_ref[...] = jnp.zeros((NH * CH, CH), F32)
            acccw_ref[...] = jnp.zeros((HALO, DB), F32)
            bs_s[...] = jnp.zeros((CH, DA), F32)
            dycbuf[tm:tm + HALO, :] = jnp.zeros((HALO, DB), F32)

        shift1 = mod_ref[0:1, :]
        scale1 = mod_ref[1:2, :]
        gate1 = mod_ref[2:3, :]
        g1v = g1_ref[...]
        xv = x_ref[...]
        r1 = _rs(xv)
        xn1 = xv * r1
        val = zvg_ref[:, 0:DB]
        gate = zvg_ref[:, DB:]
        gu = gu_ref[...]
        dgelu_u = dgu_ref[...]
        dgelu_v = dgv_ref[...]
        vhat = vhat_ref[...]
        rsl = rsl_ref[:, 0:1]
        lngv = lng_ref[...]
        vnb = (vhat * lngv + lnb_ref[...]).astype(BF16)
        mixed = mixed_ref[...]
        ya = gu * mixed
        ra = _rs(ya)
        yan = ya * ra
        sgt = _sig(gate)
        gl = val * sgt
        pmv = pm_ref[...]
        rsg = rsg_ref[...]
        yhat = yhat_ref[...]
        gngv = gng_ref[...]
        yg = yhat * gngv + gnb_ref[...]
        sgy = _sig(yg)
        yb = yg * sgy
        rb = _rs(yb)
        ybn = yb * rb
        dx1 = dx1_ref[...]
        accv_ref[0:1, :] += _colsum(dx1 * o_ref[...])
        dogb = (dx1 * gate1).astype(BF16)
        acc_wout[...] += _dot_tn(yb_ref[...], dogb)
        dy = _dot_nt(dogb, wout_ref[...])
        dna = dy[:, 0:DA]
        dnb = dy[:, DA:]
        ogav = oga_ref[...]
        ogbv = ogb_ref[...]
        acca_ref[2:3, :] += _colsum(dna * yan)
        acca_ref[3:4, :] += _colsum(dnb * ybn)
        ta = dna * ogav
        dya = ra * (ta - yan * jnp.mean(ta * yan, axis=-1, keepdims=True))
        tb = dnb * ogbv
        dyb = rb * (tb - ybn * jnp.mean(tb * ybn, axis=-1, keepdims=True))
        dgu = dya * mixed
        dm = dya * gu
        first = _first_head_lanes()
        zero = jnp.zeros((CH, CH), BF16)
        dvn_chunks = []
        bs_acc = bs_s[...]
        for ci in range(nch):
            dmc = dm[ci * CH:(ci + 1) * CH, :]
            bs_acc = bs_acc + dmc
            dmcb = dmc.astype(BF16)
            dvn_chunks.append(_mix_heads(wcatt_ref, dmcb, first))
            vc = vnb[ci * CH:(ci + 1) * CH, :]
            for p in range(NH // 2):
                xt = dmcb[:, p * CH:(p + 1) * CH]
                vt = vc[:, p * CH:(p + 1) * CH]
                accws_ref[(2 * p) * CH:(2 * p + 1) * CH, :] += _dot_nt(jnp.where(first, xt, zero), vt)
                accws_ref[(2 * p + 1) * CH:(2 * p + 2) * CH, :] += _dot_nt(jnp.where(first, zero, xt), vt)
        bs_s[...] = bs_acc
        dvn = jnp.concatenate(dvn_chunks, axis=0) if nch > 1 else dvn_chunks[0]
        acca_ref[0:1, :] += _colsum(dvn * vhat)
        acca_ref[1:2, :] += _colsum(dvn)
        dvh = dvn * lngv
        dgv = rsl * (dvh - jnp.mean(dvh, axis=-1, keepdims=True)
                     - vhat * jnp.mean(dvh * vhat, axis=-1, keepdims=True))
        du = dgu * dgelu_u
        dv = dgv * dgelu_v
        dyg = dyb * (sgy * (1.0 + yg * (1.0 - sgy)))
        acca_ref[5:6, :] += _colsum(dyg * yhat)
        acca_ref[6:7, :] += _colsum(dyg)
        dyh = dyg * gngv
        dyc = rsg * (dyh - _grp_mean(dyh, pmv) - yhat * _grp_mean(dyh * yhat, pmv))
        acca_ref[4:5, :] += _colsum(dyc)
        dycbuf[0:tm, :] = dyc
        _shifted_copies(dycbuf, shbuf, tm)
        dgl = jnp.zeros((tm, DB), F32)
        for k in range(KW):
            win_k = _window(dycbuf, shbuf, KW - 1 - k, tm)
            dgl = dgl + cw_ref[k:k + 1, :] * win_k
            acccw_ref[k:k + 1, :] += _colsum(win_k * gl)
        dycbuf[tm:tm + HALO, :] = dyc[0:HALO, :]
        dval = dgl * sgt
        dgate = dgl * val * sgt * (1.0 - sgt)
        dz = jnp.concatenate([du, dv, dval, dgate], axis=1)
        accb_ref[...] += _colsum(dz)
        dzb = dz.astype(BF16)
        hbv = hb_ref[...]
        dh = jnp.zeros((tm, D), F32)
        for j in range(NDEV):
            dzj = dzb[:, j * WIN_B:(j + 1) * WIN_B]
            acc_win[j] += _dot_tn(hbv, dzj)
            dh = dh + _dot_nt(dzj, win_ref[j])
        accv_ref[1:2, :] += _colsum(dh)
        dh_xn = _colsum(dh * xn1)
        accv_ref[2:3, :] += dh_xn * g1v
        accv_ref[3:4, :] += dh_xn * (1.0 + scale1)
        dxn1 = dh * (1.0 + scale1) * g1v
        gx_ref[...] = dx1 + r1 * (dxn1 - xn1 * jnp.mean(dxn1 * xn1, axis=-1, keepdims=True))

        @pl.when(i == nt - 1)
        def _():
            rows = lax.broadcasted_iota(jnp.int32, (NH * CH, CH), 0) & (CH - 1)
            cols = lax.broadcasted_iota(jnp.int32, (NH * CH, CH), 1)
            accws_ref[...] = jnp.where(cols <= rows, accws_ref[...], 0.0)
            bs = bs_s[...]
            hi = bs.astype(BF16)
            r1_ = bs - hi.astype(F32)
            mid = r1_.astype(BF16)
            lo = (r1_ - mid.astype(F32)).astype(BF16)
            ev = esel_ref[...]
            accbs_ref[...] = _dot(hi, ev) + _dot(mid, ev) + _dot(lo, ev)
            for j in range(NDEV):
                st_win[...] = acc_win[j].astype(BF16)
                pltpu.sync_copy(st_win, gwin_ref.at[j])
            for j in range(D // WOB):
                st_wout[...] = acc_wout[j * WOB:(j + 1) * WOB, :].astype(BF16)
                pltpu.sync_copy(st_wout, gwout_ref.at[pl.ds(j * WOB, WOB)])

    rev = lambda w: pl.BlockSpec((tm, w), lambda i: (nt - 1 - i, 0))
    outs = pl.pallas_call(
        body,
        name="mix_bwd",
        grid=(nt,),
        in_specs=[rev(D), rev(D), rev(2 * DB), rev(DA), rev(D), rev(D), rev(D), rev(DA), rev(DA), rev(DA), rev(DA),
                  rev(CH), rev(DB), rev(DB), _full((8, D)), _full((1, D)),
                  _resident((NDEV, D, WIN_B)), _full((1, DA)), _full((1, DA)), _full((NH * CH, CH)),
                  _full((NH * CH, CH)), _full((HALO, DB)), _full((1, DB)), _full((1, DB)), _full((1, DA)),
                  _full((1, DB)), _resident((D, D)), _full((DB, DB)), _full((DA, CH)), HBM],
        out_specs=[rev(D), _full((8, D)), _full((1, DIN)), _full((8, DA)), _full((CH, CH)),
                   _full((NH * CH, CH)), _full((HALO, DB)), HBM, HBM],
        out_shape=[jax.ShapeDtypeStruct((T, D), F32), jax.ShapeDtypeStruct((8, D), F32),
                   jax.ShapeDtypeStruct((1, DIN), F32), jax.ShapeDtypeStruct((8, DA), F32),
                   jax.ShapeDtypeStruct((CH, CH), F32), jax.ShapeDtypeStruct((NH * CH, CH), F32),
                   jax.ShapeDtypeStruct((HALO, DB), F32),
                   jax.ShapeDtypeStruct((NDEV, D, WIN_B), BF16), jax.ShapeDtypeStruct((D, D), BF16)],
        scratch_shapes=[pltpu.VMEM((tm + HALO, DB), F32), pltpu.VMEM((7, tm + SH_ROWS, DB), F32),
                        pltpu.VMEM((CH, DA), F32), pltpu.VMEM((NDEV, D, WIN_B), F32), pltpu.VMEM((D, D), F32),
                        pltpu.VMEM((D, WIN_B), BF16), pltpu.VMEM((WOB, D), BF16)],
        compiler_params=pltpu.CompilerParams(dimension_semantics=("arbitrary",), vmem_limit_bytes=VMEM_LIMIT),
    )(dx1, x, zvg, mixed, o, hb, yb, gu, dgu, dgv, vhat, rslb, yhat, rsg, mod, g1, win, lng, lnb, wcat, wcat_t, cw,
      gng, gnb, oga, ogb, wout, pm, esel, after)
    return outs[:7], outs[7:]


def _wgrad_rows(a, b, bm, tk, name):
    T, M = a.shape
    N = b.shape[1]
    nk = T // tk

    def body(a_ref, b_ref, o_ref, acc):
        k = pl.program_id(1)

        @pl.when(k == 0)
        def _():
            acc[...] = jnp.zeros((bm, N), F32)

        acc[...] += _dot_tn(a_ref[...], b_ref[...])

        @pl.when(k == nk - 1)
        def _():
            o_ref[...] = acc[...].astype(BF16)

    return pl.pallas_call(
        body, name=name, grid=(M // bm, nk),
        in_specs=[pl.BlockSpec((tk, bm), lambda j, k: (k, j)), pl.BlockSpec((tk, N), lambda j, k: (k, 0))],
        out_specs=pl.BlockSpec((bm, N), lambda j, k: (j, 0)),
        out_shape=jax.ShapeDtypeStruct((M, N), BF16),
        scratch_shapes=[pltpu.VMEM((bm, N), F32)],
        compiler_params=pltpu.CompilerParams(dimension_semantics=("arbitrary", "arbitrary"),
                                             vmem_limit_bytes=VMEM_LIMIT),
    )(a, b)


def _small_copy(src, dst, ss, rs, k, to):
    return pltpu.make_async_remote_copy(src_ref=src, dst_ref=dst, send_sem=ss.at[k], recv_sem=rs.at[k],
                                        device_id=to, device_id_type=MESH)


def _gather(c_row, ada_w, ada_b8, ada_f_w, ada_f_b8, conv_s, shards):
    nw = len(shards)

    def body(c_ref, adaw_ref, adab_ref, adafw_ref, adafb_ref, conv_ref, *rest):
        w_f32 = rest[:nw]
        call_ref, cparts_ref, cfparts_ref, convg_ref = rest[nw:nw + 4]
        w_out = rest[nw + 4:2 * nw + 4]
        part_s, partf_s = rest[2 * nw + 4:2 * nw + 6]
        w_in = rest[2 * nw + 6:3 * nw + 6]
        wss, wrs, lsem, s1, r1, s2, r2, s3, r3, s4, r4 = rest[3 * nw + 6:]
        x, y, c, idx = _place()
        me = (x, y, c)
        for a in range(nw):
            w_in[a][...] = w_f32[a][...].astype(BF16)
        ag = _AllGather(w_in, w_out, wss, wrs, lsem)
        ag.start()
        call_ref[pl.ds(idx, 1), :] = c_ref[...]
        convg_ref[idx] = conv_ref[...]
        ph1 = []
        for k in range(1, NDEV):
            to = _dev(idx ^ k)
            ph1.append(_small_copy(c_ref, call_ref.at[pl.ds(idx, 1)], s1, r1, k - 1, to))
            ph1.append(_small_copy(conv_ref, convg_ref.at[idx], s2, r2, k - 1, to))
        for cp in ph1:
            cp.start()
        for k in range(1, NDEV):
            src_dev = idx ^ k
            _small_copy(c_ref, call_ref.at[pl.ds(src_dev, 1)], s1, r1, k - 1, me).wait_recv()
            _small_copy(conv_ref, convg_ref.at[src_dev], s2, r2, k - 1, me).wait_recv()
        call = call_ref[...]
        cact = (call * _sig(call))
        part_s[...] = jnp.dot(cact, adaw_ref[...], preferred_element_type=F32,
                              precision=lax.Precision.HIGHEST) + adab_ref[pl.ds(idx, 1), :]
        partf_s[...] = jnp.dot(cact, adafw_ref[...], preferred_element_type=F32,
                               precision=lax.Precision.HIGHEST) + adafb_ref[pl.ds(idx, 1), :]
        cparts_ref[pl.ds(idx, 1), :] = part_s[pl.ds(idx, 1), :]
        cfparts_ref[pl.ds(idx, 1), :] = partf_s[pl.ds(idx, 1), :]
        ph2 = []
        for k in range(1, NDEV):
            t = idx ^ k
            ph2.append(_small_copy(part_s.at[pl.ds(t, 1)], cparts_ref.at[pl.ds(idx, 1)], s3, r3, k - 1, _dev(t)))
            ph2.append(_small_copy(partf_s.at[pl.ds(t, 1)], cfparts_ref.at[pl.ds(idx, 1)], s4, r4, k - 1, _dev(t)))
        for cp in ph2:
            cp.start()
        for k in range(1, NDEV):
            src_dev = idx ^ k
            _small_copy(part_s.at[pl.ds(0, 1)], cparts_ref.at[pl.ds(src_dev, 1)], s3, r3, k - 1, me).wait_recv()
            _small_copy(partf_s.at[pl.ds(0, 1)], cfparts_ref.at[pl.ds(src_dev, 1)], s4, r4, k - 1, me).wait_recv()
        for cp in ph1 + ph2:
            cp.wait_send()
        ag.forward()
        ag.finish()

    dma7 = pltpu.SemaphoreType.DMA((NDEV - 1,))
    outs = pl.pallas_call(
        body,
        name="gather_weights",
        in_specs=[VM] * (6 + nw),
        out_specs=[VM] * 4 + [HBM] * nw,
        out_shape=[jax.ShapeDtypeStruct((NDEV, D), F32), jax.ShapeDtypeStruct((NDEV, ada_w.shape[1]), F32),
                   jax.ShapeDtypeStruct((NDEV, ada_f_w.shape[1]), F32),
                   jax.ShapeDtypeStruct((NDEV,) + conv_s.shape, F32)]
                  + [jax.ShapeDtypeStruct((NDEV,) + s.shape, BF16) for s in shards],
        scratch_shapes=[pltpu.VMEM((NDEV, ada_w.shape[1]), F32), pltpu.VMEM((NDEV, ada_f_w.shape[1]), F32)]
                       + [pltpu.VMEM(s.shape, BF16) for s in shards] + AG_SEMS(nw) + [dma7] * 8,
        compiler_params=pltpu.CompilerParams(vmem_limit_bytes=VMEM_LIMIT),
    )(c_row, ada_w, ada_b8, ada_f_w, ada_f_b8, conv_s, *shards)
    return outs[0], outs[1], outs[2], outs[3], outs[4:]


_VEC_AT = {
    "norm1_g": (8, 0, D), "a_ln_g": (11, 0, DA), "a_ln_b": (11, DA, DA), "a_spatial_b": (12, 0, D),
    "b_conv_b": (13, 0, DB), "b_gn_g": (13, DB, DB), "b_gn_b": (14, 0, DB), "out_norm_a_g": (14, DB, DA),
    "out_norm_b_g": (15, 0, DB), "norm2_g": (16, 0, D), "norm_f_g": (17, 0, D),
}
_LOSS_ROW = 18
_CW_ROW = 24


def _reduce_small(acc_f, acc_v, acc_b, acc_a, acc_bs, acc_cw, dws, after, pair_grads):
    npg = len(pair_grads)

    def body(accf_ref, accv_ref, accb_ref, acca_ref, accbs_ref, acccw_ref, dws_ref, after_ref, *rest):
        pg = rest[:npg]
        vsum_ref, dcond_ref, wssum_ref = rest[npg:npg + 3]
        pq = rest[npg + 3:2 * npg + 3]
        vloc, vbuf, wbuf, wown, s1, r1, s2, r2, s3, r3 = rest[2 * npg + 3:2 * npg + 13]
        pland = rest[2 * npg + 13:3 * npg + 13]
        pstage = rest[3 * npg + 13:4 * npg + 13]
        ps, pr, pls, pss = rest[4 * npg + 13:]
        x, y, c, idx = _place()
        me = (x, y, c)
        sibling = (x, y, 1 - c)
        chips = [(1 - x, y), (x, 1 - y), (1 - x, 1 - y)]
        blk = lambda p: 4 * p[0] + 2 * p[1] + p[2]
        give = [blk((*ch, 1 - c)) for ch in chips] + [blk(sibling)]
        pair = [pltpu.make_async_remote_copy(src_ref=pg[a].at[b], dst_ref=pland[a].at[j], send_sem=ps.at[a, j],
                                             recv_sem=pr.at[a, j], device_id=sibling, device_id_type=MESH)
                for a in range(npg) for j, b in enumerate(give)]
        loads = [pltpu.make_async_copy(pg[a].at[blk((*ch, c))], pstage[a].at[j], pls.at[a, j])
                 for a in range(npg) for j, ch in enumerate(chips)]
        for cp in pair + loads:
            cp.start()
        vloc[...] = jnp.zeros((NVEC, D), F32)
        vloc[0:1, :] = accv_ref[1:2, :]
        vloc[1:2, :] = accv_ref[2:3, :]
        vloc[2:3, :] = accv_ref[0:1, :]
        vloc[3:4, :] = accf_ref[4:5, :]
        vloc[4:5, :] = accf_ref[5:6, :]
        vloc[5:6, :] = accf_ref[3:4, :]
        vloc[6:7, :] = accf_ref[0:1, :]
        vloc[7:8, :] = accf_ref[1:2, :]
        vloc[8:9, :] = accv_ref[3:4, :]
        vloc[9:10, :] = accb_ref[:, 0:D]
        vloc[10:11, :] = accb_ref[:, D:]
        vloc[11:12, 0:DA] = acca_ref[0:1, :]
        vloc[11:12, DA:] = acca_ref[1:2, :]
        bst = accbs_ref[...].T
        for h in range(NH):
            vloc[12:13, h * CH:(h + 1) * CH] = bst[h:h + 1, :]
        vloc[13:14, 0:DB] = acca_ref[4:5, :]
        vloc[13:14, DB:] = acca_ref[5:6, :]
        vloc[14:15, 0:DB] = acca_ref[6:7, :]
        vloc[14:15, DB:] = acca_ref[2:3, :]
        vloc[15:16, 0:DB] = acca_ref[3:4, :]
        vloc[16:17, :] = accf_ref[6:7, :]
        vloc[17:18, :] = accf_ref[2:3, :]
        vloc[_LOSS_ROW:_LOSS_ROW + 1, :] = accf_ref[7:8, :]
        vloc[_CW_ROW:_CW_ROW + HALO // 2, 0:DB] = acccw_ref[0:HALO // 2, :]
        vloc[_CW_ROW:_CW_ROW + HALO // 2, DB:] = acccw_ref[HALO // 2:, :]
        vbuf[idx] = vloc[...]
        rows_of = lambda t: pl.ds(pl.multiple_of(t * CH, CH), CH)
        wbuf[0] = dws_ref[rows_of(idx), :]
        sm = []
        for k in range(1, NDEV):
            t = idx ^ k
            sm.append(_small_copy(vloc, vbuf.at[idx], s1, r1, k - 1, _dev(t)))
            sm.append(_small_copy(dws_ref.at[rows_of(t)], wbuf.at[k], s2, r2, k - 1, _dev(t)))
        for cp in sm:
            cp.start()
        for k in range(1, NDEV):
            _small_copy(dws_ref.at[rows_of(0)], wbuf.at[k], s2, r2, k - 1, me).wait_recv()
        ws = wbuf[0]
        for k in range(1, NDEV):
            ws = ws + wbuf[k]
        wown[...] = ws
        wssum_ref[rows_of(idx), :] = ws
        ag = [_small_copy(wown, wssum_ref.at[rows_of(idx)], s3, r3, k - 1, _dev(idx ^ k)) for k in range(1, NDEV)]
        for cp in ag:
            cp.start()
        for cp in loads:
            cp.wait()
        for cp in pair:
            cp.wait_recv()
        stores = []
        for a in range(npg):
            for j in range(3):
                pstage[a][j] = (pstage[a][j].astype(F32) + pland[a][j].astype(F32)).astype(BF16)
                stores.append(pltpu.make_async_copy(pstage[a].at[j], pq[a].at[j], pss.at[a, j]))
            stores.append(pltpu.make_async_copy(pland[a].at[3], pq[a].at[3], pss.at[a, 3]))
        for cp in stores:
            cp.start()
        for k in range(1, NDEV):
            _small_copy(vloc, vbuf.at[idx ^ k], s1, r1, k - 1, me).wait_recv()
        vs = vbuf[0]
        for d in range(1, NDEV):
            vs = vs + vbuf[d]
        vsum_ref[...] = vs
        for d in range(NDEV):
            dcond_ref[d] = vbuf[d, 0:8, :]
        for k in range(1, NDEV):
            _small_copy(wown, wssum_ref.at[rows_of(idx ^ k)], s3, r3, k - 1, me).wait_recv()
        for cp in sm + ag:
            cp.wait_send()
        for cp in stores:
            cp.wait()
        for cp in pair:
            cp.wait_send()

    dma7 = pltpu.SemaphoreType.DMA((NDEV - 1,))
    dma4 = pltpu.SemaphoreType.DMA((npg, 4))
    outs = pl.pallas_call(
        body,
        name="reduce_small",
        in_specs=[VM] * 7 + [HBM] + [HBM] * npg,
        out_specs=[VM, VM, VM] + [HBM] * npg,
        out_shape=[jax.ShapeDtypeStruct((NVEC, D), F32), jax.ShapeDtypeStruct((NDEV, 8, D), F32),
                   jax.ShapeDtypeStruct(dws.shape, F32)]
                  + [jax.ShapeDtypeStruct((4,) + g.shape[1:], g.dtype) for g in pair_grads],
        scratch_shapes=[pltpu.VMEM((NVEC, D), F32), pltpu.VMEM((NDEV, NVEC, D), F32),
                        pltpu.VMEM((NDEV, CH, CH), F32), pltpu.VMEM((CH, CH), F32)] + [dma7] * 6
                       + [pltpu.VMEM((4,) + g.shape[1:], g.dtype) for g in pair_grads]
                       + [pltpu.VMEM((3,) + g.shape[1:], g.dtype) for g in pair_grads] + [dma4] * 4,
        compiler_params=pltpu.CompilerParams(vmem_limit_bytes=VMEM_LIMIT),
    )(acc_f, acc_v, acc_b, acc_a, acc_bs, acc_cw, dws, after, *pair_grads)
    return outs[0], outs[1], outs[2], outs[3:]


HBM_ONLY = pl.BlockSpec(memory_space=pltpu.HBM)
SEM = pl.BlockSpec(memory_space=pltpu.SEMAPHORE)
EFFECT = pltpu.SideEffectType.DATAFLOW_SIDE_EFFECTING


def _rs_copies(g_refs, land_refs, sems, chips):
    x, y, c, idx = _place()
    if chips:
        routes = [(j, j, (*ch, c)) for j, ch in enumerate([(1 - x, y), (x, 1 - y), (1 - x, 1 - y)])]
    else:
        routes = [(idx ^ k, k - 1, _dev(idx ^ k)) for k in range(1, NDEV)]
    cps = []
    for src, dst, to in routes:
        for a in range(len(g_refs)):
            n = len(cps)
            cps.append(pltpu.make_async_remote_copy(
                src_ref=g_refs[a].at[src], dst_ref=land_refs[a].at[dst], send_sem=sems[2 * n],
                recv_sem=sems[2 * n + 1], device_id=to, device_id_type=MESH))
    return cps


def _rs_start(grads, name, after=(), chips=False):
    nw = len(grads)
    npeer = 3 if chips else NDEV - 1
    nsem = 2 * nw * npeer
    lands = [lax.empty((npeer,) + g.shape[1:], g.dtype) for g in grads]

    def body(*refs):
        g_refs, land_refs = refs[:nw], refs[nw:2 * nw]
        sems = refs[2 * nw + len(after):2 * nw + len(after) + nsem]
        token = refs[-1]
        for cp in _rs_copies(g_refs, land_refs, sems, chips):
            cp.start()
        token[...] = jnp.zeros_like(token)

    outs = pl.pallas_call(
        body, name=name,
        out_shape=(*[pltpu.SemaphoreType.DMA(())] * nsem,
                   *[pltpu.HBM(g.shape, g.dtype) for g in grads], *[pltpu.HBM(l.shape, l.dtype) for l in lands],
                   jax.ShapeDtypeStruct((8, CH), F32)),
        in_specs=[HBM_ONLY] * (2 * nw) + [HBM] * len(after),
        out_specs=(*[SEM] * nsem, *[HBM_ONLY] * (2 * nw), VM),
        input_output_aliases={i: nsem + i for i in range(2 * nw)},
        compiler_params=pltpu.CompilerParams(has_side_effects=EFFECT),
    )(*[pltpu.with_memory_space_constraint(g, pltpu.HBM) for g in grads],
      *[pltpu.with_memory_space_constraint(l, pltpu.HBM) for l in lands], *after)
    return outs[:nsem], outs[nsem:nsem + nw], outs[nsem + nw:nsem + 2 * nw], outs[-1]


def _rs_wait(sems, g_thru, land_thru, after, name, chips=False):
    nw = len(g_thru)
    nsem = len(sems)

    def body(*refs):
        g_refs, land_refs = refs[:nw], refs[nw:2 * nw]
        for cp in _rs_copies(g_refs, land_refs, refs[2 * nw:2 * nw + nsem], chips):
            cp.wait_send()
            cp.wait_recv()

    outs = pl.pallas_call(
        body, name=name,
        out_shape=tuple(pltpu.HBM(a.shape, a.dtype) for a in list(g_thru) + list(land_thru)),
        in_specs=[HBM_ONLY] * (2 * nw) + [SEM] * nsem + [HBM] * len(after),
        out_specs=tuple([HBM_ONLY] * (2 * nw)),
        input_output_aliases={i: i for i in range(2 * nw)},
        compiler_params=pltpu.CompilerParams(has_side_effects=EFFECT),
    )(*g_thru, *land_thru, *sems, *after)
    return outs[:nw], outs[nw:]


def _adamw(w, g, m, v):
    m2 = ADAM_B1 * m + (1.0 - ADAM_B1) * g
    v2 = ADAM_B2 * v + (1.0 - ADAM_B2) * (g * g)
    m_hat = m2 / (1.0 - ADAM_B1 ** ADAM_STEP)
    v_hat = v2 / (1.0 - ADAM_B2 ** ADAM_STEP)
    delta = -ADAM_LR * (m_hat / (jnp.sqrt(v_hat) + ADAM_EPS) + ADAM_WD * w)
    return delta, m2, v2


def _adam_big(r, w, m, v, rb, name, own, after=None, sib=None):
    R, C = w.shape
    ns = r.shape[0]
    g_all, idx1 = own

    def body(idx_ref, r_ref, own_ref, *refs):
        w_ref, m_ref, v_ref, g_ref, d_ref, m2_ref, v2_ref = refs[len(refs) - 7:]
        g = own_ref[0].astype(F32)
        if sib is not None:
            g = g + refs[0][0].astype(F32)
        for k in range(ns):
            g = g + r_ref[k].astype(F32)
        g_ref[...] = g
        d_ref[...], m2_ref[...], v2_ref[...] = _adamw(w_ref[...], g, m_ref[...], v_ref[...])

    t2 = pl.BlockSpec((rb, C), lambda i, idx_ref: (i, 0))
    sd = jax.ShapeDtypeStruct((R, C), F32)
    extra_specs = ([pl.BlockSpec((1, rb, C), lambda i, idx_ref: (3, i, 0))] if sib is not None else []) \
        + ([HBM] if after is not None else [])
    extra = ([sib] if sib is not None else []) + ([after] if after is not None else [])
    return pl.pallas_call(
        body, name=name,
        grid_spec=pltpu.PrefetchScalarGridSpec(
            num_scalar_prefetch=1, grid=(R // rb,),
            in_specs=[pl.BlockSpec((ns, rb, C), lambda i, idx_ref: (0, i, 0)),
                      pl.BlockSpec((1, rb, C), lambda i, idx_ref: (idx_ref[0], i, 0))] + extra_specs + [t2, t2, t2],
            out_specs=[t2, t2, t2, t2]),
        out_shape=[sd, sd, sd, sd],
        compiler_params=pltpu.CompilerParams(dimension_semantics=("arbitrary",), vmem_limit_bytes=VMEM_LIMIT),
    )(idx1, r, g_all, *extra, w, m, v)


def _adam_ada(cact_t, dcs, w, m, v, rb, name):
    R, C = w.shape

    def body(ct_ref, dc_ref, w_ref, m_ref, v_ref, g_ref, d_ref, m2_ref, v2_ref):
        g = jnp.dot(ct_ref[...], dc_ref[...], preferred_element_type=F32, precision=lax.Precision.HIGHEST)
        g_ref[...] = g
        d_ref[...], m2_ref[...], v2_ref[...] = _adamw(w_ref[...], g, m_ref[...], v_ref[...])

    t2 = pl.BlockSpec((rb, C), lambda i: (i, 0))
    sd = jax.ShapeDtypeStruct((R, C), F32)
    return pl.pallas_call(
        body, name=name, grid=(R // rb,),
        in_specs=[pl.BlockSpec((rb, NDEV), lambda i: (i, 0)), _full((NDEV, C)), t2, t2, t2],
        out_specs=[t2, t2, t2, t2], out_shape=[sd, sd, sd, sd],
        compiler_params=pltpu.CompilerParams(dimension_semantics=("arbitrary",), vmem_limit_bytes=VMEM_LIMIT),
    )(cact_t, dcs, w, m, v)


_SMALL = ["ada_b", "ada_f_b", "norm1_g", "b_in", "a_ln_g", "a_ln_b", "a_spatial_b", "b_conv_b", "b_gn_g", "b_gn_b",
          "out_norm_a_g", "out_norm_b_g", "norm2_g", "norm_f_g", "a_spatial_w", "b_conv_w"]


def _adam_small(vsum, wssum, gcw, params):
    names = _SMALL
    flat = []
    for n in names:
        flat += list(params[n])

    def body(vs_ref, ws_ref, gcw_ref, *rest):
        ins = rest[:3 * len(names)]
        outs = rest[3 * len(names):]
        for pi, n in enumerate(names):
            w_ref, m_ref, v_ref = ins[3 * pi:3 * pi + 3]
            g_ref, d_ref, m2_ref, v2_ref = outs[4 * pi:4 * pi + 4]
            if n in ("ada_b", "ada_f_b", "b_in"):
                row0 = {"ada_b": 0, "ada_f_b": 6, "b_in": 9}[n]
                pieces = [(vs_ref[row0 + r:row0 + r + 1, :], slice(r * D, (r + 1) * D))
                          for r in range(w_ref.shape[1] // D)]
            elif n == "a_spatial_w":
                pieces = [(ws_ref[...], slice(None))]
            elif n == "b_conv_w":
                pieces = [(gcw_ref[...], slice(None))]
            else:
                row, off, width = _VEC_AT[n]
                pieces = [(vs_ref[row:row + 1, off:off + width], slice(None))]
            for g, cs in pieces:
                g_ref[:, cs] = g
                d_ref[:, cs], m2_ref[:, cs], v2_ref[:, cs] = _adamw(w_ref[:, cs], g, m_ref[:, cs], v_ref[:, cs])

    out_shape = []
    for n in names:
        out_shape += [jax.ShapeDtypeStruct(params[n][0].shape, F32)] * 4
    outs = pl.pallas_call(
        body, name="adam_small",
        in_specs=[VM] * (3 + len(flat)), out_specs=[VM] * len(out_shape), out_shape=out_shape,
        compiler_params=pltpu.CompilerParams(vmem_limit_bytes=VMEM_LIMIT),
    )(vsum, wssum, gcw, *flat)
    return {n: outs[4 * pi:4 * pi + 4] for pi, n in enumerate(names)}


def _token_tile(T, want):
    return want if T % want == 0 else T


def kernel(x, c, ada_w, ada_b, norm1_g, w_in, b_in, a_ln_g, a_ln_b, a_spatial_w, a_spatial_b, b_conv_w, b_conv_b, b_gn_g, b_gn_b, out_norm_a_g, out_norm_b_g, w_out, norm2_g, w_ffn_in, w_ffn_out, ada_f_w, ada_f_b, norm_f_g, loss_target, m_ada_w, m_ada_b, m_norm1_g, m_w_in, m_b_in, m_a_ln_g, m_a_ln_b, m_a_spatial_w, m_a_spatial_b, m_b_conv_w, m_b_conv_b, m_b_gn_g, m_b_gn_b, m_out_norm_a_g, m_out_norm_b_g, m_w_out, m_norm2_g, m_w_ffn_in, m_w_ffn_out, m_ada_f_w, m_ada_f_b, m_norm_f_g, v_ada_w, v_ada_b, v_norm1_g, v_w_in, v_b_in, v_a_ln_g, v_a_ln_b, v_a_spatial_w, v_a_spatial_b, v_b_conv_w, v_b_conv_b, v_b_gn_g, v_b_gn_b, v_out_norm_a_g, v_out_norm_b_g, v_w_out, v_norm2_g, v_w_ffn_in, v_w_ffn_out, v_ada_f_w, v_ada_f_b, v_norm_f_g):
    T = x.shape[1]
    idx = 4 * lax.axis_index("x") + 2 * lax.axis_index("y") + lax.axis_index("c")
    x2d = x.reshape(T, D)
    tgt = loss_target.reshape(T, D)

    conv_s = jnp.pad(b_conv_w[0], ((0, HALO - KW), (0, 0)))
    call, cparts, cfparts, convg, (win_g, wout_g) = _gather(
        c, ada_w[0], ada_b.reshape(NDEV, -1), ada_f_w, ada_f_b.reshape(NDEV, -1), conv_s,
        [w_in[0], w_out[0]])
    wout = wout_g.reshape(D, D)
    mod = jnp.concatenate([cparts.reshape(6, D), cfparts.reshape(2, D)], axis=0)
    cw = jnp.transpose(convg, (1, 0, 2)).reshape(HALO, DB)

    tril = jnp.tril(jnp.ones((CH, CH), dtype=bool))
    wsm = jnp.where(tril[None], a_spatial_w[0], 0.0).astype(BF16)
    wcat = wsm.reshape(NH * CH, CH)
    wcat_t = jnp.transpose(wsm, (0, 2, 1)).reshape(NH * CH, CH)
    bsf = jnp.repeat(a_spatial_b[0].T, DA // NH, axis=1)
    pm = jnp.asarray(_GROUP_MEAN, BF16)
    esel = jnp.asarray(_HEAD_SELECT, BF16)

    tm = _token_tile(T, 256)
    tk = _token_tile(T, 2048)
    (x1, hb, zvg, mixed, yb, o, gu, dgelu_u, dgelu_v, vhat, rslb, yhat, rsg), (wfi_g, wfo_g) = _mix_fwd(
        x2d, mod, norm1_g, win_g, b_in, a_ln_g, a_ln_b, wcat, bsf, cw, b_conv_b, b_gn_g, b_gn_b, out_norm_a_g,
        out_norm_b_g, wout, pm, [w_ffn_in[0].T, w_ffn_out[0]], _token_tile(T, 512))
    dx1, h2b, dgu, act, dxg, acc_f = _ffn(x1, tgt, mod, norm2_g, norm_f_g.reshape(1, D),
                                          wfi_g.reshape(2 * DFF, D), wfo_g.reshape(DFF, D), tm)
    g_wfi = _wgrad_rows(dgu, h2b, 2 * WFI_B, tk, "wgrad_ffn_in").reshape(NDEV, WFI_B, D)
    g_wfo = _wgrad_rows(act, dxg, 2 * WFI_B, tk, "wgrad_ffn_out").reshape(NDEV, DFF // NDEV, D)
    f_sems, f_thru, f_land, f_token = _rs_start([g_wfi, g_wfo], "rs_ffn_start")
    (gx, acc_v, acc_b, acc_a, acc_bs, acc_ws, acc_cw), (g_win, g_wout) = _mix_bwd(
        dx1, x2d, zvg, mixed, o, hb, yb, gu, dgelu_u, dgelu_v, vhat, rslb, yhat, rsg, mod, norm1_g, win_g, a_ln_g,
        a_ln_b, wcat, wcat_t, cw, b_gn_g, b_gn_b, out_norm_a_g, out_norm_b_g, wout, pm, esel, f_token, tm)
    (g_wfi_d, g_wfo_d), (r_wfi, r_wfo) = _rs_wait(f_sems, f_thru, f_land, [acc_v], "rs_ffn_wait")
    g_wout = g_wout.reshape(NDEV, D // NDEV, D)

    vsum, dcond_all, wssum, (q_win, q_wout) = _reduce_small(acc_f, acc_v, acc_b, acc_a, acc_bs, acc_cw, acc_ws,
                                                            g_wfi_d, [g_win, g_wout])
    sems, g_thru, land_thru, token = _rs_start([q_win, q_wout], "rs_mix_start", after=(vsum,), chips=True)

    own = lambda g: (g, jnp.reshape(idx, (1,)).astype(jnp.int32))
    res = {}
    upd_wfi_t = _adam_big(r_wfi, w_ffn_in[0].T, m_w_ffn_in[0].T, v_w_ffn_in[0].T, WFI_B // 2, "adam_w_ffn_in",
                          own=own(g_wfi_d), after=token)
    res["w_ffn_in"] = tuple(a.T for a in upd_wfi_t)
    res["w_ffn_out"] = _adam_big(r_wfo, w_ffn_out[0], m_w_ffn_out[0], v_w_ffn_out[0], DFF // NDEV // 2,
                                 "adam_w_ffn_out", own=own(g_wfo_d), after=token)
    cact_t = (call * jax.nn.sigmoid(call)).T
    dcond = dcond_all.reshape(NDEV, 8 * D)
    nada = ada_w.shape[2]
    nadf = ada_f_w.shape[1]
    dcs = lax.dynamic_slice(dcond, (0, idx * nada), (NDEV, nada))
    dcfs = lax.dynamic_slice(dcond, (0, 6 * D + idx * nadf), (NDEV, nadf))
    res["ada_w"] = _adam_ada(cact_t, dcs, ada_w[0], m_ada_w[0], v_ada_w[0], 512, "adam_ada_w")
    res["ada_f_w"] = _adam_ada(cact_t, dcfs, ada_f_w, m_ada_f_w, v_ada_f_w, 512, "adam_ada_f_w")
    ncw = b_conv_w.shape[2]
    gcw = jnp.concatenate([lax.dynamic_slice(vsum, (_CW_ROW, idx * ncw), (HALO // 2, ncw)),
                           lax.dynamic_slice(vsum, (_CW_ROW, DB + idx * ncw), (HALO // 2, ncw))], axis=0)[:KW]
    two = lambda a: a.reshape(1, -1) if a.ndim == 1 else a.reshape(-1, a.shape[-1])
    small_in = {
        "ada_b": (ada_b, m_ada_b, v_ada_b), "ada_f_b": (ada_f_b, m_ada_f_b, v_ada_f_b),
        "norm1_g": (norm1_g, m_norm1_g, v_norm1_g), "b_in": (b_in, m_b_in, v_b_in),
        "a_ln_g": (a_ln_g, m_a_ln_g, v_a_ln_g), "a_ln_b": (a_ln_b, m_a_ln_b, v_a_ln_b),
        "a_spatial_b": (a_spatial_b.reshape(1, D), m_a_spatial_b.reshape(1, D), v_a_spatial_b.reshape(1, D)),
        "b_conv_b": (b_conv_b, m_b_conv_b, v_b_conv_b), "b_gn_g": (b_gn_g, m_b_gn_g, v_b_gn_g),
        "b_gn_b": (b_gn_b, m_b_gn_b, v_b_gn_b), "out_norm_a_g": (out_norm_a_g, m_out_norm_a_g, v_out_norm_a_g),
        "out_norm_b_g": (out_norm_b_g, m_out_norm_b_g, v_out_norm_b_g),
        "norm2_g": (norm2_g, m_norm2_g, v_norm2_g), "norm_f_g": (norm_f_g, m_norm_f_g, v_norm_f_g),
        "a_spatial_w": (a_spatial_w, m_a_spatial_w, v_a_spatial_w),
        "b_conv_w": (b_conv_w[0], m_b_conv_w[0], v_b_conv_w[0]),
    }
    small_in = {n: tuple(two(a) for a in t) for n, t in small_in.items()}
    res.update(_adam_small(vsum, wssum, gcw, small_in))
    (q_win_d, q_wout_d), (r_win, r_wout) = _rs_wait(
        sems, g_thru, land_thru,
        [upd_wfi_t[0], res["w_ffn_out"][0], res["ada_w"][0], res["ada_f_w"][0], res["norm_f_g"][0]],
        "rs_mix_wait", chips=True)
    res["w_in"] = _adam_big(r_win, w_in[0], m_w_in[0], v_w_in[0], 512, "adam_w_in", own=own(g_win), sib=q_win_d)
    res["w_out"] = _adam_big(r_wout, w_out[0], m_w_out[0], v_w_out[0], D // NDEV, "adam_w_out", own=own(g_wout),
                             sib=q_wout_d)

    loss = 0.5 / D * jnp.sum(vsum[_LOSS_ROW])
    shapes = {"ada_w": ada_w, "ada_b": ada_b, "norm1_g": norm1_g, "w_in": w_in, "b_in": b_in, "a_ln_g": a_ln_g,
              "a_ln_b": a_ln_b, "a_spatial_w": a_spatial_w, "a_spatial_b": a_spatial_b, "b_conv_w": b_conv_w,
              "b_conv_b": b_conv_b, "b_gn_g": b_gn_g, "b_gn_b": b_gn_b, "out_norm_a_g": out_norm_a_g,
              "out_norm_b_g": out_norm_b_g, "w_out": w_out, "norm2_g": norm2_g, "w_ffn_in": w_ffn_in,
              "w_ffn_out": w_ffn_out, "ada_f_w": ada_f_w, "ada_f_b": ada_f_b, "norm_f_g": norm_f_g}
    order = list(shapes)
    outs = [loss, gx.reshape(x.shape)]
    for which in range(4):
        outs += [res[n][which].reshape(shapes[n].shape) for n in order]
    return tuple(outs)
```

```python
import math

import numpy as np

import jax
import jax.numpy as jnp
from jax import lax
from jax.experimental import pallas as pl
from jax.experimental.pallas import tpu as pltpu

F32 = jnp.float32
BF16 = jnp.bfloat16

D = 1024
DA = 512
DB = 512
DIN = 2048
DFF = 2816
NH = 8
CH = 128
KW = 31
HALO = 32
NDEV = 8
WIN_B = DIN // NDEV
WFI_B = 2 * DFF // NDEV
EPS = 1e-6
NVEC = 40
VMEM_LIMIT = 56 * 1024 * 1024

ADAM_LR, ADAM_B1, ADAM_B2, ADAM_EPS, ADAM_WD, ADAM_STEP = 0.001, 0.9, 0.999, 1e-08, 0.01, 10

MESH = pl.DeviceIdType.MESH

_LANE = np.arange(DB)
_GROUP_MEAN = np.where((_LANE[:, None] >> 6) == (_LANE[None, :] >> 6), 1.0 / 64.0, 0.0).astype(np.float32)
_HEAD_SELECT = np.where((_LANE[:, None] >> 6) == np.arange(CH)[None, :], 1.0, 0.0).astype(np.float32)


def _dot(a, b):
    return jnp.dot(a, b, preferred_element_type=F32)


def _dot_nt(a, b):
    return lax.dot_general(a, b, (((1,), (1,)), ((), ())), preferred_element_type=F32)


def _dot_tn(a, b):
    return lax.dot_general(a, b, (((0,), (0,)), ((), ())), preferred_element_type=F32)


def _rs(v):
    return lax.rsqrt(jnp.mean(v * v, axis=-1, keepdims=True) + EPS)


def _sig(v):
    return 1.0 / (1.0 + jnp.exp(-v))


_INV_SQRT2 = 1.0 / math.sqrt(2.0)
_INV_SQRT2PI = 1.0 / math.sqrt(2.0 * math.pi)


def _gelu_parts(v):
    cdf = 0.5 * (1.0 + lax.erf(v * _INV_SQRT2))
    pdf = jnp.exp(-0.5 * v * v) * _INV_SQRT2PI
    return v * cdf, cdf + v * pdf


def _grp_mean(v, pm):
    hi = v.astype(BF16)
    lo = (v - hi.astype(F32)).astype(BF16)
    return _dot(hi, pm) + _dot(lo, pm)


def _colsum(v):
    return jnp.sum(v, axis=0, keepdims=True)


def _full(shape):
    nd = len(shape)
    return pl.BlockSpec(shape, lambda *_: (0,) * nd)


def _resident(shape):
    nd = len(shape)
    return pl.BlockSpec(shape, lambda *_: (0,) * nd, pipeline_mode=pl.Buffered(1))


HBM = pl.BlockSpec(memory_space=pl.ANY)
VM = pl.BlockSpec(memory_space=pltpu.VMEM)


SH_ROWS = HALO - 8


def _shifted_copies(buf, shbuf, tm):
    for b in range(1, 8):
        shbuf[b - 1] = buf[b:b + tm + SH_ROWS, :]


def _window(buf, shbuf, off, tm):
    a, b = divmod(off, 8)
    if b == 0:
        return buf[8 * a:8 * a + tm, :]
    return shbuf[b - 1, 8 * a:8 * a + tm, :]


def _first_head_lanes():
    return lax.broadcasted_iota(jnp.int32, (CH, CH), 1) < (DA // NH)


def _mix_heads(w_ref, vb, first):
    outs = []
    for p in range(NH // 2):
        v = vb[:, p * CH:(p + 1) * CH]
        a = _dot(w_ref[(2 * p) * CH:(2 * p + 1) * CH, :], v)
        b = _dot(w_ref[(2 * p + 1) * CH:(2 * p + 2) * CH, :], v)
        outs.append(jnp.where(first, a, b))
    return jnp.concatenate(outs, axis=1)


def _place():
    x, y, c = lax.axis_index("x"), lax.axis_index("y"), lax.axis_index("c")
    return x, y, c, 4 * x + 2 * y + c


def _dev(t):
    return (t >> 2, (t >> 1) & 1, t & 1)


class _AllGather:
    def __init__(self, w_in, w_out, wss, wrs, lsem):
        x, y, c, idx = _place()
        me, sibling = (x, y, c), (x, y, 1 - c)
        chips = [(1 - x, y), (x, 1 - y), (1 - x, 1 - y)]
        nw = len(w_in)

        def blk(p):
            return 4 * p[0] + 2 * p[1] + p[2]

        def wcopy(a, k, block, to, src=None):
            dst = w_out[a].at[blk(block)]
            return pltpu.make_async_remote_copy(src_ref=dst if src is None else src, dst_ref=dst,
                                                send_sem=wss.at[a, k], recv_sem=wrs.at[a, k],
                                                device_id=to, device_id_type=MESH)

        self.mine = [pltpu.make_async_copy(w_in[a], w_out[a].at[idx], lsem.at[a]) for a in range(nw)]
        self.first = []
        for a in range(nw):
            self.first.append(wcopy(a, 0, me, sibling, src=w_in[a]))
            self.first += [wcopy(a, 1 + j, me, (*chip, c), src=w_in[a]) for j, chip in enumerate(chips)]
        self.landed = [[wcopy(a, 1 + j, (*chip, c), me) for a in range(nw)] for j, chip in enumerate(chips)]
        self.passed = [[wcopy(a, 4 + j, (*chip, c), sibling) for a in range(nw)] for j, chip in enumerate(chips)]
        self.from_sibling = []
        for a in range(nw):
            self.from_sibling.append(wcopy(a, 0, sibling, me))
            self.from_sibling += [wcopy(a, 4 + j, (*chip, 1 - c), me) for j, chip in enumerate(chips)]

    def start(self):
        for cp in self.mine + self.first:
            cp.start()

    def forward(self):
        for land, pas in zip(self.landed, self.passed):
            for l, p in zip(land, pas):
                l.wait_recv()
                p.start()

    def finish(self):
        for cp in self.from_sibling:
            cp.wait_recv()
        for cp in self.first:
            cp.wait_send()
        for pas in self.passed:
            for p in pas:
                p.wait_send()
        for cp in self.mine:
            cp.wait()


AG_SEMS = lambda nw: [pltpu.SemaphoreType.DMA((nw, 7)), pltpu.SemaphoreType.DMA((nw, 7)),
                      pltpu.SemaphoreType.DMA((nw,))]


def _mix_fwd(x, mod, g1, win, b_in, lng, lnb, wcat, bsf, cw, cb, gng, gnb, oga, ogb, wout, pm, ffn_shards, tm):
    T = x.shape[0]
    nt = T // tm
    nch = tm // CH
    nw = len(ffn_shards)
    fwd_step = (5 * nt) // 8
    saved = [(D, F32), (D, BF16), (2 * DB, F32), (DA, F32), (D, BF16), (D, F32), (DA, F32), (DA, F32), (DA, F32),
             (DA, F32), (CH, F32), (DB, F32), (DB, F32)]
    NSAVE = len(saved)

    def body(x_ref, mod_ref, g1_ref, win_ref, bin_ref, lng_ref, lnb_ref, wcat_ref, bsf_ref, cw_ref, cb_ref,
             gng_ref, gnb_ref, oga_ref, ogb_ref, wout_ref, pm_ref, *rest):
        sh_f32 = rest[:nw]
        (x1_ref, h_ref, zvg_ref, mixed_ref, y_ref, o_ref, gu_ref, dgu_ref, dgv_ref, vhat_ref, rsl_ref, yhat_ref,
         rsg_ref) = rest[nw:nw + NSAVE]
        sh_out = rest[nw + NSAVE:2 * nw + NSAVE]
        glbuf, shbuf = rest[2 * nw + NSAVE:2 * nw + NSAVE + 2]
        sh_in = rest[2 * nw + NSAVE + 2:3 * nw + NSAVE + 2]
        wss, wrs, lsem = rest[3 * nw + NSAVE + 2:]
        i = pl.program_id(0)

        @pl.when(i == 0)
        def _():
            for a in range(nw):
                sh_in[a][...] = sh_f32[a][...].astype(BF16)
            _AllGather(sh_in, sh_out, wss, wrs, lsem).start()

        xv = x_ref[...]
        shift1 = mod_ref[0:1, :]
        scale1 = mod_ref[1:2, :]
        gate1 = mod_ref[2:3, :]
        h = (xv * _rs(xv) * g1_ref[...]) * (1.0 + scale1) + shift1
        hb = h.astype(BF16)
        h_ref[...] = hb
        z = jnp.concatenate([_dot(hb, win_ref[j]) for j in range(NDEV)], axis=1) + bin_ref[...]
        zvg_ref[...] = z[:, 2 * DA:]
        gu, dgelu_u = _gelu_parts(z[:, 0:DA])
        gv, dgelu_v = _gelu_parts(z[:, DA:2 * DA])
        gu_ref[...] = gu
        dgu_ref[...] = dgelu_u
        dgv_ref[...] = dgelu_v
        xc = gv - jnp.mean(gv, axis=-1, keepdims=True)
        rsl = lax.rsqrt(jnp.mean(xc * xc, axis=-1, keepdims=True) + EPS)
        vhat = xc * rsl
        vhat_ref[...] = vhat
        rsl_ref[...] = jnp.broadcast_to(rsl, (tm, CH))
        vnb = (vhat * lng_ref[...] + lnb_ref[...]).astype(BF16)
        first = _first_head_lanes()
        chunks = []
        for ci in range(nch):
            chunks.append(_mix_heads(wcat_ref, vnb[ci * CH:(ci + 1) * CH, :], first) + bsf_ref[...])
        mixed = jnp.concatenate(chunks, axis=0) if nch > 1 else chunks[0]
        mixed_ref[...] = mixed
        ya = gu * mixed
        gl = z[:, 2 * DA:2 * DA + DB] * _sig(z[:, 2 * DA + DB:])

        @pl.when(i == 0)
        def _():
            glbuf[0:HALO, :] = jnp.zeros((HALO, DB), F32)

        glbuf[HALO:HALO + tm, :] = gl
        _shifted_copies(glbuf, shbuf, tm)
        yc = jnp.zeros((tm, DB), F32) + cb_ref[...]
        for k in range(KW):
            yc = yc + cw_ref[k:k + 1, :] * _window(glbuf, shbuf, HALO - (KW - 1) + k, tm)
        glbuf[0:HALO, :] = gl[tm - HALO:, :]
        pmv = pm_ref[...]
        dc = yc - _grp_mean(yc, pmv)
        rsg = lax.rsqrt(_grp_mean(dc * dc, pmv) + EPS)
        yhat = dc * rsg
        yhat_ref[...] = yhat
        rsg_ref[...] = rsg
        yg = yhat * gng_ref[...] + gnb_ref[...]
        yb = yg * _sig(yg)
        na = ya * _rs(ya) * oga_ref[...]
        nb = yb * _rs(yb) * ogb_ref[...]
        yv = jnp.concatenate([na, nb], axis=1).astype(BF16)
        y_ref[...] = yv
        o = _dot(yv, wout_ref[...])
        o_ref[...] = o
        x1_ref[...] = xv + gate1 * o

        @pl.when(i == fwd_step)
        def _():
            _AllGather(sh_in, sh_out, wss, wrs, lsem).forward()

        @pl.when(i == nt - 1)
        def _():
            _AllGather(sh_in, sh_out, wss, wrs, lsem).finish()

    tile = lambda w: pl.BlockSpec((tm, w), lambda i: (i, 0))
    outs = pl.pallas_call(
        body,
        name="mix_fwd",
        grid=(nt,),
        in_specs=[tile(D), _full((8, D)), _full((1, D)), _resident((NDEV, D, WIN_B)), _full((1, DIN)),
                  _full((1, DA)), _full((1, DA)), _full((NH * CH, CH)), _full((CH, DA)), _full((HALO, DB)),
                  _full((1, DB)), _full((1, DB)), _full((1, DB)), _full((1, DA)), _full((1, DB)),
                  _resident((D, D)), _full((DB, DB))] + [_resident(s.shape) for s in ffn_shards],
        out_specs=[tile(w) for w, _ in saved] + [HBM] * nw,
        out_shape=[jax.ShapeDtypeStruct((T, w), dt) for w, dt in saved]
                  + [jax.ShapeDtypeStruct((NDEV,) + s.shape, BF16) for s in ffn_shards],
        scratch_shapes=[pltpu.VMEM((HALO + tm, DB), F32), pltpu.VMEM((7, tm + SH_ROWS, DB), F32)]
                       + [pltpu.VMEM(s.shape, BF16) for s in ffn_shards] + AG_SEMS(nw),
        compiler_params=pltpu.CompilerParams(dimension_semantics=("arbitrary",), vmem_limit_bytes=VMEM_LIMIT),
    )(x, mod, g1, win, b_in, lng, lnb, wcat, bsf, cw, cb, gng, gnb, oga, ogb, wout, pm, *ffn_shards)
    return outs[:NSAVE], outs[NSAVE:]


FF_BLOCKS = ((0, 1024), (1024, 1024), (2048, 768))


def _ffn(x1, tgt, mod, g2, gf, wfi_t, wfo, tm):
    T = x1.shape[0]
    nt = T // tm

    def body(x1_ref, tgt_ref, mod_ref, g2_ref, gf_ref, wfi_ref, wfo_ref,
             dx1_ref, h2_ref, dgu_ref, act_ref, dxg_ref, acc_ref, g_s, u_s):
        i = pl.program_id(0)

        @pl.when(i == 0)
        def _():
            acc_ref[...] = jnp.zeros((8, D), F32)

        x1 = x1_ref[...]
        shift2 = mod_ref[3:4, :]
        scale2 = mod_ref[4:5, :]
        gate2 = mod_ref[5:6, :]
        shiftf = mod_ref[6:7, :]
        scalef = mod_ref[7:8, :]
        g2v = g2_ref[...]
        gfv = gf_ref[...]
        r2 = _rs(x1)
        xn2 = x1 * r2
        h2b = (xn2 * g2v * (1.0 + scale2) + shift2).astype(BF16)
        h2_ref[...] = h2b
        f = jnp.zeros((tm, D), F32)
        for o, w in FF_BLOCKS:
            g = _dot_nt(h2b, wfi_ref[o:o + w, :])
            u = _dot_nt(h2b, wfi_ref[DFF + o:DFF + o + w, :])
            g_s[:, o:o + w] = g
            u_s[:, o:o + w] = u
            actb = (g * _sig(g) * u).astype(BF16)
            act_ref[:, o:o + w] = actb
            f = f + _dot(actb, wfo_ref[o:o + w, :])
        x2 = x1 + gate2 * f
        rf = _rs(x2)
        xnf = x2 * rf
        out = xnf * gfv * (1.0 + scalef) + shiftf
        e = out - tgt_ref[...]
        dout = e * (1.0 / D)
        acc_ref[7:8, :] += _colsum(e * e)
        acc_ref[0:1, :] += _colsum(dout)
        acc_ref[1:2, :] += _colsum(dout * xnf * gfv)
        acc_ref[2:3, :] += _colsum(dout * (1.0 + scalef) * xnf)
        dxnf = dout * (1.0 + scalef) * gfv
        dx2 = rf * (dxnf - xnf * jnp.mean(dxnf * xnf, axis=-1, keepdims=True))
        acc_ref[3:4, :] += _colsum(dx2 * f)
        dxgb = (dx2 * gate2).astype(BF16)
        dxg_ref[...] = dxgb
        dh2 = jnp.zeros((tm, D), F32)
        for o, w in FF_BLOCKS:
            dact = _dot_nt(dxgb, wfo_ref[o:o + w, :])
            g = g_s[:, o:o + w]
            u = u_s[:, o:o + w]
            s = _sig(g)
            dgb = (dact * u * (s * (1.0 + g * (1.0 - s)))).astype(BF16)
            dub = (dact * (g * s)).astype(BF16)
            dgu_ref[:, o:o + w] = dgb
            dgu_ref[:, DFF + o:DFF + o + w] = dub
            dh2 = dh2 + _dot(dgb, wfi_ref[o:o + w, :])
            dh2 = dh2 + _dot(dub, wfi_ref[DFF + o:DFF + o + w, :])
        acc_ref[4:5, :] += _colsum(dh2)
        acc_ref[5:6, :] += _colsum(dh2 * xn2 * g2v)
        acc_ref[6:7, :] += _colsum(dh2 * (1.0 + scale2) * xn2)
        dxn2 = dh2 * (1.0 + scale2) * g2v
        dx1_ref[...] = dx2 + r2 * (dxn2 - xn2 * jnp.mean(dxn2 * xn2, axis=-1, keepdims=True))

    tile = lambda w: pl.BlockSpec((tm, w), lambda i: (i, 0))
    return pl.pallas_call(
        body,
        name="ffn_fwd_bwd",
        grid=(nt,),
        in_specs=[tile(D), tile(D), _full((8, D)), _full((1, D)), _full((1, D)),
                  _resident((2 * DFF, D)), _resident((DFF, D))],
        out_specs=[tile(D), tile(D), tile(2 * DFF), tile(DFF), tile(D), _full((8, D))],
        out_shape=[jax.ShapeDtypeStruct((T, D), F32), jax.ShapeDtypeStruct((T, D), BF16),
                   jax.ShapeDtypeStruct((T, 2 * DFF), BF16), jax.ShapeDtypeStruct((T, DFF), BF16),
                   jax.ShapeDtypeStruct((T, D), BF16), jax.ShapeDtypeStruct((8, D), F32)],
        scratch_shapes=[pltpu.VMEM((tm, DFF), F32), pltpu.VMEM((tm, DFF), F32)],
        compiler_params=pltpu.CompilerParams(dimension_semantics=("arbitrary",), vmem_limit_bytes=VMEM_LIMIT),
    )(x1, tgt, mod, g2, gf, wfi_t, wfo)


def _mix_bwd(dx1, x, zvg, mixed, o, hb, yb, gu, dgu, dgv, vhat, rslb, yhat, rsg, mod, g1, win, lng, lnb, wcat, wcat_t,
             cw, gng, gnb, oga, ogb, wout, pm, esel, after, tm):
    T = x.shape[0]
    nt = T // tm
    nch = tm // CH
    WOB = 256

    def body(dx1_ref, x_ref, zvg_ref, mixed_ref, o_ref, hb_ref, yb_ref, gu_ref, dgu_ref, dgv_ref, vhat_ref, rsl_ref,
             yhat_ref, rsg_ref, mod_ref, g1_ref, win_ref, lng_ref, lnb_ref, wcat_ref, wcatt_ref, cw_ref, gng_ref,
             gnb_ref, oga_ref, ogb_ref, wout_ref, pm_ref, esel_ref, after_ref,
             gx_ref, accv_ref, accb_ref, acca_ref, accbs_ref, accws_ref, acccw_ref, gwin_ref, gwout_ref,
             dycbuf, shbuf, bs_s, acc_win, acc_wout, st_win, st_wout):
        i = pl.program_id(0)

        @pl.when(i == 0)
        def _():
            acc_win[...] = jnp.zeros((NDEV, D, WIN_B), F32)
            acc_wout[...] = jnp.zeros((D, D), F32)
            accv_ref[...] = jnp.zeros((8, D), F32)
            accb_ref[...] = jnp.zeros((1, DIN), F32)
            acca_ref[...] = jnp.zeros((8, DA), F32)
            accws_ref[...] = jnp.zeros((NH * CH, CH), F32)
            acccw_ref[...] = jnp.zeros((HALO, DB), F32)
            bs_s[...] = jnp.zeros((CH, DA), F32)
            dycbuf[tm:tm + HALO, :] = jnp.zeros((HALO, DB), F32)

        shift1 = mod_ref[0:1, :]
        scale1 = mod_ref[1:2, :]
        gate1 = mod_ref[2:3, :]
        g1v = g1_ref[...]
        xv = x_ref[...]
        r1 = _rs(xv)
        xn1 = xv * r1
        val = zvg_ref[:, 0:DB]
        gate = zvg_ref[:, DB:]
        gu = gu_ref[...]
        dgelu_u = dgu_ref[...]
        dgelu_v = dgv_ref[...]
        vhat = vhat_ref[...]
        rsl = rsl_ref[:, 0:1]
        lngv = lng_ref[...]
        vnb = (vhat * lngv + lnb_ref[...]).astype(BF16)
        mixed = mixed_ref[...]
        ya = gu * mixed
        ra = _rs(ya)
        yan = ya * ra
        sgt = _sig(gate)
        gl = val * sgt
        pmv = pm_ref[...]
        rsg = rsg_ref[...]
        yhat = yhat_ref[...]
        gngv = gng_ref[...]
        yg = yhat * gngv + gnb_ref[...]
        sgy = _sig(yg)
        yb = yg * sgy
        rb = _rs(yb)
        ybn = yb * rb
        dx1 = dx1_ref[...]
        accv_ref[0:1, :] += _colsum(dx1 * o_ref[...])
        dogb = (dx1 * gate1).astype(BF16)
        acc_wout[...] += _dot_tn(yb_ref[...], dogb)
        dy = _dot_nt(dogb, wout_ref[...])
        dna = dy[:, 0:DA]
        dnb = dy[:, DA:]
        ogav = oga_ref[...]
        ogbv = ogb_ref[...]
        acca_ref[2:3, :] += _colsum(dna * yan)
        acca_ref[3:4, :] += _colsum(dnb * ybn)
        ta = dna * ogav
        dya = ra * (ta - yan * jnp.mean(ta * yan, axis=-1, keepdims=True))
        tb = dnb * ogbv
        dyb = rb * (tb - ybn * jnp.mean(tb * ybn, axis=-1, keepdims=True))
        dgu = dya * mixed
        dm = dya * gu
        first = _first_head_lanes()
        zero = jnp.zeros((CH, CH), BF16)
        dvn_chunks = []
        bs_acc = bs_s[...]
        for ci in range(nch):
            dmc = dm[ci * CH:(ci + 1) * CH, :]
            bs_acc = bs_acc + dmc
            dmcb = dmc.astype(BF16)
            dvn_chunks.append(_mix_heads(wcatt_ref, dmcb, first))
            vc = vnb[ci * CH:(ci + 1) * CH, :]
            for p in range(NH // 2):
                xt = dmcb[:, p * CH:(p + 1) * CH]
                vt = vc[:, p * CH:(p + 1) * CH]
                accws_ref[(2 * p) * CH:(2 * p + 1) * CH, :] += _dot_nt(jnp.where(first, xt, zero), vt)
                accws_ref[(2 * p + 1) * CH:(2 * p + 2) * CH, :] += _dot_nt(jnp.where(first, zero, xt), vt)
        bs_s[...] = bs_acc
        dvn = jnp.concatenate(dvn_chunks, axis=0) if nch > 1 else dvn_chunks[0]
        acca_ref[0:1, :] += _colsum(dvn * vhat)
        acca_ref[1:2, :] += _colsum(dvn)
        dvh = dvn * lngv
        dgv = rsl * (dvh - jnp.mean(dvh, axis=-1, keepdims=True)
                     - vhat * jnp.mean(dvh * vhat, axis=-1, keepdims=True))
        du = dgu * dgelu_u
        dv = dgv * dgelu_v
        dyg = dyb * (sgy * (1.0 + yg * (1.0 - sgy)))
        acca_ref[5:6, :] += _colsum(dyg * yhat)
        acca_ref[6:7, :] += _colsum(dyg)
        dyh = dyg * gngv
        dyc = rsg * (dyh - _grp_mean(dyh, pmv) - yhat * _grp_mean(dyh * yhat, pmv))
        acca_ref[4:5, :] += _colsum(dyc)
        dycbuf[0:tm, :] = dyc
        _shifted_copies(dycbuf, shbuf, tm)
        dgl = jnp.zeros((tm, DB), F32)
        for k in range(KW):
            win_k = _window(dycbuf, shbuf, KW - 1 - k, tm)
            dgl = dgl + cw_ref[k:k + 1, :] * win_k
            acccw_ref[k:k + 1, :] += _colsum(win_k * gl)
        dycbuf[tm:tm + HALO, :] = dyc[0:HALO, :]
        dval = dgl * sgt
        dgate = dgl * val * sgt * (1.0 - sgt)
        dz = jnp.concatenate([du, dv, dval, dgate], axis=1)
        accb_ref[...] += _colsum(dz)
        dzb = dz.astype(BF16)
        hbv = hb_ref[...]
        dh = jnp.zeros((tm, D), F32)
        for j in range(NDEV):
            dzj = dzb[:, j * WIN_B:(j + 1) * WIN_B]
            acc_win[j] += _dot_tn(hbv, dzj)
            dh = dh + _dot_nt(dzj, win_ref[j])
        accv_ref[1:2, :] += _colsum(dh)
        dh_xn = _colsum(dh * xn1)
        accv_ref[2:3, :] += dh_xn * g1v
        accv_ref[3:4, :] += dh_xn * (1.0 + scale1)
        dxn1 = dh * (1.0 + scale1) * g1v
        gx_ref[...] = dx1 + r1 * (dxn1 - xn1 * jnp.mean(dxn1 * xn1, axis=-1, keepdims=True))

        @pl.when(i == nt - 1)
        def _():
            rows = lax.broadcasted_iota(jnp.int32, (NH * CH, CH), 0) & (CH - 1)
            cols = lax.broadcasted_iota(jnp.int32, (NH * CH, CH), 1)
            accws_ref[...] = jnp.where(cols <= rows, accws_ref[...], 0.0)
            bs = bs_s[...]
            hi = bs.astype(BF16)
            r1_ = bs - hi.astype(F32)
            mid = r1_.astype(BF16)
            lo = (r1_ - mid.astype(F32)).astype(BF16)
            ev = esel_ref[...]
            accbs_ref[...] = _dot(hi, ev) + _dot(mid, ev) + _dot(lo, ev)
            for j in range(NDEV):
                st_win[...] = acc_win[j].astype(BF16)
                pltpu.sync_copy(st_win, gwin_ref.at[j])
            for j in range(D // WOB):
                st_wout[...] = acc_wout[j * WOB:(j + 1) * WOB, :].astype(BF16)
                pltpu.sync_copy(st_wout, gwout_ref.at[pl.ds(j * WOB, WOB)])

    rev = lambda w: pl.BlockSpec((tm, w), lambda i: (nt - 1 - i, 0))
    outs = pl.pallas_call(
        body,
        name="mix_bwd",
        grid=(nt,),
        in_specs=[rev(D), rev(D), rev(2 * DB), rev(DA), rev(D), rev(D), rev(D), rev(DA), rev(DA), rev(DA), rev(DA),
                  rev(CH), rev(DB), rev(DB), _full((8, D)), _full((1, D)),
                  _resident((NDEV, D, WIN_B)), _full((1, DA)), _full((1, DA)), _full((NH * CH, CH)),
                  _full((NH * CH, CH)), _full((HALO, DB)), _full((1, DB)), _full((1, DB)), _full((1, DA)),
                  _full((1, DB)), _resident((D, D)), _full((DB, DB)), _full((DA, CH)), HBM],
        out_specs=[rev(D), _full((8, D)), _full((1, DIN)), _full((8, DA)), _full((CH, CH)),
                   _full((NH * CH, CH)), _full((HALO, DB)), HBM, HBM],
        out_shape=[jax.ShapeDtypeStruct((T, D), F32), jax.ShapeDtypeStruct((8, D), F32),
                   jax.ShapeDtypeStruct((1, DIN), F32), jax.ShapeDtypeStruct((8, DA), F32),
                   jax.ShapeDtypeStruct((CH, CH), F32), jax.ShapeDtypeStruct((NH * CH, CH), F32),
                   jax.ShapeDtypeStruct((HALO, DB), F32),
                   jax.ShapeDtypeStruct((NDEV, D, WIN_B), BF16), jax.ShapeDtypeStruct((D, D), BF16)],
        scratch_shapes=[pltpu.VMEM((tm + HALO, DB), F32), pltpu.VMEM((7, tm + SH_ROWS, DB), F32),
                        pltpu.VMEM((CH, DA), F32), pltpu.VMEM((NDEV, D, WIN_B), F32), pltpu.VMEM((D, D), F32),
                        pltpu.VMEM((D, WIN_B), BF16), pltpu.VMEM((WOB, D), BF16)],
        compiler_params=pltpu.CompilerParams(dimension_semantics=("arbitrary",), vmem_limit_bytes=VMEM_LIMIT),
    )(dx1, x, zvg, mixed, o, hb, yb, gu, dgu, dgv, vhat, rslb, yhat, rsg, mod, g1, win, lng, lnb, wcat, wcat_t, cw,
      gng, gnb, oga, ogb, wout, pm, esel, after)
    return outs[:7], outs[7:]


def _wgrad_rows(a, b, bm, tk, name):
    T, M = a.shape
    N = b.shape[1]
    nk = T // tk

    def body(a_ref, b_ref, o_ref, acc):
        k = pl.program_id(1)

        @pl.when(k == 0)
        def _():
            acc[...] = jnp.zeros((bm, N), F32)

        acc[...] += _dot_tn(a_ref[...], b_ref[...])

        @pl.when(k == nk - 1)
        def _():
            o_ref[...] = acc[...].astype(BF16)

    return pl.pallas_call(
        body, name=name, grid=(M // bm, nk),
        in_specs=[pl.BlockSpec((tk, bm), lambda j, k: (k, j)), pl.BlockSpec((tk, N), lambda j, k: (k, 0))],
        out_specs=pl.BlockSpec((bm, N), lambda j, k: (j, 0)),
        out_shape=jax.ShapeDtypeStruct((M, N), BF16),
        scratch_shapes=[pltpu.VMEM((bm, N), F32)],
        compiler_params=pltpu.CompilerParams(dimension_semantics=("arbitrary", "arbitrary"),
                                             vmem_limit_bytes=VMEM_LIMIT),
    )(a, b)


def _small_copy(src, dst, ss, rs, k, to):
    return pltpu.make_async_remote_copy(src_ref=src, dst_ref=dst, send_sem=ss.at[k], recv_sem=rs.at[k],
                                        device_id=to, device_id_type=MESH)


def _gather(c_row, ada_w, ada_b8, ada_f_w, ada_f_b8, conv_s, shards):
    nw = len(shards)

    def body(c_ref, adaw_ref, adab_ref, adafw_ref, adafb_ref, conv_ref, *rest):
        w_f32 = rest[:nw]
        call_ref, cparts_ref, cfparts_ref, convg_ref = rest[nw:nw + 4]
        w_out = rest[nw + 4:2 * nw + 4]
        part_s, partf_s = rest[2 * nw + 4:2 * nw + 6]
        w_in = rest[2 * nw + 6:3 * nw + 6]
        wss, wrs, lsem, s1, r1, s2, r2, s3, r3, s4, r4 = rest[3 * nw + 6:]
        x, y, c, idx = _place()
        me = (x, y, c)
        for a in range(nw):
            w_in[a][...] = w_f32[a][...].astype(BF16)
        ag = _AllGather(w_in, w_out, wss, wrs, lsem)
        ag.start()
        call_ref[pl.ds(idx, 1), :] = c_ref[...]
        convg_ref[idx] = conv_ref[...]
        ph1 = []
        for k in range(1, NDEV):
            to = _dev(idx ^ k)
            ph1.append(_small_copy(c_ref, call_ref.at[pl.ds(idx, 1)], s1, r1, k - 1, to))
            ph1.append(_small_copy(conv_ref, convg_ref.at[idx], s2, r2, k - 1, to))
        for cp in ph1:
            cp.start()
        for k in range(1, NDEV):
            src_dev = idx ^ k
            _small_copy(c_ref, call_ref.at[pl.ds(src_dev, 1)], s1, r1, k - 1, me).wait_recv()
            _small_copy(conv_ref, convg_ref.at[src_dev], s2, r2, k - 1, me).wait_recv()
        call = call_ref[...]
        cact = (call * _sig(call))
        part_s[...] = jnp.dot(cact, adaw_ref[...], preferred_element_type=F32,
                              precision=lax.Precision.HIGHEST) + adab_ref[pl.ds(idx, 1), :]
        partf_s[...] = jnp.dot(cact, adafw_ref[...], preferred_element_type=F32,
                               precision=lax.Precision.HIGHEST) + adafb_ref[pl.ds(idx, 1), :]
        cparts_ref[pl.ds(idx, 1), :] = part_s[pl.ds(idx, 1), :]
        cfparts_ref[pl.ds(idx, 1), :] = partf_s[pl.ds(idx, 1), :]
        ph2 = []
        for k in range(1, NDEV):
            t = idx ^ k
            ph2.append(_small_copy(part_s.at[pl.ds(t, 1)], cparts_ref.at[pl.ds(idx, 1)], s3, r3, k - 1, _dev(t)))
            ph2.append(_small_copy(partf_s.at[pl.ds(t, 1)], cfparts_ref.at[pl.ds(idx, 1)], s4, r4, k - 1, _dev(t)))
        for cp in ph2:
            cp.start()
        for k in range(1, NDEV):
            src_dev = idx ^ k
            _small_copy(part_s.at[pl.ds(0, 1)], cparts_ref.at[pl.ds(src_dev, 1)], s3, r3, k - 1, me).wait_recv()
            _small_copy(partf_s.at[pl.ds(0, 1)], cfparts_ref.at[pl.ds(src_dev, 1)], s4, r4, k - 1, me).wait_recv()
        for cp in ph1 + ph2:
            cp.wait_send()
        ag.forward()
        ag.finish()

    dma7 = pltpu.SemaphoreType.DMA((NDEV - 1,))
    outs = pl.pallas_call(
        body,
        name="gather_weights",
        in_specs=[VM] * (6 + nw),
        out_specs=[VM] * 4 + [HBM] * nw,
        out_shape=[jax.ShapeDtypeStruct((NDEV, D), F32), jax.ShapeDtypeStruct((NDEV, ada_w.shape[1]), F32),
                   jax.ShapeDtypeStruct((NDEV, ada_f_w.shape[1]), F32),
                   jax.ShapeDtypeStruct((NDEV,) + conv_s.shape, F32)]
                  + [jax.ShapeDtypeStruct((NDEV,) + s.shape, BF16) for s in shards],
        scratch_shapes=[pltpu.VMEM((NDEV, ada_w.shape[1]), F32), pltpu.VMEM((NDEV, ada_f_w.shape[1]), F32)]
                       + [pltpu.VMEM(s.shape, BF16) for s in shards] + AG_SEMS(nw) + [dma7] * 8,
        compiler_params=pltpu.CompilerParams(vmem_limit_bytes=VMEM_LIMIT),
    )(c_row, ada_w, ada_b8, ada_f_w, ada_f_b8, conv_s, *shards)
    return outs[0], outs[1], outs[2], outs[3], outs[4:]


_VEC_AT = {
    "norm1_g": (8, 0, D), "a_ln_g": (11, 0, DA), "a_ln_b": (11, DA, DA), "a_spatial_b": (12, 0, D),
    "b_conv_b": (13, 0, DB), "b_gn_g": (13, DB, DB), "b_gn_b": (14, 0, DB), "out_norm_a_g": (14, DB, DA),
    "out_norm_b_g": (15, 0, DB), "norm2_g": (16, 0, D), "norm_f_g": (17, 0, D),
}
_LOSS_ROW = 18
_CW_ROW = 24


def _reduce_small(acc_f, acc_v, acc_b, acc_a, acc_bs, acc_cw, dws, after, pair_grads):
    npg = len(pair_grads)

    def body(accf_ref, accv_ref, accb_ref, acca_ref, accbs_ref, acccw_ref, dws_ref, after_ref, *rest):
        pg = rest[:npg]
        vsum_ref, dcond_ref, wssum_ref = rest[npg:npg + 3]
        pq = rest[npg + 3:2 * npg + 3]
        vloc, vbuf, wbuf, wown, s1, r1, s2, r2, s3, r3 = rest[2 * npg + 3:2 * npg + 13]
        pland = rest[2 * npg + 13:3 * npg + 13]
        pstage = rest[3 * npg + 13:4 * npg + 13]
        ps, pr, pls, pss = rest[4 * npg + 13:]
        x, y, c, idx = _place()
        me = (x, y, c)
        sibling = (x, y, 1 - c)
        chips = [(1 - x, y), (x, 1 - y), (1 - x, 1 - y)]
        blk = lambda p: 4 * p[0] + 2 * p[1] + p[2]
        give = [blk((*ch, 1 - c)) for ch in chips] + [blk(sibling)]
        pair = [pltpu.make_async_remote_copy(src_ref=pg[a].at[b], dst_ref=pland[a].at[j], send_sem=ps.at[a, j],
                                             recv_sem=pr.at[a, j], device_id=sibling, device_id_type=MESH)
                for a in range(npg) for j, b in enumerate(give)]
        loads = [pltpu.make_async_copy(pg[a].at[blk((*ch, c))], pstage[a].at[j], pls.at[a, j])
                 for a in range(npg) for j, ch in enumerate(chips)]
        for cp in pair + loads:
            cp.start()
        vloc[...] = jnp.zeros((NVEC, D), F32)
        vloc[0:1, :] = accv_ref[1:2, :]
        vloc[1:2, :] = accv_ref[2:3, :]
        vloc[2:3, :] = accv_ref[0:1, :]
        vloc[3:4, :] = accf_ref[4:5, :]
        vloc[4:5, :] = accf_ref[5:6, :]
        vloc[5:6, :] = accf_ref[3:4, :]
        vloc[6:7, :] = accf_ref[0:1, :]
        vloc[7:8, :] = accf_ref[1:2, :]
        vloc[8:9, :] = accv_ref[3:4, :]
        vloc[9:10, :] = accb_ref[:, 0:D]
        vloc[10:11, :] = accb_ref[:, D:]
        vloc[11:12, 0:DA] = acca_ref[0:1, :]
        vloc[11:12, DA:] = acca_ref[1:2, :]
        bst = accbs_ref[...].T
        for h in range(NH):
            vloc[12:13, h * CH:(h + 1) * CH] = bst[h:h + 1, :]
        vloc[13:14, 0:DB] = acca_ref[4:5, :]
        vloc[13:14, DB:] = acca_ref[5:6, :]
        vloc[14:15, 0:DB] = acca_ref[6:7, :]
        vloc[14:15, DB:] = acca_ref[2:3, :]
        vloc[15:16, 0:DB] = acca_ref[3:4, :]
        vloc[16:17, :] = accf_ref[6:7, :]
        vloc[17:18, :] = accf_ref[2:3, :]
        vloc[_LOSS_ROW:_LOSS_ROW + 1, :] = accf_ref[7:8, :]
        vloc[_CW_ROW:_CW_ROW + HALO // 2, 0:DB] = acccw_ref[0:HALO // 2, :]
        vloc[_CW_ROW:_CW_ROW + HALO // 2, DB:] = acccw_ref[HALO // 2:, :]
        vbuf[idx] = vloc[...]
        rows_of = lambda t: pl.ds(pl.multiple_of(t * CH, CH), CH)
        wbuf[0] = dws_ref[rows_of(idx), :]
        sm = []
        for k in range(1, NDEV):
            t = idx ^ k
            sm.append(_small_copy(vloc, vbuf.at[idx], s1, r1, k - 1, _dev(t)))
            sm.append(_small_copy(dws_ref.at[rows_of(t)], wbuf.at[k], s2, r2, k - 1, _dev(t)))
        for cp in sm:
            cp.start()
        for k in range(1, NDEV):
            _small_copy(dws_ref.at[rows_of(0)], wbuf.at[k], s2, r2, k - 1, me).wait_recv()
        ws = wbuf[0]
        for k in range(1, NDEV):
            ws = ws + wbuf[k]
        wown[...] = ws
        wssum_ref[rows_of(idx), :] = ws
        ag = [_small_copy(wown, wssum_ref.at[rows_of(idx)], s3, r3, k - 1, _dev(idx ^ k)) for k in range(1, NDEV)]
        for cp in ag:
            cp.start()
        for cp in loads:
            cp.wait()
        for cp in pair:
            cp.wait_recv()
        stores = []
        for a in range(npg):
            for j in range(3):
                pstage[a][j] = (pstage[a][j].astype(F32) + pland[a][j].astype(F32)).astype(BF16)
                stores.append(pltpu.make_async_copy(pstage[a].at[j], pq[a].at[j], pss.at[a, j]))
            stores.append(pltpu.make_async_copy(pland[a].at[3], pq[a].at[3], pss.at[a, 3]))
        for cp in stores:
            cp.start()
        for k in range(1, NDEV):
            _small_copy(vloc, vbuf.at[idx ^ k], s1, r1, k - 1, me).wait_recv()
        vs = vbuf[0]
        for d in range(1, NDEV):
            vs = vs + vbuf[d]
        vsum_ref[...] = vs
        for d in range(NDEV):
            dcond_ref[d] = vbuf[d, 0:8, :]
        for k in range(1, NDEV):
            _small_copy(wown, wssum_ref.at[rows_of(idx ^ k)], s3, r3, k - 1, me).wait_recv()
        for cp in sm + ag:
            cp.wait_send()
        for cp in stores:
            cp.wait()
        for cp in pair:
            cp.wait_send()

    dma7 = pltpu.SemaphoreType.DMA((NDEV - 1,))
    dma4 = pltpu.SemaphoreType.DMA((npg, 4))
    outs = pl.pallas_call(
        body,
        name="reduce_small",
        in_specs=[VM] * 7 + [HBM] + [HBM] * npg,
        out_specs=[VM, VM, VM] + [HBM] * npg,
        out_shape=[jax.ShapeDtypeStruct((NVEC, D), F32), jax.ShapeDtypeStruct((NDEV, 8, D), F32),
                   jax.ShapeDtypeStruct(dws.shape, F32)]
                  + [jax.ShapeDtypeStruct((4,) + g.shape[1:], g.dtype) for g in pair_grads],
        scratch_shapes=[pltpu.VMEM((NVEC, D), F32), pltpu.VMEM((NDEV, NVEC, D), F32),
                        pltpu.VMEM((NDEV, CH, CH), F32), pltpu.VMEM((CH, CH), F32)] + [dma7] * 6
                       + [pltpu.VMEM((4,) + g.shape[1:], g.dtype) for g in pair_grads]
                       + [pltpu.VMEM((3,) + g.shape[1:], g.dtype) for g in pair_grads] + [dma4] * 4,
        compiler_params=pltpu.CompilerParams(vmem_limit_bytes=VMEM_LIMIT),
    )(acc_f, acc_v, acc_b, acc_a, acc_bs, acc_cw, dws, after, *pair_grads)
    return outs[0], outs[1], outs[2], outs[3:]


HBM_ONLY = pl.BlockSpec(memory_space=pltpu.HBM)
SEM = pl.BlockSpec(memory_space=pltpu.SEMAPHORE)
EFFECT = pltpu.SideEffectType.DATAFLOW_SIDE_EFFECTING


def _rs_copies(g_refs, land_refs, sems, chips):
    x, y, c, idx = _place()
    if chips:
        routes = [(j, j, (*ch, c)) for j, ch in enumerate([(1 - x, y), (x, 1 - y), (1 - x, 1 - y)])]
    else:
        routes = [(idx ^ k, k - 1, _dev(idx ^ k)) for k in range(1, NDEV)]
    cps = []
    for src, dst, to in routes:
        for a in range(len(g_refs)):
            n = len(cps)
            cps.append(pltpu.make_async_remote_copy(
                src_ref=g_refs[a].at[src], dst_ref=land_refs[a].at[dst], send_sem=sems[2 * n],
                recv_sem=sems[2 * n + 1], device_id=to, device_id_type=MESH))
    return cps


def _rs_start(grads, name, after=(), chips=False):
    nw = len(grads)
    npeer = 3 if chips else NDEV - 1
    nsem = 2 * nw * npeer
    lands = [lax.empty((npeer,) + g.shape[1:], g.dtype) for g in grads]

    def body(*refs):
        g_refs, land_refs = refs[:nw], refs[nw:2 * nw]
        sems = refs[2 * nw + len(after):2 * nw + len(after) + nsem]
        token = refs[-1]
        for cp in _rs_copies(g_refs, land_refs, sems, chips):
            cp.start()
        token[...] = jnp.zeros_like(token)

    outs = pl.pallas_call(
        body, name=name,
        out_shape=(*[pltpu.SemaphoreType.DMA(())] * nsem,
                   *[pltpu.HBM(g.shape, g.dtype) for g in grads], *[pltpu.HBM(l.shape, l.dtype) for l in lands],
                   jax.ShapeDtypeStruct((8, CH), F32)),
        in_specs=[HBM_ONLY] * (2 * nw) + [HBM] * len(after),
        out_specs=(*[SEM] * nsem, *[HBM_ONLY] * (2 * nw), VM),
        input_output_aliases={i: nsem + i for i in range(2 * nw)},
        compiler_params=pltpu.CompilerParams(has_side_effects=EFFECT),
    )(*[pltpu.with_memory_space_constraint(g, pltpu.HBM) for g in grads],
      *[pltpu.with_memory_space_constraint(l, pltpu.HBM) for l in lands], *after)
    return outs[:nsem], outs[nsem:nsem + nw], outs[nsem + nw:nsem + 2 * nw], outs[-1]


def _rs_wait(sems, g_thru, land_thru, after, name, chips=False):
    nw = len(g_thru)
    nsem = len(sems)

    def body(*refs):
        g_refs, land_refs = refs[:nw], refs[nw:2 * nw]
        for cp in _rs_copies(g_refs, land_refs, refs[2 * nw:2 * nw + nsem], chips):
            cp.wait_send()
            cp.wait_recv()

    outs = pl.pallas_call(
        body, name=name,
        out_shape=tuple(pltpu.HBM(a.shape, a.dtype) for a in list(g_thru) + list(land_thru)),
        in_specs=[HBM_ONLY] * (2 * nw) + [SEM] * nsem + [HBM] * len(after),
        out_specs=tuple([HBM_ONLY] * (2 * nw)),
        input_output_aliases={i: i for i in range(2 * nw)},
        compiler_params=pltpu.CompilerParams(has_side_effects=EFFECT),
    )(*g_thru, *land_thru, *sems, *after)
    return outs[:nw], outs[nw:]


def _adamw(w, g, m, v):
    m2 = ADAM_B1 * m + (1.0 - ADAM_B1) * g
    v2 = ADAM_B2 * v + (1.0 - ADAM_B2) * (g * g)
    m_hat = m2 / (1.0 - ADAM_B1 ** ADAM_STEP)
    v_hat = v2 / (1.0 - ADAM_B2 ** ADAM_STEP)
    delta = -ADAM_LR * (m_hat / (jnp.sqrt(v_hat) + ADAM_EPS) + ADAM_WD * w)
    return delta, m2, v2


def _adam_big(r, w, m, v, rb, name, own, after=None, sib=None):
    R, C = w.shape
    ns = r.shape[0]
    g_all, idx1 = own

    def body(idx_ref, r_ref, own_ref, *refs):
        w_ref, m_ref, v_ref, g_ref, d_ref, m2_ref, v2_ref = refs[len(refs) - 7:]
        g = own_ref[0].astype(F32)
        if sib is not None:
            g = g + refs[0][0].astype(F32)
        for k in range(ns):
            g = g + r_ref[k].astype(F32)
        g_ref[...] = g
        d_ref[...], m2_ref[...], v2_ref[...] = _adamw(w_ref[...], g, m_ref[...], v_ref[...])

    t2 = pl.BlockSpec((rb, C), lambda i, idx_ref: (i, 0))
    sd = jax.ShapeDtypeStruct((R, C), F32)
    extra_specs = ([pl.BlockSpec((1, rb, C), lambda i, idx_ref: (3, i, 0))] if sib is not None else []) \
        + ([HBM] if after is not None else [])
    extra = ([sib] if sib is not None else []) + ([after] if after is not None else [])
    return pl.pallas_call(
        body, name=name,
        grid_spec=pltpu.PrefetchScalarGridSpec(
            num_scalar_prefetch=1, grid=(R // rb,),
            in_specs=[pl.BlockSpec((ns, rb, C), lambda i, idx_ref: (0, i, 0)),
                      pl.BlockSpec((1, rb, C), lambda i, idx_ref: (idx_ref[0], i, 0))] + extra_specs + [t2, t2, t2],
            out_specs=[t2, t2, t2, t2]),
        out_shape=[sd, sd, sd, sd],
        compiler_params=pltpu.CompilerParams(dimension_semantics=("arbitrary",), vmem_limit_bytes=VMEM_LIMIT),
    )(idx1, r, g_all, *extra, w, m, v)


def _adam_ada(c_all, dcs, w, m, v, rb, name):
    R, C = w.shape

    def body(c_ref, dc_ref, w_ref, m_ref, v_ref, g_ref, d_ref, m2_ref, v2_ref):
        cv = c_ref[...]
        g = lax.dot_general(cv * _sig(cv), dc_ref[...], (((0,), (0,)), ((), ())), preferred_element_type=F32,
                            precision=lax.Precision.HIGHEST)
        g_ref[...] = g
        d_ref[...], m2_ref[...], v2_ref[...] = _adamw(w_ref[...], g, m_ref[...], v_ref[...])

    t2 = pl.BlockSpec((rb, C), lambda i: (i, 0))
    sd = jax.ShapeDtypeStruct((R, C), F32)
    return pl.pallas_call(
        body, name=name, grid=(R // rb,),
        in_specs=[pl.BlockSpec((NDEV, rb), lambda i: (0, i)), _full((NDEV, C)), t2, t2, t2],
        out_specs=[t2, t2, t2, t2], out_shape=[sd, sd, sd, sd],
        compiler_params=pltpu.CompilerParams(dimension_semantics=("arbitrary",), vmem_limit_bytes=VMEM_LIMIT),
    )(c_all, dcs, w, m, v)


_SMALL = ["ada_b", "ada_f_b", "norm1_g", "b_in", "a_ln_g", "a_ln_b", "a_spatial_b", "b_conv_b", "b_gn_g", "b_gn_b",
          "out_norm_a_g", "out_norm_b_g", "norm2_g", "norm_f_g", "a_spatial_w", "b_conv_w"]


def _adam_small(vsum, wssum, gcw, params):
    names = _SMALL
    flat = []
    for n in names:
        flat += list(params[n])

    def body(vs_ref, ws_ref, gcw_ref, *rest):
        ins = rest[:3 * len(names)]
        outs = rest[3 * len(names):]
        for pi, n in enumerate(names):
            w_ref, m_ref, v_ref = ins[3 * pi:3 * pi + 3]
            g_ref, d_ref, m2_ref, v2_ref = outs[4 * pi:4 * pi + 4]
            if n in ("ada_b", "ada_f_b", "b_in"):
                row0 = {"ada_b": 0, "ada_f_b": 6, "b_in": 9}[n]
                pieces = [(vs_ref[row0 + r:row0 + r + 1, :], slice(r * D, (r + 1) * D))
                          for r in range(w_ref.shape[1] // D)]
            elif n == "a_spatial_w":
                pieces = [(ws_ref[...], slice(None))]
            elif n == "b_conv_w":
                pieces = [(gcw_ref[...], slice(None))]
            else:
                row, off, width = _VEC_AT[n]
                pieces = [(vs_ref[row:row + 1, off:off + width], slice(None))]
            for g, cs in pieces:
                g_ref[:, cs] = g
                d_ref[:, cs], m2_ref[:, cs], v2_ref[:, cs] = _adamw(w_ref[:, cs], g, m_ref[:, cs], v_ref[:, cs])

    out_shape = []
    for n in names:
        out_shape += [jax.ShapeDtypeStruct(params[n][0].shape, F32)] * 4
    outs = pl.pallas_call(
        body, name="adam_small",
        in_specs=[VM] * (3 + len(flat)), out_specs=[VM] * len(out_shape), out_shape=out_shape,
        compiler_params=pltpu.CompilerParams(vmem_limit_bytes=VMEM_LIMIT),
    )(vsum, wssum, gcw, *flat)
    return {n: outs[4 * pi:4 * pi + 4] for pi, n in enumerate(names)}


def _token_tile(T, want):
    return want if T % want == 0 else T


def kernel(x, c, ada_w, ada_b, norm1_g, w_in, b_in, a_ln_g, a_ln_b, a_spatial_w, a_spatial_b, b_conv_w, b_conv_b, b_gn_g, b_gn_b, out_norm_a_g, out_norm_b_g, w_out, norm2_g, w_ffn_in, w_ffn_out, ada_f_w, ada_f_b, norm_f_g, loss_target, m_ada_w, m_ada_b, m_norm1_g, m_w_in, m_b_in, m_a_ln_g, m_a_ln_b, m_a_spatial_w, m_a_spatial_b, m_b_conv_w, m_b_conv_b, m_b_gn_g, m_b_gn_b, m_out_norm_a_g, m_out_norm_b_g, m_w_out, m_norm2_g, m_w_ffn_in, m_w_ffn_out, m_ada_f_w, m_ada_f_b, m_norm_f_g, v_ada_w, v_ada_b, v_norm1_g, v_w_in, v_b_in, v_a_ln_g, v_a_ln_b, v_a_spatial_w, v_a_spatial_b, v_b_conv_w, v_b_conv_b, v_b_gn_g, v_b_gn_b, v_out_norm_a_g, v_out_norm_b_g, v_w_out, v_norm2_g, v_w_ffn_in, v_w_ffn_out, v_ada_f_w, v_ada_f_b, v_norm_f_g):
    T = x.shape[1]
    idx = 4 * lax.axis_index("x") + 2 * lax.axis_index("y") + lax.axis_index("c")
    x2d = x.reshape(T, D)
    tgt = loss_target.reshape(T, D)

    conv_s = jnp.pad(b_conv_w[0], ((0, HALO - KW), (0, 0)))
    call, cparts, cfparts, convg, (win_g, wout_g) = _gather(
        c, ada_w[0], ada_b.reshape(NDEV, -1), ada_f_w, ada_f_b.reshape(NDEV, -1), conv_s,
        [w_in[0], w_out[0]])
    wout = wout_g.reshape(D, D)
    mod = jnp.concatenate([cparts.reshape(6, D), cfparts.reshape(2, D)], axis=0)
    cw = jnp.transpose(convg, (1, 0, 2)).reshape(HALO, DB)

    tril = jnp.tril(jnp.ones((CH, CH), dtype=bool))
    wsm = jnp.where(tril[None], a_spatial_w[0], 0.0).astype(BF16)
    wcat = wsm.reshape(NH * CH, CH)
    wcat_t = jnp.transpose(wsm, (0, 2, 1)).reshape(NH * CH, CH)
    bsf = jnp.repeat(a_spatial_b[0].T, DA // NH, axis=1)
    pm = jnp.asarray(_GROUP_MEAN, BF16)
    esel = jnp.asarray(_HEAD_SELECT, BF16)

    tm = _token_tile(T, 256)
    tk = _token_tile(T, 2048)
    (x1, hb, zvg, mixed, yb, o, gu, dgelu_u, dgelu_v, vhat, rslb, yhat, rsg), (wfi_g, wfo_g) = _mix_fwd(
        x2d, mod, norm1_g, win_g, b_in, a_ln_g, a_ln_b, wcat, bsf, cw, b_conv_b, b_gn_g, b_gn_b, out_norm_a_g,
        out_norm_b_g, wout, pm, [w_ffn_in[0].T, w_ffn_out[0]], _token_tile(T, 512))
    dx1, h2b, dgu, act, dxg, acc_f = _ffn(x1, tgt, mod, norm2_g, norm_f_g.reshape(1, D),
                                          wfi_g.reshape(2 * DFF, D), wfo_g.reshape(DFF, D), tm)
    g_wfi = _wgrad_rows(dgu, h2b, 2 * WFI_B, tk, "wgrad_ffn_in").reshape(NDEV, WFI_B, D)
    g_wfo = _wgrad_rows(act, dxg, 2 * WFI_B, tk, "wgrad_ffn_out").reshape(NDEV, DFF // NDEV, D)
    f_sems, f_thru, f_land, f_token = _rs_start([g_wfi, g_wfo], "rs_ffn_start")
    (gx, acc_v, acc_b, acc_a, acc_bs, acc_ws, acc_cw), (g_win, g_wout) = _mix_bwd(
        dx1, x2d, zvg, mixed, o, hb, yb, gu, dgelu_u, dgelu_v, vhat, rslb, yhat, rsg, mod, norm1_g, win_g, a_ln_g,
        a_ln_b, wcat, wcat_t, cw, b_gn_g, b_gn_b, out_norm_a_g, out_norm_b_g, wout, pm, esel, f_token, tm)
    (g_wfi_d, g_wfo_d), (r_wfi, r_wfo) = _rs_wait(f_sems, f_thru, f_land, [acc_v], "rs_ffn_wait")
    g_wout = g_wout.reshape(NDEV, D // NDEV, D)

    vsum, dcond_all, wssum, (q_win, q_wout) = _reduce_small(acc_f, acc_v, acc_b, acc_a, acc_bs, acc_cw, acc_ws,
                                                            g_wfi_d, [g_win, g_wout])
    sems, g_thru, land_thru, token = _rs_start([q_win, q_wout], "rs_mix_start", after=(vsum,), chips=True)

    own = lambda g: (g, jnp.reshape(idx, (1,)).astype(jnp.int32))
    res = {}
    upd_wfi_t = _adam_big(r_wfi, w_ffn_in[0].T, m_w_ffn_in[0].T, v_w_ffn_in[0].T, WFI_B // 2, "adam_w_ffn_in",
                          own=own(g_wfi_d), after=token)
    res["w_ffn_in"] = tuple(a.T for a in upd_wfi_t)
    res["w_ffn_out"] = _adam_big(r_wfo, w_ffn_out[0], m_w_ffn_out[0], v_w_ffn_out[0], DFF // NDEV // 2,
                                 "adam_w_ffn_out", own=own(g_wfo_d), after=token)
    dcond = dcond_all.reshape(NDEV, 8 * D)
    nada = ada_w.shape[2]
    nadf = ada_f_w.shape[1]
    dcs = lax.dynamic_slice(dcond, (0, idx * nada), (NDEV, nada))
    dcfs = lax.dynamic_slice(dcond, (0, 6 * D + idx * nadf), (NDEV, nadf))
    res["ada_w"] = _adam_ada(call, dcs, ada_w[0], m_ada_w[0], v_ada_w[0], 512, "adam_ada_w")
    res["ada_f_w"] = _adam_ada(call, dcfs, ada_f_w, m_ada_f_w, v_ada_f_w, 512, "adam_ada_f_w")
    ncw = b_conv_w.shape[2]
    gcw = jnp.concatenate([lax.dynamic_slice(vsum, (_CW_ROW, idx * ncw), (HALO // 2, ncw)),
                           lax.dynamic_slice(vsum, (_CW_ROW, DB + idx * ncw), (HALO // 2, ncw))], axis=0)[:KW]
    two = lambda a: a.reshape(1, -1) if a.ndim == 1 else a.reshape(-1, a.shape[-1])
    small_in = {
        "ada_b": (ada_b, m_ada_b, v_ada_b), "ada_f_b": (ada_f_b, m_ada_f_b, v_ada_f_b),
        "norm1_g": (norm1_g, m_norm1_g, v_norm1_g), "b_in": (b_in, m_b_in, v_b_in),
        "a_ln_g": (a_ln_g, m_a_ln_g, v_a_ln_g), "a_ln_b": (a_ln_b, m_a_ln_b, v_a_ln_b),
        "a_spatial_b": (a_spatial_b.reshape(1, D), m_a_spatial_b.reshape(1, D), v_a_spatial_b.reshape(1, D)),
        "b_conv_b": (b_conv_b, m_b_conv_b, v_b_conv_b), "b_gn_g": (b_gn_g, m_b_gn_g, v_b_gn_g),
        "b_gn_b": (b_gn_b, m_b_gn_b, v_b_gn_b), "out_norm_a_g": (out_norm_a_g, m_out_norm_a_g, v_out_norm_a_g),
        "out_norm_b_g": (out_norm_b_g, m_out_norm_b_g, v_out_norm_b_g),
        "norm2_g": (norm2_g, m_norm2_g, v_norm2_g), "norm_f_g": (norm_f_g, m_norm_f_g, v_norm_f_g),
        "a_spatial_w": (a_spatial_w, m_a_spatial_w, v_a_spatial_w),
        "b_conv_w": (b_conv_w[0], m_b_conv_w[0], v_b_conv_w[0]),
    }
    small_in = {n: tuple(two(a) for a in t) for n, t in small_in.items()}
    res.update(_adam_small(vsum, wssum, gcw, small_in))
    (q_win_d, q_wout_d), (r_win, r_wout) = _rs_wait(
        sems, g_thru, land_thru,
        [upd_wfi_t[0], res["w_ffn_out"][0], res["ada_w"][0], res["ada_f_w"][0], res["norm_f_g"][0]],
        "rs_mix_wait", chips=True)
    res["w_in"] = _adam_big(r_win, w_in[0], m_w_in[0], v_w_in[0], 512, "adam_w_in", own=own(g_win), sib=q_win_d)
    res["w_out"] = _adam_big(r_wout, w_out[0], m_w_out[0], v_w_out[0], D // NDEV, "adam_w_out", own=own(g_wout),
                             sib=q_wout_d)

    loss = 0.5 / D * jnp.sum(vsum[_LOSS_ROW])
    shapes = {"ada_w": ada_w, "ada_b": ada_b, "norm1_g": norm1_g, "w_in": w_in, "b_in": b_in, "a_ln_g": a_ln_g,
              "a_ln_b": a_ln_b, "a_spatial_w": a_spatial_w, "a_spatial_b": a_spatial_b, "b_conv_w": b_conv_w,
              "b_conv_b": b_conv_b, "b_gn_g": b_gn_g, "b_gn_b": b_gn_b, "out_norm_a_g": out_norm_a_g,
              "out_norm_b_g": out_norm_b_g, "w_out": w_out, "norm2_g": norm2_g, "w_ffn_in": w_ffn_in,
              "w_ffn_out": w_ffn_out, "ada_f_w": ada_f_w, "ada_f_b": ada_f_b, "norm_f_g": norm_f_g}
    order = list(shapes)
    outs = [loss, gx.reshape(x.shape)]
    for which in range(4):
        outs += [res[n][which].reshape(shapes[n].shape) for n in order]
    return tuple(outs)
```

```python
import math

import numpy as np

import jax
import jax.numpy as jnp
from jax import lax
from jax.experimental import pallas as pl
from jax.experimental.pallas import tpu as pltpu

F32 = jnp.float32
BF16 = jnp.bfloat16

D = 1024
DA = 512
DB = 512
DIN = 2048
DFF = 2816
NH = 8
CH = 128
KW = 31
HALO = 32
NDEV = 8
WIN_B = DIN // NDEV
WFI_B = 2 * DFF // NDEV
EPS = 1e-6
NVEC = 40
VMEM_LIMIT = 56 * 1024 * 1024

ADAM_LR, ADAM_B1, ADAM_B2, ADAM_EPS, ADAM_WD, ADAM_STEP = 0.001, 0.9, 0.999, 1e-08, 0.01, 10

MESH = pl.DeviceIdType.MESH

_LANE = np.arange(DB)
_GROUP_MEAN = np.where((_LANE[:, None] >> 6) == (_LANE[None, :] >> 6), 1.0 / 64.0, 0.0).astype(np.float32)
_HEAD_SELECT = np.where((_LANE[:, None] >> 6) == np.arange(CH)[None, :], 1.0, 0.0).astype(np.float32)


def _dot(a, b):
    return jnp.dot(a, b, preferred_element_type=F32)


def _dot_nt(a, b):
    return lax.dot_general(a, b, (((1,), (1,)), ((), ())), preferred_element_type=F32)


def _dot_tn(a, b):
    return lax.dot_general(a, b, (((0,), (0,)), ((), ())), preferred_element_type=F32)


def _rs(v):
    return lax.rsqrt(jnp.mean(v * v, axis=-1, keepdims=True) + EPS)


def _sig(v):
    return 1.0 / (1.0 + jnp.exp(-v))


_INV_SQRT2 = 1.0 / math.sqrt(2.0)
_INV_SQRT2PI = 1.0 / math.sqrt(2.0 * math.pi)


def _gelu_parts(v):
    cdf = 0.5 * (1.0 + lax.erf(v * _INV_SQRT2))
    pdf = jnp.exp(-0.5 * v * v) * _INV_SQRT2PI
    return v * cdf, cdf + v * pdf


def _grp_mean(v, pm):
    hi = v.astype(BF16)
    lo = (v - hi.astype(F32)).astype(BF16)
    return _dot(hi, pm) + _dot(lo, pm)


def _colsum(v):
    return jnp.sum(v, axis=0, keepdims=True)


def _full(shape):
    nd = len(shape)
    return pl.BlockSpec(shape, lambda *_: (0,) * nd)


def _resident(shape):
    nd = len(shape)
    return pl.BlockSpec(shape, lambda *_: (0,) * nd, pipeline_mode=pl.Buffered(1))


HBM = pl.BlockSpec(memory_space=pl.ANY)
VM = pl.BlockSpec(memory_space=pltpu.VMEM)


SH_ROWS = HALO - 8


def _shifted_copies(buf, shbuf, tm):
    for b in range(1, 8):
        shbuf[b - 1] = buf[b:b + tm + SH_ROWS, :]


def _window(buf, shbuf, off, tm):
    a, b = divmod(off, 8)
    if b == 0:
        return buf[8 * a:8 * a + tm, :]
    return shbuf[b - 1, 8 * a:8 * a + tm, :]


def _first_head_lanes():
    return lax.broadcasted_iota(jnp.int32, (CH, CH), 1) < (DA // NH)


def _mix_heads(w_ref, vb, first):
    outs = []
    for p in range(NH // 2):
        v = vb[:, p * CH:(p + 1) * CH]
        a = _dot(w_ref[(2 * p) * CH:(2 * p + 1) * CH, :], v)
        b = _dot(w_ref[(2 * p + 1) * CH:(2 * p + 2) * CH, :], v)
        outs.append(jnp.where(first, a, b))
    return jnp.concatenate(outs, axis=1)


def _place():
    x, y, c = lax.axis_index("x"), lax.axis_index("y"), lax.axis_index("c")
    return x, y, c, 4 * x + 2 * y + c


def _dev(t):
    return (t >> 2, (t >> 1) & 1, t & 1)


class _AllGather:
    def __init__(self, w_in, w_out, wss, wrs, lsem):
        x, y, c, idx = _place()
        me, sibling = (x, y, c), (x, y, 1 - c)
        chips = [(1 - x, y), (x, 1 - y), (1 - x, 1 - y)]
        nw = len(w_in)

        def blk(p):
            return 4 * p[0] + 2 * p[1] + p[2]

        def wcopy(a, k, block, to, src=None):
            dst = w_out[a].at[blk(block)]
            return pltpu.make_async_remote_copy(src_ref=dst if src is None else src, dst_ref=dst,
                                                send_sem=wss.at[a, k], recv_sem=wrs.at[a, k],
                                                device_id=to, device_id_type=MESH)

        self.mine = [pltpu.make_async_copy(w_in[a], w_out[a].at[idx], lsem.at[a]) for a in range(nw)]
        self.first = []
        for a in range(nw):
            self.first.append(wcopy(a, 0, me, sibling, src=w_in[a]))
            self.first += [wcopy(a, 1 + j, me, (*chip, c), src=w_in[a]) for j, chip in enumerate(chips)]
        self.landed = [[wcopy(a, 1 + j, (*chip, c), me) for a in range(nw)] for j, chip in enumerate(chips)]
        self.passed = [[wcopy(a, 4 + j, (*chip, c), sibling) for a in range(nw)] for j, chip in enumerate(chips)]
        self.from_sibling = []
        for a in range(nw):
            self.from_sibling.append(wcopy(a, 0, sibling, me))
            self.from_sibling += [wcopy(a, 4 + j, (*chip, 1 - c), me) for j, chip in enumerate(chips)]

    def start(self):
        for cp in self.mine + self.first:
            cp.start()

    def forward(self):
        for land, pas in zip(self.landed, self.passed):
            for l, p in zip(land, pas):
                l.wait_recv()
                p.start()

    def finish(self):
        for cp in self.from_sibling:
            cp.wait_recv()
        for cp in self.first:
            cp.wait_send()
        for pas in self.passed:
            for p in pas:
                p.wait_send()
        for cp in self.mine:
            cp.wait()


AG_SEMS = lambda nw: [pltpu.SemaphoreType.DMA((nw, 7)), pltpu.SemaphoreType.DMA((nw, 7)),
                      pltpu.SemaphoreType.DMA((nw,))]


def _mix_fwd(x, mod, g1, win, b_in, lng, lnb, wcat, bsf, cw, cb, gng, gnb, oga, ogb, wout, pm, ffn_shards, tm):
    T = x.shape[0]
    nt = T // tm
    nch = tm // CH
    nw = len(ffn_shards)
    fwd_step = (5 * nt) // 8
    saved = [(D, F32), (D, BF16), (2 * DB, F32), (DA, F32), (D, BF16), (D, F32), (DA, F32), (DA, F32), (DA, F32),
             (DA, F32), (CH, F32), (DB, F32), (DB, F32)]
    NSAVE = len(saved)

    def body(x_ref, mod_ref, g1_ref, win_ref, bin_ref, lng_ref, lnb_ref, wcat_ref, bsf_ref, cw_ref, cb_ref,
             gng_ref, gnb_ref, oga_ref, ogb_ref, wout_ref, pm_ref, *rest):
        sh_f32 = rest[:nw]
        (x1_ref, h_ref, zvg_ref, mixed_ref, y_ref, o_ref, gu_ref, dgu_ref, dgv_ref, vhat_ref, rsl_ref, yhat_ref,
         rsg_ref) = rest[nw:nw + NSAVE]
        sh_out = rest[nw + NSAVE:2 * nw + NSAVE]
        glbuf, shbuf = rest[2 * nw + NSAVE:2 * nw + NSAVE + 2]
        sh_in = rest[2 * nw + NSAVE + 2:3 * nw + NSAVE + 2]
        wss, wrs, lsem = rest[3 * nw + NSAVE + 2:]
        i = pl.program_id(0)

        @pl.when(i == 0)
        def _():
            for a in range(nw):
                sh_in[a][...] = sh_f32[a][...].astype(BF16)
            _AllGather(sh_in, sh_out, wss, wrs, lsem).start()

        xv = x_ref[...]
        shift1 = mod_ref[0:1, :]
        scale1 = mod_ref[1:2, :]
        gate1 = mod_ref[2:3, :]
        h = (xv * _rs(xv) * g1_ref[...]) * (1.0 + scale1) + shift1
        hb = h.astype(BF16)
        h_ref[...] = hb
        z = jnp.concatenate([_dot(hb, win_ref[j]) for j in range(NDEV)], axis=1) + bin_ref[...]
        zvg_ref[...] = z[:, 2 * DA:]
        gu, dgelu_u = _gelu_parts(z[:, 0:DA])
        gv, dgelu_v = _gelu_parts(z[:, DA:2 * DA])
        gu_ref[...] = gu
        dgu_ref[...] = dgelu_u
        dgv_ref[...] = dgelu_v
        xc = gv - jnp.mean(gv, axis=-1, keepdims=True)
        rsl = lax.rsqrt(jnp.mean(xc * xc, axis=-1, keepdims=True) + EPS)
        vhat = xc * rsl
        vhat_ref[...] = vhat
        rsl_ref[...] = jnp.broadcast_to(rsl, (tm, CH))
        vnb = (vhat * lng_ref[...] + lnb_ref[...]).astype(BF16)
        first = _first_head_lanes()
        chunks = []
        for ci in range(nch):
            chunks.append(_mix_heads(wcat_ref, vnb[ci * CH:(ci + 1) * CH, :], first) + bsf_ref[...])
        mixed = jnp.concatenate(chunks, axis=0) if nch > 1 else chunks[0]
        mixed_ref[...] = mixed
        ya = gu * mixed
        gl = z[:, 2 * DA:2 * DA + DB] * _sig(z[:, 2 * DA + DB:])

        @pl.when(i == 0)
        def _():
            glbuf[0:HALO, :] = jnp.zeros((HALO, DB), F32)

        glbuf[HALO:HALO + tm, :] = gl
        _shifted_copies(glbuf, shbuf, tm)
        yc = jnp.zeros((tm, DB), F32) + cb_ref[...]
        for k in range(KW):
            yc = yc + cw_ref[k:k + 1, :] * _window(glbuf, shbuf, HALO - (KW - 1) + k, tm)
        glbuf[0:HALO, :] = gl[tm - HALO:, :]
        pmv = pm_ref[...]
        dc = yc - _grp_mean(yc, pmv)
        rsg = lax.rsqrt(_grp_mean(dc * dc, pmv) + EPS)
        yhat = dc * rsg
        yhat_ref[...] = yhat
        rsg_ref[...] = rsg
        yg = yhat * gng_ref[...] + gnb_ref[...]
        yb = yg * _sig(yg)
        na = ya * _rs(ya) * oga_ref[...]
        nb = yb * _rs(yb) * ogb_ref[...]
        yv = jnp.concatenate([na, nb], axis=1).astype(BF16)
        y_ref[...] = yv
        o = _dot(yv, wout_ref[...])
        o_ref[...] = o
        x1_ref[...] = xv + gate1 * o

        @pl.when(i == fwd_step)
        def _():
            _AllGather(sh_in, sh_out, wss, wrs, lsem).forward()

        @pl.when(i == nt - 1)
        def _():
            _AllGather(sh_in, sh_out, wss, wrs, lsem).finish()

    tile = lambda w: pl.BlockSpec((tm, w), lambda i: (i, 0))
    outs = pl.pallas_call(
        body,
        name="mix_fwd",
        grid=(nt,),
        in_specs=[tile(D), _full((8, D)), _full((1, D)), _resident((NDEV, D, WIN_B)), _full((1, DIN)),
                  _full((1, DA)), _full((1, DA)), _full((NH * CH, CH)), _full((CH, DA)), _full((HALO, DB)),
                  _full((1, DB)), _full((1, DB)), _full((1, DB)), _full((1, DA)), _full((1, DB)),
                  _resident((D, D)), _full((DB, DB))] + [_resident(s.shape) for s in ffn_shards],
        out_specs=[tile(w) for w, _ in saved] + [HBM] * nw,
        out_shape=[jax.ShapeDtypeStruct((T, w), dt) for w, dt in saved]
                  + [jax.ShapeDtypeStruct((NDEV,) + s.shape, BF16) for s in ffn_shards],
        scratch_shapes=[pltpu.VMEM((HALO + tm, DB), F32), pltpu.VMEM((7, tm + SH_ROWS, DB), F32)]
                       + [pltpu.VMEM(s.shape, BF16) for s in ffn_shards] + AG_SEMS(nw),
        compiler_params=pltpu.CompilerParams(dimension_semantics=("arbitrary",), vmem_limit_bytes=VMEM_LIMIT),
    )(x, mod, g1, win, b_in, lng, lnb, wcat, bsf, cw, cb, gng, gnb, oga, ogb, wout, pm, *ffn_shards)
    return outs[:NSAVE], outs[NSAVE:]


FF_BLOCKS = ((0, 1024), (1024, 1024), (2048, 768))


def _ffn(x1, tgt, mod, g2, gf, wfi_t, wfo, tm):
    T = x1.shape[0]
    nt = T // tm

    def body(x1_ref, tgt_ref, mod_ref, g2_ref, gf_ref, wfi_ref, wfo_ref,
             dx1_ref, h2_ref, dgu_ref, act_ref, dxg_ref, acc_ref, g_s, u_s):
        i = pl.program_id(0)

        @pl.when(i == 0)
        def _():
            acc_ref[...] = jnp.zeros((8, D), F32)

        x1 = x1_ref[...]
        shift2 = mod_ref[3:4, :]
        scale2 = mod_ref[4:5, :]
        gate2 = mod_ref[5:6, :]
        shiftf = mod_ref[6:7, :]
        scalef = mod_ref[7:8, :]
        g2v = g2_ref[...]
        gfv = gf_ref[...]
        r2 = _rs(x1)
        xn2 = x1 * r2
        h2b = (xn2 * g2v * (1.0 + scale2) + shift2).astype(BF16)
        h2_ref[...] = h2b
        f = jnp.zeros((tm, D), F32)
        for o, w in FF_BLOCKS:
            g = _dot_nt(h2b, wfi_ref[o:o + w, :])
            u = _dot_nt(h2b, wfi_ref[DFF + o:DFF + o + w, :])
            g_s[:, o:o + w] = g
            u_s[:, o:o + w] = u
            actb = (g * _sig(g) * u).astype(BF16)
            act_ref[:, o:o + w] = actb
            f = f + _dot(actb, wfo_ref[o:o + w, :])
        x2 = x1 + gate2 * f
        rf = _rs(x2)
        xnf = x2 * rf
        out = xnf * gfv * (1.0 + scalef) + shiftf
        e = out - tgt_ref[...]
        dout = e * (1.0 / D)
        acc_ref[7:8, :] += _colsum(e * e)
        acc_ref[0:1, :] += _colsum(dout)
        acc_ref[1:2, :] += _colsum(dout * xnf * gfv)
        acc_ref[2:3, :] += _colsum(dout * (1.0 + scalef) * xnf)
        dxnf = dout * (1.0 + scalef) * gfv
        dx2 = rf * (dxnf - xnf * jnp.mean(dxnf * xnf, axis=-1, keepdims=True))
        acc_ref[3:4, :] += _colsum(dx2 * f)
        dxgb = (dx2 * gate2).astype(BF16)
        dxg_ref[...] = dxgb
        dh2 = jnp.zeros((tm, D), F32)
        for o, w in FF_BLOCKS:
            dact = _dot_nt(dxgb, wfo_ref[o:o + w, :])
            g = g_s[:, o:o + w]
            u = u_s[:, o:o + w]
            s = _sig(g)
            dgb = (dact * u * (s * (1.0 + g * (1.0 - s)))).astype(BF16)
            dub = (dact * (g * s)).astype(BF16)
            dgu_ref[:, o:o + w] = dgb
            dgu_ref[:, DFF + o:DFF + o + w] = dub
            dh2 = dh2 + _dot(dgb, wfi_ref[o:o + w, :])
            dh2 = dh2 + _dot(dub, wfi_ref[DFF + o:DFF + o + w, :])
        acc_ref[4:5, :] += _colsum(dh2)
        acc_ref[5:6, :] += _colsum(dh2 * xn2 * g2v)
        acc_ref[6:7, :] += _colsum(dh2 * (1.0 + scale2) * xn2)
        dxn2 = dh2 * (1.0 + scale2) * g2v
        dx1_ref[...] = dx2 + r2 * (dxn2 - xn2 * jnp.mean(dxn2 * xn2, axis=-1, keepdims=True))

    tile = lambda w: pl.BlockSpec((tm, w), lambda i: (i, 0))
    return pl.pallas_call(
        body,
        name="ffn_fwd_bwd",
        grid=(nt,),
        in_specs=[tile(D), tile(D), _full((8, D)), _full((1, D)), _full((1, D)),
                  _resident((2 * DFF, D)), _resident((DFF, D))],
        out_specs=[tile(D), tile(D), tile(2 * DFF), tile(DFF), tile(D), _full((8, D))],
        out_shape=[jax.ShapeDtypeStruct((T, D), F32), jax.ShapeDtypeStruct((T, D), BF16),
                   jax.ShapeDtypeStruct((T, 2 * DFF), BF16), jax.ShapeDtypeStruct((T, DFF), BF16),
                   jax.ShapeDtypeStruct((T, D), BF16), jax.ShapeDtypeStruct((8, D), F32)],
        scratch_shapes=[pltpu.VMEM((tm, DFF), F32), pltpu.VMEM((tm, DFF), F32)],
        compiler_params=pltpu.CompilerParams(dimension_semantics=("arbitrary",), vmem_limit_bytes=VMEM_LIMIT),
    )(x1, tgt, mod, g2, gf, wfi_t, wfo)


def _mix_bwd(dx1, x, zvg, mixed, o, hb, yb, gu, dgu, dgv, vhat, rslb, yhat, rsg, mod, g1, win, lng, lnb, wcat, wcat_t,
             cw, gng, gnb, oga, ogb, wout, pm, esel, after, tm):
    T = x.shape[0]
    nt = T // tm
    nch = tm // CH
    WOB = 256

    def body(dx1_ref, x_ref, zvg_ref, mixed_ref, o_ref, hb_ref, yb_ref, gu_ref, dgu_ref, dgv_ref, vhat_ref, rsl_ref,
             yhat_ref, rsg_ref, mod_ref, g1_ref, win_ref, lng_ref, lnb_ref, wcat_ref, wcatt_ref, cw_ref, gng_ref,
             gnb_ref, oga_ref, ogb_ref, wout_ref, pm_ref, esel_ref, after_ref,
             gx_ref, accv_ref, accb_ref, acca_ref, accbs_ref, accws_ref, acccw_ref, gwin_ref, gwout_ref,
             dycbuf, shbuf, bs_s, acc_win, acc_wout, st_win, st_wout):
        i = pl.program_id(0)

        @pl.when(i == 0)
        def _():
            acc_win[...] = jnp.zeros((NDEV, D, WIN_B), F32)
            acc_wout[...] = jnp.zeros((D, D), F32)
            accv_ref[...] = jnp.zeros((8, D), F32)
            accb_ref[...] = jnp.zeros((1, DIN), F32)
            acca_ref[...] = jnp.zeros((8, DA), F32)
            accws_ref[...] = jnp.zeros((NH * CH, CH), F32)
            acccw_ref[...] = jnp.zeros((HALO, DB), F32)
            bs_s[...] = jnp.zeros((CH, DA), F32)
            dycbuf[tm:tm + HALO, :] = jnp.zeros((HALO, DB), F32)

        shift1 = mod_ref[0:1, :]
        scale1 = mod_ref[1:2, :]
        gate1 = mod_ref[2:3, :]
        g1v = g1_ref[...]
        xv = x_ref[...]
        r1 = _rs(xv)
        xn1 = xv * r1
        val = zvg_ref[:, 0:DB]
        gate = zvg_ref[:, DB:]
        gu = gu_ref[...]
        dgelu_u = dgu_ref[...]
        dgelu_v = dgv_ref[...]
        vhat = vhat_ref[...]
        rsl = rsl_ref[:, 0:1]
        lngv = lng_ref[...]
        vnb = (vhat * lngv + lnb_ref[...]).astype(BF16)
        mixed = mixed_ref[...]
        ya = gu * mixed
        ra = _rs(ya)
        yan = ya * ra
        sgt = _sig(gate)
        gl = val * sgt
        pmv = pm_ref[...]
        rsg = rsg_ref[...]
        yhat = yhat_ref[...]
        gngv = gng_ref[...]
        yg = yhat * gngv + gnb_ref[...]
        sgy = _sig(yg)
        yb = yg * sgy
        rb = _rs(yb)
        ybn = yb * rb
        dx1 = dx1_ref[...]
        accv_ref[0:1, :] += _colsum(dx1 * o_ref[...])
        dogb = (dx1 * gate1).astype(BF16)
        acc_wout[...] += _dot_tn(yb_ref[...], dogb)
        dy = _dot_nt(dogb, wout_ref[...])
        dna = dy[:, 0:DA]
        dnb = dy[:, DA:]
        ogav = oga_ref[...]
        ogbv = ogb_ref[...]
        acca_ref[2:3, :] += _colsum(dna * yan)
        acca_ref[3:4, :] += _colsum(dnb * ybn)
        ta = dna * ogav
        dya = ra * (ta - yan * jnp.mean(ta * yan, axis=-1, keepdims=True))
        tb = dnb * ogbv
        dyb = rb * (tb - ybn * jnp.mean(tb * ybn, axis=-1, keepdims=True))
        dgu = dya * mixed
        dm = dya * gu
        first = _first_head_lanes()
        zero = jnp.zeros((CH, CH), BF16)
        dvn_chunks = []
        bs_acc = bs_s[...]
        for ci in range(nch):
            dmc = dm[ci * CH:(ci + 1) * CH, :]
            bs_acc = bs_acc + dmc
            dmcb = dmc.astype(BF16)
            dvn_chunks.append(_mix_heads(wcatt_ref, dmcb, first))
            vc = vnb[ci * CH:(ci + 1) * CH, :]
            for p in range(NH // 2):
                xt = dmcb[:, p * CH:(p + 1) * CH]
                vt = vc[:, p * CH:(p + 1) * CH]
                accws_ref[(2 * p) * CH:(2 * p + 1) * CH, :] += _dot_nt(jnp.where(first, xt, zero), vt)
                accws_ref[(2 * p + 1) * CH:(2 * p + 2) * CH, :] += _dot_nt(jnp.where(first, zero, xt), vt)
        bs_s[...] = bs_acc
        dvn = jnp.concatenate(dvn_chunks, axis=0) if nch > 1 else dvn_chunks[0]
        acca_ref[0:1, :] += _colsum(dvn * vhat)
        acca_ref[1:2, :] += _colsum(dvn)
        dvh = dvn * lngv
        dgv = rsl * (dvh - jnp.mean(dvh, axis=-1, keepdims=True)
                     - vhat * jnp.mean(dvh * vhat, axis=-1, keepdims=True))
        du = dgu * dgelu_u
        dv = dgv * dgelu_v
        dyg = dyb * (sgy * (1.0 + yg * (1.0 - sgy)))
        acca_ref[5:6, :] += _colsum(dyg * yhat)
        acca_ref[6:7, :] += _colsum(dyg)
        dyh = dyg * gngv
        dyc = rsg * (dyh - _grp_mean(dyh, pmv) - yhat * _grp_mean(dyh * yhat, pmv))
        acca_ref[4:5, :] += _colsum(dyc)
        dycbuf[0:tm, :] = dyc
        _shifted_copies(dycbuf, shbuf, tm)
        dgl = jnp.zeros((tm, DB), F32)
        for k in range(KW):
            win_k = _window(dycbuf, shbuf, KW - 1 - k, tm)
            dgl = dgl + cw_ref[k:k + 1, :] * win_k
            acccw_ref[k:k + 1, :] += _colsum(win_k * gl)
        dycbuf[tm:tm + HALO, :] = dyc[0:HALO, :]
        dval = dgl * sgt
        dgate = dgl * val * sgt * (1.0 - sgt)
        dz = jnp.concatenate([du, dv, dval, dgate], axis=1)
        accb_ref[...] += _colsum(dz)
        dzb = dz.astype(BF16)
        hbv = hb_ref[...]
        dh = jnp.zeros((tm, D), F32)
        for j in range(NDEV):
            dzj = dzb[:, j * WIN_B:(j + 1) * WIN_B]
            acc_win[j] += _dot_tn(hbv, dzj)
            dh = dh + _dot_nt(dzj, win_ref[j])
        accv_ref[1:2, :] += _colsum(dh)
        dh_xn = _colsum(dh * xn1)
        accv_ref[2:3, :] += dh_xn * g1v
        accv_ref[3:4, :] += dh_xn * (1.0 + scale1)
        dxn1 = dh * (1.0 + scale1) * g1v
        gx_ref[...] = dx1 + r1 * (dxn1 - xn1 * jnp.mean(dxn1 * xn1, axis=-1, keepdims=True))

        @pl.when(i == nt - 1)
        def _():
            rows = lax.broadcasted_iota(jnp.int32, (NH * CH, CH), 0) & (CH - 1)
            cols = lax.broadcasted_iota(jnp.int32, (NH * CH, CH), 1)
            accws_ref[...] = jnp.where(cols <= rows, accws_ref[...], 0.0)
            bs = bs_s[...]
            hi = bs.astype(BF16)
            r1_ = bs - hi.astype(F32)
            mid = r1_.astype(BF16)
            lo = (r1_ - mid.astype(F32)).astype(BF16)
            ev = esel_ref[...]
            accbs_ref[...] = _dot(hi, ev) + _dot(mid, ev) + _dot(lo, ev)
            for j in range(NDEV):
                st_win[...] = acc_win[j].astype(BF16)
                pltpu.sync_copy(st_win, gwin_ref.at[j])
            for j in range(D // WOB):
                st_wout[...] = acc_wout[j * WOB:(j + 1) * WOB, :].astype(BF16)
                pltpu.sync_copy(st_wout, gwout_ref.at[pl.ds(j * WOB, WOB)])

    rev = lambda w: pl.BlockSpec((tm, w), lambda i: (nt - 1 - i, 0))
    outs = pl.pallas_call(
        body,
        name="mix_bwd",
        grid=(nt,),
        in_specs=[rev(D), rev(D), rev(2 * DB), rev(DA), rev(D), rev(D), rev(D), rev(DA), rev(DA), rev(DA), rev(DA),
                  rev(CH), rev(DB), rev(DB), _full((8, D)), _full((1, D)),
                  _resident((NDEV, D, WIN_B)), _full((1, DA)), _full((1, DA)), _full((NH * CH, CH)),
                  _full((NH * CH, CH)), _full((HALO, DB)), _full((1, DB)), _full((1, DB)), _full((1, DA)),
                  _full((1, DB)), _resident((D, D)), _full((DB, DB)), _full((DA, CH)), HBM],
        out_specs=[rev(D), _full((8, D)), _full((1, DIN)), _full((8, DA)), _full((CH, CH)),
                   _full((NH * CH, CH)), _full((HALO, DB)), HBM, HBM],
        out_shape=[jax.ShapeDtypeStruct((T, D), F32), jax.ShapeDtypeStruct((8, D), F32),
                   jax.ShapeDtypeStruct((1, DIN), F32), jax.ShapeDtypeStruct((8, DA), F32),
                   jax.ShapeDtypeStruct((CH, CH), F32), jax.ShapeDtypeStruct((NH * CH, CH), F32),
                   jax.ShapeDtypeStruct((HALO, DB), F32),
                   jax.ShapeDtypeStruct((NDEV, D, WIN_B), BF16), jax.ShapeDtypeStruct((D, D), BF16)],
        scratch_shapes=[pltpu.VMEM((tm + HALO, DB), F32), pltpu.VMEM((7, tm + SH_ROWS, DB), F32),
                        pltpu.VMEM((CH, DA), F32), pltpu.VMEM((NDEV, D, WIN_B), F32), pltpu.VMEM((D, D), F32),
                        pltpu.VMEM((D, WIN_B), BF16), pltpu.VMEM((WOB, D), BF16)],
        compiler_params=pltpu.CompilerParams(dimension_semantics=("arbitrary",), vmem_limit_bytes=VMEM_LIMIT),
    )(dx1, x, zvg, mixed, o, hb, yb, gu, dgu, dgv, vhat, rslb, yhat, rsg, mod, g1, win, lng, lnb, wcat, wcat_t, cw,
      gng, gnb, oga, ogb, wout, pm, esel, after)
    return outs[:7], outs[7:]


def _wgrad_rows(a, b, bm, tk, name):
    T, M = a.shape
    N = b.shape[1]
    nk = T // tk

    def body(a_ref, b_ref, o_ref, acc):
        k = pl.program_id(1)

        @pl.when(k == 0)
        def _():
            acc[...] = jnp.zeros((bm, N), F32)

        acc[...] += _dot_tn(a_ref[...], b_ref[...])

        @pl.when(k == nk - 1)
        def _():
            o_ref[...] = acc[...].astype(BF16)

    return pl.pallas_call(
        body, name=name, grid=(M // bm, nk),
        in_specs=[pl.BlockSpec((tk, bm), lambda j, k: (k, j)), pl.BlockSpec((tk, N), lambda j, k: (k, 0))],
        out_specs=pl.BlockSpec((bm, N), lambda j, k: (j, 0)),
        out_shape=jax.ShapeDtypeStruct((M, N), BF16),
        scratch_shapes=[pltpu.VMEM((bm, N), F32)],
        compiler_params=pltpu.CompilerParams(dimension_semantics=("arbitrary", "arbitrary"),
                                             vmem_limit_bytes=VMEM_LIMIT),
    )(a, b)


def _small_copy(src, dst, ss, rs, k, to):
    return pltpu.make_async_remote_copy(src_ref=src, dst_ref=dst, send_sem=ss.at[k], recv_sem=rs.at[k],
                                        device_id=to, device_id_type=MESH)


def _gather(c_row, ada_w, ada_b8, ada_f_w, ada_f_b8, conv_s, shards):
    nw = len(shards)

    def body(c_ref, adaw_ref, adab_ref, adafw_ref, adafb_ref, conv_ref, *rest):
        w_f32 = rest[:nw]
        call_ref, cparts_ref, cfparts_ref, convg_ref = rest[nw:nw + 4]
        w_out = rest[nw + 4:2 * nw + 4]
        part_s, partf_s = rest[2 * nw + 4:2 * nw + 6]
        w_in = rest[2 * nw + 6:3 * nw + 6]
        wss, wrs, lsem, s1, r1, s2, r2, s3, r3, s4, r4 = rest[3 * nw + 6:]
        x, y, c, idx = _place()
        me = (x, y, c)
        for a in range(nw):
            w_in[a][...] = w_f32[a][...].astype(BF16)
        ag = _AllGather(w_in, w_out, wss, wrs, lsem)
        ag.start()
        call_ref[pl.ds(idx, 1), :] = c_ref[...]
        convg_ref[idx] = conv_ref[...]
        ph1 = []
        for k in range(1, NDEV):
            to = _dev(idx ^ k)
            ph1.append(_small_copy(c_ref, call_ref.at[pl.ds(idx, 1)], s1, r1, k - 1, to))
            ph1.append(_small_copy(conv_ref, convg_ref.at[idx], s2, r2, k - 1, to))
        for cp in ph1:
            cp.start()
        for k in range(1, NDEV):
            src_dev = idx ^ k
            _small_copy(c_ref, call_ref.at[pl.ds(src_dev, 1)], s1, r1, k - 1, me).wait_recv()
            _small_copy(conv_ref, convg_ref.at[src_dev], s2, r2, k - 1, me).wait_recv()
        call = call_ref[...]
        cact = (call * _sig(call))
        part_s[...] = jnp.dot(cact, adaw_ref[...], preferred_element_type=F32,
                              precision=lax.Precision.HIGHEST) + adab_ref[pl.ds(idx, 1), :]
        partf_s[...] = jnp.dot(cact, adafw_ref[...], preferred_element_type=F32,
                               precision=lax.Precision.HIGHEST) + adafb_ref[pl.ds(idx, 1), :]
        cparts_ref[pl.ds(idx, 1), :] = part_s[pl.ds(idx, 1), :]
        cfparts_ref[pl.ds(idx, 1), :] = partf_s[pl.ds(idx, 1), :]
        ph2 = []
        for k in range(1, NDEV):
            t = idx ^ k
            ph2.append(_small_copy(part_s.at[pl.ds(t, 1)], cparts_ref.at[pl.ds(idx, 1)], s3, r3, k - 1, _dev(t)))
            ph2.append(_small_copy(partf_s.at[pl.ds(t, 1)], cfparts_ref.at[pl.ds(idx, 1)], s4, r4, k - 1, _dev(t)))
        for cp in ph2:
            cp.start()
        for k in range(1, NDEV):
            src_dev = idx ^ k
            _small_copy(part_s.at[pl.ds(0, 1)], cparts_ref.at[pl.ds(src_dev, 1)], s3, r3, k - 1, me).wait_recv()
            _small_copy(partf_s.at[pl.ds(0, 1)], cfparts_ref.at[pl.ds(src_dev, 1)], s4, r4, k - 1, me).wait_recv()
        for cp in ph1 + ph2:
            cp.wait_send()
        ag.forward()
        ag.finish()

    dma7 = pltpu.SemaphoreType.DMA((NDEV - 1,))
    outs = pl.pallas_call(
        body,
        name="gather_weights",
        in_specs=[VM] * (6 + nw),
        out_specs=[VM] * 4 + [HBM] * nw,
        out_shape=[jax.ShapeDtypeStruct((NDEV, D), F32), jax.ShapeDtypeStruct((NDEV, ada_w.shape[1]), F32),
                   jax.ShapeDtypeStruct((NDEV, ada_f_w.shape[1]), F32),
                   jax.ShapeDtypeStruct((NDEV,) + conv_s.shape, F32)]
                  + [jax.ShapeDtypeStruct((NDEV,) + s.shape, BF16) for s in shards],
        scratch_shapes=[pltpu.VMEM((NDEV, ada_w.shape[1]), F32), pltpu.VMEM((NDEV, ada_f_w.shape[1]), F32)]
                       + [pltpu.VMEM(s.shape, BF16) for s in shards] + AG_SEMS(nw) + [dma7] * 8,
        compiler_params=pltpu.CompilerParams(vmem_limit_bytes=VMEM_LIMIT),
    )(c_row, ada_w, ada_b8, ada_f_w, ada_f_b8, conv_s, *shards)
    return outs[0], outs[1], outs[2], outs[3], outs[4:]


_VEC_AT = {
    "norm1_g": (8, 0, D), "a_ln_g": (11, 0, DA), "a_ln_b": (11, DA, DA), "a_spatial_b": (12, 0, D),
    "b_conv_b": (13, 0, DB), "b_gn_g": (13, DB, DB), "b_gn_b": (14, 0, DB), "out_norm_a_g": (14, DB, DA),
    "out_norm_b_g": (15, 0, DB), "norm2_g": (16, 0, D), "norm_f_g": (17, 0, D),
}
_LOSS_ROW = 18
_CW_ROW = 24


def _reduce_small(acc_f, acc_v, acc_b, acc_a, acc_bs, acc_cw, dws, after, pair_grads):
    npg = len(pair_grads)

    def body(accf_ref, accv_ref, accb_ref, acca_ref, accbs_ref, acccw_ref, dws_ref, after_ref, *rest):
        pg = rest[:npg]
        vsum_ref, dcond_ref, wssum_ref = rest[npg:npg + 3]
        pq = rest[npg + 3:2 * npg + 3]
        vloc, vbuf, wbuf, wown, vss, vrs, vls, s2, r2, s3, r3 = rest[2 * npg + 3:2 * npg + 14]
        pland = rest[2 * npg + 14:3 * npg + 14]
        pstage = rest[3 * npg + 14:4 * npg + 14]
        ps, pr, pls, pss = rest[4 * npg + 14:]
        x, y, c, idx = _place()
        me = (x, y, c)
        sibling = (x, y, 1 - c)
        chips = [(1 - x, y), (x, 1 - y), (1 - x, 1 - y)]
        blk = lambda p: 4 * p[0] + 2 * p[1] + p[2]
        give = [blk((*ch, 1 - c)) for ch in chips] + [blk(sibling)]
        pair = [pltpu.make_async_remote_copy(src_ref=pg[a].at[b], dst_ref=pland[a].at[j], send_sem=ps.at[a, j],
                                             recv_sem=pr.at[a, j], device_id=sibling, device_id_type=MESH)
                for a in range(npg) for j, b in enumerate(give)]
        loads = [pltpu.make_async_copy(pg[a].at[blk((*ch, c))], pstage[a].at[j], pls.at[a, j])
                 for a in range(npg) for j, ch in enumerate(chips)]
        for cp in pair + loads:
            cp.start()
        vloc[...] = jnp.zeros((NVEC, D), F32)
        vloc[0:1, :] = accv_ref[1:2, :]
        vloc[1:2, :] = accv_ref[2:3, :]
        vloc[2:3, :] = accv_ref[0:1, :]
        vloc[3:4, :] = accf_ref[4:5, :]
        vloc[4:5, :] = accf_ref[5:6, :]
        vloc[5:6, :] = accf_ref[3:4, :]
        vloc[6:7, :] = accf_ref[0:1, :]
        vloc[7:8, :] = accf_ref[1:2, :]
        vloc[8:9, :] = accv_ref[3:4, :]
        vloc[9:10, :] = accb_ref[:, 0:D]
        vloc[10:11, :] = accb_ref[:, D:]
        vloc[11:12, 0:DA] = acca_ref[0:1, :]
        vloc[11:12, DA:] = acca_ref[1:2, :]
        bst = accbs_ref[...].T
        for h in range(NH):
            vloc[12:13, h * CH:(h + 1) * CH] = bst[h:h + 1, :]
        vloc[13:14, 0:DB] = acca_ref[4:5, :]
        vloc[13:14, DB:] = acca_ref[5:6, :]
        vloc[14:15, 0:DB] = acca_ref[6:7, :]
        vloc[14:15, DB:] = acca_ref[2:3, :]
        vloc[15:16, 0:DB] = acca_ref[3:4, :]
        vloc[16:17, :] = accf_ref[6:7, :]
        vloc[17:18, :] = accf_ref[2:3, :]
        vloc[_LOSS_ROW:_LOSS_ROW + 1, :] = accf_ref[7:8, :]
        vloc[_CW_ROW:_CW_ROW + HALO // 2, 0:DB] = acccw_ref[0:HALO // 2, :]
        vloc[_CW_ROW:_CW_ROW + HALO // 2, DB:] = acccw_ref[HALO // 2:, :]
        gather_vecs = _AllGather([vloc], [vbuf], vss, vrs, vls)
        gather_vecs.start()
        rows_of = lambda t: pl.ds(pl.multiple_of(t * CH, CH), CH)
        wbuf[0] = dws_ref[rows_of(idx), :]
        sm = []
        for k in range(1, NDEV):
            t = idx ^ k
            sm.append(_small_copy(dws_ref.at[rows_of(t)], wbuf.at[k], s2, r2, k - 1, _dev(t)))
        for cp in sm:
            cp.start()
        for k in range(1, NDEV):
            _small_copy(dws_ref.at[rows_of(0)], wbuf.at[k], s2, r2, k - 1, me).wait_recv()
        ws = wbuf[0]
        for k in range(1, NDEV):
            ws = ws + wbuf[k]
        wown[...] = ws
        wssum_ref[rows_of(idx), :] = ws
        ag = [_small_copy(wown, wssum_ref.at[rows_of(idx)], s3, r3, k - 1, _dev(idx ^ k)) for k in range(1, NDEV)]
        for cp in ag:
            cp.start()
        for cp in loads:
            cp.wait()
        for cp in pair:
            cp.wait_recv()
        stores = []
        for a in range(npg):
            for j in range(3):
                pstage[a][j] = (pstage[a][j].astype(F32) + pland[a][j].astype(F32)).astype(BF16)
                stores.append(pltpu.make_async_copy(pstage[a].at[j], pq[a].at[j], pss.at[a, j]))
            stores.append(pltpu.make_async_copy(pland[a].at[3], pq[a].at[3], pss.at[a, 3]))
        for cp in stores:
            cp.start()
        gather_vecs.forward()
        gather_vecs.finish()
        vs = vbuf[0]
        for d in range(1, NDEV):
            vs = vs + vbuf[d]
        vsum_ref[...] = vs
        for d in range(NDEV):
            dcond_ref[d] = vbuf[d, 0:8, :]
        for k in range(1, NDEV):
            _small_copy(wown, wssum_ref.at[rows_of(idx ^ k)], s3, r3, k - 1, me).wait_recv()
        for cp in sm + ag:
            cp.wait_send()
        for cp in stores:
            cp.wait()
        for cp in pair:
            cp.wait_send()

    dma7 = pltpu.SemaphoreType.DMA((NDEV - 1,))
    dma4 = pltpu.SemaphoreType.DMA((npg, 4))
    outs = pl.pallas_call(
        body,
        name="reduce_small",
        in_specs=[VM] * 7 + [HBM] + [HBM] * npg,
        out_specs=[VM, VM, VM] + [HBM] * npg,
        out_shape=[jax.ShapeDtypeStruct((NVEC, D), F32), jax.ShapeDtypeStruct((NDEV, 8, D), F32),
                   jax.ShapeDtypeStruct(dws.shape, F32)]
                  + [jax.ShapeDtypeStruct((4,) + g.shape[1:], g.dtype) for g in pair_grads],
        scratch_shapes=[pltpu.VMEM((NVEC, D), F32), pltpu.VMEM((NDEV, NVEC, D), F32),
                        pltpu.VMEM((NDEV, CH, CH), F32), pltpu.VMEM((CH, CH), F32)] + AG_SEMS(1) + [dma7] * 4
                       + [pltpu.VMEM((4,) + g.shape[1:], g.dtype) for g in pair_grads]
                       + [pltpu.VMEM((3,) + g.shape[1:], g.dtype) for g in pair_grads] + [dma4] * 4,
        compiler_params=pltpu.CompilerParams(vmem_limit_bytes=VMEM_LIMIT),
    )(acc_f, acc_v, acc_b, acc_a, acc_bs, acc_cw, dws, after, *pair_grads)
    return outs[0], outs[1], outs[2], outs[3:]


HBM_ONLY = pl.BlockSpec(memory_space=pltpu.HBM)
SEM = pl.BlockSpec(memory_space=pltpu.SEMAPHORE)
EFFECT = pltpu.SideEffectType.DATAFLOW_SIDE_EFFECTING


def _rs_copies(g_refs, land_refs, sems, chips):
    x, y, c, idx = _place()
    if chips:
        routes = [(j, j, (*ch, c)) for j, ch in enumerate([(1 - x, y), (x, 1 - y), (1 - x, 1 - y)])]
    else:
        routes = [(idx ^ k, k - 1, _dev(idx ^ k)) for k in range(1, NDEV)]
    cps = []
    for src, dst, to in routes:
        for a in range(len(g_refs)):
            n = len(cps)
            cps.append(pltpu.make_async_remote_copy(
                src_ref=g_refs[a].at[src], dst_ref=land_refs[a].at[dst], send_sem=sems[2 * n],
                recv_sem=sems[2 * n + 1], device_id=to, device_id_type=MESH))
    return cps


def _rs_start(grads, name, after=(), chips=False):
    nw = len(grads)
    npeer = 3 if chips else NDEV - 1
    nsem = 2 * nw * npeer
    lands = [lax.empty((npeer,) + g.shape[1:], g.dtype) for g in grads]

    def body(*refs):
        g_refs, land_refs = refs[:nw], refs[nw:2 * nw]
        sems = refs[2 * nw + len(after):2 * nw + len(after) + nsem]
        token = refs[-1]
        for cp in _rs_copies(g_refs, land_refs, sems, chips):
            cp.start()
        token[...] = jnp.zeros_like(token)

    outs = pl.pallas_call(
        body, name=name,
        out_shape=(*[pltpu.SemaphoreType.DMA(())] * nsem,
                   *[pltpu.HBM(g.shape, g.dtype) for g in grads], *[pltpu.HBM(l.shape, l.dtype) for l in lands],
                   jax.ShapeDtypeStruct((8, CH), F32)),
        in_specs=[HBM_ONLY] * (2 * nw) + [HBM] * len(after),
        out_specs=(*[SEM] * nsem, *[HBM_ONLY] * (2 * nw), VM),
        input_output_aliases={i: nsem + i for i in range(2 * nw)},
        compiler_params=pltpu.CompilerParams(has_side_effects=EFFECT),
    )(*[pltpu.with_memory_space_constraint(g, pltpu.HBM) for g in grads],
      *[pltpu.with_memory_space_constraint(l, pltpu.HBM) for l in lands], *after)
    return outs[:nsem], outs[nsem:nsem + nw], outs[nsem + nw:nsem + 2 * nw], outs[-1]


def _rs_wait(sems, g_thru, land_thru, after, name, chips=False):
    nw = len(g_thru)
    nsem = len(sems)

    def body(*refs):
        g_refs, land_refs = refs[:nw], refs[nw:2 * nw]
        for cp in _rs_copies(g_refs, land_refs, refs[2 * nw:2 * nw + nsem], chips):
            cp.wait_send()
            cp.wait_recv()

    outs = pl.pallas_call(
        body, name=name,
        out_shape=tuple(pltpu.HBM(a.shape, a.dtype) for a in list(g_thru) + list(land_thru)),
        in_specs=[HBM_ONLY] * (2 * nw) + [SEM] * nsem + [HBM] * len(after),
        out_specs=tuple([HBM_ONLY] * (2 * nw)),
        input_output_aliases={i: i for i in range(2 * nw)},
        compiler_params=pltpu.CompilerParams(has_side_effects=EFFECT),
    )(*g_thru, *land_thru, *sems, *after)
    return outs[:nw], outs[nw:]


def _adamw(w, g, m, v):
    m2 = ADAM_B1 * m + (1.0 - ADAM_B1) * g
    v2 = ADAM_B2 * v + (1.0 - ADAM_B2) * (g * g)
    m_hat = m2 / (1.0 - ADAM_B1 ** ADAM_STEP)
    v_hat = v2 / (1.0 - ADAM_B2 ** ADAM_STEP)
    delta = -ADAM_LR * (m_hat / (jnp.sqrt(v_hat) + ADAM_EPS) + ADAM_WD * w)
    return delta, m2, v2


def _adam_big(r, w, m, v, rb, name, own, after=None, sib=None):
    R, C = w.shape
    ns = r.shape[0]
    g_all, idx1 = own

    def body(idx_ref, r_ref, own_ref, *refs):
        w_ref, m_ref, v_ref, g_ref, d_ref, m2_ref, v2_ref = refs[len(refs) - 7:]
        g = own_ref[0].astype(F32)
        if sib is not None:
            g = g + refs[0][0].astype(F32)
        for k in range(ns):
            g = g + r_ref[k].astype(F32)
        g_ref[...] = g
        d_ref[...], m2_ref[...], v2_ref[...] = _adamw(w_ref[...], g, m_ref[...], v_ref[...])

    t2 = pl.BlockSpec((rb, C), lambda i, idx_ref: (i, 0))
    sd = jax.ShapeDtypeStruct((R, C), F32)
    extra_specs = ([pl.BlockSpec((1, rb, C), lambda i, idx_ref: (3, i, 0))] if sib is not None else []) \
        + ([HBM] if after is not None else [])
    extra = ([sib] if sib is not None else []) + ([after] if after is not None else [])
    return pl.pallas_call(
        body, name=name,
        grid_spec=pltpu.PrefetchScalarGridSpec(
            num_scalar_prefetch=1, grid=(R // rb,),
            in_specs=[pl.BlockSpec((ns, rb, C), lambda i, idx_ref: (0, i, 0)),
                      pl.BlockSpec((1, rb, C), lambda i, idx_ref: (idx_ref[0], i, 0))] + extra_specs + [t2, t2, t2],
            out_specs=[t2, t2, t2, t2]),
        out_shape=[sd, sd, sd, sd],
        compiler_params=pltpu.CompilerParams(dimension_semantics=("arbitrary",), vmem_limit_bytes=VMEM_LIMIT),
    )(idx1, r, g_all, *extra, w, m, v)


def _adam_ada(c_all, dcs, w, m, v, rb, name):
    R, C = w.shape

    def body(c_ref, dc_ref, w_ref, m_ref, v_ref, g_ref, d_ref, m2_ref, v2_ref):
        cv = c_ref[...]
        g = lax.dot_general(cv * _sig(cv), dc_ref[...], (((0,), (0,)), ((), ())), preferred_element_type=F32,
                            precision=lax.Precision.HIGHEST)
        g_ref[...] = g
        d_ref[...], m2_ref[...], v2_ref[...] = _adamw(w_ref[...], g, m_ref[...], v_ref[...])

    t2 = pl.BlockSpec((rb, C), lambda i: (i, 0))
    sd = jax.ShapeDtypeStruct((R, C), F32)
    return pl.pallas_call(
        body, name=name, grid=(R // rb,),
        in_specs=[pl.BlockSpec((NDEV, rb), lambda i: (0, i)), _full((NDEV, C)), t2, t2, t2],
        out_specs=[t2, t2, t2, t2], out_shape=[sd, sd, sd, sd],
        compiler_params=pltpu.CompilerParams(dimension_semantics=("arbitrary",), vmem_limit_bytes=VMEM_LIMIT),
    )(c_all, dcs, w, m, v)


_SMALL = ["ada_b", "ada_f_b", "norm1_g", "b_in", "a_ln_g", "a_ln_b", "a_spatial_b", "b_conv_b", "b_gn_g", "b_gn_b",
          "out_norm_a_g", "out_norm_b_g", "norm2_g", "norm_f_g", "a_spatial_w", "b_conv_w"]


def _adam_small(vsum, wssum, gcw, params):
    names = _SMALL
    flat = []
    for n in names:
        flat += list(params[n])

    def body(vs_ref, ws_ref, gcw_ref, *rest):
        ins = rest[:3 * len(names)]
        outs = rest[3 * len(names):]
        for pi, n in enumerate(names):
            w_ref, m_ref, v_ref = ins[3 * pi:3 * pi + 3]
            g_ref, d_ref, m2_ref, v2_ref = outs[4 * pi:4 * pi + 4]
            if n in ("ada_b", "ada_f_b", "b_in"):
                row0 = {"ada_b": 0, "ada_f_b": 6, "b_in": 9}[n]
                pieces = [(vs_ref[row0 + r:row0 + r + 1, :], slice(r * D, (r + 1) * D))
                          for r in range(w_ref.shape[1] // D)]
            elif n == "a_spatial_w":
                pieces = [(ws_ref[...], slice(None))]
            elif n == "b_conv_w":
                pieces = [(gcw_ref[...], slice(None))]
            else:
                row, off, width = _VEC_AT[n]
                pieces = [(vs_ref[row:row + 1, off:off + width], slice(None))]
            for g, cs in pieces:
                g_ref[:, cs] = g
                d_ref[:, cs], m2_ref[:, cs], v2_ref[:, cs] = _adamw(w_ref[:, cs], g, m_ref[:, cs], v_ref[:, cs])

    out_shape = []
    for n in names:
        out_shape += [jax.ShapeDtypeStruct(params[n][0].shape, F32)] * 4
    outs = pl.pallas_call(
        body, name="adam_small",
        in_specs=[VM] * (3 + len(flat)), out_specs=[VM] * len(out_shape), out_shape=out_shape,
        compiler_params=pltpu.CompilerParams(vmem_limit_bytes=VMEM_LIMIT),
    )(vsum, wssum, gcw, *flat)
    return {n: outs[4 * pi:4 * pi + 4] for pi, n in enumerate(names)}


def _token_tile(T, want):
    return want if T % want == 0 else T


def kernel(x, c, ada_w, ada_b, norm1_g, w_in, b_in, a_ln_g, a_ln_b, a_spatial_w, a_spatial_b, b_conv_w, b_conv_b, b_gn_g, b_gn_b, out_norm_a_g, out_norm_b_g, w_out, norm2_g, w_ffn_in, w_ffn_out, ada_f_w, ada_f_b, norm_f_g, loss_target, m_ada_w, m_ada_b, m_norm1_g, m_w_in, m_b_in, m_a_ln_g, m_a_ln_b, m_a_spatial_w, m_a_spatial_b, m_b_conv_w, m_b_conv_b, m_b_gn_g, m_b_gn_b, m_out_norm_a_g, m_out_norm_b_g, m_w_out, m_norm2_g, m_w_ffn_in, m_w_ffn_out, m_ada_f_w, m_ada_f_b, m_norm_f_g, v_ada_w, v_ada_b, v_norm1_g, v_w_in, v_b_in, v_a_ln_g, v_a_ln_b, v_a_spatial_w, v_a_spatial_b, v_b_conv_w, v_b_conv_b, v_b_gn_g, v_b_gn_b, v_out_norm_a_g, v_out_norm_b_g, v_w_out, v_norm2_g, v_w_ffn_in, v_w_ffn_out, v_ada_f_w, v_ada_f_b, v_norm_f_g):
    T = x.shape[1]
    idx = 4 * lax.axis_index("x") + 2 * lax.axis_index("y") + lax.axis_index("c")
    x2d = x.reshape(T, D)
    tgt = loss_target.reshape(T, D)

    conv_s = jnp.pad(b_conv_w[0], ((0, HALO - KW), (0, 0)))
    call, cparts, cfparts, convg, (win_g, wout_g) = _gather(
        c, ada_w[0], ada_b.reshape(NDEV, -1), ada_f_w, ada_f_b.reshape(NDEV, -1), conv_s,
        [w_in[0], w_out[0]])
    wout = wout_g.reshape(D, D)
    mod = jnp.concatenate([cparts.reshape(6, D), cfparts.reshape(2, D)], axis=0)
    cw = jnp.transpose(convg, (1, 0, 2)).reshape(HALO, DB)

    tril = jnp.tril(jnp.ones((CH, CH), dtype=bool))
    wsm = jnp.where(tril[None], a_spatial_w[0], 0.0).astype(BF16)
    wcat = wsm.reshape(NH * CH, CH)
    wcat_t = jnp.transpose(wsm, (0, 2, 1)).reshape(NH * CH, CH)
    bsf = jnp.repeat(a_spatial_b[0].T, DA // NH, axis=1)
    pm = jnp.asarray(_GROUP_MEAN, BF16)
    esel = jnp.asarray(_HEAD_SELECT, BF16)

    tm = _token_tile(T, 256)
    tk = _token_tile(T, 2048)
    (x1, hb, zvg, mixed, yb, o, gu, dgelu_u, dgelu_v, vhat, rslb, yhat, rsg), (wfi_g, wfo_g) = _mix_fwd(
        x2d, mod, norm1_g, win_g, b_in, a_ln_g, a_ln_b, wcat, bsf, cw, b_conv_b, b_gn_g, b_gn_b, out_norm_a_g,
        out_norm_b_g, wout, pm, [w_ffn_in[0].T, w_ffn_out[0]], _token_tile(T, 512))
    dx1, h2b, dgu, act, dxg, acc_f = _ffn(x1, tgt, mod, norm2_g, norm_f_g.reshape(1, D),
                                          wfi_g.reshape(2 * DFF, D), wfo_g.reshape(DFF, D), tm)
    g_wfi = _wgrad_rows(dgu, h2b, 2 * WFI_B, tk, "wgrad_ffn_in").reshape(NDEV, WFI_B, D)
    g_wfo = _wgrad_rows(act, dxg, 2 * WFI_B, tk, "wgrad_ffn_out").reshape(NDEV, DFF // NDEV, D)
    f_sems, f_thru, f_land, f_token = _rs_start([g_wfi, g_wfo], "rs_ffn_start")
    (gx, acc_v, acc_b, acc_a, acc_bs, acc_ws, acc_cw), (g_win, g_wout) = _mix_bwd(
        dx1, x2d, zvg, mixed, o, hb, yb, gu, dgelu_u, dgelu_v, vhat, rslb, yhat, rsg, mod, norm1_g, win_g, a_ln_g,
        a_ln_b, wcat, wcat_t, cw, b_gn_g, b_gn_b, out_norm_a_g, out_norm_b_g, wout, pm, esel, f_token, tm)
    (g_wfi_d, g_wfo_d), (r_wfi, r_wfo) = _rs_wait(f_sems, f_thru, f_land, [acc_v], "rs_ffn_wait")
    g_wout = g_wout.reshape(NDEV, D // NDEV, D)

    vsum, dcond_all, wssum, (q_win, q_wout) = _reduce_small(acc_f, acc_v, acc_b, acc_a, acc_bs, acc_cw, acc_ws,
                                                            g_wfi_d, [g_win, g_wout])
    sems, g_thru, land_thru, token = _rs_start([q_win, q_wout], "rs_mix_start", after=(vsum,), chips=True)

    own = lambda g: (g, jnp.reshape(idx, (1,)).astype(jnp.int32))
    res = {}
    upd_wfi_t = _adam_big(r_wfi, w_ffn_in[0].T, m_w_ffn_in[0].T, v_w_ffn_in[0].T, WFI_B // 2, "adam_w_ffn_in",
                          own=own(g_wfi_d), after=token)
    res["w_ffn_in"] = tuple(a.T for a in upd_wfi_t)
    res["w_ffn_out"] = _adam_big(r_wfo, w_ffn_out[0], m_w_ffn_out[0], v_w_ffn_out[0], DFF // NDEV // 2,
                                 "adam_w_ffn_out", own=own(g_wfo_d), after=token)
    dcond = dcond_all.reshape(NDEV, 8 * D)
    nada = ada_w.shape[2]
    nadf = ada_f_w.shape[1]
    dcs = lax.dynamic_slice(dcond, (0, idx * nada), (NDEV, nada))
    dcfs = lax.dynamic_slice(dcond, (0, 6 * D + idx * nadf), (NDEV, nadf))
    res["ada_w"] = _adam_ada(call, dcs, ada_w[0], m_ada_w[0], v_ada_w[0], 512, "adam_ada_w")
    res["ada_f_w"] = _adam_ada(call, dcfs, ada_f_w, m_ada_f_w, v_ada_f_w, 512, "adam_ada_f_w")
    ncw = b_conv_w.shape[2]
    gcw = jnp.concatenate([lax.dynamic_slice(vsum, (_CW_ROW, idx * ncw), (HALO // 2, ncw)),
                           lax.dynamic_slice(vsum, (_CW_ROW, DB + idx * ncw), (HALO // 2, ncw))], axis=0)[:KW]
    two = lambda a: a.reshape(1, -1) if a.ndim == 1 else a.reshape(-1, a.shape[-1])
    small_in = {
        "ada_b": (ada_b, m_ada_b, v_ada_b), "ada_f_b": (ada_f_b, m_ada_f_b, v_ada_f_b),
        "norm1_g": (norm1_g, m_norm1_g, v_norm1_g), "b_in": (b_in, m_b_in, v_b_in),
        "a_ln_g": (a_ln_g, m_a_ln_g, v_a_ln_g), "a_ln_b": (a_ln_b, m_a_ln_b, v_a_ln_b),
        "a_spatial_b": (a_spatial_b.reshape(1, D), m_a_spatial_b.reshape(1, D), v_a_spatial_b.reshape(1, D)),
        "b_conv_b": (b_conv_b, m_b_conv_b, v_b_conv_b), "b_gn_g": (b_gn_g, m_b_gn_g, v_b_gn_g),
        "b_gn_b": (b_gn_b, m_b_gn_b, v_b_gn_b), "out_norm_a_g": (out_norm_a_g, m_out_norm_a_g, v_out_norm_a_g),
        "out_norm_b_g": (out_norm_b_g, m_out_norm_b_g, v_out_norm_b_g),
        "norm2_g": (norm2_g, m_norm2_g, v_norm2_g), "norm_f_g": (norm_f_g, m_norm_f_g, v_norm_f_g),
        "a_spatial_w": (a_spatial_w, m_a_spatial_w, v_a_spatial_w),
        "b_conv_w": (b_conv_w[0], m_b_conv_w[0], v_b_conv_w[0]),
    }
    small_in = {n: tuple(two(a) for a in t) for n, t in small_in.items()}
    res.update(_adam_small(vsum, wssum, gcw, small_in))
    (q_win_d, q_wout_d), (r_win, r_wout) = _rs_wait(
        sems, g_thru, land_thru,
        [upd_wfi_t[0], res["w_ffn_out"][0], res["ada_w"][0], res["ada_f_w"][0], res["norm_f_g"][0]],
        "rs_mix_wait", chips=True)
    res["w_in"] = _adam_big(r_win, w_in[0], m_w_in[0], v_w_in[0], 512, "adam_w_in", own=own(g_win), sib=q_win_d)
    res["w_out"] = _adam_big(r_wout, w_out[0], m_w_out[0], v_w_out[0], D // NDEV, "adam_w_out", own=own(g_wout),
                             sib=q_wout_d)

    loss = 0.5 / D * jnp.sum(vsum[_LOSS_ROW])
    shapes = {"ada_w": ada_w, "ada_b": ada_b, "norm1_g": norm1_g, "w_in": w_in, "b_in": b_in, "a_ln_g": a_ln_g,
              "a_ln_b": a_ln_b, "a_spatial_w": a_spatial_w, "a_spatial_b": a_spatial_b, "b_conv_w": b_conv_w,
              "b_conv_b": b_conv_b, "b_gn_g": b_gn_g, "b_gn_b": b_gn_b, "out_norm_a_g": out_norm_a_g,
              "out_norm_b_g": out_norm_b_g, "w_out": w_out, "norm2_g": norm2_g, "w_ffn_in": w_ffn_in,
              "w_ffn_out": w_ffn_out, "ada_f_w": ada_f_w, "ada_f_b": ada_f_b, "norm_f_g": norm_f_g}
    order = list(shapes)
    outs = [loss, gx.reshape(x.shape)]
    for which in range(4):
        outs += [res[n][which].reshape(shapes[n].shape) for n in order]
    return tuple(outs)
```

```python
import math

import numpy as np

import jax
import jax.numpy as jnp
from jax import lax
from jax.experimental import pallas as pl
from jax.experimental.pallas import tpu as pltpu

F32 = jnp.float32
BF16 = jnp.bfloat16

D = 1024
DA = 512
DB = 512
DIN = 2048
DFF = 2816
NH = 8
CH = 128
KW = 31
HALO = 32
NDEV = 8
WIN_B = DIN // NDEV
WFI_B = 2 * DFF // NDEV
EPS = 1e-6
NVEC = 40
VMEM_LIMIT = 56 * 1024 * 1024

ADAM_LR, ADAM_B1, ADAM_B2, ADAM_EPS, ADAM_WD, ADAM_STEP = 0.001, 0.9, 0.999, 1e-08, 0.01, 10

MESH = pl.DeviceIdType.MESH

_LANE = np.arange(DB)
_GROUP_MEAN = np.where((_LANE[:, None] >> 6) == (_LANE[None, :] >> 6), 1.0 / 64.0, 0.0).astype(np.float32)
_HEAD_SELECT = np.where((_LANE[:, None] >> 6) == np.arange(CH)[None, :], 1.0, 0.0).astype(np.float32)


def _dot(a, b):
    return jnp.dot(a, b, preferred_element_type=F32)


def _dot_nt(a, b):
    return lax.dot_general(a, b, (((1,), (1,)), ((), ())), preferred_element_type=F32)


def _dot_tn(a, b):
    return lax.dot_general(a, b, (((0,), (0,)), ((), ())), preferred_element_type=F32)


def _rs(v):
    return lax.rsqrt(jnp.mean(v * v, axis=-1, keepdims=True) + EPS)


def _sig(v):
    return 1.0 / (1.0 + jnp.exp(-v))


_INV_SQRT2 = 1.0 / math.sqrt(2.0)
_INV_SQRT2PI = 1.0 / math.sqrt(2.0 * math.pi)


def _gelu_parts(v):
    cdf = 0.5 * (1.0 + lax.erf(v * _INV_SQRT2))
    pdf = jnp.exp(-0.5 * v * v) * _INV_SQRT2PI
    return v * cdf, cdf + v * pdf


def _grp_mean(v, pm):
    hi = v.astype(BF16)
    lo = (v - hi.astype(F32)).astype(BF16)
    return _dot(hi, pm) + _dot(lo, pm)


def _colsum(v):
    return jnp.sum(v, axis=0, keepdims=True)


def _full(shape):
    nd = len(shape)
    return pl.BlockSpec(shape, lambda *_: (0,) * nd)


def _resident(shape):
    nd = len(shape)
    return pl.BlockSpec(shape, lambda *_: (0,) * nd, pipeline_mode=pl.Buffered(1))


HBM = pl.BlockSpec(memory_space=pl.ANY)
VM = pl.BlockSpec(memory_space=pltpu.VMEM)


SH_ROWS = HALO - 8


def _shifted_copies(buf, shbuf, tm):
    for b in range(1, 8):
        shbuf[b - 1] = buf[b:b + tm + SH_ROWS, :]


def _window(buf, shbuf, off, tm):
    a, b = divmod(off, 8)
    if b == 0:
        return buf[8 * a:8 * a + tm, :]
    return shbuf[b - 1, 8 * a:8 * a + tm, :]


def _first_head_lanes():
    return lax.broadcasted_iota(jnp.int32, (CH, CH), 1) < (DA // NH)


def _mix_heads(w_ref, vb, first):
    outs = []
    for p in range(NH // 2):
        v = vb[:, p * CH:(p + 1) * CH]
        a = _dot(w_ref[(2 * p) * CH:(2 * p + 1) * CH, :], v)
        b = _dot(w_ref[(2 * p + 1) * CH:(2 * p + 2) * CH, :], v)
        outs.append(jnp.where(first, a, b))
    return jnp.concatenate(outs, axis=1)


def _place():
    x, y, c = lax.axis_index("x"), lax.axis_index("y"), lax.axis_index("c")
    return x, y, c, 4 * x + 2 * y + c


def _dev(t):
    return (t >> 2, (t >> 1) & 1, t & 1)


class _AllGather:
    def __init__(self, w_in, w_out, wss, wrs, lsem):
        x, y, c, idx = _place()
        me, sibling = (x, y, c), (x, y, 1 - c)
        chips = [(1 - x, y), (x, 1 - y), (1 - x, 1 - y)]
        nw = len(w_in)

        def blk(p):
            return 4 * p[0] + 2 * p[1] + p[2]

        def wcopy(a, k, block, to, src=None):
            dst = w_out[a].at[blk(block)]
            return pltpu.make_async_remote_copy(src_ref=dst if src is None else src, dst_ref=dst,
                                                send_sem=wss.at[a, k], recv_sem=wrs.at[a, k],
                                                device_id=to, device_id_type=MESH)

        self.mine = [pltpu.make_async_copy(w_in[a], w_out[a].at[idx], lsem.at[a]) for a in range(nw)]
        self.first = []
        for a in range(nw):
            self.first.append(wcopy(a, 0, me, sibling, src=w_in[a]))
            self.first += [wcopy(a, 1 + j, me, (*chip, c), src=w_in[a]) for j, chip in enumerate(chips)]
        self.landed = [[wcopy(a, 1 + j, (*chip, c), me) for a in range(nw)] for j, chip in enumerate(chips)]
        self.passed = [[wcopy(a, 4 + j, (*chip, c), sibling) for a in range(nw)] for j, chip in enumerate(chips)]
        self.from_sibling = []
        for a in range(nw):
            self.from_sibling.append(wcopy(a, 0, sibling, me))
            self.from_sibling += [wcopy(a, 4 + j, (*chip, 1 - c), me) for j, chip in enumerate(chips)]

    def start(self):
        for cp in self.mine + self.first:
            cp.start()

    def forward(self):
        for land, pas in zip(self.landed, self.passed):
            for l, p in zip(land, pas):
                l.wait_recv()
                p.start()

    def finish(self):
        for cp in self.from_sibling:
            cp.wait_recv()
        for cp in self.first:
            cp.wait_send()
        for pas in self.passed:
            for p in pas:
                p.wait_send()
        for cp in self.mine:
            cp.wait()


AG_SEMS = lambda nw: [pltpu.SemaphoreType.DMA((nw, 7)), pltpu.SemaphoreType.DMA((nw, 7)),
                      pltpu.SemaphoreType.DMA((nw,))]


def _mix_fwd(x, mod, g1, win, b_in, lng, lnb, wcat, bsf, cw, cb, gng, gnb, oga, ogb, wout, pm, ffn_shards, tm):
    T = x.shape[0]
    nt = T // tm
    nch = tm // CH
    nw = len(ffn_shards)
    fwd_step = (5 * nt) // 8
    saved = [(D, F32), (D, BF16), (2 * DB, F32), (DA, F32), (D, BF16), (D, F32), (DA, F32), (DA, F32), (DA, F32),
             (DA, F32), (CH, F32), (DB, F32), (DB, F32)]
    NSAVE = len(saved)

    def body(x_ref, mod_ref, g1_ref, win_ref, bin_ref, lng_ref, lnb_ref, wcat_ref, bsf_ref, cw_ref, cb_ref,
             gng_ref, gnb_ref, oga_ref, ogb_ref, wout_ref, pm_ref, *rest):
        sh_f32 = rest[:nw]
        (x1_ref, h_ref, zvg_ref, mixed_ref, y_ref, o_ref, gu_ref, dgu_ref, dgv_ref, vhat_ref, rsl_ref, yhat_ref,
         rsg_ref) = rest[nw:nw + NSAVE]
        sh_out = rest[nw + NSAVE:2 * nw + NSAVE]
        glbuf, shbuf = rest[2 * nw + NSAVE:2 * nw + NSAVE + 2]
        sh_in = rest[2 * nw + NSAVE + 2:3 * nw + NSAVE + 2]
        wss, wrs, lsem = rest[3 * nw + NSAVE + 2:]
        i = pl.program_id(0)

        @pl.when(i == 0)
        def _():
            for a in range(nw):
                sh_in[a][...] = sh_f32[a][...].astype(BF16)
            _AllGather(sh_in, sh_out, wss, wrs, lsem).start()

        xv = x_ref[...]
        shift1 = mod_ref[0:1, :]
        scale1 = mod_ref[1:2, :]
        gate1 = mod_ref[2:3, :]
        h = (xv * _rs(xv) * g1_ref[...]) * (1.0 + scale1) + shift1
        hb = h.astype(BF16)
        h_ref[...] = hb
        z = jnp.concatenate([_dot(hb, win_ref[j]) for j in range(NDEV)], axis=1) + bin_ref[...]
        zvg_ref[...] = z[:, 2 * DA:]
        gu, dgelu_u = _gelu_parts(z[:, 0:DA])
        gv, dgelu_v = _gelu_parts(z[:, DA:2 * DA])
        gu_ref[...] = gu
        dgu_ref[...] = dgelu_u
        dgv_ref[...] = dgelu_v
        xc = gv - jnp.mean(gv, axis=-1, keepdims=True)
        rsl = lax.rsqrt(jnp.mean(xc * xc, axis=-1, keepdims=True) + EPS)
        vhat = xc * rsl
        vhat_ref[...] = vhat
        rsl_ref[...] = jnp.broadcast_to(rsl, (tm, CH))
        vnb = (vhat * lng_ref[...] + lnb_ref[...]).astype(BF16)
        first = _first_head_lanes()
        chunks = []
        for ci in range(nch):
            chunks.append(_mix_heads(wcat_ref, vnb[ci * CH:(ci + 1) * CH, :], first) + bsf_ref[...])
        mixed = jnp.concatenate(chunks, axis=0) if nch > 1 else chunks[0]
        mixed_ref[...] = mixed
        ya = gu * mixed
        gl = z[:, 2 * DA:2 * DA + DB] * _sig(z[:, 2 * DA + DB:])

        @pl.when(i == 0)
        def _():
            glbuf[0:HALO, :] = jnp.zeros((HALO, DB), F32)

        glbuf[HALO:HALO + tm, :] = gl
        _shifted_copies(glbuf, shbuf, tm)
        yc = jnp.zeros((tm, DB), F32) + cb_ref[...]
        for k in range(KW):
            yc = yc + cw_ref[k:k + 1, :] * _window(glbuf, shbuf, HALO - (KW - 1) + k, tm)
        glbuf[0:HALO, :] = gl[tm - HALO:, :]
        pmv = pm_ref[...]
        dc = yc - _grp_mean(yc, pmv)
        rsg = lax.rsqrt(_grp_mean(dc * dc, pmv) + EPS)
        yhat = dc * rsg
        yhat_ref[...] = yhat
        rsg_ref[...] = rsg
        yg = yhat * gng_ref[...] + gnb_ref[...]
        yb = yg * _sig(yg)
        na = ya * _rs(ya) * oga_ref[...]
        nb = yb * _rs(yb) * ogb_ref[...]
        yv = jnp.concatenate([na, nb], axis=1).astype(BF16)
        y_ref[...] = yv
        o = _dot(yv, wout_ref[...])
        o_ref[...] = o
        x1_ref[...] = xv + gate1 * o

        @pl.when(i == fwd_step)
        def _():
            _AllGather(sh_in, sh_out, wss, wrs, lsem).forward()

        @pl.when(i == nt - 1)
        def _():
            _AllGather(sh_in, sh_out, wss, wrs, lsem).finish()

    tile = lambda w: pl.BlockSpec((tm, w), lambda i: (i, 0))
    outs = pl.pallas_call(
        body,
        name="mix_fwd",
        grid=(nt,),
        in_specs=[tile(D), _full((8, D)), _full((1, D)), _resident((NDEV, D, WIN_B)), _full((1, DIN)),
                  _full((1, DA)), _full((1, DA)), _full((NH * CH, CH)), _full((CH, DA)), _full((HALO, DB)),
                  _full((1, DB)), _full((1, DB)), _full((1, DB)), _full((1, DA)), _full((1, DB)),
                  _resident((D, D)), _full((DB, DB))] + [_resident(s.shape) for s in ffn_shards],
        out_specs=[tile(w) for w, _ in saved] + [HBM] * nw,
        out_shape=[jax.ShapeDtypeStruct((T, w), dt) for w, dt in saved]
                  + [jax.ShapeDtypeStruct((NDEV,) + s.shape, BF16) for s in ffn_shards],
        scratch_shapes=[pltpu.VMEM((HALO + tm, DB), F32), pltpu.VMEM((7, tm + SH_ROWS, DB), F32)]
                       + [pltpu.VMEM(s.shape, BF16) for s in ffn_shards] + AG_SEMS(nw),
        compiler_params=pltpu.CompilerParams(dimension_semantics=("arbitrary",), vmem_limit_bytes=VMEM_LIMIT),
    )(x, mod, g1, win, b_in, lng, lnb, wcat, bsf, cw, cb, gng, gnb, oga, ogb, wout, pm, *ffn_shards)
    return outs[:NSAVE], outs[NSAVE:]


FF_BLOCKS = ((0, 1024), (1024, 1024), (2048, 768))


def _ffn(x1, tgt, mod, g2, gf, wfi_t, wfo, tm):
    T = x1.shape[0]
    nt = T // tm

    def body(x1_ref, tgt_ref, mod_ref, g2_ref, gf_ref, wfi_ref, wfo_ref,
             dx1_ref, h2_ref, dgu_ref, act_ref, dxg_ref, acc_ref, g_s, u_s):
        i = pl.program_id(0)

        @pl.when(i == 0)
        def _():
            acc_ref[...] = jnp.zeros((8, D), F32)

        x1 = x1_ref[...]
        shift2 = mod_ref[3:4, :]
        scale2 = mod_ref[4:5, :]
        gate2 = mod_ref[5:6, :]
        shiftf = mod_ref[6:7, :]
        scalef = mod_ref[7:8, :]
        g2v = g2_ref[...]
        gfv = gf_ref[...]
        r2 = _rs(x1)
        xn2 = x1 * r2
        h2b = (xn2 * g2v * (1.0 + scale2) + shift2).astype(BF16)
        h2_ref[...] = h2b
        f = jnp.zeros((tm, D), F32)
        for o, w in FF_BLOCKS:
            g = _dot_nt(h2b, wfi_ref[o:o + w, :])
            u = _dot_nt(h2b, wfi_ref[DFF + o:DFF + o + w, :])
            g_s[:, o:o + w] = g
            u_s[:, o:o + w] = u
            actb = (g * _sig(g) * u).astype(BF16)
            act_ref[:, o:o + w] = actb
            f = f + _dot(actb, wfo_ref[o:o + w, :])
        x2 = x1 + gate2 * f
        rf = _rs(x2)
        xnf = x2 * rf
        out = xnf * gfv * (1.0 + scalef) + shiftf
        e = out - tgt_ref[...]
        dout = e * (1.0 / D)
        acc_ref[7:8, :] += _colsum(e * e)
        acc_ref[0:1, :] += _colsum(dout)
        acc_ref[1:2, :] += _colsum(dout * xnf * gfv)
        acc_ref[2:3, :] += _colsum(dout * (1.0 + scalef) * xnf)
        dxnf = dout * (1.0 + scalef) * gfv
        dx2 = rf * (dxnf - xnf * jnp.mean(dxnf * xnf, axis=-1, keepdims=True))
        acc_ref[3:4, :] += _colsum(dx2 * f)
        dxgb = (dx2 * gate2).astype(BF16)
        dxg_ref[...] = dxgb
        dh2 = jnp.zeros((tm, D), F32)
        for o, w in FF_BLOCKS:
            dact = _dot_nt(dxgb, wfo_ref[o:o + w, :])
            g = g_s[:, o:o + w]
            u = u_s[:, o:o + w]
            s = _sig(g)
            dgb = (dact * u * (s * (1.0 + g * (1.0 - s)))).astype(BF16)
            dub = (dact * (g * s)).astype(BF16)
            dgu_ref[:, o:o + w] = dgb
            dgu_ref[:, DFF + o:DFF + o + w] = dub
            dh2 = dh2 + _dot(dgb, wfi_ref[o:o + w, :])
            dh2 = dh2 + _dot(dub, wfi_ref[DFF + o:DFF + o + w, :])
        acc_ref[4:5, :] += _colsum(dh2)
        acc_ref[5:6, :] += _colsum(dh2 * xn2 * g2v)
        acc_ref[6:7, :] += _colsum(dh2 * (1.0 + scale2) * xn2)
        dxn2 = dh2 * (1.0 + scale2) * g2v
        dx1_ref[...] = dx2 + r2 * (dxn2 - xn2 * jnp.mean(dxn2 * xn2, axis=-1, keepdims=True))

    tile = lambda w: pl.BlockSpec((tm, w), lambda i: (i, 0))
    return pl.pallas_call(
        body,
        name="ffn_fwd_bwd",
        grid=(nt,),
        in_specs=[tile(D), tile(D), _full((8, D)), _full((1, D)), _full((1, D)),
                  _resident((2 * DFF, D)), _resident((DFF, D))],
        out_specs=[tile(D), tile(D), tile(2 * DFF), tile(DFF), tile(D), _full((8, D))],
        out_shape=[jax.ShapeDtypeStruct((T, D), F32), jax.ShapeDtypeStruct((T, D), BF16),
                   jax.ShapeDtypeStruct((T, 2 * DFF), BF16), jax.ShapeDtypeStruct((T, DFF), BF16),
                   jax.ShapeDtypeStruct((T, D), BF16), jax.ShapeDtypeStruct((8, D), F32)],
        scratch_shapes=[pltpu.VMEM((tm, DFF), F32), pltpu.VMEM((tm, DFF), F32)],
        compiler_params=pltpu.CompilerParams(dimension_semantics=("arbitrary",), vmem_limit_bytes=VMEM_LIMIT),
    )(x1, tgt, mod, g2, gf, wfi_t, wfo)


def _mix_bwd(dx1, x, zvg, mixed, o, hb, yb, gu, dgu, dgv, vhat, rslb, yhat, rsg, mod, g1, win, lng, lnb, wcat, wcat_t,
             cw, gng, gnb, oga, ogb, wout, pm, esel, after, tm):
    T = x.shape[0]
    nt = T // tm
    nch = tm // CH
    WOB = 256

    def body(dx1_ref, x_ref, zvg_ref, mixed_ref, o_ref, hb_ref, yb_ref, gu_ref, dgu_ref, dgv_ref, vhat_ref, rsl_ref,
             yhat_ref, rsg_ref, mod_ref, g1_ref, win_ref, lng_ref, lnb_ref, wcat_ref, wcatt_ref, cw_ref, gng_ref,
             gnb_ref, oga_ref, ogb_ref, wout_ref, pm_ref, esel_ref, after_ref,
             gx_ref, accv_ref, accb_ref, acca_ref, accbs_ref, accws_ref, acccw_ref, gwin_ref, gwout_ref,
             dycbuf, shbuf, bs_s, acc_win, acc_wout, st_win, st_wout):
        i = pl.program_id(0)

        @pl.when(i == 0)
        def _():
            acc_win[...] = jnp.zeros((NDEV, D, WIN_B), F32)
            acc_wout[...] = jnp.zeros((D, D), F32)
            accv_ref[...] = jnp.zeros((8, D), F32)
            accb_ref[...] = jnp.zeros((1, DIN), F32)
            acca_ref[...] = jnp.zeros((8, DA), F32)
            accws_ref[...] = jnp.zeros((NH * CH, CH), F32)
            acccw_ref[...] = jnp.zeros((HALO, DB), F32)
            bs_s[...] = jnp.zeros((CH, DA), F32)
            dycbuf[tm:tm + HALO, :] = jnp.zeros((HALO, DB), F32)

        shift1 = mod_ref[0:1, :]
        scale1 = mod_ref[1:2, :]
        gate1 = mod_ref[2:3, :]
        g1v = g1_ref[...]
        xv = x_ref[...]
        r1 = _rs(xv)
        xn1 = xv * r1
        val = zvg_ref[:, 0:DB]
        gate = zvg_ref[:, DB:]
        gu = gu_ref[...]
        dgelu_u = dgu_ref[...]
        dgelu_v = dgv_ref[...]
        vhat = vhat_ref[...]
        rsl = rsl_ref[:, 0:1]
        lngv = lng_ref[...]
        vnb = (vhat * lngv + lnb_ref[...]).astype(BF16)
        mixed = mixed_ref[...]
        ya = gu * mixed
        ra = _rs(ya)
        yan = ya * ra
        sgt = _sig(gate)
        gl = val * sgt
        pmv = pm_ref[...]
        rsg = rsg_ref[...]
        yhat = yhat_ref[...]
        gngv = gng_ref[...]
        yg = yhat * gngv + gnb_ref[...]
        sgy = _sig(yg)
        yb = yg * sgy
        rb = _rs(yb)
        ybn = yb * rb
        dx1 = dx1_ref[...]
        accv_ref[0:1, :] += _colsum(dx1 * o_ref[...])
        dogb = (dx1 * gate1).astype(BF16)
        acc_wout[...] += _dot_tn(yb_ref[...], dogb)
        dy = _dot_nt(dogb, wout_ref[...])
        dna = dy[:, 0:DA]
        dnb = dy[:, DA:]
        ogav = oga_ref[...]
        ogbv = ogb_ref[...]
        acca_ref[2:3, :] += _colsum(dna * yan)
        acca_ref[3:4, :] += _colsum(dnb * ybn)
        ta = dna * ogav
        dya = ra * (ta - yan * jnp.mean(ta * yan, axis=-1, keepdims=True))
        tb = dnb * ogbv
        dyb = rb * (tb - ybn * jnp.mean(tb * ybn, axis=-1, keepdims=True))
        dgu = dya * mixed
        dm = dya * gu
        first = _first_head_lanes()
        zero = jnp.zeros((CH, CH), BF16)
        dvn_chunks = []
        bs_acc = bs_s[...]
        for ci in range(nch):
            dmc = dm[ci * CH:(ci + 1) * CH, :]
            bs_acc = bs_acc + dmc
            dmcb = dmc.astype(BF16)
            dvn_chunks.append(_mix_heads(wcatt_ref, dmcb, first))
            vc = vnb[ci * CH:(ci + 1) * CH, :]
            for p in range(NH // 2):
                xt = dmcb[:, p * CH:(p + 1) * CH]
                vt = vc[:, p * CH:(p + 1) * CH]
                accws_ref[(2 * p) * CH:(2 * p + 1) * CH, :] += _dot_nt(jnp.where(first, xt, zero), vt)
                accws_ref[(2 * p + 1) * CH:(2 * p + 2) * CH, :] += _dot_nt(jnp.where(first, zero, xt), vt)
        bs_s[...] = bs_acc
        dvn = jnp.concatenate(dvn_chunks, axis=0) if nch > 1 else dvn_chunks[0]
        acca_ref[0:1, :] += _colsum(dvn * vhat)
        acca_ref[1:2, :] += _colsum(dvn)
        dvh = dvn * lngv
        dgv = rsl * (dvh - jnp.mean(dvh, axis=-1, keepdims=True)
                     - vhat * jnp.mean(dvh * vhat, axis=-1, keepdims=True))
        du = dgu * dgelu_u
        dv = dgv * dgelu_v
        dyg = dyb * (sgy * (1.0 + yg * (1.0 - sgy)))
        acca_ref[5:6, :] += _colsum(dyg * yhat)
        acca_ref[6:7, :] += _colsum(dyg)
        dyh = dyg * gngv
        dyc = rsg * (dyh - _grp_mean(dyh, pmv) - yhat * _grp_mean(dyh * yhat, pmv))
        acca_ref[4:5, :] += _colsum(dyc)
        dycbuf[0:tm, :] = dyc
        _shifted_copies(dycbuf, shbuf, tm)
        dgl = jnp.zeros((tm, DB), F32)
        for k in range(KW):
            win_k = _window(dycbuf, shbuf, KW - 1 - k, tm)
            dgl = dgl + cw_ref[k:k + 1, :] * win_k
            acccw_ref[k:k + 1, :] += _colsum(win_k * gl)
        dycbuf[tm:tm + HALO, :] = dyc[0:HALO, :]
        dval = dgl * sgt
        dgate = dgl * val * sgt * (1.0 - sgt)
        dz = jnp.concatenate([du, dv, dval, dgate], axis=1)
        accb_ref[...] += _colsum(dz)
        dzb = dz.astype(BF16)
        hbv = hb_ref[...]
        dh = jnp.zeros((tm, D), F32)
        for j in range(NDEV):
            dzj = dzb[:, j * WIN_B:(j + 1) * WIN_B]
            acc_win[j] += _dot_tn(hbv, dzj)
            dh = dh + _dot_nt(dzj, win_ref[j])
        accv_ref[1:2, :] += _colsum(dh)
        dh_xn = _colsum(dh * xn1)
        accv_ref[2:3, :] += dh_xn * g1v
        accv_ref[3:4, :] += dh_xn * (1.0 + scale1)
        dxn1 = dh * (1.0 + scale1) * g1v
        gx_ref[...] = dx1 + r1 * (dxn1 - xn1 * jnp.mean(dxn1 * xn1, axis=-1, keepdims=True))

        @pl.when(i == nt - 1)
        def _():
            rows = lax.broadcasted_iota(jnp.int32, (NH * CH, CH), 0) & (CH - 1)
            cols = lax.broadcasted_iota(jnp.int32, (NH * CH, CH), 1)
            accws_ref[...] = jnp.where(cols <= rows, accws_ref[...], 0.0)
            bs = bs_s[...]
            hi = bs.astype(BF16)
            r1_ = bs - hi.astype(F32)
            mid = r1_.astype(BF16)
            lo = (r1_ - mid.astype(F32)).astype(BF16)
            ev = esel_ref[...]
            accbs_ref[...] = _dot(hi, ev) + _dot(mid, ev) + _dot(lo, ev)
            for j in range(NDEV):
                st_win[...] = acc_win[j].astype(BF16)
                pltpu.sync_copy(st_win, gwin_ref.at[j])
            for j in range(D // WOB):
                st_wout[...] = acc_wout[j * WOB:(j + 1) * WOB, :].astype(BF16)
                pltpu.sync_copy(st_wout, gwout_ref.at[pl.ds(j * WOB, WOB)])

    rev = lambda w: pl.BlockSpec((tm, w), lambda i: (nt - 1 - i, 0))
    outs = pl.pallas_call(
        body,
        name="mix_bwd",
        grid=(nt,),
        in_specs=[rev(D), rev(D), rev(2 * DB), rev(DA), rev(D), rev(D), rev(D), rev(DA), rev(DA), rev(DA), rev(DA),
                  rev(CH), rev(DB), rev(DB), _full((8, D)), _full((1, D)),
                  _resident((NDEV, D, WIN_B)), _full((1, DA)), _full((1, DA)), _full((NH * CH, CH)),
                  _full((NH * CH, CH)), _full((HALO, DB)), _full((1, DB)), _full((1, DB)), _full((1, DA)),
                  _full((1, DB)), _resident((D, D)), _full((DB, DB)), _full((DA, CH)), HBM],
        out_specs=[rev(D), _full((8, D)), _full((1, DIN)), _full((8, DA)), _full((CH, CH)),
                   _full((NH * CH, CH)), _full((HALO, DB)), HBM, HBM],
        out_shape=[jax.ShapeDtypeStruct((T, D), F32), jax.ShapeDtypeStruct((8, D), F32),
                   jax.ShapeDtypeStruct((1, DIN), F32), jax.ShapeDtypeStruct((8, DA), F32),
                   jax.ShapeDtypeStruct((CH, CH), F32), jax.ShapeDtypeStruct((NH * CH, CH), F32),
                   jax.ShapeDtypeStruct((HALO, DB), F32),
                   jax.ShapeDtypeStruct((NDEV, D, WIN_B), BF16), jax.ShapeDtypeStruct((D, D), BF16)],
        scratch_shapes=[pltpu.VMEM((tm + HALO, DB), F32), pltpu.VMEM((7, tm + SH_ROWS, DB), F32),
                        pltpu.VMEM((CH, DA), F32), pltpu.VMEM((NDEV, D, WIN_B), F32), pltpu.VMEM((D, D), F32),
                        pltpu.VMEM((D, WIN_B), BF16), pltpu.VMEM((WOB, D), BF16)],
        compiler_params=pltpu.CompilerParams(dimension_semantics=("arbitrary",), vmem_limit_bytes=VMEM_LIMIT),
    )(dx1, x, zvg, mixed, o, hb, yb, gu, dgu, dgv, vhat, rslb, yhat, rsg, mod, g1, win, lng, lnb, wcat, wcat_t, cw,
      gng, gnb, oga, ogb, wout, pm, esel, after)
    return outs[:7], outs[7:]


def _wgrad_rows(a, b, bm, tk, name):
    T, M = a.shape
    N = b.shape[1]
    nk = T // tk

    def body(a_ref, b_ref, o_ref, acc):
        k = pl.program_id(1)

        @pl.when(k == 0)
        def _():
            acc[...] = jnp.zeros((bm, N), F32)

        acc[...] += _dot_tn(a_ref[...], b_ref[...])

        @pl.when(k == nk - 1)
        def _():
            o_ref[...] = acc[...].astype(BF16)

    return pl.pallas_call(
        body, name=name, grid=(M // bm, nk),
        in_specs=[pl.BlockSpec((tk, bm), lambda j, k: (k, j)), pl.BlockSpec((tk, N), lambda j, k: (k, 0))],
        out_specs=pl.BlockSpec((bm, N), lambda j, k: (j, 0)),
        out_shape=jax.ShapeDtypeStruct((M, N), BF16),
        scratch_shapes=[pltpu.VMEM((bm, N), F32)],
        compiler_params=pltpu.CompilerParams(dimension_semantics=("arbitrary", "arbitrary"),
                                             vmem_limit_bytes=VMEM_LIMIT),
    )(a, b)


def _small_copy(src, dst, ss, rs, k, to):
    return pltpu.make_async_remote_copy(src_ref=src, dst_ref=dst, send_sem=ss.at[k], recv_sem=rs.at[k],
                                        device_id=to, device_id_type=MESH)


def _gather(c_row, ada_w, ada_b8, ada_f_w, ada_f_b8, conv_s, shards):
    nw = len(shards)

    def body(c_ref, adaw_ref, adab_ref, adafw_ref, adafb_ref, conv_ref, *rest):
        w_f32 = rest[:nw]
        call_ref, cparts_ref, cfparts_ref, convg_ref = rest[nw:nw + 4]
        w_out = rest[nw + 4:2 * nw + 4]
        part_s, partf_s = rest[2 * nw + 4:2 * nw + 6]
        w_in = rest[2 * nw + 6:3 * nw + 6]
        wss, wrs, lsem, s1, r1, s2, r2, s3, r3, s4, r4 = rest[3 * nw + 6:]
        x, y, c, idx = _place()
        me = (x, y, c)
        for a in range(nw):
            w_in[a][...] = w_f32[a][...].astype(BF16)
        ag = _AllGather(w_in, w_out, wss, wrs, lsem)
        ag.start()
        call_ref[pl.ds(idx, 1), :] = c_ref[...]
        convg_ref[idx] = conv_ref[...]
        ph1 = []
        for k in range(1, NDEV):
            to = _dev(idx ^ k)
            ph1.append(_small_copy(c_ref, call_ref.at[pl.ds(idx, 1)], s1, r1, k - 1, to))
            ph1.append(_small_copy(conv_ref, convg_ref.at[idx], s2, r2, k - 1, to))
        for cp in ph1:
            cp.start()
        for k in range(1, NDEV):
            src_dev = idx ^ k
            _small_copy(c_ref, call_ref.at[pl.ds(src_dev, 1)], s1, r1, k - 1, me).wait_recv()
            _small_copy(conv_ref, convg_ref.at[src_dev], s2, r2, k - 1, me).wait_recv()
        call = call_ref[...]
        cact = (call * _sig(call))
        part_s[...] = jnp.dot(cact, adaw_ref[...], preferred_element_type=F32,
                              precision=lax.Precision.HIGHEST) + adab_ref[pl.ds(idx, 1), :]
        partf_s[...] = jnp.dot(cact, adafw_ref[...], preferred_element_type=F32,
                               precision=lax.Precision.HIGHEST) + adafb_ref[pl.ds(idx, 1), :]
        cparts_ref[pl.ds(idx, 1), :] = part_s[pl.ds(idx, 1), :]
        cfparts_ref[pl.ds(idx, 1), :] = partf_s[pl.ds(idx, 1), :]
        ph2 = []
        for k in range(1, NDEV):
            t = idx ^ k
            ph2.append(_small_copy(part_s.at[pl.ds(t, 1)], cparts_ref.at[pl.ds(idx, 1)], s3, r3, k - 1, _dev(t)))
            ph2.append(_small_copy(partf_s.at[pl.ds(t, 1)], cfparts_ref.at[pl.ds(idx, 1)], s4, r4, k - 1, _dev(t)))
        for cp in ph2:
            cp.start()
        for k in range(1, NDEV):
            src_dev = idx ^ k
            _small_copy(part_s.at[pl.ds(0, 1)], cparts_ref.at[pl.ds(src_dev, 1)], s3, r3, k - 1, me).wait_recv()
            _small_copy(partf_s.at[pl.ds(0, 1)], cfparts_ref.at[pl.ds(src_dev, 1)], s4, r4, k - 1, me).wait_recv()
        for cp in ph1 + ph2:
            cp.wait_send()
        ag.forward()
        ag.finish()

    dma7 = pltpu.SemaphoreType.DMA((NDEV - 1,))
    outs = pl.pallas_call(
        body,
        name="gather_weights",
        in_specs=[VM] * (6 + nw),
        out_specs=[VM] * 4 + [HBM] * nw,
        out_shape=[jax.ShapeDtypeStruct((NDEV, D), F32), jax.ShapeDtypeStruct((NDEV, ada_w.shape[1]), F32),
                   jax.ShapeDtypeStruct((NDEV, ada_f_w.shape[1]), F32),
                   jax.ShapeDtypeStruct((NDEV,) + conv_s.shape, F32)]
                  + [jax.ShapeDtypeStruct((NDEV,) + s.shape, BF16) for s in shards],
        scratch_shapes=[pltpu.VMEM((NDEV, ada_w.shape[1]), F32), pltpu.VMEM((NDEV, ada_f_w.shape[1]), F32)]
                       + [pltpu.VMEM(s.shape, BF16) for s in shards] + AG_SEMS(nw) + [dma7] * 8,
        compiler_params=pltpu.CompilerParams(vmem_limit_bytes=VMEM_LIMIT),
    )(c_row, ada_w, ada_b8, ada_f_w, ada_f_b8, conv_s, *shards)
    return outs[0], outs[1], outs[2], outs[3], outs[4:]


_VEC_AT = {
    "norm1_g": (8, 0, D), "a_ln_g": (11, 0, DA), "a_ln_b": (11, DA, DA), "a_spatial_b": (12, 0, D),
    "b_conv_b": (13, 0, DB), "b_gn_g": (13, DB, DB), "b_gn_b": (14, 0, DB), "out_norm_a_g": (14, DB, DA),
    "out_norm_b_g": (15, 0, DB), "norm2_g": (16, 0, D), "norm_f_g": (17, 0, D),
}
_LOSS_ROW = 18
_CW_ROW = 24


def _reduce_small(acc_f, acc_v, acc_b, acc_a, acc_bs, acc_cw, dws, after, pair_grads):
    npg = len(pair_grads)

    def body(accf_ref, accv_ref, accb_ref, acca_ref, accbs_ref, acccw_ref, dws_ref, after_ref, *rest):
        pg = rest[:npg]
        vsum_ref, dcond_ref, wssum_ref = rest[npg:npg + 3]
        pq = rest[npg + 3:2 * npg + 3]
        vloc, vbuf, wbuf, wown, vss, vrs, vls, s2, r2, s3, r3 = rest[2 * npg + 3:2 * npg + 14]
        pland = rest[2 * npg + 14:3 * npg + 14]
        pstage = rest[3 * npg + 14:4 * npg + 14]
        ps, pr, pls, pss = rest[4 * npg + 14:]
        x, y, c, idx = _place()
        me = (x, y, c)
        sibling = (x, y, 1 - c)
        chips = [(1 - x, y), (x, 1 - y), (1 - x, 1 - y)]
        blk = lambda p: 4 * p[0] + 2 * p[1] + p[2]
        give = [blk((*ch, 1 - c)) for ch in chips] + [blk(sibling)]
        pair = [pltpu.make_async_remote_copy(src_ref=pg[a].at[b], dst_ref=pland[a].at[j], send_sem=ps.at[a, j],
                                             recv_sem=pr.at[a, j], device_id=sibling, device_id_type=MESH)
                for a in range(npg) for j, b in enumerate(give)]
        loads = [pltpu.make_async_copy(pg[a].at[blk((*ch, c))], pstage[a].at[j], pls.at[a, j])
                 for a in range(npg) for j, ch in enumerate(chips)]
        for cp in pair + loads:
            cp.start()
        vloc[...] = jnp.zeros((NVEC, D), F32)
        vloc[0:1, :] = accv_ref[1:2, :]
        vloc[1:2, :] = accv_ref[2:3, :]
        vloc[2:3, :] = accv_ref[0:1, :]
        vloc[3:4, :] = accf_ref[4:5, :]
        vloc[4:5, :] = accf_ref[5:6, :]
        vloc[5:6, :] = accf_ref[3:4, :]
        vloc[6:7, :] = accf_ref[0:1, :]
        vloc[7:8, :] = accf_ref[1:2, :]
        vloc[8:9, :] = accv_ref[3:4, :]
        vloc[9:10, :] = accb_ref[:, 0:D]
        vloc[10:11, :] = accb_ref[:, D:]
        vloc[11:12, 0:DA] = acca_ref[0:1, :]
        vloc[11:12, DA:] = acca_ref[1:2, :]
        bst = accbs_ref[...].T
        for h in range(NH):
            vloc[12:13, h * CH:(h + 1) * CH] = bst[h:h + 1, :]
        vloc[13:14, 0:DB] = acca_ref[4:5, :]
        vloc[13:14, DB:] = acca_ref[5:6, :]
        vloc[14:15, 0:DB] = acca_ref[6:7, :]
        vloc[14:15, DB:] = acca_ref[2:3, :]
        vloc[15:16, 0:DB] = acca_ref[3:4, :]
        vloc[16:17, :] = accf_ref[6:7, :]
        vloc[17:18, :] = accf_ref[2:3, :]
        vloc[_LOSS_ROW:_LOSS_ROW + 1, :] = accf_ref[7:8, :]
        vloc[_CW_ROW:_CW_ROW + HALO // 2, 0:DB] = acccw_ref[0:HALO // 2, :]
        vloc[_CW_ROW:_CW_ROW + HALO // 2, DB:] = acccw_ref[HALO // 2:, :]
        gather_vecs = _AllGather([vloc], [vbuf], vss, vrs, vls)
        gather_vecs.start()
        rows_of = lambda t: pl.ds(pl.multiple_of(t * CH, CH), CH)
        wbuf[0] = dws_ref[rows_of(idx), :]
        sm = []
        for k in range(1, NDEV):
            t = idx ^ k
            sm.append(_small_copy(dws_ref.at[rows_of(t)], wbuf.at[k], s2, r2, k - 1, _dev(t)))
        for cp in sm:
            cp.start()
        for k in range(1, NDEV):
            _small_copy(dws_ref.at[rows_of(0)], wbuf.at[k], s2, r2, k - 1, me).wait_recv()
        ws = wbuf[0]
        for k in range(1, NDEV):
            ws = ws + wbuf[k]
        wown[...] = ws
        wssum_ref[rows_of(idx), :] = ws
        ag = [_small_copy(wown, wssum_ref.at[rows_of(idx)], s3, r3, k - 1, _dev(idx ^ k)) for k in range(1, NDEV)]
        for cp in ag:
            cp.start()
        for cp in loads:
            cp.wait()
        for cp in pair:
            cp.wait_recv()
        stores = []
        for a in range(npg):
            for j in range(3):
                pstage[a][j] = (pstage[a][j].astype(F32) + pland[a][j].astype(F32)).astype(BF16)
                stores.append(pltpu.make_async_copy(pstage[a].at[j], pq[a].at[j], pss.at[a, j]))
            stores.append(pltpu.make_async_copy(pland[a].at[3], pq[a].at[3], pss.at[a, 3]))
        for cp in stores:
            cp.start()
        gather_vecs.forward()
        gather_vecs.finish()
        vs = vbuf[0]
        for d in range(1, NDEV):
            vs = vs + vbuf[d]
        vsum_ref[...] = vs
        for d in range(NDEV):
            dcond_ref[d] = vbuf[d, 0:8, :]
        for k in range(1, NDEV):
            _small_copy(wown, wssum_ref.at[rows_of(idx ^ k)], s3, r3, k - 1, me).wait_recv()
        for cp in sm + ag:
            cp.wait_send()
        for cp in stores:
            cp.wait()
        for cp in pair:
            cp.wait_send()

    dma7 = pltpu.SemaphoreType.DMA((NDEV - 1,))
    dma4 = pltpu.SemaphoreType.DMA((npg, 4))
    outs = pl.pallas_call(
        body,
        name="reduce_small",
        in_specs=[VM] * 7 + [HBM] + [HBM] * npg,
        out_specs=[VM, VM, VM] + [HBM] * npg,
        out_shape=[jax.ShapeDtypeStruct((NVEC, D), F32), jax.ShapeDtypeStruct((NDEV, 8, D), F32),
                   jax.ShapeDtypeStruct(dws.shape, F32)]
                  + [jax.ShapeDtypeStruct((4,) + g.shape[1:], g.dtype) for g in pair_grads],
        scratch_shapes=[pltpu.VMEM((NVEC, D), F32), pltpu.VMEM((NDEV, NVEC, D), F32),
                        pltpu.VMEM((NDEV, CH, CH), F32), pltpu.VMEM((CH, CH), F32)] + AG_SEMS(1) + [dma7] * 4
                       + [pltpu.VMEM((4,) + g.shape[1:], g.dtype) for g in pair_grads]
                       + [pltpu.VMEM((3,) + g.shape[1:], g.dtype) for g in pair_grads] + [dma4] * 4,
        compiler_params=pltpu.CompilerParams(vmem_limit_bytes=VMEM_LIMIT),
    )(acc_f, acc_v, acc_b, acc_a, acc_bs, acc_cw, dws, after, *pair_grads)
    return outs[0], outs[1], outs[2], outs[3:]


HBM_ONLY = pl.BlockSpec(memory_space=pltpu.HBM)
SEM = pl.BlockSpec(memory_space=pltpu.SEMAPHORE)
EFFECT = pltpu.SideEffectType.DATAFLOW_SIDE_EFFECTING


def _rs_copies(g_refs, land_refs, sems, chips):
    x, y, c, idx = _place()
    if chips:
        routes = [(j, j, (*ch, c)) for j, ch in enumerate([(1 - x, y), (x, 1 - y), (1 - x, 1 - y)])]
    else:
        routes = [(idx ^ k, k - 1, _dev(idx ^ k)) for k in range(1, NDEV)]
    cps = []
    for src, dst, to in routes:
        for a in range(len(g_refs)):
            n = len(cps)
            cps.append(pltpu.make_async_remote_copy(
                src_ref=g_refs[a].at[src], dst_ref=land_refs[a].at[dst], send_sem=sems[2 * n],
                recv_sem=sems[2 * n + 1], device_id=to, device_id_type=MESH))
    return cps


def _rs_start(grads, name, after=(), chips=False):
    nw = len(grads)
    npeer = 3 if chips else NDEV - 1
    nsem = 2 * nw * npeer
    lands = [lax.empty((npeer,) + g.shape[1:], g.dtype) for g in grads]

    def body(*refs):
        g_refs, land_refs = refs[:nw], refs[nw:2 * nw]
        sems = refs[2 * nw + len(after):2 * nw + len(after) + nsem]
        token = refs[-1]
        for cp in _rs_copies(g_refs, land_refs, sems, chips):
            cp.start()
        token[...] = jnp.zeros_like(token)

    outs = pl.pallas_call(
        body, name=name,
        out_shape=(*[pltpu.SemaphoreType.DMA(())] * nsem,
                   *[pltpu.HBM(g.shape, g.dtype) for g in grads], *[pltpu.HBM(l.shape, l.dtype) for l in lands],
                   jax.ShapeDtypeStruct((8, CH), F32)),
        in_specs=[HBM_ONLY] * (2 * nw) + [HBM] * len(after),
        out_specs=(*[SEM] * nsem, *[HBM_ONLY] * (2 * nw), VM),
        input_output_aliases={i: nsem + i for i in range(2 * nw)},
        compiler_params=pltpu.CompilerParams(has_side_effects=EFFECT),
    )(*[pltpu.with_memory_space_constraint(g, pltpu.HBM) for g in grads],
      *[pltpu.with_memory_space_constraint(l, pltpu.HBM) for l in lands], *after)
    return outs[:nsem], outs[nsem:nsem + nw], outs[nsem + nw:nsem + 2 * nw], outs[-1]


def _rs_wait(sems, g_thru, land_thru, after, name, chips=False):
    nw = len(g_thru)
    nsem = len(sems)

    def body(*refs):
        g_refs, land_refs = refs[:nw], refs[nw:2 * nw]
        for cp in _rs_copies(g_refs, land_refs, refs[2 * nw:2 * nw + nsem], chips):
            cp.wait_send()
            cp.wait_recv()

    outs = pl.pallas_call(
        body, name=name,
        out_shape=tuple(pltpu.HBM(a.shape, a.dtype) for a in list(g_thru) + list(land_thru)),
        in_specs=[HBM_ONLY] * (2 * nw) + [SEM] * nsem + [HBM] * len(after),
        out_specs=tuple([HBM_ONLY] * (2 * nw)),
        input_output_aliases={i: i for i in range(2 * nw)},
        compiler_params=pltpu.CompilerParams(has_side_effects=EFFECT),
    )(*g_thru, *land_thru, *sems, *after)
    return outs[:nw], outs[nw:]


def _adamw(w, g, m, v):
    m2 = ADAM_B1 * m + (1.0 - ADAM_B1) * g
    v2 = ADAM_B2 * v + (1.0 - ADAM_B2) * (g * g)
    m_hat = m2 / (1.0 - ADAM_B1 ** ADAM_STEP)
    v_hat = v2 / (1.0 - ADAM_B2 ** ADAM_STEP)
    delta = -ADAM_LR * (m_hat / (jnp.sqrt(v_hat) + ADAM_EPS) + ADAM_WD * w)
    return delta, m2, v2


def _adam_big(r, w, m, v, rb, name, own, after=None, sib=None):
    R, C = w.shape
    ns = r.shape[0]
    g_all, idx1 = own

    def body(idx_ref, r_ref, own_ref, *refs):
        w_ref, m_ref, v_ref, g_ref, d_ref, m2_ref, v2_ref = refs[len(refs) - 7:]
        g = own_ref[0].astype(F32)
        if sib is not None:
            g = g + refs[0][0].astype(F32)
        for k in range(ns):
            g = g + r_ref[k].astype(F32)
        g_ref[...] = g
        d_ref[...], m2_ref[...], v2_ref[...] = _adamw(w_ref[...], g, m_ref[...], v_ref[...])

    t2 = pl.BlockSpec((rb, C), lambda i, idx_ref: (i, 0))
    sd = jax.ShapeDtypeStruct((R, C), F32)
    extra_specs = ([pl.BlockSpec((1, rb, C), lambda i, idx_ref: (3, i, 0))] if sib is not None else []) \
        + ([HBM] if after is not None else [])
    extra = ([sib] if sib is not None else []) + ([after] if after is not None else [])
    return pl.pallas_call(
        body, name=name,
        grid_spec=pltpu.PrefetchScalarGridSpec(
            num_scalar_prefetch=1, grid=(R // rb,),
            in_specs=[pl.BlockSpec((ns, rb, C), lambda i, idx_ref: (0, i, 0)),
                      pl.BlockSpec((1, rb, C), lambda i, idx_ref: (idx_ref[0], i, 0))] + extra_specs + [t2, t2, t2],
            out_specs=[t2, t2, t2, t2]),
        out_shape=[sd, sd, sd, sd],
        compiler_params=pltpu.CompilerParams(dimension_semantics=("arbitrary",), vmem_limit_bytes=VMEM_LIMIT),
    )(idx1, r, g_all, *extra, w, m, v)


def _adam_ada(c_all, dcs, w, m, v, rb, name):
    R, C = w.shape

    def body(c_ref, dc_ref, w_ref, m_ref, v_ref, g_ref, d_ref, m2_ref, v2_ref):
        cv = c_ref[...]
        g = lax.dot_general(cv * _sig(cv), dc_ref[...], (((0,), (0,)), ((), ())), preferred_element_type=F32,
                            precision=lax.Precision.HIGHEST)
        g_ref[...] = g
        d_ref[...], m2_ref[...], v2_ref[...] = _adamw(w_ref[...], g, m_ref[...], v_ref[...])

    t2 = pl.BlockSpec((rb, C), lambda i: (i, 0))
    sd = jax.ShapeDtypeStruct((R, C), F32)
    return pl.pallas_call(
        body, name=name, grid=(R // rb,),
        in_specs=[pl.BlockSpec((NDEV, rb), lambda i: (0, i)), _full((NDEV, C)), t2, t2, t2],
        out_specs=[t2, t2, t2, t2], out_shape=[sd, sd, sd, sd],
        compiler_params=pltpu.CompilerParams(dimension_semantics=("arbitrary",), vmem_limit_bytes=VMEM_LIMIT),
    )(c_all, dcs, w, m, v)


_SMALL = ["ada_b", "ada_f_b", "norm1_g", "b_in", "a_ln_g", "a_ln_b", "a_spatial_b", "b_conv_b", "b_gn_g", "b_gn_b",
          "out_norm_a_g", "out_norm_b_g", "norm2_g", "norm_f_g", "a_spatial_w", "b_conv_w"]


def _adam_small(vsum, wssum, gcw, params):
    names = _SMALL
    flat = []
    for n in names:
        flat += list(params[n])

    def body(vs_ref, ws_ref, gcw_ref, *rest):
        ins = rest[:3 * len(names)]
        outs = rest[3 * len(names):]
        for pi, n in enumerate(names):
            w_ref, m_ref, v_ref = ins[3 * pi:3 * pi + 3]
            g_ref, d_ref, m2_ref, v2_ref = outs[4 * pi:4 * pi + 4]
            if n in ("ada_b", "ada_f_b", "b_in"):
                row0 = {"ada_b": 0, "ada_f_b": 6, "b_in": 9}[n]
                pieces = [(vs_ref[row0 + r:row0 + r + 1, :], slice(r * D, (r + 1) * D))
                          for r in range(w_ref.shape[1] // D)]
            elif n == "a_spatial_w":
                pieces = [(ws_ref[...], slice(None))]
            elif n == "b_conv_w":
                pieces = [(gcw_ref[...], slice(None))]
            else:
                row, off, width = _VEC_AT[n]
                pieces = [(vs_ref[row:row + 1, off:off + width], slice(None))]
            for g, cs in pieces:
                g_ref[:, cs] = g
                d_ref[:, cs], m2_ref[:, cs], v2_ref[:, cs] = _adamw(w_ref[:, cs], g, m_ref[:, cs], v_ref[:, cs])

    out_shape = []
    for n in names:
        out_shape += [jax.ShapeDtypeStruct(params[n][0].shape, F32)] * 4
    outs = pl.pallas_call(
        body, name="adam_small",
        in_specs=[VM] * (3 + len(flat)), out_specs=[VM] * len(out_shape), out_shape=out_shape,
        compiler_params=pltpu.CompilerParams(vmem_limit_bytes=VMEM_LIMIT),
    )(vsum, wssum, gcw, *flat)
    return {n: outs[4 * pi:4 * pi + 4] for pi, n in enumerate(names)}


def _token_tile(T, want):
    return want if T % want == 0 else T


def kernel(x, c, ada_w, ada_b, norm1_g, w_in, b_in, a_ln_g, a_ln_b, a_spatial_w, a_spatial_b, b_conv_w, b_conv_b, b_gn_g, b_gn_b, out_norm_a_g, out_norm_b_g, w_out, norm2_g, w_ffn_in, w_ffn_out, ada_f_w, ada_f_b, norm_f_g, loss_target, m_ada_w, m_ada_b, m_norm1_g, m_w_in, m_b_in, m_a_ln_g, m_a_ln_b, m_a_spatial_w, m_a_spatial_b, m_b_conv_w, m_b_conv_b, m_b_gn_g, m_b_gn_b, m_out_norm_a_g, m_out_norm_b_g, m_w_out, m_norm2_g, m_w_ffn_in, m_w_ffn_out, m_ada_f_w, m_ada_f_b, m_norm_f_g, v_ada_w, v_ada_b, v_norm1_g, v_w_in, v_b_in, v_a_ln_g, v_a_ln_b, v_a_spatial_w, v_a_spatial_b, v_b_conv_w, v_b_conv_b, v_b_gn_g, v_b_gn_b, v_out_norm_a_g, v_out_norm_b_g, v_w_out, v_norm2_g, v_w_ffn_in, v_w_ffn_out, v_ada_f_w, v_ada_f_b, v_norm_f_g):
    T = x.shape[1]
    idx = 4 * lax.axis_index("x") + 2 * lax.axis_index("y") + lax.axis_index("c")
    x2d = x.reshape(T, D)
    tgt = loss_target.reshape(T, D)

    conv_s = jnp.pad(b_conv_w[0], ((0, HALO - KW), (0, 0)))
    call, cparts, cfparts, convg, (win_g, wout_g) = _gather(
        c, ada_w[0], ada_b.reshape(NDEV, -1), ada_f_w, ada_f_b.reshape(NDEV, -1), conv_s,
        [w_in[0], w_out[0]])
    wout = wout_g.reshape(D, D)
    mod = jnp.concatenate([cparts.reshape(6, D), cfparts.reshape(2, D)], axis=0)
    cw = jnp.transpose(convg, (1, 0, 2)).reshape(HALO, DB)

    tril = jnp.tril(jnp.ones((CH, CH), dtype=bool))
    wsm = jnp.where(tril[None], a_spatial_w[0], 0.0).astype(BF16)
    wcat = wsm.reshape(NH * CH, CH)
    wcat_t = jnp.transpose(wsm, (0, 2, 1)).reshape(NH * CH, CH)
    bsf = jnp.repeat(a_spatial_b[0].T, DA // NH, axis=1)
    pm = jnp.asarray(_GROUP_MEAN, BF16)
    esel = jnp.asarray(_HEAD_SELECT, BF16)

    tm = _token_tile(T, 256)
    tk = _token_tile(T, 2048)
    (x1, hb, zvg, mixed, yb, o, gu, dgelu_u, dgelu_v, vhat, rslb, yhat, rsg), (wfi_g, wfo_g) = _mix_fwd(
        x2d, mod, norm1_g, win_g, b_in, a_ln_g, a_ln_b, wcat, bsf, cw, b_conv_b, b_gn_g, b_gn_b, out_norm_a_g,
        out_norm_b_g, wout, pm, [w_ffn_in[0].T, w_ffn_out[0]], _token_tile(T, 512))
    dx1, h2b, dgu, act, dxg, acc_f = _ffn(x1, tgt, mod, norm2_g, norm_f_g.reshape(1, D),
                                          wfi_g.reshape(2 * DFF, D), wfo_g.reshape(DFF, D), tm)
    g_wfi = _wgrad_rows(dgu, h2b, 2 * WFI_B, tk, "wgrad_ffn_in").reshape(NDEV, WFI_B, D)
    g_wfo = _wgrad_rows(act, dxg, 2 * WFI_B, tk, "wgrad_ffn_out").reshape(NDEV, DFF // NDEV, D)
    f_sems, f_thru, f_land, f_token = _rs_start([g_wfi, g_wfo], "rs_ffn_start")
    (gx, acc_v, acc_b, acc_a, acc_bs, acc_ws, acc_cw), (g_win, g_wout) = _mix_bwd(
        dx1, x2d, zvg, mixed, o, hb, yb, gu, dgelu_u, dgelu_v, vhat, rslb, yhat, rsg, mod, norm1_g, win_g, a_ln_g,
        a_ln_b, wcat, wcat_t, cw, b_gn_g, b_gn_b, out_norm_a_g, out_norm_b_g, wout, pm, esel, f_token, tm)
    (g_wfi_d, g_wfo_d), (r_wfi, r_wfo) = _rs_wait(f_sems, f_thru, f_land, [acc_v], "rs_ffn_wait")
    g_wout = g_wout.reshape(NDEV, D // NDEV, D)

    vsum, dcond_all, wssum, (q_win, q_wout) = _reduce_small(acc_f, acc_v, acc_b, acc_a, acc_bs, acc_cw, acc_ws,
                                                            g_wfi_d, [g_win, g_wout])
    sems, g_thru, land_thru, token = _rs_start([q_win, q_wout], "rs_mix_start", after=(vsum,), chips=True)

    own = lambda g: (g, jnp.reshape(idx, (1,)).astype(jnp.int32))
    res = {}
    upd_wfi_t = _adam_big(r_wfi, w_ffn_in[0].T, m_w_ffn_in[0].T, v_w_ffn_in[0].T, WFI_B // 4, "adam_w_ffn_in",
                          own=own(g_wfi_d), after=token)
    res["w_ffn_in"] = tuple(a.T for a in upd_wfi_t)
    res["w_ffn_out"] = _adam_big(r_wfo, w_ffn_out[0], m_w_ffn_out[0], v_w_ffn_out[0], DFF // NDEV // 2,
                                 "adam_w_ffn_out", own=own(g_wfo_d), after=token)
    dcond = dcond_all.reshape(NDEV, 8 * D)
    nada = ada_w.shape[2]
    nadf = ada_f_w.shape[1]
    dcs = lax.dynamic_slice(dcond, (0, idx * nada), (NDEV, nada))
    dcfs = lax.dynamic_slice(dcond, (0, 6 * D + idx * nadf), (NDEV, nadf))
    res["ada_w"] = _adam_ada(call, dcs, ada_w[0], m_ada_w[0], v_ada_w[0], 256, "adam_ada_w")
    res["ada_f_w"] = _adam_ada(call, dcfs, ada_f_w, m_ada_f_w, v_ada_f_w, 256, "adam_ada_f_w")
    ncw = b_conv_w.shape[2]
    gcw = jnp.concatenate([lax.dynamic_slice(vsum, (_CW_ROW, idx * ncw), (HALO // 2, ncw)),
                           lax.dynamic_slice(vsum, (_CW_ROW, DB + idx * ncw), (HALO // 2, ncw))], axis=0)[:KW]
    two = lambda a: a.reshape(1, -1) if a.ndim == 1 else a.reshape(-1, a.shape[-1])
    small_in = {
        "ada_b": (ada_b, m_ada_b, v_ada_b), "ada_f_b": (ada_f_b, m_ada_f_b, v_ada_f_b),
        "norm1_g": (norm1_g, m_norm1_g, v_norm1_g), "b_in": (b_in, m_b_in, v_b_in),
        "a_ln_g": (a_ln_g, m_a_ln_g, v_a_ln_g), "a_ln_b": (a_ln_b, m_a_ln_b, v_a_ln_b),
        "a_spatial_b": (a_spatial_b.reshape(1, D), m_a_spatial_b.reshape(1, D), v_a_spatial_b.reshape(1, D)),
        "b_conv_b": (b_conv_b, m_b_conv_b, v_b_conv_b), "b_gn_g": (b_gn_g, m_b_gn_g, v_b_gn_g),
        "b_gn_b": (b_gn_b, m_b_gn_b, v_b_gn_b), "out_norm_a_g": (out_norm_a_g, m_out_norm_a_g, v_out_norm_a_g),
        "out_norm_b_g": (out_norm_b_g, m_out_norm_b_g, v_out_norm_b_g),
        "norm2_g": (norm2_g, m_norm2_g, v_norm2_g), "norm_f_g": (norm_f_g, m_norm_f_g, v_norm_f_g),
        "a_spatial_w": (a_spatial_w, m_a_spatial_w, v_a_spatial_w),
        "b_conv_w": (b_conv_w[0], m_b_conv_w[0], v_b_conv_w[0]),
    }
    small_in = {n: tuple(two(a) for a in t) for n, t in small_in.items()}
    res.update(_adam_small(vsum, wssum, gcw, small_in))
    (q_win_d, q_wout_d), (r_win, r_wout) = _rs_wait(
        sems, g_thru, land_thru,
        [upd_wfi_t[0], res["w_ffn_out"][0], res["ada_w"][0], res["ada_f_w"][0], res["norm_f_g"][0]],
        "rs_mix_wait", chips=True)
    res["w_in"] = _adam_big(r_win, w_in[0], m_w_in[0], v_w_in[0], 256, "adam_w_in", own=own(g_win), sib=q_win_d)
    res["w_out"] = _adam_big(r_wout, w_out[0], m_w_out[0], v_w_out[0], D // NDEV, "adam_w_out", own=own(g_wout),
                             sib=q_wout_d)

    loss = 0.5 / D * jnp.sum(vsum[_LOSS_ROW])
    shapes = {"ada_w": ada_w, "ada_b": ada_b, "norm1_g": norm1_g, "w_in": w_in, "b_in": b_in, "a_ln_g": a_ln_g,
              "a_ln_b": a_ln_b, "a_spatial_w": a_spatial_w, "a_spatial_b": a_spatial_b, "b_conv_w": b_conv_w,
              "b_conv_b": b_conv_b, "b_gn_g": b_gn_g, "b_gn_b": b_gn_b, "out_norm_a_g": out_norm_a_g,
              "out_norm_b_g": out_norm_b_g, "w_out": w_out, "norm2_g": norm2_g, "w_ffn_in": w_ffn_in,
              "w_ffn_out": w_ffn_out, "ada_f_w": ada_f_w, "ada_f_b": ada_f_b, "norm_f_g": norm_f_g}
    order = list(shapes)
    outs = [loss, gx.reshape(x.shape)]
    for which in range(4):
        outs += [res[n][which].reshape(shapes[n].shape) for n in order]
    return tuple(outs)
```

```python
import math

import numpy as np

import jax
import jax.numpy as jnp
from jax import lax
from jax.experimental import pallas as pl
from jax.experimental.pallas import tpu as pltpu

F32 = jnp.float32
BF16 = jnp.bfloat16

D = 1024
DA = 512
DB = 512
DIN = 2048
DFF = 2816
NH = 8
CH = 128
KW = 31
HALO = 32
NDEV = 8
WIN_B = DIN // NDEV
WFI_B = 2 * DFF // NDEV
EPS = 1e-6
NVEC = 40
VMEM_LIMIT = 56 * 1024 * 1024

ADAM_LR, ADAM_B1, ADAM_B2, ADAM_EPS, ADAM_WD, ADAM_STEP = 0.001, 0.9, 0.999, 1e-08, 0.01, 10

MESH = pl.DeviceIdType.MESH

_LANE = np.arange(DB)
_GROUP_MEAN = np.where((_LANE[:, None] >> 6) == (_LANE[None, :] >> 6), 1.0 / 64.0, 0.0).astype(np.float32)
_HEAD_SELECT = np.where((_LANE[:, None] >> 6) == np.arange(CH)[None, :], 1.0, 0.0).astype(np.float32)


def _dot(a, b):
    return jnp.dot(a, b, preferred_element_type=F32)


def _dot_nt(a, b):
    return lax.dot_general(a, b, (((1,), (1,)), ((), ())), preferred_element_type=F32)


def _dot_tn(a, b):
    return lax.dot_general(a, b, (((0,), (0,)), ((), ())), preferred_element_type=F32)


def _rs(v):
    return lax.rsqrt(jnp.mean(v * v, axis=-1, keepdims=True) + EPS)


def _sig(v):
    return 1.0 / (1.0 + jnp.exp(-v))


_INV_SQRT2 = 1.0 / math.sqrt(2.0)
_INV_SQRT2PI = 1.0 / math.sqrt(2.0 * math.pi)


def _gelu_parts(v):
    cdf = 0.5 * (1.0 + lax.erf(v * _INV_SQRT2))
    pdf = jnp.exp(-0.5 * v * v) * _INV_SQRT2PI
    return v * cdf, cdf + v * pdf


def _grp_mean(v, pm):
    hi = v.astype(BF16)
    lo = (v - hi.astype(F32)).astype(BF16)
    return _dot(hi, pm) + _dot(lo, pm)


def _colsum(v):
    return jnp.sum(v, axis=0, keepdims=True)


def _full(shape):
    nd = len(shape)
    return pl.BlockSpec(shape, lambda *_: (0,) * nd)


def _resident(shape):
    nd = len(shape)
    return pl.BlockSpec(shape, lambda *_: (0,) * nd, pipeline_mode=pl.Buffered(1))


HBM = pl.BlockSpec(memory_space=pl.ANY)
VM = pl.BlockSpec(memory_space=pltpu.VMEM)


SH_ROWS = HALO - 8


def _shifted_copies(buf, shbuf, tm):
    for b in range(1, 8):
        shbuf[b - 1] = buf[b:b + tm + SH_ROWS, :]


def _window(buf, shbuf, off, tm):
    a, b = divmod(off, 8)
    if b == 0:
        return buf[8 * a:8 * a + tm, :]
    return shbuf[b - 1, 8 * a:8 * a + tm, :]


def _first_head_lanes():
    return lax.broadcasted_iota(jnp.int32, (CH, CH), 1) < (DA // NH)


def _mix_heads(w_ref, vb, first):
    outs = []
    for p in range(NH // 2):
        v = vb[:, p * CH:(p + 1) * CH]
        a = _dot(w_ref[(2 * p) * CH:(2 * p + 1) * CH, :], v)
        b = _dot(w_ref[(2 * p + 1) * CH:(2 * p + 2) * CH, :], v)
        outs.append(jnp.where(first, a, b))
    return jnp.concatenate(outs, axis=1)


def _place():
    x, y, c = lax.axis_index("x"), lax.axis_index("y"), lax.axis_index("c")
    return x, y, c, 4 * x + 2 * y + c


def _dev(t):
    return (t >> 2, (t >> 1) & 1, t & 1)


class _AllGather:
    def __init__(self, w_in, w_out, wss, wrs, lsem):
        x, y, c, idx = _place()
        me, sibling = (x, y, c), (x, y, 1 - c)
        chips = [(1 - x, y), (x, 1 - y), (1 - x, 1 - y)]
        nw = len(w_in)

        def blk(p):
            return 4 * p[0] + 2 * p[1] + p[2]

        def wcopy(a, k, block, to, src=None):
            dst = w_out[a].at[blk(block)]
            return pltpu.make_async_remote_copy(src_ref=dst if src is None else src, dst_ref=dst,
                                                send_sem=wss.at[a, k], recv_sem=wrs.at[a, k],
                                                device_id=to, device_id_type=MESH)

        self.mine = [pltpu.make_async_copy(w_in[a], w_out[a].at[idx], lsem.at[a]) for a in range(nw)]
        self.first = []
        for a in range(nw):
            self.first.append(wcopy(a, 0, me, sibling, src=w_in[a]))
            self.first += [wcopy(a, 1 + j, me, (*chip, c), src=w_in[a]) for j, chip in enumerate(chips)]
        self.landed = [[wcopy(a, 1 + j, (*chip, c), me) for a in range(nw)] for j, chip in enumerate(chips)]
        self.passed = [[wcopy(a, 4 + j, (*chip, c), sibling) for a in range(nw)] for j, chip in enumerate(chips)]
        self.from_sibling = []
        for a in range(nw):
            self.from_sibling.append(wcopy(a, 0, sibling, me))
            self.from_sibling += [wcopy(a, 4 + j, (*chip, 1 - c), me) for j, chip in enumerate(chips)]

    def start(self):
        for cp in self.mine + self.first:
            cp.start()

    def forward(self):
        for land, pas in zip(self.landed, self.passed):
            for l, p in zip(land, pas):
                l.wait_recv()
                p.start()

    def finish(self):
        for cp in self.from_sibling:
            cp.wait_recv()
        for cp in self.first:
            cp.wait_send()
        for pas in self.passed:
            for p in pas:
                p.wait_send()
        for cp in self.mine:
            cp.wait()


AG_SEMS = lambda nw: [pltpu.SemaphoreType.DMA((nw, 7)), pltpu.SemaphoreType.DMA((nw, 7)),
                      pltpu.SemaphoreType.DMA((nw,))]


def _mix_fwd(x, mod, g1, win, b_in, lng, lnb, wcat, bsf, cw, cb, gng, gnb, oga, ogb, wout, pm, ffn_shards, tm):
    T = x.shape[0]
    nt = T // tm
    nch = tm // CH
    nw = len(ffn_shards)
    fwd_step = (5 * nt) // 8
    saved = [(D, F32), (D, BF16), (2 * DB, F32), (DA, F32), (D, BF16), (D, F32), (DA, F32), (DA, F32), (DA, F32),
             (DA, F32), (CH, F32), (DB, F32), (DB, F32)]
    NSAVE = len(saved)

    def body(x_ref, mod_ref, g1_ref, win_ref, bin_ref, lng_ref, lnb_ref, wcat_ref, bsf_ref, cw_ref, cb_ref,
             gng_ref, gnb_ref, oga_ref, ogb_ref, wout_ref, pm_ref, *rest):
        sh_f32 = rest[:nw]
        (x1_ref, h_ref, zvg_ref, mixed_ref, y_ref, o_ref, gu_ref, dgu_ref, dgv_ref, vhat_ref, rsl_ref, yhat_ref,
         rsg_ref) = rest[nw:nw + NSAVE]
        sh_out = rest[nw + NSAVE:2 * nw + NSAVE]
        glbuf, shbuf = rest[2 * nw + NSAVE:2 * nw + NSAVE + 2]
        sh_in = rest[2 * nw + NSAVE + 2:3 * nw + NSAVE + 2]
        wss, wrs, lsem = rest[3 * nw + NSAVE + 2:]
        i = pl.program_id(0)

        @pl.when(i == 0)
        def _():
            for a in range(nw):
                sh_in[a][...] = sh_f32[a][...].astype(BF16)
            _AllGather(sh_in, sh_out, wss, wrs, lsem).start()

        xv = x_ref[...]
        shift1 = mod_ref[0:1, :]
        scale1 = mod_ref[1:2, :]
        gate1 = mod_ref[2:3, :]
        h = (xv * _rs(xv) * g1_ref[...]) * (1.0 + scale1) + shift1
        hb = h.astype(BF16)
        h_ref[...] = hb
        z = jnp.concatenate([_dot(hb, win_ref[j]) for j in range(NDEV)], axis=1) + bin_ref[...]
        zvg_ref[...] = z[:, 2 * DA:]
        gu, dgelu_u = _gelu_parts(z[:, 0:DA])
        gv, dgelu_v = _gelu_parts(z[:, DA:2 * DA])
        gu_ref[...] = gu
        dgu_ref[...] = dgelu_u
        dgv_ref[...] = dgelu_v
        xc = gv - jnp.mean(gv, axis=-1, keepdims=True)
        rsl = lax.rsqrt(jnp.mean(xc * xc, axis=-1, keepdims=True) + EPS)
        vhat = xc * rsl
        vhat_ref[...] = vhat
        rsl_ref[...] = jnp.broadcast_to(rsl, (tm, CH))
        vnb = (vhat * lng_ref[...] + lnb_ref[...]).astype(BF16)
        first = _first_head_lanes()
        chunks = []
        for ci in range(nch):
            chunks.append(_mix_heads(wcat_ref, vnb[ci * CH:(ci + 1) * CH, :], first) + bsf_ref[...])
        mixed = jnp.concatenate(chunks, axis=0) if nch > 1 else chunks[0]
        mixed_ref[...] = mixed
        ya = gu * mixed
        gl = z[:, 2 * DA:2 * DA + DB] * _sig(z[:, 2 * DA + DB:])

        @pl.when(i == 0)
        def _():
            glbuf[0:HALO, :] = jnp.zeros((HALO, DB), F32)

        glbuf[HALO:HALO + tm, :] = gl
        _shifted_copies(glbuf, shbuf, tm)
        yc = jnp.zeros((tm, DB), F32) + cb_ref[...]
        for k in range(KW):
            yc = yc + cw_ref[k:k + 1, :] * _window(glbuf, shbuf, HALO - (KW - 1) + k, tm)
        glbuf[0:HALO, :] = gl[tm - HALO:, :]
        pmv = pm_ref[...]
        dc = yc - _grp_mean(yc, pmv)
        rsg = lax.rsqrt(_grp_mean(dc * dc, pmv) + EPS)
        yhat = dc * rsg
        yhat_ref[...] = yhat
        rsg_ref[...] = rsg
        yg = yhat * gng_ref[...] + gnb_ref[...]
        yb = yg * _sig(yg)
        na = ya * _rs(ya) * oga_ref[...]
        nb = yb * _rs(yb) * ogb_ref[...]
        yv = jnp.concatenate([na, nb], axis=1).astype(BF16)
        y_ref[...] = yv
        o = _dot(yv, wout_ref[...])
        o_ref[...] = o
        x1_ref[...] = xv + gate1 * o

        @pl.when(i == fwd_step)
        def _():
            _AllGather(sh_in, sh_out, wss, wrs, lsem).forward()

        @pl.when(i == nt - 1)
        def _():
            _AllGather(sh_in, sh_out, wss, wrs, lsem).finish()

    tile = lambda w: pl.BlockSpec((tm, w), lambda i: (i, 0))
    outs = pl.pallas_call(
        body,
        name="mix_fwd",
        grid=(nt,),
        in_specs=[tile(D), _full((8, D)), _full((1, D)), _resident((NDEV, D, WIN_B)), _full((1, DIN)),
                  _full((1, DA)), _full((1, DA)), _full((NH * CH, CH)), _full((CH, DA)), _full((HALO, DB)),
                  _full((1, DB)), _full((1, DB)), _full((1, DB)), _full((1, DA)), _full((1, DB)),
                  _resident((D, D)), _full((DB, DB))] + [_resident(s.shape) for s in ffn_shards],
        out_specs=[tile(w) for w, _ in saved] + [HBM] * nw,
        out_shape=[jax.ShapeDtypeStruct((T, w), dt) for w, dt in saved]
                  + [jax.ShapeDtypeStruct((NDEV,) + s.shape, BF16) for s in ffn_shards],
        scratch_shapes=[pltpu.VMEM((HALO + tm, DB), F32), pltpu.VMEM((7, tm + SH_ROWS, DB), F32)]
                       + [pltpu.VMEM(s.shape, BF16) for s in ffn_shards] + AG_SEMS(nw),
        compiler_params=pltpu.CompilerParams(dimension_semantics=("arbitrary",), vmem_limit_bytes=VMEM_LIMIT),
    )(x, mod, g1, win, b_in, lng, lnb, wcat, bsf, cw, cb, gng, gnb, oga, ogb, wout, pm, *ffn_shards)
    return outs[:NSAVE], outs[NSAVE:]


FF_BLOCKS = ((0, 1024), (1024, 1024), (2048, 768))


def _ffn(x1, tgt, mod, g2, gf, wfi_t, wfo, tm):
    T = x1.shape[0]
    nt = T // tm

    def body(x1_ref, tgt_ref, mod_ref, g2_ref, gf_ref, wfi_ref, wfo_ref,
             dx1_ref, h2_ref, dgu_ref, act_ref, dxg_ref, acc_ref, g_s, u_s):
        i = pl.program_id(0)

        @pl.when(i == 0)
        def _():
            acc_ref[...] = jnp.zeros((8, D), F32)

        x1 = x1_ref[...]
        shift2 = mod_ref[3:4, :]
        scale2 = mod_ref[4:5, :]
        gate2 = mod_ref[5:6, :]
        shiftf = mod_ref[6:7, :]
        scalef = mod_ref[7:8, :]
        g2v = g2_ref[...]
        gfv = gf_ref[...]
        r2 = _rs(x1)
        xn2 = x1 * r2
        h2b = (xn2 * g2v * (1.0 + scale2) + shift2).astype(BF16)
        h2_ref[...] = h2b
        f = jnp.zeros((tm, D), F32)
        for o, w in FF_BLOCKS:
            g = _dot_nt(h2b, wfi_ref[o:o + w, :])
            u = _dot_nt(h2b, wfi_ref[DFF + o:DFF + o + w, :])
            g_s[:, o:o + w] = g
            u_s[:, o:o + w] = u
            actb = (g * _sig(g) * u).astype(BF16)
            act_ref[:, o:o + w] = actb
            f = f + _dot(actb, wfo_ref[o:o + w, :])
        x2 = x1 + gate2 * f
        rf = _rs(x2)
        xnf = x2 * rf
        out = xnf * gfv * (1.0 + scalef) + shiftf
        e = out - tgt_ref[...]
        dout = e * (1.0 / D)
        acc_ref[7:8, :] += _colsum(e * e)
        acc_ref[0:1, :] += _colsum(dout)
        acc_ref[1:2, :] += _colsum(dout * xnf * gfv)
        acc_ref[2:3, :] += _colsum(dout * (1.0 + scalef) * xnf)
        dxnf = dout * (1.0 + scalef) * gfv
        dx2 = rf * (dxnf - xnf * jnp.mean(dxnf * xnf, axis=-1, keepdims=True))
        acc_ref[3:4, :] += _colsum(dx2 * f)
        dxgb = (dx2 * gate2).astype(BF16)
        dxg_ref[...] = dxgb
        dh2 = jnp.zeros((tm, D), F32)
        for o, w in FF_BLOCKS:
            dact = _dot_nt(dxgb, wfo_ref[o:o + w, :])
            g = g_s[:, o:o + w]
            u = u_s[:, o:o + w]
            s = _sig(g)
            dgb = (dact * u * (s * (1.0 + g * (1.0 - s)))).astype(BF16)
            dub = (dact * (g * s)).astype(BF16)
            dgu_ref[:, o:o + w] = dgb
            dgu_ref[:, DFF + o:DFF + o + w] = dub
            dh2 = dh2 + _dot(dgb, wfi_ref[o:o + w, :])
            dh2 = dh2 + _dot(dub, wfi_ref[DFF + o:DFF + o + w, :])
        acc_ref[4:5, :] += _colsum(dh2)
        acc_ref[5:6, :] += _colsum(dh2 * xn2 * g2v)
        acc_ref[6:7, :] += _colsum(dh2 * (1.0 + scale2) * xn2)
        dxn2 = dh2 * (1.0 + scale2) * g2v
        dx1_ref[...] = dx2 + r2 * (dxn2 - xn2 * jnp.mean(dxn2 * xn2, axis=-1, keepdims=True))

    tile = lambda w: pl.BlockSpec((tm, w), lambda i: (i, 0))
    return pl.pallas_call(
        body,
        name="ffn_fwd_bwd",
        grid=(nt,),
        in_specs=[tile(D), tile(D), _full((8, D)), _full((1, D)), _full((1, D)),
                  _resident((2 * DFF, D)), _resident((DFF, D))],
        out_specs=[tile(D), tile(D), tile(2 * DFF), tile(DFF), tile(D), _full((8, D))],
        out_shape=[jax.ShapeDtypeStruct((T, D), F32), jax.ShapeDtypeStruct((T, D), BF16),
                   jax.ShapeDtypeStruct((T, 2 * DFF), BF16), jax.ShapeDtypeStruct((T, DFF), BF16),
                   jax.ShapeDtypeStruct((T, D), BF16), jax.ShapeDtypeStruct((8, D), F32)],
        scratch_shapes=[pltpu.VMEM((tm, DFF), F32), pltpu.VMEM((tm, DFF), F32)],
        compiler_params=pltpu.CompilerParams(dimension_semantics=("arbitrary",), vmem_limit_bytes=VMEM_LIMIT),
    )(x1, tgt, mod, g2, gf, wfi_t, wfo)


def _mix_bwd(dx1, x, zvg, mixed, o, hb, yb, gu, dgu, dgv, vhat, rslb, yhat, rsg, mod, g1, win, lng, lnb, wcat, wcat_t,
             cw, gng, gnb, oga, ogb, wout, pm, esel, after, tm):
    T = x.shape[0]
    nt = T // tm
    nch = tm // CH
    WOB = 256

    def body(dx1_ref, x_ref, zvg_ref, mixed_ref, o_ref, hb_ref, yb_ref, gu_ref, dgu_ref, dgv_ref, vhat_ref, rsl_ref,
             yhat_ref, rsg_ref, mod_ref, g1_ref, win_ref, lng_ref, lnb_ref, wcat_ref, wcatt_ref, cw_ref, gng_ref,
             gnb_ref, oga_ref, ogb_ref, wout_ref, pm_ref, esel_ref, after_ref,
             gx_ref, accv_ref, accb_ref, acca_ref, accbs_ref, accws_ref, acccw_ref, gwin_ref, gwout_ref,
             dycbuf, shbuf, bs_s, acc_win, acc_wout, st_win, st_wout):
        i = pl.program_id(0)

        @pl.when(i == 0)
        def _():
            acc_win[...] = jnp.zeros((NDEV, D, WIN_B), F32)
            acc_wout[...] = jnp.zeros((D, D), F32)
            accv_ref[...] = jnp.zeros((8, D), F32)
            accb_ref[...] = jnp.zeros((1, DIN), F32)
            acca_ref[...] = jnp.zeros((8, DA), F32)
            accws_ref[...] = jnp.zeros((NH * CH, CH), F32)
            acccw_ref[...] = jnp.zeros((HALO, DB), F32)
            bs_s[...] = jnp.zeros((CH, DA), F32)
            dycbuf[tm:tm + HALO, :] = jnp.zeros((HALO, DB), F32)

        shift1 = mod_ref[0:1, :]
        scale1 = mod_ref[1:2, :]
        gate1 = mod_ref[2:3, :]
        g1v = g1_ref[...]
        xv = x_ref[...]
        r1 = _rs(xv)
        xn1 = xv * r1
        val = zvg_ref[:, 0:DB]
        gate = zvg_ref[:, DB:]
        gu = gu_ref[...]
        dgelu_u = dgu_ref[...]
        dgelu_v = dgv_ref[...]
        vhat = vhat_ref[...]
        rsl = rsl_ref[:, 0:1]
        lngv = lng_ref[...]
        vnb = (vhat * lngv + lnb_ref[...]).astype(BF16)
        mixed = mixed_ref[...]
        ya = gu * mixed
        ra = _rs(ya)
        yan = ya * ra
        sgt = _sig(gate)
        gl = val * sgt
        pmv = pm_ref[...]
        rsg = rsg_ref[...]
        yhat = yhat_ref[...]
        gngv = gng_ref[...]
        yg = yhat * gngv + gnb_ref[...]
        sgy = _sig(yg)
        yb = yg * sgy
        rb = _rs(yb)
        ybn = yb * rb
        dx1 = dx1_ref[...]
        accv_ref[0:1, :] += _colsum(dx1 * o_ref[...])
        dogb = (dx1 * gate1).astype(BF16)
        acc_wout[...] += _dot_tn(yb_ref[...], dogb)
        dy = _dot_nt(dogb, wout_ref[...])
        dna = dy[:, 0:DA]
        dnb = dy[:, DA:]
        ogav = oga_ref[...]
        ogbv = ogb_ref[...]
        acca_ref[2:3, :] += _colsum(dna * yan)
        acca_ref[3:4, :] += _colsum(dnb * ybn)
        ta = dna * ogav
        dya = ra * (ta - yan * jnp.mean(ta * yan, axis=-1, keepdims=True))
        tb = dnb * ogbv
        dyb = rb * (tb - ybn * jnp.mean(tb * ybn, axis=-1, keepdims=True))
        dgu = dya * mixed
        dm = dya * gu
        first = _first_head_lanes()
        zero = jnp.zeros((CH, CH), BF16)
        dvn_chunks = []
        bs_acc = bs_s[...]
        for ci in range(nch):
            dmc = dm[ci * CH:(ci + 1) * CH, :]
            bs_acc = bs_acc + dmc
            dmcb = dmc.astype(BF16)
            dvn_chunks.append(_mix_heads(wcatt_ref, dmcb, first))
            vc = vnb[ci * CH:(ci + 1) * CH, :]
            for p in range(NH // 2):
                xt = dmcb[:, p * CH:(p + 1) * CH]
                vt = vc[:, p * CH:(p + 1) * CH]
                accws_ref[(2 * p) * CH:(2 * p + 1) * CH, :] += _dot_nt(jnp.where(first, xt, zero), vt)
                accws_ref[(2 * p + 1) * CH:(2 * p + 2) * CH, :] += _dot_nt(jnp.where(first, zero, xt), vt)
        bs_s[...] = bs_acc
        dvn = jnp.concatenate(dvn_chunks, axis=0) if nch > 1 else dvn_chunks[0]
        acca_ref[0:1, :] += _colsum(dvn * vhat)
        acca_ref[1:2, :] += _colsum(dvn)
        dvh = dvn * lngv
        dgv = rsl * (dvh - jnp.mean(dvh, axis=-1, keepdims=True)
                     - vhat * jnp.mean(dvh * vhat, axis=-1, keepdims=True))
        du = dgu * dgelu_u
        dv = dgv * dgelu_v
        dyg = dyb * (sgy * (1.0 + yg * (1.0 - sgy)))
        acca_ref[5:6, :] += _colsum(dyg * yhat)
        acca_ref[6:7, :] += _colsum(dyg)
        dyh = dyg * gngv
        dyc = rsg * (dyh - _grp_mean(dyh, pmv) - yhat * _grp_mean(dyh * yhat, pmv))
        acca_ref[4:5, :] += _colsum(dyc)
        dycbuf[0:tm, :] = dyc
        _shifted_copies(dycbuf, shbuf, tm)
        dgl = jnp.zeros((tm, DB), F32)
        for k in range(KW):
            win_k = _window(dycbuf, shbuf, KW - 1 - k, tm)
            dgl = dgl + cw_ref[k:k + 1, :] * win_k
            acccw_ref[k:k + 1, :] += _colsum(win_k * gl)
        dycbuf[tm:tm + HALO, :] = dyc[0:HALO, :]
        dval = dgl * sgt
        dgate = dgl * val * sgt * (1.0 - sgt)
        dz = jnp.concatenate([du, dv, dval, dgate], axis=1)
        accb_ref[...] += _colsum(dz)
        dzb = dz.astype(BF16)
        hbv = hb_ref[...]
        dh = jnp.zeros((tm, D), F32)
        for j in range(NDEV):
            dzj = dzb[:, j * WIN_B:(j + 1) * WIN_B]
            acc_win[j] += _dot_tn(hbv, dzj)
            dh = dh + _dot_nt(dzj, win_ref[j])
        accv_ref[1:2, :] += _colsum(dh)
        dh_xn = _colsum(dh * xn1)
        accv_ref[2:3, :] += dh_xn * g1v
        accv_ref[3:4, :] += dh_xn * (1.0 + scale1)
        dxn1 = dh * (1.0 + scale1) * g1v
        gx_ref[...] = dx1 + r1 * (dxn1 - xn1 * jnp.mean(dxn1 * xn1, axis=-1, keepdims=True))

        @pl.when(i == nt - 1)
        def _():
            rows = lax.broadcasted_iota(jnp.int32, (NH * CH, CH), 0) & (CH - 1)
            cols = lax.broadcasted_iota(jnp.int32, (NH * CH, CH), 1)
            accws_ref[...] = jnp.where(cols <= rows, accws_ref[...], 0.0)
            bs = bs_s[...]
            hi = bs.astype(BF16)
            r1_ = bs - hi.astype(F32)
            mid = r1_.astype(BF16)
            lo = (r1_ - mid.astype(F32)).astype(BF16)
            ev = esel_ref[...]
            accbs_ref[...] = _dot(hi, ev) + _dot(mid, ev) + _dot(lo, ev)
            for j in range(NDEV):
                st_win[...] = acc_win[j].astype(BF16)
                pltpu.sync_copy(st_win, gwin_ref.at[j])
            for j in range(D // WOB):
                st_wout[...] = acc_wout[j * WOB:(j + 1) * WOB, :].astype(BF16)
                pltpu.sync_copy(st_wout, gwout_ref.at[pl.ds(j * WOB, WOB)])

    rev = lambda w: pl.BlockSpec((tm, w), lambda i: (nt - 1 - i, 0))
    outs = pl.pallas_call(
        body,
        name="mix_bwd",
        grid=(nt,),
        in_specs=[rev(D), rev(D), rev(2 * DB), rev(DA), rev(D), rev(D), rev(D), rev(DA), rev(DA), rev(DA), rev(DA),
                  rev(CH), rev(DB), rev(DB), _full((8, D)), _full((1, D)),
                  _resident((NDEV, D, WIN_B)), _full((1, DA)), _full((1, DA)), _full((NH * CH, CH)),
                  _full((NH * CH, CH)), _full((HALO, DB)), _full((1, DB)), _full((1, DB)), _full((1, DA)),
                  _full((1, DB)), _resident((D, D)), _full((DB, DB)), _full((DA, CH)), HBM],
        out_specs=[rev(D), _full((8, D)), _full((1, DIN)), _full((8, DA)), _full((CH, CH)),
                   _full((NH * CH, CH)), _full((HALO, DB)), HBM, HBM],
        out_shape=[jax.ShapeDtypeStruct((T, D), F32), jax.ShapeDtypeStruct((8, D), F32),
                   jax.ShapeDtypeStruct((1, DIN), F32), jax.ShapeDtypeStruct((8, DA), F32),
                   jax.ShapeDtypeStruct((CH, CH), F32), jax.ShapeDtypeStruct((NH * CH, CH), F32),
                   jax.ShapeDtypeStruct((HALO, DB), F32),
                   jax.ShapeDtypeStruct((NDEV, D, WIN_B), BF16), jax.ShapeDtypeStruct((D, D), BF16)],
        scratch_shapes=[pltpu.VMEM((tm + HALO, DB), F32), pltpu.VMEM((7, tm + SH_ROWS, DB), F32),
                        pltpu.VMEM((CH, DA), F32), pltpu.VMEM((NDEV, D, WIN_B), F32), pltpu.VMEM((D, D), F32),
                        pltpu.VMEM((D, WIN_B), BF16), pltpu.VMEM((WOB, D), BF16)],
        compiler_params=pltpu.CompilerParams(dimension_semantics=("arbitrary",), vmem_limit_bytes=VMEM_LIMIT),
    )(dx1, x, zvg, mixed, o, hb, yb, gu, dgu, dgv, vhat, rslb, yhat, rsg, mod, g1, win, lng, lnb, wcat, wcat_t, cw,
      gng, gnb, oga, ogb, wout, pm, esel, after)
    return outs[:7], outs[7:]


def _wgrad_rows(a, b, bm, tk, name):
    T, M = a.shape
    N = b.shape[1]
    nk = T // tk

    def body(a_ref, b_ref, o_ref, acc):
        k = pl.program_id(1)

        @pl.when(k == 0)
        def _():
            acc[...] = jnp.zeros((bm, N), F32)

        acc[...] += _dot_tn(a_ref[...], b_ref[...])

        @pl.when(k == nk - 1)
        def _():
            o_ref[...] = acc[...].astype(BF16)

    return pl.pallas_call(
        body, name=name, grid=(M // bm, nk),
        in_specs=[pl.BlockSpec((tk, bm), lambda j, k: (k, j)), pl.BlockSpec((tk, N), lambda j, k: (k, 0))],
        out_specs=pl.BlockSpec((bm, N), lambda j, k: (j, 0)),
        out_shape=jax.ShapeDtypeStruct((M, N), BF16),
        scratch_shapes=[pltpu.VMEM((bm, N), F32)],
        compiler_params=pltpu.CompilerParams(dimension_semantics=("arbitrary", "arbitrary"),
                                             vmem_limit_bytes=VMEM_LIMIT),
    )(a, b)


def _small_copy(src, dst, ss, rs, k, to):
    return pltpu.make_async_remote_copy(src_ref=src, dst_ref=dst, send_sem=ss.at[k], recv_sem=rs.at[k],
                                        device_id=to, device_id_type=MESH)


def _gather(c_row, ada_w, ada_b8, ada_f_w, ada_f_b8, conv_s, shards):
    nw = len(shards)

    def body(c_ref, adaw_ref, adab_ref, adafw_ref, adafb_ref, conv_ref, *rest):
        w_f32 = rest[:nw]
        call_ref, cparts_ref, cfparts_ref, convg_ref = rest[nw:nw + 4]
        w_out = rest[nw + 4:2 * nw + 4]
        part_s, partf_s = rest[2 * nw + 4:2 * nw + 6]
        w_in = rest[2 * nw + 6:3 * nw + 6]
        wss, wrs, lsem, s1, r1, s2, r2, s3, r3, s4, r4 = rest[3 * nw + 6:]
        x, y, c, idx = _place()
        me = (x, y, c)
        for a in range(nw):
            w_in[a][...] = w_f32[a][...].astype(BF16)
        ag = _AllGather(w_in, w_out, wss, wrs, lsem)
        ag.start()
        call_ref[pl.ds(idx, 1), :] = c_ref[...]
        convg_ref[idx] = conv_ref[...]
        ph1 = []
        for k in range(1, NDEV):
            to = _dev(idx ^ k)
            ph1.append(_small_copy(c_ref, call_ref.at[pl.ds(idx, 1)], s1, r1, k - 1, to))
            ph1.append(_small_copy(conv_ref, convg_ref.at[idx], s2, r2, k - 1, to))
        for cp in ph1:
            cp.start()
        for k in range(1, NDEV):
            src_dev = idx ^ k
            _small_copy(c_ref, call_ref.at[pl.ds(src_dev, 1)], s1, r1, k - 1, me).wait_recv()
            _small_copy(conv_ref, convg_ref.at[src_dev], s2, r2, k - 1, me).wait_recv()
        call = call_ref[...]
        cact = (call * _sig(call))
        part_s[...] = jnp.dot(cact, adaw_ref[...], preferred_element_type=F32,
                              precision=lax.Precision.HIGHEST) + adab_ref[pl.ds(idx, 1), :]
        partf_s[...] = jnp.dot(cact, adafw_ref[...], preferred_element_type=F32,
                               precision=lax.Precision.HIGHEST) + adafb_ref[pl.ds(idx, 1), :]
        cparts_ref[pl.ds(idx, 1), :] = part_s[pl.ds(idx, 1), :]
        cfparts_ref[pl.ds(idx, 1), :] = partf_s[pl.ds(idx, 1), :]
        ph2 = []
        for k in range(1, NDEV):
            t = idx ^ k
            ph2.append(_small_copy(part_s.at[pl.ds(t, 1)], cparts_ref.at[pl.ds(idx, 1)], s3, r3, k - 1, _dev(t)))
            ph2.append(_small_copy(partf_s.at[pl.ds(t, 1)], cfparts_ref.at[pl.ds(idx, 1)], s4, r4, k - 1, _dev(t)))
        for cp in ph2:
            cp.start()
        for k in range(1, NDEV):
            src_dev = idx ^ k
            _small_copy(part_s.at[pl.ds(0, 1)], cparts_ref.at[pl.ds(src_dev, 1)], s3, r3, k - 1, me).wait_recv()
            _small_copy(partf_s.at[pl.ds(0, 1)], cfparts_ref.at[pl.ds(src_dev, 1)], s4, r4, k - 1, me).wait_recv()
        for cp in ph1 + ph2:
            cp.wait_send()
        ag.forward()
        ag.finish()

    dma7 = pltpu.SemaphoreType.DMA((NDEV - 1,))
    outs = pl.pallas_call(
        body,
        name="gather_weights",
        in_specs=[VM] * (6 + nw),
        out_specs=[VM] * 4 + [HBM] * nw,
        out_shape=[jax.ShapeDtypeStruct((NDEV, D), F32), jax.ShapeDtypeStruct((NDEV, ada_w.shape[1]), F32),
                   jax.ShapeDtypeStruct((NDEV, ada_f_w.shape[1]), F32),
                   jax.ShapeDtypeStruct((NDEV,) + conv_s.shape, F32)]
                  + [jax.ShapeDtypeStruct((NDEV,) + s.shape, BF16) for s in shards],
        scratch_shapes=[pltpu.VMEM((NDEV, ada_w.shape[1]), F32), pltpu.VMEM((NDEV, ada_f_w.shape[1]), F32)]
                       + [pltpu.VMEM(s.shape, BF16) for s in shards] + AG_SEMS(nw) + [dma7] * 8,
        compiler_params=pltpu.CompilerParams(vmem_limit_bytes=VMEM_LIMIT),
    )(c_row, ada_w, ada_b8, ada_f_w, ada_f_b8, conv_s, *shards)
    return outs[0], outs[1], outs[2], outs[3], outs[4:]


_VEC_AT = {
    "norm1_g": (8, 0, D), "a_ln_g": (11, 0, DA), "a_ln_b": (11, DA, DA), "a_spatial_b": (12, 0, D),
    "b_conv_b": (13, 0, DB), "b_gn_g": (13, DB, DB), "b_gn_b": (14, 0, DB), "out_norm_a_g": (14, DB, DA),
    "out_norm_b_g": (15, 0, DB), "norm2_g": (16, 0, D), "norm_f_g": (17, 0, D),
}
_LOSS_ROW = 18
_CW_ROW = 24


def _reduce_small(acc_f, acc_v, acc_b, acc_a, acc_bs, acc_cw, dws, after, pair_grads):
    npg = len(pair_grads)

    def body(accf_ref, accv_ref, accb_ref, acca_ref, accbs_ref, acccw_ref, dws_ref, after_ref, *rest):
        pg = rest[:npg]
        vsum_ref, dcond_ref, wssum_ref = rest[npg:npg + 3]
        pq = rest[npg + 3:2 * npg + 3]
        vloc, vbuf, wbuf, wown, vss, vrs, vls, s2, r2, s3, r3 = rest[2 * npg + 3:2 * npg + 14]
        pland = rest[2 * npg + 14:3 * npg + 14]
        pstage = rest[3 * npg + 14:4 * npg + 14]
        ps, pr, pls, pss = rest[4 * npg + 14:]
        x, y, c, idx = _place()
        me = (x, y, c)
        sibling = (x, y, 1 - c)
        chips = [(1 - x, y), (x, 1 - y), (1 - x, 1 - y)]
        blk = lambda p: 4 * p[0] + 2 * p[1] + p[2]
        give = [blk((*ch, 1 - c)) for ch in chips] + [blk(sibling)]
        pair = [pltpu.make_async_remote_copy(src_ref=pg[a].at[b], dst_ref=pland[a].at[j], send_sem=ps.at[a, j],
                                             recv_sem=pr.at[a, j], device_id=sibling, device_id_type=MESH)
                for a in range(npg) for j, b in enumerate(give)]
        loads = [pltpu.make_async_copy(pg[a].at[blk((*ch, c))], pstage[a].at[j], pls.at[a, j])
                 for a in range(npg) for j, ch in enumerate(chips)]
        for cp in pair + loads:
            cp.start()
        vloc[...] = jnp.zeros((NVEC, D), F32)
        vloc[0:1, :] = accv_ref[1:2, :]
        vloc[1:2, :] = accv_ref[2:3, :]
        vloc[2:3, :] = accv_ref[0:1, :]
        vloc[3:4, :] = accf_ref[4:5, :]
        vloc[4:5, :] = accf_ref[5:6, :]
        vloc[5:6, :] = accf_ref[3:4, :]
        vloc[6:7, :] = accf_ref[0:1, :]
        vloc[7:8, :] = accf_ref[1:2, :]
        vloc[8:9, :] = accv_ref[3:4, :]
        vloc[9:10, :] = accb_ref[:, 0:D]
        vloc[10:11, :] = accb_ref[:, D:]
        vloc[11:12, 0:DA] = acca_ref[0:1, :]
        vloc[11:12, DA:] = acca_ref[1:2, :]
        bst = accbs_ref[...].T
        for h in range(NH):
            vloc[12:13, h * CH:(h + 1) * CH] = bst[h:h + 1, :]
        vloc[13:14, 0:DB] = acca_ref[4:5, :]
        vloc[13:14, DB:] = acca_ref[5:6, :]
        vloc[14:15, 0:DB] = acca_ref[6:7, :]
        vloc[14:15, DB:] = acca_ref[2:3, :]
        vloc[15:16, 0:DB] = acca_ref[3:4, :]
        vloc[16:17, :] = accf_ref[6:7, :]
        vloc[17:18, :] = accf_ref[2:3, :]
        vloc[_LOSS_ROW:_LOSS_ROW + 1, :] = accf_ref[7:8, :]
        vloc[_CW_ROW:_CW_ROW + HALO // 2, 0:DB] = acccw_ref[0:HALO // 2, :]
        vloc[_CW_ROW:_CW_ROW + HALO // 2, DB:] = acccw_ref[HALO // 2:, :]
        gather_vecs = _AllGather([vloc], [vbuf], vss, vrs, vls)
        gather_vecs.start()
        rows_of = lambda t: pl.ds(pl.multiple_of(t * CH, CH), CH)
        wbuf[0] = dws_ref[rows_of(idx), :]
        sm = []
        for k in range(1, NDEV):
            t = idx ^ k
            sm.append(_small_copy(dws_ref.at[rows_of(t)], wbuf.at[k], s2, r2, k - 1, _dev(t)))
        for cp in sm:
            cp.start()
        for k in range(1, NDEV):
            _small_copy(dws_ref.at[rows_of(0)], wbuf.at[k], s2, r2, k - 1, me).wait_recv()
        ws = wbuf[0]
        for k in range(1, NDEV):
            ws = ws + wbuf[k]
        wown[...] = ws
        wssum_ref[rows_of(idx), :] = ws
        ag = [_small_copy(wown, wssum_ref.at[rows_of(idx)], s3, r3, k - 1, _dev(idx ^ k)) for k in range(1, NDEV)]
        for cp in ag:
            cp.start()
        for cp in loads:
            cp.wait()
        for cp in pair:
            cp.wait_recv()
        stores = []
        for a in range(npg):
            for j in range(3):
                pstage[a][j] = (pstage[a][j].astype(F32) + pland[a][j].astype(F32)).astype(BF16)
                stores.append(pltpu.make_async_copy(pstage[a].at[j], pq[a].at[j], pss.at[a, j]))
            stores.append(pltpu.make_async_copy(pland[a].at[3], pq[a].at[3], pss.at[a, 3]))
        for cp in stores:
            cp.start()
        gather_vecs.forward()
        gather_vecs.finish()
        vs = vbuf[0]
        for d in range(1, NDEV):
            vs = vs + vbuf[d]
        vsum_ref[...] = vs
        for d in range(NDEV):
            dcond_ref[d] = vbuf[d, 0:8, :]
        for k in range(1, NDEV):
            _small_copy(wown, wssum_ref.at[rows_of(idx ^ k)], s3, r3, k - 1, me).wait_recv()
        for cp in sm + ag:
            cp.wait_send()
        for cp in stores:
            cp.wait()
        for cp in pair:
            cp.wait_send()

    dma7 = pltpu.SemaphoreType.DMA((NDEV - 1,))
    dma4 = pltpu.SemaphoreType.DMA((npg, 4))
    outs = pl.pallas_call(
        body,
        name="reduce_small",
        in_specs=[VM] * 7 + [HBM] + [HBM] * npg,
        out_specs=[VM, VM, VM] + [HBM] * npg,
        out_shape=[jax.ShapeDtypeStruct((NVEC, D), F32), jax.ShapeDtypeStruct((NDEV, 8, D), F32),
                   jax.ShapeDtypeStruct(dws.shape, F32)]
                  + [jax.ShapeDtypeStruct((4,) + g.shape[1:], g.dtype) for g in pair_grads],
        scratch_shapes=[pltpu.VMEM((NVEC, D), F32), pltpu.VMEM((NDEV, NVEC, D), F32),
                        pltpu.VMEM((NDEV, CH, CH), F32), pltpu.VMEM((CH, CH), F32)] + AG_SEMS(1) + [dma7] * 4
                       + [pltpu.VMEM((4,) + g.shape[1:], g.dtype) for g in pair_grads]
                       + [pltpu.VMEM((3,) + g.shape[1:], g.dtype) for g in pair_grads] + [dma4] * 4,
        compiler_params=pltpu.CompilerParams(vmem_limit_bytes=VMEM_LIMIT),
    )(acc_f, acc_v, acc_b, acc_a, acc_bs, acc_cw, dws, after, *pair_grads)
    return outs[0], outs[1], outs[2], outs[3:]


HBM_ONLY = pl.BlockSpec(memory_space=pltpu.HBM)
SEM = pl.BlockSpec(memory_space=pltpu.SEMAPHORE)
EFFECT = pltpu.SideEffectType.DATAFLOW_SIDE_EFFECTING


def _rs_copies(g_refs, land_refs, sems, chips):
    x, y, c, idx = _place()
    if chips:
        routes = [(j, j, (*ch, c)) for j, ch in enumerate([(1 - x, y), (x, 1 - y), (1 - x, 1 - y)])]
    else:
        routes = [(idx ^ k, k - 1, _dev(idx ^ k)) for k in range(1, NDEV)]
    cps = []
    for src, dst, to in routes:
        for a in range(len(g_refs)):
            n = len(cps)
            cps.append(pltpu.make_async_remote_copy(
                src_ref=g_refs[a].at[src], dst_ref=land_refs[a].at[dst], send_sem=sems[2 * n],
                recv_sem=sems[2 * n + 1], device_id=to, device_id_type=MESH))
    return cps


def _rs_start(grads, name, after=(), chips=False):
    nw = len(grads)
    npeer = 3 if chips else NDEV - 1
    nsem = 2 * nw * npeer
    lands = [lax.empty((npeer,) + g.shape[1:], g.dtype) for g in grads]

    def body(*refs):
        g_refs, land_refs = refs[:nw], refs[nw:2 * nw]
        sems = refs[2 * nw + len(after):2 * nw + len(after) + nsem]
        token = refs[-1]
        for cp in _rs_copies(g_refs, land_refs, sems, chips):
            cp.start()
        token[...] = jnp.zeros_like(token)

    outs = pl.pallas_call(
        body, name=name,
        out_shape=(*[pltpu.SemaphoreType.DMA(())] * nsem,
                   *[pltpu.HBM(g.shape, g.dtype) for g in grads], *[pltpu.HBM(l.shape, l.dtype) for l in lands],
                   jax.ShapeDtypeStruct((8, CH), F32)),
        in_specs=[HBM_ONLY] * (2 * nw) + [HBM] * len(after),
        out_specs=(*[SEM] * nsem, *[HBM_ONLY] * (2 * nw), VM),
        input_output_aliases={i: nsem + i for i in range(2 * nw)},
        compiler_params=pltpu.CompilerParams(has_side_effects=EFFECT),
    )(*[pltpu.with_memory_space_constraint(g, pltpu.HBM) for g in grads],
      *[pltpu.with_memory_space_constraint(l, pltpu.HBM) for l in lands], *after)
    return outs[:nsem], outs[nsem:nsem + nw], outs[nsem + nw:nsem + 2 * nw], outs[-1]


def _rs_wait(sems, g_thru, land_thru, after, name, chips=False):
    nw = len(g_thru)
    nsem = len(sems)

    def body(*refs):
        g_refs, land_refs = refs[:nw], refs[nw:2 * nw]
        for cp in _rs_copies(g_refs, land_refs, refs[2 * nw:2 * nw + nsem], chips):
            cp.wait_send()
            cp.wait_recv()

    outs = pl.pallas_call(
        body, name=name,
        out_shape=tuple(pltpu.HBM(a.shape, a.dtype) for a in list(g_thru) + list(land_thru)),
        in_specs=[HBM_ONLY] * (2 * nw) + [SEM] * nsem + [HBM] * len(after),
        out_specs=tuple([HBM_ONLY] * (2 * nw)),
        input_output_aliases={i: i for i in range(2 * nw)},
        compiler_params=pltpu.CompilerParams(has_side_effects=EFFECT),
    )(*g_thru, *land_thru, *sems, *after)
    return outs[:nw], outs[nw:]


def _adamw(w, g, m, v):
    m2 = ADAM_B1 * m + (1.0 - ADAM_B1) * g
    v2 = ADAM_B2 * v + (1.0 - ADAM_B2) * (g * g)
    m_hat = m2 / (1.0 - ADAM_B1 ** ADAM_STEP)
    v_hat = v2 / (1.0 - ADAM_B2 ** ADAM_STEP)
    delta = -ADAM_LR * (m_hat / (jnp.sqrt(v_hat) + ADAM_EPS) + ADAM_WD * w)
    return delta, m2, v2


def _adam_big(r, w, m, v, rb, name, own, after=None, sib=None):
    R, C = w.shape
    ns = r.shape[0]
    g_all, idx1 = own

    def body(idx_ref, r_ref, own_ref, *refs):
        w_ref, m_ref, v_ref, g_ref, d_ref, m2_ref, v2_ref = refs[len(refs) - 7:]
        g = own_ref[0].astype(F32)
        if sib is not None:
            g = g + refs[0][0].astype(F32)
        for k in range(ns):
            g = g + r_ref[k].astype(F32)
        g_ref[...] = g
        d_ref[...], m2_ref[...], v2_ref[...] = _adamw(w_ref[...], g, m_ref[...], v_ref[...])

    t2 = pl.BlockSpec((rb, C), lambda i, idx_ref: (i, 0))
    sd = jax.ShapeDtypeStruct((R, C), F32)
    extra_specs = ([pl.BlockSpec((1, rb, C), lambda i, idx_ref: (3, i, 0))] if sib is not None else []) \
        + ([HBM] if after is not None else [])
    extra = ([sib] if sib is not None else []) + ([after] if after is not None else [])
    return pl.pallas_call(
        body, name=name,
        grid_spec=pltpu.PrefetchScalarGridSpec(
            num_scalar_prefetch=1, grid=(R // rb,),
            in_specs=[pl.BlockSpec((ns, rb, C), lambda i, idx_ref: (0, i, 0)),
                      pl.BlockSpec((1, rb, C), lambda i, idx_ref: (idx_ref[0], i, 0))] + extra_specs + [t2, t2, t2],
            out_specs=[t2, t2, t2, t2]),
        out_shape=[sd, sd, sd, sd],
        compiler_params=pltpu.CompilerParams(dimension_semantics=("arbitrary",), vmem_limit_bytes=VMEM_LIMIT),
    )(idx1, r, g_all, *extra, w, m, v)


def _adam_ada(c_all, parts, rb, name):
    R = parts[0][1].shape[0]
    n = len(parts)

    def body(c_ref, *refs):
        cv = c_ref[...]
        sc = cv * _sig(cv)
        for k in range(n):
            dc_ref, w_ref, m_ref, v_ref = refs[4 * k:4 * k + 4]
            g_ref, d_ref, m2_ref, v2_ref = refs[4 * (n + k):4 * (n + k) + 4]
            g = lax.dot_general(sc, dc_ref[...], (((0,), (0,)), ((), ())), preferred_element_type=F32,
                                precision=lax.Precision.HIGHEST)
            g_ref[...] = g
            d_ref[...], m2_ref[...], v2_ref[...] = _adamw(w_ref[...], g, m_ref[...], v_ref[...])

    in_specs, out_specs, out_shape, flat = [pl.BlockSpec((NDEV, rb), lambda i: (0, i))], [], [], []
    for dcs, w, m, v in parts:
        C = w.shape[1]
        t2 = pl.BlockSpec((rb, C), lambda i: (i, 0))
        in_specs += [_full((NDEV, C)), t2, t2, t2]
        out_specs += [t2] * 4
        out_shape += [jax.ShapeDtypeStruct((R, C), F32)] * 4
        flat += [dcs, w, m, v]
    out = pl.pallas_call(
        body, name=name, grid=(R // rb,),
        in_specs=in_specs, out_specs=out_specs, out_shape=out_shape,
        compiler_params=pltpu.CompilerParams(dimension_semantics=("arbitrary",), vmem_limit_bytes=VMEM_LIMIT),
    )(c_all, *flat)
    return [out[4 * k:4 * k + 4] for k in range(n)]


_SMALL = ["ada_b", "ada_f_b", "norm1_g", "b_in", "a_ln_g", "a_ln_b", "a_spatial_b", "b_conv_b", "b_gn_g", "b_gn_b",
          "out_norm_a_g", "out_norm_b_g", "norm2_g", "norm_f_g", "a_spatial_w", "b_conv_w"]


def _adam_small(vsum, wssum, gcw, params):
    names = _SMALL
    flat = []
    for n in names:
        flat += list(params[n])

    def body(vs_ref, ws_ref, gcw_ref, *rest):
        ins = rest[:3 * len(names)]
        outs = rest[3 * len(names):]
        for pi, n in enumerate(names):
            w_ref, m_ref, v_ref = ins[3 * pi:3 * pi + 3]
            g_ref, d_ref, m2_ref, v2_ref = outs[4 * pi:4 * pi + 4]
            if n in ("ada_b", "ada_f_b", "b_in"):
                row0 = {"ada_b": 0, "ada_f_b": 6, "b_in": 9}[n]
                pieces = [(vs_ref[row0 + r:row0 + r + 1, :], slice(r * D, (r + 1) * D))
                          for r in range(w_ref.shape[1] // D)]
            elif n == "a_spatial_w":
                pieces = [(ws_ref[...], slice(None))]
            elif n == "b_conv_w":
                pieces = [(gcw_ref[...], slice(None))]
            else:
                row, off, width = _VEC_AT[n]
                pieces = [(vs_ref[row:row + 1, off:off + width], slice(None))]
            for g, cs in pieces:
                g_ref[:, cs] = g
                d_ref[:, cs], m2_ref[:, cs], v2_ref[:, cs] = _adamw(w_ref[:, cs], g, m_ref[:, cs], v_ref[:, cs])

    out_shape = []
    for n in names:
        out_shape += [jax.ShapeDtypeStruct(params[n][0].shape, F32)] * 4
    outs = pl.pallas_call(
        body, name="adam_small",
        in_specs=[VM] * (3 + len(flat)), out_specs=[VM] * len(out_shape), out_shape=out_shape,
        compiler_params=pltpu.CompilerParams(vmem_limit_bytes=VMEM_LIMIT),
    )(vsum, wssum, gcw, *flat)
    return {n: outs[4 * pi:4 * pi + 4] for pi, n in enumerate(names)}


def _token_tile(T, want):
    return want if T % want == 0 else T


def kernel(x, c, ada_w, ada_b, norm1_g, w_in, b_in, a_ln_g, a_ln_b, a_spatial_w, a_spatial_b, b_conv_w, b_conv_b, b_gn_g, b_gn_b, out_norm_a_g, out_norm_b_g, w_out, norm2_g, w_ffn_in, w_ffn_out, ada_f_w, ada_f_b, norm_f_g, loss_target, m_ada_w, m_ada_b, m_norm1_g, m_w_in, m_b_in, m_a_ln_g, m_a_ln_b, m_a_spatial_w, m_a_spatial_b, m_b_conv_w, m_b_conv_b, m_b_gn_g, m_b_gn_b, m_out_norm_a_g, m_out_norm_b_g, m_w_out, m_norm2_g, m_w_ffn_in, m_w_ffn_out, m_ada_f_w, m_ada_f_b, m_norm_f_g, v_ada_w, v_ada_b, v_norm1_g, v_w_in, v_b_in, v_a_ln_g, v_a_ln_b, v_a_spatial_w, v_a_spatial_b, v_b_conv_w, v_b_conv_b, v_b_gn_g, v_b_gn_b, v_out_norm_a_g, v_out_norm_b_g, v_w_out, v_norm2_g, v_w_ffn_in, v_w_ffn_out, v_ada_f_w, v_ada_f_b, v_norm_f_g):
    T = x.shape[1]
    idx = 4 * lax.axis_index("x") + 2 * lax.axis_index("y") + lax.axis_index("c")
    x2d = x.reshape(T, D)
    tgt = loss_target.reshape(T, D)

    conv_s = jnp.pad(b_conv_w[0], ((0, HALO - KW), (0, 0)))
    call, cparts, cfparts, convg, (win_g, wout_g) = _gather(
        c, ada_w[0], ada_b.reshape(NDEV, -1), ada_f_w, ada_f_b.reshape(NDEV, -1), conv_s,
        [w_in[0], w_out[0]])
    wout = wout_g.reshape(D, D)
    mod = jnp.concatenate([cparts.reshape(6, D), cfparts.reshape(2, D)], axis=0)
    cw = jnp.transpose(convg, (1, 0, 2)).reshape(HALO, DB)

    tril = jnp.tril(jnp.ones((CH, CH), dtype=bool))
    wsm = jnp.where(tril[None], a_spatial_w[0], 0.0).astype(BF16)
    wcat = wsm.reshape(NH * CH, CH)
    wcat_t = jnp.transpose(wsm, (0, 2, 1)).reshape(NH * CH, CH)
    bsf = jnp.repeat(a_spatial_b[0].T, DA // NH, axis=1)
    pm = jnp.asarray(_GROUP_MEAN, BF16)
    esel = jnp.asarray(_HEAD_SELECT, BF16)

    tm = _token_tile(T, 256)
    tk = _token_tile(T, 2048)
    (x1, hb, zvg, mixed, yb, o, gu, dgelu_u, dgelu_v, vhat, rslb, yhat, rsg), (wfi_g, wfo_g) = _mix_fwd(
        x2d, mod, norm1_g, win_g, b_in, a_ln_g, a_ln_b, wcat, bsf, cw, b_conv_b, b_gn_g, b_gn_b, out_norm_a_g,
        out_norm_b_g, wout, pm, [w_ffn_in[0].T, w_ffn_out[0]], _token_tile(T, 512))
    dx1, h2b, dgu, act, dxg, acc_f = _ffn(x1, tgt, mod, norm2_g, norm_f_g.reshape(1, D),
                                          wfi_g.reshape(2 * DFF, D), wfo_g.reshape(DFF, D), tm)
    g_wfi = _wgrad_rows(dgu, h2b, 2 * WFI_B, tk, "wgrad_ffn_in").reshape(NDEV, WFI_B, D)
    g_wfo = _wgrad_rows(act, dxg, 2 * WFI_B, tk, "wgrad_ffn_out").reshape(NDEV, DFF // NDEV, D)
    f_sems, f_thru, f_land, f_token = _rs_start([g_wfi, g_wfo], "rs_ffn_start")
    (gx, acc_v, acc_b, acc_a, acc_bs, acc_ws, acc_cw), (g_win, g_wout) = _mix_bwd(
        dx1, x2d, zvg, mixed, o, hb, yb, gu, dgelu_u, dgelu_v, vhat, rslb, yhat, rsg, mod, norm1_g, win_g, a_ln_g,
        a_ln_b, wcat, wcat_t, cw, b_gn_g, b_gn_b, out_norm_a_g, out_norm_b_g, wout, pm, esel, f_token, tm)
    (g_wfi_d, g_wfo_d), (r_wfi, r_wfo) = _rs_wait(f_sems, f_thru, f_land, [acc_v], "rs_ffn_wait")
    g_wout = g_wout.reshape(NDEV, D // NDEV, D)

    vsum, dcond_all, wssum, (q_win, q_wout) = _reduce_small(acc_f, acc_v, acc_b, acc_a, acc_bs, acc_cw, acc_ws,
                                                            g_wfi_d, [g_win, g_wout])
    sems, g_thru, land_thru, token = _rs_start([q_win, q_wout], "rs_mix_start", after=(vsum,), chips=True)

    own = lambda g: (g, jnp.reshape(idx, (1,)).astype(jnp.int32))
    res = {}
    upd_wfi_t = _adam_big(r_wfi, w_ffn_in[0].T, m_w_ffn_in[0].T, v_w_ffn_in[0].T, WFI_B // 2, "adam_w_ffn_in",
                          own=own(g_wfi_d), after=token)
    res["w_ffn_in"] = tuple(a.T for a in upd_wfi_t)
    res["w_ffn_out"] = _adam_big(r_wfo, w_ffn_out[0], m_w_ffn_out[0], v_w_ffn_out[0], DFF // NDEV // 2,
                                 "adam_w_ffn_out", own=own(g_wfo_d), after=token)
    dcond = dcond_all.reshape(NDEV, 8 * D)
    nada = ada_w.shape[2]
    nadf = ada_f_w.shape[1]
    dcs = lax.dynamic_slice(dcond, (0, idx * nada), (NDEV, nada))
    dcfs = lax.dynamic_slice(dcond, (0, 6 * D + idx * nadf), (NDEV, nadf))
    res["ada_w"], res["ada_f_w"] = _adam_ada(call, [(dcs, ada_w[0], m_ada_w[0], v_ada_w[0]),
                                                    (dcfs, ada_f_w, m_ada_f_w, v_ada_f_w)], 512, "adam_ada")
    ncw = b_conv_w.shape[2]
    gcw = jnp.concatenate([lax.dynamic_slice(vsum, (_CW_ROW, idx * ncw), (HALO // 2, ncw)),
                           lax.dynamic_slice(vsum, (_CW_ROW, DB + idx * ncw), (HALO // 2, ncw))], axis=0)[:KW]
    two = lambda a: a.reshape(1, -1) if a.ndim == 1 else a.reshape(-1, a.shape[-1])
    small_in = {
        "ada_b": (ada_b, m_ada_b, v_ada_b), "ada_f_b": (ada_f_b, m_ada_f_b, v_ada_f_b),
        "norm1_g": (norm1_g, m_norm1_g, v_norm1_g), "b_in": (b_in, m_b_in, v_b_in),
        "a_ln_g": (a_ln_g, m_a_ln_g, v_a_ln_g), "a_ln_b": (a_ln_b, m_a_ln_b, v_a_ln_b),
        "a_spatial_b": (a_spatial_b.reshape(1, D), m_a_spatial_b.reshape(1, D), v_a_spatial_b.reshape(1, D)),
        "b_conv_b": (b_conv_b, m_b_conv_b, v_b_conv_b), "b_gn_g": (b_gn_g, m_b_gn_g, v_b_gn_g),
        "b_gn_b": (b_gn_b, m_b_gn_b, v_b_gn_b), "out_norm_a_g": (out_norm_a_g, m_out_norm_a_g, v_out_norm_a_g),
        "out_norm_b_g": (out_norm_b_g, m_out_norm_b_g, v_out_norm_b_g),
        "norm2_g": (norm2_g, m_norm2_g, v_norm2_g), "norm_f_g": (norm_f_g, m_norm_f_g, v_norm_f_g),
        "a_spatial_w": (a_spatial_w, m_a_spatial_w, v_a_spatial_w),
        "b_conv_w": (b_conv_w[0], m_b_conv_w[0], v_b_conv_w[0]),
    }
    small_in = {n: tuple(two(a) for a in t) for n, t in small_in.items()}
    res.update(_adam_small(vsum, wssum, gcw, small_in))
    (q_win_d, q_wout_d), (r_win, r_wout) = _rs_wait(
        sems, g_thru, land_thru,
        [upd_wfi_t[0], res["w_ffn_out"][0], res["ada_w"][0], res["ada_f_w"][0], res["norm_f_g"][0]],
        "rs_mix_wait", chips=True)
    res["w_in"] = _adam_big(r_win, w_in[0], m_w_in[0], v_w_in[0], 512, "adam_w_in", own=own(g_win), sib=q_win_d)
    res["w_out"] = _adam_big(r_wout, w_out[0], m_w_out[0], v_w_out[0], D // NDEV, "adam_w_out", own=own(g_wout),
                             sib=q_wout_d)

    loss = 0.5 / D * jnp.sum(vsum[_LOSS_ROW])
    shapes = {"ada_w": ada_w, "ada_b": ada_b, "norm1_g": norm1_g, "w_in": w_in, "b_in": b_in, "a_ln_g": a_ln_g,
              "a_ln_b": a_ln_b, "a_spatial_w": a_spatial_w, "a_spatial_b": a_spatial_b, "b_conv_w": b_conv_w,
              "b_conv_b": b_conv_b, "b_gn_g": b_gn_g, "b_gn_b": b_gn_b, "out_norm_a_g": out_norm_a_g,
              "out_norm_b_g": out_norm_b_g, "w_out": w_out, "norm2_g": norm2_g, "w_ffn_in": w_ffn_in,
              "w_ffn_out": w_ffn_out, "ada_f_w": ada_f_w, "ada_f_b": ada_f_b, "norm_f_g": norm_f_g}
    order = list(shapes)
    outs = [loss, gx.reshape(x.shape)]
    for which in range(4):
        outs += [res[n][which].reshape(shapes[n].shape) for n in order]
    return tuple(outs)
```

```python
import math

import numpy as np

import jax
import jax.numpy as jnp
from jax import lax
from jax.experimental import pallas as pl
from jax.experimental.pallas import tpu as pltpu

F32 = jnp.float32
BF16 = jnp.bfloat16

D = 1024
DA = 512
DB = 512
DIN = 2048
DFF = 2816
NH = 8
CH = 128
KW = 31
HALO = 32
NDEV = 8
WIN_B = DIN // NDEV
WFI_B = 2 * DFF // NDEV
EPS = 1e-6
NVEC = 40
VMEM_LIMIT = 56 * 1024 * 1024

ADAM_LR, ADAM_B1, ADAM_B2, ADAM_EPS, ADAM_WD, ADAM_STEP = 0.001, 0.9, 0.999, 1e-08, 0.01, 10

MESH = pl.DeviceIdType.MESH

_LANE = np.arange(DB)
_GROUP_MEAN = np.where((_LANE[:, None] >> 6) == (_LANE[None, :] >> 6), 1.0 / 64.0, 0.0).astype(np.float32)
_HEAD_SELECT = np.where((_LANE[:, None] >> 6) == np.arange(CH)[None, :], 1.0, 0.0).astype(np.float32)


def _dot(a, b):
    return jnp.dot(a, b, preferred_element_type=F32)


def _dot_nt(a, b):
    return lax.dot_general(a, b, (((1,), (1,)), ((), ())), preferred_element_type=F32)


def _dot_tn(a, b):
    return lax.dot_general(a, b, (((0,), (0,)), ((), ())), preferred_element_type=F32)


def _rs(v):
    return lax.rsqrt(jnp.mean(v * v, axis=-1, keepdims=True) + EPS)


def _sig(v):
    return 1.0 / (1.0 + jnp.exp(-v))


_INV_SQRT2 = 1.0 / math.sqrt(2.0)
_INV_SQRT2PI = 1.0 / math.sqrt(2.0 * math.pi)


def _gelu_parts(v):
    cdf = 0.5 * (1.0 + lax.erf(v * _INV_SQRT2))
    pdf = jnp.exp(-0.5 * v * v) * _INV_SQRT2PI
    return v * cdf, cdf + v * pdf


def _grp_mean(v, pm):
    hi = v.astype(BF16)
    lo = (v - hi.astype(F32)).astype(BF16)
    return _dot(hi, pm) + _dot(lo, pm)


def _colsum(v):
    return jnp.sum(v, axis=0, keepdims=True)


def _full(shape):
    nd = len(shape)
    return pl.BlockSpec(shape, lambda *_: (0,) * nd)


def _resident(shape):
    nd = len(shape)
    return pl.BlockSpec(shape, lambda *_: (0,) * nd, pipeline_mode=pl.Buffered(1))


HBM = pl.BlockSpec(memory_space=pl.ANY)
VM = pl.BlockSpec(memory_space=pltpu.VMEM)


SH_ROWS = HALO - 8


def _shifted_copies(buf, shbuf, tm):
    for b in range(1, 8):
        shbuf[b - 1] = buf[b:b + tm + SH_ROWS, :]


def _window(buf, shbuf, off, tm):
    a, b = divmod(off, 8)
    if b == 0:
        return buf[8 * a:8 * a + tm, :]
    return shbuf[b - 1, 8 * a:8 * a + tm, :]


def _first_head_lanes():
    return lax.broadcasted_iota(jnp.int32, (CH, CH), 1) < (DA // NH)


def _mix_heads(w_ref, vb, first):
    outs = []
    for p in range(NH // 2):
        v = vb[:, p * CH:(p + 1) * CH]
        a = _dot(w_ref[(2 * p) * CH:(2 * p + 1) * CH, :], v)
        b = _dot(w_ref[(2 * p + 1) * CH:(2 * p + 2) * CH, :], v)
        outs.append(jnp.where(first, a, b))
    return jnp.concatenate(outs, axis=1)


def _place():
    x, y, c = lax.axis_index("x"), lax.axis_index("y"), lax.axis_index("c")
    return x, y, c, 4 * x + 2 * y + c


def _dev(t):
    return (t >> 2, (t >> 1) & 1, t & 1)


class _AllGather:
    def __init__(self, w_in, w_out, wss, wrs, lsem):
        x, y, c, idx = _place()
        me, sibling = (x, y, c), (x, y, 1 - c)
        chips = [(1 - x, y), (x, 1 - y), (1 - x, 1 - y)]
        nw = len(w_in)

        def blk(p):
            return 4 * p[0] + 2 * p[1] + p[2]

        def wcopy(a, k, block, to, src=None):
            dst = w_out[a].at[blk(block)]
            return pltpu.make_async_remote_copy(src_ref=dst if src is None else src, dst_ref=dst,
                                                send_sem=wss.at[a, k], recv_sem=wrs.at[a, k],
                                                device_id=to, device_id_type=MESH)

        self.mine = [pltpu.make_async_copy(w_in[a], w_out[a].at[idx], lsem.at[a]) for a in range(nw)]
        self.first = []
        for a in range(nw):
            self.first.append(wcopy(a, 0, me, sibling, src=w_in[a]))
            self.first += [wcopy(a, 1 + j, me, (*chip, c), src=w_in[a]) for j, chip in enumerate(chips)]
        self.landed = [[wcopy(a, 1 + j, (*chip, c), me) for a in range(nw)] for j, chip in enumerate(chips)]
        self.passed = [[wcopy(a, 4 + j, (*chip, c), sibling) for a in range(nw)] for j, chip in enumerate(chips)]
        self.from_sibling = []
        for a in range(nw):
            self.from_sibling.append(wcopy(a, 0, sibling, me))
            self.from_sibling += [wcopy(a, 4 + j, (*chip, 1 - c), me) for j, chip in enumerate(chips)]

    def start(self):
        for cp in self.mine + self.first:
            cp.start()

    def forward(self):
        for land, pas in zip(self.landed, self.passed):
            for l, p in zip(land, pas):
                l.wait_recv()
                p.start()

    def finish(self):
        for cp in self.from_sibling:
            cp.wait_recv()
        for cp in self.first:
            cp.wait_send()
        for pas in self.passed:
            for p in pas:
                p.wait_send()
        for cp in self.mine:
            cp.wait()


AG_SEMS = lambda nw: [pltpu.SemaphoreType.DMA((nw, 7)), pltpu.SemaphoreType.DMA((nw, 7)),
                      pltpu.SemaphoreType.DMA((nw,))]


def _mix_fwd(x, mod, g1, win, b_in, lng, lnb, wcat, bsf, cw, cb, gng, gnb, oga, ogb, wout, pm, ffn_shards, tm):
    T = x.shape[0]
    nt = T // tm
    nch = tm // CH
    nw = len(ffn_shards)
    fwd_step = (5 * nt) // 8
    saved = [(D, F32), (D, BF16), (2 * DB, F32), (DA, F32), (D, BF16), (D, F32), (DA, F32), (DA, F32), (DA, F32),
             (DA, F32), (CH, F32), (DB, F32), (DB, F32)]
    NSAVE = len(saved)

    def body(x_ref, mod_ref, g1_ref, win_ref, bin_ref, lng_ref, lnb_ref, wcat_ref, bsf_ref, cw_ref, cb_ref,
             gng_ref, gnb_ref, oga_ref, ogb_ref, wout_ref, pm_ref, *rest):
        sh_f32 = rest[:nw]
        (x1_ref, h_ref, zvg_ref, mixed_ref, y_ref, o_ref, gu_ref, dgu_ref, dgv_ref, vhat_ref, rsl_ref, yhat_ref,
         rsg_ref) = rest[nw:nw + NSAVE]
        sh_out = rest[nw + NSAVE:2 * nw + NSAVE]
        glbuf, shbuf = rest[2 * nw + NSAVE:2 * nw + NSAVE + 2]
        sh_in = rest[2 * nw + NSAVE + 2:3 * nw + NSAVE + 2]
        wss, wrs, lsem = rest[3 * nw + NSAVE + 2:]
        i = pl.program_id(0)

        @pl.when(i == 0)
        def _():
            for a in range(nw):
                sh_in[a][...] = sh_f32[a][...].astype(BF16)
            _AllGather(sh_in, sh_out, wss, wrs, lsem).start()

        xv = x_ref[...]
        shift1 = mod_ref[0:1, :]
        scale1 = mod_ref[1:2, :]
        gate1 = mod_ref[2:3, :]
        h = (xv * _rs(xv) * g1_ref[...]) * (1.0 + scale1) + shift1
        hb = h.astype(BF16)
        h_ref[...] = hb
        z = jnp.concatenate([_dot(hb, win_ref[j]) for j in range(NDEV)], axis=1) + bin_ref[...]
        zvg_ref[...] = z[:, 2 * DA:]
        gu, dgelu_u = _gelu_parts(z[:, 0:DA])
        gv, dgelu_v = _gelu_parts(z[:, DA:2 * DA])
        gu_ref[...] = gu
        dgu_ref[...] = dgelu_u
        dgv_ref[...] = dgelu_v
        xc = gv - jnp.mean(gv, axis=-1, keepdims=True)
        rsl = lax.rsqrt(jnp.mean(xc * xc, axis=-1, keepdims=True) + EPS)
        vhat = xc * rsl
        vhat_ref[...] = vhat
        rsl_ref[...] = jnp.broadcast_to(rsl, (tm, CH))
        vnb = (vhat * lng_ref[...] + lnb_ref[...]).astype(BF16)
        first = _first_head_lanes()
        chunks = []
        for ci in range(nch):
            chunks.append(_mix_heads(wcat_ref, vnb[ci * CH:(ci + 1) * CH, :], first) + bsf_ref[...])
        mixed = jnp.concatenate(chunks, axis=0) if nch > 1 else chunks[0]
        mixed_ref[...] = mixed
        ya = gu * mixed
        gl = z[:, 2 * DA:2 * DA + DB] * _sig(z[:, 2 * DA + DB:])

        @pl.when(i == 0)
        def _():
            glbuf[0:HALO, :] = jnp.zeros((HALO, DB), F32)

        glbuf[HALO:HALO + tm, :] = gl
        _shifted_copies(glbuf, shbuf, tm)
        yc = jnp.zeros((tm, DB), F32) + cb_ref[...]
        for k in range(KW):
            yc = yc + cw_ref[k:k + 1, :] * _window(glbuf, shbuf, HALO - (KW - 1) + k, tm)
        glbuf[0:HALO, :] = gl[tm - HALO:, :]
        pmv = pm_ref[...]
        dc = yc - _grp_mean(yc, pmv)
        rsg = lax.rsqrt(_grp_mean(dc * dc, pmv) + EPS)
        yhat = dc * rsg
        yhat_ref[...] = yhat
        rsg_ref[...] = rsg
        yg = yhat * gng_ref[...] + gnb_ref[...]
        yb = yg * _sig(yg)
        na = ya * _rs(ya) * oga_ref[...]
        nb = yb * _rs(yb) * ogb_ref[...]
        yv = jnp.concatenate([na, nb], axis=1).astype(BF16)
        y_ref[...] = yv
        o = _dot(yv, wout_ref[...])
        o_ref[...] = o
        x1_ref[...] = xv + gate1 * o

        @pl.when(i == fwd_step)
        def _():
            _AllGather(sh_in, sh_out, wss, wrs, lsem).forward()

        @pl.when(i == nt - 1)
        def _():
            _AllGather(sh_in, sh_out, wss, wrs, lsem).finish()

    tile = lambda w: pl.BlockSpec((tm, w), lambda i: (i, 0))
    outs = pl.pallas_call(
        body,
        name="mix_fwd",
        grid=(nt,),
        in_specs=[tile(D), _full((8, D)), _full((1, D)), _resident((NDEV, D, WIN_B)), _full((1, DIN)),
                  _full((1, DA)), _full((1, DA)), _full((NH * CH, CH)), _full((CH, DA)), _full((HALO, DB)),
                  _full((1, DB)), _full((1, DB)), _full((1, DB)), _full((1, DA)), _full((1, DB)),
                  _resident((D, D)), _full((DB, DB))] + [_resident(s.shape) for s in ffn_shards],
        out_specs=[tile(w) for w, _ in saved] + [HBM] * nw,
        out_shape=[jax.ShapeDtypeStruct((T, w), dt) for w, dt in saved]
                  + [jax.ShapeDtypeStruct((NDEV,) + s.shape, BF16) for s in ffn_shards],
        scratch_shapes=[pltpu.VMEM((HALO + tm, DB), F32), pltpu.VMEM((7, tm + SH_ROWS, DB), F32)]
                       + [pltpu.VMEM(s.shape, BF16) for s in ffn_shards] + AG_SEMS(nw),
        compiler_params=pltpu.CompilerParams(dimension_semantics=("arbitrary",), vmem_limit_bytes=VMEM_LIMIT),
    )(x, mod, g1, win, b_in, lng, lnb, wcat, bsf, cw, cb, gng, gnb, oga, ogb, wout, pm, *ffn_shards)
    return outs[:NSAVE], outs[NSAVE:]


FF_BLOCKS = ((0, 1024), (1024, 1024), (2048, 768))


def _ffn(x1, tgt, mod, g2, gf, wfi_t, wfo, tm):
    T = x1.shape[0]
    nt = T // tm

    def body(x1_ref, tgt_ref, mod_ref, g2_ref, gf_ref, wfi_ref, wfo_ref,
             dx1_ref, h2_ref, dgu_ref, act_ref, dxg_ref, acc_ref, g_s, u_s):
        i = pl.program_id(0)

        @pl.when(i == 0)
        def _():
            acc_ref[...] = jnp.zeros((8, D), F32)

        x1 = x1_ref[...]
        shift2 = mod_ref[3:4, :]
        scale2 = mod_ref[4:5, :]
        gate2 = mod_ref[5:6, :]
        shiftf = mod_ref[6:7, :]
        scalef = mod_ref[7:8, :]
        g2v = g2_ref[...]
        gfv = gf_ref[...]
        r2 = _rs(x1)
        xn2 = x1 * r2
        h2b = (xn2 * g2v * (1.0 + scale2) + shift2).astype(BF16)
        h2_ref[...] = h2b
        f = jnp.zeros((tm, D), F32)
        for o, w in FF_BLOCKS:
            g = _dot_nt(h2b, wfi_ref[o:o + w, :])
            u = _dot_nt(h2b, wfi_ref[DFF + o:DFF + o + w, :])
            g_s[:, o:o + w] = g
            u_s[:, o:o + w] = u
            actb = (g * _sig(g) * u).astype(BF16)
            act_ref[:, o:o + w] = actb
            f = f + _dot(actb, wfo_ref[o:o + w, :])
        x2 = x1 + gate2 * f
        rf = _rs(x2)
        xnf = x2 * rf
        out = xnf * gfv * (1.0 + scalef) + shiftf
        e = out - tgt_ref[...]
        dout = e * (1.0 / D)
        acc_ref[7:8, :] += _colsum(e * e)
        acc_ref[0:1, :] += _colsum(dout)
        acc_ref[1:2, :] += _colsum(dout * xnf * gfv)
        acc_ref[2:3, :] += _colsum(dout * (1.0 + scalef) * xnf)
        dxnf = dout * (1.0 + scalef) * gfv
        dx2 = rf * (dxnf - xnf * jnp.mean(dxnf * xnf, axis=-1, keepdims=True))
        acc_ref[3:4, :] += _colsum(dx2 * f)
        dxgb = (dx2 * gate2).astype(BF16)
        dxg_ref[...] = dxgb
        dh2 = jnp.zeros((tm, D), F32)
        for o, w in FF_BLOCKS:
            dact = _dot_nt(dxgb, wfo_ref[o:o + w, :])
            g = g_s[:, o:o + w]
            u = u_s[:, o:o + w]
            s = _sig(g)
            dgb = (dact * u * (s * (1.0 + g * (1.0 - s)))).astype(BF16)
            dub = (dact * (g * s)).astype(BF16)
            dgu_ref[:, o:o + w] = dgb
            dgu_ref[:, DFF + o:DFF + o + w] = dub
            dh2 = dh2 + _dot(dgb, wfi_ref[o:o + w, :])
            dh2 = dh2 + _dot(dub, wfi_ref[DFF + o:DFF + o + w, :])
        acc_ref[4:5, :] += _colsum(dh2)
        acc_ref[5:6, :] += _colsum(dh2 * xn2 * g2v)
        acc_ref[6:7, :] += _colsum(dh2 * (1.0 + scale2) * xn2)
        dxn2 = dh2 * (1.0 + scale2) * g2v
        dx1_ref[...] = dx2 + r2 * (dxn2 - xn2 * jnp.mean(dxn2 * xn2, axis=-1, keepdims=True))

    tile = lambda w: pl.BlockSpec((tm, w), lambda i: (i, 0))
    return pl.pallas_call(
        body,
        name="ffn_fwd_bwd",
        grid=(nt,),
        in_specs=[tile(D), tile(D), _full((8, D)), _full((1, D)), _full((1, D)),
                  _resident((2 * DFF, D)), _resident((DFF, D))],
        out_specs=[tile(D), tile(D), tile(2 * DFF), tile(DFF), tile(D), _full((8, D))],
        out_shape=[jax.ShapeDtypeStruct((T, D), F32), jax.ShapeDtypeStruct((T, D), BF16),
                   jax.ShapeDtypeStruct((T, 2 * DFF), BF16), jax.ShapeDtypeStruct((T, DFF), BF16),
                   jax.ShapeDtypeStruct((T, D), BF16), jax.ShapeDtypeStruct((8, D), F32)],
        scratch_shapes=[pltpu.VMEM((tm, DFF), F32), pltpu.VMEM((tm, DFF), F32)],
        compiler_params=pltpu.CompilerParams(dimension_semantics=("arbitrary",), vmem_limit_bytes=VMEM_LIMIT),
    )(x1, tgt, mod, g2, gf, wfi_t, wfo)


def _mix_bwd(dx1, x, zvg, mixed, o, hb, yb, gu, dgu, dgv, vhat, rslb, yhat, rsg, mod, g1, win, lng, lnb, wcat, wcat_t,
             cw, gng, gnb, oga, ogb, wout, pm, esel, after, tm):
    T = x.shape[0]
    nt = T // tm
    nch = tm // CH
    WOB = 256

    def body(dx1_ref, x_ref, zvg_ref, mixed_ref, o_ref, hb_ref, yb_ref, gu_ref, dgu_ref, dgv_ref, vhat_ref, rsl_ref,
             yhat_ref, rsg_ref, mod_ref, g1_ref, win_ref, lng_ref, lnb_ref, wcat_ref, wcatt_ref, cw_ref, gng_ref,
             gnb_ref, oga_ref, ogb_ref, wout_ref, pm_ref, esel_ref, after_ref,
             gx_ref, accv_ref, accb_ref, acca_ref, accbs_ref, accws_ref, acccw_ref, gwin_ref, gwout_ref,
             dycbuf, shbuf, bs_s, acc_win, acc_wout, st_win, st_wout):
        i = pl.program_id(0)

        @pl.when(i == 0)
        def _():
            acc_win[...] = jnp.zeros((NDEV, D, WIN_B), F32)
            acc_wout[...] = jnp.zeros((D, D), F32)
            accv_ref[...] = jnp.zeros((8, D), F32)
            accb_ref[...] = jnp.zeros((1, DIN), F32)
            acca_ref[...] = jnp.zeros((8, DA), F32)
            accws_ref[...] = jnp.zeros((NH * CH, CH), F32)
            acccw_ref[...] = jnp.zeros((HALO, DB), F32)
            bs_s[...] = jnp.zeros((CH, DA), F32)
            dycbuf[tm:tm + HALO, :] = jnp.zeros((HALO, DB), F32)

        shift1 = mod_ref[0:1, :]
        scale1 = mod_ref[1:2, :]
        gate1 = mod_ref[2:3, :]
        g1v = g1_ref[...]
        xv = x_ref[...]
        r1 = _rs(xv)
        xn1 = xv * r1
        val = zvg_ref[:, 0:DB]
        gate = zvg_ref[:, DB:]
        gu = gu_ref[...]
        dgelu_u = dgu_ref[...]
        dgelu_v = dgv_ref[...]
        vhat = vhat_ref[...]
        rsl = rsl_ref[:, 0:1]
        lngv = lng_ref[...]
        vnb = (vhat * lngv + lnb_ref[...]).astype(BF16)
        mixed = mixed_ref[...]
        ya = gu * mixed
        ra = _rs(ya)
        yan = ya * ra
        sgt = _sig(gate)
        gl = val * sgt
        pmv = pm_ref[...]
        rsg = rsg_ref[...]
        yhat = yhat_ref[...]
        gngv = gng_ref[...]
        yg = yhat * gngv + gnb_ref[...]
        sgy = _sig(yg)
        yb = yg * sgy
        rb = _rs(yb)
        ybn = yb * rb
        dx1 = dx1_ref[...]
        accv_ref[0:1, :] += _colsum(dx1 * o_ref[...])
        dogb = (dx1 * gate1).astype(BF16)
        acc_wout[...] += _dot_tn(yb_ref[...], dogb)
        dy = _dot_nt(dogb, wout_ref[...])
        dna = dy[:, 0:DA]
        dnb = dy[:, DA:]
        ogav = oga_ref[...]
        ogbv = ogb_ref[...]
        acca_ref[2:3, :] += _colsum(dna * yan)
        acca_ref[3:4, :] += _colsum(dnb * ybn)
        ta = dna * ogav
        dya = ra * (ta - yan * jnp.mean(ta * yan, axis=-1, keepdims=True))
        tb = dnb * ogbv
        dyb = rb * (tb - ybn * jnp.mean(tb * ybn, axis=-1, keepdims=True))
        dgu = dya * mixed
        dm = dya * gu
        first = _first_head_lanes()
        zero = jnp.zeros((CH, CH), BF16)
        dvn_chunks = []
        bs_acc = bs_s[...]
        for ci in range(nch):
            dmc = dm[ci * CH:(ci + 1) * CH, :]
            bs_acc = bs_acc + dmc
            dmcb = dmc.astype(BF16)
            dvn_chunks.append(_mix_heads(wcatt_ref, dmcb, first))
            vc = vnb[ci * CH:(ci + 1) * CH, :]
            for p in range(NH // 2):
                xt = dmcb[:, p * CH:(p + 1) * CH]
                vt = vc[:, p * CH:(p + 1) * CH]
                accws_ref[(2 * p) * CH:(2 * p + 1) * CH, :] += _dot_nt(jnp.where(first, xt, zero), vt)
                accws_ref[(2 * p + 1) * CH:(2 * p + 2) * CH, :] += _dot_nt(jnp.where(first, zero, xt), vt)
        bs_s[...] = bs_acc
        dvn = jnp.concatenate(dvn_chunks, axis=0) if nch > 1 else dvn_chunks[0]
        acca_ref[0:1, :] += _colsum(dvn * vhat)
        acca_ref[1:2, :] += _colsum(dvn)
        dvh = dvn * lngv
        dgv = rsl * (dvh - jnp.mean(dvh, axis=-1, keepdims=True)
                     - vhat * jnp.mean(dvh * vhat, axis=-1, keepdims=True))
        du = dgu * dgelu_u
        dv = dgv * dgelu_v
        dyg = dyb * (sgy * (1.0 + yg * (1.0 - sgy)))
        acca_ref[5:6, :] += _colsum(dyg * yhat)
        acca_ref[6:7, :] += _colsum(dyg)
        dyh = dyg * gngv
        dyc = rsg * (dyh - _grp_mean(dyh, pmv) - yhat * _grp_mean(dyh * yhat, pmv))
        acca_ref[4:5, :] += _colsum(dyc)
        dycbuf[0:tm, :] = dyc
        _shifted_copies(dycbuf, shbuf, tm)
        dgl = jnp.zeros((tm, DB), F32)
        for k in range(KW):
            win_k = _window(dycbuf, shbuf, KW - 1 - k, tm)
            dgl = dgl + cw_ref[k:k + 1, :] * win_k
            acccw_ref[k:k + 1, :] += _colsum(win_k * gl)
        dycbuf[tm:tm + HALO, :] = dyc[0:HALO, :]
        dval = dgl * sgt
        dgate = dgl * val * sgt * (1.0 - sgt)
        dz = jnp.concatenate([du, dv, dval, dgate], axis=1)
        accb_ref[...] += _colsum(dz)
        dzb = dz.astype(BF16)
        hbv = hb_ref[...]
        dh = jnp.zeros((tm, D), F32)
        for j in range(NDEV):
            dzj = dzb[:, j * WIN_B:(j + 1) * WIN_B]
            acc_win[j] += _dot_tn(hbv, dzj)
            dh = dh + _dot_nt(dzj, win_ref[j])
        accv_ref[1:2, :] += _colsum(dh)
        dh_xn = _colsum(dh * xn1)
        accv_ref[2:3, :] += dh_xn * g1v
        accv_ref[3:4, :] += dh_xn * (1.0 + scale1)
        dxn1 = dh * (1.0 + scale1) * g1v
        gx_ref[...] = dx1 + r1 * (dxn1 - xn1 * jnp.mean(dxn1 * xn1, axis=-1, keepdims=True))

        @pl.when(i == nt - 1)
        def _():
            rows = lax.broadcasted_iota(jnp.int32, (NH * CH, CH), 0) & (CH - 1)
            cols = lax.broadcasted_iota(jnp.int32, (NH * CH, CH), 1)
            accws_ref[...] = jnp.where(cols <= rows, accws_ref[...], 0.0)
            bs = bs_s[...]
            hi = bs.astype(BF16)
            r1_ = bs - hi.astype(F32)
            mid = r1_.astype(BF16)
            lo = (r1_ - mid.astype(F32)).astype(BF16)
            ev = esel_ref[...]
            accbs_ref[...] = _dot(hi, ev) + _dot(mid, ev) + _dot(lo, ev)
            for j in range(NDEV):
                st_win[...] = acc_win[j].astype(BF16)
                pltpu.sync_copy(st_win, gwin_ref.at[j])
            for j in range(D // WOB):
                st_wout[...] = acc_wout[j * WOB:(j + 1) * WOB, :].astype(BF16)
                pltpu.sync_copy(st_wout, gwout_ref.at[pl.ds(j * WOB, WOB)])

    rev = lambda w: pl.BlockSpec((tm, w), lambda i: (nt - 1 - i, 0))
    outs = pl.pallas_call(
        body,
        name="mix_bwd",
        grid=(nt,),
        in_specs=[rev(D), rev(D), rev(2 * DB), rev(DA), rev(D), rev(D), rev(D), rev(DA), rev(DA), rev(DA), rev(DA),
                  rev(CH), rev(DB), rev(DB), _full((8, D)), _full((1, D)),
                  _resident((NDEV, D, WIN_B)), _full((1, DA)), _full((1, DA)), _full((NH * CH, CH)),
                  _full((NH * CH, CH)), _full((HALO, DB)), _full((1, DB)), _full((1, DB)), _full((1, DA)),
                  _full((1, DB)), _resident((D, D)), _full((DB, DB)), _full((DA, CH)), HBM],
        out_specs=[rev(D), _full((8, D)), _full((1, DIN)), _full((8, DA)), _full((CH, CH)),
                   _full((NH * CH, CH)), _full((HALO, DB)), HBM, HBM],
        out_shape=[jax.ShapeDtypeStruct((T, D), F32), jax.ShapeDtypeStruct((8, D), F32),
                   jax.ShapeDtypeStruct((1, DIN), F32), jax.ShapeDtypeStruct((8, DA), F32),
                   jax.ShapeDtypeStruct((CH, CH), F32), jax.ShapeDtypeStruct((NH * CH, CH), F32),
                   jax.ShapeDtypeStruct((HALO, DB), F32),
                   jax.ShapeDtypeStruct((NDEV, D, WIN_B), BF16), jax.ShapeDtypeStruct((D, D), BF16)],
        scratch_shapes=[pltpu.VMEM((tm + HALO, DB), F32), pltpu.VMEM((7, tm + SH_ROWS, DB), F32),
                        pltpu.VMEM((CH, DA), F32), pltpu.VMEM((NDEV, D, WIN_B), F32), pltpu.VMEM((D, D), F32),
                        pltpu.VMEM((D, WIN_B), BF16), pltpu.VMEM((WOB, D), BF16)],
        compiler_params=pltpu.CompilerParams(dimension_semantics=("arbitrary",), vmem_limit_bytes=VMEM_LIMIT),
    )(dx1, x, zvg, mixed, o, hb, yb, gu, dgu, dgv, vhat, rslb, yhat, rsg, mod, g1, win, lng, lnb, wcat, wcat_t, cw,
      gng, gnb, oga, ogb, wout, pm, esel, after)
    return outs[:7], outs[7:]


def _wgrad_rows(a, b, bm, tk, name):
    T, M = a.shape
    N = b.shape[1]
    nk = T // tk

    def body(a_ref, b_ref, o_ref, acc):
        k = pl.program_id(1)

        @pl.when(k == 0)
        def _():
            acc[...] = jnp.zeros((bm, N), F32)

        acc[...] += _dot_tn(a_ref[...], b_ref[...])

        @pl.when(k == nk - 1)
        def _():
            o_ref[...] = acc[...].astype(BF16)

    return pl.pallas_call(
        body, name=name, grid=(M // bm, nk),
        in_specs=[pl.BlockSpec((tk, bm), lambda j, k: (k, j)), pl.BlockSpec((tk, N), lambda j, k: (k, 0))],
        out_specs=pl.BlockSpec((bm, N), lambda j, k: (j, 0)),
        out_shape=jax.ShapeDtypeStruct((M, N), BF16),
        scratch_shapes=[pltpu.VMEM((bm, N), F32)],
        compiler_params=pltpu.CompilerParams(dimension_semantics=("arbitrary", "arbitrary"),
                                             vmem_limit_bytes=VMEM_LIMIT),
    )(a, b)


def _small_copy(src, dst, ss, rs, k, to):
    return pltpu.make_async_remote_copy(src_ref=src, dst_ref=dst, send_sem=ss.at[k], recv_sem=rs.at[k],
                                        device_id=to, device_id_type=MESH)


def _gather(c_row, ada_w, ada_b8, ada_f_w, ada_f_b8, conv_s, shards):
    nw = len(shards)

    def body(c_ref, adaw_ref, adab_ref, adafw_ref, adafb_ref, conv_ref, *rest):
        w_f32 = rest[:nw]
        call_ref, cparts_ref, cfparts_ref, convg_ref = rest[nw:nw + 4]
        w_out = rest[nw + 4:2 * nw + 4]
        part_s, partf_s = rest[2 * nw + 4:2 * nw + 6]
        w_in = rest[2 * nw + 6:3 * nw + 6]
        wss, wrs, lsem, s1, r1, s2, r2, s3, r3, s4, r4 = rest[3 * nw + 6:]
        x, y, c, idx = _place()
        me = (x, y, c)
        for a in range(nw):
            w_in[a][...] = w_f32[a][...].astype(BF16)
        ag = _AllGather(w_in, w_out, wss, wrs, lsem)
        ag.start()
        call_ref[pl.ds(idx, 1), :] = c_ref[...]
        convg_ref[idx] = conv_ref[...]
        ph1 = []
        for k in range(1, NDEV):
            to = _dev(idx ^ k)
            ph1.append(_small_copy(c_ref, call_ref.at[pl.ds(idx, 1)], s1, r1, k - 1, to))
            ph1.append(_small_copy(conv_ref, convg_ref.at[idx], s2, r2, k - 1, to))
        for cp in ph1:
            cp.start()
        for k in range(1, NDEV):
            src_dev = idx ^ k
            _small_copy(c_ref, call_ref.at[pl.ds(src_dev, 1)], s1, r1, k - 1, me).wait_recv()
            _small_copy(conv_ref, convg_ref.at[src_dev], s2, r2, k - 1, me).wait_recv()
        call = call_ref[...]
        cact = (call * _sig(call))
        part_s[...] = jnp.dot(cact, adaw_ref[...], preferred_element_type=F32,
                              precision=lax.Precision.HIGHEST) + adab_ref[pl.ds(idx, 1), :]
        partf_s[...] = jnp.dot(cact, adafw_ref[...], preferred_element_type=F32,
                               precision=lax.Precision.HIGHEST) + adafb_ref[pl.ds(idx, 1), :]
        cparts_ref[pl.ds(idx, 1), :] = part_s[pl.ds(idx, 1), :]
        cfparts_ref[pl.ds(idx, 1), :] = partf_s[pl.ds(idx, 1), :]
        ph2 = []
        for k in range(1, NDEV):
            t = idx ^ k
            ph2.append(_small_copy(part_s.at[pl.ds(t, 1)], cparts_ref.at[pl.ds(idx, 1)], s3, r3, k - 1, _dev(t)))
            ph2.append(_small_copy(partf_s.at[pl.ds(t, 1)], cfparts_ref.at[pl.ds(idx, 1)], s4, r4, k - 1, _dev(t)))
        for cp in ph2:
            cp.start()
        for k in range(1, NDEV):
            src_dev = idx ^ k
            _small_copy(part_s.at[pl.ds(0, 1)], cparts_ref.at[pl.ds(src_dev, 1)], s3, r3, k - 1, me).wait_recv()
            _small_copy(partf_s.at[pl.ds(0, 1)], cfparts_ref.at[pl.ds(src_dev, 1)], s4, r4, k - 1, me).wait_recv()
        for cp in ph1 + ph2:
            cp.wait_send()
        ag.forward()
        ag.finish()

    dma7 = pltpu.SemaphoreType.DMA((NDEV - 1,))
    outs = pl.pallas_call(
        body,
        name="gather_weights",
        in_specs=[VM] * (6 + nw),
        out_specs=[VM] * 4 + [HBM] * nw,
        out_shape=[jax.ShapeDtypeStruct((NDEV, D), F32), jax.ShapeDtypeStruct((NDEV, ada_w.shape[1]), F32),
                   jax.ShapeDtypeStruct((NDEV, ada_f_w.shape[1]), F32),
                   jax.ShapeDtypeStruct((NDEV,) + conv_s.shape, F32)]
                  + [jax.ShapeDtypeStruct((NDEV,) + s.shape, BF16) for s in shards],
        scratch_shapes=[pltpu.VMEM((NDEV, ada_w.shape[1]), F32), pltpu.VMEM((NDEV, ada_f_w.shape[1]), F32)]
                       + [pltpu.VMEM(s.shape, BF16) for s in shards] + AG_SEMS(nw) + [dma7] * 8,
        compiler_params=pltpu.CompilerParams(vmem_limit_bytes=VMEM_LIMIT),
    )(c_row, ada_w, ada_b8, ada_f_w, ada_f_b8, conv_s, *shards)
    return outs[0], outs[1], outs[2], outs[3], outs[4:]


_VEC_AT = {
    "norm1_g": (8, 0, D), "a_ln_g": (11, 0, DA), "a_ln_b": (11, DA, DA), "a_spatial_b": (12, 0, D),
    "b_conv_b": (13, 0, DB), "b_gn_g": (13, DB, DB), "b_gn_b": (14, 0, DB), "out_norm_a_g": (14, DB, DA),
    "out_norm_b_g": (15, 0, DB), "norm2_g": (16, 0, D), "norm_f_g": (17, 0, D),
}
_LOSS_ROW = 18
_CW_ROW = 24


def _reduce_small(acc_f, acc_v, acc_b, acc_a, acc_bs, acc_cw, dws, after, pair_grads):
    npg = len(pair_grads)

    def body(accf_ref, accv_ref, accb_ref, acca_ref, accbs_ref, acccw_ref, dws_ref, after_ref, *rest):
        pg = rest[:npg]
        vsum_ref, dcond_ref, wssum_ref = rest[npg:npg + 3]
        pq = rest[npg + 3:2 * npg + 3]
        vloc, vbuf, wbuf, wown, vss, vrs, vls, s2, r2, s3, r3 = rest[2 * npg + 3:2 * npg + 14]
        pland = rest[2 * npg + 14:3 * npg + 14]
        pstage = rest[3 * npg + 14:4 * npg + 14]
        ps, pr, pls, pss = rest[4 * npg + 14:]
        x, y, c, idx = _place()
        me = (x, y, c)
        sibling = (x, y, 1 - c)
        chips = [(1 - x, y), (x, 1 - y), (1 - x, 1 - y)]
        blk = lambda p: 4 * p[0] + 2 * p[1] + p[2]
        give = [blk((*ch, 1 - c)) for ch in chips] + [blk(sibling)]
        pair = [pltpu.make_async_remote_copy(src_ref=pg[a].at[b], dst_ref=pland[a].at[j], send_sem=ps.at[a, j],
                                             recv_sem=pr.at[a, j], device_id=sibling, device_id_type=MESH)
                for a in range(npg) for j, b in enumerate(give)]
        loads = [pltpu.make_async_copy(pg[a].at[blk((*ch, c))], pstage[a].at[j], pls.at[a, j])
                 for a in range(npg) for j, ch in enumerate(chips)]
        for cp in pair + loads:
            cp.start()
        vloc[...] = jnp.zeros((NVEC, D), F32)
        vloc[0:1, :] = accv_ref[1:2, :]
        vloc[1:2, :] = accv_ref[2:3, :]
        vloc[2:3, :] = accv_ref[0:1, :]
        vloc[3:4, :] = accf_ref[4:5, :]
        vloc[4:5, :] = accf_ref[5:6, :]
        vloc[5:6, :] = accf_ref[3:4, :]
        vloc[6:7, :] = accf_ref[0:1, :]
        vloc[7:8, :] = accf_ref[1:2, :]
        vloc[8:9, :] = accv_ref[3:4, :]
        vloc[9:10, :] = accb_ref[:, 0:D]
        vloc[10:11, :] = accb_ref[:, D:]
        vloc[11:12, 0:DA] = acca_ref[0:1, :]
        vloc[11:12, DA:] = acca_ref[1:2, :]
        bst = accbs_ref[...].T
        for h in range(NH):
            vloc[12:13, h * CH:(h + 1) * CH] = bst[h:h + 1, :]
        vloc[13:14, 0:DB] = acca_ref[4:5, :]
        vloc[13:14, DB:] = acca_ref[5:6, :]
        vloc[14:15, 0:DB] = acca_ref[6:7, :]
        vloc[14:15, DB:] = acca_ref[2:3, :]
        vloc[15:16, 0:DB] = acca_ref[3:4, :]
        vloc[16:17, :] = accf_ref[6:7, :]
        vloc[17:18, :] = accf_ref[2:3, :]
        vloc[_LOSS_ROW:_LOSS_ROW + 1, :] = accf_ref[7:8, :]
        vloc[_CW_ROW:_CW_ROW + HALO // 2, 0:DB] = acccw_ref[0:HALO // 2, :]
        vloc[_CW_ROW:_CW_ROW + HALO // 2, DB:] = acccw_ref[HALO // 2:, :]
        gather_vecs = _AllGather([vloc], [vbuf], vss, vrs, vls)
        gather_vecs.start()
        rows_of = lambda t: pl.ds(pl.multiple_of(t * CH, CH), CH)
        wbuf[0] = dws_ref[rows_of(idx), :]
        sm = []
        for k in range(1, NDEV):
            t = idx ^ k
            sm.append(_small_copy(dws_ref.at[rows_of(t)], wbuf.at[k], s2, r2, k - 1, _dev(t)))
        for cp in sm:
            cp.start()
        for k in range(1, NDEV):
            _small_copy(dws_ref.at[rows_of(0)], wbuf.at[k], s2, r2, k - 1, me).wait_recv()
        ws = wbuf[0]
        for k in range(1, NDEV):
            ws = ws + wbuf[k]
        wown[...] = ws
        wssum_ref[rows_of(idx), :] = ws
        ag = [_small_copy(wown, wssum_ref.at[rows_of(idx)], s3, r3, k - 1, _dev(idx ^ k)) for k in range(1, NDEV)]
        for cp in ag:
            cp.start()
        for cp in loads:
            cp.wait()
        for cp in pair:
            cp.wait_recv()
        stores = []
        for a in range(npg):
            for j in range(3):
                pstage[a][j] = (pstage[a][j].astype(F32) + pland[a][j].astype(F32)).astype(BF16)
                stores.append(pltpu.make_async_copy(pstage[a].at[j], pq[a].at[j], pss.at[a, j]))
            stores.append(pltpu.make_async_copy(pland[a].at[3], pq[a].at[3], pss.at[a, 3]))
        for cp in stores:
            cp.start()
        gather_vecs.forward()
        gather_vecs.finish()
        vs = vbuf[0]
        for d in range(1, NDEV):
            vs = vs + vbuf[d]
        vsum_ref[...] = vs
        for d in range(NDEV):
            dcond_ref[d] = vbuf[d, 0:8, :]
        for k in range(1, NDEV):
            _small_copy(wown, wssum_ref.at[rows_of(idx ^ k)], s3, r3, k - 1, me).wait_recv()
        for cp in sm + ag:
            cp.wait_send()
        for cp in stores:
            cp.wait()
        for cp in pair:
            cp.wait_send()

    dma7 = pltpu.SemaphoreType.DMA((NDEV - 1,))
    dma4 = pltpu.SemaphoreType.DMA((npg, 4))
    outs = pl.pallas_call(
        body,
        name="reduce_small",
        in_specs=[VM] * 7 + [HBM] + [HBM] * npg,
        out_specs=[VM, VM, VM] + [HBM] * npg,
        out_shape=[jax.ShapeDtypeStruct((NVEC, D), F32), jax.ShapeDtypeStruct((NDEV, 8, D), F32),
                   jax.ShapeDtypeStruct(dws.shape, F32)]
                  + [jax.ShapeDtypeStruct((4,) + g.shape[1:], g.dtype) for g in pair_grads],
        scratch_shapes=[pltpu.VMEM((NVEC, D), F32), pltpu.VMEM((NDEV, NVEC, D), F32),
                        pltpu.VMEM((NDEV, CH, CH), F32), pltpu.VMEM((CH, CH), F32)] + AG_SEMS(1) + [dma7] * 4
                       + [pltpu.VMEM((4,) + g.shape[1:], g.dtype) for g in pair_grads]
                       + [pltpu.VMEM((3,) + g.shape[1:], g.dtype) for g in pair_grads] + [dma4] * 4,
        compiler_params=pltpu.CompilerParams(vmem_limit_bytes=VMEM_LIMIT),
    )(acc_f, acc_v, acc_b, acc_a, acc_bs, acc_cw, dws, after, *pair_grads)
    return outs[0], outs[1], outs[2], outs[3:]


HBM_ONLY = pl.BlockSpec(memory_space=pltpu.HBM)
SEM = pl.BlockSpec(memory_space=pltpu.SEMAPHORE)
EFFECT = pltpu.SideEffectType.DATAFLOW_SIDE_EFFECTING


def _rs_copies(g_refs, land_refs, sems, chips):
    x, y, c, idx = _place()
    if chips:
        routes = [(j, j, (*ch, c)) for j, ch in enumerate([(1 - x, y), (x, 1 - y), (1 - x, 1 - y)])]
    else:
        routes = [(idx ^ k, k - 1, _dev(idx ^ k)) for k in range(1, NDEV)]
    cps = []
    for src, dst, to in routes:
        for a in range(len(g_refs)):
            n = len(cps)
            cps.append(pltpu.make_async_remote_copy(
                src_ref=g_refs[a].at[src], dst_ref=land_refs[a].at[dst], send_sem=sems[2 * n],
                recv_sem=sems[2 * n + 1], device_id=to, device_id_type=MESH))
    return cps


def _rs_start(grads, name, after=(), chips=False):
    nw = len(grads)
    npeer = 3 if chips else NDEV - 1
    nsem = 2 * nw * npeer
    lands = [lax.empty((npeer,) + g.shape[1:], g.dtype) for g in grads]

    def body(*refs):
        g_refs, land_refs = refs[:nw], refs[nw:2 * nw]
        sems = refs[2 * nw + len(after):2 * nw + len(after) + nsem]
        token = refs[-1]
        for cp in _rs_copies(g_refs, land_refs, sems, chips):
            cp.start()
        token[...] = jnp.zeros_like(token)

    outs = pl.pallas_call(
        body, name=name,
        out_shape=(*[pltpu.SemaphoreType.DMA(())] * nsem,
                   *[pltpu.HBM(g.shape, g.dtype) for g in grads], *[pltpu.HBM(l.shape, l.dtype) for l in lands],
                   jax.ShapeDtypeStruct((8, CH), F32)),
        in_specs=[HBM_ONLY] * (2 * nw) + [HBM] * len(after),
        out_specs=(*[SEM] * nsem, *[HBM_ONLY] * (2 * nw), VM),
        input_output_aliases={i: nsem + i for i in range(2 * nw)},
        compiler_params=pltpu.CompilerParams(has_side_effects=EFFECT),
    )(*[pltpu.with_memory_space_constraint(g, pltpu.HBM) for g in grads],
      *[pltpu.with_memory_space_constraint(l, pltpu.HBM) for l in lands], *after)
    return outs[:nsem], outs[nsem:nsem + nw], outs[nsem + nw:nsem + 2 * nw], outs[-1]


def _rs_wait(sems, g_thru, land_thru, after, name, chips=False):
    nw = len(g_thru)
    nsem = len(sems)

    def body(*refs):
        g_refs, land_refs = refs[:nw], refs[nw:2 * nw]
        for cp in _rs_copies(g_refs, land_refs, refs[2 * nw:2 * nw + nsem], chips):
            cp.wait_send()
            cp.wait_recv()

    outs = pl.pallas_call(
        body, name=name,
        out_shape=tuple(pltpu.HBM(a.shape, a.dtype) for a in list(g_thru) + list(land_thru)),
        in_specs=[HBM_ONLY] * (2 * nw) + [SEM] * nsem + [HBM] * len(after),
        out_specs=tuple([HBM_ONLY] * (2 * nw)),
        input_output_aliases={i: i for i in range(2 * nw)},
        compiler_params=pltpu.CompilerParams(has_side_effects=EFFECT),
    )(*g_thru, *land_thru, *sems, *after)
    return outs[:nw], outs[nw:]


def _adamw(w, g, m, v):
    m2 = ADAM_B1 * m + (1.0 - ADAM_B1) * g
    v2 = ADAM_B2 * v + (1.0 - ADAM_B2) * (g * g)
    m_hat = m2 / (1.0 - ADAM_B1 ** ADAM_STEP)
    v_hat = v2 / (1.0 - ADAM_B2 ** ADAM_STEP)
    delta = -ADAM_LR * (m_hat / (jnp.sqrt(v_hat) + ADAM_EPS) + ADAM_WD * w)
    return delta, m2, v2


def _adam_big(r, w, m, v, rb, name, own, after=None, sib=None):
    R, C = w.shape
    ns = r.shape[0]
    g_all, idx1 = own

    def body(idx_ref, r_ref, own_ref, *refs):
        w_ref, m_ref, v_ref, g_ref, d_ref, m2_ref, v2_ref = refs[len(refs) - 7:]
        g = own_ref[0].astype(F32)
        if sib is not None:
            g = g + refs[0][0].astype(F32)
        for k in range(ns):
            g = g + r_ref[k].astype(F32)
        g_ref[...] = g
        d_ref[...], m2_ref[...], v2_ref[...] = _adamw(w_ref[...], g, m_ref[...], v_ref[...])

    t2 = pl.BlockSpec((rb, C), lambda i, idx_ref: (i, 0))
    sd = jax.ShapeDtypeStruct((R, C), F32)
    extra_specs = ([pl.BlockSpec((1, rb, C), lambda i, idx_ref: (3, i, 0))] if sib is not None else []) \
        + ([HBM] if after is not None else [])
    extra = ([sib] if sib is not None else []) + ([after] if after is not None else [])
    return pl.pallas_call(
        body, name=name,
        grid_spec=pltpu.PrefetchScalarGridSpec(
            num_scalar_prefetch=1, grid=(R // rb,),
            in_specs=[pl.BlockSpec((ns, rb, C), lambda i, idx_ref: (0, i, 0)),
                      pl.BlockSpec((1, rb, C), lambda i, idx_ref: (idx_ref[0], i, 0))] + extra_specs + [t2, t2, t2],
            out_specs=[t2, t2, t2, t2]),
        out_shape=[sd, sd, sd, sd],
        compiler_params=pltpu.CompilerParams(dimension_semantics=("arbitrary",), vmem_limit_bytes=VMEM_LIMIT),
    )(idx1, r, g_all, *extra, w, m, v)


def _adam_pair(parts, idx1, after, name):
    n = len(parts)
    steps = parts[0][2].shape[0] // parts[0][5]

    def body(idx_ref, *refs):
        for k in range(n):
            r_ref, own_ref, w_ref, m_ref, v_ref = refs[5 * k:5 * k + 5]
            g_ref, d_ref, m2_ref, v2_ref = refs[5 * n + 1 + 4 * k:5 * n + 5 + 4 * k]
            g = own_ref[0].astype(F32)
            for j in range(r_ref.shape[0]):
                g = g + r_ref[j].astype(F32)
            g_ref[...] = g
            d_ref[...], m2_ref[...], v2_ref[...] = _adamw(w_ref[...], g, m_ref[...], v_ref[...])

    in_specs, out_specs, out_shape, flat = [], [], [], []
    for r, g_all, w, m, v, rb in parts:
        R, C = w.shape
        assert R // rb == steps
        t2 = pl.BlockSpec((rb, C), lambda i, idx_ref: (i, 0))
        in_specs += [pl.BlockSpec((r.shape[0], rb, C), lambda i, idx_ref: (0, i, 0)),
                     pl.BlockSpec((1, rb, C), lambda i, idx_ref: (idx_ref[0], i, 0)), t2, t2, t2]
        out_specs += [t2] * 4
        out_shape += [jax.ShapeDtypeStruct((R, C), F32)] * 4
        flat += [r, g_all, w, m, v]
    out = pl.pallas_call(
        body, name=name,
        grid_spec=pltpu.PrefetchScalarGridSpec(num_scalar_prefetch=1, grid=(steps,), in_specs=in_specs + [HBM],
                                               out_specs=out_specs),
        out_shape=out_shape,
        compiler_params=pltpu.CompilerParams(dimension_semantics=("arbitrary",), vmem_limit_bytes=VMEM_LIMIT),
    )(idx1, *flat, after)
    return [out[4 * k:4 * k + 4] for k in range(n)]


def _adam_ada(c_all, parts, rb, name):
    R = parts[0][1].shape[0]
    n = len(parts)

    def body(c_ref, *refs):
        cv = c_ref[...]
        sc = cv * _sig(cv)
        for k in range(n):
            dc_ref, w_ref, m_ref, v_ref = refs[4 * k:4 * k + 4]
            g_ref, d_ref, m2_ref, v2_ref = refs[4 * (n + k):4 * (n + k) + 4]
            g = lax.dot_general(sc, dc_ref[...], (((0,), (0,)), ((), ())), preferred_element_type=F32,
                                precision=lax.Precision.HIGHEST)
            g_ref[...] = g
            d_ref[...], m2_ref[...], v2_ref[...] = _adamw(w_ref[...], g, m_ref[...], v_ref[...])

    in_specs, out_specs, out_shape, flat = [pl.BlockSpec((NDEV, rb), lambda i: (0, i))], [], [], []
    for dcs, w, m, v in parts:
        C = w.shape[1]
        t2 = pl.BlockSpec((rb, C), lambda i: (i, 0))
        in_specs += [_full((NDEV, C)), t2, t2, t2]
        out_specs += [t2] * 4
        out_shape += [jax.ShapeDtypeStruct((R, C), F32)] * 4
        flat += [dcs, w, m, v]
    out = pl.pallas_call(
        body, name=name, grid=(R // rb,),
        in_specs=in_specs, out_specs=out_specs, out_shape=out_shape,
        compiler_params=pltpu.CompilerParams(dimension_semantics=("arbitrary",), vmem_limit_bytes=VMEM_LIMIT),
    )(c_all, *flat)
    return [out[4 * k:4 * k + 4] for k in range(n)]


_SMALL = ["ada_b", "ada_f_b", "norm1_g", "b_in", "a_ln_g", "a_ln_b", "a_spatial_b", "b_conv_b", "b_gn_g", "b_gn_b",
          "out_norm_a_g", "out_norm_b_g", "norm2_g", "norm_f_g", "a_spatial_w", "b_conv_w"]


def _adam_small(vsum, wssum, gcw, params):
    names = _SMALL
    flat = []
    for n in names:
        flat += list(params[n])

    def body(vs_ref, ws_ref, gcw_ref, *rest):
        ins = rest[:3 * len(names)]
        outs = rest[3 * len(names):]
        for pi, n in enumerate(names):
            w_ref, m_ref, v_ref = ins[3 * pi:3 * pi + 3]
            g_ref, d_ref, m2_ref, v2_ref = outs[4 * pi:4 * pi + 4]
            if n in ("ada_b", "ada_f_b", "b_in"):
                row0 = {"ada_b": 0, "ada_f_b": 6, "b_in": 9}[n]
                pieces = [(vs_ref[row0 + r:row0 + r + 1, :], slice(r * D, (r + 1) * D))
                          for r in range(w_ref.shape[1] // D)]
            elif n == "a_spatial_w":
                pieces = [(ws_ref[...], slice(None))]
            elif n == "b_conv_w":
                pieces = [(gcw_ref[...], slice(None))]
            else:
                row, off, width = _VEC_AT[n]
                pieces = [(vs_ref[row:row + 1, off:off + width], slice(None))]
            for g, cs in pieces:
                g_ref[:, cs] = g
                d_ref[:, cs], m2_ref[:, cs], v2_ref[:, cs] = _adamw(w_ref[:, cs], g, m_ref[:, cs], v_ref[:, cs])

    out_shape = []
    for n in names:
        out_shape += [jax.ShapeDtypeStruct(params[n][0].shape, F32)] * 4
    outs = pl.pallas_call(
        body, name="adam_small",
        in_specs=[VM] * (3 + len(flat)), out_specs=[VM] * len(out_shape), out_shape=out_shape,
        compiler_params=pltpu.CompilerParams(vmem_limit_bytes=VMEM_LIMIT),
    )(vsum, wssum, gcw, *flat)
    return {n: outs[4 * pi:4 * pi + 4] for pi, n in enumerate(names)}


def _token_tile(T, want):
    return want if T % want == 0 else T


def kernel(x, c, ada_w, ada_b, norm1_g, w_in, b_in, a_ln_g, a_ln_b, a_spatial_w, a_spatial_b, b_conv_w, b_conv_b, b_gn_g, b_gn_b, out_norm_a_g, out_norm_b_g, w_out, norm2_g, w_ffn_in, w_ffn_out, ada_f_w, ada_f_b, norm_f_g, loss_target, m_ada_w, m_ada_b, m_norm1_g, m_w_in, m_b_in, m_a_ln_g, m_a_ln_b, m_a_spatial_w, m_a_spatial_b, m_b_conv_w, m_b_conv_b, m_b_gn_g, m_b_gn_b, m_out_norm_a_g, m_out_norm_b_g, m_w_out, m_norm2_g, m_w_ffn_in, m_w_ffn_out, m_ada_f_w, m_ada_f_b, m_norm_f_g, v_ada_w, v_ada_b, v_norm1_g, v_w_in, v_b_in, v_a_ln_g, v_a_ln_b, v_a_spatial_w, v_a_spatial_b, v_b_conv_w, v_b_conv_b, v_b_gn_g, v_b_gn_b, v_out_norm_a_g, v_out_norm_b_g, v_w_out, v_norm2_g, v_w_ffn_in, v_w_ffn_out, v_ada_f_w, v_ada_f_b, v_norm_f_g):
    T = x.shape[1]
    idx = 4 * lax.axis_index("x") + 2 * lax.axis_index("y") + lax.axis_index("c")
    x2d = x.reshape(T, D)
    tgt = loss_target.reshape(T, D)

    conv_s = jnp.pad(b_conv_w[0], ((0, HALO - KW), (0, 0)))
    call, cparts, cfparts, convg, (win_g, wout_g) = _gather(
        c, ada_w[0], ada_b.reshape(NDEV, -1), ada_f_w, ada_f_b.reshape(NDEV, -1), conv_s,
        [w_in[0], w_out[0]])
    wout = wout_g.reshape(D, D)
    mod = jnp.concatenate([cparts.reshape(6, D), cfparts.reshape(2, D)], axis=0)
    cw = jnp.transpose(convg, (1, 0, 2)).reshape(HALO, DB)

    tril = jnp.tril(jnp.ones((CH, CH), dtype=bool))
    wsm = jnp.where(tril[None], a_spatial_w[0], 0.0).astype(BF16)
    wcat = wsm.reshape(NH * CH, CH)
    wcat_t = jnp.transpose(wsm, (0, 2, 1)).reshape(NH * CH, CH)
    bsf = jnp.repeat(a_spatial_b[0].T, DA // NH, axis=1)
    pm = jnp.asarray(_GROUP_MEAN, BF16)
    esel = jnp.asarray(_HEAD_SELECT, BF16)

    tm = _token_tile(T, 256)
    tk = _token_tile(T, 2048)
    (x1, hb, zvg, mixed, yb, o, gu, dgelu_u, dgelu_v, vhat, rslb, yhat, rsg), (wfi_g, wfo_g) = _mix_fwd(
        x2d, mod, norm1_g, win_g, b_in, a_ln_g, a_ln_b, wcat, bsf, cw, b_conv_b, b_gn_g, b_gn_b, out_norm_a_g,
        out_norm_b_g, wout, pm, [w_ffn_in[0].T, w_ffn_out[0]], _token_tile(T, 512))
    dx1, h2b, dgu, act, dxg, acc_f = _ffn(x1, tgt, mod, norm2_g, norm_f_g.reshape(1, D),
                                          wfi_g.reshape(2 * DFF, D), wfo_g.reshape(DFF, D), tm)
    g_wfi = _wgrad_rows(dgu, h2b, 2 * WFI_B, tk, "wgrad_ffn_in").reshape(NDEV, WFI_B, D)
    g_wfo = _wgrad_rows(act, dxg, 2 * WFI_B, tk, "wgrad_ffn_out").reshape(NDEV, DFF // NDEV, D)
    f_sems, f_thru, f_land, f_token = _rs_start([g_wfi, g_wfo], "rs_ffn_start")
    (gx, acc_v, acc_b, acc_a, acc_bs, acc_ws, acc_cw), (g_win, g_wout) = _mix_bwd(
        dx1, x2d, zvg, mixed, o, hb, yb, gu, dgelu_u, dgelu_v, vhat, rslb, yhat, rsg, mod, norm1_g, win_g, a_ln_g,
        a_ln_b, wcat, wcat_t, cw, b_gn_g, b_gn_b, out_norm_a_g, out_norm_b_g, wout, pm, esel, f_token, tm)
    (g_wfi_d, g_wfo_d), (r_wfi, r_wfo) = _rs_wait(f_sems, f_thru, f_land, [acc_v], "rs_ffn_wait")
    g_wout = g_wout.reshape(NDEV, D // NDEV, D)

    vsum, dcond_all, wssum, (q_win, q_wout) = _reduce_small(acc_f, acc_v, acc_b, acc_a, acc_bs, acc_cw, acc_ws,
                                                            g_wfi_d, [g_win, g_wout])
    sems, g_thru, land_thru, token = _rs_start([q_win, q_wout], "rs_mix_start", after=(vsum,), chips=True)

    own = lambda g: (g, jnp.reshape(idx, (1,)).astype(jnp.int32))
    res = {}
    upd_wfi_t, res["w_ffn_out"] = _adam_pair(
        [(r_wfi, g_wfi_d, w_ffn_in[0].T, m_w_ffn_in[0].T, v_w_ffn_in[0].T, WFI_B // 2),
         (r_wfo, g_wfo_d, w_ffn_out[0], m_w_ffn_out[0], v_w_ffn_out[0], DFF // NDEV // 2)],
        own(g_wfi_d)[1], token, "adam_w_ffn")
    res["w_ffn_in"] = tuple(a.T for a in upd_wfi_t)
    dcond = dcond_all.reshape(NDEV, 8 * D)
    nada = ada_w.shape[2]
    nadf = ada_f_w.shape[1]
    dcs = lax.dynamic_slice(dcond, (0, idx * nada), (NDEV, nada))
    dcfs = lax.dynamic_slice(dcond, (0, 6 * D + idx * nadf), (NDEV, nadf))
    res["ada_w"], res["ada_f_w"] = _adam_ada(call, [(dcs, ada_w[0], m_ada_w[0], v_ada_w[0]),
                                                    (dcfs, ada_f_w, m_ada_f_w, v_ada_f_w)], 512, "adam_ada")
    ncw = b_conv_w.shape[2]
    gcw = jnp.concatenate([lax.dynamic_slice(vsum, (_CW_ROW, idx * ncw), (HALO // 2, ncw)),
                           lax.dynamic_slice(vsum, (_CW_ROW, DB + idx * ncw), (HALO // 2, ncw))], axis=0)[:KW]
    two = lambda a: a.reshape(1, -1) if a.ndim == 1 else a.reshape(-1, a.shape[-1])
    small_in = {
        "ada_b": (ada_b, m_ada_b, v_ada_b), "ada_f_b": (ada_f_b, m_ada_f_b, v_ada_f_b),
        "norm1_g": (norm1_g, m_norm1_g, v_norm1_g), "b_in": (b_in, m_b_in, v_b_in),
        "a_ln_g": (a_ln_g, m_a_ln_g, v_a_ln_g), "a_ln_b": (a_ln_b, m_a_ln_b, v_a_ln_b),
        "a_spatial_b": (a_spatial_b.reshape(1, D), m_a_spatial_b.reshape(1, D), v_a_spatial_b.reshape(1, D)),
        "b_conv_b": (b_conv_b, m_b_conv_b, v_b_conv_b), "b_gn_g": (b_gn_g, m_b_gn_g, v_b_gn_g),
        "b_gn_b": (b_gn_b, m_b_gn_b, v_b_gn_b), "out_norm_a_g": (out_norm_a_g, m_out_norm_a_g, v_out_norm_a_g),
        "out_norm_b_g": (out_norm_b_g, m_out_norm_b_g, v_out_norm_b_g),
        "norm2_g": (norm2_g, m_norm2_g, v_norm2_g), "norm_f_g": (norm_f_g, m_norm_f_g, v_norm_f_g),
        "a_spatial_w": (a_spatial_w, m_a_spatial_w, v_a_spatial_w),
        "b_conv_w": (b_conv_w[0], m_b_conv_w[0], v_b_conv_w[0]),
    }
    small_in = {n: tuple(two(a) for a in t) for n, t in small_in.items()}
    res.update(_adam_small(vsum, wssum, gcw, small_in))
    (q_win_d, q_wout_d), (r_win, r_wout) = _rs_wait(
        sems, g_thru, land_thru,
        [upd_wfi_t[0], res["w_ffn_out"][0], res["ada_w"][0], res["ada_f_w"][0], res["norm_f_g"][0]],
        "rs_mix_wait", chips=True)
    res["w_in"] = _adam_big(r_win, w_in[0], m_w_in[0], v_w_in[0], 512, "adam_w_in", own=own(g_win), sib=q_win_d)
    res["w_out"] = _adam_big(r_wout, w_out[0], m_w_out[0], v_w_out[0], D // NDEV, "adam_w_out", own=own(g_wout),
                             sib=q_wout_d)

    loss = 0.5 / D * jnp.sum(vsum[_LOSS_ROW])
    shapes = {"ada_w": ada_w, "ada_b": ada_b, "norm1_g": norm1_g, "w_in": w_in, "b_in": b_in, "a_ln_g": a_ln_g,
              "a_ln_b": a_ln_b, "a_spatial_w": a_spatial_w, "a_spatial_b": a_spatial_b, "b_conv_w": b_conv_w,
              "b_conv_b": b_conv_b, "b_gn_g": b_gn_g, "b_gn_b": b_gn_b, "out_norm_a_g": out_norm_a_g,
              "out_norm_b_g": out_norm_b_g, "w_out": w_out, "norm2_g": norm2_g, "w_ffn_in": w_ffn_in,
              "w_ffn_out": w_ffn_out, "ada_f_w": ada_f_w, "ada_f_b": ada_f_b, "norm_f_g": norm_f_g}
    order = list(shapes)
    outs = [loss, gx.reshape(x.shape)]
    for which in range(4):
        outs += [res[n][which].reshape(shapes[n].shape) for n in order]
    return tuple(outs)
```
